```python
import jax, jax.numpy as jnp
from jax import lax
import numpy as np

D_MODEL = 2048
BATCH = 16
SEQ = 256
DEPTH = 1
DEC_BATCH = 2
DEC_SEQ = 1024
PAST_LEN = 256

GRID_W = 64
N_DIR = 2
A_W = D_MODEL // 2
B_W = D_MODEL - A_W
DK_A = 128
DV_A = 128
H_A = A_W // DV_A
DV_B = 256
DK_B = DV_B // 2
H_B = B_W // DV_B
CONV_W = 3
CHUNK = 64
FFN = 4 * D_MODEL
EPS = 1e-6
SPLIT_SIZES = (3 * A_W, A_W, N_DIR * H_A, N_DIR * H_A,
               H_B * DK_B, H_B * DK_B, B_W, B_W, N_DIR * H_B, N_DIR * H_B)
PROJ_W = sum(SPLIT_SIZES)

kernel_name = "hybrid_deltanet_mlstm_diffusion_step"


def rmsnorm(x, g):
    xf = x.astype(jnp.float32)
    y = xf * lax.rsqrt(jnp.mean(xf * xf, axis=-1, keepdims=True) + EPS)
    return y.astype(x.dtype) * g


def l2norm(x):
    return x * lax.rsqrt(jnp.sum(x * x, axis=-1, keepdims=True) + EPS)


def flip(t):
    return jnp.flip(t, axis=1)


def conv_centred(x, w):
    pad = CONV_W // 2
    t = x.shape[1]
    xp = jnp.pad(x, ((0, 0), (pad, pad), (0, 0)))
    return sum(xp[:, j:j + t] * w[:, j] for j in range(CONV_W))


def to_chunks(x):
    b, t = x.shape[:2]
    x = x.reshape(b, t // CHUNK, CHUNK, *x.shape[2:])
    return jnp.swapaxes(x, 2, 3)


def from_chunks(y):
    n, b, h, c, v = y.shape
    return jnp.transpose(y, (1, 0, 3, 2, 4)).reshape(b, n * c, h, v)


def gated_delta_chunked(q, k, v, g, beta, s0):
    dk = q.shape[-1]
    dv = v.shape[-1]
    qc = to_chunks(q) * dk ** -0.5
    kc = to_chunks(k)
    vc = to_chunks(v)
    bc = to_chunks(beta)
    gc = jnp.cumsum(to_chunks(g), axis=-1)
    incl = jnp.tril(jnp.ones((CHUNK, CHUNK), bool))
    strict = jnp.tril(jnp.ones((CHUNK, CHUNK), bool), -1)
    decay = jnp.exp(jnp.where(incl, gc[..., :, None] - gc[..., None, :], -jnp.inf))
    kb = kc * bc[..., None]
    lower = jnp.where(strict, jnp.einsum('bnhik,bnhjk->bnhij', kb, kc) * decay, 0.0)
    a_mat = lower + jnp.eye(CHUNK, dtype=lower.dtype)
    rhs = jnp.concatenate([vc * bc[..., None], kb * jnp.exp(gc)[..., None]], axis=-1)
    sol = lax.linalg.triangular_solve(a_mat, rhs, left_side=True, lower=True, unit_diagonal=True)
    u, w = sol[..., :dv], sol[..., dv:]
    attn = jnp.einsum('bnhik,bnhjk->bnhij', qc, kc) * decay
    qg = qc * jnp.exp(gc)[..., None]
    g_last = gc[..., -1]
    kd = kc * jnp.exp(g_last[..., None] - gc)[..., None]

    def step(s, xs):
        qg_i, kd_i, u_i, w_i, attn_i, gl_i = xs
        v_new = u_i - jnp.einsum('bhck,bhkv->bhcv', w_i, s)
        o = jnp.einsum('bhck,bhkv->bhcv', qg_i, s) + jnp.einsum('bhij,bhjv->bhiv', attn_i, v_new)
        s = s * jnp.exp(gl_i)[..., None, None] + jnp.einsum('bhck,bhcv->bhkv', kd_i, v_new)
        return s, o

    xs = tuple(jnp.moveaxis(t, 1, 0) for t in (qg, kd, u, w, attn, g_last))
    s_final, o = lax.scan(step, s0, xs)
    return from_chunks(o), s_final


def mlstm_chunked(q, k, v, log_i, log_f, c0, n0, m0):
    dk = q.shape[-1]
    qc = to_chunks(q) * dk ** -0.5
    kc = to_chunks(k)
    vc = to_chunks(v)
    ic = to_chunks(log_i)
    bc = jnp.cumsum(to_chunks(log_f), axis=-1)
    incl = jnp.tril(jnp.ones((CHUNK, CHUNK), bool))
    log_d = jnp.where(incl, bc[..., :, None] - bc[..., None, :] + ic[..., None, :], -jnp.inf)
    log_end = bc[..., -1:] - bc + ic
    qk = jnp.einsum('bnhik,bnhjk->bnhij', qc, kc)

    def step(carry, xs):
        c, n, m = carry
        q_i, k_i, v_i, b_i, ld_i, qk_i, le_i = xs
        inter = b_i + m[..., None]
        m_t = jnp.maximum(inter, jnp.max(ld_i, axis=-1))
        dw = jnp.exp(ld_i - m_t[..., None]) * qk_i
        iw = jnp.exp(inter - m_t)
        num = iw[..., None] * jnp.einsum('bhck,bhkv->bhcv', q_i, c) + jnp.einsum('bhij,bhjv->bhiv', dw, v_i)
        den = iw * jnp.einsum('bhck,bhk->bhc', q_i, n) + jnp.sum(dw, axis=-1)
        h = num / jnp.maximum(jnp.abs(den), jnp.exp(-m_t))[..., None]
        b_last = b_i[..., -1] + m
        m_new = jnp.maximum(b_last, jnp.max(le_i, axis=-1))
        ks = k_i * jnp.exp(le_i - m_new[..., None])[..., None]
        dec = jnp.exp(b_last - m_new)
        c = dec[..., None, None] * c + jnp.einsum('bhck,bhcv->bhkv', ks, v_i)
        n = dec[..., None] * n + jnp.sum(ks, axis=-2)
        return (c, n, m_new), h

    xs = tuple(jnp.moveaxis(t, 1, 0) for t in (qc, kc, vc, bc, log_d, qk, log_end))
    (c_f, n_f, m_f), h = lax.scan(step, (c0, n0, m0), xs)
    return from_chunks(h), c_f, n_f, m_f


def mixer(h, grid, init_states, w_in, conv_w, a_log, dt_bias, norm_a, ibias, fbias, norm_b, w_out):
    f32 = jnp.float32
    bn, t, _ = h.shape
    s0, c0, n0, m0 = (s.astype(f32) for s in init_states)
    proj = h @ w_in
    split_idx = np.cumsum(SPLIT_SIZES)[:-1].tolist()
    aqkv, ag, aa, ab, bq, bk, bv, bo, bi, bf = jnp.split(proj, split_idx, axis=-1)
    if grid:
        rows = t // GRID_W
        aqkv = conv_centred(aqkv.reshape(bn * rows, GRID_W, 3 * A_W), conv_w).reshape(bn, t, 3 * A_W)
    else:
        aqkv = conv_centred(aqkv, conv_w)
    aq, ak, av = jnp.split(jax.nn.silu(aqkv).astype(f32), 3, axis=-1)
    aq = l2norm(aq.reshape(bn, t, H_A, DK_A))
    ak = l2norm(ak.reshape(bn, t, H_A, DK_A))
    av = av.reshape(bn, t, H_A, DV_A)
    g = -jnp.exp(a_log.astype(f32)) * jax.nn.softplus(aa.reshape(bn, t, N_DIR, H_A).astype(f32) + dt_bias)
    beta = jax.nn.sigmoid(ab.reshape(bn, t, N_DIR, H_A).astype(f32))
    oa_f, sa_f = gated_delta_chunked(aq, ak, av, g[:, :, 0], beta[:, :, 0], s0[:, 0])
    oa_b, sa_b = gated_delta_chunked(flip(aq), flip(ak), flip(av), flip(g[:, :, 1]), flip(beta[:, :, 1]), s0[:, 1])
    ya = rmsnorm(oa_f + flip(oa_b), norm_a) * jax.nn.silu(ag.reshape(bn, t, H_A, DV_A).astype(f32))
    bq = bq.reshape(bn, t, H_B, DK_B).astype(f32)
    bk = bk.reshape(bn, t, H_B, DK_B).astype(f32)
    bv = bv.reshape(bn, t, H_B, DV_B).astype(f32)
    li = bi.reshape(bn, t, N_DIR, H_B).astype(f32) + ibias
    lf = jax.nn.log_sigmoid(bf.reshape(bn, t, N_DIR, H_B).astype(f32) + fbias)
    hb_f, cf, nf, mf = mlstm_chunked(bq, bk, bv, li[:, :, 0], lf[:, :, 0], c0[:, 0], n0[:, 0], m0[:, 0])
    hb_b, cb, nb, mb = mlstm_chunked(flip(bq), flip(bk), flip(bv), flip(li[:, :, 1]), flip(lf[:, :, 1]),
                                     c0[:, 1], n0[:, 1], m0[:, 1])
    yb = rmsnorm(hb_f + flip(hb_b), norm_b) * jax.nn.sigmoid(bo.reshape(bn, t, H_B, DV_B).astype(f32))
    y = jnp.concatenate([ya.reshape(bn, t, A_W), yb.reshape(bn, t, B_W)], axis=-1).astype(h.dtype) @ w_out
    states = (jnp.stack([sa_f, sa_b], axis=1), jnp.stack([cf, cb], axis=1),
              jnp.stack([nf, nb], axis=1), jnp.stack([mf, mb], axis=1))
    return y, states


def block(x, mod, init_states, lp, grid):
    (pre1, post1, pre2, post2, w_in, conv_w, a_log, dt_bias, norm_a, ibias, fbias, norm_b,
     w_out, w1, w2) = lp
    shift1, scale1, gate1, shift2, scale2, gate2 = jnp.split(mod, 6, axis=-1)
    h = rmsnorm(x, pre1) * (1.0 + scale1) + shift1
    mix, states = mixer(h, grid, init_states, w_in, conv_w, a_log, dt_bias, norm_a, ibias, fbias, norm_b, w_out)
    x = x + gate1 * rmsnorm(mix, post1)
    h = rmsnorm(x, pre2) * (1.0 + scale2) + shift2
    f = jnp.square(jax.nn.relu(h @ w1)) @ w2
    x = x + gate2 * rmsnorm(f, post2)
    return x, states


def setup_inputs(seed: int = 0) -> dict:
    key = jax.random.key(seed)
    ks = jax.random.split(key, 32)
    nrm = jax.random.normal
    d = D_MODEL
    dt = jnp.exp(jax.random.uniform(ks[17], (DEPTH, N_DIR, H_A), minval=np.log(1e-3), maxval=np.log(1e-1)))
    return {
        'x_prompt': nrm(ks[0], (BATCH, SEQ, d), jnp.float32),
        'x_sample': nrm(ks[1], (DEC_BATCH, DEC_SEQ, d), jnp.float32),
        'state_delta': 0.05 * nrm(ks[2], (DEC_BATCH, DEPTH, N_DIR, H_A, DK_A, DV_A), jnp.float32),
        'state_mlstm_C': 0.1 * nrm(ks[3], (DEC_BATCH, DEPTH, N_DIR, H_B, DK_B, DV_B), jnp.float32),
        'state_mlstm_n': 0.1 * nrm(ks[4], (DEC_BATCH, DEPTH, N_DIR, H_B, DK_B), jnp.float32),
        'state_mlstm_m': 0.5 * nrm(ks[5], (DEC_BATCH, DEPTH, N_DIR, H_B), jnp.float32),
        'c': nrm(ks[6], (DEC_BATCH, d), jnp.float32),
        'c_ctx': nrm(ks[7], (d,), jnp.float32),
        'w_ada': 0.5 * d ** -0.5 * nrm(ks[8], (DEPTH, d, 6 * d), jnp.float32),
        'b_ada': 0.01 * nrm(ks[9], (DEPTH, 6 * d), jnp.float32),
        'norm_mix_pre': 1.0 + 0.05 * nrm(ks[10], (DEPTH, d), jnp.float32),
        'norm_mix_post': 1.0 + 0.05 * nrm(ks[11], (DEPTH, d), jnp.float32),
        'norm_ffn_pre': 1.0 + 0.05 * nrm(ks[12], (DEPTH, d), jnp.float32),
        'norm_ffn_post': 1.0 + 0.05 * nrm(ks[13], (DEPTH, d), jnp.float32),
        'w_in': d ** -0.5 * nrm(ks[14], (DEPTH, d, PROJ_W), jnp.float32),
        'conv_w': CONV_W ** -0.5 * nrm(ks[15], (DEPTH, 3 * A_W, CONV_W), jnp.float32),
        'a_log': jnp.log(jax.random.uniform(ks[16], (DEPTH, N_DIR, H_A), minval=1.0, maxval=16.0)),
        'dt_bias': dt + jnp.log(-jnp.expm1(-dt)),
        'norm_a': 1.0 + 0.05 * nrm(ks[18], (DEPTH, DV_A), jnp.float32),
        'mlstm_ibias': -1.0 + 0.1 * nrm(ks[19], (DEPTH, N_DIR, H_B), jnp.float32),
        'mlstm_fbias': jax.random.uniform(ks[20], (DEPTH, N_DIR, H_B), minval=3.0, maxval=6.0),
        'norm_b': 1.0 + 0.05 * nrm(ks[21], (DEPTH, DV_B), jnp.float32),
        'w_out': d ** -0.5 * nrm(ks[22], (DEPTH, d, d), jnp.float32),
        'w_ffn1': d ** -0.5 * nrm(ks[23], (DEPTH, d, FFN), jnp.float32),
        'w_ffn2': FFN ** -0.5 * nrm(ks[24], (DEPTH, FFN, d), jnp.float32),
    }


def reference(x_prompt, x_sample, state_delta, state_mlstm_C, state_mlstm_n, state_mlstm_m, c, c_ctx,
              w_ada, b_ada, norm_mix_pre, norm_mix_post, norm_ffn_pre, norm_ffn_post, w_in, conv_w,
              a_log, dt_bias, norm_a, mlstm_ibias, mlstm_fbias, norm_b, w_out, w_ffn1, w_ffn2):
    f32 = jnp.float32
    n_ctx = x_prompt.shape[0]
    zero_states = (jnp.zeros((n_ctx, N_DIR, H_A, DK_A, DV_A), f32),
                   jnp.zeros((n_ctx, N_DIR, H_B, DK_B, DV_B), f32),
                   jnp.zeros((n_ctx, N_DIR, H_B, DK_B), f32),
                   jnp.zeros((n_ctx, N_DIR, H_B), f32))
    y_prompt, y_sample = x_prompt, x_sample
    ctx_states = ([], [], [], [])
    for l in range(DEPTH):
        lp = (norm_mix_pre[l], norm_mix_post[l], norm_ffn_pre[l], norm_ffn_post[l], w_in[l], conv_w[l],
              a_log[l], dt_bias[l], norm_a[l], mlstm_ibias[l], mlstm_fbias[l], norm_b[l],
              w_out[l], w_ffn1[l], w_ffn2[l])
        mod_ctx = (jax.nn.silu(c_ctx) @ w_ada[l] + b_ada[l])[None, None, :]
        mod_lat = (jax.nn.silu(c) @ w_ada[l] + b_ada[l])[:, None, :]
        y_prompt, st = block(y_prompt, mod_ctx, zero_states, lp, False)
        for acc, s in zip(ctx_states, st):
            acc.append(s.astype(x_prompt.dtype))
        cached = (state_delta[:, l], state_mlstm_C[:, l], state_mlstm_n[:, l], state_mlstm_m[:, l])
        y_sample, _ = block(y_sample, mod_lat, cached, lp, True)
    new_state_delta = jnp.stack(ctx_states[0], axis=1)
    new_state_mlstm_C = jnp.stack(ctx_states[1], axis=1)
    new_state_mlstm_n = jnp.stack(ctx_states[2], axis=1)
    new_state_mlstm_m = jnp.stack(ctx_states[3], axis=1)
    return (y_prompt, y_sample, new_state_delta, new_state_mlstm_C, new_state_mlstm_n, new_state_mlstm_m)
```

```python
import functools

import jax
import jax.numpy as jnp
from jax import lax
from jax.experimental import pallas as pl
from jax.experimental.pallas import tpu as pltpu

F32 = jnp.float32
BF16 = jnp.bfloat16

D_MODEL = 2048
N_DIR = 2
A_W = D_MODEL // 2
B_W = D_MODEL - A_W
DK_A = 128
DV_A = 128
H_A = A_W // DV_A
DV_B = 256
DK_B = DV_B // 2
H_B = B_W // DV_B
GRID_W = 64
CHUNK = 64
FFN = 4 * D_MODEL
EPS = 1e-6
LANES = 128
NEG = -1e30

QKV_W = 3 * A_W
REST_W = A_W + 2 * H_B * DK_B + 2 * B_W
OFF_AG, OFF_BQ, OFF_BK, OFF_BV, OFF_BO = 0, A_W, A_W + H_B * DK_B, A_W + 2 * H_B * DK_B, A_W + 2 * H_B * DK_B + B_W
G_OFF, BETA_OFF = 0, N_DIR * H_A
LI_OFF = 2 * N_DIR * H_A
LF_OFF = LI_OFF + N_DIR * H_B

VMEM_LIMIT = 56 * 1024 * 1024


def _sigmoid(x):
    return 1.0 / (1.0 + jnp.exp(-x))


def _softplus(x):
    return jnp.maximum(x, 0.0) + jnp.log1p(jnp.exp(-jnp.abs(x)))


def _dot(a, b):
    return jnp.dot(a.astype(BF16), b.astype(BF16), preferred_element_type=F32)


def _dot_f32(a, b):
    return jnp.dot(a, b, precision=lax.Precision.HIGHEST, preferred_element_type=F32)


def _rms(x):
    return x * lax.rsqrt(jnp.mean(x * x, axis=-1, keepdims=True) + EPS)


def _params(sem):
    return pltpu.CompilerParams(dimension_semantics=sem, vmem_limit_bytes=VMEM_LIMIT)


def _ada_body(c_ref, w_ref, b_ref, o_ref):
    c = c_ref[...]
    o_ref[...] = _dot(c * _sigmoid(c), w_ref[...]) + b_ref[...]


def _ada(c_all, w, b):
    n = w.shape[1]
    tn = 1024
    return pl.pallas_call(
        _ada_body,
        grid=(n // tn,),
        in_specs=[pl.BlockSpec(c_all.shape, lambda j: (0, 0)),
                  pl.BlockSpec((D_MODEL, tn), lambda j: (0, j)),
                  pl.BlockSpec((1, tn), lambda j: (0, j))],
        out_specs=pl.BlockSpec((c_all.shape[0], tn), lambda j: (0, j)),
        out_shape=jax.ShapeDtypeStruct((c_all.shape[0], n), F32),
        compiler_params=_params(("arbitrary",)),
        name="ada",
    )(c_all, w, b)


def _proj_prologue(x_ref, mod_ref, g_ref, h_scr):
    @pl.when(pl.program_id(1) == 0)
    def _():
        y = _rms(x_ref[...]) * g_ref[...]
        h_scr[...] = (y * (1.0 + mod_ref[0, 1:2, :]) + mod_ref[0, 0:1, :]).astype(BF16)


def _proj_qkv_body(x_ref, mod_ref, g_ref, w_ref, cw_ref, o_ref, h_scr, *, seq_len):
    _proj_prologue(x_ref, mod_ref, g_ref, h_scr)
    j = pl.program_id(1)
    acc = jnp.dot(h_scr[...], w_ref[...], preferred_element_type=F32)
    tm, tn = acc.shape
    pos = lax.broadcasted_iota(jnp.int32, (tm, tn), 0) & (seq_len - 1)
    prev = jnp.where(pos == 0, 0.0, pltpu.roll(acc, 1, 0))
    nxt = jnp.where(pos == seq_len - 1, 0.0, pltpu.roll(acc, tm - 1, 0))
    y = prev * cw_ref[0:1, :] + acc * cw_ref[1:2, :] + nxt * cw_ref[2:3, :]
    y = y * _sigmoid(y)
    tiles_per_part = A_W // tn

    def _l2(scale):
        for g in range(tn // LANES):
            blk = y[:, g * LANES:(g + 1) * LANES]
            inv = lax.rsqrt(jnp.sum(blk * blk, axis=-1, keepdims=True) + EPS)
            o_ref[:, g * LANES:(g + 1) * LANES] = blk * (inv * scale)

    @pl.when(j < tiles_per_part)
    def _():
        _l2(DK_A ** -0.5)

    @pl.when((j >= tiles_per_part) & (j < 2 * tiles_per_part))
    def _():
        _l2(1.0)

    @pl.when(j >= 2 * tiles_per_part)
    def _():
        o_ref[...] = y


def _proj_rest_body(x_ref, mod_ref, g_ref, w_ref, o_ref, h_scr, *, tn):
    _proj_prologue(x_ref, mod_ref, g_ref, h_scr)
    j = pl.program_id(1)
    acc = jnp.dot(h_scr[...], w_ref[...], preferred_element_type=F32)
    c0 = j * tn

    @pl.when(c0 < OFF_BQ)
    def _():
        o_ref[...] = acc * _sigmoid(acc)

    @pl.when((c0 >= OFF_BQ) & (c0 < OFF_BK))
    def _():
        o_ref[...] = acc * (DK_B ** -0.5)

    @pl.when((c0 >= OFF_BK) & (c0 < OFF_BO))
    def _():
        o_ref[...] = acc

    @pl.when(c0 >= OFF_BO)
    def _():
        o_ref[...] = _sigmoid(acc)


def _proj_gate_body(x_ref, mod_ref, g_ref, w_ref, p_ref, o_ref, h_scr):
    _proj_prologue(x_ref, mod_ref, g_ref, h_scr)
    z = jnp.dot(h_scr[...], w_ref[...], preferred_element_type=F32)
    lane = lax.broadcasted_iota(jnp.int32, z.shape, 1)
    g = -jnp.exp(p_ref[0:1, :]) * _softplus(z + p_ref[1:2, :])
    beta = _sigmoid(z)
    li = z + p_ref[2:3, :]
    lf = -_softplus(-li)
    o_ref[...] = jnp.where(lane < BETA_OFF, g,
                           jnp.where(lane < LI_OFF, beta,
                                     jnp.where(lane < LF_OFF, li,
                                               jnp.where(lane < LF_OFF + N_DIR * H_B, lf, 0.0))))


def _proj(body, x2d, mod3, mod_map, pre_g, w, aux, tn, name):
    m = x2d.shape[0]
    n = w.shape[1]
    tm = 1024
    in_specs = [pl.BlockSpec((tm, D_MODEL), lambda i, j: (i, 0)),
                pl.BlockSpec((1, 6, D_MODEL), lambda i, j: (mod_map(i), 0, 0)),
                pl.BlockSpec((1, D_MODEL), lambda i, j: (0, 0)),
                pl.BlockSpec((D_MODEL, tn), lambda i, j: (0, j))]
    args = [x2d, mod3, pre_g, w]
    if aux is not None:
        in_specs.append(pl.BlockSpec((aux.shape[0], tn), lambda i, j: (0, j)))
        args.append(aux)
    return pl.pallas_call(
        body,
        grid=(m // tm, n // tn),
        in_specs=in_specs,
        out_specs=pl.BlockSpec((tm, tn), lambda i, j: (i, j)),
        out_shape=jax.ShapeDtypeStruct((m, n), F32),
        scratch_shapes=[pltpu.VMEM((tm, D_MODEL), BF16)],
        compiler_params=_params(("parallel", "arbitrary")),
        name=name,
    )(*args)


def _masks(d):
    row = lax.broadcasted_iota(jnp.int32, (CHUNK, CHUNK), 0)
    col = lax.broadcasted_iota(jnp.int32, (CHUNK, CHUNK), 1)
    if d == 0:
        return row >= col, row > col, row == col
    return row <= col, row < col, row == col


def _transpose_chunk(x):
    full = jnp.concatenate([x, jnp.zeros_like(x)], axis=0)
    return full.T[:, :CHUNK]


def _gate_cumsums(gates, incl):
    gc = _dot_f32(incl.astype(F32), gates)
    return gc, _transpose_chunk(gc), _transpose_chunk(gates)


def _tri_inverse(lmat, eye):
    p = eye - lmat
    lp = lmat
    for _ in range(5):
        lp = _dot_f32(lp, lp)
        p = p + _dot_f32(p, lp)
    return p


def _delta_body(*refs, n_chunks, has_init, emit_state):
    qkv = refs[0:2]
    ag = refs[2:4]
    gt = refs[4:6]
    norm_ref = refs[6]
    pos = 7
    s0_ref = None
    if has_init:
        s0_ref = refs[pos]
        pos += 1
    ya_ref = refs[pos]
    pos += 1
    sout_ref = None
    if emit_state:
        sout_ref = refs[pos]
        pos += 1
    s_scr, o_scr = refs[pos], refs[pos + 1]

    n = pl.program_id(1)

    @pl.when(n == 0)
    def _():
        o_scr[...] = jnp.zeros_like(o_scr)
        if has_init:
            s_scr[...] = s0_ref[0]
        else:
            s_scr[...] = jnp.zeros_like(s_scr)

    for d in range(N_DIR):
        incl, strict, diag = _masks(d)
        eye = diag.astype(F32)
        gates = gt[d][...]
        gc, gc_t, gates_t = _gate_cumsums(gates, incl)
        last = CHUNK - 1 if d == 0 else 0
        cidx = n if d == 0 else n_chunks - 1 - n
        rows = pl.ds(pl.multiple_of(cidx * CHUNK, CHUNK), CHUNK)
        for h in range(H_A):
            c = d * H_A + h
            hs = slice(h * DK_A, (h + 1) * DK_A)
            q = qkv[d][:, h * DK_A:(h + 1) * DK_A]
            k = qkv[d][:, A_W + h * DK_A:A_W + (h + 1) * DK_A]
            v = qkv[d][:, 2 * A_W + h * DV_A:2 * A_W + (h + 1) * DV_A]
            k_t = _transpose_chunk(k).astype(BF16)
            gcol = gc[:, G_OFF + c:G_OFF + c + 1]
            grow = gc_t[G_OFF + c:G_OFF + c + 1, :]
            bcol = gates[:, BETA_OFF + c:BETA_OFF + c + 1]
            brow = gates_t[BETA_OFF + c:BETA_OFF + c + 1, :]
            glast = gc[last:last + 1, G_OFF + c:G_OFF + c + 1]
            decay = jnp.exp(jnp.where(incl, gcol - grow, NEG))
            kk = _dot(k, k_t)
            qk = _dot(q, k_t)
            lmat = jnp.where(strict, kk * decay, 0.0) * bcol
            tb = _tri_inverse(lmat, eye) * brow
            u = _dot_f32(tb, v)
            w = _dot_f32(tb * jnp.exp(grow), k)
            s = s_scr[d, h]
            s16 = s.astype(BF16)
            v_new = (u - _dot(w, s16)).astype(BF16)
            o = _dot(q, s16) * jnp.exp(gcol) + _dot(qk * decay, v_new)
            kd_t = k_t * jnp.exp(glast - grow)
            s_scr[d, h] = s * jnp.exp(glast) + _dot(kd_t, v_new)
            tot = o + o_scr[rows, hs]
            o_scr[rows, hs] = tot
            ya_ref[rows, hs] = (_rms(tot) * norm_ref[...] * ag[d][:, hs]).astype(BF16)

    if emit_state:
        @pl.when(n == n_chunks - 1)
        def _():
            sout_ref[0] = s_scr[...]


def _delta(qkv, rest, gates, norm_a, s0, batch, n_chunks, emit_state):
    t = n_chunks * CHUNK
    has_init = s0 is not None

    def fwd(b, n):
        return (b * n_chunks + n, 0)

    def bwd(b, n):
        return (b * n_chunks + n_chunks - 1 - n, 0)

    in_specs = [pl.BlockSpec((CHUNK, QKV_W), fwd), pl.BlockSpec((CHUNK, QKV_W), bwd),
                pl.BlockSpec((CHUNK, A_W), fwd), pl.BlockSpec((CHUNK, A_W), bwd),
                pl.BlockSpec((CHUNK, LANES), fwd), pl.BlockSpec((CHUNK, LANES), bwd),
                pl.BlockSpec((1, DV_A), lambda b, n: (0, 0))]
    args = [qkv, qkv, rest, rest, gates, gates, norm_a]
    if has_init:
        in_specs.append(pl.BlockSpec((1, N_DIR, H_A, DK_A, DV_A), lambda b, n: (b, 0, 0, 0, 0)))
        args.append(s0)
    out_specs = [pl.BlockSpec((t, A_W), lambda b, n: (b, 0))]
    out_shape = [jax.ShapeDtypeStruct((batch * t, A_W), BF16)]
    if emit_state:
        out_specs.append(pl.BlockSpec((1, N_DIR, H_A, DK_A, DV_A), lambda b, n: (b, 0, 0, 0, 0)))
        out_shape.append(jax.ShapeDtypeStruct((batch, N_DIR, H_A, DK_A, DV_A), F32))
    return pl.pallas_call(
        functools.partial(_delta_body, n_chunks=n_chunks, has_init=has_init, emit_state=emit_state),
        grid=(batch, n_chunks),
        in_specs=in_specs,
        out_specs=out_specs,
        out_shape=out_shape,
        scratch_shapes=[pltpu.VMEM((N_DIR, H_A, DK_A, DV_A), F32), pltpu.VMEM((t, A_W), F32)],
        compiler_params=_params(("parallel", "arbitrary")),
        name="delta_scan",
    )(*args)


CAUG_W = DV_B + LANES


def _mlstm_body(*refs, n_chunks, has_init, emit_state):
    rest = refs[0:2]
    gt = refs[2:4]
    norm_ref = refs[4]
    pos = 5
    c0_ref = m0_ref = None
    if has_init:
        c0_ref, m0_ref = refs[pos], refs[pos + 1]
        pos += 2
    yb_ref = refs[pos]
    pos += 1
    cout_ref = mout_ref = None
    if emit_state:
        cout_ref, mout_ref = refs[pos], refs[pos + 1]
        pos += 2
    c_scr, m_scr, o_scr = refs[pos], refs[pos + 1], refs[pos + 2]

    n = pl.program_id(1)

    @pl.when(n == 0)
    def _():
        o_scr[...] = jnp.zeros_like(o_scr)
        if has_init:
            c_scr[...] = c0_ref[0]
            m_scr[...] = m0_ref[0]
        else:
            c_scr[...] = jnp.zeros_like(c_scr)
            m_scr[...] = jnp.zeros_like(m_scr)

    ones = jnp.ones((CHUNK, LANES), BF16)
    for d in range(N_DIR):
        incl, _, _ = _masks(d)
        gates = gt[d][...]
        gc, gc_t, gates_t = _gate_cumsums(gates, incl)
        last = CHUNK - 1 if d == 0 else 0
        cidx = n if d == 0 else n_chunks - 1 - n
        rows = pl.ds(pl.multiple_of(cidx * CHUNK, CHUNK), CHUNK)
        for h in range(H_B):
            c = d * H_B + h
            vs = slice(h * DV_B, (h + 1) * DV_B)
            q = rest[d][:, OFF_BQ + h * DK_B:OFF_BQ + (h + 1) * DK_B]
            k = rest[d][:, OFF_BK + h * DK_B:OFF_BK + (h + 1) * DK_B]
            v = rest[d][:, OFF_BV + h * DV_B:OFF_BV + (h + 1) * DV_B]
            ogate = rest[d][:, OFF_BO + h * DV_B:OFF_BO + (h + 1) * DV_B]
            k_t = _transpose_chunk(k)
            v_aug = jnp.concatenate([v.astype(BF16), ones], axis=1)
            bcol = gc[:, LF_OFF + c:LF_OFF + c + 1]
            brow = gc_t[LF_OFF + c:LF_OFF + c + 1, :]
            irow = gates_t[LI_OFF + c:LI_OFF + c + 1, :]
            blast = gc[last:last + 1, LF_OFF + c:LF_OFF + c + 1]
            m_old = m_scr[c:c + 1, 0:1]
            log_d = jnp.where(incl, bcol - brow + irow, NEG)
            inter = bcol + m_old
            m_t = jnp.maximum(inter, jnp.max(log_d, axis=-1, keepdims=True))
            dw = jnp.exp(log_d - m_t) * _dot(q, k_t)
            iw = jnp.exp(inter - m_t)
            c_aug = c_scr[d, h]
            num_aug = iw * _dot(q, c_aug) + _dot(dw, v_aug)
            den = jnp.maximum(jnp.abs(num_aug[:, DV_B:]), jnp.exp(-m_t))
            hb = jnp.concatenate([num_aug[:, :LANES] / den, num_aug[:, LANES:DV_B] / den], axis=1)
            le = blast - brow + irow
            b_last = blast + m_old
            m_new = jnp.maximum(b_last, jnp.max(le, axis=-1, keepdims=True))
            ks_t = k_t * jnp.exp(le - m_new)
            c_scr[d, h] = jnp.exp(b_last - m_new) * c_aug + _dot(ks_t, v_aug)
            m_scr[c:c + 1, :] = jnp.broadcast_to(m_new, (1, LANES))
            tot = hb + o_scr[rows, vs]
            o_scr[rows, vs] = tot
            yb_ref[rows, vs] = (_rms(tot) * norm_ref[...] * ogate).astype(BF16)

    if emit_state:
        @pl.when(n == n_chunks - 1)
        def _():
            cout_ref[0] = c_scr[...]
            mout_ref[0] = m_scr[...]


def _mlstm(rest, gates, norm_b, c0, m0, batch, n_chunks, emit_state):
    t = n_chunks * CHUNK
    has_init = c0 is not None

    def fwd(b, n):
        return (b * n_chunks + n, 0)

    def bwd(b, n):
        return (b * n_chunks + n_chunks - 1 - n, 0)

    cspec = pl.BlockSpec((1, N_DIR, H_B, DK_B, CAUG_W), lambda b, n: (b, 0, 0, 0, 0))
    mspec = pl.BlockSpec((1, N_DIR * H_B, LANES), lambda b, n: (b, 0, 0))
    in_specs = [pl.BlockSpec((CHUNK, REST_W), fwd), pl.BlockSpec((CHUNK, REST_W), bwd),
                pl.BlockSpec((CHUNK, LANES), fwd), pl.BlockSpec((CHUNK, LANES), bwd),
                pl.BlockSpec((1, DV_B), lambda b, n: (0, 0))]
    args = [rest, rest, gates, gates, norm_b]
    if has_init:
        in_specs += [cspec, mspec]
        args += [c0, m0]
    out_specs = [pl.BlockSpec((t, B_W), lambda b, n: (b, 0))]
    out_shape = [jax.ShapeDtypeStruct((batch * t, B_W), BF16)]
    if emit_state:
        out_specs += [cspec, mspec]
        out_shape += [jax.ShapeDtypeStruct((batch, N_DIR, H_B, DK_B, CAUG_W), F32),
                      jax.ShapeDtypeStruct((batch, N_DIR * H_B, LANES), F32)]
    return pl.pallas_call(
        functools.partial(_mlstm_body, n_chunks=n_chunks, has_init=has_init, emit_state=emit_state),
        grid=(batch, n_chunks),
        in_specs=in_specs,
        out_specs=out_specs,
        out_shape=out_shape,
        scratch_shapes=[pltpu.VMEM((N_DIR, H_B, DK_B, CAUG_W), F32),
                        pltpu.VMEM((N_DIR * H_B, LANES), F32),
                        pltpu.VMEM((t, B_W), F32)],
        compiler_params=_params(("parallel", "arbitrary")),
        name="mlstm_scan",
    )(*args)


def _outproj_body(ya_ref, yb_ref, wa_ref, wb_ref, x_ref, mod_ref, post1_ref, pre2_ref, x1_ref, h2_ref):
    mix = (jnp.dot(ya_ref[...], wa_ref[...], preferred_element_type=F32)
           + jnp.dot(yb_ref[...], wb_ref[...], preferred_element_type=F32))
    x1 = x_ref[...] + mod_ref[0, 2:3, :] * (_rms(mix) * post1_ref[...])
    x1_ref[...] = x1
    h2 = _rms(x1) * pre2_ref[...] * (1.0 + mod_ref[0, 4:5, :]) + mod_ref[0, 3:4, :]
    h2_ref[...] = h2.astype(BF16)


def _outproj(ya, yb, w_out, x2d, mod3, mod_map, post1, pre2, tm):
    m = x2d.shape[0]
    row = lambda i: (i, 0)
    const = lambda i: (0, 0)
    return pl.pallas_call(
        _outproj_body,
        grid=(m // tm,),
        in_specs=[pl.BlockSpec((tm, A_W), row), pl.BlockSpec((tm, B_W), row),
                  pl.BlockSpec((A_W, D_MODEL), lambda i: (0, 0)),
                  pl.BlockSpec((B_W, D_MODEL), lambda i: (1, 0)),
                  pl.BlockSpec((tm, D_MODEL), row),
                  pl.BlockSpec((1, 6, D_MODEL), lambda i: (mod_map(i * tm), 0, 0)),
                  pl.BlockSpec((1, D_MODEL), const), pl.BlockSpec((1, D_MODEL), const)],
        out_specs=[pl.BlockSpec((tm, D_MODEL), row), pl.BlockSpec((tm, D_MODEL), row)],
        out_shape=[jax.ShapeDtypeStruct((m, D_MODEL), F32), jax.ShapeDtypeStruct((m, D_MODEL), BF16)],
        compiler_params=_params(("parallel",)),
        name="outproj",
    )(ya, yb, w_out, w_out, x2d, mod3, post1, pre2)


def _ffn_body(h2_ref, w1_ref, w2_ref, x1_ref, mod_ref, post2_ref, o_ref, acc_ref):
    kk = pl.program_id(1)
    a = jnp.maximum(jnp.dot(h2_ref[...], w1_ref[...], preferred_element_type=F32), 0.0)
    contrib = jnp.dot((a * a).astype(BF16), w2_ref[...], preferred_element_type=F32)

    @pl.when(kk == 0)
    def _():
        acc_ref[...] = contrib

    @pl.when(kk > 0)
    def _():
        acc_ref[...] += contrib

    @pl.when(kk == pl.num_programs(1) - 1)
    def _():
        o_ref[...] = x1_ref[...] + mod_ref[0, 5:6, :] * (_rms(acc_ref[...]) * post2_ref[...])


def _ffn(h2, w1, w2, x1, mod3, mod_map, post2, tm, fc):
    m = h2.shape[0]
    return pl.pallas_call(
        _ffn_body,
        grid=(m // tm, FFN // fc),
        in_specs=[pl.BlockSpec((tm, D_MODEL), lambda i, k: (i, 0)),
                  pl.BlockSpec((D_MODEL, fc), lambda i, k: (0, k)),
                  pl.BlockSpec((fc, D_MODEL), lambda i, k: (k, 0)),
                  pl.BlockSpec((tm, D_MODEL), lambda i, k: (i, 0)),
                  pl.BlockSpec((1, 6, D_MODEL), lambda i, k: (mod_map(i * tm), 0, 0)),
                  pl.BlockSpec((1, D_MODEL), lambda i, k: (0, 0))],
        out_specs=pl.BlockSpec((tm, D_MODEL), lambda i, k: (i, 0)),
        out_shape=jax.ShapeDtypeStruct((m, D_MODEL), F32),
        scratch_shapes=[pltpu.VMEM((tm, D_MODEL), F32)],
        compiler_params=_params(("parallel", "arbitrary")),
        name="ffn",
    )(h2, w1, w2, x1, mod3, post2)


def _block(x, mod3, mod_of_row, lp, init, seq_len, emit_state):
    bsz, t, _ = x.shape
    x2d = x.reshape(bsz * t, D_MODEL)
    n_chunks = t // CHUNK
    tile_mod = lambda i: mod_of_row(i * 1024)
    qkv = _proj(functools.partial(_proj_qkv_body, seq_len=seq_len), x2d, mod3, tile_mod,
                lp["pre1"], lp["w_qkv"], lp["conv_w"], 512, "proj_qkv")
    rest = _proj(functools.partial(_proj_rest_body, tn=512), x2d, mod3, tile_mod,
                 lp["pre1"], lp["w_rest"], None, 512, "proj_rest")
    gates = _proj(_proj_gate_body, x2d, mod3, tile_mod, lp["pre1"], lp["w_gate"], lp["gate_p"],
                  LANES, "proj_gate")
    s0, c0, m0 = init if init is not None else (None, None, None)
    d_out = _delta(qkv, rest, gates, lp["norm_a"], s0, bsz, n_chunks, emit_state)
    m_out = _mlstm(rest, gates, lp["norm_b"], c0, m0, bsz, n_chunks, emit_state)
    x1, h2 = _outproj(d_out[0], m_out[0], lp["w_out"], x2d, mod3, mod_of_row, lp["post1"], lp["pre2"], 256)
    y = _ffn(h2, lp["w1"], lp["w2"], x1, mod3, mod_of_row, lp["post2"], 512, 512)
    states = None
    if emit_state:
        states = (d_out[1], m_out[1][..., :DV_B], m_out[1][..., DV_B], m_out[2][..., 0].reshape(bsz, N_DIR, H_B))
    return y.reshape(bsz, t, D_MODEL), states


def _layer_params(l, norm_mix_pre, norm_mix_post, norm_ffn_pre, norm_ffn_post, w_in, conv_w, a_log, dt_bias,
                  norm_a, mlstm_ibias, mlstm_fbias, norm_b, w_out, w_ffn1, w_ffn2):
    w = w_in[l]
    o_ag = QKV_W
    o_aa = o_ag + A_W
    o_ab = o_aa + N_DIR * H_A
    o_bq = o_ab + N_DIR * H_A
    o_bi = o_bq + 2 * H_B * DK_B + 2 * B_W
    o_bf = o_bi + N_DIR * H_B
    n_gate = 2 * N_DIR * H_A + 2 * N_DIR * H_B
    w_gate = jnp.concatenate([w[:, o_aa:o_bq], w[:, o_bi:o_bf + N_DIR * H_B],
                              jnp.zeros((D_MODEL, LANES - n_gate), F32)], axis=1)

    def lane_row(vals, off):
        return jnp.zeros((LANES,), F32).at[off:off + vals.size].set(vals.reshape(-1))

    gate_p = jnp.stack([lane_row(a_log[l], G_OFF), lane_row(dt_bias[l], G_OFF),
                        lane_row(mlstm_ibias[l], LI_OFF) + lane_row(mlstm_fbias[l], LF_OFF)]
                       + [jnp.zeros((LANES,), F32)] * 5)
    row = lambda v: v[l].reshape(1, -1)
    return dict(
        pre1=row(norm_mix_pre), post1=row(norm_mix_post), pre2=row(norm_ffn_pre), post2=row(norm_ffn_post),
        w_qkv=w[:, :QKV_W].astype(BF16),
        w_rest=jnp.concatenate([w[:, o_ag:o_aa], w[:, o_bq:o_bi]], axis=1).astype(BF16),
        w_gate=w_gate.astype(BF16), gate_p=gate_p,
        conv_w=jnp.concatenate([conv_w[l].T, jnp.zeros((5, QKV_W), F32)], axis=0),
        norm_a=row(norm_a), norm_b=row(norm_b),
        w_out=w_out[l].astype(BF16), w1=w_ffn1[l].astype(BF16), w2=w_ffn2[l].astype(BF16))


def kernel(x_prompt, x_sample, state_delta, state_mlstm_C, state_mlstm_n, state_mlstm_m, c, c_ctx, w_ada, b_ada, norm_mix_pre, norm_mix_post, norm_ffn_pre, norm_ffn_post, w_in, conv_w, a_log, dt_bias, norm_a, mlstm_ibias, mlstm_fbias, norm_b, w_out, w_ffn1, w_ffn2):
    depth = w_in.shape[0]
    n_lat = x_sample.shape[0]
    t_lat = x_sample.shape[1]
    cond = jnp.concatenate([c_ctx[None, :], c, jnp.zeros((8 - 1 - n_lat, D_MODEL), F32)], axis=0)
    y_prompt, y_sample = x_prompt, x_sample
    acc = ([], [], [], [])
    for l in range(depth):
        lp = _layer_params(l, norm_mix_pre, norm_mix_post, norm_ffn_pre, norm_ffn_post, w_in, conv_w, a_log,
                           dt_bias, norm_a, mlstm_ibias, mlstm_fbias, norm_b, w_out, w_ffn1, w_ffn2)
        mod = _ada(cond, w_ada[l], b_ada[l].reshape(1, -1))
        mod3 = mod[:1 + n_lat].reshape(1 + n_lat, 6, D_MODEL)
        y_prompt, st = _block(y_prompt, mod3, lambda r: 0, lp, None, x_prompt.shape[1], True)
        for a, s in zip(acc, st):
            a.append(s)
        c_aug0 = jnp.concatenate(
            [state_mlstm_C[:, l],
             jnp.broadcast_to(state_mlstm_n[:, l][..., None], state_mlstm_n[:, l].shape + (LANES,))], axis=-1)
        m0 = jnp.broadcast_to(state_mlstm_m[:, l].reshape(n_lat, N_DIR * H_B, 1), (n_lat, N_DIR * H_B, LANES))
        y_sample, _ = _block(y_sample, mod3, lambda r: 1 + r // t_lat, lp, (state_delta[:, l], c_aug0, m0),
                             GRID_W, False)
    return (y_prompt, y_sample) + tuple(jnp.stack(a, axis=1) for a in acc)
```

```python
import functools

import jax
import jax.numpy as jnp
from jax import lax
from jax.experimental import pallas as pl
from jax.experimental.pallas import tpu as pltpu

F32 = jnp.float32
BF16 = jnp.bfloat16

D_MODEL = 2048
N_DIR = 2
A_W = D_MODEL // 2
B_W = D_MODEL - A_W
DK_A = 128
DV_A = 128
H_A = A_W // DV_A
DV_B = 256
DK_B = DV_B // 2
H_B = B_W // DV_B
GRID_W = 64
CHUNK = 64
FFN = 4 * D_MODEL
EPS = 1e-6
LANES = 128
NEG = -1e30

QKV_W = 3 * A_W
REST_W = A_W + 2 * H_B * DK_B + 2 * B_W
OFF_AG, OFF_BQ, OFF_BK, OFF_BV, OFF_BO = 0, A_W, A_W + H_B * DK_B, A_W + 2 * H_B * DK_B, A_W + 2 * H_B * DK_B + B_W
G_OFF, BETA_OFF = 0, N_DIR * H_A
LI_OFF = 2 * N_DIR * H_A
LF_OFF = LI_OFF + N_DIR * H_B

VMEM_LIMIT = 56 * 1024 * 1024


def _sigmoid(x):
    return 1.0 / (1.0 + jnp.exp(-x))


def _softplus(x):
    return jnp.maximum(x, 0.0) + jnp.log1p(jnp.exp(-jnp.abs(x)))


def _dot(a, b):
    return jnp.dot(a.astype(BF16), b.astype(BF16), preferred_element_type=F32)


def _dot_f32(a, b):
    return jnp.dot(a, b, precision=lax.Precision.HIGHEST, preferred_element_type=F32)


def _rms(x):
    return x * lax.rsqrt(jnp.mean(x * x, axis=-1, keepdims=True) + EPS)


def _params(sem):
    return pltpu.CompilerParams(dimension_semantics=sem, vmem_limit_bytes=VMEM_LIMIT)


def _ada_body(c_ref, w_ref, b_ref, o_ref):
    c = c_ref[...]
    o_ref[...] = _dot(c * _sigmoid(c), w_ref[...]) + b_ref[...]


def _ada(c_all, w, b):
    n = w.shape[1]
    tn = 1024
    return pl.pallas_call(
        _ada_body,
        grid=(n // tn,),
        in_specs=[pl.BlockSpec(c_all.shape, lambda j: (0, 0)),
                  pl.BlockSpec((D_MODEL, tn), lambda j: (0, j)),
                  pl.BlockSpec((1, tn), lambda j: (0, j))],
        out_specs=pl.BlockSpec((c_all.shape[0], tn), lambda j: (0, j)),
        out_shape=jax.ShapeDtypeStruct((c_all.shape[0], n), F32),
        compiler_params=_params(("arbitrary",)),
        name="ada",
    )(c_all, w, b)


def _proj_prologue(x_ref, mod_ref, g_ref, h_scr):
    @pl.when(pl.program_id(1) == 0)
    def _():
        y = _rms(x_ref[...]) * g_ref[...]
        h_scr[...] = (y * (1.0 + mod_ref[0, 1:2, :]) + mod_ref[0, 0:1, :]).astype(BF16)


def _proj_qkv_body(x_ref, mod_ref, g_ref, w_ref, cw_ref, o_ref, h_scr, *, seq_len):
    _proj_prologue(x_ref, mod_ref, g_ref, h_scr)
    j = pl.program_id(1)
    acc = jnp.dot(h_scr[...], w_ref[...], preferred_element_type=F32)
    tm, tn = acc.shape
    pos = lax.broadcasted_iota(jnp.int32, (tm, tn), 0) & (seq_len - 1)
    prev = jnp.where(pos == 0, 0.0, pltpu.roll(acc, 1, 0))
    nxt = jnp.where(pos == seq_len - 1, 0.0, pltpu.roll(acc, tm - 1, 0))
    y = prev * cw_ref[0:1, :] + acc * cw_ref[1:2, :] + nxt * cw_ref[2:3, :]
    y = y * _sigmoid(y)
    tiles_per_part = A_W // tn

    def _l2(scale):
        for g in range(tn // LANES):
            blk = y[:, g * LANES:(g + 1) * LANES]
            inv = lax.rsqrt(jnp.sum(blk * blk, axis=-1, keepdims=True) + EPS)
            o_ref[:, g * LANES:(g + 1) * LANES] = blk * (inv * scale)

    @pl.when(j < tiles_per_part)
    def _():
        _l2(DK_A ** -0.5)

    @pl.when((j >= tiles_per_part) & (j < 2 * tiles_per_part))
    def _():
        _l2(1.0)

    @pl.when(j >= 2 * tiles_per_part)
    def _():
        o_ref[...] = y


def _proj_rest_body(x_ref, mod_ref, g_ref, w_ref, o_ref, h_scr, *, tn):
    _proj_prologue(x_ref, mod_ref, g_ref, h_scr)
    j = pl.program_id(1)
    acc = jnp.dot(h_scr[...], w_ref[...], preferred_element_type=F32)
    c0 = j * tn

    @pl.when(c0 < OFF_BQ)
    def _():
        o_ref[...] = acc * _sigmoid(acc)

    @pl.when((c0 >= OFF_BQ) & (c0 < OFF_BK))
    def _():
        o_ref[...] = acc * (DK_B ** -0.5)

    @pl.when((c0 >= OFF_BK) & (c0 < OFF_BO))
    def _():
        o_ref[...] = acc

    @pl.when(c0 >= OFF_BO)
    def _():
        o_ref[...] = _sigmoid(acc)


def _proj_gate_body(x_ref, mod_ref, g_ref, w_ref, p_ref, o_ref, h_scr):
    _proj_prologue(x_ref, mod_ref, g_ref, h_scr)
    z = jnp.dot(h_scr[...], w_ref[...], preferred_element_type=F32)
    lane = lax.broadcasted_iota(jnp.int32, z.shape, 1)
    g = -jnp.exp(p_ref[0:1, :]) * _softplus(z + p_ref[1:2, :])
    beta = _sigmoid(z)
    li = z + p_ref[2:3, :]
    lf = -_softplus(-li)
    o_ref[...] = jnp.where(lane < BETA_OFF, g,
                           jnp.where(lane < LI_OFF, beta,
                                     jnp.where(lane < LF_OFF, li,
                                               jnp.where(lane < LF_OFF + N_DIR * H_B, lf, 0.0))))


def _proj(body, x2d, mod3, mod_map, pre_g, w, aux, tn, name):
    m = x2d.shape[0]
    n = w.shape[1]
    tm = 1024
    in_specs = [pl.BlockSpec((tm, D_MODEL), lambda i, j: (i, 0)),
                pl.BlockSpec((1, 6, D_MODEL), lambda i, j: (mod_map(i), 0, 0)),
                pl.BlockSpec((1, D_MODEL), lambda i, j: (0, 0)),
                pl.BlockSpec((D_MODEL, tn), lambda i, j: (0, j))]
    args = [x2d, mod3, pre_g, w]
    if aux is not None:
        in_specs.append(pl.BlockSpec((aux.shape[0], tn), lambda i, j: (0, j)))
        args.append(aux)
    return pl.pallas_call(
        body,
        grid=(m // tm, n // tn),
        in_specs=in_specs,
        out_specs=pl.BlockSpec((tm, tn), lambda i, j: (i, j)),
        out_shape=jax.ShapeDtypeStruct((m, n), F32),
        scratch_shapes=[pltpu.VMEM((tm, D_MODEL), BF16)],
        compiler_params=_params(("parallel", "arbitrary")),
        name=name,
    )(*args)


def _masks(d):
    row = lax.broadcasted_iota(jnp.int32, (CHUNK, CHUNK), 0)
    col = lax.broadcasted_iota(jnp.int32, (CHUNK, CHUNK), 1)
    if d == 0:
        return row >= col, row > col, row == col
    return row <= col, row < col, row == col


def _transpose_chunk(x):
    full = jnp.concatenate([x, jnp.zeros_like(x)], axis=0)
    return full.T[:, :CHUNK]


def _gate_cumsums(gates, incl):
    gc = _dot_f32(incl.astype(F32), gates)
    return gc, _transpose_chunk(gc), _transpose_chunk(gates)


GROUP = 4
N_GROUPS = H_A // GROUP
PAIR = LANES // CHUNK


def _split(x):
    hi = x.astype(BF16)
    return hi, (x - hi.astype(F32)).astype(BF16)


def _store_blocks(ref, x, blk_r, blk_c, col0=0):
    for h in range(GROUP):
        ref[h * blk_r:(h + 1) * blk_r, col0 + h * blk_c:col0 + (h + 1) * blk_c] = x[:, h * blk_c:(h + 1) * blk_c]


def _dot3(a_hi, a_lo, b_hi, b_lo):
    lhs = jnp.concatenate([a_hi, a_lo, a_hi], axis=1)
    rhs = jnp.concatenate([b_hi, b_hi, b_lo], axis=0)
    return jnp.dot(lhs, rhs, preferred_element_type=F32)


def _tri_inverse_wide(lmats, eye_w, bd_hi, bd_lo):
    def bd_dot(i, a_hi, a_lo, b_hi, b_lo):
        _store_blocks(bd_hi.at[i], b_hi, CHUNK, CHUNK)
        _store_blocks(bd_lo.at[i], b_lo, CHUNK, CHUNK)
        return _dot3(a_hi, a_lo, bd_hi[i], bd_lo[i])

    idx = range(len(lmats))
    s = [eye_w - l for l in lmats]
    p = []
    for i in idx:
        m_hi, m_lo = _split(-lmats[i])
        p.append(bd_dot(i, m_hi, m_lo, m_hi, m_lo))
    for _ in range(4):
        for i in idx:
            p_hi, p_lo = _split(p[i])
            s_hi, s_lo = _split(s[i])
            r = bd_dot(i, jnp.concatenate([p_hi, s_hi], axis=0), jnp.concatenate([p_lo, s_lo], axis=0), p_hi, p_lo)
            p[i] = r[:CHUNK]
            s[i] = s[i] + r[CHUNK:]
    for i in idx:
        p_hi, p_lo = _split(p[i])
        s_hi, s_lo = _split(s[i])
        s[i] = s[i] + bd_dot(i, s_hi, s_lo, p_hi, p_lo)
    return s


def _col_bcast(tile, c, width=LANES):
    return jnp.broadcast_to(tile[:, c:c + 1], (tile.shape[0], width))


def _delta_body(*refs, n_chunks, has_init, emit_state):
    qkv = refs[0:2]
    ag = refs[2:4]
    gt = refs[4:6]
    norm_ref = refs[6]
    pos = 7
    s0_ref = None
    if has_init:
        s0_ref = refs[pos]
        pos += 1
    ya_ref = refs[pos]
    pos += 1
    sout_ref = None
    if emit_state:
        sout_ref = refs[pos]
        pos += 1
    s_scr, o_scr, bdn_hi, bdn_lo, bdk, bduw_hi, bduw_lo, bds, bdv = refs[pos:pos + 9]

    n = pl.program_id(1)
    gw = GROUP * DK_A

    @pl.when(n == 0)
    def _():
        o_scr[...] = jnp.zeros_like(o_scr)
        for ref in (bdn_hi, bdn_lo, bdk, bduw_hi, bduw_lo, bds, bdv):
            ref[...] = jnp.zeros_like(ref)
        for d in range(N_DIR):
            for h in range(H_A):
                blk = s0_ref[0, d, h] if has_init else jnp.zeros((DK_A, DV_A), F32)
                s_scr[d, h // GROUP, :, (h % GROUP) * DV_A:(h % GROUP + 1) * DV_A] = blk

    row = lax.broadcasted_iota(jnp.int32, (CHUNK, LANES), 0)
    lane = lax.broadcasted_iota(jnp.int32, (CHUNK, LANES), 1)
    col = lane & (CHUNK - 1)
    left = lane < CHUNK
    eye_w = jnp.concatenate([(row == col).astype(F32)] * (GROUP // PAIR), axis=1)
    probs = [(d, g) for d in range(N_DIR) for g in range(N_GROUPS)]
    gates, gc, gc_t, eg, egl, masks = {}, {}, {}, {}, {}, {}
    for d in range(N_DIR):
        masks[d] = (row >= col, row > col) if d == 0 else (row <= col, row < col)
        gates[d] = gt[d][...]
        gc[d] = _dot_f32(_masks(d)[0].astype(F32), gates[d])
        gc_t[d] = jnp.concatenate([gc[d], gc[d]], axis=0).T
        last = CHUNK - 1 if d == 0 else 0
        eg[d] = jnp.exp(gc[d])
        egl[d] = jnp.exp(gc[d][last:last + 1, :] - gc[d])

    def cols_of(d, g):
        return [G_OFF + d * H_A + g * GROUP + hl for hl in range(GROUP)]

    q16, ks, beta_xs, decays, grams = [], [], [], [], []
    for gi, (d, g) in enumerate(probs):
        cols = cols_of(d, g)
        incl, _ = masks[d]
        k = qkv[d][:, A_W + g * gw:A_W + (g + 1) * gw]
        beta_x = jnp.concatenate([_col_bcast(gates[d], BETA_OFF - G_OFF + c) for c in cols], axis=1)
        decay = []
        for p in range(GROUP // PAIR):
            c0, c1 = cols[PAIR * p], cols[PAIR * p + 1]
            gcol = jnp.where(left, _col_bcast(gc[d], c0), _col_bcast(gc[d], c1))
            grow = jnp.where(left[0:1], gc_t[d][c0:c0 + 1, :], gc_t[d][c1:c1 + 1, :])
            decay.append(jnp.exp(jnp.where(incl, gcol - grow, NEG)))
        decays.append(jnp.concatenate(decay, axis=1))
        q16.append(qkv[d][:, g * gw:(g + 1) * gw].astype(BF16))
        ks.append(k)
        beta_xs.append(beta_x)
        _store_blocks(bdk.at[gi], k.astype(BF16), CHUNK, DK_A)
    for gi, (d, g) in enumerate(probs):
        grams.append(lax.dot_general(jnp.concatenate([q16[gi], (ks[gi] * beta_xs[gi]).astype(BF16)], axis=0),
                                     bdk[gi], (((1,), (1,)), ((), ())), preferred_element_type=F32))
    attns, lmats = [], []
    for gi, (d, g) in enumerate(probs):
        strict_w = jnp.concatenate([masks[d][1]] * (GROUP // PAIR), axis=1)
        attns.append((grams[gi][:CHUNK] * decays[gi]).astype(BF16))
        lmats.append(jnp.where(strict_w, grams[gi][CHUNK:] * decays[gi], 0.0))
    ainvs = _tri_inverse_wide(lmats, eye_w, bdn_hi, bdn_lo)
    eg_xs, kd_ts = [], []
    for gi, (d, g) in enumerate(probs):
        cols = cols_of(d, g)
        v = qkv[d][:, 2 * A_W + g * gw:2 * A_W + (g + 1) * gw]
        eg_x = jnp.concatenate([_col_bcast(eg[d], c) for c in cols], axis=1)
        egl_x = jnp.concatenate([_col_bcast(egl[d], c) for c in cols], axis=1)
        eg_xs.append(eg_x)
        bv_hi, bv_lo = _split(v * beta_xs[gi])
        bk_hi, bk_lo = _split(ks[gi] * (beta_xs[gi] * eg_x))
        _store_blocks(bduw_hi.at[gi], bv_hi, CHUNK, DV_A)
        _store_blocks(bduw_lo.at[gi], bv_lo, CHUNK, DV_A)
        _store_blocks(bduw_hi.at[gi], bk_hi, CHUNK, DK_A, col0=gw)
        _store_blocks(bduw_lo.at[gi], bk_lo, CHUNK, DK_A, col0=gw)
        kd = ks[gi] * egl_x
        kd_ts.append(jnp.concatenate([kd[:, hl * DK_A:(hl + 1) * DK_A] for hl in range(GROUP)], axis=0)
                     .T.astype(BF16))
        _store_blocks(bds.at[gi], s_scr[d, g].astype(BF16), DK_A, DV_A)
    uws = []
    for gi in range(len(probs)):
        t_hi, t_lo = _split(ainvs[gi])
        uws.append(_dot3(t_hi, t_lo, bduw_hi[gi], bduw_lo[gi]))
    ws_qs = []
    for gi in range(len(probs)):
        ws_qs.append(jnp.dot(jnp.concatenate([uws[gi][:, gw:].astype(BF16), q16[gi]], axis=0), bds[gi],
                             preferred_element_type=F32))
    for gi in range(len(probs)):
        _store_blocks(bdv.at[gi], (uws[gi][:, :gw] - ws_qs[gi][:CHUNK]).astype(BF16), CHUNK, DV_A)
    rs = []
    for gi in range(len(probs)):
        rs.append(jnp.dot(jnp.concatenate([attns[gi], kd_ts[gi]], axis=0), bdv[gi], preferred_element_type=F32))
    for gi, (d, g) in enumerate(probs):
        cols = cols_of(d, g)
        last = CHUNK - 1 if d == 0 else 0
        o = ws_qs[gi][CHUNK:] * eg_xs[gi] + rs[gi][:CHUNK]
        eg_last = jnp.concatenate(
            [jnp.broadcast_to(eg[d][last:last + 1, c:c + 1], (1, DV_A)) for c in cols], axis=1)
        s_scr[d, g] = s_scr[d, g] * eg_last + rs[gi][CHUNK:]
        cidx = n if d == 0 else n_chunks - 1 - n
        rows = pl.ds(pl.multiple_of(cidx * CHUNK, CHUNK), CHUNK)
        tot = o + o_scr[rows, g * gw:(g + 1) * gw]
        o_scr[rows, g * gw:(g + 1) * gw] = tot
        for hl in range(GROUP):
            hs = slice((g * GROUP + hl) * DV_A, (g * GROUP + hl + 1) * DV_A)
            blk = tot[:, hl * DV_A:(hl + 1) * DV_A]
            ya_ref[rows, hs] = (_rms(blk) * norm_ref[...] * ag[d][:, hs]).astype(BF16)

    if emit_state:
        @pl.when(n == n_chunks - 1)
        def _():
            for d in range(N_DIR):
                for h in range(H_A):
                    sout_ref[0, d, h] = s_scr[d, h // GROUP, :, (h % GROUP) * DV_A:(h % GROUP + 1) * DV_A]


def _delta(qkv, rest, gates, norm_a, s0, batch, n_chunks, emit_state):
    t = n_chunks * CHUNK
    has_init = s0 is not None
    ng = N_DIR * N_GROUPS

    def fwd(b, n):
        return (b * n_chunks + n, 0)

    def bwd(b, n):
        return (b * n_chunks + n_chunks - 1 - n, 0)

    in_specs = [pl.BlockSpec((CHUNK, QKV_W), fwd), pl.BlockSpec((CHUNK, QKV_W), bwd),
                pl.BlockSpec((CHUNK, A_W), fwd), pl.BlockSpec((CHUNK, A_W), bwd),
                pl.BlockSpec((CHUNK, LANES), fwd), pl.BlockSpec((CHUNK, LANES), bwd),
                pl.BlockSpec((1, DV_A), lambda b, n: (0, 0))]
    args = [qkv, qkv, rest, rest, gates, gates, norm_a]
    if has_init:
        in_specs.append(pl.BlockSpec((1, N_DIR, H_A, DK_A, DV_A), lambda b, n: (b, 0, 0, 0, 0)))
        args.append(s0)
    out_specs = [pl.BlockSpec((t, A_W), lambda b, n: (b, 0))]
    out_shape = [jax.ShapeDtypeStruct((batch * t, A_W), BF16)]
    if emit_state:
        out_specs.append(pl.BlockSpec((1, N_DIR, H_A, DK_A, DV_A), lambda b, n: (b, 0, 0, 0, 0)))
        out_shape.append(jax.ShapeDtypeStruct((batch, N_DIR, H_A, DK_A, DV_A), F32))
    return pl.pallas_call(
        functools.partial(_delta_body, n_chunks=n_chunks, has_init=has_init, emit_state=emit_state),
        grid=(batch, n_chunks),
        in_specs=in_specs,
        out_specs=out_specs,
        out_shape=out_shape,
        scratch_shapes=[pltpu.VMEM((N_DIR, N_GROUPS, DK_A, GROUP * DV_A), F32), pltpu.VMEM((t, A_W), F32),
                        pltpu.VMEM((ng, GROUP * CHUNK, GROUP * CHUNK), BF16),
                        pltpu.VMEM((ng, GROUP * CHUNK, GROUP * CHUNK), BF16),
                        pltpu.VMEM((ng, GROUP * CHUNK, GROUP * DK_A), BF16),
                        pltpu.VMEM((ng, GROUP * CHUNK, 2 * GROUP * DK_A), BF16),
                        pltpu.VMEM((ng, GROUP * CHUNK, 2 * GROUP * DK_A), BF16),
                        pltpu.VMEM((ng, GROUP * DK_A, GROUP * DV_A), BF16),
                        pltpu.VMEM((ng, GROUP * CHUNK, GROUP * DV_A), BF16)],
        compiler_params=_params(("parallel", "arbitrary")),
        name="delta_scan",
    )(*args)


CAUG_W = DV_B + LANES


def _mlstm_body(*refs, n_chunks, has_init, emit_state):
    rest = refs[0:2]
    gt = refs[2:4]
    norm_ref = refs[4]
    pos = 5
    c0_ref = m0_ref = None
    if has_init:
        c0_ref, m0_ref = refs[pos], refs[pos + 1]
        pos += 2
    yb_ref = refs[pos]
    pos += 1
    cout_ref = mout_ref = None
    if emit_state:
        cout_ref, mout_ref = refs[pos], refs[pos + 1]
        pos += 2
    c_scr, m_scr, o_scr = refs[pos], refs[pos + 1], refs[pos + 2]

    n = pl.program_id(1)

    @pl.when(n == 0)
    def _():
        o_scr[...] = jnp.zeros_like(o_scr)
        if has_init:
            c_scr[...] = c0_ref[0]
            m_scr[...] = m0_ref[0]
        else:
            c_scr[...] = jnp.zeros_like(c_scr)
            m_scr[...] = jnp.zeros_like(m_scr)

    ones = jnp.ones((CHUNK, LANES), BF16)
    probs = [(d, h) for d in range(N_DIR) for h in range(H_B)]
    idx = range(len(probs))
    gcs = {}
    for d in range(N_DIR):
        incl, _, _ = _masks(d)
        gates = gt[d][...]
        gcs[d] = (incl,) + _gate_cumsums(gates, incl)
    q16, k_t, v_aug, qk = [], [], [], []
    for d, h in probs:
        q16.append(rest[d][:, OFF_BQ + h * DK_B:OFF_BQ + (h + 1) * DK_B].astype(BF16))
        k_t.append(_transpose_chunk(rest[d][:, OFF_BK + h * DK_B:OFF_BK + (h + 1) * DK_B]))
        v = rest[d][:, OFF_BV + h * DV_B:OFF_BV + (h + 1) * DV_B]
        v_aug.append(jnp.concatenate([v.astype(BF16), ones], axis=1))
    for i in idx:
        qk.append(_dot(q16[i], k_t[i]))
    qc = [_dot(q16[i], c_scr[d, h]) for i, (d, h) in enumerate(probs)]
    dw, iw, m_t, ks_t, dec = [], [], [], [], []
    for i, (d, h) in enumerate(probs):
        incl, gc, gc_t, gates_t = gcs[d]
        c = d * H_B + h
        last = CHUNK - 1 if d == 0 else 0
        bcol = gc[:, LF_OFF + c:LF_OFF + c + 1]
        brow = gc_t[LF_OFF + c:LF_OFF + c + 1, :]
        irow = gates_t[LI_OFF + c:LI_OFF + c + 1, :]
        blast = gc[last:last + 1, LF_OFF + c:LF_OFF + c + 1]
        m_old = m_scr[c:c + 1, 0:1]
        log_d = jnp.where(incl, bcol - brow + irow, NEG)
        inter = bcol + m_old
        m_t.append(jnp.maximum(inter, jnp.max(log_d, axis=-1, keepdims=True)))
        dw.append(jnp.exp(log_d - m_t[i]) * qk[i])
        iw.append(jnp.exp(inter - m_t[i]))
        le = blast - brow + irow
        b_last = blast + m_old
        m_new = jnp.maximum(b_last, jnp.max(le, axis=-1, keepdims=True))
        ks_t.append(k_t[i] * jnp.exp(le - m_new))
        dec.append(jnp.exp(b_last - m_new))
        m_scr[c:c + 1, :] = jnp.broadcast_to(m_new, (1, LANES))
    intra = [_dot(dw[i], v_aug[i]) for i in idx]
    upd = [_dot(ks_t[i], v_aug[i]) for i in idx]
    for i, (d, h) in enumerate(probs):
        vs = slice(h * DV_B, (h + 1) * DV_B)
        num_aug = iw[i] * qc[i] + intra[i]
        den = jnp.maximum(jnp.abs(num_aug[:, DV_B:]), jnp.exp(-m_t[i]))
        hb = jnp.concatenate([num_aug[:, :LANES] / den, num_aug[:, LANES:DV_B] / den], axis=1)
        c_scr[d, h] = dec[i] * c_scr[d, h] + upd[i]
        cidx = n if d == 0 else n_chunks - 1 - n
        rows = pl.ds(pl.multiple_of(cidx * CHUNK, CHUNK), CHUNK)
        tot = hb + o_scr[rows, vs]
        o_scr[rows, vs] = tot
        ogate = rest[d][:, OFF_BO + h * DV_B:OFF_BO + (h + 1) * DV_B]
        yb_ref[rows, vs] = (_rms(tot) * norm_ref[...] * ogate).astype(BF16)

    if emit_state:
        @pl.when(n == n_chunks - 1)
        def _():
            cout_ref[0] = c_scr[...]
            mout_ref[0] = m_scr[...]


def _mlstm(rest, gates, norm_b, c0, m0, batch, n_chunks, emit_state):
    t = n_chunks * CHUNK
    has_init = c0 is not None

    def fwd(b, n):
        return (b * n_chunks + n, 0)

    def bwd(b, n):
        return (b * n_chunks + n_chunks - 1 - n, 0)

    cspec = pl.BlockSpec((1, N_DIR, H_B, DK_B, CAUG_W), lambda b, n: (b, 0, 0, 0, 0))
    mspec = pl.BlockSpec((1, N_DIR * H_B, LANES), lambda b, n: (b, 0, 0))
    in_specs = [pl.BlockSpec((CHUNK, REST_W), fwd), pl.BlockSpec((CHUNK, REST_W), bwd),
                pl.BlockSpec((CHUNK, LANES), fwd), pl.BlockSpec((CHUNK, LANES), bwd),
                pl.BlockSpec((1, DV_B), lambda b, n: (0, 0))]
    args = [rest, rest, gates, gates, norm_b]
    if has_init:
        in_specs += [cspec, mspec]
        args += [c0, m0]
    out_specs = [pl.BlockSpec((t, B_W), lambda b, n: (b, 0))]
    out_shape = [jax.ShapeDtypeStruct((batch * t, B_W), BF16)]
    if emit_state:
        out_specs += [cspec, mspec]
        out_shape += [jax.ShapeDtypeStruct((batch, N_DIR, H_B, DK_B, CAUG_W), F32),
                      jax.ShapeDtypeStruct((batch, N_DIR * H_B, LANES), F32)]
    return pl.pallas_call(
        functools.partial(_mlstm_body, n_chunks=n_chunks, has_init=has_init, emit_state=emit_state),
        grid=(batch, n_chunks),
        in_specs=in_specs,
        out_specs=out_specs,
        out_shape=out_shape,
        scratch_shapes=[pltpu.VMEM((N_DIR, H_B, DK_B, CAUG_W), F32),
                        pltpu.VMEM((N_DIR * H_B, LANES), F32),
                        pltpu.VMEM((t, B_W), F32)],
        compiler_params=_params(("parallel", "arbitrary")),
        name="mlstm_scan",
    )(*args)


def _outproj_body(ya_ref, yb_ref, wa_ref, wb_ref, x_ref, mod_ref, post1_ref, pre2_ref, x1_ref, h2_ref):
    mix = (jnp.dot(ya_ref[...], wa_ref[...], preferred_element_type=F32)
           + jnp.dot(yb_ref[...], wb_ref[...], preferred_element_type=F32))
    x1 = x_ref[...] + mod_ref[0, 2:3, :] * (_rms(mix) * post1_ref[...])
    x1_ref[...] = x1
    h2 = _rms(x1) * pre2_ref[...] * (1.0 + mod_ref[0, 4:5, :]) + mod_ref[0, 3:4, :]
    h2_ref[...] = h2.astype(BF16)


def _outproj(ya, yb, w_out, x2d, mod3, mod_map, post1, pre2, tm):
    m = x2d.shape[0]
    row = lambda i: (i, 0)
    const = lambda i: (0, 0)
    return pl.pallas_call(
        _outproj_body,
        grid=(m // tm,),
        in_specs=[pl.BlockSpec((tm, A_W), row), pl.BlockSpec((tm, B_W), row),
                  pl.BlockSpec((A_W, D_MODEL), lambda i: (0, 0)),
                  pl.BlockSpec((B_W, D_MODEL), lambda i: (1, 0)),
                  pl.BlockSpec((tm, D_MODEL), row),
                  pl.BlockSpec((1, 6, D_MODEL), lambda i: (mod_map(i * tm), 0, 0)),
                  pl.BlockSpec((1, D_MODEL), const), pl.BlockSpec((1, D_MODEL), const)],
        out_specs=[pl.BlockSpec((tm, D_MODEL), row), pl.BlockSpec((tm, D_MODEL), row)],
        out_shape=[jax.ShapeDtypeStruct((m, D_MODEL), F32), jax.ShapeDtypeStruct((m, D_MODEL), BF16)],
        compiler_params=_params(("parallel",)),
        name="outproj",
    )(ya, yb, w_out, w_out, x2d, mod3, post1, pre2)


def _ffn_body(h2_ref, w1_ref, w2_ref, x1_ref, mod_ref, post2_ref, o_ref, acc_ref):
    kk = pl.program_id(1)
    a = jnp.maximum(jnp.dot(h2_ref[...], w1_ref[...], preferred_element_type=F32), 0.0)
    contrib = jnp.dot((a * a).astype(BF16), w2_ref[...], preferred_element_type=F32)

    @pl.when(kk == 0)
    def _():
        acc_ref[...] = contrib

    @pl.when(kk > 0)
    def _():
        acc_ref[...] += contrib

    @pl.when(kk == pl.num_programs(1) - 1)
    def _():
        o_ref[...] = x1_ref[...] + mod_ref[0, 5:6, :] * (_rms(acc_ref[...]) * post2_ref[...])


def _ffn(h2, w1, w2, x1, mod3, mod_map, post2, tm, fc):
    m = h2.shape[0]
    return pl.pallas_call(
        _ffn_body,
        grid=(m // tm, FFN // fc),
        in_specs=[pl.BlockSpec((tm, D_MODEL), lambda i, k: (i, 0)),
                  pl.BlockSpec((D_MODEL, fc), lambda i, k: (0, k)),
                  pl.BlockSpec((fc, D_MODEL), lambda i, k: (k, 0)),
                  pl.BlockSpec((tm, D_MODEL), lambda i, k: (i, 0)),
                  pl.BlockSpec((1, 6, D_MODEL), lambda i, k: (mod_map(i * tm), 0, 0)),
                  pl.BlockSpec((1, D_MODEL), lambda i, k: (0, 0))],
        out_specs=pl.BlockSpec((tm, D_MODEL), lambda i, k: (i, 0)),
        out_shape=jax.ShapeDtypeStruct((m, D_MODEL), F32),
        scratch_shapes=[pltpu.VMEM((tm, D_MODEL), F32)],
        compiler_params=_params(("parallel", "arbitrary")),
        name="ffn",
    )(h2, w1, w2, x1, mod3, post2)


def _block(x, mod3, mod_of_row, lp, init, seq_len, emit_state):
    bsz, t, _ = x.shape
    x2d = x.reshape(bsz * t, D_MODEL)
    n_chunks = t // CHUNK
    tile_mod = lambda i: mod_of_row(i * 1024)
    qkv = _proj(functools.partial(_proj_qkv_body, seq_len=seq_len), x2d, mod3, tile_mod,
                lp["pre1"], lp["w_qkv"], lp["conv_w"], 512, "proj_qkv")
    rest = _proj(functools.partial(_proj_rest_body, tn=512), x2d, mod3, tile_mod,
                 lp["pre1"], lp["w_rest"], None, 512, "proj_rest")
    gates = _proj(_proj_gate_body, x2d, mod3, tile_mod, lp["pre1"], lp["w_gate"], lp["gate_p"],
                  LANES, "proj_gate")
    s0, c0, m0 = init if init is not None else (None, None, None)
    d_out = _delta(qkv, rest, gates, lp["norm_a"], s0, bsz, n_chunks, emit_state)
    m_out = _mlstm(rest, gates, lp["norm_b"], c0, m0, bsz, n_chunks, emit_state)
    x1, h2 = _outproj(d_out[0], m_out[0], lp["w_out"], x2d, mod3, mod_of_row, lp["post1"], lp["pre2"], 256)
    y = _ffn(h2, lp["w1"], lp["w2"], x1, mod3, mod_of_row, lp["post2"], 512, 512)
    states = None
    if emit_state:
        states = (d_out[1], m_out[1][..., :DV_B], m_out[1][..., DV_B], m_out[2][..., 0].reshape(bsz, N_DIR, H_B))
    return y.reshape(bsz, t, D_MODEL), states


def _layer_params(l, norm_mix_pre, norm_mix_post, norm_ffn_pre, norm_ffn_post, w_in, conv_w, a_log, dt_bias,
                  norm_a, mlstm_ibias, mlstm_fbias, norm_b, w_out, w_ffn1, w_ffn2):
    w = w_in[l]
    o_ag = QKV_W
    o_aa = o_ag + A_W
    o_ab = o_aa + N_DIR * H_A
    o_bq = o_ab + N_DIR * H_A
    o_bi = o_bq + 2 * H_B * DK_B + 2 * B_W
    o_bf = o_bi + N_DIR * H_B
    n_gate = 2 * N_DIR * H_A + 2 * N_DIR * H_B
    w_gate = jnp.concatenate([w[:, o_aa:o_bq], w[:, o_bi:o_bf + N_DIR * H_B],
                              jnp.zeros((D_MODEL, LANES - n_gate), F32)], axis=1)

    def lane_row(vals, off):
        return jnp.zeros((LANES,), F32).at[off:off + vals.size].set(vals.reshape(-1))

    gate_p = jnp.stack([lane_row(a_log[l], G_OFF), lane_row(dt_bias[l], G_OFF),
                        lane_row(mlstm_ibias[l], LI_OFF) + lane_row(mlstm_fbias[l], LF_OFF)]
                       + [jnp.zeros((LANES,), F32)] * 5)
    row = lambda v: v[l].reshape(1, -1)
    return dict(
        pre1=row(norm_mix_pre), post1=row(norm_mix_post), pre2=row(norm_ffn_pre), post2=row(norm_ffn_post),
        w_qkv=w[:, :QKV_W].astype(BF16),
        w_rest=jnp.concatenate([w[:, o_ag:o_aa], w[:, o_bq:o_bi]], axis=1).astype(BF16),
        w_gate=w_gate.astype(BF16), gate_p=gate_p,
        conv_w=jnp.concatenate([conv_w[l].T, jnp.zeros((5, QKV_W), F32)], axis=0),
        norm_a=row(norm_a), norm_b=row(norm_b),
        w_out=w_out[l].astype(BF16), w1=w_ffn1[l].astype(BF16), w2=w_ffn2[l].astype(BF16))


def kernel(x_prompt, x_sample, state_delta, state_mlstm_C, state_mlstm_n, state_mlstm_m, c, c_ctx, w_ada, b_ada, norm_mix_pre, norm_mix_post, norm_ffn_pre, norm_ffn_post, w_in, conv_w, a_log, dt_bias, norm_a, mlstm_ibias, mlstm_fbias, norm_b, w_out, w_ffn1, w_ffn2):
    depth = w_in.shape[0]
    n_lat = x_sample.shape[0]
    t_lat = x_sample.shape[1]
    cond = jnp.concatenate([c_ctx[None, :], c, jnp.zeros((8 - 1 - n_lat, D_MODEL), F32)], axis=0)
    y_prompt, y_sample = x_prompt, x_sample
    acc = ([], [], [], [])
    for l in range(depth):
        lp = _layer_params(l, norm_mix_pre, norm_mix_post, norm_ffn_pre, norm_ffn_post, w_in, conv_w, a_log,
                           dt_bias, norm_a, mlstm_ibias, mlstm_fbias, norm_b, w_out, w_ffn1, w_ffn2)
        mod = _ada(cond, w_ada[l], b_ada[l].reshape(1, -1))
        mod3 = mod[:1 + n_lat].reshape(1 + n_lat, 6, D_MODEL)
        y_prompt, st = _block(y_prompt, mod3, lambda r: 0, lp, None, x_prompt.shape[1], True)
        for a, s in zip(acc, st):
            a.append(s)
        c_aug0 = jnp.concatenate(
            [state_mlstm_C[:, l],
             jnp.broadcast_to(state_mlstm_n[:, l][..., None], state_mlstm_n[:, l].shape + (LANES,))], axis=-1)
        m0 = jnp.broadcast_to(state_mlstm_m[:, l].reshape(n_lat, N_DIR * H_B, 1), (n_lat, N_DIR * H_B, LANES))
        y_sample, _ = _block(y_sample, mod3, lambda r: 1 + r // t_lat, lp, (state_delta[:, l], c_aug0, m0),
                             GRID_W, False)
    return (y_prompt, y_sample) + tuple(jnp.stack(a, axis=1) for a in acc)
```

```python
import functools

import jax
import jax.numpy as jnp
from jax import lax
from jax.experimental import pallas as pl
from jax.experimental.pallas import tpu as pltpu

F32 = jnp.float32
BF16 = jnp.bfloat16

D_MODEL = 2048
N_DIR = 2
A_W = D_MODEL // 2
B_W = D_MODEL - A_W
DK_A = 128
DV_A = 128
H_A = A_W // DV_A
DV_B = 256
DK_B = DV_B // 2
H_B = B_W // DV_B
GRID_W = 64
CHUNK = 64
FFN = 4 * D_MODEL
EPS = 1e-6
LANES = 128
NEG = -1e30

QKV_W = 3 * A_W
REST_W = A_W + 2 * H_B * DK_B + 2 * B_W
OFF_AG, OFF_BQ, OFF_BK, OFF_BV, OFF_BO = 0, A_W, A_W + H_B * DK_B, A_W + 2 * H_B * DK_B, A_W + 2 * H_B * DK_B + B_W
G_OFF, BETA_OFF = 0, N_DIR * H_A
LI_OFF = 2 * N_DIR * H_A
LF_OFF = LI_OFF + N_DIR * H_B

VMEM_LIMIT = 56 * 1024 * 1024


def _sigmoid(x):
    return 1.0 / (1.0 + jnp.exp(-x))


def _softplus(x):
    return jnp.maximum(x, 0.0) + jnp.log1p(jnp.exp(-jnp.abs(x)))


def _dot(a, b):
    return jnp.dot(a.astype(BF16), b.astype(BF16), preferred_element_type=F32)


def _dot_f32(a, b):
    return jnp.dot(a, b, precision=lax.Precision.HIGHEST, preferred_element_type=F32)


def _rms(x):
    return x * lax.rsqrt(jnp.mean(x * x, axis=-1, keepdims=True) + EPS)


def _params(sem):
    return pltpu.CompilerParams(dimension_semantics=sem, vmem_limit_bytes=VMEM_LIMIT)


def _ada_body(c_ref, w_ref, b_ref, o_ref):
    c = c_ref[...]
    o_ref[...] = _dot(c * _sigmoid(c), w_ref[...]) + b_ref[...]


def _ada(c_all, w_ada, b, l):
    n = w_ada.shape[2]
    tn = 1024
    return pl.pallas_call(
        _ada_body,
        grid=(n // tn,),
        in_specs=[pl.BlockSpec(c_all.shape, lambda j: (0, 0)),
                  pl.BlockSpec((None, D_MODEL, tn), lambda j: (l, 0, j)),
                  pl.BlockSpec((1, tn), lambda j: (0, j))],
        out_specs=pl.BlockSpec((c_all.shape[0], tn), lambda j: (0, j)),
        out_shape=jax.ShapeDtypeStruct((c_all.shape[0], n), F32),
        compiler_params=_params(("arbitrary",)),
        name="ada",
    )(c_all, w_ada, b)


PROJ_TM = 1024
PROJ_RB = 256
PROJ_TN = 512
N_QKV_T = QKV_W // PROJ_TN
N_REST_T = REST_W // PROJ_TN
PROJ_W_PAD = (N_QKV_T + N_REST_T + 1) * PROJ_TN


def _proj_body(x_ref, mod_ref, g_ref, w_ref, cw_ref, gp_ref, qkv_ref, rest_ref, gate_ref, h_scr, *, seq_len):
    j = pl.program_id(1)

    @pl.when(j == 0)
    def _():
        y = _rms(x_ref[...]) * g_ref[...]
        h_scr[...] = (y * (1.0 + mod_ref[0, 1:2, :]) + mod_ref[0, 0:1, :]).astype(BF16)

    def run(n_cols, epilogue, out_ref):
        for rb in range(PROJ_TM // PROJ_RB):
            rs = slice(rb * PROJ_RB, (rb + 1) * PROJ_RB)
            epilogue(jnp.dot(h_scr[rs, :], w_ref[:, :n_cols], preferred_element_type=F32), out_ref, rs)

    def conv_silu(acc):
        pos = lax.broadcasted_iota(jnp.int32, acc.shape, 0) & (seq_len - 1)
        prev = jnp.where(pos == 0, 0.0, pltpu.roll(acc, 1, 0))
        nxt = jnp.where(pos == seq_len - 1, 0.0, pltpu.roll(acc, acc.shape[0] - 1, 0))
        y = prev * cw_ref[0:1, :] + acc * cw_ref[1:2, :] + nxt * cw_ref[2:3, :]
        return y * _sigmoid(y)

    def ep_l2(scale):
        def f(acc, out_ref, rs):
            y = conv_silu(acc)
            for g in range(PROJ_TN // LANES):
                blk = y[:, g * LANES:(g + 1) * LANES]
                inv = lax.rsqrt(jnp.sum(blk * blk, axis=-1, keepdims=True) + EPS)
                out_ref[rs, g * LANES:(g + 1) * LANES] = blk * (inv * scale)
        return f

    def ep_map(fn):
        def f(acc, out_ref, rs):
            out_ref[rs, :] = fn(acc)
        return f

    def ep_gate(z, out_ref, rs):
        lane = lax.broadcasted_iota(jnp.int32, z.shape, 1)
        g = -jnp.exp(gp_ref[0:1, :]) * _softplus(z + gp_ref[1:2, :])
        li = z + gp_ref[2:3, :]
        out_ref[rs, :] = jnp.where(lane < BETA_OFF, g,
                                   jnp.where(lane < LI_OFF, _sigmoid(z),
                                             jnp.where(lane < LF_OFF, li,
                                                       jnp.where(lane < LF_OFF + N_DIR * H_B, -_softplus(-li), 0.0))))

    tp = A_W // PROJ_TN
    c0 = (j - N_QKV_T) * PROJ_TN
    in_rest = (j >= N_QKV_T) & (j < N_QKV_T + N_REST_T)
    variants = [
        (j < tp, PROJ_TN, ep_l2(DK_A ** -0.5), qkv_ref),
        ((j >= tp) & (j < 2 * tp), PROJ_TN, ep_l2(1.0), qkv_ref),
        ((j >= 2 * tp) & (j < N_QKV_T), PROJ_TN, ep_map(conv_silu), qkv_ref),
        (in_rest & (c0 < OFF_BQ), PROJ_TN, ep_map(lambda a: a * _sigmoid(a)), rest_ref),
        (in_rest & (c0 >= OFF_BQ) & (c0 < OFF_BK), PROJ_TN, ep_map(lambda a: a * (DK_B ** -0.5)), rest_ref),
        (in_rest & (c0 >= OFF_BK) & (c0 < OFF_BO), PROJ_TN, ep_map(lambda a: a), rest_ref),
        (in_rest & (c0 >= OFF_BO), PROJ_TN, ep_map(_sigmoid), rest_ref),
        (j == N_QKV_T + N_REST_T, LANES, ep_gate, gate_ref),
    ]
    for cond, n_cols, epilogue, out_ref in variants:
        pl.when(cond)(functools.partial(run, n_cols, epilogue, out_ref))


def _proj(x2d, mod3, mod_map, pre_g, w, conv_w, gate_p, seq_len):
    m = x2d.shape[0]
    tm, tn = PROJ_TM, PROJ_TN
    assert PROJ_RB % seq_len == 0 and m % tm == 0
    return pl.pallas_call(
        functools.partial(_proj_body, seq_len=seq_len),
        grid=(m // tm, PROJ_W_PAD // tn),
        in_specs=[pl.BlockSpec((tm, D_MODEL), lambda i, j: (i, 0)),
                  pl.BlockSpec((1, 6, D_MODEL), lambda i, j: (mod_map(i), 0, 0)),
                  pl.BlockSpec((1, D_MODEL), lambda i, j: (0, 0)),
                  pl.BlockSpec((D_MODEL, tn), lambda i, j: (0, j)),
                  pl.BlockSpec((8, tn), lambda i, j: (0, jnp.minimum(j, N_QKV_T - 1))),
                  pl.BlockSpec((8, LANES), lambda i, j: (0, 0))],
        out_specs=[pl.BlockSpec((tm, tn), lambda i, j: (i, jnp.minimum(j, N_QKV_T - 1))),
                   pl.BlockSpec((tm, tn), lambda i, j: (i, jnp.clip(j - N_QKV_T, 0, N_REST_T - 1))),
                   pl.BlockSpec((tm, LANES), lambda i, j: (i, 0))],
        out_shape=[jax.ShapeDtypeStruct((m, QKV_W), F32), jax.ShapeDtypeStruct((m, REST_W), F32),
                   jax.ShapeDtypeStruct((m, LANES), F32)],
        scratch_shapes=[pltpu.VMEM((tm, D_MODEL), BF16)],
        compiler_params=_params(("parallel", "arbitrary")),
        name="proj",
    )(x2d, mod3, pre_g, w, conv_w, gate_p)


def _masks(d):
    row = lax.broadcasted_iota(jnp.int32, (CHUNK, CHUNK), 0)
    col = lax.broadcasted_iota(jnp.int32, (CHUNK, CHUNK), 1)
    if d == 0:
        return row >= col, row > col, row == col
    return row <= col, row < col, row == col


def _transpose_chunk(x):
    full = jnp.concatenate([x, jnp.zeros_like(x)], axis=0)
    return full.T[:, :CHUNK]


def _gate_cumsums(gates, incl):
    gc = _dot_f32(incl.astype(F32), gates)
    return gc, _transpose_chunk(gc), _transpose_chunk(gates)


GROUP = 4
N_GROUPS = H_A // GROUP
PAIR = LANES // CHUNK


def _split(x):
    hi = x.astype(BF16)
    return hi, (x - hi.astype(F32)).astype(BF16)


def _store_blocks(ref, x, blk_r, blk_c, col0=0):
    for h in range(GROUP):
        ref[h * blk_r:(h + 1) * blk_r, col0 + h * blk_c:col0 + (h + 1) * blk_c] = x[:, h * blk_c:(h + 1) * blk_c]


def _dot3(a_hi, a_lo, b_hi, b_lo):
    lhs = jnp.concatenate([a_hi, a_lo, a_hi], axis=1)
    rhs = jnp.concatenate([b_hi, b_hi, b_lo], axis=0)
    return jnp.dot(lhs, rhs, preferred_element_type=F32)


def _tri_inverse_wide(lmats, eye_w, bd_hi, bd_lo):
    def bd_dot(i, a_hi, a_lo, b_hi, b_lo):
        _store_blocks(bd_hi.at[i], b_hi, CHUNK, CHUNK)
        _store_blocks(bd_lo.at[i], b_lo, CHUNK, CHUNK)
        return _dot3(a_hi, a_lo, bd_hi[i], bd_lo[i])

    idx = range(len(lmats))
    s = [eye_w - l for l in lmats]
    p = []
    for i in idx:
        m_hi, m_lo = _split(-lmats[i])
        p.append(bd_dot(i, m_hi, m_lo, m_hi, m_lo))
    for _ in range(4):
        for i in idx:
            p_hi, p_lo = _split(p[i])
            s_hi, s_lo = _split(s[i])
            r = bd_dot(i, jnp.concatenate([p_hi, s_hi], axis=0), jnp.concatenate([p_lo, s_lo], axis=0), p_hi, p_lo)
            p[i] = r[:CHUNK]
            s[i] = s[i] + r[CHUNK:]
    for i in idx:
        p_hi, p_lo = _split(p[i])
        s_hi, s_lo = _split(s[i])
        s[i] = s[i] + bd_dot(i, s_hi, s_lo, p_hi, p_lo)
    return s


def _col_bcast(tile, c, width=LANES):
    return jnp.broadcast_to(tile[:, c:c + 1], (tile.shape[0], width))


def _delta_body(*refs, n_chunks, has_init, emit_state):
    qkv = refs[0:2]
    ag = refs[2:4]
    gt = refs[4:6]
    norm_ref = refs[6]
    pos = 7
    s0_ref = None
    if has_init:
        s0_ref = refs[pos]
        pos += 1
    ya_ref = refs[pos]
    pos += 1
    sout_ref = None
    if emit_state:
        sout_ref = refs[pos]
        pos += 1
    s_scr, o_scr, bdn_hi, bdn_lo, bdk, bduw_hi, bduw_lo, bds, bdv = refs[pos:pos + 9]

    n = pl.program_id(1)
    gw = GROUP * DK_A

    @pl.when(n == 0)
    def _():
        o_scr[...] = jnp.zeros_like(o_scr)
        for ref in (bdn_hi, bdn_lo, bdk, bduw_hi, bduw_lo, bds, bdv):
            ref[...] = jnp.zeros_like(ref)
        for d in range(N_DIR):
            for h in range(H_A):
                blk = s0_ref[0, d, h] if has_init else jnp.zeros((DK_A, DV_A), F32)
                s_scr[d, h // GROUP, :, (h % GROUP) * DV_A:(h % GROUP + 1) * DV_A] = blk

    row = lax.broadcasted_iota(jnp.int32, (CHUNK, LANES), 0)
    lane = lax.broadcasted_iota(jnp.int32, (CHUNK, LANES), 1)
    col = lane & (CHUNK - 1)
    left = lane < CHUNK
    eye_w = jnp.concatenate([(row == col).astype(F32)] * (GROUP // PAIR), axis=1)
    probs = [(d, g) for d in range(N_DIR) for g in range(N_GROUPS)]
    gates, gc, gc_t, eg, egl, masks = {}, {}, {}, {}, {}, {}
    for d in range(N_DIR):
        masks[d] = (row >= col, row > col) if d == 0 else (row <= col, row < col)
        gates[d] = gt[d][...]
        gc[d] = _dot_f32(_masks(d)[0].astype(F32), gates[d])
        gc_t[d] = jnp.concatenate([gc[d], gc[d]], axis=0).T
        last = CHUNK - 1 if d == 0 else 0
        eg[d] = jnp.exp(gc[d])
        egl[d] = jnp.exp(gc[d][last:last + 1, :] - gc[d])

    def cols_of(d, g):
        return [G_OFF + d * H_A + g * GROUP + hl for hl in range(GROUP)]

    q16, ks, beta_xs, decays, grams = [], [], [], [], []
    for gi, (d, g) in enumerate(probs):
        cols = cols_of(d, g)
        incl, _ = masks[d]
        k = qkv[d][:, A_W + g * gw:A_W + (g + 1) * gw]
        beta_x = jnp.concatenate([_col_bcast(gates[d], BETA_OFF - G_OFF + c) for c in cols], axis=1)
        decay = []
        for p in range(GROUP // PAIR):
            c0, c1 = cols[PAIR * p], cols[PAIR * p + 1]
            gcol = jnp.where(left, _col_bcast(gc[d], c0), _col_bcast(gc[d], c1))
            grow = jnp.where(left[0:1], gc_t[d][c0:c0 + 1, :], gc_t[d][c1:c1 + 1, :])
            decay.append(jnp.exp(jnp.where(incl, gcol - grow, NEG)))
        decays.append(jnp.concatenate(decay, axis=1))
        q16.append(qkv[d][:, g * gw:(g + 1) * gw].astype(BF16))
        ks.append(k)
        beta_xs.append(beta_x)
        _store_blocks(bdk.at[gi], k.astype(BF16), CHUNK, DK_A)
    for gi, (d, g) in enumerate(probs):
        grams.append(lax.dot_general(jnp.concatenate([q16[gi], (ks[gi] * beta_xs[gi]).astype(BF16)], axis=0),
                                     bdk[gi], (((1,), (1,)), ((), ())), preferred_element_type=F32))
    attns, lmats = [], []
    for gi, (d, g) in enumerate(probs):
        strict_w = jnp.concatenate([masks[d][1]] * (GROUP // PAIR), axis=1)
        attns.append((grams[gi][:CHUNK] * decays[gi]).astype(BF16))
        lmats.append(jnp.where(strict_w, grams[gi][CHUNK:] * decays[gi], 0.0))
    ainvs = _tri_inverse_wide(lmats, eye_w, bdn_hi, bdn_lo)
    eg_xs, kd_ts = [], []
    for gi, (d, g) in enumerate(probs):
        cols = cols_of(d, g)
        v = qkv[d][:, 2 * A_W + g * gw:2 * A_W + (g + 1) * gw]
        eg_x = jnp.concatenate([_col_bcast(eg[d], c) for c in cols], axis=1)
        egl_x = jnp.concatenate([_col_bcast(egl[d], c) for c in cols], axis=1)
        eg_xs.append(eg_x)
        bv_hi, bv_lo = _split(v * beta_xs[gi])
        bk_hi, bk_lo = _split(ks[gi] * (beta_xs[gi] * eg_x))
        _store_blocks(bduw_hi.at[gi], bv_hi, CHUNK, DV_A)
        _store_blocks(bduw_lo.at[gi], bv_lo, CHUNK, DV_A)
        _store_blocks(bduw_hi.at[gi], bk_hi, CHUNK, DK_A, col0=gw)
        _store_blocks(bduw_lo.at[gi], bk_lo, CHUNK, DK_A, col0=gw)
        kd = ks[gi] * egl_x
        kd_ts.append(jnp.concatenate([kd[:, hl * DK_A:(hl + 1) * DK_A] for hl in range(GROUP)], axis=0)
                     .T.astype(BF16))
        _store_blocks(bds.at[gi], s_scr[d, g].astype(BF16), DK_A, DV_A)
    uws = []
    for gi in range(len(probs)):
        t_hi, t_lo = _split(ainvs[gi])
        uws.append(_dot3(t_hi, t_lo, bduw_hi[gi], bduw_lo[gi]))
    ws_qs = []
    for gi in range(len(probs)):
        ws_qs.append(jnp.dot(jnp.concatenate([uws[gi][:, gw:].astype(BF16), q16[gi]], axis=0), bds[gi],
                             preferred_element_type=F32))
    for gi in range(len(probs)):
        _store_blocks(bdv.at[gi], (uws[gi][:, :gw] - ws_qs[gi][:CHUNK]).astype(BF16), CHUNK, DV_A)
    rs = []
    for gi in range(len(probs)):
        rs.append(jnp.dot(jnp.concatenate([attns[gi], kd_ts[gi]], axis=0), bdv[gi], preferred_element_type=F32))
    for gi, (d, g) in enumerate(probs):
        cols = cols_of(d, g)
        last = CHUNK - 1 if d == 0 else 0
        o = ws_qs[gi][CHUNK:] * eg_xs[gi] + rs[gi][:CHUNK]
        eg_last = jnp.concatenate(
            [jnp.broadcast_to(eg[d][last:last + 1, c:c + 1], (1, DV_A)) for c in cols], axis=1)
        s_scr[d, g] = s_scr[d, g] * eg_last + rs[gi][CHUNK:]
        cidx = n if d == 0 else n_chunks - 1 - n
        rows = pl.ds(pl.multiple_of(cidx * CHUNK, CHUNK), CHUNK)
        tot = o + o_scr[rows, g * gw:(g + 1) * gw]
        o_scr[rows, g * gw:(g + 1) * gw] = tot
        for hl in range(GROUP):
            hs = slice((g * GROUP + hl) * DV_A, (g * GROUP + hl + 1) * DV_A)
            blk = tot[:, hl * DV_A:(hl + 1) * DV_A]
            ya_ref[rows, hs] = (_rms(blk) * norm_ref[...] * ag[d][:, hs]).astype(BF16)

    if emit_state:
        @pl.when(n == n_chunks - 1)
        def _():
            for d in range(N_DIR):
                for h in range(H_A):
                    sout_ref[0, d, h] = s_scr[d, h // GROUP, :, (h % GROUP) * DV_A:(h % GROUP + 1) * DV_A]


def _delta(qkv, rest, gates, norm_a, s0, batch, n_chunks, emit_state):
    t = n_chunks * CHUNK
    has_init = s0 is not None
    ng = N_DIR * N_GROUPS

    def fwd(b, n):
        return (b * n_chunks + n, 0)

    def bwd(b, n):
        return (b * n_chunks + n_chunks - 1 - n, 0)

    in_specs = [pl.BlockSpec((CHUNK, QKV_W), fwd), pl.BlockSpec((CHUNK, QKV_W), bwd),
                pl.BlockSpec((CHUNK, A_W), fwd), pl.BlockSpec((CHUNK, A_W), bwd),
                pl.BlockSpec((CHUNK, LANES), fwd), pl.BlockSpec((CHUNK, LANES), bwd),
                pl.BlockSpec((1, DV_A), lambda b, n: (0, 0))]
    args = [qkv, qkv, rest, rest, gates, gates, norm_a]
    if has_init:
        in_specs.append(pl.BlockSpec((1, N_DIR, H_A, DK_A, DV_A), lambda b, n: (b, 0, 0, 0, 0)))
        args.append(s0)
    out_specs = [pl.BlockSpec((t, A_W), lambda b, n: (b, 0))]
    out_shape = [jax.ShapeDtypeStruct((batch * t, A_W), BF16)]
    if emit_state:
        out_specs.append(pl.BlockSpec((1, N_DIR, H_A, DK_A, DV_A), lambda b, n: (b, 0, 0, 0, 0)))
        out_shape.append(jax.ShapeDtypeStruct((batch, N_DIR, H_A, DK_A, DV_A), F32))
    return pl.pallas_call(
        functools.partial(_delta_body, n_chunks=n_chunks, has_init=has_init, emit_state=emit_state),
        grid=(batch, n_chunks),
        in_specs=in_specs,
        out_specs=out_specs,
        out_shape=out_shape,
        scratch_shapes=[pltpu.VMEM((N_DIR, N_GROUPS, DK_A, GROUP * DV_A), F32), pltpu.VMEM((t, A_W), F32),
                        pltpu.VMEM((ng, GROUP * CHUNK, GROUP * CHUNK), BF16),
                        pltpu.VMEM((ng, GROUP * CHUNK, GROUP * CHUNK), BF16),
                        pltpu.VMEM((ng, GROUP * CHUNK, GROUP * DK_A), BF16),
                        pltpu.VMEM((ng, GROUP * CHUNK, 2 * GROUP * DK_A), BF16),
                        pltpu.VMEM((ng, GROUP * CHUNK, 2 * GROUP * DK_A), BF16),
                        pltpu.VMEM((ng, GROUP * DK_A, GROUP * DV_A), BF16),
                        pltpu.VMEM((ng, GROUP * CHUNK, GROUP * DV_A), BF16)],
        compiler_params=_params(("parallel", "arbitrary")),
        name="delta_scan",
    )(*args)


CAUG_W = DV_B + LANES


def _mlstm_body(*refs, n_chunks, has_init, emit_state):
    rest = refs[0:2]
    gt = refs[2:4]
    norm_ref = refs[4]
    pos = 5
    c0_ref = m0_ref = None
    if has_init:
        c0_ref, m0_ref = refs[pos], refs[pos + 1]
        pos += 2
    yb_ref = refs[pos]
    pos += 1
    cout_ref = mout_ref = None
    if emit_state:
        cout_ref, mout_ref = refs[pos], refs[pos + 1]
        pos += 2
    c_scr, m_scr, o_scr = refs[pos], refs[pos + 1], refs[pos + 2]

    n = pl.program_id(1)

    @pl.when(n == 0)
    def _():
        o_scr[...] = jnp.zeros_like(o_scr)
        if has_init:
            c_scr[...] = c0_ref[0]
            m_scr[...] = m0_ref[0]
        else:
            c_scr[...] = jnp.zeros_like(c_scr)
            m_scr[...] = jnp.zeros_like(m_scr)

    ones = jnp.ones((CHUNK, LANES), BF16)
    probs = [(d, h) for d in range(N_DIR) for h in range(H_B)]
    idx = range(len(probs))
    gcs = {}
    for d in range(N_DIR):
        incl, _, _ = _masks(d)
        gates = gt[d][...]
        gcs[d] = (incl,) + _gate_cumsums(gates, incl)
    q16, k_t, v_aug, qk = [], [], [], []
    for d, h in probs:
        q16.append(rest[d][:, OFF_BQ + h * DK_B:OFF_BQ + (h + 1) * DK_B].astype(BF16))
        k_t.append(_transpose_chunk(rest[d][:, OFF_BK + h * DK_B:OFF_BK + (h + 1) * DK_B]))
        v = rest[d][:, OFF_BV + h * DV_B:OFF_BV + (h + 1) * DV_B]
        v_aug.append(jnp.concatenate([v.astype(BF16), ones], axis=1))
    for i in idx:
        qk.append(_dot(q16[i], k_t[i]))
    qc = [_dot(q16[i], c_scr[d, h]) for i, (d, h) in enumerate(probs)]
    dw, iw, m_t, ks_t, dec = [], [], [], [], []
    for i, (d, h) in enumerate(probs):
        incl, gc, gc_t, gates_t = gcs[d]
        c = d * H_B + h
        last = CHUNK - 1 if d == 0 else 0
        bcol = gc[:, LF_OFF + c:LF_OFF + c + 1]
        brow = gc_t[LF_OFF + c:LF_OFF + c + 1, :]
        irow = gates_t[LI_OFF + c:LI_OFF + c + 1, :]
        blast = gc[last:last + 1, LF_OFF + c:LF_OFF + c + 1]
        m_old = m_scr[c:c + 1, 0:1]
        log_d = jnp.where(incl, bcol - brow + irow, NEG)
        inter = bcol + m_old
        m_t.append(jnp.maximum(inter, jnp.max(log_d, axis=-1, keepdims=True)))
        dw.append(jnp.exp(log_d - m_t[i]) * qk[i])
        iw.append(jnp.exp(inter - m_t[i]))
        le = blast - brow + irow
        b_last = blast + m_old
        m_new = jnp.maximum(b_last, jnp.max(le, axis=-1, keepdims=True))
        ks_t.append(k_t[i] * jnp.exp(le - m_new))
        dec.append(jnp.exp(b_last - m_new))
        m_scr[c:c + 1, :] = jnp.broadcast_to(m_new, (1, LANES))
    intra = [_dot(dw[i], v_aug[i]) for i in idx]
    upd = [_dot(ks_t[i], v_aug[i]) for i in idx]
    for i, (d, h) in enumerate(probs):
        vs = slice(h * DV_B, (h + 1) * DV_B)
        num_aug = iw[i] * qc[i] + intra[i]
        den = jnp.maximum(jnp.abs(num_aug[:, DV_B:]), jnp.exp(-m_t[i]))
        hb = jnp.concatenate([num_aug[:, :LANES] / den, num_aug[:, LANES:DV_B] / den], axis=1)
        c_scr[d, h] = dec[i] * c_scr[d, h] + upd[i]
        cidx = n if d == 0 else n_chunks - 1 - n
        rows = pl.ds(pl.multiple_of(cidx * CHUNK, CHUNK), CHUNK)
        tot = hb + o_scr[rows, vs]
        o_scr[rows, vs] = tot
        ogate = rest[d][:, OFF_BO + h * DV_B:OFF_BO + (h + 1) * DV_B]
        yb_ref[rows, vs] = (_rms(tot) * norm_ref[...] * ogate).astype(BF16)

    if emit_state:
        @pl.when(n == n_chunks - 1)
        def _():
            cout_ref[0] = c_scr[...]
            mout_ref[0] = m_scr[...]


def _mlstm(rest, gates, norm_b, c0, m0, batch, n_chunks, emit_state):
    t = n_chunks * CHUNK
    has_init = c0 is not None

    def fwd(b, n):
        return (b * n_chunks + n, 0)

    def bwd(b, n):
        return (b * n_chunks + n_chunks - 1 - n, 0)

    cspec = pl.BlockSpec((1, N_DIR, H_B, DK_B, CAUG_W), lambda b, n: (b, 0, 0, 0, 0))
    mspec = pl.BlockSpec((1, N_DIR * H_B, LANES), lambda b, n: (b, 0, 0))
    in_specs = [pl.BlockSpec((CHUNK, REST_W), fwd), pl.BlockSpec((CHUNK, REST_W), bwd),
                pl.BlockSpec((CHUNK, LANES), fwd), pl.BlockSpec((CHUNK, LANES), bwd),
                pl.BlockSpec((1, DV_B), lambda b, n: (0, 0))]
    args = [rest, rest, gates, gates, norm_b]
    if has_init:
        in_specs += [cspec, mspec]
        args += [c0, m0]
    out_specs = [pl.BlockSpec((t, B_W), lambda b, n: (b, 0))]
    out_shape = [jax.ShapeDtypeStruct((batch * t, B_W), BF16)]
    if emit_state:
        out_specs += [cspec, mspec]
        out_shape += [jax.ShapeDtypeStruct((batch, N_DIR, H_B, DK_B, CAUG_W), F32),
                      jax.ShapeDtypeStruct((batch, N_DIR * H_B, LANES), F32)]
    return pl.pallas_call(
        functools.partial(_mlstm_body, n_chunks=n_chunks, has_init=has_init, emit_state=emit_state),
        grid=(batch, n_chunks),
        in_specs=in_specs,
        out_specs=out_specs,
        out_shape=out_shape,
        scratch_shapes=[pltpu.VMEM((N_DIR, H_B, DK_B, CAUG_W), F32),
                        pltpu.VMEM((N_DIR * H_B, LANES), F32),
                        pltpu.VMEM((t, B_W), F32)],
        compiler_params=_params(("parallel", "arbitrary")),
        name="mlstm_scan",
    )(*args)


def _outproj_body(ya_ref, yb_ref, wa_ref, wb_ref, x_ref, mod_ref, post1_ref, pre2_ref, x1_ref, h2_ref):
    mix = (jnp.dot(ya_ref[...], wa_ref[...], preferred_element_type=F32)
           + jnp.dot(yb_ref[...], wb_ref[...], preferred_element_type=F32))
    x1 = x_ref[...] + mod_ref[0, 2:3, :] * (_rms(mix) * post1_ref[...])
    x1_ref[...] = x1
    h2 = _rms(x1) * pre2_ref[...] * (1.0 + mod_ref[0, 4:5, :]) + mod_ref[0, 3:4, :]
    h2_ref[...] = h2.astype(BF16)


def _outproj(ya, yb, w_out, x2d, mod3, mod_map, post1, pre2, tm):
    m = x2d.shape[0]
    row = lambda i: (i, 0)
    const = lambda i: (0, 0)
    return pl.pallas_call(
        _outproj_body,
        grid=(m // tm,),
        in_specs=[pl.BlockSpec((tm, A_W), row), pl.BlockSpec((tm, B_W), row),
                  pl.BlockSpec((A_W, D_MODEL), lambda i: (0, 0)),
                  pl.BlockSpec((B_W, D_MODEL), lambda i: (1, 0)),
                  pl.BlockSpec((tm, D_MODEL), row),
                  pl.BlockSpec((1, 6, D_MODEL), lambda i: (mod_map(i * tm), 0, 0)),
                  pl.BlockSpec((1, D_MODEL), const), pl.BlockSpec((1, D_MODEL), const)],
        out_specs=[pl.BlockSpec((tm, D_MODEL), row), pl.BlockSpec((tm, D_MODEL), row)],
        out_shape=[jax.ShapeDtypeStruct((m, D_MODEL), F32), jax.ShapeDtypeStruct((m, D_MODEL), BF16)],
        compiler_params=_params(("parallel",)),
        name="outproj",
    )(ya, yb, w_out, w_out, x2d, mod3, post1, pre2)


def _ffn_body(h2_ref, w1_ref, w2_ref, x1_ref, mod_ref, post2_ref, o_ref):
    kk = pl.program_id(1)
    a = jnp.maximum(jnp.dot(h2_ref[...], w1_ref[...], preferred_element_type=F32), 0.0)
    contrib = jnp.dot((a * a).astype(BF16), w2_ref[...], preferred_element_type=F32)

    @pl.when(kk == 0)
    def _():
        o_ref[...] = contrib

    @pl.when((kk > 0) & (kk < pl.num_programs(1) - 1))
    def _():
        o_ref[...] += contrib

    @pl.when(kk == pl.num_programs(1) - 1)
    def _():
        f = o_ref[...] + contrib
        o_ref[...] = x1_ref[...] + mod_ref[0, 5:6, :] * (_rms(f) * post2_ref[...])


def _ffn(h2, w1, w2, x1, mod3, mod_map, post2, tm, fc):
    m = h2.shape[0]
    return pl.pallas_call(
        _ffn_body,
        grid=(m // tm, FFN // fc),
        in_specs=[pl.BlockSpec((tm, D_MODEL), lambda i, k: (i, 0)),
                  pl.BlockSpec((D_MODEL, fc), lambda i, k: (0, k)),
                  pl.BlockSpec((fc, D_MODEL), lambda i, k: (k, 0)),
                  pl.BlockSpec((tm, D_MODEL), lambda i, k: (i, 0)),
                  pl.BlockSpec((1, 6, D_MODEL), lambda i, k: (mod_map(i * tm), 0, 0)),
                  pl.BlockSpec((1, D_MODEL), lambda i, k: (0, 0))],
        out_specs=pl.BlockSpec((tm, D_MODEL), lambda i, k: (i, 0)),
        out_shape=jax.ShapeDtypeStruct((m, D_MODEL), F32),
        compiler_params=_params(("parallel", "arbitrary")),
        name="ffn",
    )(h2, w1, w2, x1, mod3, post2)


def _block(x, mod3, mod_of_row, lp, init, seq_len, emit_state):
    bsz, t, _ = x.shape
    x2d = x.reshape(bsz * t, D_MODEL)
    n_chunks = t // CHUNK
    qkv, rest, gates = _proj(x2d, mod3, lambda i: mod_of_row(i * PROJ_TM), lp["pre1"], lp["w_in"], lp["conv_w"],
                             lp["gate_p"], seq_len)
    s0, c0, m0 = init if init is not None else (None, None, None)
    d_out = _delta(qkv, rest, gates, lp["norm_a"], s0, bsz, n_chunks, emit_state)
    m_out = _mlstm(rest, gates, lp["norm_b"], c0, m0, bsz, n_chunks, emit_state)
    x1, h2 = _outproj(d_out[0], m_out[0], lp["w_out"], x2d, mod3, mod_of_row, lp["post1"], lp["pre2"], 256)
    y = _ffn(h2, lp["w1"], lp["w2"], x1, mod3, mod_of_row, lp["post2"], 512, 1024)
    states = None
    if emit_state:
        states = (d_out[1], m_out[1][..., :DV_B], m_out[1][..., DV_B], m_out[2][..., 0].reshape(bsz, N_DIR, H_B))
    return y.reshape(bsz, t, D_MODEL), states


def _layer_params(l, norm_mix_pre, norm_mix_post, norm_ffn_pre, norm_ffn_post, w_in, conv_w, a_log, dt_bias,
                  norm_a, mlstm_ibias, mlstm_fbias, norm_b, w_out, w_ffn1, w_ffn2):
    w = w_in[l]
    o_ag = QKV_W
    o_aa = o_ag + A_W
    o_ab = o_aa + N_DIR * H_A
    o_bq = o_ab + N_DIR * H_A
    o_bi = o_bq + 2 * H_B * DK_B + 2 * B_W
    o_bf = o_bi + N_DIR * H_B
    n_gate = 2 * N_DIR * H_A + 2 * N_DIR * H_B
    w_all = jnp.concatenate([w[:, :o_aa], w[:, o_bq:o_bi], w[:, o_aa:o_bq], w[:, o_bi:o_bf + N_DIR * H_B],
                             jnp.zeros((D_MODEL, PROJ_TN - n_gate), F32)], axis=1).astype(BF16)

    def lane_row(vals, off):
        return jnp.zeros((LANES,), F32).at[off:off + vals.size].set(vals.reshape(-1))

    gate_p = jnp.stack([lane_row(a_log[l], G_OFF), lane_row(dt_bias[l], G_OFF),
                        lane_row(mlstm_ibias[l], LI_OFF) + lane_row(mlstm_fbias[l], LF_OFF)]
                       + [jnp.zeros((LANES,), F32)] * 5)
    row = lambda v: v[l].reshape(1, -1)
    return dict(
        pre1=row(norm_mix_pre), post1=row(norm_mix_post), pre2=row(norm_ffn_pre), post2=row(norm_ffn_post),
        w_in=w_all, gate_p=gate_p,
        conv_w=jnp.concatenate([conv_w[l].T, jnp.zeros((5, QKV_W), F32)], axis=0),
        norm_a=row(norm_a), norm_b=row(norm_b),
        w_out=w_out[l].astype(BF16), w1=w_ffn1[l].astype(BF16), w2=w_ffn2[l].astype(BF16))


def kernel(x_prompt, x_sample, state_delta, state_mlstm_C, state_mlstm_n, state_mlstm_m, c, c_ctx, w_ada, b_ada, norm_mix_pre, norm_mix_post, norm_ffn_pre, norm_ffn_post, w_in, conv_w, a_log, dt_bias, norm_a, mlstm_ibias, mlstm_fbias, norm_b, w_out, w_ffn1, w_ffn2):
    depth = w_in.shape[0]
    n_lat = x_sample.shape[0]
    t_lat = x_sample.shape[1]
    cond = jnp.concatenate([c_ctx[None, :], c, jnp.zeros((8 - 1 - n_lat, D_MODEL), F32)], axis=0)
    y_prompt, y_sample = x_prompt, x_sample
    acc = ([], [], [], [])
    for l in range(depth):
        lp = _layer_params(l, norm_mix_pre, norm_mix_post, norm_ffn_pre, norm_ffn_post, w_in, conv_w, a_log,
                           dt_bias, norm_a, mlstm_ibias, mlstm_fbias, norm_b, w_out, w_ffn1, w_ffn2)
        mod = _ada(cond, w_ada, b_ada[l].reshape(1, -1), l)
        mod3 = mod[:1 + n_lat].reshape(1 + n_lat, 6, D_MODEL)
        y_prompt, st = _block(y_prompt, mod3, lambda r: 0, lp, None, x_prompt.shape[1], True)
        for a, s in zip(acc, st):
            a.append(s)
        c_aug0 = jnp.concatenate(
            [state_mlstm_C[:, l],
             jnp.broadcast_to(state_mlstm_n[:, l][..., None], state_mlstm_n[:, l].shape + (LANES,))], axis=-1)
        m0 = jnp.broadcast_to(state_mlstm_m[:, l].reshape(n_lat, N_DIR * H_B, 1), (n_lat, N_DIR * H_B, LANES))
        y_sample, _ = _block(y_sample, mod3, lambda r: 1 + r // t_lat, lp, (state_delta[:, l], c_aug0, m0),
                             GRID_W, False)
    return (y_prompt, y_sample) + tuple(jnp.stack(a, axis=1) for a in acc)
```

```python
import functools

import jax
import jax.numpy as jnp
from jax import lax
from jax.experimental import pallas as pl
from jax.experimental.pallas import tpu as pltpu

F32 = jnp.float32
BF16 = jnp.bfloat16

D_MODEL = 2048
N_DIR = 2
A_W = D_MODEL // 2
B_W = D_MODEL - A_W
DK_A = 128
DV_A = 128
H_A = A_W // DV_A
DV_B = 256
DK_B = DV_B // 2
H_B = B_W // DV_B
GRID_W = 64
CHUNK = 64
FFN = 4 * D_MODEL
EPS = 1e-6
LANES = 128
NEG = -1e30

QKV_W = 3 * A_W
REST_W = A_W + 2 * H_B * DK_B + 2 * B_W
OFF_AG, OFF_BQ, OFF_BK, OFF_BV, OFF_BO = 0, A_W, A_W + H_B * DK_B, A_W + 2 * H_B * DK_B, A_W + 2 * H_B * DK_B + B_W
G_OFF, BETA_OFF = 0, N_DIR * H_A
LI_OFF = 2 * N_DIR * H_A
LF_OFF = LI_OFF + N_DIR * H_B

VMEM_LIMIT = 56 * 1024 * 1024


def _sigmoid(x):
    return 1.0 / (1.0 + jnp.exp(-x))


def _softplus(x):
    return jnp.maximum(x, 0.0) + jnp.log1p(jnp.exp(-jnp.abs(x)))


def _dot(a, b):
    return jnp.dot(a.astype(BF16), b.astype(BF16), preferred_element_type=F32)


def _dot_f32(a, b):
    return jnp.dot(a, b, precision=lax.Precision.HIGHEST, preferred_element_type=F32)


def _rms(x):
    return x * lax.rsqrt(jnp.mean(x * x, axis=-1, keepdims=True) + EPS)


def _params(sem):
    return pltpu.CompilerParams(dimension_semantics=sem, vmem_limit_bytes=VMEM_LIMIT)


def _ada_body(c_ref, w_ref, b_ref, o_ref):
    c = c_ref[...]
    o_ref[...] = _dot(c * _sigmoid(c), w_ref[...]) + b_ref[...]


def _ada(c_all, w_ada, b, l):
    n = w_ada.shape[2]
    tn = 1024
    return pl.pallas_call(
        _ada_body,
        grid=(n // tn,),
        in_specs=[pl.BlockSpec(c_all.shape, lambda j: (0, 0)),
                  pl.BlockSpec((None, D_MODEL, tn), lambda j: (l, 0, j)),
                  pl.BlockSpec((1, tn), lambda j: (0, j))],
        out_specs=pl.BlockSpec((c_all.shape[0], tn), lambda j: (0, j)),
        out_shape=jax.ShapeDtypeStruct((c_all.shape[0], n), F32),
        compiler_params=_params(("arbitrary",)),
        name="ada",
    )(c_all, w_ada, b)


PROJ_TM = 1024
PROJ_RB = 256
PROJ_TN = 512
N_QKV_T = QKV_W // PROJ_TN
N_REST_T = REST_W // PROJ_TN
N_HEAD_T = (QKV_W + A_W) // PROJ_TN
N_TAIL_T = N_QKV_T + N_REST_T + 1 - N_HEAD_T


def _proj_body(x_ref, mod_ref, g_ref, wh_ref, wt_ref, cw_ref, gp_ref, qkv_ref, rest_ref, gate_ref, h_scr, w_scr, *,
               seq_len):
    j = pl.program_id(1)

    @pl.when(j == 0)
    def _():
        y = _rms(x_ref[...]) * g_ref[...]
        h_scr[...] = (y * (1.0 + mod_ref[0, 1:2, :]) + mod_ref[0, 0:1, :]).astype(BF16)

    def run(head, n_cols, epilogue, out_ref):
        w_ref = wt_ref
        if head:
            w_scr[...] = wh_ref[...].astype(BF16)
            w_ref = w_scr
        for rb in range(PROJ_TM // PROJ_RB):
            rs = slice(rb * PROJ_RB, (rb + 1) * PROJ_RB)
            epilogue(jnp.dot(h_scr[rs, :], w_ref[:, :n_cols], preferred_element_type=F32), out_ref, rs)

    def conv_silu(acc):
        pos = lax.broadcasted_iota(jnp.int32, acc.shape, 0) & (seq_len - 1)
        prev = jnp.where(pos == 0, 0.0, pltpu.roll(acc, 1, 0))
        nxt = jnp.where(pos == seq_len - 1, 0.0, pltpu.roll(acc, acc.shape[0] - 1, 0))
        y = prev * cw_ref[0:1, :] + acc * cw_ref[1:2, :] + nxt * cw_ref[2:3, :]
        return y * _sigmoid(y)

    def ep_l2(scale):
        def f(acc, out_ref, rs):
            y = conv_silu(acc)
            for g in range(PROJ_TN // LANES):
                blk = y[:, g * LANES:(g + 1) * LANES]
                inv = lax.rsqrt(jnp.sum(blk * blk, axis=-1, keepdims=True) + EPS)
                out_ref[rs, g * LANES:(g + 1) * LANES] = blk * (inv * scale)
        return f

    def ep_map(fn):
        def f(acc, out_ref, rs):
            out_ref[rs, :] = fn(acc)
        return f

    def ep_gate(z, out_ref, rs):
        lane = lax.broadcasted_iota(jnp.int32, z.shape, 1)
        g = -jnp.exp(gp_ref[0:1, :]) * _softplus(z + gp_ref[1:2, :])
        li = z + gp_ref[2:3, :]
        out_ref[rs, :] = jnp.where(lane < BETA_OFF, g,
                                   jnp.where(lane < LI_OFF, _sigmoid(z),
                                             jnp.where(lane < LF_OFF, li,
                                                       jnp.where(lane < LF_OFF + N_DIR * H_B, -_softplus(-li), 0.0))))

    tp = A_W // PROJ_TN
    c0 = (j - N_QKV_T) * PROJ_TN
    in_rest = (j >= N_QKV_T) & (j < N_QKV_T + N_REST_T)
    variants = [
        (j < tp, True, PROJ_TN, ep_l2(DK_A ** -0.5), qkv_ref),
        ((j >= tp) & (j < 2 * tp), True, PROJ_TN, ep_l2(1.0), qkv_ref),
        ((j >= 2 * tp) & (j < N_QKV_T), True, PROJ_TN, ep_map(conv_silu), qkv_ref),
        (in_rest & (c0 < OFF_BQ), True, PROJ_TN, ep_map(lambda a: a * _sigmoid(a)), rest_ref),
        (in_rest & (c0 >= OFF_BQ) & (c0 < OFF_BK), False, PROJ_TN, ep_map(lambda a: a * (DK_B ** -0.5)), rest_ref),
        (in_rest & (c0 >= OFF_BK) & (c0 < OFF_BO), False, PROJ_TN, ep_map(lambda a: a), rest_ref),
        (in_rest & (c0 >= OFF_BO), False, PROJ_TN, ep_map(_sigmoid), rest_ref),
        (j == N_QKV_T + N_REST_T, False, LANES, ep_gate, gate_ref),
    ]
    for cond, head, n_cols, epilogue, out_ref in variants:
        pl.when(cond)(functools.partial(run, head, n_cols, epilogue, out_ref))


def _proj(x2d, mod3, mod_map, pre_g, w_in, l, w_tail, conv_w, gate_p, seq_len):
    m = x2d.shape[0]
    tm, tn = PROJ_TM, PROJ_TN
    assert PROJ_RB % seq_len == 0 and m % tm == 0
    return pl.pallas_call(
        functools.partial(_proj_body, seq_len=seq_len),
        grid=(m // tm, N_HEAD_T + N_TAIL_T),
        in_specs=[pl.BlockSpec((tm, D_MODEL), lambda i, j: (i, 0)),
                  pl.BlockSpec((1, 6, D_MODEL), lambda i, j: (mod_map(i), 0, 0)),
                  pl.BlockSpec((1, D_MODEL), lambda i, j: (0, 0)),
                  pl.BlockSpec((None, D_MODEL, tn), lambda i, j: (l, 0, jnp.minimum(j, N_HEAD_T - 1))),
                  pl.BlockSpec((D_MODEL, tn), lambda i, j: (0, jnp.maximum(j - N_HEAD_T, 0))),
                  pl.BlockSpec((8, tn), lambda i, j: (0, jnp.minimum(j, N_QKV_T - 1))),
                  pl.BlockSpec((8, LANES), lambda i, j: (0, 0))],
        out_specs=[pl.BlockSpec((tm, tn), lambda i, j: (i, jnp.minimum(j, N_QKV_T - 1))),
                   pl.BlockSpec((tm, tn), lambda i, j: (i, jnp.clip(j - N_QKV_T, 0, N_REST_T - 1))),
                   pl.BlockSpec((tm, LANES), lambda i, j: (i, 0))],
        out_shape=[jax.ShapeDtypeStruct((m, QKV_W), F32), jax.ShapeDtypeStruct((m, REST_W), F32),
                   jax.ShapeDtypeStruct((m, LANES), F32)],
        scratch_shapes=[pltpu.VMEM((tm, D_MODEL), BF16), pltpu.VMEM((D_MODEL, tn), BF16)],
        compiler_params=_params(("parallel", "arbitrary")),
        name="proj",
    )(x2d, mod3, pre_g, w_in, w_tail, conv_w, gate_p)


def _masks(d):
    row = lax.broadcasted_iota(jnp.int32, (CHUNK, CHUNK), 0)
    col = lax.broadcasted_iota(jnp.int32, (CHUNK, CHUNK), 1)
    if d == 0:
        return row >= col, row > col, row == col
    return row <= col, row < col, row == col


def _transpose_chunk(x):
    full = jnp.concatenate([x, jnp.zeros_like(x)], axis=0)
    return full.T[:, :CHUNK]


def _gate_cumsums(gates, incl):
    gc = _dot_f32(incl.astype(F32), gates)
    return gc, _transpose_chunk(gc), _transpose_chunk(gates)


GROUP = 4
N_GROUPS = H_A // GROUP
PAIR = LANES // CHUNK


def _split(x):
    hi = x.astype(BF16)
    return hi, (x - hi.astype(F32)).astype(BF16)


def _store_blocks(ref, x, blk_r, blk_c, col0=0):
    for h in range(GROUP):
        ref[h * blk_r:(h + 1) * blk_r, col0 + h * blk_c:col0 + (h + 1) * blk_c] = x[:, h * blk_c:(h + 1) * blk_c]


def _dot3(a_hi, a_lo, b_hi, b_lo):
    lhs = jnp.concatenate([a_hi, a_lo, a_hi], axis=1)
    rhs = jnp.concatenate([b_hi, b_hi, b_lo], axis=0)
    return jnp.dot(lhs, rhs, preferred_element_type=F32)


def _tri_inverse_wide(lmats, eye_w, bd_hi, bd_lo):
    def bd_dot(i, a_hi, a_lo, b_hi, b_lo):
        _store_blocks(bd_hi.at[i], b_hi, CHUNK, CHUNK)
        _store_blocks(bd_lo.at[i], b_lo, CHUNK, CHUNK)
        return _dot3(a_hi, a_lo, bd_hi[i], bd_lo[i])

    idx = range(len(lmats))
    s = [eye_w - l for l in lmats]
    p = []
    for i in idx:
        m_hi, m_lo = _split(-lmats[i])
        p.append(bd_dot(i, m_hi, m_lo, m_hi, m_lo))
    for _ in range(4):
        for i in idx:
            p_hi, p_lo = _split(p[i])
            s_hi, s_lo = _split(s[i])
            r = bd_dot(i, jnp.concatenate([p_hi, s_hi], axis=0), jnp.concatenate([p_lo, s_lo], axis=0), p_hi, p_lo)
            p[i] = r[:CHUNK]
            s[i] = s[i] + r[CHUNK:]
    for i in idx:
        p_hi, p_lo = _split(p[i])
        s_hi, s_lo = _split(s[i])
        s[i] = s[i] + bd_dot(i, s_hi, s_lo, p_hi, p_lo)
    return s


def _col_bcast(tile, c, width=LANES):
    return jnp.broadcast_to(tile[:, c:c + 1], (tile.shape[0], width))


def _delta_body(*refs, n_chunks, has_init, emit_state):
    qkv = refs[0:2]
    ag = refs[2:4]
    gt = refs[4:6]
    norm_ref = refs[6]
    pos = 7
    s0_ref = None
    if has_init:
        s0_ref = refs[pos]
        pos += 1
    ya_ref = refs[pos]
    pos += 1
    sout_ref = None
    if emit_state:
        sout_ref = refs[pos]
        pos += 1
    s_scr, o_scr, bdn_hi, bdn_lo, bdk, bduw_hi, bduw_lo, bds, bdv = refs[pos:pos + 9]

    n = pl.program_id(1)
    gw = GROUP * DK_A

    @pl.when(n == 0)
    def _():
        o_scr[...] = jnp.zeros_like(o_scr)
        for ref in (bdn_hi, bdn_lo, bdk, bduw_hi, bduw_lo, bds, bdv):
            ref[...] = jnp.zeros_like(ref)
        for d in range(N_DIR):
            for h in range(H_A):
                blk = s0_ref[0, d, h] if has_init else jnp.zeros((DK_A, DV_A), F32)
                s_scr[d, h // GROUP, :, (h % GROUP) * DV_A:(h % GROUP + 1) * DV_A] = blk

    row = lax.broadcasted_iota(jnp.int32, (CHUNK, LANES), 0)
    lane = lax.broadcasted_iota(jnp.int32, (CHUNK, LANES), 1)
    col = lane & (CHUNK - 1)
    left = lane < CHUNK
    eye_w = jnp.concatenate([(row == col).astype(F32)] * (GROUP // PAIR), axis=1)
    probs = [(d, g) for d in range(N_DIR) for g in range(N_GROUPS)]
    gates, gc, gc_t, eg, egl, masks = {}, {}, {}, {}, {}, {}
    for d in range(N_DIR):
        masks[d] = (row >= col, row > col) if d == 0 else (row <= col, row < col)
        gates[d] = gt[d][...]
        gc[d] = _dot_f32(_masks(d)[0].astype(F32), gates[d])
        gc_t[d] = jnp.concatenate([gc[d], gc[d]], axis=0).T
        last = CHUNK - 1 if d == 0 else 0
        eg[d] = jnp.exp(gc[d])
        egl[d] = jnp.exp(gc[d][last:last + 1, :] - gc[d])

    def cols_of(d, g):
        return [G_OFF + d * H_A + g * GROUP + hl for hl in range(GROUP)]

    q16, ks, beta_xs, decays, grams = [], [], [], [], []
    for gi, (d, g) in enumerate(probs):
        cols = cols_of(d, g)
        incl, _ = masks[d]
        k = qkv[d][:, A_W + g * gw:A_W + (g + 1) * gw]
        beta_x = jnp.concatenate([_col_bcast(gates[d], BETA_OFF - G_OFF + c) for c in cols], axis=1)
        decay = []
        for p in range(GROUP // PAIR):
            c0, c1 = cols[PAIR * p], cols[PAIR * p + 1]
            gcol = jnp.where(left, _col_bcast(gc[d], c0), _col_bcast(gc[d], c1))
            grow = jnp.where(left[0:1], gc_t[d][c0:c0 + 1, :], gc_t[d][c1:c1 + 1, :])
            decay.append(jnp.exp(jnp.where(incl, gcol - grow, NEG)))
        decays.append(jnp.concatenate(decay, axis=1))
        q16.append(qkv[d][:, g * gw:(g + 1) * gw].astype(BF16))
        ks.append(k)
        beta_xs.append(beta_x)
        _store_blocks(bdk.at[gi], k.astype(BF16), CHUNK, DK_A)
    for gi, (d, g) in enumerate(probs):
        grams.append(lax.dot_general(jnp.concatenate([q16[gi], (ks[gi] * beta_xs[gi]).astype(BF16)], axis=0),
                                     bdk[gi], (((1,), (1,)), ((), ())), preferred_element_type=F32))
    attns, lmats = [], []
    for gi, (d, g) in enumerate(probs):
        strict_w = jnp.concatenate([masks[d][1]] * (GROUP // PAIR), axis=1)
        attns.append((grams[gi][:CHUNK] * decays[gi]).astype(BF16))
        lmats.append(jnp.where(strict_w, grams[gi][CHUNK:] * decays[gi], 0.0))
    ainvs = _tri_inverse_wide(lmats, eye_w, bdn_hi, bdn_lo)
    eg_xs, kd_ts = [], []
    for gi, (d, g) in enumerate(probs):
        cols = cols_of(d, g)
        v = qkv[d][:, 2 * A_W + g * gw:2 * A_W + (g + 1) * gw]
        eg_x = jnp.concatenate([_col_bcast(eg[d], c) for c in cols], axis=1)
        egl_x = jnp.concatenate([_col_bcast(egl[d], c) for c in cols], axis=1)
        eg_xs.append(eg_x)
        bv_hi, bv_lo = _split(v * beta_xs[gi])
        bk_hi, bk_lo = _split(ks[gi] * (beta_xs[gi] * eg_x))
        _store_blocks(bduw_hi.at[gi], bv_hi, CHUNK, DV_A)
        _store_blocks(bduw_lo.at[gi], bv_lo, CHUNK, DV_A)
        _store_blocks(bduw_hi.at[gi], bk_hi, CHUNK, DK_A, col0=gw)
        _store_blocks(bduw_lo.at[gi], bk_lo, CHUNK, DK_A, col0=gw)
        kd = ks[gi] * egl_x
        kd_ts.append(jnp.concatenate([kd[:, hl * DK_A:(hl + 1) * DK_A] for hl in range(GROUP)], axis=0)
                     .T.astype(BF16))
        _store_blocks(bds.at[gi], s_scr[d, g].astype(BF16), DK_A, DV_A)
    uws = []
    for gi in range(len(probs)):
        t_hi, t_lo = _split(ainvs[gi])
        uws.append(_dot3(t_hi, t_lo, bduw_hi[gi], bduw_lo[gi]))
    ws_qs = []
    for gi in range(len(probs)):
        ws_qs.append(jnp.dot(jnp.concatenate([uws[gi][:, gw:].astype(BF16), q16[gi]], axis=0), bds[gi],
                             preferred_element_type=F32))
    for gi in range(len(probs)):
        _store_blocks(bdv.at[gi], (uws[gi][:, :gw] - ws_qs[gi][:CHUNK]).astype(BF16), CHUNK, DV_A)
    rs = []
    for gi in range(len(probs)):
        rs.append(jnp.dot(jnp.concatenate([attns[gi], kd_ts[gi]], axis=0), bdv[gi], preferred_element_type=F32))
    for gi, (d, g) in enumerate(probs):
        cols = cols_of(d, g)
        last = CHUNK - 1 if d == 0 else 0
        o = ws_qs[gi][CHUNK:] * eg_xs[gi] + rs[gi][:CHUNK]
        eg_last = jnp.concatenate(
            [jnp.broadcast_to(eg[d][last:last + 1, c:c + 1], (1, DV_A)) for c in cols], axis=1)
        s_scr[d, g] = s_scr[d, g] * eg_last + rs[gi][CHUNK:]
        cidx = n if d == 0 else n_chunks - 1 - n
        rows = pl.ds(pl.multiple_of(cidx * CHUNK, CHUNK), CHUNK)
        tot = o + o_scr[rows, g * gw:(g + 1) * gw]
        o_scr[rows, g * gw:(g + 1) * gw] = tot
        for hl in range(GROUP):
            hs = slice((g * GROUP + hl) * DV_A, (g * GROUP + hl + 1) * DV_A)
            blk = tot[:, hl * DV_A:(hl + 1) * DV_A]
            ya_ref[rows, hs] = (_rms(blk) * norm_ref[...] * ag[d][:, hs]).astype(BF16)

    if emit_state:
        @pl.when(n == n_chunks - 1)
        def _():
            for d in range(N_DIR):
                for h in range(H_A):
                    sout_ref[0, d, h] = s_scr[d, h // GROUP, :, (h % GROUP) * DV_A:(h % GROUP + 1) * DV_A]


def _delta(qkv, rest, gates, norm_a, s0, batch, n_chunks, emit_state):
    t = n_chunks * CHUNK
    has_init = s0 is not None
    ng = N_DIR * N_GROUPS

    def fwd(b, n):
        return (b * n_chunks + n, 0)

    def bwd(b, n):
        return (b * n_chunks + n_chunks - 1 - n, 0)

    in_specs = [pl.BlockSpec((CHUNK, QKV_W), fwd), pl.BlockSpec((CHUNK, QKV_W), bwd),
                pl.BlockSpec((CHUNK, A_W), fwd), pl.BlockSpec((CHUNK, A_W), bwd),
                pl.BlockSpec((CHUNK, LANES), fwd), pl.BlockSpec((CHUNK, LANES), bwd),
                pl.BlockSpec((1, DV_A), lambda b, n: (0, 0))]
    args = [qkv, qkv, rest, rest, gates, gates, norm_a]
    if has_init:
        in_specs.append(pl.BlockSpec((1, N_DIR, H_A, DK_A, DV_A), lambda b, n: (b, 0, 0, 0, 0)))
        args.append(s0)
    out_specs = [pl.BlockSpec((t, A_W), lambda b, n: (b, 0))]
    out_shape = [jax.ShapeDtypeStruct((batch * t, A_W), BF16)]
    if emit_state:
        out_specs.append(pl.BlockSpec((1, N_DIR, H_A, DK_A, DV_A), lambda b, n: (b, 0, 0, 0, 0)))
        out_shape.append(jax.ShapeDtypeStruct((batch, N_DIR, H_A, DK_A, DV_A), F32))
    return pl.pallas_call(
        functools.partial(_delta_body, n_chunks=n_chunks, has_init=has_init, emit_state=emit_state),
        grid=(batch, n_chunks),
        in_specs=in_specs,
        out_specs=out_specs,
        out_shape=out_shape,
        scratch_shapes=[pltpu.VMEM((N_DIR, N_GROUPS, DK_A, GROUP * DV_A), F32), pltpu.VMEM((t, A_W), F32),
                        pltpu.VMEM((ng, GROUP * CHUNK, GROUP * CHUNK), BF16),
                        pltpu.VMEM((ng, GROUP * CHUNK, GROUP * CHUNK), BF16),
                        pltpu.VMEM((ng, GROUP * CHUNK, GROUP * DK_A), BF16),
                        pltpu.VMEM((ng, GROUP * CHUNK, 2 * GROUP * DK_A), BF16),
                        pltpu.VMEM((ng, GROUP * CHUNK, 2 * GROUP * DK_A), BF16),
                        pltpu.VMEM((ng, GROUP * DK_A, GROUP * DV_A), BF16),
                        pltpu.VMEM((ng, GROUP * CHUNK, GROUP * DV_A), BF16)],
        compiler_params=_params(("parallel", "arbitrary")),
        name="delta_scan",
    )(*args)


CAUG_W = DV_B + LANES


def _mlstm_body(*refs, n_chunks, has_init, emit_state):
    rest = refs[0:2]
    gt = refs[2:4]
    norm_ref = refs[4]
    pos = 5
    c0_ref = m0_ref = None
    if has_init:
        c0_ref, m0_ref = refs[pos], refs[pos + 1]
        pos += 2
    yb_ref = refs[pos]
    pos += 1
    cout_ref = mout_ref = None
    if emit_state:
        cout_ref, mout_ref = refs[pos], refs[pos + 1]
        pos += 2
    c_scr, m_scr, o_scr = refs[pos], refs[pos + 1], refs[pos + 2]

    n = pl.program_id(1)

    @pl.when(n == 0)
    def _():
        o_scr[...] = jnp.zeros_like(o_scr)
        if has_init:
            c_scr[...] = c0_ref[0]
            m_scr[...] = m0_ref[0]
        else:
            c_scr[...] = jnp.zeros_like(c_scr)
            m_scr[...] = jnp.zeros_like(m_scr)

    ones = jnp.ones((CHUNK, LANES), BF16)
    probs = [(d, h) for d in range(N_DIR) for h in range(H_B)]
    idx = range(len(probs))
    gcs = {}
    for d in range(N_DIR):
        incl, _, _ = _masks(d)
        gates = gt[d][...]
        gcs[d] = (incl,) + _gate_cumsums(gates, incl)
    q16, k_t, v_aug, qk = [], [], [], []
    for d, h in probs:
        q16.append(rest[d][:, OFF_BQ + h * DK_B:OFF_BQ + (h + 1) * DK_B].astype(BF16))
        k_t.append(_transpose_chunk(rest[d][:, OFF_BK + h * DK_B:OFF_BK + (h + 1) * DK_B]))
        v = rest[d][:, OFF_BV + h * DV_B:OFF_BV + (h + 1) * DV_B]
        v_aug.append(jnp.concatenate([v.astype(BF16), ones], axis=1))
    for i in idx:
        qk.append(_dot(q16[i], k_t[i]))
    qc = [_dot(q16[i], c_scr[d, h]) for i, (d, h) in enumerate(probs)]
    dw, iw, m_t, ks_t, dec = [], [], [], [], []
    for i, (d, h) in enumerate(probs):
        incl, gc, gc_t, gates_t = gcs[d]
        c = d * H_B + h
        last = CHUNK - 1 if d == 0 else 0
        bcol = gc[:, LF_OFF + c:LF_OFF + c + 1]
        brow = gc_t[LF_OFF + c:LF_OFF + c + 1, :]
        irow = gates_t[LI_OFF + c:LI_OFF + c + 1, :]
        blast = gc[last:last + 1, LF_OFF + c:LF_OFF + c + 1]
        m_old = m_scr[c:c + 1, 0:1]
        log_d = jnp.where(incl, bcol - brow + irow, NEG)
        inter = bcol + m_old
        m_t.append(jnp.maximum(inter, jnp.max(log_d, axis=-1, keepdims=True)))
        dw.append(jnp.exp(log_d - m_t[i]) * qk[i])
        iw.append(jnp.exp(inter - m_t[i]))
        le = blast - brow + irow
        b_last = blast + m_old
        m_new = jnp.maximum(b_last, jnp.max(le, axis=-1, keepdims=True))
        ks_t.append(k_t[i] * jnp.exp(le - m_new))
        dec.append(jnp.exp(b_last - m_new))
        m_scr[c:c + 1, :] = jnp.broadcast_to(m_new, (1, LANES))
    intra = [_dot(dw[i], v_aug[i]) for i in idx]
    upd = [_dot(ks_t[i], v_aug[i]) for i in idx]
    for i, (d, h) in enumerate(probs):
        vs = slice(h * DV_B, (h + 1) * DV_B)
        num_aug = iw[i] * qc[i] + intra[i]
        den = jnp.maximum(jnp.abs(num_aug[:, DV_B:]), jnp.exp(-m_t[i]))
        hb = jnp.concatenate([num_aug[:, :LANES] / den, num_aug[:, LANES:DV_B] / den], axis=1)
        c_scr[d, h] = dec[i] * c_scr[d, h] + upd[i]
        cidx = n if d == 0 else n_chunks - 1 - n
        rows = pl.ds(pl.multiple_of(cidx * CHUNK, CHUNK), CHUNK)
        tot = hb + o_scr[rows, vs]
        o_scr[rows, vs] = tot
        ogate = rest[d][:, OFF_BO + h * DV_B:OFF_BO + (h + 1) * DV_B]
        yb_ref[rows, vs] = (_rms(tot) * norm_ref[...] * ogate).astype(BF16)

    if emit_state:
        @pl.when(n == n_chunks - 1)
        def _():
            cout_ref[0] = c_scr[...]
            mout_ref[0] = m_scr[...]


def _mlstm(rest, gates, norm_b, c0, m0, batch, n_chunks, emit_state):
    t = n_chunks * CHUNK
    has_init = c0 is not None

    def fwd(b, n):
        return (b * n_chunks + n, 0)

    def bwd(b, n):
        return (b * n_chunks + n_chunks - 1 - n, 0)

    cspec = pl.BlockSpec((1, N_DIR, H_B, DK_B, CAUG_W), lambda b, n: (b, 0, 0, 0, 0))
    mspec = pl.BlockSpec((1, N_DIR * H_B, LANES), lambda b, n: (b, 0, 0))
    in_specs = [pl.BlockSpec((CHUNK, REST_W), fwd), pl.BlockSpec((CHUNK, REST_W), bwd),
                pl.BlockSpec((CHUNK, LANES), fwd), pl.BlockSpec((CHUNK, LANES), bwd),
                pl.BlockSpec((1, DV_B), lambda b, n: (0, 0))]
    args = [rest, rest, gates, gates, norm_b]
    if has_init:
        in_specs += [cspec, mspec]
        args += [c0, m0]
    out_specs = [pl.BlockSpec((t, B_W), lambda b, n: (b, 0))]
    out_shape = [jax.ShapeDtypeStruct((batch * t, B_W), BF16)]
    if emit_state:
        out_specs += [cspec, mspec]
        out_shape += [jax.ShapeDtypeStruct((batch, N_DIR, H_B, DK_B, CAUG_W), F32),
                      jax.ShapeDtypeStruct((batch, N_DIR * H_B, LANES), F32)]
    return pl.pallas_call(
        functools.partial(_mlstm_body, n_chunks=n_chunks, has_init=has_init, emit_state=emit_state),
        grid=(batch, n_chunks),
        in_specs=in_specs,
        out_specs=out_specs,
        out_shape=out_shape,
        scratch_shapes=[pltpu.VMEM((N_DIR, H_B, DK_B, CAUG_W), F32),
                        pltpu.VMEM((N_DIR * H_B, LANES), F32),
                        pltpu.VMEM((t, B_W), F32)],
        compiler_params=_params(("parallel", "arbitrary")),
        name="mlstm_scan",
    )(*args)


def _outproj_body(ya_ref, yb_ref, wa_ref, wb_ref, x_ref, mod_ref, post1_ref, pre2_ref, x1_ref, h2_ref):
    mix = (jnp.dot(ya_ref[...], wa_ref[...], preferred_element_type=F32)
           + jnp.dot(yb_ref[...], wb_ref[...], preferred_element_type=F32))
    x1 = x_ref[...] + mod_ref[0, 2:3, :] * (_rms(mix) * post1_ref[...])
    x1_ref[...] = x1
    h2 = _rms(x1) * pre2_ref[...] * (1.0 + mod_ref[0, 4:5, :]) + mod_ref[0, 3:4, :]
    h2_ref[...] = h2.astype(BF16)


def _outproj(ya, yb, w_out, x2d, mod3, mod_map, post1, pre2, tm):
    m = x2d.shape[0]
    row = lambda i: (i, 0)
    const = lambda i: (0, 0)
    return pl.pallas_call(
        _outproj_body,
        grid=(m // tm,),
        in_specs=[pl.BlockSpec((tm, A_W), row), pl.BlockSpec((tm, B_W), row),
                  pl.BlockSpec((A_W, D_MODEL), lambda i: (0, 0)),
                  pl.BlockSpec((B_W, D_MODEL), lambda i: (1, 0)),
                  pl.BlockSpec((tm, D_MODEL), row),
                  pl.BlockSpec((1, 6, D_MODEL), lambda i: (mod_map(i * tm), 0, 0)),
                  pl.BlockSpec((1, D_MODEL), const), pl.BlockSpec((1, D_MODEL), const)],
        out_specs=[pl.BlockSpec((tm, D_MODEL), row), pl.BlockSpec((tm, D_MODEL), row)],
        out_shape=[jax.ShapeDtypeStruct((m, D_MODEL), F32), jax.ShapeDtypeStruct((m, D_MODEL), BF16)],
        compiler_params=_params(("parallel",)),
        name="outproj",
    )(ya, yb, w_out, w_out, x2d, mod3, post1, pre2)


def _ffn_body(h2_ref, w1_ref, w2_ref, x1_ref, mod_ref, post2_ref, o_ref):
    kk = pl.program_id(1)
    a = jnp.maximum(jnp.dot(h2_ref[...], w1_ref[...], preferred_element_type=F32), 0.0)
    contrib = jnp.dot((a * a).astype(BF16), w2_ref[...], preferred_element_type=F32)

    @pl.when(kk == 0)
    def _():
        o_ref[...] = contrib

    @pl.when((kk > 0) & (kk < pl.num_programs(1) - 1))
    def _():
        o_ref[...] += contrib

    @pl.when(kk == pl.num_programs(1) - 1)
    def _():
        f = o_ref[...] + contrib
        o_ref[...] = x1_ref[...] + mod_ref[0, 5:6, :] * (_rms(f) * post2_ref[...])


def _ffn(h2, w1, w2, x1, mod3, mod_map, post2, tm, fc):
    m = h2.shape[0]
    return pl.pallas_call(
        _ffn_body,
        grid=(m // tm, FFN // fc),
        in_specs=[pl.BlockSpec((tm, D_MODEL), lambda i, k: (i, 0)),
                  pl.BlockSpec((D_MODEL, fc), lambda i, k: (0, k)),
                  pl.BlockSpec((fc, D_MODEL), lambda i, k: (k, 0)),
                  pl.BlockSpec((tm, D_MODEL), lambda i, k: (i, 0)),
                  pl.BlockSpec((1, 6, D_MODEL), lambda i, k: (mod_map(i * tm), 0, 0)),
                  pl.BlockSpec((1, D_MODEL), lambda i, k: (0, 0))],
        out_specs=pl.BlockSpec((tm, D_MODEL), lambda i, k: (i, 0)),
        out_shape=jax.ShapeDtypeStruct((m, D_MODEL), F32),
        compiler_params=_params(("parallel", "arbitrary")),
        name="ffn",
    )(h2, w1, w2, x1, mod3, post2)


def _block(x, mod3, mod_of_row, lp, init, seq_len, emit_state):
    bsz, t, _ = x.shape
    x2d = x.reshape(bsz * t, D_MODEL)
    n_chunks = t // CHUNK
    qkv, rest, gates = _proj(x2d, mod3, lambda i: mod_of_row(i * PROJ_TM), lp["pre1"], lp["w_in"], lp["l"],
                             lp["w_tail"], lp["conv_w"], lp["gate_p"], seq_len)
    s0, c0, m0 = init if init is not None else (None, None, None)
    d_out = _delta(qkv, rest, gates, lp["norm_a"], s0, bsz, n_chunks, emit_state)
    m_out = _mlstm(rest, gates, lp["norm_b"], c0, m0, bsz, n_chunks, emit_state)
    x1, h2 = _outproj(d_out[0], m_out[0], lp["w_out"], x2d, mod3, mod_of_row, lp["post1"], lp["pre2"], 256)
    y = _ffn(h2, lp["w1"], lp["w2"], x1, mod3, mod_of_row, lp["post2"], 512, 1024)
    states = None
    if emit_state:
        states = (d_out[1], m_out[1][..., :DV_B], m_out[1][..., DV_B], m_out[2][..., 0].reshape(bsz, N_DIR, H_B))
    return y.reshape(bsz, t, D_MODEL), states


def _layer_params(l, norm_mix_pre, norm_mix_post, norm_ffn_pre, norm_ffn_post, w_in, conv_w, a_log, dt_bias,
                  norm_a, mlstm_ibias, mlstm_fbias, norm_b, w_out, w_ffn1, w_ffn2):
    w = w_in[l]
    o_ag = QKV_W
    o_aa = o_ag + A_W
    o_ab = o_aa + N_DIR * H_A
    o_bq = o_ab + N_DIR * H_A
    o_bi = o_bq + 2 * H_B * DK_B + 2 * B_W
    o_bf = o_bi + N_DIR * H_B
    n_gate = 2 * N_DIR * H_A + 2 * N_DIR * H_B
    w_tail = jnp.concatenate([w[:, o_bq:o_bi], w[:, o_aa:o_bq], w[:, o_bi:o_bf + N_DIR * H_B]], axis=1).astype(BF16)
    w_tail = jnp.pad(w_tail, ((0, 0), (0, N_TAIL_T * PROJ_TN - w_tail.shape[1])))

    def lane_row(vals, off):
        return jnp.zeros((LANES,), F32).at[off:off + vals.size].set(vals.reshape(-1))

    gate_p = jnp.stack([lane_row(a_log[l], G_OFF), lane_row(dt_bias[l], G_OFF),
                        lane_row(mlstm_ibias[l], LI_OFF) + lane_row(mlstm_fbias[l], LF_OFF)]
                       + [jnp.zeros((LANES,), F32)] * 5)
    row = lambda v: v[l].reshape(1, -1)
    return dict(
        pre1=row(norm_mix_pre), post1=row(norm_mix_post), pre2=row(norm_ffn_pre), post2=row(norm_ffn_post),
        w_in=w_in, l=l, w_tail=w_tail, gate_p=gate_p,
        conv_w=jnp.concatenate([conv_w[l].T, jnp.zeros((5, QKV_W), F32)], axis=0),
        norm_a=row(norm_a), norm_b=row(norm_b),
        w_out=w_out[l].astype(BF16), w1=w_ffn1[l].astype(BF16), w2=w_ffn2[l].astype(BF16))


def kernel(x_prompt, x_sample, state_delta, state_mlstm_C, state_mlstm_n, state_mlstm_m, c, c_ctx, w_ada, b_ada, norm_mix_pre, norm_mix_post, norm_ffn_pre, norm_ffn_post, w_in, conv_w, a_log, dt_bias, norm_a, mlstm_ibias, mlstm_fbias, norm_b, w_out, w_ffn1, w_ffn2):
    depth = w_in.shape[0]
    n_lat = x_sample.shape[0]
    t_lat = x_sample.shape[1]
    cond = jnp.concatenate([c_ctx[None, :], c, jnp.zeros((8 - 1 - n_lat, D_MODEL), F32)], axis=0)
    y_prompt, y_sample = x_prompt, x_sample
    acc = ([], [], [], [])
    for l in range(depth):
        lp = _layer_params(l, norm_mix_pre, norm_mix_post, norm_ffn_pre, norm_ffn_post, w_in, conv_w, a_log,
                           dt_bias, norm_a, mlstm_ibias, mlstm_fbias, norm_b, w_out, w_ffn1, w_ffn2)
        mod = _ada(cond, w_ada, b_ada[l].reshape(1, -1), l)
        mod3 = mod[:1 + n_lat].reshape(1 + n_lat, 6, D_MODEL)
        y_prompt, st = _block(y_prompt, mod3, lambda r: 0, lp, None, x_prompt.shape[1], True)
        for a, s in zip(acc, st):
            a.append(s)
        c_aug0 = jnp.concatenate(
            [state_mlstm_C[:, l],
             jnp.broadcast_to(state_mlstm_n[:, l][..., None], state_mlstm_n[:, l].shape + (LANES,))], axis=-1)
        m0 = jnp.broadcast_to(state_mlstm_m[:, l].reshape(n_lat, N_DIR * H_B, 1), (n_lat, N_DIR * H_B, LANES))
        y_sample, _ = _block(y_sample, mod3, lambda r: 1 + r // t_lat, lp, (state_delta[:, l], c_aug0, m0),
                             GRID_W, False)
    return (y_prompt, y_sample) + tuple(jnp.stack(a, axis=1) for a in acc)
```

```python
import functools

import jax
import jax.numpy as jnp
from jax import lax
from jax.experimental import pallas as pl
from jax.experimental.pallas import tpu as pltpu

F32 = jnp.float32
BF16 = jnp.bfloat16

D_MODEL = 2048
N_DIR = 2
A_W = D_MODEL // 2
B_W = D_MODEL - A_W
DK_A = 128
DV_A = 128
H_A = A_W // DV_A
DV_B = 256
DK_B = DV_B // 2
H_B = B_W // DV_B
GRID_W = 64
CHUNK = 64
FFN = 4 * D_MODEL
EPS = 1e-6
LANES = 128
NEG = -1e30

QKV_W = 3 * A_W
REST_W = A_W + 2 * H_B * DK_B + 2 * B_W
OFF_AG, OFF_BQ, OFF_BK, OFF_BV, OFF_BO = 0, A_W, A_W + H_B * DK_B, A_W + 2 * H_B * DK_B, A_W + 2 * H_B * DK_B + B_W
G_OFF, BETA_OFF = 0, N_DIR * H_A
LI_OFF = 2 * N_DIR * H_A
LF_OFF = LI_OFF + N_DIR * H_B

VMEM_LIMIT = 56 * 1024 * 1024


def _sigmoid(x):
    return 1.0 / (1.0 + jnp.exp(-x))


def _softplus(x):
    return jnp.maximum(x, 0.0) + jnp.log1p(jnp.exp(-jnp.abs(x)))


def _dot(a, b):
    return jnp.dot(a.astype(BF16), b.astype(BF16), preferred_element_type=F32)


def _dot_f32(a, b):
    return jnp.dot(a, b, precision=lax.Precision.HIGHEST, preferred_element_type=F32)


def _rms(x):
    return x * lax.rsqrt(jnp.mean(x * x, axis=-1, keepdims=True) + EPS)


def _params(sem):
    return pltpu.CompilerParams(dimension_semantics=sem, vmem_limit_bytes=VMEM_LIMIT)


def _ada_body(c_ref, w_ref, b_ref, o_ref):
    c = c_ref[...]
    o_ref[...] = _dot(c * _sigmoid(c), w_ref[...]) + b_ref[...]


def _ada(c_all, w_ada, b, l):
    n = w_ada.shape[2]
    tn = 1024
    return pl.pallas_call(
        _ada_body,
        grid=(n // tn,),
        in_specs=[pl.BlockSpec(c_all.shape, lambda j: (0, 0)),
                  pl.BlockSpec((None, D_MODEL, tn), lambda j: (l, 0, j)),
                  pl.BlockSpec((1, tn), lambda j: (0, j))],
        out_specs=pl.BlockSpec((c_all.shape[0], tn), lambda j: (0, j)),
        out_shape=jax.ShapeDtypeStruct((c_all.shape[0], n), F32),
        compiler_params=_params(("arbitrary",)),
        name="ada",
    )(c_all, w_ada, b)


PROJ_TM = 1024
PROJ_RB = 256
PROJ_TN = 512
N_QKV_T = QKV_W // PROJ_TN
N_REST_T = REST_W // PROJ_TN
N_HEAD_T = (QKV_W + A_W) // PROJ_TN
N_TAIL_T = N_QKV_T + N_REST_T + 1 - N_HEAD_T


def _proj_body(x_ref, mod_ref, g_ref, wh_ref, wt_ref, cw_ref, gp_ref, qkv_ref, rest_ref, gate_ref, h_scr, *, seq_len):
    j = pl.program_id(1)

    @pl.when(j == 0)
    def _():
        y = _rms(x_ref[...]) * g_ref[...]
        h_scr[...] = (y * (1.0 + mod_ref[0, 1:2, :]) + mod_ref[0, 0:1, :]).astype(BF16)

    def run(head, n_cols, epilogue, out_ref):
        w_ref = wh_ref if head else wt_ref
        for rb in range(PROJ_TM // PROJ_RB):
            rs = slice(rb * PROJ_RB, (rb + 1) * PROJ_RB)
            epilogue(jnp.dot(h_scr[rs, :], w_ref[:, :n_cols], preferred_element_type=F32), out_ref, rs)

    def conv_silu(acc):
        pos = lax.broadcasted_iota(jnp.int32, acc.shape, 0) & (seq_len - 1)
        prev = jnp.where(pos == 0, 0.0, pltpu.roll(acc, 1, 0))
        nxt = jnp.where(pos == seq_len - 1, 0.0, pltpu.roll(acc, acc.shape[0] - 1, 0))
        y = prev * cw_ref[0:1, :] + acc * cw_ref[1:2, :] + nxt * cw_ref[2:3, :]
        return y * _sigmoid(y)

    def ep_l2(scale):
        def f(acc, out_ref, rs):
            y = conv_silu(acc)
            for g in range(PROJ_TN // LANES):
                blk = y[:, g * LANES:(g + 1) * LANES]
                inv = lax.rsqrt(jnp.sum(blk * blk, axis=-1, keepdims=True) + EPS)
                out_ref[rs, g * LANES:(g + 1) * LANES] = blk * (inv * scale)
        return f

    def ep_map(fn):
        def f(acc, out_ref, rs):
            out_ref[rs, :] = fn(acc)
        return f

    def ep_gate(z, out_ref, rs):
        lane = lax.broadcasted_iota(jnp.int32, z.shape, 1)
        g = -jnp.exp(gp_ref[0:1, :]) * _softplus(z + gp_ref[1:2, :])
        li = z + gp_ref[2:3, :]
        out_ref[rs, :] = jnp.where(lane < BETA_OFF, g,
                                   jnp.where(lane < LI_OFF, _sigmoid(z),
                                             jnp.where(lane < LF_OFF, li,
                                                       jnp.where(lane < LF_OFF + N_DIR * H_B, -_softplus(-li), 0.0))))

    tp = A_W // PROJ_TN
    c0 = (j - N_QKV_T) * PROJ_TN
    in_rest = (j >= N_QKV_T) & (j < N_QKV_T + N_REST_T)
    variants = [
        (j < tp, True, PROJ_TN, ep_l2(DK_A ** -0.5), qkv_ref),
        ((j >= tp) & (j < 2 * tp), True, PROJ_TN, ep_l2(1.0), qkv_ref),
        ((j >= 2 * tp) & (j < N_QKV_T), True, PROJ_TN, ep_map(conv_silu), qkv_ref),
        (in_rest & (c0 < OFF_BQ), True, PROJ_TN, ep_map(lambda a: a * _sigmoid(a)), rest_ref),
        (in_rest & (c0 >= OFF_BQ) & (c0 < OFF_BK), False, PROJ_TN, ep_map(lambda a: a * (DK_B ** -0.5)), rest_ref),
        (in_rest & (c0 >= OFF_BK) & (c0 < OFF_BO), False, PROJ_TN, ep_map(lambda a: a), rest_ref),
        (in_rest & (c0 >= OFF_BO), False, PROJ_TN, ep_map(_sigmoid), rest_ref),
        (j == N_QKV_T + N_REST_T, False, LANES, ep_gate, gate_ref),
    ]
    for cond, head, n_cols, epilogue, out_ref in variants:
        pl.when(cond)(functools.partial(run, head, n_cols, epilogue, out_ref))


def _proj(x2d, mod3, mod_map, pre_g, w_head, w_tail, conv_w, gate_p, seq_len):
    m = x2d.shape[0]
    tm, tn = PROJ_TM, PROJ_TN
    assert PROJ_RB % seq_len == 0 and m % tm == 0
    return pl.pallas_call(
        functools.partial(_proj_body, seq_len=seq_len),
        grid=(m // tm, N_HEAD_T + N_TAIL_T),
        in_specs=[pl.BlockSpec((tm, D_MODEL), lambda i, j: (i, 0)),
                  pl.BlockSpec((1, 6, D_MODEL), lambda i, j: (mod_map(i), 0, 0)),
                  pl.BlockSpec((1, D_MODEL), lambda i, j: (0, 0)),
                  pl.BlockSpec((D_MODEL, tn), lambda i, j: (0, jnp.minimum(j, N_HEAD_T - 1))),
                  pl.BlockSpec((D_MODEL, tn), lambda i, j: (0, jnp.maximum(j - N_HEAD_T, 0))),
                  pl.BlockSpec((8, tn), lambda i, j: (0, jnp.minimum(j, N_QKV_T - 1))),
                  pl.BlockSpec((8, LANES), lambda i, j: (0, 0))],
        out_specs=[pl.BlockSpec((tm, tn), lambda i, j: (i, jnp.minimum(j, N_QKV_T - 1))),
                   pl.BlockSpec((tm, tn), lambda i, j: (i, jnp.clip(j - N_QKV_T, 0, N_REST_T - 1))),
                   pl.BlockSpec((tm, LANES), lambda i, j: (i, 0))],
        out_shape=[jax.ShapeDtypeStruct((m, QKV_W), F32), jax.ShapeDtypeStruct((m, REST_W), F32),
                   jax.ShapeDtypeStruct((m, LANES), F32)],
        scratch_shapes=[pltpu.VMEM((tm, D_MODEL), BF16)],
        compiler_params=_params(("parallel", "arbitrary")),
        name="proj",
    )(x2d, mod3, pre_g, w_head, w_tail, conv_w, gate_p)


def _masks(d):
    row = lax.broadcasted_iota(jnp.int32, (CHUNK, CHUNK), 0)
    col = lax.broadcasted_iota(jnp.int32, (CHUNK, CHUNK), 1)
    if d == 0:
        return row >= col, row > col, row == col
    return row <= col, row < col, row == col


GROUP = 4
N_GROUPS = H_A // GROUP
PAIR = LANES // CHUNK


def _split(x):
    hi = x.astype(BF16)
    return hi, (x - hi.astype(F32)).astype(BF16)


def _store_blocks(ref, x, blk_r, blk_c, col0=0):
    for h in range(GROUP):
        ref[h * blk_r:(h + 1) * blk_r, col0 + h * blk_c:col0 + (h + 1) * blk_c] = x[:, h * blk_c:(h + 1) * blk_c]


def _dot3(a_hi, a_lo, b_hi, b_lo):
    lhs = jnp.concatenate([a_hi, a_lo, a_hi], axis=1)
    rhs = jnp.concatenate([b_hi, b_hi, b_lo], axis=0)
    return jnp.dot(lhs, rhs, preferred_element_type=F32)


def _tri_inverse_wide(lmats, eye_w, bd_hi, bd_lo):
    def bd_dot(i, a_hi, a_lo, b_hi, b_lo):
        _store_blocks(bd_hi.at[i], b_hi, CHUNK, CHUNK)
        _store_blocks(bd_lo.at[i], b_lo, CHUNK, CHUNK)
        return _dot3(a_hi, a_lo, bd_hi[i], bd_lo[i])

    idx = range(len(lmats))
    s = [eye_w - l for l in lmats]
    p = []
    for i in idx:
        m_hi, m_lo = _split(-lmats[i])
        p.append(bd_dot(i, m_hi, m_lo, m_hi, m_lo))
    for _ in range(4):
        for i in idx:
            p_hi, p_lo = _split(p[i])
            s_hi, s_lo = _split(s[i])
            r = bd_dot(i, jnp.concatenate([p_hi, s_hi], axis=0), jnp.concatenate([p_lo, s_lo], axis=0), p_hi, p_lo)
            p[i] = r[:CHUNK]
            s[i] = s[i] + r[CHUNK:]
    for i in idx:
        p_hi, p_lo = _split(p[i])
        s_hi, s_lo = _split(s[i])
        s[i] = s[i] + bd_dot(i, s_hi, s_lo, p_hi, p_lo)
    return s


def _col_bcast(tile, c, width=LANES):
    return jnp.broadcast_to(tile[:, c:c + 1], (tile.shape[0], width))


def _delta_body(*refs, n_chunks, has_init, emit_state):
    qkv = refs[0:2]
    ag = refs[2:4]
    gt = refs[4:6]
    norm_ref = refs[6]
    pos = 7
    s0_ref = None
    if has_init:
        s0_ref = refs[pos]
        pos += 1
    ya_ref = refs[pos]
    pos += 1
    sout_ref = None
    if emit_state:
        sout_ref = refs[pos]
        pos += 1
    s_scr, o_scr, bdn_hi, bdn_lo, bdk, bduw_hi, bduw_lo, bds, bdv = refs[pos:pos + 9]

    n = pl.program_id(1)
    gw = GROUP * DK_A

    @pl.when(n == 0)
    def _():
        o_scr[...] = jnp.zeros_like(o_scr)
        for ref in (bdn_hi, bdn_lo, bdk, bduw_hi, bduw_lo, bds, bdv):
            ref[...] = jnp.zeros_like(ref)
        for d in range(N_DIR):
            for h in range(H_A):
                blk = s0_ref[0, d, h] if has_init else jnp.zeros((DK_A, DV_A), F32)
                s_scr[d, h // GROUP, :, (h % GROUP) * DV_A:(h % GROUP + 1) * DV_A] = blk

    row = lax.broadcasted_iota(jnp.int32, (CHUNK, LANES), 0)
    lane = lax.broadcasted_iota(jnp.int32, (CHUNK, LANES), 1)
    col = lane & (CHUNK - 1)
    left = lane < CHUNK
    eye_w = jnp.concatenate([(row == col).astype(F32)] * (GROUP // PAIR), axis=1)
    probs = [(d, g) for d in range(N_DIR) for g in range(N_GROUPS)]
    gates, gc, gc_t, eg, egl, masks = {}, {}, {}, {}, {}, {}
    for d in range(N_DIR):
        masks[d] = (row >= col, row > col) if d == 0 else (row <= col, row < col)
        gates[d] = gt[d][...]
        gc[d] = _dot_f32(_masks(d)[0].astype(F32), gates[d])
        gc_t[d] = jnp.concatenate([gc[d], gc[d]], axis=0).T
        last = CHUNK - 1 if d == 0 else 0
        eg[d] = jnp.exp(gc[d])
        egl[d] = jnp.exp(gc[d][last:last + 1, :] - gc[d])

    def cols_of(d, g):
        return [G_OFF + d * H_A + g * GROUP + hl for hl in range(GROUP)]

    q16, ks, beta_xs, decays, grams = [], [], [], [], []
    for gi, (d, g) in enumerate(probs):
        cols = cols_of(d, g)
        incl, _ = masks[d]
        k = qkv[d][:, A_W + g * gw:A_W + (g + 1) * gw]
        beta_x = jnp.concatenate([_col_bcast(gates[d], BETA_OFF - G_OFF + c) for c in cols], axis=1)
        decay = []
        for p in range(GROUP // PAIR):
            c0, c1 = cols[PAIR * p], cols[PAIR * p + 1]
            gcol = jnp.where(left, _col_bcast(gc[d], c0), _col_bcast(gc[d], c1))
            grow = jnp.where(left[0:1], gc_t[d][c0:c0 + 1, :], gc_t[d][c1:c1 + 1, :])
            decay.append(jnp.exp(jnp.where(incl, gcol - grow, NEG)))
        decays.append(jnp.concatenate(decay, axis=1))
        q16.append(qkv[d][:, g * gw:(g + 1) * gw].astype(BF16))
        ks.append(k)
        beta_xs.append(beta_x)
        _store_blocks(bdk.at[gi], k.astype(BF16), CHUNK, DK_A)
    for gi, (d, g) in enumerate(probs):
        grams.append(lax.dot_general(jnp.concatenate([q16[gi], (ks[gi] * beta_xs[gi]).astype(BF16)], axis=0),
                                     bdk[gi], (((1,), (1,)), ((), ())), preferred_element_type=F32))
    attns, lmats = [], []
    for gi, (d, g) in enumerate(probs):
        strict_w = jnp.concatenate([masks[d][1]] * (GROUP // PAIR), axis=1)
        attns.append((grams[gi][:CHUNK] * decays[gi]).astype(BF16))
        lmats.append(jnp.where(strict_w, grams[gi][CHUNK:] * decays[gi], 0.0))
    ainvs = _tri_inverse_wide(lmats, eye_w, bdn_hi, bdn_lo)
    eg_xs, kd_ts = [], []
    for gi, (d, g) in enumerate(probs):
        cols = cols_of(d, g)
        v = qkv[d][:, 2 * A_W + g * gw:2 * A_W + (g + 1) * gw]
        eg_x = jnp.concatenate([_col_bcast(eg[d], c) for c in cols], axis=1)
        egl_x = jnp.concatenate([_col_bcast(egl[d], c) for c in cols], axis=1)
        eg_xs.append(eg_x)
        bv_hi, bv_lo = _split(v * beta_xs[gi])
        bk_hi, bk_lo = _split(ks[gi] * (beta_xs[gi] * eg_x))
        _store_blocks(bduw_hi.at[gi], bv_hi, CHUNK, DV_A)
        _store_blocks(bduw_lo.at[gi], bv_lo, CHUNK, DV_A)
        _store_blocks(bduw_hi.at[gi], bk_hi, CHUNK, DK_A, col0=gw)
        _store_blocks(bduw_lo.at[gi], bk_lo, CHUNK, DK_A, col0=gw)
        kd = ks[gi] * egl_x
        kd_ts.append(jnp.concatenate([kd[:, hl * DK_A:(hl + 1) * DK_A] for hl in range(GROUP)], axis=0)
                     .T.astype(BF16))
        _store_blocks(bds.at[gi], s_scr[d, g].astype(BF16), DK_A, DV_A)
    uws = []
    for gi in range(len(probs)):
        t_hi, t_lo = _split(ainvs[gi])
        uws.append(_dot3(t_hi, t_lo, bduw_hi[gi], bduw_lo[gi]))
    ws_qs = []
    for gi in range(len(probs)):
        ws_qs.append(jnp.dot(jnp.concatenate([uws[gi][:, gw:].astype(BF16), q16[gi]], axis=0), bds[gi],
                             preferred_element_type=F32))
    for gi in range(len(probs)):
        _store_blocks(bdv.at[gi], (uws[gi][:, :gw] - ws_qs[gi][:CHUNK]).astype(BF16), CHUNK, DV_A)
    rs = []
    for gi in range(len(probs)):
        rs.append(jnp.dot(jnp.concatenate([attns[gi], kd_ts[gi]], axis=0), bdv[gi], preferred_element_type=F32))
    for gi, (d, g) in enumerate(probs):
        cols = cols_of(d, g)
        last = CHUNK - 1 if d == 0 else 0
        o = ws_qs[gi][CHUNK:] * eg_xs[gi] + rs[gi][:CHUNK]
        eg_last = jnp.concatenate(
            [jnp.broadcast_to(eg[d][last:last + 1, c:c + 1], (1, DV_A)) for c in cols], axis=1)
        s_scr[d, g] = s_scr[d, g] * eg_last + rs[gi][CHUNK:]
        cidx = n if d == 0 else n_chunks - 1 - n
        rows = pl.ds(pl.multiple_of(cidx * CHUNK, CHUNK), CHUNK)
        tot = o + o_scr[rows, g * gw:(g + 1) * gw]
        o_scr[rows, g * gw:(g + 1) * gw] = tot
        for hl in range(GROUP):
            hs = slice((g * GROUP + hl) * DV_A, (g * GROUP + hl + 1) * DV_A)
            blk = tot[:, hl * DV_A:(hl + 1) * DV_A]
            ya_ref[rows, hs] = (_rms(blk) * norm_ref[...] * ag[d][:, hs]).astype(BF16)

    if emit_state:
        @pl.when(n == n_chunks - 1)
        def _():
            for d in range(N_DIR):
                for h in range(H_A):
                    sout_ref[0, d, h] = s_scr[d, h // GROUP, :, (h % GROUP) * DV_A:(h % GROUP + 1) * DV_A]


def _delta(qkv, rest, gates, norm_a, s0, batch, n_chunks, emit_state):
    t = n_chunks * CHUNK
    has_init = s0 is not None
    ng = N_DIR * N_GROUPS

    def fwd(b, n):
        return (b * n_chunks + n, 0)

    def bwd(b, n):
        return (b * n_chunks + n_chunks - 1 - n, 0)

    in_specs = [pl.BlockSpec((CHUNK, QKV_W), fwd), pl.BlockSpec((CHUNK, QKV_W), bwd),
                pl.BlockSpec((CHUNK, A_W), fwd), pl.BlockSpec((CHUNK, A_W), bwd),
                pl.BlockSpec((CHUNK, LANES), fwd), pl.BlockSpec((CHUNK, LANES), bwd),
                pl.BlockSpec((1, DV_A), lambda b, n: (0, 0))]
    args = [qkv, qkv, rest, rest, gates, gates, norm_a]
    if has_init:
        in_specs.append(pl.BlockSpec((1, N_DIR, H_A, DK_A, DV_A), lambda b, n: (b, 0, 0, 0, 0)))
        args.append(s0)
    out_specs = [pl.BlockSpec((t, A_W), lambda b, n: (b, 0))]
    out_shape = [jax.ShapeDtypeStruct((batch * t, A_W), BF16)]
    if emit_state:
        out_specs.append(pl.BlockSpec((1, N_DIR, H_A, DK_A, DV_A), lambda b, n: (b, 0, 0, 0, 0)))
        out_shape.append(jax.ShapeDtypeStruct((batch, N_DIR, H_A, DK_A, DV_A), F32))
    return pl.pallas_call(
        functools.partial(_delta_body, n_chunks=n_chunks, has_init=has_init, emit_state=emit_state),
        grid=(batch, n_chunks),
        in_specs=in_specs,
        out_specs=out_specs,
        out_shape=out_shape,
        scratch_shapes=[pltpu.VMEM((N_DIR, N_GROUPS, DK_A, GROUP * DV_A), F32), pltpu.VMEM((t, A_W), F32),
                        pltpu.VMEM((ng, GROUP * CHUNK, GROUP * CHUNK), BF16),
                        pltpu.VMEM((ng, GROUP * CHUNK, GROUP * CHUNK), BF16),
                        pltpu.VMEM((ng, GROUP * CHUNK, GROUP * DK_A), BF16),
                        pltpu.VMEM((ng, GROUP * CHUNK, 2 * GROUP * DK_A), BF16),
                        pltpu.VMEM((ng, GROUP * CHUNK, 2 * GROUP * DK_A), BF16),
                        pltpu.VMEM((ng, GROUP * DK_A, GROUP * DV_A), BF16),
                        pltpu.VMEM((ng, GROUP * CHUNK, GROUP * DV_A), BF16)],
        compiler_params=_params(("parallel", "arbitrary")),
        name="delta_scan",
    )(*args)


CAUG_W = DV_B + LANES
MW = H_B * CAUG_W


def _scan_max(x, d):
    row = lax.broadcasted_iota(jnp.int32, x.shape, 0)
    s = 1
    while s < CHUNK:
        if d == 0:
            shifted = jnp.where(row >= s, pltpu.roll(x, s, 0), NEG)
        else:
            shifted = jnp.where(row < CHUNK - s, pltpu.roll(x, CHUNK - s, 0), NEG)
        x = jnp.maximum(x, shifted)
        s *= 2
    return x


def _mlstm_body(*refs, n_chunks, has_init, emit_state):
    rest = refs[0:2]
    gt = refs[2:4]
    norm_ref = refs[4]
    pos = 5
    c0_ref = m0_ref = None
    if has_init:
        c0_ref, m0_ref = refs[pos], refs[pos + 1]
        pos += 2
    yb_ref = refs[pos]
    pos += 1
    cout_ref = nout_ref = mout_ref = None
    if emit_state:
        cout_ref, nout_ref, mout_ref = refs[pos:pos + 3]
        pos += 3
    c_scr, m_scr, o_scr, bdk, bdc, bdv = refs[pos:pos + 6]

    n = pl.program_id(1)
    qw = H_B * DK_B

    @pl.when(n == 0)
    def _():
        o_scr[...] = jnp.zeros_like(o_scr)
        for ref in (bdk, bdc, bdv):
            ref[...] = jnp.zeros_like(ref)
        for d in range(N_DIR):
            for h in range(H_B):
                bdv[d, h * CHUNK:(h + 1) * CHUNK, h * CAUG_W + DV_B:(h + 1) * CAUG_W] = jnp.ones((CHUNK, LANES), BF16)
        if has_init:
            c_scr[...] = c0_ref[0]
            m_scr[...] = m0_ref[0]
        else:
            c_scr[...] = jnp.zeros_like(c_scr)
            m_scr[...] = jnp.zeros_like(m_scr)

    row = lax.broadcasted_iota(jnp.int32, (CHUNK, LANES), 0)
    lane = lax.broadcasted_iota(jnp.int32, (CHUNK, LANES), 1)
    col = lane & (CHUNK - 1)
    left = lane < CHUNK
    dirs = range(N_DIR)
    nc, a_t, iw, emt, ksc, dec_row = {}, {}, {}, {}, {}, {}
    for d in dirs:
        lo = LF_OFF + d * H_B
        mine = (lane >= lo) & (lane < lo + H_B)
        g = gt[d][...]
        gc = jnp.where(mine, _dot_f32(_masks(d)[0].astype(F32), g), 0.0)
        a = jnp.where(mine, pltpu.roll(g, LF_OFF - LI_OFF, 1), 0.0) - gc
        last = CHUNK - 1 if d == 0 else 0
        m_old = m_scr[d][0:1, :]
        mx = jnp.maximum(m_old, _scan_max(a, d))
        mxl = mx[last:last + 1, :]
        nc[d] = -mx
        a_t[d] = jnp.concatenate([a, a], axis=0).T
        iw[d] = jnp.exp(m_old - mx)
        emt[d] = jnp.exp(-(gc + mx))
        ksc[d] = jnp.exp(a - mxl)
        dec_row[d] = jnp.exp(m_old - mxl)
        m_scr[d] = jnp.broadcast_to(gc[last:last + 1, :] + mxl, (8, LANES))

    q16, ks_t = {}, {}
    for d in dirs:
        lo = LF_OFF + d * H_B
        q16[d] = rest[d][:, OFF_BQ:OFF_BQ + qw].astype(BF16)
        k = rest[d][:, OFF_BK:OFF_BK + qw]
        _store_blocks(bdk.at[d], k.astype(BF16), CHUNK, DK_B)
        ks = k * jnp.concatenate([_col_bcast(ksc[d], lo + h) for h in range(H_B)], axis=1)
        ks_t[d] = jnp.concatenate([ks[:, h * DK_B:(h + 1) * DK_B] for h in range(H_B)], axis=0).T.astype(BF16)
        for h in range(H_B):
            bdv[d, h * CHUNK:(h + 1) * CHUNK, h * CAUG_W:h * CAUG_W + DV_B] = (
                rest[d][:, OFF_BV + h * DV_B:OFF_BV + (h + 1) * DV_B].astype(BF16))
        _store_blocks(bdc.at[d], c_scr[d].astype(BF16), DK_B, CAUG_W)
    qk = {d: lax.dot_general(q16[d], bdk[d], (((1,), (1,)), ((), ())), preferred_element_type=F32) for d in dirs}
    lhs = {}
    for d in dirs:
        lo = LF_OFF + d * H_B
        incl = row >= col if d == 0 else row <= col
        log_w = []
        for p in range(H_B // PAIR):
            l0, l1 = lo + PAIR * p, lo + PAIR * p + 1
            ccol = jnp.where(left, _col_bcast(nc[d], l0), _col_bcast(nc[d], l1))
            crow = jnp.where(left[0:1], a_t[d][l0:l0 + 1, :], a_t[d][l1:l1 + 1, :])
            log_w.append(jnp.where(incl, ccol + crow, NEG))
        dw = jnp.exp(jnp.concatenate(log_w, axis=1)) * qk[d]
        iw_x = jnp.concatenate([_col_bcast(iw[d], lo + h) for h in range(H_B)], axis=1)
        lhs[d] = jnp.concatenate([(rest[d][:, OFF_BQ:OFF_BQ + qw] * iw_x).astype(BF16), dw.astype(BF16)], axis=1)
    num = {d: jnp.dot(lhs[d], jnp.concatenate([bdc[d], bdv[d]], axis=0), preferred_element_type=F32) for d in dirs}
    upd = {d: jnp.dot(ks_t[d], bdv[d], preferred_element_type=F32) for d in dirs}
    for d in dirs:
        lo = LF_OFF + d * H_B
        dec_x = jnp.concatenate([jnp.broadcast_to(dec_row[d][:, lo + h:lo + h + 1], (1, CAUG_W)) for h in range(H_B)],
                                axis=1)
        c_scr[d] = c_scr[d] * dec_x + upd[d]
        cidx = n if d == 0 else n_chunks - 1 - n
        rows = pl.ds(pl.multiple_of(cidx * CHUNK, CHUNK), CHUNK)
        for h in range(H_B):
            vs = slice(h * DV_B, (h + 1) * DV_B)
            den = jnp.maximum(jnp.abs(num[d][:, h * CAUG_W + DV_B:(h + 1) * CAUG_W]), _col_bcast(emt[d], lo + h))
            hb = jnp.concatenate([num[d][:, h * CAUG_W:h * CAUG_W + LANES] / den,
                                  num[d][:, h * CAUG_W + LANES:h * CAUG_W + DV_B] / den], axis=1)
            tot = hb + o_scr[rows, vs]
            o_scr[rows, vs] = tot
            ogate = rest[d][:, OFF_BO + h * DV_B:OFF_BO + (h + 1) * DV_B]
            yb_ref[rows, vs] = (_rms(tot) * norm_ref[...] * ogate).astype(BF16)

    if emit_state:
        @pl.when(n == n_chunks - 1)
        def _():
            for d in dirs:
                for h in range(H_B):
                    cout_ref[0, d, h] = c_scr[d, :, h * CAUG_W:h * CAUG_W + DV_B]
                    nout_ref[0, d, h] = c_scr[d, :, h * CAUG_W + DV_B:(h + 1) * CAUG_W]
            mout_ref[0] = m_scr[...]


def _mlstm(rest, gates, norm_b, c0, m0, batch, n_chunks, emit_state):
    t = n_chunks * CHUNK
    has_init = c0 is not None

    def fwd(b, n):
        return (b * n_chunks + n, 0)

    def bwd(b, n):
        return (b * n_chunks + n_chunks - 1 - n, 0)

    mspec = pl.BlockSpec((1, N_DIR, 8, LANES), lambda b, n: (b, 0, 0, 0))
    in_specs = [pl.BlockSpec((CHUNK, REST_W), fwd), pl.BlockSpec((CHUNK, REST_W), bwd),
                pl.BlockSpec((CHUNK, LANES), fwd), pl.BlockSpec((CHUNK, LANES), bwd),
                pl.BlockSpec((1, DV_B), lambda b, n: (0, 0))]
    args = [rest, rest, gates, gates, norm_b]
    if has_init:
        in_specs += [pl.BlockSpec((1, N_DIR, DK_B, MW), lambda b, n: (b, 0, 0, 0)), mspec]
        args += [c0, m0]
    out_specs = [pl.BlockSpec((t, B_W), lambda b, n: (b, 0))]
    out_shape = [jax.ShapeDtypeStruct((batch * t, B_W), BF16)]
    if emit_state:
        out_specs += [pl.BlockSpec((1, N_DIR, H_B, DK_B, DV_B), lambda b, n: (b, 0, 0, 0, 0)),
                      pl.BlockSpec((1, N_DIR, H_B, DK_B, LANES), lambda b, n: (b, 0, 0, 0, 0)), mspec]
        out_shape += [jax.ShapeDtypeStruct((batch, N_DIR, H_B, DK_B, DV_B), F32),
                      jax.ShapeDtypeStruct((batch, N_DIR, H_B, DK_B, LANES), F32),
                      jax.ShapeDtypeStruct((batch, N_DIR, 8, LANES), F32)]
    return pl.pallas_call(
        functools.partial(_mlstm_body, n_chunks=n_chunks, has_init=has_init, emit_state=emit_state),
        grid=(batch, n_chunks),
        in_specs=in_specs,
        out_specs=out_specs,
        out_shape=out_shape,
        scratch_shapes=[pltpu.VMEM((N_DIR, DK_B, MW), F32),
                        pltpu.VMEM((N_DIR, 8, LANES), F32),
                        pltpu.VMEM((t, B_W), F32),
                        pltpu.VMEM((N_DIR, H_B * CHUNK, H_B * DK_B), BF16),
                        pltpu.VMEM((N_DIR, H_B * DK_B, MW), BF16),
                        pltpu.VMEM((N_DIR, H_B * CHUNK, MW), BF16)],
        compiler_params=_params(("parallel", "arbitrary")),
        name="mlstm_scan",
    )(*args)


def _outproj_body(ya_ref, yb_ref, wa_ref, wb_ref, x_ref, mod_ref, post1_ref, pre2_ref, x1_ref, h2_ref):
    mix = (jnp.dot(ya_ref[...], wa_ref[...], preferred_element_type=F32)
           + jnp.dot(yb_ref[...], wb_ref[...], preferred_element_type=F32))
    x1 = x_ref[...] + mod_ref[0, 2:3, :] * (_rms(mix) * post1_ref[...])
    x1_ref[...] = x1
    h2 = _rms(x1) * pre2_ref[...] * (1.0 + mod_ref[0, 4:5, :]) + mod_ref[0, 3:4, :]
    h2_ref[...] = h2.astype(BF16)


def _outproj(ya, yb, w_out, x2d, mod3, mod_map, post1, pre2, tm):
    m = x2d.shape[0]
    row = lambda i: (i, 0)
    const = lambda i: (0, 0)
    return pl.pallas_call(
        _outproj_body,
        grid=(m // tm,),
        in_specs=[pl.BlockSpec((tm, A_W), row), pl.BlockSpec((tm, B_W), row),
                  pl.BlockSpec((A_W, D_MODEL), lambda i: (0, 0)),
                  pl.BlockSpec((B_W, D_MODEL), lambda i: (1, 0)),
                  pl.BlockSpec((tm, D_MODEL), row),
                  pl.BlockSpec((1, 6, D_MODEL), lambda i: (mod_map(i * tm), 0, 0)),
                  pl.BlockSpec((1, D_MODEL), const), pl.BlockSpec((1, D_MODEL), const)],
        out_specs=[pl.BlockSpec((tm, D_MODEL), row), pl.BlockSpec((tm, D_MODEL), row)],
        out_shape=[jax.ShapeDtypeStruct((m, D_MODEL), F32), jax.ShapeDtypeStruct((m, D_MODEL), BF16)],
        compiler_params=_params(("parallel",)),
        name="outproj",
    )(ya, yb, w_out, w_out, x2d, mod3, post1, pre2)


def _ffn_body(h2_ref, w1_ref, w2_ref, x1_ref, mod_ref, post2_ref, o_ref):
    kk = pl.program_id(1)
    a = jnp.maximum(jnp.dot(h2_ref[...], w1_ref[...], preferred_element_type=F32), 0.0)
    contrib = jnp.dot((a * a).astype(BF16), w2_ref[...], preferred_element_type=F32)

    @pl.when(kk == 0)
    def _():
        o_ref[...] = contrib

    @pl.when((kk > 0) & (kk < pl.num_programs(1) - 1))
    def _():
        o_ref[...] += contrib

    @pl.when(kk == pl.num_programs(1) - 1)
    def _():
        f = o_ref[...] + contrib
        o_ref[...] = x1_ref[...] + mod_ref[0, 5:6, :] * (_rms(f) * post2_ref[...])


def _ffn(h2, w1, w2, x1, mod3, mod_map, post2, tm, fc):
    m = h2.shape[0]
    return pl.pallas_call(
        _ffn_body,
        grid=(m // tm, FFN // fc),
        in_specs=[pl.BlockSpec((tm, D_MODEL), lambda i, k: (i, 0)),
                  pl.BlockSpec((D_MODEL, fc), lambda i, k: (0, k)),
                  pl.BlockSpec((fc, D_MODEL), lambda i, k: (k, 0)),
                  pl.BlockSpec((tm, D_MODEL), lambda i, k: (i, 0)),
                  pl.BlockSpec((1, 6, D_MODEL), lambda i, k: (mod_map(i * tm), 0, 0)),
                  pl.BlockSpec((1, D_MODEL), lambda i, k: (0, 0))],
        out_specs=pl.BlockSpec((tm, D_MODEL), lambda i, k: (i, 0)),
        out_shape=jax.ShapeDtypeStruct((m, D_MODEL), F32),
        compiler_params=_params(("parallel", "arbitrary")),
        name="ffn",
    )(h2, w1, w2, x1, mod3, post2)


def _block(x, mod3, mod_of_row, lp, init, seq_len, emit_state):
    bsz, t, _ = x.shape
    x2d = x.reshape(bsz * t, D_MODEL)
    n_chunks = t // CHUNK
    qkv, rest, gates = _proj(x2d, mod3, lambda i: mod_of_row(i * PROJ_TM), lp["pre1"], lp["w_head"],
                             lp["w_tail"], lp["conv_w"], lp["gate_p"], seq_len)
    s0, c0, m0 = init if init is not None else (None, None, None)
    d_out = _delta(qkv, rest, gates, lp["norm_a"], s0, bsz, n_chunks, emit_state)
    m_out = _mlstm(rest, gates, lp["norm_b"], c0, m0, bsz, n_chunks, emit_state)
    x1, h2 = _outproj(d_out[0], m_out[0], lp["w_out"], x2d, mod3, mod_of_row, lp["post1"], lp["pre2"], 256)
    y = _ffn(h2, lp["w1"], lp["w2"], x1, mod3, mod_of_row, lp["post2"], 512, 1024)
    states = None
    if emit_state:
        m_fin = jnp.stack([m_out[3][:, d, 0, LF_OFF + d * H_B:LF_OFF + (d + 1) * H_B] for d in range(N_DIR)], axis=1)
        states = (d_out[1], m_out[1], m_out[2][..., 0], m_fin)
    return y.reshape(bsz, t, D_MODEL), states


def _layer_params(l, norm_mix_pre, norm_mix_post, norm_ffn_pre, norm_ffn_post, w_in, conv_w, a_log, dt_bias,
                  norm_a, mlstm_ibias, mlstm_fbias, norm_b, w_out, w_ffn1, w_ffn2):
    w = w_in[l]
    o_ag = QKV_W
    o_aa = o_ag + A_W
    o_ab = o_aa + N_DIR * H_A
    o_bq = o_ab + N_DIR * H_A
    o_bi = o_bq + 2 * H_B * DK_B + 2 * B_W
    o_bf = o_bi + N_DIR * H_B
    n_gate = 2 * N_DIR * H_A + 2 * N_DIR * H_B
    w_tail = jnp.concatenate([w[:, o_bq:o_bi].astype(BF16), w[:, o_aa:o_bq].astype(BF16),
                              w[:, o_bi:o_bf + N_DIR * H_B].astype(BF16),
                              jnp.zeros((D_MODEL, N_TAIL_T * PROJ_TN - (o_bi - o_bq) - n_gate), BF16)], axis=1)

    def lane_row(vals, off):
        return jnp.zeros((LANES,), F32).at[off:off + vals.size].set(vals.reshape(-1))

    gate_p = jnp.stack([lane_row(a_log[l], G_OFF), lane_row(dt_bias[l], G_OFF),
                        lane_row(mlstm_ibias[l], LI_OFF) + lane_row(mlstm_fbias[l], LF_OFF)]
                       + [jnp.zeros((LANES,), F32)] * 5)
    row = lambda v: v[l].reshape(1, -1)
    return dict(
        pre1=row(norm_mix_pre), post1=row(norm_mix_post), pre2=row(norm_ffn_pre), post2=row(norm_ffn_post),
        w_head=w[:, :o_aa].astype(BF16), w_tail=w_tail, gate_p=gate_p,
        conv_w=jnp.concatenate([conv_w[l].T, jnp.zeros((5, QKV_W), F32)], axis=0),
        norm_a=row(norm_a), norm_b=row(norm_b),
        w_out=w_out[l].astype(BF16), w1=w_ffn1[l].astype(BF16), w2=w_ffn2[l].astype(BF16))


def kernel(x_prompt, x_sample, state_delta, state_mlstm_C, state_mlstm_n, state_mlstm_m, c, c_ctx, w_ada, b_ada, norm_mix_pre, norm_mix_post, norm_ffn_pre, norm_ffn_post, w_in, conv_w, a_log, dt_bias, norm_a, mlstm_ibias, mlstm_fbias, norm_b, w_out, w_ffn1, w_ffn2):
    depth = w_in.shape[0]
    n_lat = x_sample.shape[0]
    t_lat = x_sample.shape[1]
    cond = jnp.concatenate([c_ctx[None, :], c, jnp.zeros((8 - 1 - n_lat, D_MODEL), F32)], axis=0)
    y_prompt, y_sample = x_prompt, x_sample
    acc = ([], [], [], [])
    for l in range(depth):
        lp = _layer_params(l, norm_mix_pre, norm_mix_post, norm_ffn_pre, norm_ffn_post, w_in, conv_w, a_log,
                           dt_bias, norm_a, mlstm_ibias, mlstm_fbias, norm_b, w_out, w_ffn1, w_ffn2)
        mod = _ada(cond, w_ada, b_ada[l].reshape(1, -1), l)
        mod3 = mod[:1 + n_lat].reshape(1 + n_lat, 6, D_MODEL)
        y_prompt, st = _block(y_prompt, mod3, lambda r: 0, lp, None, x_prompt.shape[1], True)
        for a, s in zip(acc, st):
            a.append(s)
        n_rep = jnp.broadcast_to(state_mlstm_n[:, l][..., None], state_mlstm_n[:, l].shape + (LANES,))
        c_aug0 = jnp.concatenate([state_mlstm_C[:, l], n_rep], axis=-1)
        c_aug0 = c_aug0.transpose(0, 1, 3, 2, 4).reshape(n_lat, N_DIR, DK_B, MW)
        m0 = jnp.zeros((n_lat, N_DIR, LANES), F32)
        for d in range(N_DIR):
            m0 = m0.at[:, d, LF_OFF + d * H_B:LF_OFF + (d + 1) * H_B].set(state_mlstm_m[:, l, d])
        m0 = jnp.broadcast_to(m0[:, :, None, :], (n_lat, N_DIR, 8, LANES))
        y_sample, _ = _block(y_sample, mod3, lambda r: 1 + r // t_lat, lp, (state_delta[:, l], c_aug0, m0),
                             GRID_W, False)
    return (y_prompt, y_sample) + tuple(jnp.stack(a, axis=1) for a in acc)
```

```python
import functools

import jax
import jax.numpy as jnp
from jax import lax
from jax.experimental import pallas as pl
from jax.experimental.pallas import tpu as pltpu

F32 = jnp.float32
BF16 = jnp.bfloat16

D_MODEL = 2048
N_DIR = 2
A_W = D_MODEL // 2
B_W = D_MODEL - A_W
DK_A = 128
DV_A = 128
H_A = A_W // DV_A
DV_B = 256
DK_B = DV_B // 2
H_B = B_W // DV_B
GRID_W = 64
CHUNK = 64
FFN = 4 * D_MODEL
EPS = 1e-6
LANES = 128
NEG = -1e30

QKV_W = 3 * A_W
REST_W = A_W + 2 * H_B * DK_B + 2 * B_W
OFF_AG, OFF_BQ, OFF_BK, OFF_BV, OFF_BO = 0, A_W, A_W + H_B * DK_B, A_W + 2 * H_B * DK_B, A_W + 2 * H_B * DK_B + B_W
G_OFF, BETA_OFF = 0, N_DIR * H_A
LI_OFF = 2 * N_DIR * H_A
LF_OFF = LI_OFF + N_DIR * H_B

VMEM_LIMIT = 56 * 1024 * 1024


def _sigmoid(x):
    return 1.0 / (1.0 + jnp.exp(-x))


def _softplus(x):
    return jnp.maximum(x, 0.0) + jnp.log1p(jnp.exp(-jnp.abs(x)))


def _dot(a, b):
    return jnp.dot(a.astype(BF16), b.astype(BF16), preferred_element_type=F32)


def _dot_f32(a, b):
    return jnp.dot(a, b, precision=lax.Precision.HIGHEST, preferred_element_type=F32)


def _rms(x):
    return x * lax.rsqrt(jnp.mean(x * x, axis=-1, keepdims=True) + EPS)


def _params(sem):
    return pltpu.CompilerParams(dimension_semantics=sem, vmem_limit_bytes=VMEM_LIMIT)


def _ada_body(c_ref, w_ref, b_ref, o_ref):
    c = c_ref[...]
    o_ref[...] = _dot(c * _sigmoid(c), w_ref[...]) + b_ref[...]


def _ada(c_all, w_ada, b, l):
    n = w_ada.shape[2]
    tn = 1024
    return pl.pallas_call(
        _ada_body,
        grid=(n // tn,),
        in_specs=[pl.BlockSpec(c_all.shape, lambda j: (0, 0)),
                  pl.BlockSpec((None, D_MODEL, tn), lambda j: (l, 0, j)),
                  pl.BlockSpec((1, tn), lambda j: (0, j))],
        out_specs=pl.BlockSpec((c_all.shape[0], tn), lambda j: (0, j)),
        out_shape=jax.ShapeDtypeStruct((c_all.shape[0], n), F32),
        compiler_params=_params(("arbitrary",)),
        name="ada",
    )(c_all, w_ada, b)


PROJ_TM = 1024
PROJ_RB = 512
PROJ_TN = 512
N_QKV_T = QKV_W // PROJ_TN
N_REST_T = REST_W // PROJ_TN
N_HEAD_T = (QKV_W + A_W) // PROJ_TN
N_MID_T = N_QKV_T + N_REST_T - N_HEAD_T


def _proj_body(x_ref, mod_ref, g_ref, wh_ref, wb_ref, wg_ref, cw_ref, gp_ref, qkv_ref, rest_ref, gate_ref, h_scr, *,
               seq_len):
    j = pl.program_id(1)

    @pl.when(j == 0)
    def _():
        y = _rms(x_ref[...]) * g_ref[...]
        h_scr[...] = (y * (1.0 + mod_ref[0, 1:2, :]) + mod_ref[0, 0:1, :]).astype(BF16)

    def run(w_ref, epilogue, out_ref):
        for rb in range(PROJ_TM // PROJ_RB):
            rs = slice(rb * PROJ_RB, (rb + 1) * PROJ_RB)
            epilogue(jnp.dot(h_scr[rs, :], w_ref[...], preferred_element_type=F32), out_ref, rs)

    def conv_silu(acc):
        pos = lax.broadcasted_iota(jnp.int32, acc.shape, 0) & (seq_len - 1)
        prev = jnp.where(pos == 0, 0.0, pltpu.roll(acc, 1, 0))
        nxt = jnp.where(pos == seq_len - 1, 0.0, pltpu.roll(acc, acc.shape[0] - 1, 0))
        y = prev * cw_ref[0:1, :] + acc * cw_ref[1:2, :] + nxt * cw_ref[2:3, :]
        return y * _sigmoid(y)

    def ep_l2(scale):
        def f(acc, out_ref, rs):
            y = conv_silu(acc)
            for g in range(PROJ_TN // LANES):
                blk = y[:, g * LANES:(g + 1) * LANES]
                inv = lax.rsqrt(jnp.sum(blk * blk, axis=-1, keepdims=True) + EPS)
                out_ref[rs, g * LANES:(g + 1) * LANES] = blk * (inv * scale)
        return f

    def ep_map(fn):
        def f(acc, out_ref, rs):
            out_ref[rs, :] = fn(acc)
        return f

    def ep_gate(z, out_ref, rs):
        lane = lax.broadcasted_iota(jnp.int32, z.shape, 1)
        g = -jnp.exp(gp_ref[0:1, :]) * _softplus(z + gp_ref[1:2, :])
        li = z + gp_ref[2:3, :]
        out_ref[rs, :] = jnp.where(lane < BETA_OFF, g,
                                   jnp.where(lane < LI_OFF, _sigmoid(z),
                                             jnp.where(lane < LF_OFF, li,
                                                       jnp.where(lane < LF_OFF + N_DIR * H_B, -_softplus(-li), 0.0))))

    tp = A_W // PROJ_TN
    c0 = (j - N_QKV_T) * PROJ_TN
    in_rest = (j >= N_QKV_T) & (j < N_QKV_T + N_REST_T)
    variants = [
        (j < tp, wh_ref, ep_l2(DK_A ** -0.5), qkv_ref),
        ((j >= tp) & (j < 2 * tp), wh_ref, ep_l2(1.0), qkv_ref),
        ((j >= 2 * tp) & (j < N_QKV_T), wh_ref, ep_map(conv_silu), qkv_ref),
        (in_rest & (c0 < OFF_BQ), wh_ref, ep_map(lambda a: a * _sigmoid(a)), rest_ref),
        (in_rest & (c0 >= OFF_BQ) & (c0 < OFF_BK), wb_ref, ep_map(lambda a: a * (DK_B ** -0.5)), rest_ref),
        (in_rest & (c0 >= OFF_BK) & (c0 < OFF_BO), wb_ref, ep_map(lambda a: a), rest_ref),
        (in_rest & (c0 >= OFF_BO), wb_ref, ep_map(_sigmoid), rest_ref),
        (j == N_QKV_T + N_REST_T, wg_ref, ep_gate, gate_ref),
    ]
    for cond, w_ref, epilogue, out_ref in variants:
        pl.when(cond)(functools.partial(run, w_ref, epilogue, out_ref))


def _proj(x2d, mod3, mod_map, pre_g, w16, w_mid, w_gate, conv_w, gate_p, seq_len):
    m = x2d.shape[0]
    tm, tn = PROJ_TM, PROJ_TN
    assert PROJ_RB % seq_len == 0 and m % tm == 0
    return pl.pallas_call(
        functools.partial(_proj_body, seq_len=seq_len),
        grid=(m // tm, N_HEAD_T + N_MID_T + 1),
        in_specs=[pl.BlockSpec((tm, D_MODEL), lambda i, j: (i, 0)),
                  pl.BlockSpec((1, 6, D_MODEL), lambda i, j: (mod_map(i), 0, 0)),
                  pl.BlockSpec((1, D_MODEL), lambda i, j: (0, 0)),
                  pl.BlockSpec((D_MODEL, tn), lambda i, j: (0, jnp.minimum(j, N_HEAD_T - 1))),
                  pl.BlockSpec((D_MODEL, tn), lambda i, j: (0, jnp.clip(j - N_HEAD_T, 0, N_MID_T - 1))),
                  pl.BlockSpec((D_MODEL, LANES), lambda i, j: (0, 0)),
                  pl.BlockSpec((8, tn), lambda i, j: (0, jnp.minimum(j, N_QKV_T - 1))),
                  pl.BlockSpec((8, LANES), lambda i, j: (0, 0))],
        out_specs=[pl.BlockSpec((tm, tn), lambda i, j: (i, jnp.minimum(j, N_QKV_T - 1))),
                   pl.BlockSpec((tm, tn), lambda i, j: (i, jnp.clip(j - N_QKV_T, 0, N_REST_T - 1))),
                   pl.BlockSpec((tm, LANES), lambda i, j: (i, 0))],
        out_shape=[jax.ShapeDtypeStruct((m, QKV_W), F32), jax.ShapeDtypeStruct((m, REST_W), F32),
                   jax.ShapeDtypeStruct((m, LANES), F32)],
        scratch_shapes=[pltpu.VMEM((tm, D_MODEL), BF16)],
        compiler_params=_params(("parallel", "arbitrary")),
        name="proj",
    )(x2d, mod3, pre_g, w16, w_mid, w_gate, conv_w, gate_p)


def _masks(d):
    row = lax.broadcasted_iota(jnp.int32, (CHUNK, CHUNK), 0)
    col = lax.broadcasted_iota(jnp.int32, (CHUNK, CHUNK), 1)
    if d == 0:
        return row >= col, row > col, row == col
    return row <= col, row < col, row == col


GROUP = 4
N_GROUPS = H_A // GROUP
PAIR = LANES // CHUNK


def _split(x):
    hi = x.astype(BF16)
    return hi, (x - hi.astype(F32)).astype(BF16)


def _store_blocks(ref, x, blk_r, blk_c, col0=0):
    for h in range(GROUP):
        ref[h * blk_r:(h + 1) * blk_r, col0 + h * blk_c:col0 + (h + 1) * blk_c] = x[:, h * blk_c:(h + 1) * blk_c]


def _dot3(a_hi, a_lo, b_hi, b_lo):
    lhs = jnp.concatenate([a_hi, a_lo, a_hi], axis=1)
    rhs = jnp.concatenate([b_hi, b_hi, b_lo], axis=0)
    return jnp.dot(lhs, rhs, preferred_element_type=F32)


def _tri_inverse_wide(lmats, eye_w, bd_hi, bd_lo):
    def bd_dot(i, a_hi, a_lo, b_hi, b_lo):
        _store_blocks(bd_hi.at[i], b_hi, CHUNK, CHUNK)
        _store_blocks(bd_lo.at[i], b_lo, CHUNK, CHUNK)
        return _dot3(a_hi, a_lo, bd_hi[i], bd_lo[i])

    idx = range(len(lmats))
    s = [eye_w - l for l in lmats]
    p = []
    for i in idx:
        m_hi, m_lo = _split(-lmats[i])
        p.append(bd_dot(i, m_hi, m_lo, m_hi, m_lo))
    for _ in range(4):
        for i in idx:
            p_hi, p_lo = _split(p[i])
            s_hi, s_lo = _split(s[i])
            r = bd_dot(i, jnp.concatenate([p_hi, s_hi], axis=0), jnp.concatenate([p_lo, s_lo], axis=0), p_hi, p_lo)
            p[i] = r[:CHUNK]
            s[i] = s[i] + r[CHUNK:]
    for i in idx:
        p_hi, p_lo = _split(p[i])
        s_hi, s_lo = _split(s[i])
        s[i] = s[i] + bd_dot(i, s_hi, s_lo, p_hi, p_lo)
    return s


def _col_bcast(tile, c, width=LANES):
    return jnp.broadcast_to(tile[:, c:c + 1], (tile.shape[0], width))


def _delta_body(*refs, n_chunks, has_init, emit_state):
    qkv = refs[0:2]
    ag = refs[2:4]
    gt = refs[4:6]
    norm_ref = refs[6]
    pos = 7
    s0_ref = None
    if has_init:
        s0_ref = refs[pos]
        pos += 1
    ya_ref = refs[pos]
    pos += 1
    sout_ref = None
    if emit_state:
        sout_ref = refs[pos]
        pos += 1
    s_scr, o_scr, bdn_hi, bdn_lo, bdk, bduw, bds, bdv = refs[pos:pos + 8]

    n = pl.program_id(1)
    gw = GROUP * DK_A

    @pl.when(n == 0)
    def _():
        o_scr[...] = jnp.zeros_like(o_scr)
        for ref in (bdn_hi, bdn_lo, bdk, bduw, bds, bdv):
            ref[...] = jnp.zeros_like(ref)
        for d in range(N_DIR):
            for h in range(H_A):
                blk = s0_ref[0, d, h] if has_init else jnp.zeros((DK_A, DV_A), F32)
                s_scr[d, h // GROUP, :, (h % GROUP) * DV_A:(h % GROUP + 1) * DV_A] = blk

    row = lax.broadcasted_iota(jnp.int32, (CHUNK, LANES), 0)
    lane = lax.broadcasted_iota(jnp.int32, (CHUNK, LANES), 1)
    col = lane & (CHUNK - 1)
    left = lane < CHUNK
    eye_w = jnp.concatenate([(row == col).astype(F32)] * (GROUP // PAIR), axis=1)
    probs = [(d, g) for d in range(N_DIR) for g in range(N_GROUPS)]
    gates, gc, gc_t, eg, egl, masks = {}, {}, {}, {}, {}, {}
    for d in range(N_DIR):
        masks[d] = (row >= col, row > col) if d == 0 else (row <= col, row < col)
        gates[d] = gt[d][...]
        gc[d] = _dot_f32(_masks(d)[0].astype(F32), gates[d])
        gc_t[d] = jnp.concatenate([gc[d], gc[d]], axis=0).T
        last = CHUNK - 1 if d == 0 else 0
        eg[d] = jnp.exp(gc[d])
        egl[d] = jnp.exp(gc[d][last:last + 1, :] - gc[d])

    def cols_of(d, g):
        return [G_OFF + d * H_A + g * GROUP + hl for hl in range(GROUP)]

    q16, ks, beta_xs, decays, grams = [], [], [], [], []
    for gi, (d, g) in enumerate(probs):
        cols = cols_of(d, g)
        incl, _ = masks[d]
        k = qkv[d][:, A_W + g * gw:A_W + (g + 1) * gw]
        beta_x = jnp.concatenate([_col_bcast(gates[d], BETA_OFF - G_OFF + c) for c in cols], axis=1)
        decay = []
        for p in range(GROUP // PAIR):
            c0, c1 = cols[PAIR * p], cols[PAIR * p + 1]
            gcol = jnp.where(left, _col_bcast(gc[d], c0), _col_bcast(gc[d], c1))
            grow = jnp.where(left[0:1], gc_t[d][c0:c0 + 1, :], gc_t[d][c1:c1 + 1, :])
            decay.append(jnp.exp(jnp.where(incl, gcol - grow, NEG)))
        decays.append(jnp.concatenate(decay, axis=1))
        q16.append(qkv[d][:, g * gw:(g + 1) * gw].astype(BF16))
        ks.append(k)
        beta_xs.append(beta_x)
        _store_blocks(bdk.at[gi], k.astype(BF16), CHUNK, DK_A)
    for gi, (d, g) in enumerate(probs):
        grams.append(lax.dot_general(jnp.concatenate([q16[gi], (ks[gi] * beta_xs[gi]).astype(BF16)], axis=0),
                                     bdk[gi], (((1,), (1,)), ((), ())), preferred_element_type=F32))
    attns, lmats = [], []
    for gi, (d, g) in enumerate(probs):
        strict_w = jnp.concatenate([masks[d][1]] * (GROUP // PAIR), axis=1)
        attns.append((grams[gi][:CHUNK] * decays[gi]).astype(BF16))
        lmats.append(jnp.where(strict_w, grams[gi][CHUNK:] * decays[gi], 0.0))
    ainvs = _tri_inverse_wide(lmats, eye_w, bdn_hi, bdn_lo)
    eg_xs, kd_ts = [], []
    for gi, (d, g) in enumerate(probs):
        cols = cols_of(d, g)
        v = qkv[d][:, 2 * A_W + g * gw:2 * A_W + (g + 1) * gw]
        eg_x = jnp.concatenate([_col_bcast(eg[d], c) for c in cols], axis=1)
        egl_x = jnp.concatenate([_col_bcast(egl[d], c) for c in cols], axis=1)
        eg_xs.append(eg_x)
        _store_blocks(bduw.at[gi], (v * beta_xs[gi]).astype(BF16), CHUNK, DV_A)
        _store_blocks(bduw.at[gi], (ks[gi] * (beta_xs[gi] * eg_x)).astype(BF16), CHUNK, DK_A, col0=gw)
        kd = ks[gi] * egl_x
        kd_ts.append(jnp.concatenate([kd[:, hl * DK_A:(hl + 1) * DK_A] for hl in range(GROUP)], axis=0)
                     .T.astype(BF16))
        _store_blocks(bds.at[gi], s_scr[d, g].astype(BF16), DK_A, DV_A)
    uws = []
    for gi in range(len(probs)):
        t_hi, t_lo = _split(ainvs[gi])
        uws.append(jnp.dot(jnp.concatenate([t_hi, t_lo], axis=1), jnp.concatenate([bduw[gi], bduw[gi]], axis=0),
                           preferred_element_type=F32))
    ws_qs = []
    for gi in range(len(probs)):
        ws_qs.append(jnp.dot(jnp.concatenate([uws[gi][:, gw:].astype(BF16), q16[gi]], axis=0), bds[gi],
                             preferred_element_type=F32))
    for gi in range(len(probs)):
        _store_blocks(bdv.at[gi], (uws[gi][:, :gw] - ws_qs[gi][:CHUNK]).astype(BF16), CHUNK, DV_A)
    rs = []
    for gi in range(len(probs)):
        rs.append(jnp.dot(jnp.concatenate([attns[gi], kd_ts[gi]], axis=0), bdv[gi], preferred_element_type=F32))
    for gi, (d, g) in enumerate(probs):
        cols = cols_of(d, g)
        last = CHUNK - 1 if d == 0 else 0
        o = ws_qs[gi][CHUNK:] * eg_xs[gi] + rs[gi][:CHUNK]
        eg_last = jnp.concatenate(
            [jnp.broadcast_to(eg[d][last:last + 1, c:c + 1], (1, DV_A)) for c in cols], axis=1)
        s_scr[d, g] = s_scr[d, g] * eg_last + rs[gi][CHUNK:]
        cidx = n if d == 0 else n_chunks - 1 - n
        rows = pl.ds(pl.multiple_of(cidx * CHUNK, CHUNK), CHUNK)
        tot = o + o_scr[rows, g * gw:(g + 1) * gw]
        o_scr[rows, g * gw:(g + 1) * gw] = tot
        for hl in range(GROUP):
            hs = slice((g * GROUP + hl) * DV_A, (g * GROUP + hl + 1) * DV_A)
            blk = tot[:, hl * DV_A:(hl + 1) * DV_A]
            ya_ref[rows, hs] = (_rms(blk) * norm_ref[...] * ag[d][:, hs]).astype(BF16)

    if emit_state:
        @pl.when(n == n_chunks - 1)
        def _():
            for d in range(N_DIR):
                for h in range(H_A):
                    sout_ref[0, d, h] = s_scr[d, h // GROUP, :, (h % GROUP) * DV_A:(h % GROUP + 1) * DV_A]


def _delta(qkv, rest, gates, norm_a, s0, batch, n_chunks, emit_state):
    t = n_chunks * CHUNK
    has_init = s0 is not None
    ng = N_DIR * N_GROUPS

    def fwd(b, n):
        return (b * n_chunks + n, 0)

    def bwd(b, n):
        return (b * n_chunks + n_chunks - 1 - n, 0)

    in_specs = [pl.BlockSpec((CHUNK, QKV_W), fwd), pl.BlockSpec((CHUNK, QKV_W), bwd),
                pl.BlockSpec((CHUNK, A_W), fwd), pl.BlockSpec((CHUNK, A_W), bwd),
                pl.BlockSpec((CHUNK, LANES), fwd), pl.BlockSpec((CHUNK, LANES), bwd),
                pl.BlockSpec((1, DV_A), lambda b, n: (0, 0))]
    args = [qkv, qkv, rest, rest, gates, gates, norm_a]
    if has_init:
        in_specs.append(pl.BlockSpec((1, N_DIR, H_A, DK_A, DV_A), lambda b, n: (b, 0, 0, 0, 0)))
        args.append(s0)
    out_specs = [pl.BlockSpec((t, A_W), lambda b, n: (b, 0))]
    out_shape = [jax.ShapeDtypeStruct((batch * t, A_W), BF16)]
    if emit_state:
        out_specs.append(pl.BlockSpec((1, N_DIR, H_A, DK_A, DV_A), lambda b, n: (b, 0, 0, 0, 0)))
        out_shape.append(jax.ShapeDtypeStruct((batch, N_DIR, H_A, DK_A, DV_A), F32))
    return pl.pallas_call(
        functools.partial(_delta_body, n_chunks=n_chunks, has_init=has_init, emit_state=emit_state),
        grid=(batch, n_chunks),
        in_specs=in_specs,
        out_specs=out_specs,
        out_shape=out_shape,
        scratch_shapes=[pltpu.VMEM((N_DIR, N_GROUPS, DK_A, GROUP * DV_A), F32), pltpu.VMEM((t, A_W), F32),
                        pltpu.VMEM((ng, GROUP * CHUNK, GROUP * CHUNK), BF16),
                        pltpu.VMEM((ng, GROUP * CHUNK, GROUP * CHUNK), BF16),
                        pltpu.VMEM((ng, GROUP * CHUNK, GROUP * DK_A), BF16),
                        pltpu.VMEM((ng, GROUP * CHUNK, 2 * GROUP * DK_A), BF16),
                        pltpu.VMEM((ng, GROUP * DK_A, GROUP * DV_A), BF16),
                        pltpu.VMEM((ng, GROUP * CHUNK, GROUP * DV_A), BF16)],
        compiler_params=_params(("parallel", "arbitrary")),
        name="delta_scan",
    )(*args)


CAUG_W = DV_B + LANES
MW = H_B * CAUG_W


def _scan_max(x, d):
    row = lax.broadcasted_iota(jnp.int32, x.shape, 0)
    s = 1
    while s < CHUNK:
        if d == 0:
            shifted = jnp.where(row >= s, pltpu.roll(x, s, 0), NEG)
        else:
            shifted = jnp.where(row < CHUNK - s, pltpu.roll(x, CHUNK - s, 0), NEG)
        x = jnp.maximum(x, shifted)
        s *= 2
    return x


def _mlstm_body(*refs, n_chunks, has_init, emit_state):
    rest = refs[0:2]
    gt = refs[2:4]
    norm_ref = refs[4]
    pos = 5
    c0_ref = m0_ref = None
    if has_init:
        c0_ref, m0_ref = refs[pos], refs[pos + 1]
        pos += 2
    yb_ref = refs[pos]
    pos += 1
    cout_ref = nout_ref = mout_ref = None
    if emit_state:
        cout_ref, nout_ref, mout_ref = refs[pos:pos + 3]
        pos += 3
    c_scr, m_scr, o_scr, bdk, bdc, bdv = refs[pos:pos + 6]

    n = pl.program_id(1)
    qw = H_B * DK_B

    @pl.when(n == 0)
    def _():
        o_scr[...] = jnp.zeros_like(o_scr)
        for ref in (bdk, bdc, bdv):
            ref[...] = jnp.zeros_like(ref)
        for d in range(N_DIR):
            for h in range(H_B):
                bdv[d, h * CHUNK:(h + 1) * CHUNK, h * CAUG_W + DV_B:(h + 1) * CAUG_W] = jnp.ones((CHUNK, LANES), BF16)
        if has_init:
            c_scr[...] = c0_ref[0]
            m_scr[...] = m0_ref[0]
        else:
            c_scr[...] = jnp.zeros_like(c_scr)
            m_scr[...] = jnp.zeros_like(m_scr)

    row = lax.broadcasted_iota(jnp.int32, (CHUNK, LANES), 0)
    lane = lax.broadcasted_iota(jnp.int32, (CHUNK, LANES), 1)
    col = lane & (CHUNK - 1)
    left = lane < CHUNK
    dirs = range(N_DIR)
    nc, a_t, iw, emt, ksc, dec_row = {}, {}, {}, {}, {}, {}
    for d in dirs:
        lo = LF_OFF + d * H_B
        mine = (lane >= lo) & (lane < lo + H_B)
        g = gt[d][...]
        gc = jnp.where(mine, _dot_f32(_masks(d)[0].astype(F32), g), 0.0)
        a = jnp.where(mine, pltpu.roll(g, LF_OFF - LI_OFF, 1), 0.0) - gc
        last = CHUNK - 1 if d == 0 else 0
        m_old = m_scr[d][0:1, :]
        mx = jnp.maximum(m_old, _scan_max(a, d))
        mxl = mx[last:last + 1, :]
        nc[d] = -mx
        a_t[d] = jnp.concatenate([a, a], axis=0).T
        iw[d] = jnp.exp(m_old - mx)
        emt[d] = jnp.exp(-(gc + mx))
        ksc[d] = jnp.exp(a - mxl)
        dec_row[d] = jnp.exp(m_old - mxl)
        m_scr[d] = jnp.broadcast_to(gc[last:last + 1, :] + mxl, (8, LANES))

    q16, ks_t = {}, {}
    for d in dirs:
        lo = LF_OFF + d * H_B
        q16[d] = rest[d][:, OFF_BQ:OFF_BQ + qw].astype(BF16)
        k = rest[d][:, OFF_BK:OFF_BK + qw]
        _store_blocks(bdk.at[d], k.astype(BF16), CHUNK, DK_B)
        ks = k * jnp.concatenate([_col_bcast(ksc[d], lo + h) for h in range(H_B)], axis=1)
        ks_t[d] = jnp.concatenate([ks[:, h * DK_B:(h + 1) * DK_B] for h in range(H_B)], axis=0).T.astype(BF16)
        for h in range(H_B):
            bdv[d, h * CHUNK:(h + 1) * CHUNK, h * CAUG_W:h * CAUG_W + DV_B] = (
                rest[d][:, OFF_BV + h * DV_B:OFF_BV + (h + 1) * DV_B].astype(BF16))
        _store_blocks(bdc.at[d], c_scr[d].astype(BF16), DK_B, CAUG_W)
    qk = {d: lax.dot_general(q16[d], bdk[d], (((1,), (1,)), ((), ())), preferred_element_type=F32) for d in dirs}
    lhs = {}
    for d in dirs:
        lo = LF_OFF + d * H_B
        incl = row >= col if d == 0 else row <= col
        log_w = []
        for p in range(H_B // PAIR):
            l0, l1 = lo + PAIR * p, lo + PAIR * p + 1
            ccol = jnp.where(left, _col_bcast(nc[d], l0), _col_bcast(nc[d], l1))
            crow = jnp.where(left[0:1], a_t[d][l0:l0 + 1, :], a_t[d][l1:l1 + 1, :])
            log_w.append(jnp.where(incl, ccol + crow, NEG))
        dw = jnp.exp(jnp.concatenate(log_w, axis=1)) * qk[d]
        iw_x = jnp.concatenate([_col_bcast(iw[d], lo + h) for h in range(H_B)], axis=1)
        lhs[d] = jnp.concatenate([(rest[d][:, OFF_BQ:OFF_BQ + qw] * iw_x).astype(BF16), dw.astype(BF16)], axis=1)
    num = {d: jnp.dot(lhs[d], jnp.concatenate([bdc[d], bdv[d]], axis=0), preferred_element_type=F32) for d in dirs}
    upd = {d: jnp.dot(ks_t[d], bdv[d], preferred_element_type=F32) for d in dirs}
    for d in dirs:
        lo = LF_OFF + d * H_B
        dec_x = jnp.concatenate([jnp.broadcast_to(dec_row[d][:, lo + h:lo + h + 1], (1, CAUG_W)) for h in range(H_B)],
                                axis=1)
        c_scr[d] = c_scr[d] * dec_x + upd[d]
        cidx = n if d == 0 else n_chunks - 1 - n
        rows = pl.ds(pl.multiple_of(cidx * CHUNK, CHUNK), CHUNK)
        for h in range(H_B):
            vs = slice(h * DV_B, (h + 1) * DV_B)
            den = jnp.maximum(jnp.abs(num[d][:, h * CAUG_W + DV_B:(h + 1) * CAUG_W]), _col_bcast(emt[d], lo + h))
            hb = jnp.concatenate([num[d][:, h * CAUG_W:h * CAUG_W + LANES] / den,
                                  num[d][:, h * CAUG_W + LANES:h * CAUG_W + DV_B] / den], axis=1)
            tot = hb + o_scr[rows, vs]
            o_scr[rows, vs] = tot
            ogate = rest[d][:, OFF_BO + h * DV_B:OFF_BO + (h + 1) * DV_B]
            yb_ref[rows, vs] = (_rms(tot) * norm_ref[...] * ogate).astype(BF16)

    if emit_state:
        @pl.when(n == n_chunks - 1)
        def _():
            for d in dirs:
                for h in range(H_B):
                    cout_ref[0, d, h] = c_scr[d, :, h * CAUG_W:h * CAUG_W + DV_B]
                    nout_ref[0, d, h] = c_scr[d, :, h * CAUG_W + DV_B:(h + 1) * CAUG_W]
            mout_ref[0] = m_scr[...]


def _mlstm(rest, gates, norm_b, c0, m0, batch, n_chunks, emit_state):
    t = n_chunks * CHUNK
    has_init = c0 is not None

    def fwd(b, n):
        return (b * n_chunks + n, 0)

    def bwd(b, n):
        return (b * n_chunks + n_chunks - 1 - n, 0)

    mspec = pl.BlockSpec((1, N_DIR, 8, LANES), lambda b, n: (b, 0, 0, 0))
    in_specs = [pl.BlockSpec((CHUNK, REST_W), fwd), pl.BlockSpec((CHUNK, REST_W), bwd),
                pl.BlockSpec((CHUNK, LANES), fwd), pl.BlockSpec((CHUNK, LANES), bwd),
                pl.BlockSpec((1, DV_B), lambda b, n: (0, 0))]
    args = [rest, rest, gates, gates, norm_b]
    if has_init:
        in_specs += [pl.BlockSpec((1, N_DIR, DK_B, MW), lambda b, n: (b, 0, 0, 0)), mspec]
        args += [c0, m0]
    out_specs = [pl.BlockSpec((t, B_W), lambda b, n: (b, 0))]
    out_shape = [jax.ShapeDtypeStruct((batch * t, B_W), BF16)]
    if emit_state:
        out_specs += [pl.BlockSpec((1, N_DIR, H_B, DK_B, DV_B), lambda b, n: (b, 0, 0, 0, 0)),
                      pl.BlockSpec((1, N_DIR, H_B, DK_B, LANES), lambda b, n: (b, 0, 0, 0, 0)), mspec]
        out_shape += [jax.ShapeDtypeStruct((batch, N_DIR, H_B, DK_B, DV_B), F32),
                      jax.ShapeDtypeStruct((batch, N_DIR, H_B, DK_B, LANES), F32),
                      jax.ShapeDtypeStruct((batch, N_DIR, 8, LANES), F32)]
    return pl.pallas_call(
        functools.partial(_mlstm_body, n_chunks=n_chunks, has_init=has_init, emit_state=emit_state),
        grid=(batch, n_chunks),
        in_specs=in_specs,
        out_specs=out_specs,
        out_shape=out_shape,
        scratch_shapes=[pltpu.VMEM((N_DIR, DK_B, MW), F32),
                        pltpu.VMEM((N_DIR, 8, LANES), F32),
                        pltpu.VMEM((t, B_W), F32),
                        pltpu.VMEM((N_DIR, H_B * CHUNK, H_B * DK_B), BF16),
                        pltpu.VMEM((N_DIR, H_B * DK_B, MW), BF16),
                        pltpu.VMEM((N_DIR, H_B * CHUNK, MW), BF16)],
        compiler_params=_params(("parallel", "arbitrary")),
        name="mlstm_scan",
    )(*args)


def _outproj_body(ya_ref, yb_ref, wa_ref, wb_ref, x_ref, mod_ref, post1_ref, pre2_ref, x1_ref, h2_ref):
    mix = (jnp.dot(ya_ref[...], wa_ref[...], preferred_element_type=F32)
           + jnp.dot(yb_ref[...], wb_ref[...], preferred_element_type=F32))
    x1 = x_ref[...] + mod_ref[0, 2:3, :] * (_rms(mix) * post1_ref[...])
    x1_ref[...] = x1
    h2 = _rms(x1) * pre2_ref[...] * (1.0 + mod_ref[0, 4:5, :]) + mod_ref[0, 3:4, :]
    h2_ref[...] = h2.astype(BF16)


def _outproj(ya, yb, w_out, x2d, mod3, mod_map, post1, pre2, tm):
    m = x2d.shape[0]
    row = lambda i: (i, 0)
    const = lambda i: (0, 0)
    return pl.pallas_call(
        _outproj_body,
        grid=(m // tm,),
        in_specs=[pl.BlockSpec((tm, A_W), row), pl.BlockSpec((tm, B_W), row),
                  pl.BlockSpec((A_W, D_MODEL), lambda i: (0, 0)),
                  pl.BlockSpec((B_W, D_MODEL), lambda i: (1, 0)),
                  pl.BlockSpec((tm, D_MODEL), row),
                  pl.BlockSpec((1, 6, D_MODEL), lambda i: (mod_map(i * tm), 0, 0)),
                  pl.BlockSpec((1, D_MODEL), const), pl.BlockSpec((1, D_MODEL), const)],
        out_specs=[pl.BlockSpec((tm, D_MODEL), row), pl.BlockSpec((tm, D_MODEL), row)],
        out_shape=[jax.ShapeDtypeStruct((m, D_MODEL), F32), jax.ShapeDtypeStruct((m, D_MODEL), BF16)],
        compiler_params=_params(("parallel",)),
        name="outproj",
    )(ya, yb, w_out, w_out, x2d, mod3, post1, pre2)


def _ffn_body(h2_ref, w1_ref, w2_ref, x1_ref, mod_ref, post2_ref, o_ref):
    kk = pl.program_id(1)
    a = jnp.maximum(jnp.dot(h2_ref[...], w1_ref[...], preferred_element_type=F32), 0.0)
    contrib = jnp.dot((a * a).astype(BF16), w2_ref[...], preferred_element_type=F32)

    @pl.when(kk == 0)
    def _():
        o_ref[...] = contrib

    @pl.when((kk > 0) & (kk < pl.num_programs(1) - 1))
    def _():
        o_ref[...] += contrib

    @pl.when(kk == pl.num_programs(1) - 1)
    def _():
        f = o_ref[...] + contrib
        o_ref[...] = x1_ref[...] + mod_ref[0, 5:6, :] * (_rms(f) * post2_ref[...])


def _ffn(h2, w1, w2, x1, mod3, mod_map, post2, tm, fc):
    m = h2.shape[0]
    return pl.pallas_call(
        _ffn_body,
        grid=(m // tm, FFN // fc),
        in_specs=[pl.BlockSpec((tm, D_MODEL), lambda i, k: (i, 0)),
                  pl.BlockSpec((D_MODEL, fc), lambda i, k: (0, k)),
                  pl.BlockSpec((fc, D_MODEL), lambda i, k: (k, 0)),
                  pl.BlockSpec((tm, D_MODEL), lambda i, k: (i, 0)),
                  pl.BlockSpec((1, 6, D_MODEL), lambda i, k: (mod_map(i * tm), 0, 0)),
                  pl.BlockSpec((1, D_MODEL), lambda i, k: (0, 0))],
        out_specs=pl.BlockSpec((tm, D_MODEL), lambda i, k: (i, 0)),
        out_shape=jax.ShapeDtypeStruct((m, D_MODEL), F32),
        compiler_params=_params(("parallel", "arbitrary")),
        name="ffn",
    )(h2, w1, w2, x1, mod3, post2)


def _block(x, mod3, mod_of_row, lp, init, seq_len, emit_state):
    bsz, t, _ = x.shape
    x2d = x.reshape(bsz * t, D_MODEL)
    n_chunks = t // CHUNK
    qkv, rest, gates = _proj(x2d, mod3, lambda i: mod_of_row(i * PROJ_TM), lp["pre1"], lp["w16"],
                             lp["w_mid"], lp["w_gate"], lp["conv_w"], lp["gate_p"], seq_len)
    s0, c0, m0 = init if init is not None else (None, None, None)
    d_out = _delta(qkv, rest, gates, lp["norm_a"], s0, bsz, n_chunks, emit_state)
    m_out = _mlstm(rest, gates, lp["norm_b"], c0, m0, bsz, n_chunks, emit_state)
    x1, h2 = _outproj(d_out[0], m_out[0], lp["w_out"], x2d, mod3, mod_of_row, lp["post1"], lp["pre2"], 256)
    y = _ffn(h2, lp["w1"], lp["w2"], x1, mod3, mod_of_row, lp["post2"], 512, 1024)
    states = None
    if emit_state:
        m_fin = jnp.stack([m_out[3][:, d, 0, LF_OFF + d * H_B:LF_OFF + (d + 1) * H_B] for d in range(N_DIR)], axis=1)
        states = (d_out[1], m_out[1], m_out[2][..., 0], m_fin)
    return y.reshape(bsz, t, D_MODEL), states


def _layer_params(l, norm_mix_pre, norm_mix_post, norm_ffn_pre, norm_ffn_post, w_in, conv_w, a_log, dt_bias,
                  norm_a, mlstm_ibias, mlstm_fbias, norm_b, w_out, w_ffn1, w_ffn2):
    w = w_in[l]
    o_ag = QKV_W
    o_aa = o_ag + A_W
    o_ab = o_aa + N_DIR * H_A
    o_bq = o_ab + N_DIR * H_A
    o_bi = o_bq + 2 * H_B * DK_B + 2 * B_W
    o_bf = o_bi + N_DIR * H_B
    n_gate = 2 * N_DIR * H_A + 2 * N_DIR * H_B
    w16 = w.astype(BF16)
    w_gate = jnp.concatenate([w16[:, o_aa:o_bq], w16[:, o_bi:o_bf + N_DIR * H_B],
                              jnp.zeros((D_MODEL, LANES - n_gate), BF16)], axis=1)

    def lane_row(vals, off):
        return jnp.zeros((LANES,), F32).at[off:off + vals.size].set(vals.reshape(-1))

    gate_p = jnp.stack([lane_row(a_log[l], G_OFF), lane_row(dt_bias[l], G_OFF),
                        lane_row(mlstm_ibias[l], LI_OFF) + lane_row(mlstm_fbias[l], LF_OFF)]
                       + [jnp.zeros((LANES,), F32)] * 5)
    row = lambda v: v[l].reshape(1, -1)
    return dict(
        pre1=row(norm_mix_pre), post1=row(norm_mix_post), pre2=row(norm_ffn_pre), post2=row(norm_ffn_post),
        w16=w16, w_mid=w16[:, o_bq:o_bi], w_gate=w_gate, gate_p=gate_p,
        conv_w=jnp.concatenate([conv_w[l].T, jnp.zeros((5, QKV_W), F32)], axis=0),
        norm_a=row(norm_a), norm_b=row(norm_b),
        w_out=w_out[l].astype(BF16), w1=w_ffn1[l].astype(BF16), w2=w_ffn2[l].astype(BF16))


def kernel(x_prompt, x_sample, state_delta, state_mlstm_C, state_mlstm_n, state_mlstm_m, c, c_ctx, w_ada, b_ada, norm_mix_pre, norm_mix_post, norm_ffn_pre, norm_ffn_post, w_in, conv_w, a_log, dt_bias, norm_a, mlstm_ibias, mlstm_fbias, norm_b, w_out, w_ffn1, w_ffn2):
    depth = w_in.shape[0]
    n_lat = x_sample.shape[0]
    t_lat = x_sample.shape[1]
    cond = jnp.concatenate([c_ctx[None, :], c, jnp.zeros((8 - 1 - n_lat, D_MODEL), F32)], axis=0)
    y_prompt, y_sample = x_prompt, x_sample
    acc = ([], [], [], [])
    for l in range(depth):
        lp = _layer_params(l, norm_mix_pre, norm_mix_post, norm_ffn_pre, norm_ffn_post, w_in, conv_w, a_log,
                           dt_bias, norm_a, mlstm_ibias, mlstm_fbias, norm_b, w_out, w_ffn1, w_ffn2)
        mod = _ada(cond, w_ada, b_ada[l].reshape(1, -1), l)
        mod3 = mod[:1 + n_lat].reshape(1 + n_lat, 6, D_MODEL)
        y_prompt, st = _block(y_prompt, mod3, lambda r: 0, lp, None, x_prompt.shape[1], True)
        for a, s in zip(acc, st):
            a.append(s)
        n_rep = jnp.broadcast_to(state_mlstm_n[:, l][..., None], state_mlstm_n[:, l].shape + (LANES,))
        c_aug0 = jnp.concatenate([state_mlstm_C[:, l], n_rep], axis=-1)
        c_aug0 = c_aug0.transpose(0, 1, 3, 2, 4).reshape(n_lat, N_DIR, DK_B, MW)
        m0 = jnp.zeros((n_lat, N_DIR, LANES), F32)
        for d in range(N_DIR):
            m0 = m0.at[:, d, LF_OFF + d * H_B:LF_OFF + (d + 1) * H_B].set(state_mlstm_m[:, l, d])
        m0 = jnp.broadcast_to(m0[:, :, None, :], (n_lat, N_DIR, 8, LANES))
        y_sample, _ = _block(y_sample, mod3, lambda r: 1 + r // t_lat, lp, (state_delta[:, l], c_aug0, m0),
                             GRID_W, False)
    return (y_prompt, y_sample) + tuple(jnp.stack(a, axis=1) for a in acc)
```

```python
import functools

import jax
import jax.numpy as jnp
from jax import lax
from jax.experimental import pallas as pl
from jax.experimental.pallas import tpu as pltpu

F32 = jnp.float32
BF16 = jnp.bfloat16

D_MODEL = 2048
N_DIR = 2
A_W = D_MODEL // 2
B_W = D_MODEL - A_W
DK_A = 128
DV_A = 128
H_A = A_W // DV_A
DV_B = 256
DK_B = DV_B // 2
H_B = B_W // DV_B
GRID_W = 64
CHUNK = 64
FFN = 4 * D_MODEL
EPS = 1e-6
LANES = 128
NEG = -1e30

QKV_W = 3 * A_W
REST_W = A_W + 2 * H_B * DK_B + 2 * B_W
OFF_AG, OFF_BQ, OFF_BK, OFF_BV, OFF_BO = 0, A_W, A_W + H_B * DK_B, A_W + 2 * H_B * DK_B, A_W + 2 * H_B * DK_B + B_W
G_OFF, BETA_OFF = 0, N_DIR * H_A
LI_OFF = 2 * N_DIR * H_A
LF_OFF = LI_OFF + N_DIR * H_B

VMEM_LIMIT = 56 * 1024 * 1024


def _sigmoid(x):
    return 1.0 / (1.0 + jnp.exp(-x))


def _softplus(x):
    return jnp.maximum(x, 0.0) + jnp.log1p(jnp.exp(-jnp.abs(x)))


def _dot(a, b):
    return jnp.dot(a.astype(BF16), b.astype(BF16), preferred_element_type=F32)


def _dot_f32(a, b):
    return jnp.dot(a, b, precision=lax.Precision.HIGHEST, preferred_element_type=F32)


def _rms(x):
    return x * lax.rsqrt(jnp.mean(x * x, axis=-1, keepdims=True) + EPS)


def _params(sem):
    return pltpu.CompilerParams(dimension_semantics=sem, vmem_limit_bytes=VMEM_LIMIT)


def _ada_body(c_ref, w_ref, b_ref, o_ref):
    c = c_ref[...]
    o_ref[...] = _dot(c * _sigmoid(c), w_ref[...]) + b_ref[...]


def _ada(c_all, w_ada, b, l):
    n = w_ada.shape[2]
    tn = 1024
    return pl.pallas_call(
        _ada_body,
        grid=(n // tn,),
        in_specs=[pl.BlockSpec(c_all.shape, lambda j: (0, 0)),
                  pl.BlockSpec((None, D_MODEL, tn), lambda j: (l, 0, j)),
                  pl.BlockSpec((1, tn), lambda j: (0, j))],
        out_specs=pl.BlockSpec((c_all.shape[0], tn), lambda j: (0, j)),
        out_shape=jax.ShapeDtypeStruct((c_all.shape[0], n), F32),
        compiler_params=_params(("arbitrary",)),
        name="ada",
    )(c_all, w_ada, b)


PROJ_TM = 1024
PROJ_RB = 256
PROJ_TN = 512
N_QKV_T = QKV_W // PROJ_TN
N_REST_T = REST_W // PROJ_TN
N_HEAD_T = (QKV_W + A_W) // PROJ_TN
N_MID_T = N_QKV_T + N_REST_T - N_HEAD_T


def _proj_body(x_ref, mod_ref, g_ref, wh_ref, wb_ref, wg_ref, cw_ref, gp_ref, qkv_ref, rest_ref, gate_ref, h_scr, *,
               seq_len):
    j = pl.program_id(1)
    blocks = [slice(rb * PROJ_RB, (rb + 1) * PROJ_RB) for rb in range(PROJ_TM // PROJ_RB)]

    def run(w_ref, epilogue, out_ref):
        for rs in blocks:
            epilogue(jnp.dot(h_scr[rs, :], w_ref[...], preferred_element_type=F32), out_ref, rs)

    def run_first():
        for rs in blocks:
            y = _rms(x_ref[rs, :]) * g_ref[...]
            h_scr[rs, :] = (y * (1.0 + mod_ref[0, 1:2, :]) + mod_ref[0, 0:1, :]).astype(BF16)
            ep_l2(DK_A ** -0.5)(jnp.dot(h_scr[rs, :], wh_ref[...], preferred_element_type=F32), qkv_ref, rs)
            ep_gate(jnp.dot(h_scr[rs, :], wg_ref[...], preferred_element_type=F32), gate_ref, rs)

    def conv_silu(acc):
        pos = lax.broadcasted_iota(jnp.int32, acc.shape, 0) & (seq_len - 1)
        prev = jnp.where(pos == 0, 0.0, pltpu.roll(acc, 1, 0))
        nxt = jnp.where(pos == seq_len - 1, 0.0, pltpu.roll(acc, acc.shape[0] - 1, 0))
        y = prev * cw_ref[0:1, :] + acc * cw_ref[1:2, :] + nxt * cw_ref[2:3, :]
        return y * _sigmoid(y)

    def ep_l2(scale):
        def f(acc, out_ref, rs):
            y = conv_silu(acc)
            for g in range(PROJ_TN // LANES):
                blk = y[:, g * LANES:(g + 1) * LANES]
                inv = lax.rsqrt(jnp.sum(blk * blk, axis=-1, keepdims=True) + EPS)
                out_ref[rs, g * LANES:(g + 1) * LANES] = blk * (inv * scale)
        return f

    def ep_map(fn):
        def f(acc, out_ref, rs):
            out_ref[rs, :] = fn(acc)
        return f

    def ep_gate(z, out_ref, rs):
        lane = lax.broadcasted_iota(jnp.int32, z.shape, 1)
        g = -jnp.exp(gp_ref[0:1, :]) * _softplus(z + gp_ref[1:2, :])
        li = z + gp_ref[2:3, :]
        out_ref[rs, :] = jnp.where(lane < BETA_OFF, g,
                                   jnp.where(lane < LI_OFF, _sigmoid(z),
                                             jnp.where(lane < LF_OFF, li,
                                                       jnp.where(lane < LF_OFF + N_DIR * H_B, -_softplus(-li), 0.0))))

    tp = A_W // PROJ_TN
    c0 = (j - N_QKV_T) * PROJ_TN
    in_rest = (j >= N_QKV_T) & (j < N_QKV_T + N_REST_T)
    pl.when(j == 0)(run_first)
    variants = [
        ((j > 0) & (j < tp), wh_ref, ep_l2(DK_A ** -0.5), qkv_ref),
        ((j >= tp) & (j < 2 * tp), wh_ref, ep_l2(1.0), qkv_ref),
        ((j >= 2 * tp) & (j < N_QKV_T), wh_ref, ep_map(conv_silu), qkv_ref),
        (in_rest & (c0 < OFF_BQ), wh_ref, ep_map(lambda a: a * _sigmoid(a)), rest_ref),
        (in_rest & (c0 >= OFF_BQ) & (c0 < OFF_BK), wb_ref, ep_map(lambda a: a * (DK_B ** -0.5)), rest_ref),
        (in_rest & (c0 >= OFF_BK) & (c0 < OFF_BO), wb_ref, ep_map(lambda a: a), rest_ref),
        (in_rest & (c0 >= OFF_BO), wb_ref, ep_map(_sigmoid), rest_ref),
    ]
    for cond, w_ref, epilogue, out_ref in variants:
        pl.when(cond)(functools.partial(run, w_ref, epilogue, out_ref))


def _proj(x2d, mod3, mod_map, pre_g, w16, w_mid, w_gate, conv_w, gate_p, seq_len):
    m = x2d.shape[0]
    tm, tn = PROJ_TM, PROJ_TN
    assert PROJ_RB % seq_len == 0 and m % tm == 0
    return pl.pallas_call(
        functools.partial(_proj_body, seq_len=seq_len),
        grid=(m // tm, N_HEAD_T + N_MID_T),
        in_specs=[pl.BlockSpec((tm, D_MODEL), lambda i, j: (i, 0)),
                  pl.BlockSpec((1, 6, D_MODEL), lambda i, j: (mod_map(i), 0, 0)),
                  pl.BlockSpec((1, D_MODEL), lambda i, j: (0, 0)),
                  pl.BlockSpec((D_MODEL, tn), lambda i, j: (0, jnp.minimum(j, N_HEAD_T - 1))),
                  pl.BlockSpec((D_MODEL, tn), lambda i, j: (0, jnp.clip(j - N_HEAD_T, 0, N_MID_T - 1))),
                  pl.BlockSpec((D_MODEL, LANES), lambda i, j: (0, 0)),
                  pl.BlockSpec((8, tn), lambda i, j: (0, jnp.minimum(j, N_QKV_T - 1))),
                  pl.BlockSpec((8, LANES), lambda i, j: (0, 0))],
        out_specs=[pl.BlockSpec((tm, tn), lambda i, j: (i, jnp.minimum(j, N_QKV_T - 1))),
                   pl.BlockSpec((tm, tn), lambda i, j: (i, jnp.clip(j - N_QKV_T, 0, N_REST_T - 1))),
                   pl.BlockSpec((tm, LANES), lambda i, j: (i, 0))],
        out_shape=[jax.ShapeDtypeStruct((m, QKV_W), F32), jax.ShapeDtypeStruct((m, REST_W), F32),
                   jax.ShapeDtypeStruct((m, LANES), F32)],
        scratch_shapes=[pltpu.VMEM((tm, D_MODEL), BF16)],
        compiler_params=_params(("parallel", "arbitrary")),
        name="proj",
    )(x2d, mod3, pre_g, w16, w_mid, w_gate, conv_w, gate_p)


def _masks(d):
    row = lax.broadcasted_iota(jnp.int32, (CHUNK, CHUNK), 0)
    col = lax.broadcasted_iota(jnp.int32, (CHUNK, CHUNK), 1)
    if d == 0:
        return row >= col, row > col, row == col
    return row <= col, row < col, row == col


GROUP = 4
N_GROUPS = H_A // GROUP
PAIR = LANES // CHUNK


def _split(x):
    hi = x.astype(BF16)
    return hi, (x - hi.astype(F32)).astype(BF16)


def _store_blocks(ref, x, blk_r, blk_c, col0=0):
    for h in range(GROUP):
        ref[h * blk_r:(h + 1) * blk_r, col0 + h * blk_c:col0 + (h + 1) * blk_c] = x[:, h * blk_c:(h + 1) * blk_c]


def _dot3(a_hi, a_lo, b_hi, b_lo):
    lhs = jnp.concatenate([a_hi, a_lo, a_hi], axis=1)
    rhs = jnp.concatenate([b_hi, b_hi, b_lo], axis=0)
    return jnp.dot(lhs, rhs, preferred_element_type=F32)


def _tri_inverse_wide(lmats, eye_w, bd_hi, bd_lo):
    def bd_dot(i, a_hi, a_lo, b_hi, b_lo):
        _store_blocks(bd_hi.at[i], b_hi, CHUNK, CHUNK)
        _store_blocks(bd_lo.at[i], b_lo, CHUNK, CHUNK)
        return _dot3(a_hi, a_lo, bd_hi[i], bd_lo[i])

    idx = range(len(lmats))
    s = [eye_w - l for l in lmats]
    p = []
    for i in idx:
        m_hi, m_lo = _split(-lmats[i])
        p.append(bd_dot(i, m_hi, m_lo, m_hi, m_lo))
    for _ in range(4):
        for i in idx:
            p_hi, p_lo = _split(p[i])
            s_hi, s_lo = _split(s[i])
            r = bd_dot(i, jnp.concatenate([p_hi, s_hi], axis=0), jnp.concatenate([p_lo, s_lo], axis=0), p_hi, p_lo)
            p[i] = r[:CHUNK]
            s[i] = s[i] + r[CHUNK:]
    for i in idx:
        p_hi, p_lo = _split(p[i])
        s_hi, s_lo = _split(s[i])
        s[i] = s[i] + bd_dot(i, s_hi, s_lo, p_hi, p_lo)
    return s


def _col_bcast(tile, c, width=LANES):
    return jnp.broadcast_to(tile[:, c:c + 1], (tile.shape[0], width))


def _delta_body(*refs, n_chunks, has_init, emit_state):
    qkv = refs[0:2]
    ag = refs[2:4]
    gt = refs[4:6]
    norm_ref = refs[6]
    pos = 7
    s0_ref = None
    if has_init:
        s0_ref = refs[pos]
        pos += 1
    ya_ref = refs[pos]
    pos += 1
    sout_ref = None
    if emit_state:
        sout_ref = refs[pos]
        pos += 1
    s_scr, o_scr, bdn_hi, bdn_lo, bdk, bduw, bds, bdv = refs[pos:pos + 8]

    n = pl.program_id(1)
    gw = GROUP * DK_A

    @pl.when((n == 0) & (pl.program_id(0) == 0))
    def _():
        for ref in (bdn_hi, bdn_lo, bdk, bduw, bds, bdv):
            ref[...] = jnp.zeros_like(ref)

    @pl.when(n == 0)
    def _():
        o_scr[...] = jnp.zeros_like(o_scr)
        for d in range(N_DIR):
            for h in range(H_A):
                blk = s0_ref[0, d, h] if has_init else jnp.zeros((DK_A, DV_A), F32)
                s_scr[d, h // GROUP, :, (h % GROUP) * DV_A:(h % GROUP + 1) * DV_A] = blk

    row = lax.broadcasted_iota(jnp.int32, (CHUNK, LANES), 0)
    lane = lax.broadcasted_iota(jnp.int32, (CHUNK, LANES), 1)
    col = lane & (CHUNK - 1)
    left = lane < CHUNK
    eye_w = jnp.concatenate([(row == col).astype(F32)] * (GROUP // PAIR), axis=1)
    probs = [(d, g) for d in range(N_DIR) for g in range(N_GROUPS)]
    gates, gc, gc_t, eg, egl, masks = {}, {}, {}, {}, {}, {}
    for d in range(N_DIR):
        masks[d] = (row >= col, row > col) if d == 0 else (row <= col, row < col)
        gates[d] = gt[d][...]
        gc[d] = _dot_f32(_masks(d)[0].astype(F32), gates[d])
        gc_t[d] = jnp.concatenate([gc[d], gc[d]], axis=0).T
        last = CHUNK - 1 if d == 0 else 0
        eg[d] = jnp.exp(gc[d])
        egl[d] = jnp.exp(gc[d][last:last + 1, :] - gc[d])

    def cols_of(d, g):
        return [G_OFF + d * H_A + g * GROUP + hl for hl in range(GROUP)]

    q16, ks, beta_xs, decays, grams = [], [], [], [], []
    for gi, (d, g) in enumerate(probs):
        cols = cols_of(d, g)
        incl, _ = masks[d]
        k = qkv[d][:, A_W + g * gw:A_W + (g + 1) * gw]
        beta_x = jnp.concatenate([_col_bcast(gates[d], BETA_OFF - G_OFF + c) for c in cols], axis=1)
        decay = []
        for p in range(GROUP // PAIR):
            c0, c1 = cols[PAIR * p], cols[PAIR * p + 1]
            gcol = jnp.where(left, _col_bcast(gc[d], c0), _col_bcast(gc[d], c1))
            grow = jnp.where(left[0:1], gc_t[d][c0:c0 + 1, :], gc_t[d][c1:c1 + 1, :])
            decay.append(jnp.exp(jnp.where(incl, gcol - grow, NEG)))
        decays.append(jnp.concatenate(decay, axis=1))
        q16.append(qkv[d][:, g * gw:(g + 1) * gw].astype(BF16))
        ks.append(k)
        beta_xs.append(beta_x)
        _store_blocks(bdk.at[gi], k.astype(BF16), CHUNK, DK_A)
    for gi, (d, g) in enumerate(probs):
        grams.append(lax.dot_general(jnp.concatenate([q16[gi], (ks[gi] * beta_xs[gi]).astype(BF16)], axis=0),
                                     bdk[gi], (((1,), (1,)), ((), ())), preferred_element_type=F32))
    attns, lmats = [], []
    for gi, (d, g) in enumerate(probs):
        strict_w = jnp.concatenate([masks[d][1]] * (GROUP // PAIR), axis=1)
        attns.append((grams[gi][:CHUNK] * decays[gi]).astype(BF16))
        lmats.append(jnp.where(strict_w, grams[gi][CHUNK:] * decays[gi], 0.0))
    ainvs = _tri_inverse_wide(lmats, eye_w, bdn_hi, bdn_lo)
    eg_xs, kd_ts = [], []
    for gi, (d, g) in enumerate(probs):
        cols = cols_of(d, g)
        v = qkv[d][:, 2 * A_W + g * gw:2 * A_W + (g + 1) * gw]
        eg_x = jnp.concatenate([_col_bcast(eg[d], c) for c in cols], axis=1)
        egl_x = jnp.concatenate([_col_bcast(egl[d], c) for c in cols], axis=1)
        eg_xs.append(eg_x)
        _store_blocks(bduw.at[gi], (v * beta_xs[gi]).astype(BF16), CHUNK, DV_A)
        _store_blocks(bduw.at[gi], (ks[gi] * (beta_xs[gi] * eg_x)).astype(BF16), CHUNK, DK_A, col0=gw)
        kd = ks[gi] * egl_x
        kd_ts.append(jnp.concatenate([kd[:, hl * DK_A:(hl + 1) * DK_A] for hl in range(GROUP)], axis=0)
                     .T.astype(BF16))
        _store_blocks(bds.at[gi], s_scr[d, g].astype(BF16), DK_A, DV_A)
    uws = []
    for gi in range(len(probs)):
        t_hi, t_lo = _split(ainvs[gi])
        uws.append(jnp.dot(jnp.concatenate([t_hi, t_lo], axis=1), jnp.concatenate([bduw[gi], bduw[gi]], axis=0),
                           preferred_element_type=F32))
    ws_qs = []
    hw = gw // 2
    for gi in range(len(probs)):
        wq = jnp.concatenate([uws[gi][:, gw:].astype(BF16), q16[gi]], axis=0)
        ws_qs.append(jnp.concatenate(
            [jnp.dot(wq[:, p * hw:(p + 1) * hw], bds[gi, p * hw:(p + 1) * hw, p * hw:(p + 1) * hw],
                     preferred_element_type=F32) for p in range(2)], axis=1))
    for gi in range(len(probs)):
        _store_blocks(bdv.at[gi], (uws[gi][:, :gw] - ws_qs[gi][:CHUNK]).astype(BF16), CHUNK, DV_A)
    rs = []
    for gi in range(len(probs)):
        rs.append(jnp.dot(jnp.concatenate([attns[gi], kd_ts[gi]], axis=0), bdv[gi], preferred_element_type=F32))
    for gi, (d, g) in enumerate(probs):
        cols = cols_of(d, g)
        last = CHUNK - 1 if d == 0 else 0
        o = ws_qs[gi][CHUNK:] * eg_xs[gi] + rs[gi][:CHUNK]
        eg_last = jnp.concatenate(
            [jnp.broadcast_to(eg[d][last:last + 1, c:c + 1], (1, DV_A)) for c in cols], axis=1)
        s_scr[d, g] = s_scr[d, g] * eg_last + rs[gi][CHUNK:]
        cidx = n if d == 0 else n_chunks - 1 - n
        rows = pl.ds(pl.multiple_of(cidx * CHUNK, CHUNK), CHUNK)
        tot = o + o_scr[rows, g * gw:(g + 1) * gw]
        o_scr[rows, g * gw:(g + 1) * gw] = tot
        for hl in range(GROUP):
            hs = slice((g * GROUP + hl) * DV_A, (g * GROUP + hl + 1) * DV_A)
            blk = tot[:, hl * DV_A:(hl + 1) * DV_A]
            ya_ref[rows, hs] = (_rms(blk) * norm_ref[...] * ag[d][:, hs]).astype(BF16)

    if emit_state:
        @pl.when(n == n_chunks - 1)
        def _():
            for d in range(N_DIR):
                for h in range(H_A):
                    sout_ref[0, d, h] = s_scr[d, h // GROUP, :, (h % GROUP) * DV_A:(h % GROUP + 1) * DV_A]


def _delta(qkv, rest, gates, norm_a, s0, batch, n_chunks, emit_state):
    t = n_chunks * CHUNK
    has_init = s0 is not None
    ng = N_DIR * N_GROUPS

    def fwd(b, n):
        return (b * n_chunks + n, 0)

    def bwd(b, n):
        return (b * n_chunks + n_chunks - 1 - n, 0)

    in_specs = [pl.BlockSpec((CHUNK, QKV_W), fwd), pl.BlockSpec((CHUNK, QKV_W), bwd),
                pl.BlockSpec((CHUNK, A_W), fwd), pl.BlockSpec((CHUNK, A_W), bwd),
                pl.BlockSpec((CHUNK, LANES), fwd), pl.BlockSpec((CHUNK, LANES), bwd),
                pl.BlockSpec((1, DV_A), lambda b, n: (0, 0))]
    args = [qkv, qkv, rest, rest, gates, gates, norm_a]
    if has_init:
        in_specs.append(pl.BlockSpec((1, N_DIR, H_A, DK_A, DV_A), lambda b, n: (b, 0, 0, 0, 0)))
        args.append(s0)
    out_specs = [pl.BlockSpec((t, A_W), lambda b, n: (b, 0))]
    out_shape = [jax.ShapeDtypeStruct((batch * t, A_W), BF16)]
    if emit_state:
        out_specs.append(pl.BlockSpec((1, N_DIR, H_A, DK_A, DV_A), lambda b, n: (b, 0, 0, 0, 0)))
        out_shape.append(jax.ShapeDtypeStruct((batch, N_DIR, H_A, DK_A, DV_A), F32))
    return pl.pallas_call(
        functools.partial(_delta_body, n_chunks=n_chunks, has_init=has_init, emit_state=emit_state),
        grid=(batch, n_chunks),
        in_specs=in_specs,
        out_specs=out_specs,
        out_shape=out_shape,
        scratch_shapes=[pltpu.VMEM((N_DIR, N_GROUPS, DK_A, GROUP * DV_A), F32), pltpu.VMEM((t, A_W), F32),
                        pltpu.VMEM((ng, GROUP * CHUNK, GROUP * CHUNK), BF16),
                        pltpu.VMEM((ng, GROUP * CHUNK, GROUP * CHUNK), BF16),
                        pltpu.VMEM((ng, GROUP * CHUNK, GROUP * DK_A), BF16),
                        pltpu.VMEM((ng, GROUP * CHUNK, 2 * GROUP * DK_A), BF16),
                        pltpu.VMEM((ng, GROUP * DK_A, GROUP * DV_A), BF16),
                        pltpu.VMEM((ng, GROUP * CHUNK, GROUP * DV_A), BF16)],
        compiler_params=_params(("arbitrary", "arbitrary")),
        name="delta_scan",
    )(*args)


CAUG_W = DV_B + LANES
MW = H_B * CAUG_W


def _scan_max(x, d):
    row = lax.broadcasted_iota(jnp.int32, x.shape, 0)
    s = 1
    while s < CHUNK:
        if d == 0:
            shifted = jnp.where(row >= s, pltpu.roll(x, s, 0), NEG)
        else:
            shifted = jnp.where(row < CHUNK - s, pltpu.roll(x, CHUNK - s, 0), NEG)
        x = jnp.maximum(x, shifted)
        s *= 2
    return x


def _mlstm_body(*refs, n_chunks, has_init, emit_state):
    rest = refs[0:2]
    gt = refs[2:4]
    norm_ref = refs[4]
    pos = 5
    c0_ref = m0_ref = None
    if has_init:
        c0_ref, m0_ref = refs[pos], refs[pos + 1]
        pos += 2
    yb_ref = refs[pos]
    pos += 1
    cout_ref = nout_ref = mout_ref = None
    if emit_state:
        cout_ref, nout_ref, mout_ref = refs[pos:pos + 3]
        pos += 3
    c_scr, m_scr, o_scr, bdk, bdc, bdv = refs[pos:pos + 6]

    n = pl.program_id(1)
    qw = H_B * DK_B

    @pl.when((n == 0) & (pl.program_id(0) == 0))
    def _():
        for ref in (bdk, bdc, bdv):
            ref[...] = jnp.zeros_like(ref)
        for d in range(N_DIR):
            for h in range(H_B):
                bdv[d, h * CHUNK:(h + 1) * CHUNK, h * CAUG_W + DV_B:(h + 1) * CAUG_W] = jnp.ones((CHUNK, LANES), BF16)

    @pl.when(n == 0)
    def _():
        o_scr[...] = jnp.zeros_like(o_scr)
        if has_init:
            c_scr[...] = c0_ref[0]
            m_scr[...] = m0_ref[0]
        else:
            c_scr[...] = jnp.zeros_like(c_scr)
            m_scr[...] = jnp.zeros_like(m_scr)

    row = lax.broadcasted_iota(jnp.int32, (CHUNK, LANES), 0)
    lane = lax.broadcasted_iota(jnp.int32, (CHUNK, LANES), 1)
    col = lane & (CHUNK - 1)
    left = lane < CHUNK
    dirs = range(N_DIR)
    nc, a_t, iw, emt, ksc, dec_row = {}, {}, {}, {}, {}, {}
    for d in dirs:
        lo = LF_OFF + d * H_B
        mine = (lane >= lo) & (lane < lo + H_B)
        g = gt[d][...]
        gc = jnp.where(mine, _dot_f32(_masks(d)[0].astype(F32), g), 0.0)
        a = jnp.where(mine, pltpu.roll(g, LF_OFF - LI_OFF, 1), 0.0) - gc
        last = CHUNK - 1 if d == 0 else 0
        m_old = m_scr[d][0:1, :]
        mx = jnp.maximum(m_old, _scan_max(a, d))
        mxl = mx[last:last + 1, :]
        nc[d] = -mx
        a_t[d] = jnp.concatenate([a, a], axis=0).T
        iw[d] = jnp.exp(m_old - mx)
        emt[d] = jnp.exp(-(gc + mx))
        ksc[d] = jnp.exp(a - mxl)
        dec_row[d] = jnp.exp(m_old - mxl)
        m_scr[d] = jnp.broadcast_to(gc[last:last + 1, :] + mxl, (8, LANES))

    q16, ks_t = {}, {}
    for d in dirs:
        lo = LF_OFF + d * H_B
        q16[d] = rest[d][:, OFF_BQ:OFF_BQ + qw].astype(BF16)
        k = rest[d][:, OFF_BK:OFF_BK + qw]
        _store_blocks(bdk.at[d], k.astype(BF16), CHUNK, DK_B)
        ks = k * jnp.concatenate([_col_bcast(ksc[d], lo + h) for h in range(H_B)], axis=1)
        ks_t[d] = jnp.concatenate([ks[:, h * DK_B:(h + 1) * DK_B] for h in range(H_B)], axis=0).T.astype(BF16)
        for h in range(H_B):
            bdv[d, h * CHUNK:(h + 1) * CHUNK, h * CAUG_W:h * CAUG_W + DV_B] = (
                rest[d][:, OFF_BV + h * DV_B:OFF_BV + (h + 1) * DV_B].astype(BF16))
        _store_blocks(bdc.at[d], c_scr[d].astype(BF16), DK_B, CAUG_W)
    qk = {d: lax.dot_general(q16[d], bdk[d], (((1,), (1,)), ((), ())), preferred_element_type=F32) for d in dirs}
    lhs = {}
    for d in dirs:
        lo = LF_OFF + d * H_B
        incl = row >= col if d == 0 else row <= col
        log_w = []
        for p in range(H_B // PAIR):
            l0, l1 = lo + PAIR * p, lo + PAIR * p + 1
            ccol = jnp.where(left, _col_bcast(nc[d], l0), _col_bcast(nc[d], l1))
            crow = jnp.where(left[0:1], a_t[d][l0:l0 + 1, :], a_t[d][l1:l1 + 1, :])
            log_w.append(jnp.where(incl, ccol + crow, NEG))
        dw = jnp.exp(jnp.concatenate(log_w, axis=1)) * qk[d]
        iw_x = jnp.concatenate([_col_bcast(iw[d], lo + h) for h in range(H_B)], axis=1)
        lhs[d] = ((rest[d][:, OFF_BQ:OFF_BQ + qw] * iw_x).astype(BF16), dw.astype(BF16))
    num = {}
    for d in dirs:
        parts = []
        for p in range(H_B // PAIR):
            ql, dl, cl = slice(p * PAIR * DK_B, (p + 1) * PAIR * DK_B), slice(p * LANES, (p + 1) * LANES), \
                slice(p * PAIR * CAUG_W, (p + 1) * PAIR * CAUG_W)
            parts.append(jnp.dot(jnp.concatenate([lhs[d][0][:, ql], lhs[d][1][:, dl]], axis=1),
                                 jnp.concatenate([bdc[d, ql, cl], bdv[d, dl, cl]], axis=0),
                                 preferred_element_type=F32))
        num[d] = jnp.concatenate(parts, axis=1)
    upd = {d: jnp.dot(ks_t[d], bdv[d], preferred_element_type=F32) for d in dirs}
    for d in dirs:
        lo = LF_OFF + d * H_B
        dec_x = jnp.concatenate([jnp.broadcast_to(dec_row[d][:, lo + h:lo + h + 1], (1, CAUG_W)) for h in range(H_B)],
                                axis=1)
        c_scr[d] = c_scr[d] * dec_x + upd[d]
        cidx = n if d == 0 else n_chunks - 1 - n
        rows = pl.ds(pl.multiple_of(cidx * CHUNK, CHUNK), CHUNK)
        for h in range(H_B):
            vs = slice(h * DV_B, (h + 1) * DV_B)
            den = jnp.maximum(jnp.abs(num[d][:, h * CAUG_W + DV_B:(h + 1) * CAUG_W]), _col_bcast(emt[d], lo + h))
            hb = jnp.concatenate([num[d][:, h * CAUG_W:h * CAUG_W + LANES] / den,
                                  num[d][:, h * CAUG_W + LANES:h * CAUG_W + DV_B] / den], axis=1)
            tot = hb + o_scr[rows, vs]
            o_scr[rows, vs] = tot
            ogate = rest[d][:, OFF_BO + h * DV_B:OFF_BO + (h + 1) * DV_B]
            yb_ref[rows, vs] = (_rms(tot) * norm_ref[...] * ogate).astype(BF16)

    if emit_state:
        @pl.when(n == n_chunks - 1)
        def _():
            for d in dirs:
                for h in range(H_B):
                    cout_ref[0, d, h] = c_scr[d, :, h * CAUG_W:h * CAUG_W + DV_B]
                    nout_ref[0, d, h] = c_scr[d, :, h * CAUG_W + DV_B:(h + 1) * CAUG_W]
            mout_ref[0] = m_scr[...]


def _mlstm(rest, gates, norm_b, c0, m0, batch, n_chunks, emit_state):
    t = n_chunks * CHUNK
    has_init = c0 is not None

    def fwd(b, n):
        return (b * n_chunks + n, 0)

    def bwd(b, n):
        return (b * n_chunks + n_chunks - 1 - n, 0)

    mspec = pl.BlockSpec((1, N_DIR, 8, LANES), lambda b, n: (b, 0, 0, 0))
    in_specs = [pl.BlockSpec((CHUNK, REST_W), fwd), pl.BlockSpec((CHUNK, REST_W), bwd),
                pl.BlockSpec((CHUNK, LANES), fwd), pl.BlockSpec((CHUNK, LANES), bwd),
                pl.BlockSpec((1, DV_B), lambda b, n: (0, 0))]
    args = [rest, rest, gates, gates, norm_b]
    if has_init:
        in_specs += [pl.BlockSpec((1, N_DIR, DK_B, MW), lambda b, n: (b, 0, 0, 0)), mspec]
        args += [c0, m0]
    out_specs = [pl.BlockSpec((t, B_W), lambda b, n: (b, 0))]
    out_shape = [jax.ShapeDtypeStruct((batch * t, B_W), BF16)]
    if emit_state:
        out_specs += [pl.BlockSpec((1, N_DIR, H_B, DK_B, DV_B), lambda b, n: (b, 0, 0, 0, 0)),
                      pl.BlockSpec((1, N_DIR, H_B, DK_B, LANES), lambda b, n: (b, 0, 0, 0, 0)), mspec]
        out_shape += [jax.ShapeDtypeStruct((batch, N_DIR, H_B, DK_B, DV_B), F32),
                      jax.ShapeDtypeStruct((batch, N_DIR, H_B, DK_B, LANES), F32),
                      jax.ShapeDtypeStruct((batch, N_DIR, 8, LANES), F32)]
    return pl.pallas_call(
        functools.partial(_mlstm_body, n_chunks=n_chunks, has_init=has_init, emit_state=emit_state),
        grid=(batch, n_chunks),
        in_specs=in_specs,
        out_specs=out_specs,
        out_shape=out_shape,
        scratch_shapes=[pltpu.VMEM((N_DIR, DK_B, MW), F32),
                        pltpu.VMEM((N_DIR, 8, LANES), F32),
                        pltpu.VMEM((t, B_W), F32),
                        pltpu.VMEM((N_DIR, H_B * CHUNK, H_B * DK_B), BF16),
                        pltpu.VMEM((N_DIR, H_B * DK_B, MW), BF16),
                        pltpu.VMEM((N_DIR, H_B * CHUNK, MW), BF16)],
        compiler_params=_params(("arbitrary", "arbitrary")),
        name="mlstm_scan",
    )(*args)


OUT_RB = 256


def _outproj_body(ya_ref, yb_ref, wa_ref, wb_ref, x_ref, mod_ref, post1_ref, pre2_ref, x1_ref, h2_ref):
    for rb in range(x_ref.shape[0] // OUT_RB):
        rs = slice(rb * OUT_RB, (rb + 1) * OUT_RB)
        mix = (jnp.dot(ya_ref[rs, :], wa_ref[...], preferred_element_type=F32)
               + jnp.dot(yb_ref[rs, :], wb_ref[...], preferred_element_type=F32))
        x1 = x_ref[rs, :] + mod_ref[0, 2:3, :] * (_rms(mix) * post1_ref[...])
        x1_ref[rs, :] = x1
        h2 = _rms(x1) * pre2_ref[...] * (1.0 + mod_ref[0, 4:5, :]) + mod_ref[0, 3:4, :]
        h2_ref[rs, :] = h2.astype(BF16)


def _outproj(ya, yb, w_out, x2d, mod3, mod_map, post1, pre2, tm):
    m = x2d.shape[0]
    row = lambda i: (i, 0)
    const = lambda i: (0, 0)
    return pl.pallas_call(
        _outproj_body,
        grid=(m // tm,),
        in_specs=[pl.BlockSpec((tm, A_W), row), pl.BlockSpec((tm, B_W), row),
                  pl.BlockSpec((A_W, D_MODEL), lambda i: (0, 0)),
                  pl.BlockSpec((B_W, D_MODEL), lambda i: (1, 0)),
                  pl.BlockSpec((tm, D_MODEL), row),
                  pl.BlockSpec((1, 6, D_MODEL), lambda i: (mod_map(i * tm), 0, 0)),
                  pl.BlockSpec((1, D_MODEL), const), pl.BlockSpec((1, D_MODEL), const)],
        out_specs=[pl.BlockSpec((tm, D_MODEL), row), pl.BlockSpec((tm, D_MODEL), row)],
        out_shape=[jax.ShapeDtypeStruct((m, D_MODEL), F32), jax.ShapeDtypeStruct((m, D_MODEL), BF16)],
        compiler_params=_params(("parallel",)),
        name="outproj",
    )(ya, yb, w_out, w_out, x2d, mod3, post1, pre2)


def _ffn_body(h2_ref, w1_ref, w2_ref, x1_ref, mod_ref, post2_ref, o_ref):
    kk = pl.program_id(1)
    a = jnp.maximum(jnp.dot(h2_ref[...], w1_ref[...], preferred_element_type=F32), 0.0)
    contrib = jnp.dot((a * a).astype(BF16), w2_ref[...], preferred_element_type=F32)

    @pl.when(kk == 0)
    def _():
        o_ref[...] = contrib

    @pl.when((kk > 0) & (kk < pl.num_programs(1) - 1))
    def _():
        o_ref[...] += contrib

    @pl.when(kk == pl.num_programs(1) - 1)
    def _():
        f = o_ref[...] + contrib
        o_ref[...] = x1_ref[...] + mod_ref[0, 5:6, :] * (_rms(f) * post2_ref[...])


def _ffn(h2, w1, w2, x1, mod3, mod_map, post2, tm, fc):
    m = h2.shape[0]
    return pl.pallas_call(
        _ffn_body,
        grid=(m // tm, FFN // fc),
        in_specs=[pl.BlockSpec((tm, D_MODEL), lambda i, k: (i, 0)),
                  pl.BlockSpec((D_MODEL, fc), lambda i, k: (0, k)),
                  pl.BlockSpec((fc, D_MODEL), lambda i, k: (k, 0)),
                  pl.BlockSpec((tm, D_MODEL), lambda i, k: (i, 0)),
                  pl.BlockSpec((1, 6, D_MODEL), lambda i, k: (mod_map(i * tm), 0, 0)),
                  pl.BlockSpec((1, D_MODEL), lambda i, k: (0, 0))],
        out_specs=pl.BlockSpec((tm, D_MODEL), lambda i, k: (i, 0)),
        out_shape=jax.ShapeDtypeStruct((m, D_MODEL), F32),
        compiler_params=_params(("parallel", "arbitrary")),
        name="ffn",
    )(h2, w1, w2, x1, mod3, post2)


def _block(x, mod3, mod_of_row, lp, init, seq_len, emit_state):
    bsz, t, _ = x.shape
    x2d = x.reshape(bsz * t, D_MODEL)
    n_chunks = t // CHUNK
    qkv, rest, gates = _proj(x2d, mod3, lambda i: mod_of_row(i * PROJ_TM), lp["pre1"], lp["w16"],
                             lp["w_mid"], lp["w_gate"], lp["conv_w"], lp["gate_p"], seq_len)
    s0, c0, m0 = init if init is not None else (None, None, None)
    d_out = _delta(qkv, rest, gates, lp["norm_a"], s0, bsz, n_chunks, emit_state)
    m_out = _mlstm(rest, gates, lp["norm_b"], c0, m0, bsz, n_chunks, emit_state)
    x1, h2 = _outproj(d_out[0], m_out[0], lp["w_out"], x2d, mod3, mod_of_row, lp["post1"], lp["pre2"], 512)
    y = _ffn(h2, lp["w1"], lp["w2"], x1, mod3, mod_of_row, lp["post2"], 512, 1024)
    states = None
    if emit_state:
        m_fin = jnp.stack([m_out[3][:, d, 0, LF_OFF + d * H_B:LF_OFF + (d + 1) * H_B] for d in range(N_DIR)], axis=1)
        states = (d_out[1], m_out[1], m_out[2][..., 0], m_fin)
    return y.reshape(bsz, t, D_MODEL), states


def _layer_params(l, norm_mix_pre, norm_mix_post, norm_ffn_pre, norm_ffn_post, w_in, conv_w, a_log, dt_bias,
                  norm_a, mlstm_ibias, mlstm_fbias, norm_b, w_out, w_ffn1, w_ffn2):
    w = w_in[l]
    o_ag = QKV_W
    o_aa = o_ag + A_W
    o_ab = o_aa + N_DIR * H_A
    o_bq = o_ab + N_DIR * H_A
    o_bi = o_bq + 2 * H_B * DK_B + 2 * B_W
    o_bf = o_bi + N_DIR * H_B
    n_gate = 2 * N_DIR * H_A + 2 * N_DIR * H_B
    w16 = w.astype(BF16)
    w_gate = jnp.concatenate([w16[:, o_aa:o_bq], w16[:, o_bi:o_bf + N_DIR * H_B],
                              jnp.zeros((D_MODEL, LANES - n_gate), BF16)], axis=1)

    def lane_row(vals, off):
        return jnp.zeros((LANES,), F32).at[off:off + vals.size].set(vals.reshape(-1))

    gate_p = jnp.stack([lane_row(a_log[l], G_OFF), lane_row(dt_bias[l], G_OFF),
                        lane_row(mlstm_ibias[l], LI_OFF) + lane_row(mlstm_fbias[l], LF_OFF)]
                       + [jnp.zeros((LANES,), F32)] * 5)
    row = lambda v: v[l].reshape(1, -1)
    return dict(
        pre1=row(norm_mix_pre), post1=row(norm_mix_post), pre2=row(norm_ffn_pre), post2=row(norm_ffn_post),
        w16=w16, w_mid=w16[:, o_bq:o_bi], w_gate=w_gate, gate_p=gate_p,
        conv_w=jnp.concatenate([conv_w[l].T, jnp.zeros((5, QKV_W), F32)], axis=0),
        norm_a=row(norm_a), norm_b=row(norm_b),
        w_out=w_out[l].astype(BF16), w1=w_ffn1[l].astype(BF16), w2=w_ffn2[l].astype(BF16))


def kernel(x_prompt, x_sample, state_delta, state_mlstm_C, state_mlstm_n, state_mlstm_m, c, c_ctx, w_ada, b_ada, norm_mix_pre, norm_mix_post, norm_ffn_pre, norm_ffn_post, w_in, conv_w, a_log, dt_bias, norm_a, mlstm_ibias, mlstm_fbias, norm_b, w_out, w_ffn1, w_ffn2):
    depth = w_in.shape[0]
    n_lat = x_sample.shape[0]
    t_lat = x_sample.shape[1]
    cond = jnp.concatenate([c_ctx[None, :], c, jnp.zeros((8 - 1 - n_lat, D_MODEL), F32)], axis=0)
    y_prompt, y_sample = x_prompt, x_sample
    acc = ([], [], [], [])
    for l in range(depth):
        lp = _layer_params(l, norm_mix_pre, norm_mix_post, norm_ffn_pre, norm_ffn_post, w_in, conv_w, a_log,
                           dt_bias, norm_a, mlstm_ibias, mlstm_fbias, norm_b, w_out, w_ffn1, w_ffn2)
        mod = _ada(cond, w_ada, b_ada[l].reshape(1, -1), l)
        mod3 = mod[:1 + n_lat].reshape(1 + n_lat, 6, D_MODEL)
        y_prompt, st = _block(y_prompt, mod3, lambda r: 0, lp, None, x_prompt.shape[1], True)
        for a, s in zip(acc, st):
            a.append(s)
        n_rep = jnp.broadcast_to(state_mlstm_n[:, l][..., None], state_mlstm_n[:, l].shape + (LANES,))
        c_aug0 = jnp.concatenate([state_mlstm_C[:, l], n_rep], axis=-1)
        c_aug0 = c_aug0.transpose(0, 1, 3, 2, 4).reshape(n_lat, N_DIR, DK_B, MW)
        m0 = jnp.zeros((n_lat, N_DIR, LANES), F32)
        for d in range(N_DIR):
            m0 = m0.at[:, d, LF_OFF + d * H_B:LF_OFF + (d + 1) * H_B].set(state_mlstm_m[:, l, d])
        m0 = jnp.broadcast_to(m0[:, :, None, :], (n_lat, N_DIR, 8, LANES))
        y_sample, _ = _block(y_sample, mod3, lambda r: 1 + r // t_lat, lp, (state_delta[:, l], c_aug0, m0),
                             GRID_W, False)
    return (y_prompt, y_sample) + tuple(jnp.stack(a, axis=1) for a in acc)
```

```python
import functools

import jax
import jax.numpy as jnp
from jax import lax
from jax.experimental import pallas as pl
from jax.experimental.pallas import tpu as pltpu

F32 = jnp.float32
BF16 = jnp.bfloat16

D_MODEL = 2048
N_DIR = 2
A_W = D_MODEL // 2
B_W = D_MODEL - A_W
DK_A = 128
DV_A = 128
H_A = A_W // DV_A
DV_B = 256
DK_B = DV_B // 2
H_B = B_W // DV_B
GRID_W = 64
CHUNK = 64
FFN = 4 * D_MODEL
EPS = 1e-6
LANES = 128
NEG = -1e30

QKV_W = 3 * A_W
REST_W = A_W + 2 * H_B * DK_B + 2 * B_W
OFF_AG, OFF_BQ, OFF_BK, OFF_BV, OFF_BO = 0, A_W, A_W + H_B * DK_B, A_W + 2 * H_B * DK_B, A_W + 2 * H_B * DK_B + B_W
G_OFF, BETA_OFF = 0, N_DIR * H_A
LI_OFF = 2 * N_DIR * H_A
LF_OFF = LI_OFF + N_DIR * H_B

VMEM_LIMIT = 56 * 1024 * 1024


def _sigmoid(x):
    return 1.0 / (1.0 + jnp.exp(-x))


def _softplus(x):
    return jnp.maximum(x, 0.0) + jnp.log1p(jnp.exp(-jnp.abs(x)))


def _dot(a, b):
    return jnp.dot(a.astype(BF16), b.astype(BF16), preferred_element_type=F32)


def _dot_f32(a, b):
    return jnp.dot(a, b, precision=lax.Precision.HIGHEST, preferred_element_type=F32)


def _rms(x):
    return x * lax.rsqrt(jnp.mean(x * x, axis=-1, keepdims=True) + EPS)


def _params(sem):
    return pltpu.CompilerParams(dimension_semantics=sem, vmem_limit_bytes=VMEM_LIMIT)


def _ada_body(c_ref, w_ref, b_ref, o_ref):
    c = c_ref[...]
    o_ref[...] = _dot(c * _sigmoid(c), w_ref[...]) + b_ref[...]


def _ada(c_all, w_ada, b, l):
    n = w_ada.shape[2]
    tn = 1024
    return pl.pallas_call(
        _ada_body,
        grid=(n // tn,),
        in_specs=[pl.BlockSpec(c_all.shape, lambda j: (0, 0)),
                  pl.BlockSpec((None, D_MODEL, tn), lambda j: (l, 0, j)),
                  pl.BlockSpec((1, tn), lambda j: (0, j))],
        out_specs=pl.BlockSpec((c_all.shape[0], tn), lambda j: (0, j)),
        out_shape=jax.ShapeDtypeStruct((c_all.shape[0], n), F32),
        compiler_params=_params(("arbitrary",)),
        name="ada",
    )(c_all, w_ada, b)


PROJ_TM = 1024
PROJ_RB = 256
PROJ_TN = 512
N_QKV_T = QKV_W // PROJ_TN
N_REST_T = REST_W // PROJ_TN
N_HEAD_T = (QKV_W + A_W) // PROJ_TN
N_MID_T = N_QKV_T + N_REST_T - N_HEAD_T


def _proj_body(x_ref, mod_ref, g_ref, wh_ref, wb_ref, wg_ref, cw_ref, gp_ref, qkv_ref, rest_ref, gate_ref, h_scr, *,
               seq_len):
    j = pl.program_id(1)
    blocks = [slice(rb * PROJ_RB, (rb + 1) * PROJ_RB) for rb in range(PROJ_TM // PROJ_RB)]

    def run(w_ref, epilogue, out_ref):
        for rs in blocks:
            epilogue(jnp.dot(h_scr[rs, :], w_ref[...], preferred_element_type=F32), out_ref, rs)

    def run_first():
        for rs in blocks:
            y = _rms(x_ref[rs, :]) * g_ref[...]
            h_scr[rs, :] = (y * (1.0 + mod_ref[0, 1:2, :]) + mod_ref[0, 0:1, :]).astype(BF16)
            ep_l2(DK_A ** -0.5)(jnp.dot(h_scr[rs, :], wh_ref[...], preferred_element_type=F32), qkv_ref, rs)
            ep_gate(jnp.dot(h_scr[rs, :], wg_ref[...], preferred_element_type=F32), gate_ref, rs)

    def conv_silu(acc):
        pos = lax.broadcasted_iota(jnp.int32, acc.shape, 0) & (seq_len - 1)
        prev = jnp.where(pos == 0, 0.0, pltpu.roll(acc, 1, 0))
        nxt = jnp.where(pos == seq_len - 1, 0.0, pltpu.roll(acc, acc.shape[0] - 1, 0))
        y = prev * cw_ref[0:1, :] + acc * cw_ref[1:2, :] + nxt * cw_ref[2:3, :]
        return y * _sigmoid(y)

    def ep_l2(scale):
        def f(acc, out_ref, rs):
            y = conv_silu(acc)
            for g in range(PROJ_TN // LANES):
                blk = y[:, g * LANES:(g + 1) * LANES]
                inv = lax.rsqrt(jnp.sum(blk * blk, axis=-1, keepdims=True) + EPS)
                out_ref[rs, g * LANES:(g + 1) * LANES] = blk * (inv * scale)
        return f

    def ep_map(fn):
        def f(acc, out_ref, rs):
            out_ref[rs, :] = fn(acc)
        return f

    def ep_gate(z, out_ref, rs):
        lane = lax.broadcasted_iota(jnp.int32, z.shape, 1)
        g = -jnp.exp(gp_ref[0:1, :]) * _softplus(z + gp_ref[1:2, :])
        li = z + gp_ref[2:3, :]
        out_ref[rs, :] = jnp.where(lane < BETA_OFF, g,
                                   jnp.where(lane < LI_OFF, _sigmoid(z),
                                             jnp.where(lane < LF_OFF, li,
                                                       jnp.where(lane < LF_OFF + N_DIR * H_B, -_softplus(-li), 0.0))))

    tp = A_W // PROJ_TN
    c0 = (j - N_QKV_T) * PROJ_TN
    in_rest = (j >= N_QKV_T) & (j < N_QKV_T + N_REST_T)
    pl.when(j == 0)(run_first)
    variants = [
        ((j > 0) & (j < tp), wh_ref, ep_l2(DK_A ** -0.5), qkv_ref),
        ((j >= tp) & (j < 2 * tp), wh_ref, ep_l2(1.0), qkv_ref),
        ((j >= 2 * tp) & (j < N_QKV_T), wh_ref, ep_map(conv_silu), qkv_ref),
        (in_rest & (c0 < OFF_BQ), wh_ref, ep_map(lambda a: a * _sigmoid(a)), rest_ref),
        (in_rest & (c0 >= OFF_BQ) & (c0 < OFF_BK), wb_ref, ep_map(lambda a: a * (DK_B ** -0.5)), rest_ref),
        (in_rest & (c0 >= OFF_BK) & (c0 < OFF_BO), wb_ref, ep_map(lambda a: a), rest_ref),
        (in_rest & (c0 >= OFF_BO), wb_ref, ep_map(_sigmoid), rest_ref),
    ]
    for cond, w_ref, epilogue, out_ref in variants:
        pl.when(cond)(functools.partial(run, w_ref, epilogue, out_ref))


def _proj(x2d, mod3, mod_map, pre_g, w16, w_mid, w_gate, conv_w, gate_p, seq_len):
    m = x2d.shape[0]
    tm, tn = PROJ_TM, PROJ_TN
    assert PROJ_RB % seq_len == 0 and m % tm == 0
    return pl.pallas_call(
        functools.partial(_proj_body, seq_len=seq_len),
        grid=(m // tm, N_HEAD_T + N_MID_T),
        in_specs=[pl.BlockSpec((tm, D_MODEL), lambda i, j: (i, 0)),
                  pl.BlockSpec((1, 6, D_MODEL), lambda i, j: (mod_map(i), 0, 0)),
                  pl.BlockSpec((1, D_MODEL), lambda i, j: (0, 0)),
                  pl.BlockSpec((D_MODEL, tn), lambda i, j: (0, jnp.minimum(j, N_HEAD_T - 1))),
                  pl.BlockSpec((D_MODEL, tn), lambda i, j: (0, jnp.clip(j - N_HEAD_T, 0, N_MID_T - 1))),
                  pl.BlockSpec((D_MODEL, LANES), lambda i, j: (0, 0)),
                  pl.BlockSpec((8, tn), lambda i, j: (0, jnp.minimum(j, N_QKV_T - 1))),
                  pl.BlockSpec((8, LANES), lambda i, j: (0, 0))],
        out_specs=[pl.BlockSpec((tm, tn), lambda i, j: (i, jnp.minimum(j, N_QKV_T - 1))),
                   pl.BlockSpec((tm, tn), lambda i, j: (i, jnp.clip(j - N_QKV_T, 0, N_REST_T - 1))),
                   pl.BlockSpec((tm, LANES), lambda i, j: (i, 0))],
        out_shape=[jax.ShapeDtypeStruct((m, QKV_W), F32), jax.ShapeDtypeStruct((m, REST_W), F32),
                   jax.ShapeDtypeStruct((m, LANES), F32)],
        scratch_shapes=[pltpu.VMEM((tm, D_MODEL), BF16)],
        compiler_params=_params(("parallel", "arbitrary")),
        name="proj",
    )(x2d, mod3, pre_g, w16, w_mid, w_gate, conv_w, gate_p)


def _masks(d):
    row = lax.broadcasted_iota(jnp.int32, (CHUNK, CHUNK), 0)
    col = lax.broadcasted_iota(jnp.int32, (CHUNK, CHUNK), 1)
    if d == 0:
        return row >= col, row > col, row == col
    return row <= col, row < col, row == col


GROUP = 4
N_GROUPS = H_A // GROUP
PAIR = LANES // CHUNK


def _split(x):
    hi = x.astype(BF16)
    return hi, (x - hi.astype(F32)).astype(BF16)


def _store_blocks(ref, x, blk_r, blk_c, col0=0):
    for h in range(GROUP):
        ref[h * blk_r:(h + 1) * blk_r, col0 + h * blk_c:col0 + (h + 1) * blk_c] = x[:, h * blk_c:(h + 1) * blk_c]


def _dot3(a_hi, a_lo, b_hi, b_lo):
    lhs = jnp.concatenate([a_hi, a_lo, a_hi], axis=1)
    rhs = jnp.concatenate([b_hi, b_hi, b_lo], axis=0)
    return jnp.dot(lhs, rhs, preferred_element_type=F32)


def _tri_inverse_wide(lmats, eye_w, bd_hi, bd_lo):
    def bd_dot(i, a_hi, a_lo, b_hi, b_lo):
        _store_blocks(bd_hi.at[i], b_hi, CHUNK, CHUNK)
        _store_blocks(bd_lo.at[i], b_lo, CHUNK, CHUNK)
        return _dot3(a_hi, a_lo, bd_hi[i], bd_lo[i])

    idx = range(len(lmats))
    s = [eye_w - l for l in lmats]
    p = []
    for i in idx:
        m_hi, m_lo = _split(-lmats[i])
        p.append(bd_dot(i, m_hi, m_lo, m_hi, m_lo))
    for _ in range(4):
        for i in idx:
            p_hi, p_lo = _split(p[i])
            s_hi, s_lo = _split(s[i])
            r = bd_dot(i, jnp.concatenate([p_hi, s_hi], axis=0), jnp.concatenate([p_lo, s_lo], axis=0), p_hi, p_lo)
            p[i] = r[:CHUNK]
            s[i] = s[i] + r[CHUNK:]
    for i in idx:
        p_hi, p_lo = _split(p[i])
        s_hi, s_lo = _split(s[i])
        s[i] = s[i] + bd_dot(i, s_hi, s_lo, p_hi, p_lo)
    return s


def _col_bcast(tile, c, width=LANES):
    return jnp.broadcast_to(tile[:, c:c + 1], (tile.shape[0], width))


def _delta_body(*refs, n_chunks, bb, has_init, emit_state):
    qkv = refs[0:2]
    ag = refs[2:4]
    gt = refs[4:6]
    norm_ref = refs[6]
    pos = 7
    s0_ref = None
    if has_init:
        s0_ref = refs[pos]
        pos += 1
    ya_ref = refs[pos]
    pos += 1
    sout_ref = None
    if emit_state:
        sout_ref = refs[pos]
        pos += 1
    s_scr, o_scr, bdn_hi, bdn_lo, bdk, bduw, bds, bdv = refs[pos:pos + 8]

    n = pl.program_id(1)
    gw = GROUP * DK_A

    @pl.when((n == 0) & (pl.program_id(0) == 0))
    def _():
        for ref in (bdn_hi, bdn_lo, bdk, bduw, bds, bdv):
            ref[...] = jnp.zeros_like(ref)

    @pl.when(n == 0)
    def _():
        o_scr[...] = jnp.zeros_like(o_scr)
        for bi in range(bb):
            for d in range(N_DIR):
                for h in range(H_A):
                    blk = s0_ref[bi, d, h] if has_init else jnp.zeros((DK_A, DV_A), F32)
                    s_scr[bi, d, h // GROUP, :, (h % GROUP) * DV_A:(h % GROUP + 1) * DV_A] = blk

    row = lax.broadcasted_iota(jnp.int32, (CHUNK, LANES), 0)
    lane = lax.broadcasted_iota(jnp.int32, (CHUNK, LANES), 1)
    col = lane & (CHUNK - 1)
    left = lane < CHUNK
    eye_w = jnp.concatenate([(row == col).astype(F32)] * (GROUP // PAIR), axis=1)
    probs = [(bi, d, g) for bi in range(bb) for d in range(N_DIR) for g in range(N_GROUPS)]
    masks = {0: (row >= col, row > col), 1: (row <= col, row < col)}
    gates, gc, gc_t, eg, egl = {}, {}, {}, {}, {}
    for bi in range(bb):
        for d in range(N_DIR):
            key = (bi, d)
            gates[key] = gt[d][bi]
            gc[key] = _dot_f32(_masks(d)[0].astype(F32), gates[key])
            gc_t[key] = jnp.concatenate([gc[key], gc[key]], axis=0).T
            last = CHUNK - 1 if d == 0 else 0
            eg[key] = jnp.exp(gc[key])
            egl[key] = jnp.exp(gc[key][last:last + 1, :] - gc[key])

    def cols_of(d, g):
        return [G_OFF + d * H_A + g * GROUP + hl for hl in range(GROUP)]

    q16, ks, beta_xs, decays, grams = [], [], [], [], []
    for gi, (bi, d, g) in enumerate(probs):
        key = (bi, d)
        cols = cols_of(d, g)
        incl, _ = masks[d]
        k = qkv[d][bi, :, A_W + g * gw:A_W + (g + 1) * gw]
        beta_x = jnp.concatenate([_col_bcast(gates[key], BETA_OFF - G_OFF + c) for c in cols], axis=1)
        decay = []
        for p in range(GROUP // PAIR):
            c0, c1 = cols[PAIR * p], cols[PAIR * p + 1]
            gcol = jnp.where(left, _col_bcast(gc[key], c0), _col_bcast(gc[key], c1))
            grow = jnp.where(left[0:1], gc_t[key][c0:c0 + 1, :], gc_t[key][c1:c1 + 1, :])
            decay.append(jnp.exp(jnp.where(incl, gcol - grow, NEG)))
        decays.append(jnp.concatenate(decay, axis=1))
        q16.append(qkv[d][bi, :, g * gw:(g + 1) * gw].astype(BF16))
        ks.append(k)
        beta_xs.append(beta_x)
        _store_blocks(bdk.at[gi], k.astype(BF16), CHUNK, DK_A)
    for gi in range(len(probs)):
        grams.append(lax.dot_general(jnp.concatenate([q16[gi], (ks[gi] * beta_xs[gi]).astype(BF16)], axis=0),
                                     bdk[gi], (((1,), (1,)), ((), ())), preferred_element_type=F32))
    attns, lmats = [], []
    for gi, (bi, d, g) in enumerate(probs):
        strict_w = jnp.concatenate([masks[d][1]] * (GROUP // PAIR), axis=1)
        attns.append((grams[gi][:CHUNK] * decays[gi]).astype(BF16))
        lmats.append(jnp.where(strict_w, grams[gi][CHUNK:] * decays[gi], 0.0))
    ainvs = _tri_inverse_wide(lmats, eye_w, bdn_hi, bdn_lo)
    eg_xs, kd_ts = [], []
    for gi, (bi, d, g) in enumerate(probs):
        key = (bi, d)
        cols = cols_of(d, g)
        v = qkv[d][bi, :, 2 * A_W + g * gw:2 * A_W + (g + 1) * gw]
        eg_x = jnp.concatenate([_col_bcast(eg[key], c) for c in cols], axis=1)
        egl_x = jnp.concatenate([_col_bcast(egl[key], c) for c in cols], axis=1)
        eg_xs.append(eg_x)
        _store_blocks(bduw.at[gi], (v * beta_xs[gi]).astype(BF16), CHUNK, DV_A)
        _store_blocks(bduw.at[gi], (ks[gi] * (beta_xs[gi] * eg_x)).astype(BF16), CHUNK, DK_A, col0=gw)
        kd = ks[gi] * egl_x
        kd_ts.append(jnp.concatenate([kd[:, hl * DK_A:(hl + 1) * DK_A] for hl in range(GROUP)], axis=0)
                     .T.astype(BF16))
        _store_blocks(bds.at[gi], s_scr[bi, d, g].astype(BF16), DK_A, DV_A)
    uws = []
    for gi in range(len(probs)):
        t_hi, t_lo = _split(ainvs[gi])
        uws.append(jnp.dot(jnp.concatenate([t_hi, t_lo], axis=1), jnp.concatenate([bduw[gi], bduw[gi]], axis=0),
                           preferred_element_type=F32))
    ws_qs = []
    hw = gw // 2
    for gi in range(len(probs)):
        wq = jnp.concatenate([uws[gi][:, gw:].astype(BF16), q16[gi]], axis=0)
        ws_qs.append(jnp.concatenate(
            [jnp.dot(wq[:, p * hw:(p + 1) * hw], bds[gi, p * hw:(p + 1) * hw, p * hw:(p + 1) * hw],
                     preferred_element_type=F32) for p in range(2)], axis=1))
    for gi in range(len(probs)):
        _store_blocks(bdv.at[gi], (uws[gi][:, :gw] - ws_qs[gi][:CHUNK]).astype(BF16), CHUNK, DV_A)
    rs = []
    for gi in range(len(probs)):
        rs.append(jnp.dot(jnp.concatenate([attns[gi], kd_ts[gi]], axis=0), bdv[gi], preferred_element_type=F32))
    for gi, (bi, d, g) in enumerate(probs):
        cols = cols_of(d, g)
        last = CHUNK - 1 if d == 0 else 0
        o = ws_qs[gi][CHUNK:] * eg_xs[gi] + rs[gi][:CHUNK]
        eg_last = jnp.concatenate(
            [jnp.broadcast_to(eg[bi, d][last:last + 1, c:c + 1], (1, DV_A)) for c in cols], axis=1)
        s_scr[bi, d, g] = s_scr[bi, d, g] * eg_last + rs[gi][CHUNK:]
        cidx = n if d == 0 else n_chunks - 1 - n
        rows = pl.ds(pl.multiple_of(cidx * CHUNK, CHUNK), CHUNK)
        tot = o + o_scr[bi, rows, g * gw:(g + 1) * gw]
        o_scr[bi, rows, g * gw:(g + 1) * gw] = tot
        for hl in range(GROUP):
            hs = slice((g * GROUP + hl) * DV_A, (g * GROUP + hl + 1) * DV_A)
            blk = tot[:, hl * DV_A:(hl + 1) * DV_A]
            ya_ref[bi, rows, hs] = (_rms(blk) * norm_ref[...] * ag[d][bi, :, hs]).astype(BF16)

    if emit_state:
        @pl.when(n == n_chunks - 1)
        def _():
            for bi in range(bb):
                for d in range(N_DIR):
                    for h in range(H_A):
                        sout_ref[bi, d, h] = s_scr[bi, d, h // GROUP, :, (h % GROUP) * DV_A:(h % GROUP + 1) * DV_A]


SCAN_BB = 2


def _scan_specs(n_chunks, width):
    return (pl.BlockSpec((SCAN_BB, CHUNK, width), lambda b, n: (b, n, 0)),
            pl.BlockSpec((SCAN_BB, CHUNK, width), lambda b, n: (b, n_chunks - 1 - n, 0)))


def _delta(qkv, rest, gates, norm_a, s0, batch, n_chunks, emit_state):
    t = n_chunks * CHUNK
    has_init = s0 is not None
    bb = SCAN_BB
    ng = bb * N_DIR * N_GROUPS
    qkv, rest, gates = (a.reshape(batch, t, a.shape[-1]) for a in (qkv, rest, gates))
    in_specs = [*_scan_specs(n_chunks, QKV_W), *_scan_specs(n_chunks, A_W), *_scan_specs(n_chunks, LANES),
                pl.BlockSpec((1, DV_A), lambda b, n: (0, 0))]
    args = [qkv, qkv, rest, rest, gates, gates, norm_a]
    sspec = pl.BlockSpec((bb, N_DIR, H_A, DK_A, DV_A), lambda b, n: (b, 0, 0, 0, 0))
    if has_init:
        in_specs.append(sspec)
        args.append(s0)
    out_specs = [pl.BlockSpec((bb, t, A_W), lambda b, n: (b, 0, 0))]
    out_shape = [jax.ShapeDtypeStruct((batch, t, A_W), BF16)]
    if emit_state:
        out_specs.append(sspec)
        out_shape.append(jax.ShapeDtypeStruct((batch, N_DIR, H_A, DK_A, DV_A), F32))
    out = pl.pallas_call(
        functools.partial(_delta_body, n_chunks=n_chunks, bb=bb, has_init=has_init, emit_state=emit_state),
        grid=(batch // bb, n_chunks),
        in_specs=in_specs,
        out_specs=out_specs,
        out_shape=out_shape,
        scratch_shapes=[pltpu.VMEM((bb, N_DIR, N_GROUPS, DK_A, GROUP * DV_A), F32), pltpu.VMEM((bb, t, A_W), F32),
                        pltpu.VMEM((ng, GROUP * CHUNK, GROUP * CHUNK), BF16),
                        pltpu.VMEM((ng, GROUP * CHUNK, GROUP * CHUNK), BF16),
                        pltpu.VMEM((ng, GROUP * CHUNK, GROUP * DK_A), BF16),
                        pltpu.VMEM((ng, GROUP * CHUNK, 2 * GROUP * DK_A), BF16),
                        pltpu.VMEM((ng, GROUP * DK_A, GROUP * DV_A), BF16),
                        pltpu.VMEM((ng, GROUP * CHUNK, GROUP * DV_A), BF16)],
        compiler_params=_params(("arbitrary", "arbitrary")),
        name="delta_scan",
    )(*args)
    return [out[0].reshape(batch * t, A_W), *out[1:]]


CAUG_W = DV_B + LANES
MW = H_B * CAUG_W


def _scan_max(x, d):
    row = lax.broadcasted_iota(jnp.int32, x.shape, 0)
    s = 1
    while s < CHUNK:
        if d == 0:
            shifted = jnp.where(row >= s, pltpu.roll(x, s, 0), NEG)
        else:
            shifted = jnp.where(row < CHUNK - s, pltpu.roll(x, CHUNK - s, 0), NEG)
        x = jnp.maximum(x, shifted)
        s *= 2
    return x


def _mlstm_body(*refs, n_chunks, bb, has_init, emit_state):
    rest = refs[0:2]
    gt = refs[2:4]
    norm_ref = refs[4]
    pos = 5
    c0_ref = m0_ref = None
    if has_init:
        c0_ref, m0_ref = refs[pos], refs[pos + 1]
        pos += 2
    yb_ref = refs[pos]
    pos += 1
    cout_ref = nout_ref = mout_ref = None
    if emit_state:
        cout_ref, nout_ref, mout_ref = refs[pos:pos + 3]
        pos += 3
    c_scr, m_scr, o_scr, bdk, bdc, bdv = refs[pos:pos + 6]

    n = pl.program_id(1)
    qw = H_B * DK_B

    @pl.when((n == 0) & (pl.program_id(0) == 0))
    def _():
        for ref in (bdk, bdc, bdv):
            ref[...] = jnp.zeros_like(ref)
        for i in range(bb * N_DIR):
            for h in range(H_B):
                bdv[i, h * CHUNK:(h + 1) * CHUNK, h * CAUG_W + DV_B:(h + 1) * CAUG_W] = jnp.ones((CHUNK, LANES), BF16)

    @pl.when(n == 0)
    def _():
        o_scr[...] = jnp.zeros_like(o_scr)
        if has_init:
            c_scr[...] = c0_ref[...]
            m_scr[...] = m0_ref[...]
        else:
            c_scr[...] = jnp.zeros_like(c_scr)
            m_scr[...] = jnp.zeros_like(m_scr)

    row = lax.broadcasted_iota(jnp.int32, (CHUNK, LANES), 0)
    lane = lax.broadcasted_iota(jnp.int32, (CHUNK, LANES), 1)
    col = lane & (CHUNK - 1)
    left = lane < CHUNK
    dirs = [(bi, d) for bi in range(bb) for d in range(N_DIR)]
    slot = {key: i for i, key in enumerate(dirs)}
    nc, a_t, iw, emt, ksc, dec_row = {}, {}, {}, {}, {}, {}
    for key in dirs:
        bi, d = key
        lo = LF_OFF + d * H_B
        mine = (lane >= lo) & (lane < lo + H_B)
        g = gt[d][bi]
        gc = jnp.where(mine, _dot_f32(_masks(d)[0].astype(F32), g), 0.0)
        a = jnp.where(mine, pltpu.roll(g, LF_OFF - LI_OFF, 1), 0.0) - gc
        last = CHUNK - 1 if d == 0 else 0
        m_old = m_scr[bi, d][0:1, :]
        mx = jnp.maximum(m_old, _scan_max(a, d))
        mxl = mx[last:last + 1, :]
        nc[key] = -mx
        a_t[key] = jnp.concatenate([a, a], axis=0).T
        iw[key] = jnp.exp(m_old - mx)
        emt[key] = jnp.exp(-(gc + mx))
        ksc[key] = jnp.exp(a - mxl)
        dec_row[key] = jnp.exp(m_old - mxl)
        m_scr[bi, d] = jnp.broadcast_to(gc[last:last + 1, :] + mxl, (8, LANES))

    q16, ks_t = {}, {}
    for key in dirs:
        bi, d = key
        i = slot[key]
        lo = LF_OFF + d * H_B
        q16[key] = rest[d][bi, :, OFF_BQ:OFF_BQ + qw].astype(BF16)
        k = rest[d][bi, :, OFF_BK:OFF_BK + qw]
        _store_blocks(bdk.at[i], k.astype(BF16), CHUNK, DK_B)
        ks = k * jnp.concatenate([_col_bcast(ksc[key], lo + h) for h in range(H_B)], axis=1)
        ks_t[key] = jnp.concatenate([ks[:, h * DK_B:(h + 1) * DK_B] for h in range(H_B)], axis=0).T.astype(BF16)
        for h in range(H_B):
            bdv[i, h * CHUNK:(h + 1) * CHUNK, h * CAUG_W:h * CAUG_W + DV_B] = (
                rest[d][bi, :, OFF_BV + h * DV_B:OFF_BV + (h + 1) * DV_B].astype(BF16))
        _store_blocks(bdc.at[i], c_scr[bi, d].astype(BF16), DK_B, CAUG_W)
    qk = {key: lax.dot_general(q16[key], bdk[slot[key]], (((1,), (1,)), ((), ())), preferred_element_type=F32)
          for key in dirs}
    lhs = {}
    for key in dirs:
        bi, d = key
        lo = LF_OFF + d * H_B
        incl = row >= col if d == 0 else row <= col
        log_w = []
        for p in range(H_B // PAIR):
            l0, l1 = lo + PAIR * p, lo + PAIR * p + 1
            ccol = jnp.where(left, _col_bcast(nc[key], l0), _col_bcast(nc[key], l1))
            crow = jnp.where(left[0:1], a_t[key][l0:l0 + 1, :], a_t[key][l1:l1 + 1, :])
            log_w.append(jnp.where(incl, ccol + crow, NEG))
        dw = jnp.exp(jnp.concatenate(log_w, axis=1)) * qk[key]
        iw_x = jnp.concatenate([_col_bcast(iw[key], lo + h) for h in range(H_B)], axis=1)
        lhs[key] = ((rest[d][bi, :, OFF_BQ:OFF_BQ + qw] * iw_x).astype(BF16), dw.astype(BF16))
    num = {}
    for key in dirs:
        i = slot[key]
        parts = []
        for p in range(H_B // PAIR):
            ql, dl, cl = slice(p * PAIR * DK_B, (p + 1) * PAIR * DK_B), slice(p * LANES, (p + 1) * LANES), \
                slice(p * PAIR * CAUG_W, (p + 1) * PAIR * CAUG_W)
            parts.append(jnp.dot(jnp.concatenate([lhs[key][0][:, ql], lhs[key][1][:, dl]], axis=1),
                                 jnp.concatenate([bdc[i, ql, cl], bdv[i, dl, cl]], axis=0),
                                 preferred_element_type=F32))
        num[key] = jnp.concatenate(parts, axis=1)
    upd = {key: jnp.dot(ks_t[key], bdv[slot[key]], preferred_element_type=F32) for key in dirs}
    for key in dirs:
        bi, d = key
        lo = LF_OFF + d * H_B
        dec_x = jnp.concatenate(
            [jnp.broadcast_to(dec_row[key][:, lo + h:lo + h + 1], (1, CAUG_W)) for h in range(H_B)], axis=1)
        c_scr[bi, d] = c_scr[bi, d] * dec_x + upd[key]
        cidx = n if d == 0 else n_chunks - 1 - n
        rows = pl.ds(pl.multiple_of(cidx * CHUNK, CHUNK), CHUNK)
        for h in range(H_B):
            vs = slice(h * DV_B, (h + 1) * DV_B)
            den = jnp.maximum(jnp.abs(num[key][:, h * CAUG_W + DV_B:(h + 1) * CAUG_W]), _col_bcast(emt[key], lo + h))
            hb = jnp.concatenate([num[key][:, h * CAUG_W:h * CAUG_W + LANES] / den,
                                  num[key][:, h * CAUG_W + LANES:h * CAUG_W + DV_B] / den], axis=1)
            tot = hb + o_scr[bi, rows, vs]
            o_scr[bi, rows, vs] = tot
            ogate = rest[d][bi, :, OFF_BO + h * DV_B:OFF_BO + (h + 1) * DV_B]
            yb_ref[bi, rows, vs] = (_rms(tot) * norm_ref[...] * ogate).astype(BF16)

    if emit_state:
        @pl.when(n == n_chunks - 1)
        def _():
            for bi, d in dirs:
                for h in range(H_B):
                    cout_ref[bi, d, h] = c_scr[bi, d, :, h * CAUG_W:h * CAUG_W + DV_B]
                    nout_ref[bi, d, h] = c_scr[bi, d, :, h * CAUG_W + DV_B:(h + 1) * CAUG_W]
            mout_ref[...] = m_scr[...]


def _mlstm(rest, gates, norm_b, c0, m0, batch, n_chunks, emit_state):
    t = n_chunks * CHUNK
    has_init = c0 is not None
    bb = SCAN_BB
    rest, gates = (a.reshape(batch, t, a.shape[-1]) for a in (rest, gates))
    mspec = pl.BlockSpec((bb, N_DIR, 8, LANES), lambda b, n: (b, 0, 0, 0))
    in_specs = [*_scan_specs(n_chunks, REST_W), *_scan_specs(n_chunks, LANES),
                pl.BlockSpec((1, DV_B), lambda b, n: (0, 0))]
    args = [rest, rest, gates, gates, norm_b]
    if has_init:
        in_specs += [pl.BlockSpec((bb, N_DIR, DK_B, MW), lambda b, n: (b, 0, 0, 0)), mspec]
        args += [c0, m0]
    out_specs = [pl.BlockSpec((bb, t, B_W), lambda b, n: (b, 0, 0))]
    out_shape = [jax.ShapeDtypeStruct((batch, t, B_W), BF16)]
    if emit_state:
        out_specs += [pl.BlockSpec((bb, N_DIR, H_B, DK_B, DV_B), lambda b, n: (b, 0, 0, 0, 0)),
                      pl.BlockSpec((bb, N_DIR, H_B, DK_B, LANES), lambda b, n: (b, 0, 0, 0, 0)), mspec]
        out_shape += [jax.ShapeDtypeStruct((batch, N_DIR, H_B, DK_B, DV_B), F32),
                      jax.ShapeDtypeStruct((batch, N_DIR, H_B, DK_B, LANES), F32),
                      jax.ShapeDtypeStruct((batch, N_DIR, 8, LANES), F32)]
    out = pl.pallas_call(
        functools.partial(_mlstm_body, n_chunks=n_chunks, bb=bb, has_init=has_init, emit_state=emit_state),
        grid=(batch // bb, n_chunks),
        in_specs=in_specs,
        out_specs=out_specs,
        out_shape=out_shape,
        scratch_shapes=[pltpu.VMEM((bb, N_DIR, DK_B, MW), F32),
                        pltpu.VMEM((bb, N_DIR, 8, LANES), F32),
                        pltpu.VMEM((bb, t, B_W), F32),
                        pltpu.VMEM((bb * N_DIR, H_B * CHUNK, H_B * DK_B), BF16),
                        pltpu.VMEM((bb * N_DIR, H_B * DK_B, MW), BF16),
                        pltpu.VMEM((bb * N_DIR, H_B * CHUNK, MW), BF16)],
        compiler_params=_params(("arbitrary", "arbitrary")),
        name="mlstm_scan",
    )(*args)
    return [out[0].reshape(batch * t, B_W), *out[1:]]


OUT_RB = 256


def _outproj_body(ya_ref, yb_ref, wa_ref, wb_ref, x_ref, mod_ref, post1_ref, pre2_ref, x1_ref, h2_ref):
    for rb in range(x_ref.shape[0] // OUT_RB):
        rs = slice(rb * OUT_RB, (rb + 1) * OUT_RB)
        mix = (jnp.dot(ya_ref[rs, :], wa_ref[...], preferred_element_type=F32)
               + jnp.dot(yb_ref[rs, :], wb_ref[...], preferred_element_type=F32))
        x1 = x_ref[rs, :] + mod_ref[0, 2:3, :] * (_rms(mix) * post1_ref[...])
        x1_ref[rs, :] = x1
        h2 = _rms(x1) * pre2_ref[...] * (1.0 + mod_ref[0, 4:5, :]) + mod_ref[0, 3:4, :]
        h2_ref[rs, :] = h2.astype(BF16)


def _outproj(ya, yb, w_out, x2d, mod3, mod_map, post1, pre2, tm):
    m = x2d.shape[0]
    row = lambda i: (i, 0)
    const = lambda i: (0, 0)
    return pl.pallas_call(
        _outproj_body,
        grid=(m // tm,),
        in_specs=[pl.BlockSpec((tm, A_W), row), pl.BlockSpec((tm, B_W), row),
                  pl.BlockSpec((A_W, D_MODEL), lambda i: (0, 0)),
                  pl.BlockSpec((B_W, D_MODEL), lambda i: (1, 0)),
                  pl.BlockSpec((tm, D_MODEL), row),
                  pl.BlockSpec((1, 6, D_MODEL), lambda i: (mod_map(i * tm), 0, 0)),
                  pl.BlockSpec((1, D_MODEL), const), pl.BlockSpec((1, D_MODEL), const)],
        out_specs=[pl.BlockSpec((tm, D_MODEL), row), pl.BlockSpec((tm, D_MODEL), row)],
        out_shape=[jax.ShapeDtypeStruct((m, D_MODEL), F32), jax.ShapeDtypeStruct((m, D_MODEL), BF16)],
        compiler_params=_params(("parallel",)),
        name="outproj",
    )(ya, yb, w_out, w_out, x2d, mod3, post1, pre2)


def _ffn_body(h2_ref, w1_ref, w2_ref, x1_ref, mod_ref, post2_ref, o_ref):
    kk = pl.program_id(1)
    a = jnp.maximum(jnp.dot(h2_ref[...], w1_ref[...], preferred_element_type=F32), 0.0)
    contrib = jnp.dot((a * a).astype(BF16), w2_ref[...], preferred_element_type=F32)

    @pl.when(kk == 0)
    def _():
        o_ref[...] = contrib

    @pl.when((kk > 0) & (kk < pl.num_programs(1) - 1))
    def _():
        o_ref[...] += contrib

    @pl.when(kk == pl.num_programs(1) - 1)
    def _():
        f = o_ref[...] + contrib
        o_ref[...] = x1_ref[...] + mod_ref[0, 5:6, :] * (_rms(f) * post2_ref[...])


def _ffn(h2, w1, w2, x1, mod3, mod_map, post2, tm, fc):
    m = h2.shape[0]
    return pl.pallas_call(
        _ffn_body,
        grid=(m // tm, FFN // fc),
        in_specs=[pl.BlockSpec((tm, D_MODEL), lambda i, k: (i, 0)),
                  pl.BlockSpec((D_MODEL, fc), lambda i, k: (0, k)),
                  pl.BlockSpec((fc, D_MODEL), lambda i, k: (k, 0)),
                  pl.BlockSpec((tm, D_MODEL), lambda i, k: (i, 0)),
                  pl.BlockSpec((1, 6, D_MODEL), lambda i, k: (mod_map(i * tm), 0, 0)),
                  pl.BlockSpec((1, D_MODEL), lambda i, k: (0, 0))],
        out_specs=pl.BlockSpec((tm, D_MODEL), lambda i, k: (i, 0)),
        out_shape=jax.ShapeDtypeStruct((m, D_MODEL), F32),
        compiler_params=_params(("parallel", "arbitrary")),
        name="ffn",
    )(h2, w1, w2, x1, mod3, post2)


def _block(x, mod3, mod_of_row, lp, init, seq_len, emit_state):
    bsz, t, _ = x.shape
    x2d = x.reshape(bsz * t, D_MODEL)
    n_chunks = t // CHUNK
    qkv, rest, gates = _proj(x2d, mod3, lambda i: mod_of_row(i * PROJ_TM), lp["pre1"], lp["w16"],
                             lp["w_mid"], lp["w_gate"], lp["conv_w"], lp["gate_p"], seq_len)
    s0, c0, m0 = init if init is not None else (None, None, None)
    d_out = _delta(qkv, rest, gates, lp["norm_a"], s0, bsz, n_chunks, emit_state)
    m_out = _mlstm(rest, gates, lp["norm_b"], c0, m0, bsz, n_chunks, emit_state)
    x1, h2 = _outproj(d_out[0], m_out[0], lp["w_out"], x2d, mod3, mod_of_row, lp["post1"], lp["pre2"], 512)
    y = _ffn(h2, lp["w1"], lp["w2"], x1, mod3, mod_of_row, lp["post2"], 512, 1024)
    states = None
    if emit_state:
        m_fin = jnp.stack([m_out[3][:, d, 0, LF_OFF + d * H_B:LF_OFF + (d + 1) * H_B] for d in range(N_DIR)], axis=1)
        states = (d_out[1], m_out[1], m_out[2][..., 0], m_fin)
    return y.reshape(bsz, t, D_MODEL), states


def _layer_params(l, norm_mix_pre, norm_mix_post, norm_ffn_pre, norm_ffn_post, w_in, conv_w, a_log, dt_bias,
                  norm_a, mlstm_ibias, mlstm_fbias, norm_b, w_out, w_ffn1, w_ffn2):
    w = w_in[l]
    o_ag = QKV_W
    o_aa = o_ag + A_W
    o_ab = o_aa + N_DIR * H_A
    o_bq = o_ab + N_DIR * H_A
    o_bi = o_bq + 2 * H_B * DK_B + 2 * B_W
    o_bf = o_bi + N_DIR * H_B
    n_gate = 2 * N_DIR * H_A + 2 * N_DIR * H_B
    w16 = w.astype(BF16)
    w_gate = jnp.concatenate([w16[:, o_aa:o_bq], w16[:, o_bi:o_bf + N_DIR * H_B],
                              jnp.zeros((D_MODEL, LANES - n_gate), BF16)], axis=1)

    def lane_row(vals, off):
        return jnp.zeros((LANES,), F32).at[off:off + vals.size].set(vals.reshape(-1))

    gate_p = jnp.stack([lane_row(a_log[l], G_OFF), lane_row(dt_bias[l], G_OFF),
                        lane_row(mlstm_ibias[l], LI_OFF) + lane_row(mlstm_fbias[l], LF_OFF)]
                       + [jnp.zeros((LANES,), F32)] * 5)
    row = lambda v: v[l].reshape(1, -1)
    return dict(
        pre1=row(norm_mix_pre), post1=row(norm_mix_post), pre2=row(norm_ffn_pre), post2=row(norm_ffn_post),
        w16=w16, w_mid=w16[:, o_bq:o_bi], w_gate=w_gate, gate_p=gate_p,
        conv_w=jnp.concatenate([conv_w[l].T, jnp.zeros((5, QKV_W), F32)], axis=0),
        norm_a=row(norm_a), norm_b=row(norm_b),
        w_out=w_out[l].astype(BF16), w1=w_ffn1[l].astype(BF16), w2=w_ffn2[l].astype(BF16))


def kernel(x_prompt, x_sample, state_delta, state_mlstm_C, state_mlstm_n, state_mlstm_m, c, c_ctx, w_ada, b_ada, norm_mix_pre, norm_mix_post, norm_ffn_pre, norm_ffn_post, w_in, conv_w, a_log, dt_bias, norm_a, mlstm_ibias, mlstm_fbias, norm_b, w_out, w_ffn1, w_ffn2):
    depth = w_in.shape[0]
    n_lat = x_sample.shape[0]
    t_lat = x_sample.shape[1]
    cond = jnp.concatenate([c_ctx[None, :], c, jnp.zeros((8 - 1 - n_lat, D_MODEL), F32)], axis=0)
    y_prompt, y_sample = x_prompt, x_sample
    acc = ([], [], [], [])
    for l in range(depth):
        lp = _layer_params(l, norm_mix_pre, norm_mix_post, norm_ffn_pre, norm_ffn_post, w_in, conv_w, a_log,
                           dt_bias, norm_a, mlstm_ibias, mlstm_fbias, norm_b, w_out, w_ffn1, w_ffn2)
        mod = _ada(cond, w_ada, b_ada[l].reshape(1, -1), l)
        mod3 = mod[:1 + n_lat].reshape(1 + n_lat, 6, D_MODEL)
        y_prompt, st = _block(y_prompt, mod3, lambda r: 0, lp, None, x_prompt.shape[1], True)
        for a, s in zip(acc, st):
            a.append(s)
        n_rep = jnp.broadcast_to(state_mlstm_n[:, l][..., None], state_mlstm_n[:, l].shape + (LANES,))
        c_aug0 = jnp.concatenate([state_mlstm_C[:, l], n_rep], axis=-1)
        c_aug0 = c_aug0.transpose(0, 1, 3, 2, 4).reshape(n_lat, N_DIR, DK_B, MW)
        m0 = jnp.zeros((n_lat, N_DIR, LANES), F32)
        for d in range(N_DIR):
            m0 = m0.at[:, d, LF_OFF + d * H_B:LF_OFF + (d + 1) * H_B].set(state_mlstm_m[:, l, d])
        m0 = jnp.broadcast_to(m0[:, :, None, :], (n_lat, N_DIR, 8, LANES))
        y_sample, _ = _block(y_sample, mod3, lambda r: 1 + r // t_lat, lp, (state_delta[:, l], c_aug0, m0),
                             GRID_W, False)
    return (y_prompt, y_sample) + tuple(jnp.stack(a, axis=1) for a in acc)
```

```python
import functools

import jax
import jax.numpy as jnp
from jax import lax
from jax.experimental import pallas as pl
from jax.experimental.pallas import tpu as pltpu

F32 = jnp.float32
BF16 = jnp.bfloat16

D_MODEL = 2048
N_DIR = 2
A_W = D_MODEL // 2
B_W = D_MODEL - A_W
DK_A = 128
DV_A = 128
H_A = A_W // DV_A
DV_B = 256
DK_B = DV_B // 2
H_B = B_W // DV_B
GRID_W = 64
CHUNK = 64
FFN = 4 * D_MODEL
EPS = 1e-6
LANES = 128
NEG = -1e30

QKV_W = 3 * A_W
REST_W = A_W + 2 * H_B * DK_B + 2 * B_W
OFF_AG, OFF_BQ, OFF_BK, OFF_BV, OFF_BO = 0, A_W, A_W + H_B * DK_B, A_W + 2 * H_B * DK_B, A_W + 2 * H_B * DK_B + B_W
G_OFF, BETA_OFF = 0, N_DIR * H_A
LI_OFF = 2 * N_DIR * H_A
LF_OFF = LI_OFF + N_DIR * H_B

VMEM_LIMIT = 56 * 1024 * 1024


def _sigmoid(x):
    return 1.0 / (1.0 + jnp.exp(-x))


def _softplus(x):
    return jnp.maximum(x, 0.0) + jnp.log1p(jnp.exp(-jnp.abs(x)))


def _dot(a, b):
    return jnp.dot(a.astype(BF16), b.astype(BF16), preferred_element_type=F32)


def _dot_f32(a, b):
    return jnp.dot(a, b, precision=lax.Precision.HIGHEST, preferred_element_type=F32)


def _rms(x):
    return x * lax.rsqrt(jnp.mean(x * x, axis=-1, keepdims=True) + EPS)


def _params(sem):
    return pltpu.CompilerParams(dimension_semantics=sem, vmem_limit_bytes=VMEM_LIMIT)


def _ada_body(c_ref, w_ref, b_ref, o_ref):
    c = c_ref[...]
    o_ref[...] = _dot(c * _sigmoid(c), w_ref[...]) + b_ref[...]


def _ada(c_all, w_ada, b, l):
    n = w_ada.shape[2]
    tn = 1024
    return pl.pallas_call(
        _ada_body,
        grid=(n // tn,),
        in_specs=[pl.BlockSpec(c_all.shape, lambda j: (0, 0)),
                  pl.BlockSpec((None, D_MODEL, tn), lambda j: (l, 0, j)),
                  pl.BlockSpec((1, tn), lambda j: (0, j))],
        out_specs=pl.BlockSpec((c_all.shape[0], tn), lambda j: (0, j)),
        out_shape=jax.ShapeDtypeStruct((c_all.shape[0], n), F32),
        compiler_params=_params(("arbitrary",)),
        name="ada",
    )(c_all, w_ada, b)


PROJ_TM = 1024
PROJ_RB = 256
PROJ_TN = 512
N_QKV_T = QKV_W // PROJ_TN
N_REST_T = REST_W // PROJ_TN
N_HEAD_T = (QKV_W + A_W) // PROJ_TN
N_MID_T = N_QKV_T + N_REST_T - N_HEAD_T


def _proj_body(x_ref, mod_ref, g_ref, wh_ref, wb_ref, wg_ref, cw_ref, gp_ref, qkv_ref, rest_ref, gate_ref, h_scr, *,
               seq_len):
    j = pl.program_id(1)
    blocks = [slice(rb * PROJ_RB, (rb + 1) * PROJ_RB) for rb in range(PROJ_TM // PROJ_RB)]

    def run(w_ref, epilogue, out_ref):
        for rs in blocks:
            epilogue(jnp.dot(h_scr[rs, :], w_ref[...], preferred_element_type=F32), out_ref, rs)

    def run_first():
        for rs in blocks:
            y = _rms(x_ref[rs, :]) * g_ref[...]
            h_scr[rs, :] = (y * (1.0 + mod_ref[0, 1:2, :]) + mod_ref[0, 0:1, :]).astype(BF16)
            ep_l2(DK_A ** -0.5)(jnp.dot(h_scr[rs, :], wh_ref[...], preferred_element_type=F32), qkv_ref, rs)
            ep_gate(jnp.dot(h_scr[rs, :], wg_ref[...], preferred_element_type=F32), gate_ref, rs)

    def conv_silu(acc):
        pos = lax.broadcasted_iota(jnp.int32, acc.shape, 0) & (seq_len - 1)
        prev = jnp.where(pos == 0, 0.0, pltpu.roll(acc, 1, 0))
        nxt = jnp.where(pos == seq_len - 1, 0.0, pltpu.roll(acc, acc.shape[0] - 1, 0))
        y = prev * cw_ref[0:1, :] + acc * cw_ref[1:2, :] + nxt * cw_ref[2:3, :]
        return y * _sigmoid(y)

    def ep_l2(scale):
        def f(acc, out_ref, rs):
            y = conv_silu(acc)
            for g in range(PROJ_TN // LANES):
                blk = y[:, g * LANES:(g + 1) * LANES]
                inv = lax.rsqrt(jnp.sum(blk * blk, axis=-1, keepdims=True) + EPS)
                out_ref[rs, g * LANES:(g + 1) * LANES] = (blk * (inv * scale)).astype(out_ref.dtype)
        return f

    def ep_map(fn):
        def f(acc, out_ref, rs):
            out_ref[rs, :] = fn(acc).astype(out_ref.dtype)
        return f

    def ep_gate(z, out_ref, rs):
        lane = lax.broadcasted_iota(jnp.int32, z.shape, 1)
        g = -jnp.exp(gp_ref[0:1, :]) * _softplus(z + gp_ref[1:2, :])
        li = z + gp_ref[2:3, :]
        out_ref[rs, :] = jnp.where(lane < BETA_OFF, g,
                                   jnp.where(lane < LI_OFF, _sigmoid(z),
                                             jnp.where(lane < LF_OFF, li,
                                                       jnp.where(lane < LF_OFF + N_DIR * H_B, -_softplus(-li), 0.0))))

    tp = A_W // PROJ_TN
    c0 = (j - N_QKV_T) * PROJ_TN
    in_rest = (j >= N_QKV_T) & (j < N_QKV_T + N_REST_T)
    pl.when(j == 0)(run_first)
    variants = [
        ((j > 0) & (j < tp), wh_ref, ep_l2(DK_A ** -0.5), qkv_ref),
        ((j >= tp) & (j < 2 * tp), wh_ref, ep_l2(1.0), qkv_ref),
        ((j >= 2 * tp) & (j < N_QKV_T), wh_ref, ep_map(conv_silu), qkv_ref),
        (in_rest & (c0 < OFF_BQ), wh_ref, ep_map(lambda a: a * _sigmoid(a)), rest_ref),
        (in_rest & (c0 >= OFF_BQ) & (c0 < OFF_BK), wb_ref, ep_map(lambda a: a * (DK_B ** -0.5)), rest_ref),
        (in_rest & (c0 >= OFF_BK) & (c0 < OFF_BO), wb_ref, ep_map(lambda a: a), rest_ref),
        (in_rest & (c0 >= OFF_BO), wb_ref, ep_map(_sigmoid), rest_ref),
    ]
    for cond, w_ref, epilogue, out_ref in variants:
        pl.when(cond)(functools.partial(run, w_ref, epilogue, out_ref))


def _proj(x2d, mod3, mod_map, pre_g, w16, w_mid, w_gate, conv_w, gate_p, seq_len):
    m = x2d.shape[0]
    tm, tn = PROJ_TM, PROJ_TN
    assert PROJ_RB % seq_len == 0 and m % tm == 0
    return pl.pallas_call(
        functools.partial(_proj_body, seq_len=seq_len),
        grid=(m // tm, N_HEAD_T + N_MID_T),
        in_specs=[pl.BlockSpec((tm, D_MODEL), lambda i, j: (i, 0)),
                  pl.BlockSpec((1, 6, D_MODEL), lambda i, j: (mod_map(i), 0, 0)),
                  pl.BlockSpec((1, D_MODEL), lambda i, j: (0, 0)),
                  pl.BlockSpec((D_MODEL, tn), lambda i, j: (0, jnp.minimum(j, N_HEAD_T - 1))),
                  pl.BlockSpec((D_MODEL, tn), lambda i, j: (0, jnp.clip(j - N_HEAD_T, 0, N_MID_T - 1))),
                  pl.BlockSpec((D_MODEL, LANES), lambda i, j: (0, 0)),
                  pl.BlockSpec((8, tn), lambda i, j: (0, jnp.minimum(j, N_QKV_T - 1))),
                  pl.BlockSpec((8, LANES), lambda i, j: (0, 0))],
        out_specs=[pl.BlockSpec((tm, tn), lambda i, j: (i, jnp.minimum(j, N_QKV_T - 1))),
                   pl.BlockSpec((tm, tn), lambda i, j: (i, jnp.clip(j - N_QKV_T, 0, N_REST_T - 1))),
                   pl.BlockSpec((tm, LANES), lambda i, j: (i, 0))],
        out_shape=[jax.ShapeDtypeStruct((m, QKV_W), BF16), jax.ShapeDtypeStruct((m, REST_W), BF16),
                   jax.ShapeDtypeStruct((m, LANES), F32)],
        scratch_shapes=[pltpu.VMEM((tm, D_MODEL), BF16)],
        compiler_params=_params(("parallel", "arbitrary")),
        name="proj",
    )(x2d, mod3, pre_g, w16, w_mid, w_gate, conv_w, gate_p)


def _masks(d):
    row = lax.broadcasted_iota(jnp.int32, (CHUNK, CHUNK), 0)
    col = lax.broadcasted_iota(jnp.int32, (CHUNK, CHUNK), 1)
    if d == 0:
        return row >= col, row > col, row == col
    return row <= col, row < col, row == col


GROUP = 4
N_GROUPS = H_A // GROUP
PAIR = LANES // CHUNK


def _split(x):
    hi = x.astype(BF16)
    return hi, (x - hi.astype(F32)).astype(BF16)


def _store_blocks(ref, x, blk_r, blk_c, col0=0):
    for h in range(GROUP):
        ref[h * blk_r:(h + 1) * blk_r, col0 + h * blk_c:col0 + (h + 1) * blk_c] = x[:, h * blk_c:(h + 1) * blk_c]


def _dot3(a_hi, a_lo, b_hi, b_lo):
    lhs = jnp.concatenate([a_hi, a_lo, a_hi], axis=1)
    rhs = jnp.concatenate([b_hi, b_hi, b_lo], axis=0)
    return jnp.dot(lhs, rhs, preferred_element_type=F32)


def _tri_inverse_wide(lmats, eye_w, bd_hi, bd_lo):
    def bd_dot(i, a_hi, a_lo, b_hi, b_lo):
        _store_blocks(bd_hi.at[i], b_hi, CHUNK, CHUNK)
        _store_blocks(bd_lo.at[i], b_lo, CHUNK, CHUNK)
        return _dot3(a_hi, a_lo, bd_hi[i], bd_lo[i])

    idx = range(len(lmats))
    s = [eye_w - l for l in lmats]
    p = []
    for i in idx:
        m_hi, m_lo = _split(-lmats[i])
        p.append(bd_dot(i, m_hi, m_lo, m_hi, m_lo))
    for _ in range(4):
        for i in idx:
            p_hi, p_lo = _split(p[i])
            s_hi, s_lo = _split(s[i])
            r = bd_dot(i, jnp.concatenate([p_hi, s_hi], axis=0), jnp.concatenate([p_lo, s_lo], axis=0), p_hi, p_lo)
            p[i] = r[:CHUNK]
            s[i] = s[i] + r[CHUNK:]
    for i in idx:
        p_hi, p_lo = _split(p[i])
        s_hi, s_lo = _split(s[i])
        s[i] = s[i] + bd_dot(i, s_hi, s_lo, p_hi, p_lo)
    return s


def _col_bcast(tile, c, width=LANES):
    return jnp.broadcast_to(tile[:, c:c + 1], (tile.shape[0], width))


def _delta_body(*refs, n_chunks, bb, has_init, emit_state):
    qkv = refs[0:2]
    ag = refs[2:4]
    gt = refs[4:6]
    norm_ref = refs[6]
    pos = 7
    s0_ref = None
    if has_init:
        s0_ref = refs[pos]
        pos += 1
    ya_ref = refs[pos]
    pos += 1
    sout_ref = None
    if emit_state:
        sout_ref = refs[pos]
        pos += 1
    s_scr, o_scr, bdn_hi, bdn_lo, bdk, bduw, bds, bdv = refs[pos:pos + 8]

    n = pl.program_id(1)
    gw = GROUP * DK_A

    @pl.when((n == 0) & (pl.program_id(0) == 0))
    def _():
        for ref in (bdn_hi, bdn_lo, bdk, bduw, bds, bdv):
            ref[...] = jnp.zeros_like(ref)

    @pl.when(n == 0)
    def _():
        o_scr[...] = jnp.zeros_like(o_scr)
        for bi in range(bb):
            for d in range(N_DIR):
                for h in range(H_A):
                    blk = s0_ref[bi, d, h] if has_init else jnp.zeros((DK_A, DV_A), F32)
                    s_scr[bi, d, h // GROUP, :, (h % GROUP) * DV_A:(h % GROUP + 1) * DV_A] = blk

    row = lax.broadcasted_iota(jnp.int32, (CHUNK, LANES), 0)
    lane = lax.broadcasted_iota(jnp.int32, (CHUNK, LANES), 1)
    col = lane & (CHUNK - 1)
    left = lane < CHUNK
    eye_w = jnp.concatenate([(row == col).astype(F32)] * (GROUP // PAIR), axis=1)
    probs = [(bi, d, g) for bi in range(bb) for d in range(N_DIR) for g in range(N_GROUPS)]
    masks = {0: (row >= col, row > col), 1: (row <= col, row < col)}
    gates, gc, gc_t, eg, egl = {}, {}, {}, {}, {}
    for bi in range(bb):
        for d in range(N_DIR):
            key = (bi, d)
            gates[key] = gt[d][bi]
            gc[key] = _dot_f32(_masks(d)[0].astype(F32), gates[key])
            gc_t[key] = jnp.concatenate([gc[key], gc[key]], axis=0).T
            last = CHUNK - 1 if d == 0 else 0
            eg[key] = jnp.exp(gc[key])
            egl[key] = jnp.exp(gc[key][last:last + 1, :] - gc[key])

    def cols_of(d, g):
        return [G_OFF + d * H_A + g * GROUP + hl for hl in range(GROUP)]

    q16, ks, beta_xs, decays, grams = [], [], [], [], []
    for gi, (bi, d, g) in enumerate(probs):
        key = (bi, d)
        cols = cols_of(d, g)
        incl, _ = masks[d]
        k = qkv[d][bi, :, A_W + g * gw:A_W + (g + 1) * gw]
        beta_x = jnp.concatenate([_col_bcast(gates[key], BETA_OFF - G_OFF + c) for c in cols], axis=1)
        decay = []
        for p in range(GROUP // PAIR):
            c0, c1 = cols[PAIR * p], cols[PAIR * p + 1]
            gcol = jnp.where(left, _col_bcast(gc[key], c0), _col_bcast(gc[key], c1))
            grow = jnp.where(left[0:1], gc_t[key][c0:c0 + 1, :], gc_t[key][c1:c1 + 1, :])
            decay.append(jnp.exp(jnp.where(incl, gcol - grow, NEG)))
        decays.append(jnp.concatenate(decay, axis=1))
        q16.append(qkv[d][bi, :, g * gw:(g + 1) * gw].astype(BF16))
        ks.append(k)
        beta_xs.append(beta_x)
        _store_blocks(bdk.at[gi], k.astype(BF16), CHUNK, DK_A)
    for gi in range(len(probs)):
        grams.append(lax.dot_general(jnp.concatenate([q16[gi], (ks[gi] * beta_xs[gi]).astype(BF16)], axis=0),
                                     bdk[gi], (((1,), (1,)), ((), ())), preferred_element_type=F32))
    attns, lmats = [], []
    for gi, (bi, d, g) in enumerate(probs):
        strict_w = jnp.concatenate([masks[d][1]] * (GROUP // PAIR), axis=1)
        attns.append((grams[gi][:CHUNK] * decays[gi]).astype(BF16))
        lmats.append(jnp.where(strict_w, grams[gi][CHUNK:] * decays[gi], 0.0))
    ainvs = _tri_inverse_wide(lmats, eye_w, bdn_hi, bdn_lo)
    eg_xs, kd_ts = [], []
    for gi, (bi, d, g) in enumerate(probs):
        key = (bi, d)
        cols = cols_of(d, g)
        v = qkv[d][bi, :, 2 * A_W + g * gw:2 * A_W + (g + 1) * gw]
        eg_x = jnp.concatenate([_col_bcast(eg[key], c) for c in cols], axis=1)
        egl_x = jnp.concatenate([_col_bcast(egl[key], c) for c in cols], axis=1)
        eg_xs.append(eg_x)
        _store_blocks(bduw.at[gi], (v * beta_xs[gi]).astype(BF16), CHUNK, DV_A)
        _store_blocks(bduw.at[gi], (ks[gi] * (beta_xs[gi] * eg_x)).astype(BF16), CHUNK, DK_A, col0=gw)
        kd = ks[gi] * egl_x
        kd_ts.append(jnp.concatenate([kd[:, hl * DK_A:(hl + 1) * DK_A] for hl in range(GROUP)], axis=0)
                     .T.astype(BF16))
        _store_blocks(bds.at[gi], s_scr[bi, d, g].astype(BF16), DK_A, DV_A)
    uws = []
    for gi in range(len(probs)):
        t_hi, t_lo = _split(ainvs[gi])
        uws.append(jnp.dot(jnp.concatenate([t_hi, t_lo], axis=1), jnp.concatenate([bduw[gi], bduw[gi]], axis=0),
                           preferred_element_type=F32))
    ws_qs = []
    hw = gw // 2
    for gi in range(len(probs)):
        wq = jnp.concatenate([uws[gi][:, gw:].astype(BF16), q16[gi]], axis=0)
        ws_qs.append(jnp.concatenate(
            [jnp.dot(wq[:, p * hw:(p + 1) * hw], bds[gi, p * hw:(p + 1) * hw, p * hw:(p + 1) * hw],
                     preferred_element_type=F32) for p in range(2)], axis=1))
    for gi in range(len(probs)):
        _store_blocks(bdv.at[gi], (uws[gi][:, :gw] - ws_qs[gi][:CHUNK]).astype(BF16), CHUNK, DV_A)
    rs = []
    for gi in range(len(probs)):
        rs.append(jnp.dot(jnp.concatenate([attns[gi], kd_ts[gi]], axis=0), bdv[gi], preferred_element_type=F32))
    for gi, (bi, d, g) in enumerate(probs):
        cols = cols_of(d, g)
        last = CHUNK - 1 if d == 0 else 0
        o = ws_qs[gi][CHUNK:] * eg_xs[gi] + rs[gi][:CHUNK]
        eg_last = jnp.concatenate(
            [jnp.broadcast_to(eg[bi, d][last:last + 1, c:c + 1], (1, DV_A)) for c in cols], axis=1)
        s_scr[bi, d, g] = s_scr[bi, d, g] * eg_last + rs[gi][CHUNK:]
        cidx = n if d == 0 else n_chunks - 1 - n
        rows = pl.ds(pl.multiple_of(cidx * CHUNK, CHUNK), CHUNK)
        tot = o + o_scr[bi, rows, g * gw:(g + 1) * gw]
        o_scr[bi, rows, g * gw:(g + 1) * gw] = tot
        for hl in range(GROUP):
            hs = slice((g * GROUP + hl) * DV_A, (g * GROUP + hl + 1) * DV_A)
            blk = tot[:, hl * DV_A:(hl + 1) * DV_A]
            ya_ref[bi, rows, hs] = (_rms(blk) * norm_ref[...] * ag[d][bi, :, hs]).astype(BF16)

    if emit_state:
        @pl.when(n == n_chunks - 1)
        def _():
            for bi in range(bb):
                for d in range(N_DIR):
                    for h in range(H_A):
                        sout_ref[bi, d, h] = s_scr[bi, d, h // GROUP, :, (h % GROUP) * DV_A:(h % GROUP + 1) * DV_A]


SCAN_BB = 2


def _scan_specs(n_chunks, width):
    return (pl.BlockSpec((SCAN_BB, CHUNK, width), lambda b, n: (b, n, 0)),
            pl.BlockSpec((SCAN_BB, CHUNK, width), lambda b, n: (b, n_chunks - 1 - n, 0)))


def _delta(qkv, rest, gates, norm_a, s0, batch, n_chunks, emit_state):
    t = n_chunks * CHUNK
    has_init = s0 is not None
    bb = SCAN_BB
    ng = bb * N_DIR * N_GROUPS
    qkv, rest, gates = (a.reshape(batch, t, a.shape[-1]) for a in (qkv, rest, gates))
    in_specs = [*_scan_specs(n_chunks, QKV_W), *_scan_specs(n_chunks, A_W), *_scan_specs(n_chunks, LANES),
                pl.BlockSpec((1, DV_A), lambda b, n: (0, 0))]
    args = [qkv, qkv, rest, rest, gates, gates, norm_a]
    sspec = pl.BlockSpec((bb, N_DIR, H_A, DK_A, DV_A), lambda b, n: (b, 0, 0, 0, 0))
    if has_init:
        in_specs.append(sspec)
        args.append(s0)
    out_specs = [pl.BlockSpec((bb, t, A_W), lambda b, n: (b, 0, 0))]
    out_shape = [jax.ShapeDtypeStruct((batch, t, A_W), BF16)]
    if emit_state:
        out_specs.append(sspec)
        out_shape.append(jax.ShapeDtypeStruct((batch, N_DIR, H_A, DK_A, DV_A), F32))
    out = pl.pallas_call(
        functools.partial(_delta_body, n_chunks=n_chunks, bb=bb, has_init=has_init, emit_state=emit_state),
        grid=(batch // bb, n_chunks),
        in_specs=in_specs,
        out_specs=out_specs,
        out_shape=out_shape,
        scratch_shapes=[pltpu.VMEM((bb, N_DIR, N_GROUPS, DK_A, GROUP * DV_A), F32), pltpu.VMEM((bb, t, A_W), F32),
                        pltpu.VMEM((ng, GROUP * CHUNK, GROUP * CHUNK), BF16),
                        pltpu.VMEM((ng, GROUP * CHUNK, GROUP * CHUNK), BF16),
                        pltpu.VMEM((ng, GROUP * CHUNK, GROUP * DK_A), BF16),
                        pltpu.VMEM((ng, GROUP * CHUNK, 2 * GROUP * DK_A), BF16),
                        pltpu.VMEM((ng, GROUP * DK_A, GROUP * DV_A), BF16),
                        pltpu.VMEM((ng, GROUP * CHUNK, GROUP * DV_A), BF16)],
        compiler_params=_params(("arbitrary", "arbitrary")),
        name="delta_scan",
    )(*args)
    return [out[0].reshape(batch * t, A_W), *out[1:]]


CAUG_W = DV_B + LANES
MW = H_B * CAUG_W


def _scan_max(x, d):
    row = lax.broadcasted_iota(jnp.int32, x.shape, 0)
    s = 1
    while s < CHUNK:
        if d == 0:
            shifted = jnp.where(row >= s, pltpu.roll(x, s, 0), NEG)
        else:
            shifted = jnp.where(row < CHUNK - s, pltpu.roll(x, CHUNK - s, 0), NEG)
        x = jnp.maximum(x, shifted)
        s *= 2
    return x


def _mlstm_body(*refs, n_chunks, bb, has_init, emit_state):
    rest = refs[0:2]
    gt = refs[2:4]
    norm_ref = refs[4]
    pos = 5
    c0_ref = m0_ref = None
    if has_init:
        c0_ref, m0_ref = refs[pos], refs[pos + 1]
        pos += 2
    yb_ref = refs[pos]
    pos += 1
    cout_ref = nout_ref = mout_ref = None
    if emit_state:
        cout_ref, nout_ref, mout_ref = refs[pos:pos + 3]
        pos += 3
    c_scr, m_scr, o_scr, bdk, bdc, bdv = refs[pos:pos + 6]

    n = pl.program_id(1)
    qw = H_B * DK_B

    @pl.when((n == 0) & (pl.program_id(0) == 0))
    def _():
        for ref in (bdk, bdc, bdv):
            ref[...] = jnp.zeros_like(ref)
        for i in range(bb * N_DIR):
            for h in range(H_B):
                bdv[i, h * CHUNK:(h + 1) * CHUNK, h * CAUG_W + DV_B:(h + 1) * CAUG_W] = jnp.ones((CHUNK, LANES), BF16)

    @pl.when(n == 0)
    def _():
        o_scr[...] = jnp.zeros_like(o_scr)
        if has_init:
            c_scr[...] = c0_ref[...]
            m_scr[...] = m0_ref[...]
        else:
            c_scr[...] = jnp.zeros_like(c_scr)
            m_scr[...] = jnp.zeros_like(m_scr)

    row = lax.broadcasted_iota(jnp.int32, (CHUNK, LANES), 0)
    lane = lax.broadcasted_iota(jnp.int32, (CHUNK, LANES), 1)
    col = lane & (CHUNK - 1)
    left = lane < CHUNK
    dirs = [(bi, d) for bi in range(bb) for d in range(N_DIR)]
    slot = {key: i for i, key in enumerate(dirs)}
    nc, a_t, iw, emt, ksc, dec_row = {}, {}, {}, {}, {}, {}
    for key in dirs:
        bi, d = key
        lo = LF_OFF + d * H_B
        mine = (lane >= lo) & (lane < lo + H_B)
        g = gt[d][bi]
        gc = jnp.where(mine, _dot_f32(_masks(d)[0].astype(F32), g), 0.0)
        a = jnp.where(mine, pltpu.roll(g, LF_OFF - LI_OFF, 1), 0.0) - gc
        last = CHUNK - 1 if d == 0 else 0
        m_old = m_scr[bi, d][0:1, :]
        mx = jnp.maximum(m_old, _scan_max(a, d))
        mxl = mx[last:last + 1, :]
        nc[key] = -mx
        a_t[key] = jnp.concatenate([a, a], axis=0).T
        iw[key] = jnp.exp(m_old - mx)
        emt[key] = jnp.exp(-(gc + mx))
        ksc[key] = jnp.exp(a - mxl)
        dec_row[key] = jnp.exp(m_old - mxl)
        m_scr[bi, d] = jnp.broadcast_to(gc[last:last + 1, :] + mxl, (8, LANES))

    q16, ks_t = {}, {}
    for key in dirs:
        bi, d = key
        i = slot[key]
        lo = LF_OFF + d * H_B
        q16[key] = rest[d][bi, :, OFF_BQ:OFF_BQ + qw].astype(BF16)
        k = rest[d][bi, :, OFF_BK:OFF_BK + qw]
        _store_blocks(bdk.at[i], k.astype(BF16), CHUNK, DK_B)
        ks = k * jnp.concatenate([_col_bcast(ksc[key], lo + h) for h in range(H_B)], axis=1)
        ks_t[key] = jnp.concatenate([ks[:, h * DK_B:(h + 1) * DK_B] for h in range(H_B)], axis=0).T.astype(BF16)
        for h in range(H_B):
            bdv[i, h * CHUNK:(h + 1) * CHUNK, h * CAUG_W:h * CAUG_W + DV_B] = (
                rest[d][bi, :, OFF_BV + h * DV_B:OFF_BV + (h + 1) * DV_B].astype(BF16))
        _store_blocks(bdc.at[i], c_scr[bi, d].astype(BF16), DK_B, CAUG_W)
    qk = {key: lax.dot_general(q16[key], bdk[slot[key]], (((1,), (1,)), ((), ())), preferred_element_type=F32)
          for key in dirs}
    lhs = {}
    for key in dirs:
        bi, d = key
        lo = LF_OFF + d * H_B
        incl = row >= col if d == 0 else row <= col
        log_w = []
        for p in range(H_B // PAIR):
            l0, l1 = lo + PAIR * p, lo + PAIR * p + 1
            ccol = jnp.where(left, _col_bcast(nc[key], l0), _col_bcast(nc[key], l1))
            crow = jnp.where(left[0:1], a_t[key][l0:l0 + 1, :], a_t[key][l1:l1 + 1, :])
            log_w.append(jnp.where(incl, ccol + crow, NEG))
        dw = jnp.exp(jnp.concatenate(log_w, axis=1)) * qk[key]
        iw_x = jnp.concatenate([_col_bcast(iw[key], lo + h) for h in range(H_B)], axis=1)
        lhs[key] = ((rest[d][bi, :, OFF_BQ:OFF_BQ + qw] * iw_x).astype(BF16), dw.astype(BF16))
    num = {}
    for key in dirs:
        i = slot[key]
        parts = []
        for p in range(H_B // PAIR):
            ql, dl, cl = slice(p * PAIR * DK_B, (p + 1) * PAIR * DK_B), slice(p * LANES, (p + 1) * LANES), \
                slice(p * PAIR * CAUG_W, (p + 1) * PAIR * CAUG_W)
            parts.append(jnp.dot(jnp.concatenate([lhs[key][0][:, ql], lhs[key][1][:, dl]], axis=1),
                                 jnp.concatenate([bdc[i, ql, cl], bdv[i, dl, cl]], axis=0),
                                 preferred_element_type=F32))
        num[key] = jnp.concatenate(parts, axis=1)
    upd = {key: jnp.dot(ks_t[key], bdv[slot[key]], preferred_element_type=F32) for key in dirs}
    for key in dirs:
        bi, d = key
        lo = LF_OFF + d * H_B
        dec_x = jnp.concatenate(
            [jnp.broadcast_to(dec_row[key][:, lo + h:lo + h + 1], (1, CAUG_W)) for h in range(H_B)], axis=1)
        c_scr[bi, d] = c_scr[bi, d] * dec_x + upd[key]
        cidx = n if d == 0 else n_chunks - 1 - n
        rows = pl.ds(pl.multiple_of(cidx * CHUNK, CHUNK), CHUNK)
        for h in range(H_B):
            vs = slice(h * DV_B, (h + 1) * DV_B)
            den = jnp.maximum(jnp.abs(num[key][:, h * CAUG_W + DV_B:(h + 1) * CAUG_W]), _col_bcast(emt[key], lo + h))
            hb = jnp.concatenate([num[key][:, h * CAUG_W:h * CAUG_W + LANES] / den,
                                  num[key][:, h * CAUG_W + LANES:h * CAUG_W + DV_B] / den], axis=1)
            tot = hb + o_scr[bi, rows, vs]
            o_scr[bi, rows, vs] = tot
            ogate = rest[d][bi, :, OFF_BO + h * DV_B:OFF_BO + (h + 1) * DV_B]
            yb_ref[bi, rows, vs] = (_rms(tot) * norm_ref[...] * ogate).astype(BF16)

    if emit_state:
        @pl.when(n == n_chunks - 1)
        def _():
            for bi, d in dirs:
                for h in range(H_B):
                    cout_ref[bi, d, h] = c_scr[bi, d, :, h * CAUG_W:h * CAUG_W + DV_B]
                    nout_ref[bi, d, h] = c_scr[bi, d, :, h * CAUG_W + DV_B:(h + 1) * CAUG_W]
            mout_ref[...] = m_scr[...]


def _mlstm(rest, gates, norm_b, c0, m0, batch, n_chunks, emit_state):
    t = n_chunks * CHUNK
    has_init = c0 is not None
    bb = SCAN_BB
    rest, gates = (a.reshape(batch, t, a.shape[-1]) for a in (rest, gates))
    mspec = pl.BlockSpec((bb, N_DIR, 8, LANES), lambda b, n: (b, 0, 0, 0))
    in_specs = [*_scan_specs(n_chunks, REST_W), *_scan_specs(n_chunks, LANES),
                pl.BlockSpec((1, DV_B), lambda b, n: (0, 0))]
    args = [rest, rest, gates, gates, norm_b]
    if has_init:
        in_specs += [pl.BlockSpec((bb, N_DIR, DK_B, MW), lambda b, n: (b, 0, 0, 0)), mspec]
        args += [c0, m0]
    out_specs = [pl.BlockSpec((bb, t, B_W), lambda b, n: (b, 0, 0))]
    out_shape = [jax.ShapeDtypeStruct((batch, t, B_W), BF16)]
    if emit_state:
        out_specs += [pl.BlockSpec((bb, N_DIR, H_B, DK_B, DV_B), lambda b, n: (b, 0, 0, 0, 0)),
                      pl.BlockSpec((bb, N_DIR, H_B, DK_B, LANES), lambda b, n: (b, 0, 0, 0, 0)), mspec]
        out_shape += [jax.ShapeDtypeStruct((batch, N_DIR, H_B, DK_B, DV_B), F32),
                      jax.ShapeDtypeStruct((batch, N_DIR, H_B, DK_B, LANES), F32),
                      jax.ShapeDtypeStruct((batch, N_DIR, 8, LANES), F32)]
    out = pl.pallas_call(
        functools.partial(_mlstm_body, n_chunks=n_chunks, bb=bb, has_init=has_init, emit_state=emit_state),
        grid=(batch // bb, n_chunks),
        in_specs=in_specs,
        out_specs=out_specs,
        out_shape=out_shape,
        scratch_shapes=[pltpu.VMEM((bb, N_DIR, DK_B, MW), F32),
                        pltpu.VMEM((bb, N_DIR, 8, LANES), F32),
                        pltpu.VMEM((bb, t, B_W), F32),
                        pltpu.VMEM((bb * N_DIR, H_B * CHUNK, H_B * DK_B), BF16),
                        pltpu.VMEM((bb * N_DIR, H_B * DK_B, MW), BF16),
                        pltpu.VMEM((bb * N_DIR, H_B * CHUNK, MW), BF16)],
        compiler_params=_params(("arbitrary", "arbitrary")),
        name="mlstm_scan",
    )(*args)
    return [out[0].reshape(batch * t, B_W), *out[1:]]


OUT_RB = 256


def _outproj_body(ya_ref, yb_ref, wa_ref, wb_ref, x_ref, mod_ref, post1_ref, pre2_ref, x1_ref, h2_ref):
    for rb in range(x_ref.shape[0] // OUT_RB):
        rs = slice(rb * OUT_RB, (rb + 1) * OUT_RB)
        mix = (jnp.dot(ya_ref[rs, :], wa_ref[...], preferred_element_type=F32)
               + jnp.dot(yb_ref[rs, :], wb_ref[...], preferred_element_type=F32))
        x1 = x_ref[rs, :] + mod_ref[0, 2:3, :] * (_rms(mix) * post1_ref[...])
        x1_ref[rs, :] = x1
        h2 = _rms(x1) * pre2_ref[...] * (1.0 + mod_ref[0, 4:5, :]) + mod_ref[0, 3:4, :]
        h2_ref[rs, :] = h2.astype(BF16)


def _outproj(ya, yb, w_out, x2d, mod3, mod_map, post1, pre2, tm):
    m = x2d.shape[0]
    row = lambda i: (i, 0)
    const = lambda i: (0, 0)
    return pl.pallas_call(
        _outproj_body,
        grid=(m // tm,),
        in_specs=[pl.BlockSpec((tm, A_W), row), pl.BlockSpec((tm, B_W), row),
                  pl.BlockSpec((A_W, D_MODEL), lambda i: (0, 0)),
                  pl.BlockSpec((B_W, D_MODEL), lambda i: (1, 0)),
                  pl.BlockSpec((tm, D_MODEL), row),
                  pl.BlockSpec((1, 6, D_MODEL), lambda i: (mod_map(i * tm), 0, 0)),
                  pl.BlockSpec((1, D_MODEL), const), pl.BlockSpec((1, D_MODEL), const)],
        out_specs=[pl.BlockSpec((tm, D_MODEL), row), pl.BlockSpec((tm, D_MODEL), row)],
        out_shape=[jax.ShapeDtypeStruct((m, D_MODEL), F32), jax.ShapeDtypeStruct((m, D_MODEL), BF16)],
        compiler_params=_params(("parallel",)),
        name="outproj",
    )(ya, yb, w_out, w_out, x2d, mod3, post1, pre2)


def _ffn_body(h2_ref, w1_ref, w2_ref, x1_ref, mod_ref, post2_ref, o_ref):
    kk = pl.program_id(1)
    a = jnp.maximum(jnp.dot(h2_ref[...], w1_ref[...], preferred_element_type=F32), 0.0)
    contrib = jnp.dot((a * a).astype(BF16), w2_ref[...], preferred_element_type=F32)

    @pl.when(kk == 0)
    def _():
        o_ref[...] = contrib

    @pl.when((kk > 0) & (kk < pl.num_programs(1) - 1))
    def _():
        o_ref[...] += contrib

    @pl.when(kk == pl.num_programs(1) - 1)
    def _():
        f = o_ref[...] + contrib
        o_ref[...] = x1_ref[...] + mod_ref[0, 5:6, :] * (_rms(f) * post2_ref[...])


def _ffn(h2, w1, w2, x1, mod3, mod_map, post2, tm, fc):
    m = h2.shape[0]
    return pl.pallas_call(
        _ffn_body,
        grid=(m // tm, FFN // fc),
        in_specs=[pl.BlockSpec((tm, D_MODEL), lambda i, k: (i, 0)),
                  pl.BlockSpec((D_MODEL, fc), lambda i, k: (0, k)),
                  pl.BlockSpec((fc, D_MODEL), lambda i, k: (k, 0)),
                  pl.BlockSpec((tm, D_MODEL), lambda i, k: (i, 0)),
                  pl.BlockSpec((1, 6, D_MODEL), lambda i, k: (mod_map(i * tm), 0, 0)),
                  pl.BlockSpec((1, D_MODEL), lambda i, k: (0, 0))],
        out_specs=pl.BlockSpec((tm, D_MODEL), lambda i, k: (i, 0)),
        out_shape=jax.ShapeDtypeStruct((m, D_MODEL), F32),
        compiler_params=_params(("parallel", "arbitrary")),
        name="ffn",
    )(h2, w1, w2, x1, mod3, post2)


def _block(x, mod3, mod_of_row, lp, init, seq_len, emit_state):
    bsz, t, _ = x.shape
    x2d = x.reshape(bsz * t, D_MODEL)
    n_chunks = t // CHUNK
    qkv, rest, gates = _proj(x2d, mod3, lambda i: mod_of_row(i * PROJ_TM), lp["pre1"], lp["w16"],
                             lp["w_mid"], lp["w_gate"], lp["conv_w"], lp["gate_p"], seq_len)
    s0, c0, m0 = init if init is not None else (None, None, None)
    d_out = _delta(qkv, rest, gates, lp["norm_a"], s0, bsz, n_chunks, emit_state)
    m_out = _mlstm(rest, gates, lp["norm_b"], c0, m0, bsz, n_chunks, emit_state)
    x1, h2 = _outproj(d_out[0], m_out[0], lp["w_out"], x2d, mod3, mod_of_row, lp["post1"], lp["pre2"], 512)
    y = _ffn(h2, lp["w1"], lp["w2"], x1, mod3, mod_of_row, lp["post2"], 512, 1024)
    states = None
    if emit_state:
        m_fin = jnp.stack([m_out[3][:, d, 0, LF_OFF + d * H_B:LF_OFF + (d + 1) * H_B] for d in range(N_DIR)], axis=1)
        states = (d_out[1], m_out[1], m_out[2][..., 0], m_fin)
    return y.reshape(bsz, t, D_MODEL), states


def _layer_params(l, norm_mix_pre, norm_mix_post, norm_ffn_pre, norm_ffn_post, w_in, conv_w, a_log, dt_bias,
                  norm_a, mlstm_ibias, mlstm_fbias, norm_b, w_out, w_ffn1, w_ffn2):
    w = w_in[l]
    o_ag = QKV_W
    o_aa = o_ag + A_W
    o_ab = o_aa + N_DIR * H_A
    o_bq = o_ab + N_DIR * H_A
    o_bi = o_bq + 2 * H_B * DK_B + 2 * B_W
    o_bf = o_bi + N_DIR * H_B
    n_gate = 2 * N_DIR * H_A + 2 * N_DIR * H_B
    w16 = w.astype(BF16)
    w_gate = jnp.concatenate([w16[:, o_aa:o_bq], w16[:, o_bi:o_bf + N_DIR * H_B],
                              jnp.zeros((D_MODEL, LANES - n_gate), BF16)], axis=1)

    def lane_row(vals, off):
        return jnp.zeros((LANES,), F32).at[off:off + vals.size].set(vals.reshape(-1))

    gate_p = jnp.stack([lane_row(a_log[l], G_OFF), lane_row(dt_bias[l], G_OFF),
                        lane_row(mlstm_ibias[l], LI_OFF) + lane_row(mlstm_fbias[l], LF_OFF)]
                       + [jnp.zeros((LANES,), F32)] * 5)
    row = lambda v: v[l].reshape(1, -1)
    return dict(
        pre1=row(norm_mix_pre), post1=row(norm_mix_post), pre2=row(norm_ffn_pre), post2=row(norm_ffn_post),
        w16=w16, w_mid=w16[:, o_bq:o_bi], w_gate=w_gate, gate_p=gate_p,
        conv_w=jnp.concatenate([conv_w[l].T, jnp.zeros((5, QKV_W), F32)], axis=0),
        norm_a=row(norm_a), norm_b=row(norm_b),
        w_out=w_out[l].astype(BF16), w1=w_ffn1[l].astype(BF16), w2=w_ffn2[l].astype(BF16))


def kernel(x_prompt, x_sample, state_delta, state_mlstm_C, state_mlstm_n, state_mlstm_m, c, c_ctx, w_ada, b_ada, norm_mix_pre, norm_mix_post, norm_ffn_pre, norm_ffn_post, w_in, conv_w, a_log, dt_bias, norm_a, mlstm_ibias, mlstm_fbias, norm_b, w_out, w_ffn1, w_ffn2):
    depth = w_in.shape[0]
    n_lat = x_sample.shape[0]
    t_lat = x_sample.shape[1]
    cond = jnp.concatenate([c_ctx[None, :], c, jnp.zeros((8 - 1 - n_lat, D_MODEL), F32)], axis=0)
    y_prompt, y_sample = x_prompt, x_sample
    acc = ([], [], [], [])
    for l in range(depth):
        lp = _layer_params(l, norm_mix_pre, norm_mix_post, norm_ffn_pre, norm_ffn_post, w_in, conv_w, a_log,
                           dt_bias, norm_a, mlstm_ibias, mlstm_fbias, norm_b, w_out, w_ffn1, w_ffn2)
        mod = _ada(cond, w_ada, b_ada[l].reshape(1, -1), l)
        mod3 = mod[:1 + n_lat].reshape(1 + n_lat, 6, D_MODEL)
        y_prompt, st = _block(y_prompt, mod3, lambda r: 0, lp, None, x_prompt.shape[1], True)
        for a, s in zip(acc, st):
            a.append(s)
        n_rep = jnp.broadcast_to(state_mlstm_n[:, l][..., None], state_mlstm_n[:, l].shape + (LANES,))
        c_aug0 = jnp.concatenate([state_mlstm_C[:, l], n_rep], axis=-1)
        c_aug0 = c_aug0.transpose(0, 1, 3, 2, 4).reshape(n_lat, N_DIR, DK_B, MW)
        m0 = jnp.zeros((n_lat, N_DIR, LANES), F32)
        for d in range(N_DIR):
            m0 = m0.at[:, d, LF_OFF + d * H_B:LF_OFF + (d + 1) * H_B].set(state_mlstm_m[:, l, d])
        m0 = jnp.broadcast_to(m0[:, :, None, :], (n_lat, N_DIR, 8, LANES))
        y_sample, _ = _block(y_sample, mod3, lambda r: 1 + r // t_lat, lp, (state_delta[:, l], c_aug0, m0),
                             GRID_W, False)
    return (y_prompt, y_sample) + tuple(jnp.stack(a, axis=1) for a in acc)
```

```python
import functools

import jax
import jax.numpy as jnp
from jax import lax
from jax.experimental import pallas as pl
from jax.experimental.pallas import tpu as pltpu

F32 = jnp.float32
BF16 = jnp.bfloat16

D_MODEL = 2048
N_DIR = 2
A_W = D_MODEL // 2
B_W = D_MODEL - A_W
DK_A = 128
DV_A = 128
H_A = A_W // DV_A
DV_B = 256
DK_B = DV_B // 2
H_B = B_W // DV_B
GRID_W = 64
CHUNK = 64
FFN = 4 * D_MODEL
EPS = 1e-6
LANES = 128
NEG = -1e30

QKV_W = 3 * A_W
REST_W = A_W + 2 * H_B * DK_B + 2 * B_W
OFF_AG, OFF_BQ, OFF_BK, OFF_BV, OFF_BO = 0, A_W, A_W + H_B * DK_B, A_W + 2 * H_B * DK_B, A_W + 2 * H_B * DK_B + B_W
G_OFF, BETA_OFF = 0, N_DIR * H_A
LI_OFF = 2 * N_DIR * H_A
LF_OFF = LI_OFF + N_DIR * H_B

VMEM_LIMIT = 56 * 1024 * 1024


def _sigmoid(x):
    return 1.0 / (1.0 + jnp.exp(-x))


def _softplus(x):
    return jnp.maximum(x, 0.0) + jnp.log1p(jnp.exp(-jnp.abs(x)))


def _dot(a, b):
    return jnp.dot(a.astype(BF16), b.astype(BF16), preferred_element_type=F32)


def _dot_f32(a, b):
    return jnp.dot(a, b, precision=lax.Precision.HIGHEST, preferred_element_type=F32)


def _rms(x):
    return x * lax.rsqrt(jnp.mean(x * x, axis=-1, keepdims=True) + EPS)


def _params(sem):
    return pltpu.CompilerParams(dimension_semantics=sem, vmem_limit_bytes=VMEM_LIMIT)


def _ada_body(c_ref, w_ref, b_ref, o_ref):
    c = c_ref[...]
    o_ref[...] = _dot(c * _sigmoid(c), w_ref[...]) + b_ref[...]


def _ada(c_all, w_ada, b, l):
    n = w_ada.shape[2]
    tn = 1024
    return pl.pallas_call(
        _ada_body,
        grid=(n // tn,),
        in_specs=[pl.BlockSpec(c_all.shape, lambda j: (0, 0)),
                  pl.BlockSpec((None, D_MODEL, tn), lambda j: (l, 0, j)),
                  pl.BlockSpec((1, tn), lambda j: (0, j))],
        out_specs=pl.BlockSpec((c_all.shape[0], tn), lambda j: (0, j)),
        out_shape=jax.ShapeDtypeStruct((c_all.shape[0], n), F32),
        compiler_params=_params(("arbitrary",)),
        name="ada",
    )(c_all, w_ada, b)


PROJ_TM = 1024
PROJ_RB = 256
PROJ_TN = 512
N_QKV_T = QKV_W // PROJ_TN
N_REST_T = REST_W // PROJ_TN
N_HEAD_T = (QKV_W + A_W) // PROJ_TN
N_MID_T = N_QKV_T + N_REST_T - N_HEAD_T


def _proj_body(x_ref, mod_ref, g_ref, wh_ref, wb_ref, wg_ref, cw_ref, gp_ref, qkv_ref, rest_ref, gate_ref, h_scr, *,
               seq_len):
    j = pl.program_id(1)
    blocks = [slice(rb * PROJ_RB, (rb + 1) * PROJ_RB) for rb in range(PROJ_TM // PROJ_RB)]

    def run(w_ref, epilogue, out_ref):
        for rs in blocks:
            epilogue(jnp.dot(h_scr[rs, :], w_ref[...], preferred_element_type=F32), out_ref, rs)

    def run_first():
        for rs in blocks:
            y = _rms(x_ref[rs, :]) * g_ref[...]
            h_scr[rs, :] = (y * (1.0 + mod_ref[0, 1:2, :]) + mod_ref[0, 0:1, :]).astype(BF16)
            ep_l2(DK_A ** -0.5)(jnp.dot(h_scr[rs, :], wh_ref[...], preferred_element_type=F32), qkv_ref, rs)
            ep_gate(jnp.dot(h_scr[rs, :], wg_ref[...], preferred_element_type=F32), gate_ref, rs)

    def conv_silu(acc, ls):
        a = acc[:, ls]
        pos = lax.broadcasted_iota(jnp.int32, a.shape, 0) & (seq_len - 1)
        prev = jnp.where(pos == 0, 0.0, pltpu.roll(a, 1, 0))
        nxt = jnp.where(pos == seq_len - 1, 0.0, pltpu.roll(a, a.shape[0] - 1, 0))
        y = prev * cw_ref[0:1, ls] + a * cw_ref[1:2, ls] + nxt * cw_ref[2:3, ls]
        return y * _sigmoid(y)

    lane_groups = [slice(g * LANES, (g + 1) * LANES) for g in range(PROJ_TN // LANES)]

    def ep_l2(scale):
        def f(acc, out_ref, rs):
            for ls in lane_groups:
                blk = conv_silu(acc, ls)
                inv = lax.rsqrt(jnp.sum(blk * blk, axis=-1, keepdims=True) + EPS)
                out_ref[rs, ls] = blk * (inv * scale)
        return f

    def ep_conv(acc, out_ref, rs):
        for ls in lane_groups:
            out_ref[rs, ls] = conv_silu(acc, ls)

    def ep_map(fn):
        def f(acc, out_ref, rs):
            out_ref[rs, :] = fn(acc)
        return f

    def ep_gate(z, out_ref, rs):
        lane = lax.broadcasted_iota(jnp.int32, z.shape, 1)
        g = -jnp.exp(gp_ref[0:1, :]) * _softplus(z + gp_ref[1:2, :])
        li = z + gp_ref[2:3, :]
        out_ref[rs, :] = jnp.where(lane < BETA_OFF, g,
                                   jnp.where(lane < LI_OFF, _sigmoid(z),
                                             jnp.where(lane < LF_OFF, li,
                                                       jnp.where(lane < LF_OFF + N_DIR * H_B, -_softplus(-li), 0.0))))

    tp = A_W // PROJ_TN
    c0 = (j - N_QKV_T) * PROJ_TN
    in_rest = (j >= N_QKV_T) & (j < N_QKV_T + N_REST_T)
    pl.when(j == 0)(run_first)
    variants = [
        ((j > 0) & (j < tp), wh_ref, ep_l2(DK_A ** -0.5), qkv_ref),
        ((j >= tp) & (j < 2 * tp), wh_ref, ep_l2(1.0), qkv_ref),
        ((j >= 2 * tp) & (j < N_QKV_T), wh_ref, ep_conv, qkv_ref),
        (in_rest & (c0 < OFF_BQ), wh_ref, ep_map(lambda a: a * _sigmoid(a)), rest_ref),
        (in_rest & (c0 >= OFF_BQ) & (c0 < OFF_BK), wb_ref, ep_map(lambda a: a * (DK_B ** -0.5)), rest_ref),
        (in_rest & (c0 >= OFF_BK) & (c0 < OFF_BO), wb_ref, ep_map(lambda a: a), rest_ref),
        (in_rest & (c0 >= OFF_BO), wb_ref, ep_map(_sigmoid), rest_ref),
    ]
    for cond, w_ref, epilogue, out_ref in variants:
        pl.when(cond)(functools.partial(run, w_ref, epilogue, out_ref))


def _proj(x2d, mod3, mod_map, pre_g, w16, w_mid, w_gate, conv_w, gate_p, seq_len):
    m = x2d.shape[0]
    tm, tn = PROJ_TM, PROJ_TN
    assert PROJ_RB % seq_len == 0 and m % tm == 0
    return pl.pallas_call(
        functools.partial(_proj_body, seq_len=seq_len),
        grid=(m // tm, N_HEAD_T + N_MID_T),
        in_specs=[pl.BlockSpec((tm, D_MODEL), lambda i, j: (i, 0)),
                  pl.BlockSpec((1, 6, D_MODEL), lambda i, j: (mod_map(i), 0, 0)),
                  pl.BlockSpec((1, D_MODEL), lambda i, j: (0, 0)),
                  pl.BlockSpec((D_MODEL, tn), lambda i, j: (0, jnp.minimum(j, N_HEAD_T - 1))),
                  pl.BlockSpec((D_MODEL, tn), lambda i, j: (0, jnp.clip(j - N_HEAD_T, 0, N_MID_T - 1))),
                  pl.BlockSpec((D_MODEL, LANES), lambda i, j: (0, 0)),
                  pl.BlockSpec((8, tn), lambda i, j: (0, jnp.minimum(j, N_QKV_T - 1))),
                  pl.BlockSpec((8, LANES), lambda i, j: (0, 0))],
        out_specs=[pl.BlockSpec((tm, tn), lambda i, j: (i, jnp.minimum(j, N_QKV_T - 1))),
                   pl.BlockSpec((tm, tn), lambda i, j: (i, jnp.clip(j - N_QKV_T, 0, N_REST_T - 1))),
                   pl.BlockSpec((tm, LANES), lambda i, j: (i, 0))],
        out_shape=[jax.ShapeDtypeStruct((m, QKV_W), F32), jax.ShapeDtypeStruct((m, REST_W), F32),
                   jax.ShapeDtypeStruct((m, LANES), F32)],
        scratch_shapes=[pltpu.VMEM((tm, D_MODEL), BF16)],
        compiler_params=_params(("parallel", "arbitrary")),
        name="proj",
    )(x2d, mod3, pre_g, w16, w_mid, w_gate, conv_w, gate_p)


def _masks(d):
    row = lax.broadcasted_iota(jnp.int32, (CHUNK, CHUNK), 0)
    col = lax.broadcasted_iota(jnp.int32, (CHUNK, CHUNK), 1)
    if d == 0:
        return row >= col, row > col, row == col
    return row <= col, row < col, row == col


GROUP = 4
N_GROUPS = H_A // GROUP
PAIR = LANES // CHUNK


def _split(x):
    hi = x.astype(BF16)
    return hi, (x - hi.astype(F32)).astype(BF16)


def _store_blocks(ref, x, blk_r, blk_c, col0=0):
    for h in range(GROUP):
        ref[h * blk_r:(h + 1) * blk_r, col0 + h * blk_c:col0 + (h + 1) * blk_c] = x[:, h * blk_c:(h + 1) * blk_c]


def _dot3(a_hi, a_lo, b_hi, b_lo):
    lhs = jnp.concatenate([a_hi, a_lo, a_hi], axis=1)
    rhs = jnp.concatenate([b_hi, b_hi, b_lo], axis=0)
    return jnp.dot(lhs, rhs, preferred_element_type=F32)


def _tri_inverse_wide(lmats, eye_w, bd_hi, bd_lo):
    def bd_dot(i, a_hi, a_lo, b_hi, b_lo):
        _store_blocks(bd_hi.at[i], b_hi, CHUNK, CHUNK)
        _store_blocks(bd_lo.at[i], b_lo, CHUNK, CHUNK)
        return _dot3(a_hi, a_lo, bd_hi[i], bd_lo[i])

    idx = range(len(lmats))
    s = [eye_w - l for l in lmats]
    p = []
    for i in idx:
        m_hi, m_lo = _split(-lmats[i])
        p.append(bd_dot(i, m_hi, m_lo, m_hi, m_lo))
    for _ in range(4):
        for i in idx:
            p_hi, p_lo = _split(p[i])
            s_hi, s_lo = _split(s[i])
            r = bd_dot(i, jnp.concatenate([p_hi, s_hi], axis=0), jnp.concatenate([p_lo, s_lo], axis=0), p_hi, p_lo)
            p[i] = r[:CHUNK]
            s[i] = s[i] + r[CHUNK:]
    for i in idx:
        p_hi, p_lo = _split(p[i])
        s_hi, s_lo = _split(s[i])
        s[i] = s[i] + bd_dot(i, s_hi, s_lo, p_hi, p_lo)
    return s


def _col_bcast(tile, c, width=LANES):
    return jnp.broadcast_to(tile[:, c:c + 1], (tile.shape[0], width))


def _delta_body(*refs, n_chunks, bb, has_init, emit_state):
    qkv = refs[0:2]
    ag = refs[2:4]
    gt = refs[4:6]
    norm_ref = refs[6]
    pos = 7
    s0_ref = None
    if has_init:
        s0_ref = refs[pos]
        pos += 1
    ya_ref = refs[pos]
    pos += 1
    sout_ref = None
    if emit_state:
        sout_ref = refs[pos]
        pos += 1
    s_scr, o_scr, bdn_hi, bdn_lo, bdk, bduw, bds, bdv = refs[pos:pos + 8]

    n = pl.program_id(1)
    gw = GROUP * DK_A

    @pl.when((n == 0) & (pl.program_id(0) == 0))
    def _():
        for ref in (bdn_hi, bdn_lo, bdk, bduw, bds, bdv):
            ref[...] = jnp.zeros_like(ref)

    @pl.when(n == 0)
    def _():
        o_scr[...] = jnp.zeros_like(o_scr)
        for bi in range(bb):
            for d in range(N_DIR):
                for h in range(H_A):
                    blk = s0_ref[bi, d, h] if has_init else jnp.zeros((DK_A, DV_A), F32)
                    s_scr[bi, d, h // GROUP, :, (h % GROUP) * DV_A:(h % GROUP + 1) * DV_A] = blk

    row = lax.broadcasted_iota(jnp.int32, (CHUNK, LANES), 0)
    lane = lax.broadcasted_iota(jnp.int32, (CHUNK, LANES), 1)
    col = lane & (CHUNK - 1)
    left = lane < CHUNK
    eye_w = jnp.concatenate([(row == col).astype(F32)] * (GROUP // PAIR), axis=1)
    probs = [(bi, d, g) for bi in range(bb) for d in range(N_DIR) for g in range(N_GROUPS)]
    masks = {0: (row >= col, row > col), 1: (row <= col, row < col)}
    gates, gc, gc_t, eg, egl = {}, {}, {}, {}, {}
    for bi in range(bb):
        for d in range(N_DIR):
            key = (bi, d)
            gates[key] = gt[d][bi]
            gc[key] = _dot_f32(_masks(d)[0].astype(F32), gates[key])
            gc_t[key] = jnp.concatenate([gc[key], gc[key]], axis=0).T
            last = CHUNK - 1 if d == 0 else 0
            eg[key] = jnp.exp(gc[key])
            egl[key] = jnp.exp(gc[key][last:last + 1, :] - gc[key])

    def cols_of(d, g):
        return [G_OFF + d * H_A + g * GROUP + hl for hl in range(GROUP)]

    q16, ks, beta_xs, decays, grams = [], [], [], [], []
    for gi, (bi, d, g) in enumerate(probs):
        key = (bi, d)
        cols = cols_of(d, g)
        incl, _ = masks[d]
        k = qkv[d][bi, :, A_W + g * gw:A_W + (g + 1) * gw]
        beta_x = jnp.concatenate([_col_bcast(gates[key], BETA_OFF - G_OFF + c) for c in cols], axis=1)
        decay = []
        for p in range(GROUP // PAIR):
            c0, c1 = cols[PAIR * p], cols[PAIR * p + 1]
            gcol = jnp.where(left, _col_bcast(gc[key], c0), _col_bcast(gc[key], c1))
            grow = jnp.where(left[0:1], gc_t[key][c0:c0 + 1, :], gc_t[key][c1:c1 + 1, :])
            decay.append(jnp.exp(jnp.where(incl, gcol - grow, NEG)))
        decays.append(jnp.concatenate(decay, axis=1))
        q16.append(qkv[d][bi, :, g * gw:(g + 1) * gw].astype(BF16))
        ks.append(k)
        beta_xs.append(beta_x)
        _store_blocks(bdk.at[gi], k.astype(BF16), CHUNK, DK_A)
    for gi in range(len(probs)):
        grams.append(lax.dot_general(jnp.concatenate([q16[gi], (ks[gi] * beta_xs[gi]).astype(BF16)], axis=0),
                                     bdk[gi], (((1,), (1,)), ((), ())), preferred_element_type=F32))
    attns, lmats = [], []
    for gi, (bi, d, g) in enumerate(probs):
        strict_w = jnp.concatenate([masks[d][1]] * (GROUP // PAIR), axis=1)
        attns.append((grams[gi][:CHUNK] * decays[gi]).astype(BF16))
        lmats.append(jnp.where(strict_w, grams[gi][CHUNK:] * decays[gi], 0.0))
    ainvs = _tri_inverse_wide(lmats, eye_w, bdn_hi, bdn_lo)
    eg_xs, kd_ts = [], []
    for gi, (bi, d, g) in enumerate(probs):
        key = (bi, d)
        cols = cols_of(d, g)
        v = qkv[d][bi, :, 2 * A_W + g * gw:2 * A_W + (g + 1) * gw]
        eg_x = jnp.concatenate([_col_bcast(eg[key], c) for c in cols], axis=1)
        egl_x = jnp.concatenate([_col_bcast(egl[key], c) for c in cols], axis=1)
        eg_xs.append(eg_x)
        _store_blocks(bduw.at[gi], (v * beta_xs[gi]).astype(BF16), CHUNK, DV_A)
        _store_blocks(bduw.at[gi], (ks[gi] * (beta_xs[gi] * eg_x)).astype(BF16), CHUNK, DK_A, col0=gw)
        kd = ks[gi] * egl_x
        kd_ts.append(jnp.concatenate([kd[:, hl * DK_A:(hl + 1) * DK_A] for hl in range(GROUP)], axis=0)
                     .T.astype(BF16))
        _store_blocks(bds.at[gi], s_scr[bi, d, g].astype(BF16), DK_A, DV_A)
    uws = []
    for gi in range(len(probs)):
        t_hi, t_lo = _split(ainvs[gi])
        uws.append(jnp.dot(jnp.concatenate([t_hi, t_lo], axis=1), jnp.concatenate([bduw[gi], bduw[gi]], axis=0),
                           preferred_element_type=F32))
    ws_qs = []
    hw = gw // 2
    for gi in range(len(probs)):
        wq = jnp.concatenate([uws[gi][:, gw:].astype(BF16), q16[gi]], axis=0)
        ws_qs.append(jnp.concatenate(
            [jnp.dot(wq[:, p * hw:(p + 1) * hw], bds[gi, p * hw:(p + 1) * hw, p * hw:(p + 1) * hw],
                     preferred_element_type=F32) for p in range(2)], axis=1))
    for gi in range(len(probs)):
        _store_blocks(bdv.at[gi], (uws[gi][:, :gw] - ws_qs[gi][:CHUNK]).astype(BF16), CHUNK, DV_A)
    rs = []
    for gi in range(len(probs)):
        rs.append(jnp.dot(jnp.concatenate([attns[gi], kd_ts[gi]], axis=0), bdv[gi], preferred_element_type=F32))
    tots = []
    for gi, (bi, d, g) in enumerate(probs):
        cols = cols_of(d, g)
        last = CHUNK - 1 if d == 0 else 0
        o = ws_qs[gi][CHUNK:] * eg_xs[gi] + rs[gi][:CHUNK]
        eg_last = jnp.concatenate(
            [jnp.broadcast_to(eg[bi, d][last:last + 1, c:c + 1], (1, DV_A)) for c in cols], axis=1)
        s_scr[bi, d, g] = s_scr[bi, d, g] * eg_last + rs[gi][CHUNK:]
        cidx = n if d == 0 else n_chunks - 1 - n
        rows = pl.ds(pl.multiple_of(cidx * CHUNK, CHUNK), CHUNK)
        tot = o + o_scr[bi, rows, g * gw:(g + 1) * gw]
        o_scr[bi, rows, g * gw:(g + 1) * gw] = tot
        tots.append((rows, tot))
    blks = [(bi, d, rows, g * GROUP + hl, tot[:, hl * DV_A:(hl + 1) * DV_A])
            for (bi, d, g), (rows, tot) in zip(probs, tots) for hl in range(GROUP)]
    inv = [lax.rsqrt(jnp.mean(blk * blk, axis=-1, keepdims=True) + EPS) for *_, blk in blks]
    for (bi, d, rows, h, blk), r in zip(blks, inv):
        hs = slice(h * DV_A, (h + 1) * DV_A)
        ya_ref[bi, rows, hs] = (blk * r * norm_ref[...] * ag[d][bi, :, hs]).astype(BF16)

    if emit_state:
        @pl.when(n == n_chunks - 1)
        def _():
            for bi in range(bb):
                for d in range(N_DIR):
                    for h in range(H_A):
                        sout_ref[bi, d, h] = s_scr[bi, d, h // GROUP, :, (h % GROUP) * DV_A:(h % GROUP + 1) * DV_A]


SCAN_BB = 2


def _scan_specs(n_chunks, width):
    return (pl.BlockSpec((SCAN_BB, CHUNK, width), lambda b, n: (b, n, 0)),
            pl.BlockSpec((SCAN_BB, CHUNK, width), lambda b, n: (b, n_chunks - 1 - n, 0)))


def _delta(qkv, rest, gates, norm_a, s0, batch, n_chunks, emit_state):
    t = n_chunks * CHUNK
    has_init = s0 is not None
    bb = SCAN_BB
    ng = bb * N_DIR * N_GROUPS
    qkv, rest, gates = (a.reshape(batch, t, a.shape[-1]) for a in (qkv, rest, gates))
    in_specs = [*_scan_specs(n_chunks, QKV_W), *_scan_specs(n_chunks, A_W), *_scan_specs(n_chunks, LANES),
                pl.BlockSpec((1, DV_A), lambda b, n: (0, 0))]
    args = [qkv, qkv, rest, rest, gates, gates, norm_a]
    sspec = pl.BlockSpec((bb, N_DIR, H_A, DK_A, DV_A), lambda b, n: (b, 0, 0, 0, 0))
    if has_init:
        in_specs.append(sspec)
        args.append(s0)
    out_specs = [pl.BlockSpec((bb, t, A_W), lambda b, n: (b, 0, 0))]
    out_shape = [jax.ShapeDtypeStruct((batch, t, A_W), BF16)]
    if emit_state:
        out_specs.append(sspec)
        out_shape.append(jax.ShapeDtypeStruct((batch, N_DIR, H_A, DK_A, DV_A), F32))
    out = pl.pallas_call(
        functools.partial(_delta_body, n_chunks=n_chunks, bb=bb, has_init=has_init, emit_state=emit_state),
        grid=(batch // bb, n_chunks),
        in_specs=in_specs,
        out_specs=out_specs,
        out_shape=out_shape,
        scratch_shapes=[pltpu.VMEM((bb, N_DIR, N_GROUPS, DK_A, GROUP * DV_A), F32), pltpu.VMEM((bb, t, A_W), F32),
                        pltpu.VMEM((ng, GROUP * CHUNK, GROUP * CHUNK), BF16),
                        pltpu.VMEM((ng, GROUP * CHUNK, GROUP * CHUNK), BF16),
                        pltpu.VMEM((ng, GROUP * CHUNK, GROUP * DK_A), BF16),
                        pltpu.VMEM((ng, GROUP * CHUNK, 2 * GROUP * DK_A), BF16),
                        pltpu.VMEM((ng, GROUP * DK_A, GROUP * DV_A), BF16),
                        pltpu.VMEM((ng, GROUP * CHUNK, GROUP * DV_A), BF16)],
        compiler_params=_params(("arbitrary", "arbitrary")),
        name="delta_scan",
    )(*args)
    return [out[0].reshape(batch * t, A_W), *out[1:]]


CAUG_W = DV_B + LANES
MW = H_B * CAUG_W


def _scan_max(x, d):
    row = lax.broadcasted_iota(jnp.int32, x.shape, 0)
    s = 1
    while s < CHUNK:
        if d == 0:
            shifted = jnp.where(row >= s, pltpu.roll(x, s, 0), NEG)
        else:
            shifted = jnp.where(row < CHUNK - s, pltpu.roll(x, CHUNK - s, 0), NEG)
        x = jnp.maximum(x, shifted)
        s *= 2
    return x


def _mlstm_body(*refs, n_chunks, bb, has_init, emit_state):
    rest = refs[0:2]
    gt = refs[2:4]
    norm_ref = refs[4]
    pos = 5
    c0_ref = m0_ref = None
    if has_init:
        c0_ref, m0_ref = refs[pos], refs[pos + 1]
        pos += 2
    yb_ref = refs[pos]
    pos += 1
    cout_ref = nout_ref = mout_ref = None
    if emit_state:
        cout_ref, nout_ref, mout_ref = refs[pos:pos + 3]
        pos += 3
    c_scr, m_scr, o_scr, bdk, bdc, bdv = refs[pos:pos + 6]

    n = pl.program_id(1)
    qw = H_B * DK_B

    @pl.when((n == 0) & (pl.program_id(0) == 0))
    def _():
        for ref in (bdk, bdc, bdv):
            ref[...] = jnp.zeros_like(ref)
        for i in range(bb * N_DIR):
            for h in range(H_B):
                bdv[i, h * CHUNK:(h + 1) * CHUNK, h * CAUG_W + DV_B:(h + 1) * CAUG_W] = jnp.ones((CHUNK, LANES), BF16)

    @pl.when(n == 0)
    def _():
        o_scr[...] = jnp.zeros_like(o_scr)
        if has_init:
            c_scr[...] = c0_ref[...]
            m_scr[...] = m0_ref[...]
        else:
            c_scr[...] = jnp.zeros_like(c_scr)
            m_scr[...] = jnp.zeros_like(m_scr)

    row = lax.broadcasted_iota(jnp.int32, (CHUNK, LANES), 0)
    lane = lax.broadcasted_iota(jnp.int32, (CHUNK, LANES), 1)
    col = lane & (CHUNK - 1)
    left = lane < CHUNK
    dirs = [(bi, d) for bi in range(bb) for d in range(N_DIR)]
    slot = {key: i for i, key in enumerate(dirs)}
    q16 = {}
    for key in dirs:
        bi, d = key
        i = slot[key]
        q16[key] = rest[d][bi, :, OFF_BQ:OFF_BQ + qw].astype(BF16)
        _store_blocks(bdk.at[i], rest[d][bi, :, OFF_BK:OFF_BK + qw].astype(BF16), CHUNK, DK_B)
        for h in range(H_B):
            bdv[i, h * CHUNK:(h + 1) * CHUNK, h * CAUG_W:h * CAUG_W + DV_B] = (
                rest[d][bi, :, OFF_BV + h * DV_B:OFF_BV + (h + 1) * DV_B].astype(BF16))
        _store_blocks(bdc.at[i], c_scr[bi, d].astype(BF16), DK_B, CAUG_W)
    qk = {key: lax.dot_general(q16[key], bdk[slot[key]], (((1,), (1,)), ((), ())), preferred_element_type=F32)
          for key in dirs}
    nc, a_t, iw, emt, ksc, dec_row = {}, {}, {}, {}, {}, {}
    for key in dirs:
        bi, d = key
        lo = LF_OFF + d * H_B
        mine = (lane >= lo) & (lane < lo + H_B)
        g = gt[d][bi]
        gc = jnp.where(mine, _dot_f32(_masks(d)[0].astype(F32), g), 0.0)
        a = jnp.where(mine, pltpu.roll(g, LF_OFF - LI_OFF, 1), 0.0) - gc
        last = CHUNK - 1 if d == 0 else 0
        m_old = m_scr[bi, d][0:1, :]
        mx = jnp.maximum(m_old, _scan_max(a, d))
        mxl = mx[last:last + 1, :]
        nc[key] = -mx
        a_t[key] = jnp.concatenate([a, a], axis=0).T
        iw[key] = jnp.exp(m_old - mx)
        emt[key] = jnp.exp(-(gc + mx))
        ksc[key] = jnp.exp(a - mxl)
        dec_row[key] = jnp.exp(m_old - mxl)
        m_scr[bi, d] = jnp.broadcast_to(gc[last:last + 1, :] + mxl, (8, LANES))

    ks_t = {}
    for key in dirs:
        bi, d = key
        lo = LF_OFF + d * H_B
        ks = rest[d][bi, :, OFF_BK:OFF_BK + qw] * jnp.concatenate(
            [_col_bcast(ksc[key], lo + h) for h in range(H_B)], axis=1)
        ks_t[key] = jnp.concatenate([ks[:, h * DK_B:(h + 1) * DK_B] for h in range(H_B)], axis=0).T.astype(BF16)
    lhs = {}
    for key in dirs:
        bi, d = key
        lo = LF_OFF + d * H_B
        incl = row >= col if d == 0 else row <= col
        log_w = []
        for p in range(H_B // PAIR):
            l0, l1 = lo + PAIR * p, lo + PAIR * p + 1
            ccol = jnp.where(left, _col_bcast(nc[key], l0), _col_bcast(nc[key], l1))
            crow = jnp.where(left[0:1], a_t[key][l0:l0 + 1, :], a_t[key][l1:l1 + 1, :])
            log_w.append(jnp.where(incl, ccol + crow, NEG))
        dw = jnp.exp(jnp.concatenate(log_w, axis=1)) * qk[key]
        iw_x = jnp.concatenate([_col_bcast(iw[key], lo + h) for h in range(H_B)], axis=1)
        lhs[key] = ((rest[d][bi, :, OFF_BQ:OFF_BQ + qw] * iw_x).astype(BF16), dw.astype(BF16))
    num = {}
    for key in dirs:
        i = slot[key]
        parts = []
        for p in range(H_B // PAIR):
            ql, dl, cl = slice(p * PAIR * DK_B, (p + 1) * PAIR * DK_B), slice(p * LANES, (p + 1) * LANES), \
                slice(p * PAIR * CAUG_W, (p + 1) * PAIR * CAUG_W)
            parts.append(jnp.dot(jnp.concatenate([lhs[key][0][:, ql], lhs[key][1][:, dl]], axis=1),
                                 jnp.concatenate([bdc[i, ql, cl], bdv[i, dl, cl]], axis=0),
                                 preferred_element_type=F32))
        num[key] = jnp.concatenate(parts, axis=1)
    upd = {key: jnp.dot(ks_t[key], bdv[slot[key]], preferred_element_type=F32) for key in dirs}
    for key in dirs:
        bi, d = key
        lo = LF_OFF + d * H_B
        dec_x = jnp.concatenate(
            [jnp.broadcast_to(dec_row[key][:, lo + h:lo + h + 1], (1, CAUG_W)) for h in range(H_B)], axis=1)
        c_scr[bi, d] = c_scr[bi, d] * dec_x + upd[key]
    heads = [(key, h) for key in dirs for h in range(H_B)]
    rows = {d: pl.ds(pl.multiple_of((n if d == 0 else n_chunks - 1 - n) * CHUNK, CHUNK), CHUNK) for d in range(N_DIR)}
    tots = []
    for key, h in heads:
        bi, d = key
        vs = slice(h * DV_B, (h + 1) * DV_B)
        den = jnp.maximum(jnp.abs(num[key][:, h * CAUG_W + DV_B:(h + 1) * CAUG_W]),
                          _col_bcast(emt[key], LF_OFF + d * H_B + h))
        hb = jnp.concatenate([num[key][:, h * CAUG_W:h * CAUG_W + LANES] / den,
                              num[key][:, h * CAUG_W + LANES:h * CAUG_W + DV_B] / den], axis=1)
        tot = hb + o_scr[bi, rows[d], vs]
        o_scr[bi, rows[d], vs] = tot
        tots.append(tot)
    inv = [lax.rsqrt(jnp.mean(tot * tot, axis=-1, keepdims=True) + EPS) for tot in tots]
    for (key, h), tot, r in zip(heads, tots, inv):
        bi, d = key
        vs = slice(h * DV_B, (h + 1) * DV_B)
        ogate = rest[d][bi, :, OFF_BO + h * DV_B:OFF_BO + (h + 1) * DV_B]
        yb_ref[bi, rows[d], vs] = (tot * r * norm_ref[...] * ogate).astype(BF16)

    if emit_state:
        @pl.when(n == n_chunks - 1)
        def _():
            for bi, d in dirs:
                for h in range(H_B):
                    cout_ref[bi, d, h] = c_scr[bi, d, :, h * CAUG_W:h * CAUG_W + DV_B]
                    nout_ref[bi, d, h] = c_scr[bi, d, :, h * CAUG_W + DV_B:(h + 1) * CAUG_W]
            mout_ref[...] = m_scr[...]


def _mlstm(rest, gates, norm_b, c0, m0, batch, n_chunks, emit_state):
    t = n_chunks * CHUNK
    has_init = c0 is not None
    bb = SCAN_BB
    rest, gates = (a.reshape(batch, t, a.shape[-1]) for a in (rest, gates))
    mspec = pl.BlockSpec((bb, N_DIR, 8, LANES), lambda b, n: (b, 0, 0, 0))
    in_specs = [*_scan_specs(n_chunks, REST_W), *_scan_specs(n_chunks, LANES),
                pl.BlockSpec((1, DV_B), lambda b, n: (0, 0))]
    args = [rest, rest, gates, gates, norm_b]
    if has_init:
        in_specs += [pl.BlockSpec((bb, N_DIR, DK_B, MW), lambda b, n: (b, 0, 0, 0)), mspec]
        args += [c0, m0]
    out_specs = [pl.BlockSpec((bb, t, B_W), lambda b, n: (b, 0, 0))]
    out_shape = [jax.ShapeDtypeStruct((batch, t, B_W), BF16)]
    if emit_state:
        out_specs += [pl.BlockSpec((bb, N_DIR, H_B, DK_B, DV_B), lambda b, n: (b, 0, 0, 0, 0)),
                      pl.BlockSpec((bb, N_DIR, H_B, DK_B, LANES), lambda b, n: (b, 0, 0, 0, 0)), mspec]
        out_shape += [jax.ShapeDtypeStruct((batch, N_DIR, H_B, DK_B, DV_B), F32),
                      jax.ShapeDtypeStruct((batch, N_DIR, H_B, DK_B, LANES), F32),
                      jax.ShapeDtypeStruct((batch, N_DIR, 8, LANES), F32)]
    out = pl.pallas_call(
        functools.partial(_mlstm_body, n_chunks=n_chunks, bb=bb, has_init=has_init, emit_state=emit_state),
        grid=(batch // bb, n_chunks),
        in_specs=in_specs,
        out_specs=out_specs,
        out_shape=out_shape,
        scratch_shapes=[pltpu.VMEM((bb, N_DIR, DK_B, MW), F32),
                        pltpu.VMEM((bb, N_DIR, 8, LANES), F32),
                        pltpu.VMEM((bb, t, B_W), F32),
                        pltpu.VMEM((bb * N_DIR, H_B * CHUNK, H_B * DK_B), BF16),
                        pltpu.VMEM((bb * N_DIR, H_B * DK_B, MW), BF16),
                        pltpu.VMEM((bb * N_DIR, H_B * CHUNK, MW), BF16)],
        compiler_params=_params(("arbitrary", "arbitrary")),
        name="mlstm_scan",
    )(*args)
    return [out[0].reshape(batch * t, B_W), *out[1:]]


OUT_RB = 256


def _outproj_body(ya_ref, yb_ref, wa_ref, wb_ref, x_ref, mod_ref, post1_ref, pre2_ref, x1_ref, h2_ref):
    for rb in range(x_ref.shape[0] // OUT_RB):
        rs = slice(rb * OUT_RB, (rb + 1) * OUT_RB)
        mix = (jnp.dot(ya_ref[rs, :], wa_ref[...], preferred_element_type=F32)
               + jnp.dot(yb_ref[rs, :], wb_ref[...], preferred_element_type=F32))
        x1 = x_ref[rs, :] + mod_ref[0, 2:3, :] * (_rms(mix) * post1_ref[...])
        x1_ref[rs, :] = x1
        h2 = _rms(x1) * pre2_ref[...] * (1.0 + mod_ref[0, 4:5, :]) + mod_ref[0, 3:4, :]
        h2_ref[rs, :] = h2.astype(BF16)


def _outproj(ya, yb, w_out, x2d, mod3, mod_map, post1, pre2, tm):
    m = x2d.shape[0]
    row = lambda i: (i, 0)
    const = lambda i: (0, 0)
    return pl.pallas_call(
        _outproj_body,
        grid=(m // tm,),
        in_specs=[pl.BlockSpec((tm, A_W), row), pl.BlockSpec((tm, B_W), row),
                  pl.BlockSpec((A_W, D_MODEL), lambda i: (0, 0)),
                  pl.BlockSpec((B_W, D_MODEL), lambda i: (1, 0)),
                  pl.BlockSpec((tm, D_MODEL), row),
                  pl.BlockSpec((1, 6, D_MODEL), lambda i: (mod_map(i * tm), 0, 0)),
                  pl.BlockSpec((1, D_MODEL), const), pl.BlockSpec((1, D_MODEL), const)],
        out_specs=[pl.BlockSpec((tm, D_MODEL), row), pl.BlockSpec((tm, D_MODEL), row)],
        out_shape=[jax.ShapeDtypeStruct((m, D_MODEL), F32), jax.ShapeDtypeStruct((m, D_MODEL), BF16)],
        compiler_params=_params(("parallel",)),
        name="outproj",
    )(ya, yb, w_out, w_out, x2d, mod3, post1, pre2)


def _ffn_body(h2_ref, w1_ref, w2_ref, x1_ref, mod_ref, post2_ref, o_ref):
    kk = pl.program_id(1)
    a = jnp.maximum(jnp.dot(h2_ref[...], w1_ref[...], preferred_element_type=F32), 0.0)
    contrib = jnp.dot((a * a).astype(BF16), w2_ref[...], preferred_element_type=F32)

    @pl.when(kk == 0)
    def _():
        o_ref[...] = contrib

    @pl.when((kk > 0) & (kk < pl.num_programs(1) - 1))
    def _():
        o_ref[...] += contrib

    @pl.when(kk == pl.num_programs(1) - 1)
    def _():
        f = o_ref[...] + contrib
        o_ref[...] = x1_ref[...] + mod_ref[0, 5:6, :] * (_rms(f) * post2_ref[...])


def _ffn(h2, w1, w2, x1, mod3, mod_map, post2, tm, fc):
    m = h2.shape[0]
    return pl.pallas_call(
        _ffn_body,
        grid=(m // tm, FFN // fc),
        in_specs=[pl.BlockSpec((tm, D_MODEL), lambda i, k: (i, 0)),
                  pl.BlockSpec((D_MODEL, fc), lambda i, k: (0, k)),
                  pl.BlockSpec((fc, D_MODEL), lambda i, k: (k, 0)),
                  pl.BlockSpec((tm, D_MODEL), lambda i, k: (i, 0)),
                  pl.BlockSpec((1, 6, D_MODEL), lambda i, k: (mod_map(i * tm), 0, 0)),
                  pl.BlockSpec((1, D_MODEL), lambda i, k: (0, 0))],
        out_specs=pl.BlockSpec((tm, D_MODEL), lambda i, k: (i, 0)),
        out_shape=jax.ShapeDtypeStruct((m, D_MODEL), F32),
        compiler_params=_params(("parallel", "arbitrary")),
        name="ffn",
    )(h2, w1, w2, x1, mod3, post2)


def _block(x, mod3, mod_of_row, lp, init, seq_len, emit_state):
    bsz, t, _ = x.shape
    x2d = x.reshape(bsz * t, D_MODEL)
    n_chunks = t // CHUNK
    qkv, rest, gates = _proj(x2d, mod3, lambda i: mod_of_row(i * PROJ_TM), lp["pre1"], lp["w16"],
                             lp["w_mid"], lp["w_gate"], lp["conv_w"], lp["gate_p"], seq_len)
    s0, c0, m0 = init if init is not None else (None, None, None)
    d_out = _delta(qkv, rest, gates, lp["norm_a"], s0, bsz, n_chunks, emit_state)
    m_out = _mlstm(rest, gates, lp["norm_b"], c0, m0, bsz, n_chunks, emit_state)
    x1, h2 = _outproj(d_out[0], m_out[0], lp["w_out"], x2d, mod3, mod_of_row, lp["post1"], lp["pre2"], 512)
    y = _ffn(h2, lp["w1"], lp["w2"], x1, mod3, mod_of_row, lp["post2"], 512, 1024)
    states = None
    if emit_state:
        m_fin = jnp.stack([m_out[3][:, d, 0, LF_OFF + d * H_B:LF_OFF + (d + 1) * H_B] for d in range(N_DIR)], axis=1)
        states = (d_out[1], m_out[1], m_out[2][..., 0], m_fin)
    return y.reshape(bsz, t, D_MODEL), states


def _layer_params(l, norm_mix_pre, norm_mix_post, norm_ffn_pre, norm_ffn_post, w_in, conv_w, a_log, dt_bias,
                  norm_a, mlstm_ibias, mlstm_fbias, norm_b, w_out, w_ffn1, w_ffn2):
    w = w_in[l]
    o_ag = QKV_W
    o_aa = o_ag + A_W
    o_ab = o_aa + N_DIR * H_A
    o_bq = o_ab + N_DIR * H_A
    o_bi = o_bq + 2 * H_B * DK_B + 2 * B_W
    o_bf = o_bi + N_DIR * H_B
    n_gate = 2 * N_DIR * H_A + 2 * N_DIR * H_B
    w16 = w.astype(BF16)
    w_gate = jnp.concatenate([w16[:, o_aa:o_bq], w16[:, o_bi:o_bf + N_DIR * H_B],
                              jnp.zeros((D_MODEL, LANES - n_gate), BF16)], axis=1)

    def lane_row(vals, off):
        return jnp.zeros((LANES,), F32).at[off:off + vals.size].set(vals.reshape(-1))

    gate_p = jnp.stack([lane_row(a_log[l], G_OFF), lane_row(dt_bias[l], G_OFF),
                        lane_row(mlstm_ibias[l], LI_OFF) + lane_row(mlstm_fbias[l], LF_OFF)]
                       + [jnp.zeros((LANES,), F32)] * 5)
    row = lambda v: v[l].reshape(1, -1)
    return dict(
        pre1=row(norm_mix_pre), post1=row(norm_mix_post), pre2=row(norm_ffn_pre), post2=row(norm_ffn_post),
        w16=w16, w_mid=w16[:, o_bq:o_bi], w_gate=w_gate, gate_p=gate_p,
        conv_w=jnp.concatenate([conv_w[l].T, jnp.zeros((5, QKV_W), F32)], axis=0),
        norm_a=row(norm_a), norm_b=row(norm_b),
        w_out=w_out[l].astype(BF16), w1=w_ffn1[l].astype(BF16), w2=w_ffn2[l].astype(BF16))


def kernel(x_prompt, x_sample, state_delta, state_mlstm_C, state_mlstm_n, state_mlstm_m, c, c_ctx, w_ada, b_ada, norm_mix_pre, norm_mix_post, norm_ffn_pre, norm_ffn_post, w_in, conv_w, a_log, dt_bias, norm_a, mlstm_ibias, mlstm_fbias, norm_b, w_out, w_ffn1, w_ffn2):
    depth = w_in.shape[0]
    n_lat = x_sample.shape[0]
    t_lat = x_sample.shape[1]
    cond = jnp.concatenate([c_ctx[None, :], c, jnp.zeros((8 - 1 - n_lat, D_MODEL), F32)], axis=0)
    y_prompt, y_sample = x_prompt, x_sample
    acc = ([], [], [], [])
    for l in range(depth):
        lp = _layer_params(l, norm_mix_pre, norm_mix_post, norm_ffn_pre, norm_ffn_post, w_in, conv_w, a_log,
                           dt_bias, norm_a, mlstm_ibias, mlstm_fbias, norm_b, w_out, w_ffn1, w_ffn2)
        mod = _ada(cond, w_ada, b_ada[l].reshape(1, -1), l)
        mod3 = mod[:1 + n_lat].reshape(1 + n_lat, 6, D_MODEL)
        y_prompt, st = _block(y_prompt, mod3, lambda r: 0, lp, None, x_prompt.shape[1], True)
        for a, s in zip(acc, st):
            a.append(s)
        n_rep = jnp.broadcast_to(state_mlstm_n[:, l][..., None], state_mlstm_n[:, l].shape + (LANES,))
        c_aug0 = jnp.concatenate([state_mlstm_C[:, l], n_rep], axis=-1)
        c_aug0 = c_aug0.transpose(0, 1, 3, 2, 4).reshape(n_lat, N_DIR, DK_B, MW)
        m0 = jnp.zeros((n_lat, N_DIR, LANES), F32)
        for d in range(N_DIR):
            m0 = m0.at[:, d, LF_OFF + d * H_B:LF_OFF + (d + 1) * H_B].set(state_mlstm_m[:, l, d])
        m0 = jnp.broadcast_to(m0[:, :, None, :], (n_lat, N_DIR, 8, LANES))
        y_sample, _ = _block(y_sample, mod3, lambda r: 1 + r // t_lat, lp, (state_delta[:, l], c_aug0, m0),
                             GRID_W, False)
    return (y_prompt, y_sample) + tuple(jnp.stack(a, axis=1) for a in acc)
```

```python
import functools

import jax
import jax.numpy as jnp
from jax import lax
from jax.experimental import pallas as pl
from jax.experimental.pallas import tpu as pltpu

F32 = jnp.float32
BF16 = jnp.bfloat16

D_MODEL = 2048
N_DIR = 2
A_W = D_MODEL // 2
B_W = D_MODEL - A_W
DK_A = 128
DV_A = 128
H_A = A_W // DV_A
DV_B = 256
DK_B = DV_B // 2
H_B = B_W // DV_B
GRID_W = 64
CHUNK = 64
FFN = 4 * D_MODEL
EPS = 1e-6
LANES = 128
NEG = -1e30

QKV_W = 3 * A_W
REST_W = A_W + 2 * H_B * DK_B + 2 * B_W
OFF_AG, OFF_BQ, OFF_BK, OFF_BV, OFF_BO = 0, A_W, A_W + H_B * DK_B, A_W + 2 * H_B * DK_B, A_W + 2 * H_B * DK_B + B_W
G_OFF, BETA_OFF = 0, N_DIR * H_A
LI_OFF = 2 * N_DIR * H_A
LF_OFF = LI_OFF + N_DIR * H_B

VMEM_LIMIT = 56 * 1024 * 1024


def _sigmoid(x):
    return 1.0 / (1.0 + jnp.exp(-x))


def _softplus(x):
    return jnp.maximum(x, 0.0) + jnp.log1p(jnp.exp(-jnp.abs(x)))


def _dot(a, b):
    return jnp.dot(a.astype(BF16), b.astype(BF16), preferred_element_type=F32)


def _dot_f32(a, b):
    return jnp.dot(a, b, precision=lax.Precision.HIGHEST, preferred_element_type=F32)


def _rms(x):
    return x * lax.rsqrt(jnp.mean(x * x, axis=-1, keepdims=True) + EPS)


def _params(sem):
    return pltpu.CompilerParams(dimension_semantics=sem, vmem_limit_bytes=VMEM_LIMIT)


def _ada_body(c_ref, w_ref, b_ref, o_ref):
    c = c_ref[...]
    o_ref[...] = _dot(c * _sigmoid(c), w_ref[...]) + b_ref[...]


def _ada(c_all, w_ada, b, l):
    n = w_ada.shape[2]
    tn = 1024
    return pl.pallas_call(
        _ada_body,
        grid=(n // tn,),
        in_specs=[pl.BlockSpec(c_all.shape, lambda j: (0, 0)),
                  pl.BlockSpec((None, D_MODEL, tn), lambda j: (l, 0, j)),
                  pl.BlockSpec((1, tn), lambda j: (0, j))],
        out_specs=pl.BlockSpec((c_all.shape[0], tn), lambda j: (0, j)),
        out_shape=jax.ShapeDtypeStruct((c_all.shape[0], n), F32),
        compiler_params=_params(("arbitrary",)),
        name="ada",
    )(c_all, w_ada, b)


PROJ_TM = 1024
PROJ_RB = 256
PROJ_TN = 512
N_QKV_T = QKV_W // PROJ_TN
N_REST_T = REST_W // PROJ_TN
N_HEAD_T = (QKV_W + A_W) // PROJ_TN
N_MID_T = N_QKV_T + N_REST_T - N_HEAD_T


def _proj_body(x_ref, mod_ref, g_ref, wh_ref, wb_ref, wg_ref, cw_ref, gp_ref, qkv_ref, rest_ref, gate_ref, h_scr, *,
               seq_len):
    j = pl.program_id(1)
    blocks = [slice(rb * PROJ_RB, (rb + 1) * PROJ_RB) for rb in range(PROJ_TM // PROJ_RB)]

    def run(w_ref, epilogue, out_ref):
        for rs in blocks:
            epilogue(jnp.dot(h_scr[rs, :], w_ref[...], preferred_element_type=F32), out_ref, rs)

    def run_first():
        for rs in blocks:
            y = _rms(x_ref[rs, :]) * g_ref[...]
            h_scr[rs, :] = (y * (1.0 + mod_ref[0, 1:2, :]) + mod_ref[0, 0:1, :]).astype(BF16)
            ep_l2(DK_A ** -0.5)(jnp.dot(h_scr[rs, :], wh_ref[...], preferred_element_type=F32), qkv_ref, rs)
            ep_gate(jnp.dot(h_scr[rs, :], wg_ref[...], preferred_element_type=F32), gate_ref, rs)

    def conv_silu(acc, ls):
        a = acc[:, ls]
        pos = lax.broadcasted_iota(jnp.int32, a.shape, 0) & (seq_len - 1)
        prev = jnp.where(pos == 0, 0.0, pltpu.roll(a, 1, 0))
        nxt = jnp.where(pos == seq_len - 1, 0.0, pltpu.roll(a, a.shape[0] - 1, 0))
        y = prev * cw_ref[0:1, ls] + a * cw_ref[1:2, ls] + nxt * cw_ref[2:3, ls]
        return y * _sigmoid(y)

    lane_groups = [slice(g * LANES, (g + 1) * LANES) for g in range(PROJ_TN // LANES)]

    def ep_l2(scale):
        def f(acc, out_ref, rs):
            for ls in lane_groups:
                blk = conv_silu(acc, ls)
                inv = lax.rsqrt(jnp.sum(blk * blk, axis=-1, keepdims=True) + EPS)
                out_ref[rs, ls] = blk * (inv * scale)
        return f

    def ep_conv(acc, out_ref, rs):
        for ls in lane_groups:
            out_ref[rs, ls] = conv_silu(acc, ls)

    def ep_map(fn):
        def f(acc, out_ref, rs):
            out_ref[rs, :] = fn(acc)
        return f

    def ep_gate(z, out_ref, rs):
        lane = lax.broadcasted_iota(jnp.int32, z.shape, 1)
        g = -jnp.exp(gp_ref[0:1, :]) * _softplus(z + gp_ref[1:2, :])
        li = z + gp_ref[2:3, :]
        out_ref[rs, :] = jnp.where(lane < BETA_OFF, g,
                                   jnp.where(lane < LI_OFF, _sigmoid(z),
                                             jnp.where(lane < LF_OFF, li,
                                                       jnp.where(lane < LF_OFF + N_DIR * H_B, -_softplus(-li), 0.0))))

    tp = A_W // PROJ_TN
    c0 = (j - N_QKV_T) * PROJ_TN
    in_rest = (j >= N_QKV_T) & (j < N_QKV_T + N_REST_T)
    pl.when(j == 0)(run_first)
    variants = [
        ((j > 0) & (j < tp), wh_ref, ep_l2(DK_A ** -0.5), qkv_ref),
        ((j >= tp) & (j < 2 * tp), wh_ref, ep_l2(1.0), qkv_ref),
        ((j >= 2 * tp) & (j < N_QKV_T), wh_ref, ep_conv, qkv_ref),
        (in_rest & (c0 < OFF_BQ), wh_ref, ep_map(lambda a: a * _sigmoid(a)), rest_ref),
        (in_rest & (c0 >= OFF_BQ) & (c0 < OFF_BK), wb_ref, ep_map(lambda a: a * (DK_B ** -0.5)), rest_ref),
        (in_rest & (c0 >= OFF_BK) & (c0 < OFF_BO), wb_ref, ep_map(lambda a: a), rest_ref),
        (in_rest & (c0 >= OFF_BO), wb_ref, ep_map(_sigmoid), rest_ref),
    ]
    for cond, w_ref, epilogue, out_ref in variants:
        pl.when(cond)(functools.partial(run, w_ref, epilogue, out_ref))


def _proj(x2d, mod3, mod_map, pre_g, w16, w_mid, w_gate, conv_w, gate_p, seq_len):
    m = x2d.shape[0]
    tm, tn = PROJ_TM, PROJ_TN
    assert PROJ_RB % seq_len == 0 and m % tm == 0
    return pl.pallas_call(
        functools.partial(_proj_body, seq_len=seq_len),
        grid=(m // tm, N_HEAD_T + N_MID_T),
        in_specs=[pl.BlockSpec((tm, D_MODEL), lambda i, j: (i, 0)),
                  pl.BlockSpec((1, 6, D_MODEL), lambda i, j: (mod_map(i), 0, 0)),
                  pl.BlockSpec((1, D_MODEL), lambda i, j: (0, 0)),
                  pl.BlockSpec((D_MODEL, tn), lambda i, j: (0, jnp.minimum(j, N_HEAD_T - 1))),
                  pl.BlockSpec((D_MODEL, tn), lambda i, j: (0, jnp.clip(j - N_HEAD_T, 0, N_MID_T - 1))),
                  pl.BlockSpec((D_MODEL, LANES), lambda i, j: (0, 0)),
                  pl.BlockSpec((8, tn), lambda i, j: (0, jnp.minimum(j, N_QKV_T - 1))),
                  pl.BlockSpec((8, LANES), lambda i, j: (0, 0))],
        out_specs=[pl.BlockSpec((tm, tn), lambda i, j: (i, jnp.minimum(j, N_QKV_T - 1))),
                   pl.BlockSpec((tm, tn), lambda i, j: (i, jnp.clip(j - N_QKV_T, 0, N_REST_T - 1))),
                   pl.BlockSpec((tm, LANES), lambda i, j: (i, 0))],
        out_shape=[jax.ShapeDtypeStruct((m, QKV_W), F32), jax.ShapeDtypeStruct((m, REST_W), F32),
                   jax.ShapeDtypeStruct((m, LANES), F32)],
        scratch_shapes=[pltpu.VMEM((tm, D_MODEL), BF16)],
        compiler_params=_params(("parallel", "arbitrary")),
        name="proj",
    )(x2d, mod3, pre_g, w16, w_mid, w_gate, conv_w, gate_p)


def _masks(d):
    row = lax.broadcasted_iota(jnp.int32, (CHUNK, CHUNK), 0)
    col = lax.broadcasted_iota(jnp.int32, (CHUNK, CHUNK), 1)
    if d == 0:
        return row >= col, row > col, row == col
    return row <= col, row < col, row == col


GROUP = 4
N_GROUPS = H_A // GROUP
PAIR = LANES // CHUNK


def _split(x):
    hi = x.astype(BF16)
    return hi, (x - hi.astype(F32)).astype(BF16)


def _store_blocks(ref, x, blk_r, blk_c, col0=0):
    for h in range(GROUP):
        ref[h * blk_r:(h + 1) * blk_r, col0 + h * blk_c:col0 + (h + 1) * blk_c] = x[:, h * blk_c:(h + 1) * blk_c]


def _dot3(a_hi, a_lo, b_hi, b_lo):
    m = a_hi.shape[0]
    r = jnp.dot(jnp.concatenate([a_hi, a_lo], axis=0), b_hi, preferred_element_type=F32)
    return r[:m] + r[m:] + jnp.dot(a_hi, b_lo, preferred_element_type=F32)


def _tri_inverse_wide(lmats, eye_w, bd_hi, bd_lo):
    def bd_dot(i, a_hi, a_lo, b_hi, b_lo):
        _store_blocks(bd_hi.at[i], b_hi, CHUNK, CHUNK)
        _store_blocks(bd_lo.at[i], b_lo, CHUNK, CHUNK)
        return _dot3(a_hi, a_lo, bd_hi[i], bd_lo[i])

    idx = range(len(lmats))
    s = [eye_w - l for l in lmats]
    p = []
    for i in idx:
        m_hi, m_lo = _split(-lmats[i])
        p.append(bd_dot(i, m_hi, m_lo, m_hi, m_lo))
    for _ in range(4):
        for i in idx:
            p_hi, p_lo = _split(p[i])
            s_hi, s_lo = _split(s[i])
            r = bd_dot(i, jnp.concatenate([p_hi, s_hi], axis=0), jnp.concatenate([p_lo, s_lo], axis=0), p_hi, p_lo)
            p[i] = r[:CHUNK]
            s[i] = s[i] + r[CHUNK:]
    for i in idx:
        p_hi, p_lo = _split(p[i])
        s_hi, s_lo = _split(s[i])
        s[i] = s[i] + bd_dot(i, s_hi, s_lo, p_hi, p_lo)
    return s


def _col_bcast(tile, c, width=LANES):
    return jnp.broadcast_to(tile[:, c:c + 1], (tile.shape[0], width))


def _delta_body(*refs, n_chunks, bb, has_init, emit_state):
    qkv = refs[0:2]
    ag = refs[2:4]
    gt = refs[4:6]
    norm_ref = refs[6]
    pos = 7
    s0_ref = None
    if has_init:
        s0_ref = refs[pos]
        pos += 1
    ya_ref = refs[pos]
    pos += 1
    sout_ref = None
    if emit_state:
        sout_ref = refs[pos]
        pos += 1
    s_scr, o_scr, bdn_hi, bdn_lo, bdk, bduw, bds, bdv = refs[pos:pos + 8]

    n = pl.program_id(1)
    gw = GROUP * DK_A

    @pl.when((n == 0) & (pl.program_id(0) == 0))
    def _():
        for ref in (bdn_hi, bdn_lo, bdk, bduw, bds, bdv):
            ref[...] = jnp.zeros_like(ref)

    @pl.when(n == 0)
    def _():
        o_scr[...] = jnp.zeros_like(o_scr)
        for bi in range(bb):
            for d in range(N_DIR):
                for h in range(H_A):
                    blk = s0_ref[bi, d, h] if has_init else jnp.zeros((DK_A, DV_A), F32)
                    s_scr[bi, d, h // GROUP, :, (h % GROUP) * DV_A:(h % GROUP + 1) * DV_A] = blk

    row = lax.broadcasted_iota(jnp.int32, (CHUNK, LANES), 0)
    lane = lax.broadcasted_iota(jnp.int32, (CHUNK, LANES), 1)
    col = lane & (CHUNK - 1)
    left = lane < CHUNK
    eye_w = jnp.concatenate([(row == col).astype(F32)] * (GROUP // PAIR), axis=1)
    probs = [(bi, d, g) for bi in range(bb) for d in range(N_DIR) for g in range(N_GROUPS)]
    masks = {0: (row >= col, row > col), 1: (row <= col, row < col)}
    gates, gc, gc_t, eg, egl = {}, {}, {}, {}, {}
    for bi in range(bb):
        for d in range(N_DIR):
            key = (bi, d)
            gates[key] = gt[d][bi]
            gc[key] = _dot_f32(_masks(d)[0].astype(F32), gates[key])
            gc_t[key] = jnp.concatenate([gc[key], gc[key]], axis=0).T
            last = CHUNK - 1 if d == 0 else 0
            eg[key] = jnp.exp(gc[key])
            egl[key] = jnp.exp(gc[key][last:last + 1, :] - gc[key])

    def cols_of(d, g):
        return [G_OFF + d * H_A + g * GROUP + hl for hl in range(GROUP)]

    q16, ks, beta_xs, decays, grams = [], [], [], [], []
    for gi, (bi, d, g) in enumerate(probs):
        key = (bi, d)
        cols = cols_of(d, g)
        incl, _ = masks[d]
        k = qkv[d][bi, :, A_W + g * gw:A_W + (g + 1) * gw]
        beta_x = jnp.concatenate([_col_bcast(gates[key], BETA_OFF - G_OFF + c) for c in cols], axis=1)
        decay = []
        for p in range(GROUP // PAIR):
            c0, c1 = cols[PAIR * p], cols[PAIR * p + 1]
            gcol = jnp.where(left, _col_bcast(gc[key], c0), _col_bcast(gc[key], c1))
            grow = jnp.where(left[0:1], gc_t[key][c0:c0 + 1, :], gc_t[key][c1:c1 + 1, :])
            decay.append(jnp.exp(jnp.where(incl, gcol - grow, NEG)))
        decays.append(jnp.concatenate(decay, axis=1))
        q16.append(qkv[d][bi, :, g * gw:(g + 1) * gw].astype(BF16))
        ks.append(k)
        beta_xs.append(beta_x)
        _store_blocks(bdk.at[gi], k.astype(BF16), CHUNK, DK_A)
    for gi in range(len(probs)):
        grams.append(lax.dot_general(jnp.concatenate([q16[gi], (ks[gi] * beta_xs[gi]).astype(BF16)], axis=0),
                                     bdk[gi], (((1,), (1,)), ((), ())), preferred_element_type=F32))
    attns, lmats = [], []
    for gi, (bi, d, g) in enumerate(probs):
        strict_w = jnp.concatenate([masks[d][1]] * (GROUP // PAIR), axis=1)
        attns.append((grams[gi][:CHUNK] * decays[gi]).astype(BF16))
        lmats.append(jnp.where(strict_w, grams[gi][CHUNK:] * decays[gi], 0.0))
    ainvs = _tri_inverse_wide(lmats, eye_w, bdn_hi, bdn_lo)
    eg_xs, kd_ts = [], []
    for gi, (bi, d, g) in enumerate(probs):
        key = (bi, d)
        cols = cols_of(d, g)
        v = qkv[d][bi, :, 2 * A_W + g * gw:2 * A_W + (g + 1) * gw]
        eg_x = jnp.concatenate([_col_bcast(eg[key], c) for c in cols], axis=1)
        egl_x = jnp.concatenate([_col_bcast(egl[key], c) for c in cols], axis=1)
        eg_xs.append(eg_x)
        _store_blocks(bduw.at[gi], (v * beta_xs[gi]).astype(BF16), CHUNK, DV_A)
        _store_blocks(bduw.at[gi], (ks[gi] * (beta_xs[gi] * eg_x)).astype(BF16), CHUNK, DK_A, col0=gw)
        kd = ks[gi] * egl_x
        kd_ts.append(jnp.concatenate([kd[:, hl * DK_A:(hl + 1) * DK_A] for hl in range(GROUP)], axis=0)
                     .T.astype(BF16))
        _store_blocks(bds.at[gi], s_scr[bi, d, g].astype(BF16), DK_A, DV_A)
    uws = []
    for gi in range(len(probs)):
        t_hi, t_lo = _split(ainvs[gi])
        r = jnp.dot(jnp.concatenate([t_hi, t_lo], axis=0), bduw[gi], preferred_element_type=F32)
        uws.append(r[:CHUNK] + r[CHUNK:])
    ws_qs = []
    hw = gw // 2
    for gi in range(len(probs)):
        wq = jnp.concatenate([uws[gi][:, gw:].astype(BF16), q16[gi]], axis=0)
        ws_qs.append(jnp.concatenate(
            [jnp.dot(wq[:, p * hw:(p + 1) * hw], bds[gi, p * hw:(p + 1) * hw, p * hw:(p + 1) * hw],
                     preferred_element_type=F32) for p in range(2)], axis=1))
    for gi in range(len(probs)):
        _store_blocks(bdv.at[gi], (uws[gi][:, :gw] - ws_qs[gi][:CHUNK]).astype(BF16), CHUNK, DV_A)
    rs = []
    for gi in range(len(probs)):
        rs.append(jnp.dot(jnp.concatenate([attns[gi], kd_ts[gi]], axis=0), bdv[gi], preferred_element_type=F32))
    tots = []
    for gi, (bi, d, g) in enumerate(probs):
        cols = cols_of(d, g)
        last = CHUNK - 1 if d == 0 else 0
        o = ws_qs[gi][CHUNK:] * eg_xs[gi] + rs[gi][:CHUNK]
        eg_last = jnp.concatenate(
            [jnp.broadcast_to(eg[bi, d][last:last + 1, c:c + 1], (1, DV_A)) for c in cols], axis=1)
        s_scr[bi, d, g] = s_scr[bi, d, g] * eg_last + rs[gi][CHUNK:]
        cidx = n if d == 0 else n_chunks - 1 - n
        rows = pl.ds(pl.multiple_of(cidx * CHUNK, CHUNK), CHUNK)
        tot = o + o_scr[bi, rows, g * gw:(g + 1) * gw]
        o_scr[bi, rows, g * gw:(g + 1) * gw] = tot
        tots.append((rows, tot))
    blks = [(bi, d, rows, g * GROUP + hl, tot[:, hl * DV_A:(hl + 1) * DV_A])
            for (bi, d, g), (rows, tot) in zip(probs, tots) for hl in range(GROUP)]
    inv = [lax.rsqrt(jnp.mean(blk * blk, axis=-1, keepdims=True) + EPS) for *_, blk in blks]
    for (bi, d, rows, h, blk), r in zip(blks, inv):
        hs = slice(h * DV_A, (h + 1) * DV_A)
        ya_ref[bi, rows, hs] = (blk * r * norm_ref[...] * ag[d][bi, :, hs]).astype(BF16)

    if emit_state:
        @pl.when(n == n_chunks - 1)
        def _():
            for bi in range(bb):
                for d in range(N_DIR):
                    for h in range(H_A):
                        sout_ref[bi, d, h] = s_scr[bi, d, h // GROUP, :, (h % GROUP) * DV_A:(h % GROUP + 1) * DV_A]


SCAN_BB = 2


def _scan_specs(bb, n_chunks, width):
    return (pl.BlockSpec((bb, CHUNK, width), lambda b, n: (b, n, 0)),
            pl.BlockSpec((bb, CHUNK, width), lambda b, n: (b, n_chunks - 1 - n, 0)))


def _delta(qkv, rest, gates, norm_a, s0, batch, n_chunks, emit_state):
    t = n_chunks * CHUNK
    has_init = s0 is not None
    bb = SCAN_BB
    ng = bb * N_DIR * N_GROUPS
    qkv, rest, gates = (a.reshape(batch, t, a.shape[-1]) for a in (qkv, rest, gates))
    in_specs = [*_scan_specs(bb, n_chunks, QKV_W), *_scan_specs(bb, n_chunks, A_W), *_scan_specs(bb, n_chunks, LANES),
                pl.BlockSpec((1, DV_A), lambda b, n: (0, 0))]
    args = [qkv, qkv, rest, rest, gates, gates, norm_a]
    sspec = pl.BlockSpec((bb, N_DIR, H_A, DK_A, DV_A), lambda b, n: (b, 0, 0, 0, 0))
    if has_init:
        in_specs.append(sspec)
        args.append(s0)
    out_specs = [pl.BlockSpec((bb, t, A_W), lambda b, n: (b, 0, 0))]
    out_shape = [jax.ShapeDtypeStruct((batch, t, A_W), BF16)]
    if emit_state:
        out_specs.append(sspec)
        out_shape.append(jax.ShapeDtypeStruct((batch, N_DIR, H_A, DK_A, DV_A), F32))
    out = pl.pallas_call(
        functools.partial(_delta_body, n_chunks=n_chunks, bb=bb, has_init=has_init, emit_state=emit_state),
        grid=(batch // bb, n_chunks),
        in_specs=in_specs,
        out_specs=out_specs,
        out_shape=out_shape,
        scratch_shapes=[pltpu.VMEM((bb, N_DIR, N_GROUPS, DK_A, GROUP * DV_A), F32), pltpu.VMEM((bb, t, A_W), F32),
                        pltpu.VMEM((ng, GROUP * CHUNK, GROUP * CHUNK), BF16),
                        pltpu.VMEM((ng, GROUP * CHUNK, GROUP * CHUNK), BF16),
                        pltpu.VMEM((ng, GROUP * CHUNK, GROUP * DK_A), BF16),
                        pltpu.VMEM((ng, GROUP * CHUNK, 2 * GROUP * DK_A), BF16),
                        pltpu.VMEM((ng, GROUP * DK_A, GROUP * DV_A), BF16),
                        pltpu.VMEM((ng, GROUP * CHUNK, GROUP * DV_A), BF16)],
        compiler_params=_params(("arbitrary", "arbitrary")),
        name="delta_scan",
    )(*args)
    return [out[0].reshape(batch * t, A_W), *out[1:]]


CAUG_W = DV_B + LANES
MW = H_B * CAUG_W


def _scan_max(x, d):
    row = lax.broadcasted_iota(jnp.int32, x.shape, 0)
    s = 1
    while s < CHUNK:
        if d == 0:
            shifted = jnp.where(row >= s, pltpu.roll(x, s, 0), NEG)
        else:
            shifted = jnp.where(row < CHUNK - s, pltpu.roll(x, CHUNK - s, 0), NEG)
        x = jnp.maximum(x, shifted)
        s *= 2
    return x


def _mlstm_body(*refs, n_chunks, bb, has_init, emit_state):
    rest = refs[0:2]
    gt = refs[2:4]
    norm_ref = refs[4]
    pos = 5
    c0_ref = m0_ref = None
    if has_init:
        c0_ref, m0_ref = refs[pos], refs[pos + 1]
        pos += 2
    yb_ref = refs[pos]
    pos += 1
    cout_ref = nout_ref = mout_ref = None
    if emit_state:
        cout_ref, nout_ref, mout_ref = refs[pos:pos + 3]
        pos += 3
    c_scr, m_scr, o_scr, bdk, bdc, bdv = refs[pos:pos + 6]

    n = pl.program_id(1)
    qw = H_B * DK_B

    @pl.when((n == 0) & (pl.program_id(0) == 0))
    def _():
        for ref in (bdk, bdc, bdv):
            ref[...] = jnp.zeros_like(ref)
        for i in range(bb * N_DIR):
            for h in range(H_B):
                bdv[i, h * CHUNK:(h + 1) * CHUNK, h * CAUG_W + DV_B:(h + 1) * CAUG_W] = jnp.ones((CHUNK, LANES), BF16)

    @pl.when(n == 0)
    def _():
        o_scr[...] = jnp.zeros_like(o_scr)
        if has_init:
            c_scr[...] = c0_ref[...]
            m_scr[...] = m0_ref[...]
        else:
            c_scr[...] = jnp.zeros_like(c_scr)
            m_scr[...] = jnp.zeros_like(m_scr)

    row = lax.broadcasted_iota(jnp.int32, (CHUNK, LANES), 0)
    lane = lax.broadcasted_iota(jnp.int32, (CHUNK, LANES), 1)
    col = lane & (CHUNK - 1)
    left = lane < CHUNK
    dirs = [(bi, d) for bi in range(bb) for d in range(N_DIR)]
    slot = {key: i for i, key in enumerate(dirs)}
    q16 = {}
    for key in dirs:
        bi, d = key
        i = slot[key]
        q16[key] = rest[d][bi, :, OFF_BQ:OFF_BQ + qw].astype(BF16)
        _store_blocks(bdk.at[i], rest[d][bi, :, OFF_BK:OFF_BK + qw].astype(BF16), CHUNK, DK_B)
        for h in range(H_B):
            bdv[i, h * CHUNK:(h + 1) * CHUNK, h * CAUG_W:h * CAUG_W + DV_B] = (
                rest[d][bi, :, OFF_BV + h * DV_B:OFF_BV + (h + 1) * DV_B].astype(BF16))
        _store_blocks(bdc.at[i], c_scr[bi, d].astype(BF16), DK_B, CAUG_W)
    qk = {key: lax.dot_general(q16[key], bdk[slot[key]], (((1,), (1,)), ((), ())), preferred_element_type=F32)
          for key in dirs}
    nc, a_t, iw, emt, ksc, dec_row = {}, {}, {}, {}, {}, {}
    for key in dirs:
        bi, d = key
        lo = LF_OFF + d * H_B
        mine = (lane >= lo) & (lane < lo + H_B)
        g = gt[d][bi]
        gc = jnp.where(mine, _dot_f32(_masks(d)[0].astype(F32), g), 0.0)
        a = jnp.where(mine, pltpu.roll(g, LF_OFF - LI_OFF, 1), 0.0) - gc
        last = CHUNK - 1 if d == 0 else 0
        m_old = m_scr[bi, d][0:1, :]
        mx = jnp.maximum(m_old, _scan_max(a, d))
        mxl = mx[last:last + 1, :]
        nc[key] = -mx
        a_t[key] = jnp.concatenate([a, a], axis=0).T
        iw[key] = jnp.exp(m_old - mx)
        emt[key] = jnp.exp(-(gc + mx))
        ksc[key] = jnp.exp(a - mxl)
        dec_row[key] = jnp.exp(m_old - mxl)
        m_scr[bi, d] = jnp.broadcast_to(gc[last:last + 1, :] + mxl, (8, LANES))

    ks_t = {}
    for key in dirs:
        bi, d = key
        lo = LF_OFF + d * H_B
        ks = rest[d][bi, :, OFF_BK:OFF_BK + qw] * jnp.concatenate(
            [_col_bcast(ksc[key], lo + h) for h in range(H_B)], axis=1)
        ks_t[key] = jnp.concatenate([ks[:, h * DK_B:(h + 1) * DK_B] for h in range(H_B)], axis=0).T.astype(BF16)
    lhs = {}
    for key in dirs:
        bi, d = key
        lo = LF_OFF + d * H_B
        incl = row >= col if d == 0 else row <= col
        log_w = []
        for p in range(H_B // PAIR):
            l0, l1 = lo + PAIR * p, lo + PAIR * p + 1
            ccol = jnp.where(left, _col_bcast(nc[key], l0), _col_bcast(nc[key], l1))
            crow = jnp.where(left[0:1], a_t[key][l0:l0 + 1, :], a_t[key][l1:l1 + 1, :])
            log_w.append(jnp.where(incl, ccol + crow, NEG))
        dw = jnp.exp(jnp.concatenate(log_w, axis=1)) * qk[key]
        iw_x = jnp.concatenate([_col_bcast(iw[key], lo + h) for h in range(H_B)], axis=1)
        lhs[key] = ((rest[d][bi, :, OFF_BQ:OFF_BQ + qw] * iw_x).astype(BF16), dw.astype(BF16))
    num = {}
    for key in dirs:
        i = slot[key]
        parts = []
        for p in range(H_B // PAIR):
            ql, dl, cl = slice(p * PAIR * DK_B, (p + 1) * PAIR * DK_B), slice(p * LANES, (p + 1) * LANES), \
                slice(p * PAIR * CAUG_W, (p + 1) * PAIR * CAUG_W)
            parts.append(jnp.dot(jnp.concatenate([lhs[key][0][:, ql], lhs[key][1][:, dl]], axis=1),
                                 jnp.concatenate([bdc[i, ql, cl], bdv[i, dl, cl]], axis=0),
                                 preferred_element_type=F32))
        num[key] = jnp.concatenate(parts, axis=1)
    upd = {key: jnp.dot(ks_t[key], bdv[slot[key]], preferred_element_type=F32) for key in dirs}
    for key in dirs:
        bi, d = key
        lo = LF_OFF + d * H_B
        dec_x = jnp.concatenate(
            [jnp.broadcast_to(dec_row[key][:, lo + h:lo + h + 1], (1, CAUG_W)) for h in range(H_B)], axis=1)
        c_scr[bi, d] = c_scr[bi, d] * dec_x + upd[key]
    heads = [(key, h) for key in dirs for h in range(H_B)]
    rows = {d: pl.ds(pl.multiple_of((n if d == 0 else n_chunks - 1 - n) * CHUNK, CHUNK), CHUNK) for d in range(N_DIR)}
    tots = []
    for key, h in heads:
        bi, d = key
        vs = slice(h * DV_B, (h + 1) * DV_B)
        den = jnp.maximum(jnp.abs(num[key][:, h * CAUG_W + DV_B:(h + 1) * CAUG_W]),
                          _col_bcast(emt[key], LF_OFF + d * H_B + h))
        hb = jnp.concatenate([num[key][:, h * CAUG_W:h * CAUG_W + LANES] / den,
                              num[key][:, h * CAUG_W + LANES:h * CAUG_W + DV_B] / den], axis=1)
        tot = hb + o_scr[bi, rows[d], vs]
        o_scr[bi, rows[d], vs] = tot
        tots.append(tot)
    inv = [lax.rsqrt(jnp.mean(tot * tot, axis=-1, keepdims=True) + EPS) for tot in tots]
    for (key, h), tot, r in zip(heads, tots, inv):
        bi, d = key
        vs = slice(h * DV_B, (h + 1) * DV_B)
        ogate = rest[d][bi, :, OFF_BO + h * DV_B:OFF_BO + (h + 1) * DV_B]
        yb_ref[bi, rows[d], vs] = (tot * r * norm_ref[...] * ogate).astype(BF16)

    if emit_state:
        @pl.when(n == n_chunks - 1)
        def _():
            for bi, d in dirs:
                for h in range(H_B):
                    cout_ref[bi, d, h] = c_scr[bi, d, :, h * CAUG_W:h * CAUG_W + DV_B]
                    nout_ref[bi, d, h] = c_scr[bi, d, :, h * CAUG_W + DV_B:(h + 1) * CAUG_W]
            mout_ref[...] = m_scr[...]


def _mlstm(rest, gates, norm_b, c0, m0, batch, n_chunks, emit_state):
    t = n_chunks * CHUNK
    has_init = c0 is not None
    bb = SCAN_BB
    rest, gates = (a.reshape(batch, t, a.shape[-1]) for a in (rest, gates))
    mspec = pl.BlockSpec((bb, N_DIR, 8, LANES), lambda b, n: (b, 0, 0, 0))
    in_specs = [*_scan_specs(bb, n_chunks, REST_W), *_scan_specs(bb, n_chunks, LANES),
                pl.BlockSpec((1, DV_B), lambda b, n: (0, 0))]
    args = [rest, rest, gates, gates, norm_b]
    if has_init:
        in_specs += [pl.BlockSpec((bb, N_DIR, DK_B, MW), lambda b, n: (b, 0, 0, 0)), mspec]
        args += [c0, m0]
    out_specs = [pl.BlockSpec((bb, t, B_W), lambda b, n: (b, 0, 0))]
    out_shape = [jax.ShapeDtypeStruct((batch, t, B_W), BF16)]
    if emit_state:
        out_specs += [pl.BlockSpec((bb, N_DIR, H_B, DK_B, DV_B), lambda b, n: (b, 0, 0, 0, 0)),
                      pl.BlockSpec((bb, N_DIR, H_B, DK_B, LANES), lambda b, n: (b, 0, 0, 0, 0)), mspec]
        out_shape += [jax.ShapeDtypeStruct((batch, N_DIR, H_B, DK_B, DV_B), F32),
                      jax.ShapeDtypeStruct((batch, N_DIR, H_B, DK_B, LANES), F32),
                      jax.ShapeDtypeStruct((batch, N_DIR, 8, LANES), F32)]
    out = pl.pallas_call(
        functools.partial(_mlstm_body, n_chunks=n_chunks, bb=bb, has_init=has_init, emit_state=emit_state),
        grid=(batch // bb, n_chunks),
        in_specs=in_specs,
        out_specs=out_specs,
        out_shape=out_shape,
        scratch_shapes=[pltpu.VMEM((bb, N_DIR, DK_B, MW), F32),
                        pltpu.VMEM((bb, N_DIR, 8, LANES), F32),
                        pltpu.VMEM((bb, t, B_W), F32),
                        pltpu.VMEM((bb * N_DIR, H_B * CHUNK, H_B * DK_B), BF16),
                        pltpu.VMEM((bb * N_DIR, H_B * DK_B, MW), BF16),
                        pltpu.VMEM((bb * N_DIR, H_B * CHUNK, MW), BF16)],
        compiler_params=_params(("arbitrary", "arbitrary")),
        name="mlstm_scan",
    )(*args)
    return [out[0].reshape(batch * t, B_W), *out[1:]]


OUT_RB = 256


def _outproj_body(ya_ref, yb_ref, wa_ref, wb_ref, x_ref, mod_ref, post1_ref, pre2_ref, x1_ref, h2_ref):
    for rb in range(x_ref.shape[0] // OUT_RB):
        rs = slice(rb * OUT_RB, (rb + 1) * OUT_RB)
        mix = (jnp.dot(ya_ref[rs, :], wa_ref[...], preferred_element_type=F32)
               + jnp.dot(yb_ref[rs, :], wb_ref[...], preferred_element_type=F32))
        x1 = x_ref[rs, :] + mod_ref[0, 2:3, :] * (_rms(mix) * post1_ref[...])
        x1_ref[rs, :] = x1
        h2 = _rms(x1) * pre2_ref[...] * (1.0 + mod_ref[0, 4:5, :]) + mod_ref[0, 3:4, :]
        h2_ref[rs, :] = h2.astype(BF16)


def _outproj(ya, yb, w_out, x2d, mod3, mod_map, post1, pre2, tm):
    m = x2d.shape[0]
    row = lambda i: (i, 0)
    const = lambda i: (0, 0)
    return pl.pallas_call(
        _outproj_body,
        grid=(m // tm,),
        in_specs=[pl.BlockSpec((tm, A_W), row), pl.BlockSpec((tm, B_W), row),
                  pl.BlockSpec((A_W, D_MODEL), lambda i: (0, 0)),
                  pl.BlockSpec((B_W, D_MODEL), lambda i: (1, 0)),
                  pl.BlockSpec((tm, D_MODEL), row),
                  pl.BlockSpec((1, 6, D_MODEL), lambda i: (mod_map(i * tm), 0, 0)),
                  pl.BlockSpec((1, D_MODEL), const), pl.BlockSpec((1, D_MODEL), const)],
        out_specs=[pl.BlockSpec((tm, D_MODEL), row), pl.BlockSpec((tm, D_MODEL), row)],
        out_shape=[jax.ShapeDtypeStruct((m, D_MODEL), F32), jax.ShapeDtypeStruct((m, D_MODEL), BF16)],
        compiler_params=_params(("parallel",)),
        name="outproj",
    )(ya, yb, w_out, w_out, x2d, mod3, post1, pre2)


def _ffn_body(h2_ref, w1_ref, w2_ref, x1_ref, mod_ref, post2_ref, o_ref):
    kk = pl.program_id(1)
    a = jnp.maximum(jnp.dot(h2_ref[...], w1_ref[...], preferred_element_type=F32), 0.0)
    contrib = jnp.dot((a * a).astype(BF16), w2_ref[...], preferred_element_type=F32)

    @pl.when(kk == 0)
    def _():
        o_ref[...] = contrib

    @pl.when((kk > 0) & (kk < pl.num_programs(1) - 1))
    def _():
        o_ref[...] += contrib

    @pl.when(kk == pl.num_programs(1) - 1)
    def _():
        f = o_ref[...] + contrib
        o_ref[...] = x1_ref[...] + mod_ref[0, 5:6, :] * (_rms(f) * post2_ref[...])


def _ffn(h2, w1, w2, x1, mod3, mod_map, post2, tm, fc):
    m = h2.shape[0]
    return pl.pallas_call(
        _ffn_body,
        grid=(m // tm, FFN // fc),
        in_specs=[pl.BlockSpec((tm, D_MODEL), lambda i, k: (i, 0)),
                  pl.BlockSpec((D_MODEL, fc), lambda i, k: (0, k)),
                  pl.BlockSpec((fc, D_MODEL), lambda i, k: (k, 0)),
                  pl.BlockSpec((tm, D_MODEL), lambda i, k: (i, 0)),
                  pl.BlockSpec((1, 6, D_MODEL), lambda i, k: (mod_map(i * tm), 0, 0)),
                  pl.BlockSpec((1, D_MODEL), lambda i, k: (0, 0))],
        out_specs=pl.BlockSpec((tm, D_MODEL), lambda i, k: (i, 0)),
        out_shape=jax.ShapeDtypeStruct((m, D_MODEL), F32),
        compiler_params=_params(("parallel", "arbitrary")),
        name="ffn",
    )(h2, w1, w2, x1, mod3, post2)


def _block(x, mod3, mod_of_row, lp, init, seq_len, emit_state):
    bsz, t, _ = x.shape
    x2d = x.reshape(bsz * t, D_MODEL)
    n_chunks = t // CHUNK
    qkv, rest, gates = _proj(x2d, mod3, lambda i: mod_of_row(i * PROJ_TM), lp["pre1"], lp["w16"],
                             lp["w_mid"], lp["w_gate"], lp["conv_w"], lp["gate_p"], seq_len)
    s0, c0, m0 = init if init is not None else (None, None, None)
    d_out = _delta(qkv, rest, gates, lp["norm_a"], s0, bsz, n_chunks, emit_state)
    m_out = _mlstm(rest, gates, lp["norm_b"], c0, m0, bsz, n_chunks, emit_state)
    x1, h2 = _outproj(d_out[0], m_out[0], lp["w_out"], x2d, mod3, mod_of_row, lp["post1"], lp["pre2"], 512)
    y = _ffn(h2, lp["w1"], lp["w2"], x1, mod3, mod_of_row, lp["post2"], 512, 1024)
    states = None
    if emit_state:
        m_fin = jnp.stack([m_out[3][:, d, 0, LF_OFF + d * H_B:LF_OFF + (d + 1) * H_B] for d in range(N_DIR)], axis=1)
        states = (d_out[1], m_out[1], m_out[2][..., 0], m_fin)
    return y.reshape(bsz, t, D_MODEL), states


def _layer_params(l, norm_mix_pre, norm_mix_post, norm_ffn_pre, norm_ffn_post, w_in, conv_w, a_log, dt_bias,
                  norm_a, mlstm_ibias, mlstm_fbias, norm_b, w_out, w_ffn1, w_ffn2):
    w = w_in[l]
    o_ag = QKV_W
    o_aa = o_ag + A_W
    o_ab = o_aa + N_DIR * H_A
    o_bq = o_ab + N_DIR * H_A
    o_bi = o_bq + 2 * H_B * DK_B + 2 * B_W
    o_bf = o_bi + N_DIR * H_B
    n_gate = 2 * N_DIR * H_A + 2 * N_DIR * H_B
    w16 = w.astype(BF16)
    w_gate = jnp.concatenate([w16[:, o_aa:o_bq], w16[:, o_bi:o_bf + N_DIR * H_B],
                              jnp.zeros((D_MODEL, LANES - n_gate), BF16)], axis=1)

    def lane_row(vals, off):
        return jnp.zeros((LANES,), F32).at[off:off + vals.size].set(vals.reshape(-1))

    gate_p = jnp.stack([lane_row(a_log[l], G_OFF), lane_row(dt_bias[l], G_OFF),
                        lane_row(mlstm_ibias[l], LI_OFF) + lane_row(mlstm_fbias[l], LF_OFF)]
                       + [jnp.zeros((LANES,), F32)] * 5)
    row = lambda v: v[l].reshape(1, -1)
    return dict(
        pre1=row(norm_mix_pre), post1=row(norm_mix_post), pre2=row(norm_ffn_pre), post2=row(norm_ffn_post),
        w16=w16, w_mid=w16[:, o_bq:o_bi], w_gate=w_gate, gate_p=gate_p,
        conv_w=jnp.concatenate([conv_w[l].T, jnp.zeros((5, QKV_W), F32)], axis=0),
        norm_a=row(norm_a), norm_b=row(norm_b),
        w_out=w_out[l].astype(BF16), w1=w_ffn1[l].astype(BF16), w2=w_ffn2[l].astype(BF16))


def kernel(x_prompt, x_sample, state_delta, state_mlstm_C, state_mlstm_n, state_mlstm_m, c, c_ctx, w_ada, b_ada, norm_mix_pre, norm_mix_post, norm_ffn_pre, norm_ffn_post, w_in, conv_w, a_log, dt_bias, norm_a, mlstm_ibias, mlstm_fbias, norm_b, w_out, w_ffn1, w_ffn2):
    depth = w_in.shape[0]
    n_lat = x_sample.shape[0]
    t_lat = x_sample.shape[1]
    cond = jnp.concatenate([c_ctx[None, :], c, jnp.zeros((8 - 1 - n_lat, D_MODEL), F32)], axis=0)
    y_prompt, y_sample = x_prompt, x_sample
    acc = ([], [], [], [])
    for l in range(depth):
        lp = _layer_params(l, norm_mix_pre, norm_mix_post, norm_ffn_pre, norm_ffn_post, w_in, conv_w, a_log,
                           dt_bias, norm_a, mlstm_ibias, mlstm_fbias, norm_b, w_out, w_ffn1, w_ffn2)
        mod = _ada(cond, w_ada, b_ada[l].reshape(1, -1), l)
        mod3 = mod[:1 + n_lat].reshape(1 + n_lat, 6, D_MODEL)
        y_prompt, st = _block(y_prompt, mod3, lambda r: 0, lp, None, x_prompt.shape[1], True)
        for a, s in zip(acc, st):
            a.append(s)
        n_rep = jnp.broadcast_to(state_mlstm_n[:, l][..., None], state_mlstm_n[:, l].shape + (LANES,))
        c_aug0 = jnp.concatenate([state_mlstm_C[:, l], n_rep], axis=-1)
        c_aug0 = c_aug0.transpose(0, 1, 3, 2, 4).reshape(n_lat, N_DIR, DK_B, MW)
        m0 = jnp.zeros((n_lat, N_DIR, LANES), F32)
        for d in range(N_DIR):
            m0 = m0.at[:, d, LF_OFF + d * H_B:LF_OFF + (d + 1) * H_B].set(state_mlstm_m[:, l, d])
        m0 = jnp.broadcast_to(m0[:, :, None, :], (n_lat, N_DIR, 8, LANES))
        y_sample, _ = _block(y_sample, mod3, lambda r: 1 + r // t_lat, lp, (state_delta[:, l], c_aug0, m0),
                             GRID_W, False)
    return (y_prompt, y_sample) + tuple(jnp.stack(a, axis=1) for a in acc)
```

```python
import functools

import jax
import jax.numpy as jnp
from jax import lax
from jax.experimental import pallas as pl
from jax.experimental.pallas import tpu as pltpu

F32 = jnp.float32
BF16 = jnp.bfloat16

D_MODEL = 2048
N_DIR = 2
A_W = D_MODEL // 2
B_W = D_MODEL - A_W
DK_A = 128
DV_A = 128
H_A = A_W // DV_A
DV_B = 256
DK_B = DV_B // 2
H_B = B_W // DV_B
GRID_W = 64
CHUNK = 64
FFN = 4 * D_MODEL
EPS = 1e-6
LANES = 128
NEG = -1e30

QKV_W = 3 * A_W
REST_W = A_W + 2 * H_B * DK_B + 2 * B_W
OFF_AG, OFF_BQ, OFF_BK, OFF_BV, OFF_BO = 0, A_W, A_W + H_B * DK_B, A_W + 2 * H_B * DK_B, A_W + 2 * H_B * DK_B + B_W
G_OFF, BETA_OFF = 0, N_DIR * H_A
LI_OFF = 2 * N_DIR * H_A
LF_OFF = LI_OFF + N_DIR * H_B

VMEM_LIMIT = 56 * 1024 * 1024


def _sigmoid(x):
    return 1.0 / (1.0 + jnp.exp(-x))


def _softplus(x):
    return jnp.maximum(x, 0.0) + jnp.log1p(jnp.exp(-jnp.abs(x)))


def _dot(a, b):
    return jnp.dot(a.astype(BF16), b.astype(BF16), preferred_element_type=F32)


def _dot_f32(a, b):
    return jnp.dot(a, b, precision=lax.Precision.HIGHEST, preferred_element_type=F32)


def _rms(x):
    return x * lax.rsqrt(jnp.mean(x * x, axis=-1, keepdims=True) + EPS)


def _params(sem):
    return pltpu.CompilerParams(dimension_semantics=sem, vmem_limit_bytes=VMEM_LIMIT)


def _ada_body(c_ref, w_ref, b_ref, o_ref):
    c = c_ref[...]
    o_ref[...] = _dot(c * _sigmoid(c), w_ref[...]) + b_ref[...]


def _ada(c_all, w_ada, b, l):
    n = w_ada.shape[2]
    tn = 1024
    return pl.pallas_call(
        _ada_body,
        grid=(n // tn,),
        in_specs=[pl.BlockSpec(c_all.shape, lambda j: (0, 0)),
                  pl.BlockSpec((None, D_MODEL, tn), lambda j: (l, 0, j)),
                  pl.BlockSpec((1, tn), lambda j: (0, j))],
        out_specs=pl.BlockSpec((c_all.shape[0], tn), lambda j: (0, j)),
        out_shape=jax.ShapeDtypeStruct((c_all.shape[0], n), F32),
        compiler_params=_params(("arbitrary",)),
        name="ada",
    )(c_all, w_ada, b)


PROJ_TM = 1024
PROJ_RB = 256
PROJ_TN = 512
N_QKV_T = QKV_W // PROJ_TN
N_REST_T = REST_W // PROJ_TN
N_HEAD_T = (QKV_W + A_W) // PROJ_TN
N_MID_T = N_QKV_T + N_REST_T - N_HEAD_T


def _proj_body(x_ref, mod_ref, g_ref, wh_ref, wb_ref, wg_ref, cw_ref, gp_ref, qkv_ref, rest_ref, gate_ref, h_scr, *,
               seq_len):
    j = pl.program_id(1)
    blocks = [slice(rb * PROJ_RB, (rb + 1) * PROJ_RB) for rb in range(PROJ_TM // PROJ_RB)]

    def run(w_ref, epilogue, out_ref):
        for rs in blocks:
            epilogue(jnp.dot(h_scr[rs, :], w_ref[...], preferred_element_type=F32), out_ref, rs)

    def run_first():
        for rs in blocks:
            y = _rms(x_ref[rs, :]) * g_ref[...]
            h_scr[rs, :] = (y * (1.0 + mod_ref[0, 1:2, :]) + mod_ref[0, 0:1, :]).astype(BF16)
            ep_l2(DK_A ** -0.5)(jnp.dot(h_scr[rs, :], wh_ref[...], preferred_element_type=F32), qkv_ref, rs)
            ep_gate(jnp.dot(h_scr[rs, :], wg_ref[...], preferred_element_type=F32), gate_ref, rs)

    def conv_silu(acc, ls):
        a = acc[:, ls]
        pos = lax.broadcasted_iota(jnp.int32, a.shape, 0) & (seq_len - 1)
        prev = jnp.where(pos == 0, 0.0, pltpu.roll(a, 1, 0))
        nxt = jnp.where(pos == seq_len - 1, 0.0, pltpu.roll(a, a.shape[0] - 1, 0))
        y = prev * cw_ref[0:1, ls] + a * cw_ref[1:2, ls] + nxt * cw_ref[2:3, ls]
        return y * _sigmoid(y)

    lane_groups = [slice(g * LANES, (g + 1) * LANES) for g in range(PROJ_TN // LANES)]

    def ep_l2(scale):
        def f(acc, out_ref, rs):
            for ls in lane_groups:
                blk = conv_silu(acc, ls)
                inv = lax.rsqrt(jnp.sum(blk * blk, axis=-1, keepdims=True) + EPS)
                out_ref[rs, ls] = blk * (inv * scale)
        return f

    def ep_conv(acc, out_ref, rs):
        for ls in lane_groups:
            out_ref[rs, ls] = conv_silu(acc, ls)

    def ep_map(fn):
        def f(acc, out_ref, rs):
            out_ref[rs, :] = fn(acc)
        return f

    def ep_gate(z, out_ref, rs):
        lane = lax.broadcasted_iota(jnp.int32, z.shape, 1)
        g = -jnp.exp(gp_ref[0:1, :]) * _softplus(z + gp_ref[1:2, :])
        li = z + gp_ref[2:3, :]
        out_ref[rs, :] = jnp.where(lane < BETA_OFF, g,
                                   jnp.where(lane < LI_OFF, _sigmoid(z),
                                             jnp.where(lane < LF_OFF, li,
                                                       jnp.where(lane < LF_OFF + N_DIR * H_B, -_softplus(-li), 0.0))))

    tp = A_W // PROJ_TN
    c0 = (j - N_QKV_T) * PROJ_TN
    in_rest = (j >= N_QKV_T) & (j < N_QKV_T + N_REST_T)
    pl.when(j == 0)(run_first)
    variants = [
        ((j > 0) & (j < tp), wh_ref, ep_l2(DK_A ** -0.5), qkv_ref),
        ((j >= tp) & (j < 2 * tp), wh_ref, ep_l2(1.0), qkv_ref),
        ((j >= 2 * tp) & (j < N_QKV_T), wh_ref, ep_conv, qkv_ref),
        (in_rest & (c0 < OFF_BQ), wh_ref, ep_map(lambda a: a * _sigmoid(a)), rest_ref),
        (in_rest & (c0 >= OFF_BQ) & (c0 < OFF_BK), wb_ref, ep_map(lambda a: a * (DK_B ** -0.5)), rest_ref),
        (in_rest & (c0 >= OFF_BK) & (c0 < OFF_BO), wb_ref, ep_map(lambda a: a), rest_ref),
        (in_rest & (c0 >= OFF_BO), wb_ref, ep_map(_sigmoid), rest_ref),
    ]
    for cond, w_ref, epilogue, out_ref in variants:
        pl.when(cond)(functools.partial(run, w_ref, epilogue, out_ref))


def _proj(x2d, mod3, mod_map, pre_g, w16, w_mid, w_gate, conv_w, gate_p, seq_len):
    m = x2d.shape[0]
    tm, tn = PROJ_TM, PROJ_TN
    assert PROJ_RB % seq_len == 0 and m % tm == 0
    return pl.pallas_call(
        functools.partial(_proj_body, seq_len=seq_len),
        grid=(m // tm, N_HEAD_T + N_MID_T),
        in_specs=[pl.BlockSpec((tm, D_MODEL), lambda i, j: (i, 0)),
                  pl.BlockSpec((1, 6, D_MODEL), lambda i, j: (mod_map(i), 0, 0)),
                  pl.BlockSpec((1, D_MODEL), lambda i, j: (0, 0)),
                  pl.BlockSpec((D_MODEL, tn), lambda i, j: (0, jnp.minimum(j, N_HEAD_T - 1))),
                  pl.BlockSpec((D_MODEL, tn), lambda i, j: (0, jnp.clip(j - N_HEAD_T, 0, N_MID_T - 1))),
                  pl.BlockSpec((D_MODEL, LANES), lambda i, j: (0, 0)),
                  pl.BlockSpec((8, tn), lambda i, j: (0, jnp.minimum(j, N_QKV_T - 1))),
                  pl.BlockSpec((8, LANES), lambda i, j: (0, 0))],
        out_specs=[pl.BlockSpec((tm, tn), lambda i, j: (i, jnp.minimum(j, N_QKV_T - 1))),
                   pl.BlockSpec((tm, tn), lambda i, j: (i, jnp.clip(j - N_QKV_T, 0, N_REST_T - 1))),
                   pl.BlockSpec((tm, LANES), lambda i, j: (i, 0))],
        out_shape=[jax.ShapeDtypeStruct((m, QKV_W), F32), jax.ShapeDtypeStruct((m, REST_W), F32),
                   jax.ShapeDtypeStruct((m, LANES), F32)],
        scratch_shapes=[pltpu.VMEM((tm, D_MODEL), BF16)],
        compiler_params=_params(("parallel", "arbitrary")),
        name="proj",
    )(x2d, mod3, pre_g, w16, w_mid, w_gate, conv_w, gate_p)


def _masks(d):
    row = lax.broadcasted_iota(jnp.int32, (CHUNK, CHUNK), 0)
    col = lax.broadcasted_iota(jnp.int32, (CHUNK, CHUNK), 1)
    if d == 0:
        return row >= col, row > col, row == col
    return row <= col, row < col, row == col


GROUP = 4
N_GROUPS = H_A // GROUP
PAIR = LANES // CHUNK


def _split(x):
    hi = x.astype(BF16)
    return hi, (x - hi.astype(F32)).astype(BF16)


def _store_blocks(ref, x, blk_r, blk_c, col0=0):
    for h in range(GROUP):
        ref[h * blk_r:(h + 1) * blk_r, col0 + h * blk_c:col0 + (h + 1) * blk_c] = x[:, h * blk_c:(h + 1) * blk_c]


def _dot3(a_hi, a_lo, b_hi, b_lo):
    m = a_hi.shape[0]
    r = jnp.dot(jnp.concatenate([a_hi, a_lo], axis=0), b_hi, preferred_element_type=F32)
    return r[:m] + r[m:] + jnp.dot(a_hi, b_lo, preferred_element_type=F32)


HALF = CHUNK // 2


def _tri_inverse_wide(lmats, upper, bd_hi, bd_lo, by_hi, by_lo):
    width = lmats[0].shape[1]
    n_blk = width // HALF
    row = lax.broadcasted_iota(jnp.int32, (HALF, width), 0)
    lane = lax.broadcasted_iota(jnp.int32, (HALF, width), 1)
    lead = (lane & (CHUNK - 1)) < HALF
    eye_d = (row == (lane & (HALF - 1))).astype(F32)

    def bd_dot(i, a, b_hi, b_lo):
        for blk in range(n_blk):
            sl = slice(blk * HALF, (blk + 1) * HALF)
            bd_hi[i, sl, sl] = b_hi[:, sl]
            bd_lo[i, sl, sl] = b_lo[:, sl]
        return _dot3(*a, bd_hi[i], bd_lo[i])

    idx = range(len(lmats))
    diag = [jnp.where(lead, l[:HALF], l[HALF:]) for l in lmats]
    s = [eye_d - dg for dg in diag]
    p = []
    for i in idx:
        m_hi, m_lo = _split(-diag[i])
        p.append(bd_dot(i, (m_hi, m_lo), m_hi, m_lo))
    for _ in range(3):
        for i in idx:
            p_hi, p_lo = _split(p[i])
            s_hi, s_lo = _split(s[i])
            r = bd_dot(i, (jnp.concatenate([p_hi, s_hi], axis=0), jnp.concatenate([p_lo, s_lo], axis=0)), p_hi, p_lo)
            p[i] = r[:HALF]
            s[i] = s[i] + r[HALF:]
    for i in idx:
        p_hi, p_lo = _split(p[i])
        s[i] = s[i] + bd_dot(i, _split(s[i]), p_hi, p_lo)
    y = []
    for i in idx:
        c_blk = jnp.where(lead, 0.0, lmats[i][:HALF]) if upper[i] else jnp.where(lead, lmats[i][HALF:], 0.0)
        y.append(bd_dot(i, _split(c_blk), *_split(s[i])))
    out = []
    for i in idx:
        y_hi, y_lo = _split(y[i])
        for h in range(width // CHUNK):
            a_sl, b_sl = slice(h * CHUNK, h * CHUNK + HALF), slice(h * CHUNK + HALF, (h + 1) * CHUNK)
            rs, cs = (a_sl, b_sl) if upper[i] else (b_sl, a_sl)
            by_hi[i, rs, cs] = y_hi[:, cs]
            by_lo[i, rs, cs] = y_lo[:, cs]
        ai, bi = jnp.where(lead, s[i], 0.0), jnp.where(lead, 0.0, s[i])
        x = _dot3(*_split(ai if upper[i] else bi), by_hi[i], by_lo[i])
        out.append(jnp.concatenate([ai - x, bi] if upper[i] else [ai, bi - x], axis=0))
    return out


def _col_bcast(tile, c, width=LANES):
    return jnp.broadcast_to(tile[:, c:c + 1], (tile.shape[0], width))


def _delta_body(*refs, n_chunks, bb, has_init, emit_state):
    qkv = refs[0:2]
    ag = refs[2:4]
    gt = refs[4:6]
    norm_ref = refs[6]
    pos = 7
    s0_ref = None
    if has_init:
        s0_ref = refs[pos]
        pos += 1
    ya_ref = refs[pos]
    pos += 1
    sout_ref = None
    if emit_state:
        sout_ref = refs[pos]
        pos += 1
    s_scr, o_scr, bdn_hi, bdn_lo, bdy_hi, bdy_lo, bdk, bduw, bds, bdv = refs[pos:pos + 10]

    n = pl.program_id(1)
    gw = GROUP * DK_A

    @pl.when((n == 0) & (pl.program_id(0) == 0))
    def _():
        for ref in (bdn_hi, bdn_lo, bdy_hi, bdy_lo, bdk, bduw, bds, bdv):
            ref[...] = jnp.zeros_like(ref)

    @pl.when(n == 0)
    def _():
        o_scr[...] = jnp.zeros_like(o_scr)
        for bi in range(bb):
            for d in range(N_DIR):
                for h in range(H_A):
                    blk = s0_ref[bi, d, h] if has_init else jnp.zeros((DK_A, DV_A), F32)
                    s_scr[bi, d, h // GROUP, :, (h % GROUP) * DV_A:(h % GROUP + 1) * DV_A] = blk

    row = lax.broadcasted_iota(jnp.int32, (CHUNK, LANES), 0)
    lane = lax.broadcasted_iota(jnp.int32, (CHUNK, LANES), 1)
    col = lane & (CHUNK - 1)
    left = lane < CHUNK
    probs = [(bi, d, g) for bi in range(bb) for d in range(N_DIR) for g in range(N_GROUPS)]
    masks = {0: (row >= col, row > col), 1: (row <= col, row < col)}
    gates, gc, gc_t, eg, egl = {}, {}, {}, {}, {}
    for bi in range(bb):
        for d in range(N_DIR):
            key = (bi, d)
            gates[key] = gt[d][bi]
            gc[key] = _dot_f32(_masks(d)[0].astype(F32), gates[key])
            gc_t[key] = jnp.concatenate([gc[key], gc[key]], axis=0).T
            last = CHUNK - 1 if d == 0 else 0
            eg[key] = jnp.exp(gc[key])
            egl[key] = jnp.exp(gc[key][last:last + 1, :] - gc[key])

    def cols_of(d, g):
        return [G_OFF + d * H_A + g * GROUP + hl for hl in range(GROUP)]

    q16, ks, beta_xs, decays, grams = [], [], [], [], []
    for gi, (bi, d, g) in enumerate(probs):
        key = (bi, d)
        cols = cols_of(d, g)
        incl, _ = masks[d]
        k = qkv[d][bi, :, A_W + g * gw:A_W + (g + 1) * gw]
        beta_x = jnp.concatenate([_col_bcast(gates[key], BETA_OFF - G_OFF + c) for c in cols], axis=1)
        decay = []
        for p in range(GROUP // PAIR):
            c0, c1 = cols[PAIR * p], cols[PAIR * p + 1]
            gcol = jnp.where(left, _col_bcast(gc[key], c0), _col_bcast(gc[key], c1))
            grow = jnp.where(left[0:1], gc_t[key][c0:c0 + 1, :], gc_t[key][c1:c1 + 1, :])
            decay.append(jnp.exp(jnp.where(incl, gcol - grow, NEG)))
        decays.append(jnp.concatenate(decay, axis=1))
        q16.append(qkv[d][bi, :, g * gw:(g + 1) * gw].astype(BF16))
        ks.append(k)
        beta_xs.append(beta_x)
        _store_blocks(bdk.at[gi], k.astype(BF16), CHUNK, DK_A)
    for gi in range(len(probs)):
        grams.append(lax.dot_general(jnp.concatenate([q16[gi], (ks[gi] * beta_xs[gi]).astype(BF16)], axis=0),
                                     bdk[gi], (((1,), (1,)), ((), ())), preferred_element_type=F32))
    attns, lmats = [], []
    for gi, (bi, d, g) in enumerate(probs):
        strict_w = jnp.concatenate([masks[d][1]] * (GROUP // PAIR), axis=1)
        attns.append((grams[gi][:CHUNK] * decays[gi]).astype(BF16))
        lmats.append(jnp.where(strict_w, grams[gi][CHUNK:] * decays[gi], 0.0))
    ainvs = _tri_inverse_wide(lmats, [d == 1 for _, d, _ in probs], bdn_hi, bdn_lo, bdy_hi, bdy_lo)
    eg_xs, kd_ts = [], []
    for gi, (bi, d, g) in enumerate(probs):
        key = (bi, d)
        cols = cols_of(d, g)
        v = qkv[d][bi, :, 2 * A_W + g * gw:2 * A_W + (g + 1) * gw]
        eg_x = jnp.concatenate([_col_bcast(eg[key], c) for c in cols], axis=1)
        egl_x = jnp.concatenate([_col_bcast(egl[key], c) for c in cols], axis=1)
        eg_xs.append(eg_x)
        _store_blocks(bduw.at[gi], (v * beta_xs[gi]).astype(BF16), CHUNK, DV_A)
        _store_blocks(bduw.at[gi], (ks[gi] * (beta_xs[gi] * eg_x)).astype(BF16), CHUNK, DK_A, col0=gw)
        kd = ks[gi] * egl_x
        kd_ts.append(jnp.concatenate([kd[:, hl * DK_A:(hl + 1) * DK_A] for hl in range(GROUP)], axis=0)
                     .T.astype(BF16))
        _store_blocks(bds.at[gi], s_scr[bi, d, g].astype(BF16), DK_A, DV_A)
    uws = []
    for gi in range(len(probs)):
        t_hi, t_lo = _split(ainvs[gi])
        r = jnp.dot(jnp.concatenate([t_hi, t_lo], axis=0), bduw[gi], preferred_element_type=F32)
        uws.append(r[:CHUNK] + r[CHUNK:])
    ws_qs = []
    hw = gw // 2
    for gi in range(len(probs)):
        wq = jnp.concatenate([uws[gi][:, gw:].astype(BF16), q16[gi]], axis=0)
        ws_qs.append(jnp.concatenate(
            [jnp.dot(wq[:, p * hw:(p + 1) * hw], bds[gi, p * hw:(p + 1) * hw, p * hw:(p + 1) * hw],
                     preferred_element_type=F32) for p in range(2)], axis=1))
    for gi in range(len(probs)):
        _store_blocks(bdv.at[gi], (uws[gi][:, :gw] - ws_qs[gi][:CHUNK]).astype(BF16), CHUNK, DV_A)
    rs = []
    for gi in range(len(probs)):
        rs.append(jnp.dot(jnp.concatenate([attns[gi], kd_ts[gi]], axis=0), bdv[gi], preferred_element_type=F32))
    tots = []
    for gi, (bi, d, g) in enumerate(probs):
        cols = cols_of(d, g)
        last = CHUNK - 1 if d == 0 else 0
        o = ws_qs[gi][CHUNK:] * eg_xs[gi] + rs[gi][:CHUNK]
        eg_last = jnp.concatenate(
            [jnp.broadcast_to(eg[bi, d][last:last + 1, c:c + 1], (1, DV_A)) for c in cols], axis=1)
        s_scr[bi, d, g] = s_scr[bi, d, g] * eg_last + rs[gi][CHUNK:]
        cidx = n if d == 0 else n_chunks - 1 - n
        rows = pl.ds(pl.multiple_of(cidx * CHUNK, CHUNK), CHUNK)
        tot = o + o_scr[bi, rows, g * gw:(g + 1) * gw]
        o_scr[bi, rows, g * gw:(g + 1) * gw] = tot
        tots.append((rows, tot))
    blks = [(bi, d, rows, g * GROUP + hl, tot[:, hl * DV_A:(hl + 1) * DV_A])
            for (bi, d, g), (rows, tot) in zip(probs, tots) for hl in range(GROUP)]
    inv = [lax.rsqrt(jnp.mean(blk * blk, axis=-1, keepdims=True) + EPS) for *_, blk in blks]
    for (bi, d, rows, h, blk), r in zip(blks, inv):
        hs = slice(h * DV_A, (h + 1) * DV_A)
        ya_ref[bi, rows, hs] = (blk * r * norm_ref[...] * ag[d][bi, :, hs]).astype(BF16)

    if emit_state:
        @pl.when(n == n_chunks - 1)
        def _():
            for bi in range(bb):
                for d in range(N_DIR):
                    for h in range(H_A):
                        sout_ref[bi, d, h] = s_scr[bi, d, h // GROUP, :, (h % GROUP) * DV_A:(h % GROUP + 1) * DV_A]


SCAN_BB = 2


def _scan_specs(bb, n_chunks, width):
    return (pl.BlockSpec((bb, CHUNK, width), lambda b, n: (b, n, 0)),
            pl.BlockSpec((bb, CHUNK, width), lambda b, n: (b, n_chunks - 1 - n, 0)))


def _delta(qkv, rest, gates, norm_a, s0, batch, n_chunks, emit_state):
    t = n_chunks * CHUNK
    has_init = s0 is not None
    bb = SCAN_BB
    ng = bb * N_DIR * N_GROUPS
    qkv, rest, gates = (a.reshape(batch, t, a.shape[-1]) for a in (qkv, rest, gates))
    in_specs = [*_scan_specs(bb, n_chunks, QKV_W), *_scan_specs(bb, n_chunks, A_W), *_scan_specs(bb, n_chunks, LANES),
                pl.BlockSpec((1, DV_A), lambda b, n: (0, 0))]
    args = [qkv, qkv, rest, rest, gates, gates, norm_a]
    sspec = pl.BlockSpec((bb, N_DIR, H_A, DK_A, DV_A), lambda b, n: (b, 0, 0, 0, 0))
    if has_init:
        in_specs.append(sspec)
        args.append(s0)
    out_specs = [pl.BlockSpec((bb, t, A_W), lambda b, n: (b, 0, 0))]
    out_shape = [jax.ShapeDtypeStruct((batch, t, A_W), BF16)]
    if emit_state:
        out_specs.append(sspec)
        out_shape.append(jax.ShapeDtypeStruct((batch, N_DIR, H_A, DK_A, DV_A), F32))
    out = pl.pallas_call(
        functools.partial(_delta_body, n_chunks=n_chunks, bb=bb, has_init=has_init, emit_state=emit_state),
        grid=(batch // bb, n_chunks),
        in_specs=in_specs,
        out_specs=out_specs,
        out_shape=out_shape,
        scratch_shapes=[pltpu.VMEM((bb, N_DIR, N_GROUPS, DK_A, GROUP * DV_A), F32), pltpu.VMEM((bb, t, A_W), F32),
                        pltpu.VMEM((ng, GROUP * CHUNK, GROUP * CHUNK), BF16),
                        pltpu.VMEM((ng, GROUP * CHUNK, GROUP * CHUNK), BF16),
                        pltpu.VMEM((ng, GROUP * CHUNK, GROUP * CHUNK), BF16),
                        pltpu.VMEM((ng, GROUP * CHUNK, GROUP * CHUNK), BF16),
                        pltpu.VMEM((ng, GROUP * CHUNK, GROUP * DK_A), BF16),
                        pltpu.VMEM((ng, GROUP * CHUNK, 2 * GROUP * DK_A), BF16),
                        pltpu.VMEM((ng, GROUP * DK_A, GROUP * DV_A), BF16),
                        pltpu.VMEM((ng, GROUP * CHUNK, GROUP * DV_A), BF16)],
        compiler_params=_params(("arbitrary", "arbitrary")),
        name="delta_scan",
    )(*args)
    return [out[0].reshape(batch * t, A_W), *out[1:]]


CAUG_W = DV_B + LANES
MW = H_B * CAUG_W


def _scan_max(x, d):
    row = lax.broadcasted_iota(jnp.int32, x.shape, 0)
    s = 1
    while s < CHUNK:
        if d == 0:
            shifted = jnp.where(row >= s, pltpu.roll(x, s, 0), NEG)
        else:
            shifted = jnp.where(row < CHUNK - s, pltpu.roll(x, CHUNK - s, 0), NEG)
        x = jnp.maximum(x, shifted)
        s *= 2
    return x


def _mlstm_body(*refs, n_chunks, bb, has_init, emit_state):
    rest = refs[0:2]
    gt = refs[2:4]
    norm_ref = refs[4]
    pos = 5
    c0_ref = m0_ref = None
    if has_init:
        c0_ref, m0_ref = refs[pos], refs[pos + 1]
        pos += 2
    yb_ref = refs[pos]
    pos += 1
    cout_ref = nout_ref = mout_ref = None
    if emit_state:
        cout_ref, nout_ref, mout_ref = refs[pos:pos + 3]
        pos += 3
    c_scr, m_scr, o_scr, bdk, bdc, bdv = refs[pos:pos + 6]

    n = pl.program_id(1)
    qw = H_B * DK_B

    @pl.when((n == 0) & (pl.program_id(0) == 0))
    def _():
        for ref in (bdk, bdc, bdv):
            ref[...] = jnp.zeros_like(ref)
        for i in range(bb * N_DIR):
            for h in range(H_B):
                bdv[i, h * CHUNK:(h + 1) * CHUNK, h * CAUG_W + DV_B:(h + 1) * CAUG_W] = jnp.ones((CHUNK, LANES), BF16)

    @pl.when(n == 0)
    def _():
        o_scr[...] = jnp.zeros_like(o_scr)
        if has_init:
            c_scr[...] = c0_ref[...]
            m_scr[...] = m0_ref[...]
        else:
            c_scr[...] = jnp.zeros_like(c_scr)
            m_scr[...] = jnp.zeros_like(m_scr)

    row = lax.broadcasted_iota(jnp.int32, (CHUNK, LANES), 0)
    lane = lax.broadcasted_iota(jnp.int32, (CHUNK, LANES), 1)
    col = lane & (CHUNK - 1)
    left = lane < CHUNK
    dirs = [(bi, d) for bi in range(bb) for d in range(N_DIR)]
    slot = {key: i for i, key in enumerate(dirs)}
    q16 = {}
    for key in dirs:
        bi, d = key
        i = slot[key]
        q16[key] = rest[d][bi, :, OFF_BQ:OFF_BQ + qw].astype(BF16)
        _store_blocks(bdk.at[i], rest[d][bi, :, OFF_BK:OFF_BK + qw].astype(BF16), CHUNK, DK_B)
        for h in range(H_B):
            bdv[i, h * CHUNK:(h + 1) * CHUNK, h * CAUG_W:h * CAUG_W + DV_B] = (
                rest[d][bi, :, OFF_BV + h * DV_B:OFF_BV + (h + 1) * DV_B].astype(BF16))
        _store_blocks(bdc.at[i], c_scr[bi, d].astype(BF16), DK_B, CAUG_W)
    qk = {key: lax.dot_general(q16[key], bdk[slot[key]], (((1,), (1,)), ((), ())), preferred_element_type=F32)
          for key in dirs}
    nc, a_t, iw, emt, ksc, dec_row = {}, {}, {}, {}, {}, {}
    for key in dirs:
        bi, d = key
        lo = LF_OFF + d * H_B
        mine = (lane >= lo) & (lane < lo + H_B)
        g = gt[d][bi]
        gc = jnp.where(mine, _dot_f32(_masks(d)[0].astype(F32), g), 0.0)
        a = jnp.where(mine, pltpu.roll(g, LF_OFF - LI_OFF, 1), 0.0) - gc
        last = CHUNK - 1 if d == 0 else 0
        m_old = m_scr[bi, d][0:1, :]
        mx = jnp.maximum(m_old, _scan_max(a, d))
        mxl = mx[last:last + 1, :]
        nc[key] = -mx
        a_t[key] = jnp.concatenate([a, a], axis=0).T
        iw[key] = jnp.exp(m_old - mx)
        emt[key] = jnp.exp(-(gc + mx))
        ksc[key] = jnp.exp(a - mxl)
        dec_row[key] = jnp.exp(m_old - mxl)
        m_scr[bi, d] = jnp.broadcast_to(gc[last:last + 1, :] + mxl, (8, LANES))

    ks_t = {}
    for key in dirs:
        bi, d = key
        lo = LF_OFF + d * H_B
        ks = rest[d][bi, :, OFF_BK:OFF_BK + qw] * jnp.concatenate(
            [_col_bcast(ksc[key], lo + h) for h in range(H_B)], axis=1)
        ks_t[key] = jnp.concatenate([ks[:, h * DK_B:(h + 1) * DK_B] for h in range(H_B)], axis=0).T.astype(BF16)
    lhs = {}
    for key in dirs:
        bi, d = key
        lo = LF_OFF + d * H_B
        incl = row >= col if d == 0 else row <= col
        log_w = []
        for p in range(H_B // PAIR):
            l0, l1 = lo + PAIR * p, lo + PAIR * p + 1
            ccol = jnp.where(left, _col_bcast(nc[key], l0), _col_bcast(nc[key], l1))
            crow = jnp.where(left[0:1], a_t[key][l0:l0 + 1, :], a_t[key][l1:l1 + 1, :])
            log_w.append(jnp.where(incl, ccol + crow, NEG))
        dw = jnp.exp(jnp.concatenate(log_w, axis=1)) * qk[key]
        iw_x = jnp.concatenate([_col_bcast(iw[key], lo + h) for h in range(H_B)], axis=1)
        lhs[key] = ((rest[d][bi, :, OFF_BQ:OFF_BQ + qw] * iw_x).astype(BF16), dw.astype(BF16))
    num = {}
    for key in dirs:
        i = slot[key]
        parts = []
        for p in range(H_B // PAIR):
            ql, dl, cl = slice(p * PAIR * DK_B, (p + 1) * PAIR * DK_B), slice(p * LANES, (p + 1) * LANES), \
                slice(p * PAIR * CAUG_W, (p + 1) * PAIR * CAUG_W)
            parts.append(jnp.dot(jnp.concatenate([lhs[key][0][:, ql], lhs[key][1][:, dl]], axis=1),
                                 jnp.concatenate([bdc[i, ql, cl], bdv[i, dl, cl]], axis=0),
                                 preferred_element_type=F32))
        num[key] = jnp.concatenate(parts, axis=1)
    upd = {key: jnp.dot(ks_t[key], bdv[slot[key]], preferred_element_type=F32) for key in dirs}
    for key in dirs:
        bi, d = key
        lo = LF_OFF + d * H_B
        dec_x = jnp.concatenate(
            [jnp.broadcast_to(dec_row[key][:, lo + h:lo + h + 1], (1, CAUG_W)) for h in range(H_B)], axis=1)
        c_scr[bi, d] = c_scr[bi, d] * dec_x + upd[key]
    heads = [(key, h) for key in dirs for h in range(H_B)]
    rows = {d: pl.ds(pl.multiple_of((n if d == 0 else n_chunks - 1 - n) * CHUNK, CHUNK), CHUNK) for d in range(N_DIR)}
    tots = []
    for key, h in heads:
        bi, d = key
        vs = slice(h * DV_B, (h + 1) * DV_B)
        den = jnp.maximum(jnp.abs(num[key][:, h * CAUG_W + DV_B:(h + 1) * CAUG_W]),
                          _col_bcast(emt[key], LF_OFF + d * H_B + h))
        hb = jnp.concatenate([num[key][:, h * CAUG_W:h * CAUG_W + LANES] / den,
                              num[key][:, h * CAUG_W + LANES:h * CAUG_W + DV_B] / den], axis=1)
        tot = hb + o_scr[bi, rows[d], vs]
        o_scr[bi, rows[d], vs] = tot
        tots.append(tot)
    inv = [lax.rsqrt(jnp.mean(tot * tot, axis=-1, keepdims=True) + EPS) for tot in tots]
    for (key, h), tot, r in zip(heads, tots, inv):
        bi, d = key
        vs = slice(h * DV_B, (h + 1) * DV_B)
        ogate = rest[d][bi, :, OFF_BO + h * DV_B:OFF_BO + (h + 1) * DV_B]
        yb_ref[bi, rows[d], vs] = (tot * r * norm_ref[...] * ogate).astype(BF16)

    if emit_state:
        @pl.when(n == n_chunks - 1)
        def _():
            for bi, d in dirs:
                for h in range(H_B):
                    cout_ref[bi, d, h] = c_scr[bi, d, :, h * CAUG_W:h * CAUG_W + DV_B]
                    nout_ref[bi, d, h] = c_scr[bi, d, :, h * CAUG_W + DV_B:(h + 1) * CAUG_W]
            mout_ref[...] = m_scr[...]


def _mlstm(rest, gates, norm_b, c0, m0, batch, n_chunks, emit_state):
    t = n_chunks * CHUNK
    has_init = c0 is not None
    bb = SCAN_BB
    rest, gates = (a.reshape(batch, t, a.shape[-1]) for a in (rest, gates))
    mspec = pl.BlockSpec((bb, N_DIR, 8, LANES), lambda b, n: (b, 0, 0, 0))
    in_specs = [*_scan_specs(bb, n_chunks, REST_W), *_scan_specs(bb, n_chunks, LANES),
                pl.BlockSpec((1, DV_B), lambda b, n: (0, 0))]
    args = [rest, rest, gates, gates, norm_b]
    if has_init:
        in_specs += [pl.BlockSpec((bb, N_DIR, DK_B, MW), lambda b, n: (b, 0, 0, 0)), mspec]
        args += [c0, m0]
    out_specs = [pl.BlockSpec((bb, t, B_W), lambda b, n: (b, 0, 0))]
    out_shape = [jax.ShapeDtypeStruct((batch, t, B_W), BF16)]
    if emit_state:
        out_specs += [pl.BlockSpec((bb, N_DIR, H_B, DK_B, DV_B), lambda b, n: (b, 0, 0, 0, 0)),
                      pl.BlockSpec((bb, N_DIR, H_B, DK_B, LANES), lambda b, n: (b, 0, 0, 0, 0)), mspec]
        out_shape += [jax.ShapeDtypeStruct((batch, N_DIR, H_B, DK_B, DV_B), F32),
                      jax.ShapeDtypeStruct((batch, N_DIR, H_B, DK_B, LANES), F32),
                      jax.ShapeDtypeStruct((batch, N_DIR, 8, LANES), F32)]
    out = pl.pallas_call(
        functools.partial(_mlstm_body, n_chunks=n_chunks, bb=bb, has_init=has_init, emit_state=emit_state),
        grid=(batch // bb, n_chunks),
        in_specs=in_specs,
        out_specs=out_specs,
        out_shape=out_shape,
        scratch_shapes=[pltpu.VMEM((bb, N_DIR, DK_B, MW), F32),
                        pltpu.VMEM((bb, N_DIR, 8, LANES), F32),
                        pltpu.VMEM((bb, t, B_W), F32),
                        pltpu.VMEM((bb * N_DIR, H_B * CHUNK, H_B * DK_B), BF16),
                        pltpu.VMEM((bb * N_DIR, H_B * DK_B, MW), BF16),
                        pltpu.VMEM((bb * N_DIR, H_B * CHUNK, MW), BF16)],
        compiler_params=_params(("arbitrary", "arbitrary")),
        name="mlstm_scan",
    )(*args)
    return [out[0].reshape(batch * t, B_W), *out[1:]]


OUT_RB = 256


def _outproj_body(ya_ref, yb_ref, wa_ref, wb_ref, x_ref, mod_ref, post1_ref, pre2_ref, x1_ref, h2_ref):
    for rb in range(x_ref.shape[0] // OUT_RB):
        rs = slice(rb * OUT_RB, (rb + 1) * OUT_RB)
        mix = (jnp.dot(ya_ref[rs, :], wa_ref[...], preferred_element_type=F32)
               + jnp.dot(yb_ref[rs, :], wb_ref[...], preferred_element_type=F32))
        x1 = x_ref[rs, :] + mod_ref[0, 2:3, :] * (_rms(mix) * post1_ref[...])
        x1_ref[rs, :] = x1
        h2 = _rms(x1) * pre2_ref[...] * (1.0 + mod_ref[0, 4:5, :]) + mod_ref[0, 3:4, :]
        h2_ref[rs, :] = h2.astype(BF16)


def _outproj(ya, yb, w_out, x2d, mod3, mod_map, post1, pre2, tm):
    m = x2d.shape[0]
    row = lambda i: (i, 0)
    const = lambda i: (0, 0)
    return pl.pallas_call(
        _outproj_body,
        grid=(m // tm,),
        in_specs=[pl.BlockSpec((tm, A_W), row), pl.BlockSpec((tm, B_W), row),
                  pl.BlockSpec((A_W, D_MODEL), lambda i: (0, 0)),
                  pl.BlockSpec((B_W, D_MODEL), lambda i: (1, 0)),
                  pl.BlockSpec((tm, D_MODEL), row),
                  pl.BlockSpec((1, 6, D_MODEL), lambda i: (mod_map(i * tm), 0, 0)),
                  pl.BlockSpec((1, D_MODEL), const), pl.BlockSpec((1, D_MODEL), const)],
        out_specs=[pl.BlockSpec((tm, D_MODEL), row), pl.BlockSpec((tm, D_MODEL), row)],
        out_shape=[jax.ShapeDtypeStruct((m, D_MODEL), F32), jax.ShapeDtypeStruct((m, D_MODEL), BF16)],
        compiler_params=_params(("parallel",)),
        name="outproj",
    )(ya, yb, w_out, w_out, x2d, mod3, post1, pre2)


def _ffn_body(h2_ref, w1_ref, w2_ref, x1_ref, mod_ref, post2_ref, o_ref):
    kk = pl.program_id(1)
    a = jnp.maximum(jnp.dot(h2_ref[...], w1_ref[...], preferred_element_type=F32), 0.0)
    contrib = jnp.dot((a * a).astype(BF16), w2_ref[...], preferred_element_type=F32)

    @pl.when(kk == 0)
    def _():
        o_ref[...] = contrib

    @pl.when((kk > 0) & (kk < pl.num_programs(1) - 1))
    def _():
        o_ref[...] += contrib

    @pl.when(kk == pl.num_programs(1) - 1)
    def _():
        f = o_ref[...] + contrib
        o_ref[...] = x1_ref[...] + mod_ref[0, 5:6, :] * (_rms(f) * post2_ref[...])


def _ffn(h2, w1, w2, x1, mod3, mod_map, post2, tm, fc):
    m = h2.shape[0]
    return pl.pallas_call(
        _ffn_body,
        grid=(m // tm, FFN // fc),
        in_specs=[pl.BlockSpec((tm, D_MODEL), lambda i, k: (i, 0)),
                  pl.BlockSpec((D_MODEL, fc), lambda i, k: (0, k)),
                  pl.BlockSpec((fc, D_MODEL), lambda i, k: (k, 0)),
                  pl.BlockSpec((tm, D_MODEL), lambda i, k: (i, 0)),
                  pl.BlockSpec((1, 6, D_MODEL), lambda i, k: (mod_map(i * tm), 0, 0)),
                  pl.BlockSpec((1, D_MODEL), lambda i, k: (0, 0))],
        out_specs=pl.BlockSpec((tm, D_MODEL), lambda i, k: (i, 0)),
        out_shape=jax.ShapeDtypeStruct((m, D_MODEL), F32),
        compiler_params=_params(("parallel", "arbitrary")),
        name="ffn",
    )(h2, w1, w2, x1, mod3, post2)


def _block(x, mod3, mod_of_row, lp, init, seq_len, emit_state):
    bsz, t, _ = x.shape
    x2d = x.reshape(bsz * t, D_MODEL)
    n_chunks = t // CHUNK
    qkv, rest, gates = _proj(x2d, mod3, lambda i: mod_of_row(i * PROJ_TM), lp["pre1"], lp["w16"],
                             lp["w_mid"], lp["w_gate"], lp["conv_w"], lp["gate_p"], seq_len)
    s0, c0, m0 = init if init is not None else (None, None, None)
    d_out = _delta(qkv, rest, gates, lp["norm_a"], s0, bsz, n_chunks, emit_state)
    m_out = _mlstm(rest, gates, lp["norm_b"], c0, m0, bsz, n_chunks, emit_state)
    x1, h2 = _outproj(d_out[0], m_out[0], lp["w_out"], x2d, mod3, mod_of_row, lp["post1"], lp["pre2"], 512)
    y = _ffn(h2, lp["w1"], lp["w2"], x1, mod3, mod_of_row, lp["post2"], 512, 1024)
    states = None
    if emit_state:
        m_fin = jnp.stack([m_out[3][:, d, 0, LF_OFF + d * H_B:LF_OFF + (d + 1) * H_B] for d in range(N_DIR)], axis=1)
        states = (d_out[1], m_out[1], m_out[2][..., 0], m_fin)
    return y.reshape(bsz, t, D_MODEL), states


def _layer_params(l, norm_mix_pre, norm_mix_post, norm_ffn_pre, norm_ffn_post, w_in, conv_w, a_log, dt_bias,
                  norm_a, mlstm_ibias, mlstm_fbias, norm_b, w_out, w_ffn1, w_ffn2):
    w = w_in[l]
    o_ag = QKV_W
    o_aa = o_ag + A_W
    o_ab = o_aa + N_DIR * H_A
    o_bq = o_ab + N_DIR * H_A
    o_bi = o_bq + 2 * H_B * DK_B + 2 * B_W
    o_bf = o_bi + N_DIR * H_B
    n_gate = 2 * N_DIR * H_A + 2 * N_DIR * H_B
    w16 = w.astype(BF16)
    w_gate = jnp.concatenate([w16[:, o_aa:o_bq], w16[:, o_bi:o_bf + N_DIR * H_B],
                              jnp.zeros((D_MODEL, LANES - n_gate), BF16)], axis=1)

    def lane_row(vals, off):
        return jnp.zeros((LANES,), F32).at[off:off + vals.size].set(vals.reshape(-1))

    gate_p = jnp.stack([lane_row(a_log[l], G_OFF), lane_row(dt_bias[l], G_OFF),
                        lane_row(mlstm_ibias[l], LI_OFF) + lane_row(mlstm_fbias[l], LF_OFF)]
                       + [jnp.zeros((LANES,), F32)] * 5)
    row = lambda v: v[l].reshape(1, -1)
    return dict(
        pre1=row(norm_mix_pre), post1=row(norm_mix_post), pre2=row(norm_ffn_pre), post2=row(norm_ffn_post),
        w16=w16, w_mid=w16[:, o_bq:o_bi], w_gate=w_gate, gate_p=gate_p,
        conv_w=jnp.concatenate([conv_w[l].T, jnp.zeros((5, QKV_W), F32)], axis=0),
        norm_a=row(norm_a), norm_b=row(norm_b),
        w_out=w_out[l].astype(BF16), w1=w_ffn1[l].astype(BF16), w2=w_ffn2[l].astype(BF16))


def kernel(x_prompt, x_sample, state_delta, state_mlstm_C, state_mlstm_n, state_mlstm_m, c, c_ctx, w_ada, b_ada, norm_mix_pre, norm_mix_post, norm_ffn_pre, norm_ffn_post, w_in, conv_w, a_log, dt_bias, norm_a, mlstm_ibias, mlstm_fbias, norm_b, w_out, w_ffn1, w_ffn2):
    depth = w_in.shape[0]
    n_lat = x_sample.shape[0]
    t_lat = x_sample.shape[1]
    cond = jnp.concatenate([c_ctx[None, :], c, jnp.zeros((8 - 1 - n_lat, D_MODEL), F32)], axis=0)
    y_prompt, y_sample = x_prompt, x_sample
    acc = ([], [], [], [])
    for l in range(depth):
        lp = _layer_params(l, norm_mix_pre, norm_mix_post, norm_ffn_pre, norm_ffn_post, w_in, conv_w, a_log,
                           dt_bias, norm_a, mlstm_ibias, mlstm_fbias, norm_b, w_out, w_ffn1, w_ffn2)
        mod = _ada(cond, w_ada, b_ada[l].reshape(1, -1), l)
        mod3 = mod[:1 + n_lat].reshape(1 + n_lat, 6, D_MODEL)
        y_prompt, st = _block(y_prompt, mod3, lambda r: 0, lp, None, x_prompt.shape[1], True)
        for a, s in zip(acc, st):
            a.append(s)
        n_rep = jnp.broadcast_to(state_mlstm_n[:, l][..., None], state_mlstm_n[:, l].shape + (LANES,))
        c_aug0 = jnp.concatenate([state_mlstm_C[:, l], n_rep], axis=-1)
        c_aug0 = c_aug0.transpose(0, 1, 3, 2, 4).reshape(n_lat, N_DIR, DK_B, MW)
        m0 = jnp.zeros((n_lat, N_DIR, LANES), F32)
        for d in range(N_DIR):
            m0 = m0.at[:, d, LF_OFF + d * H_B:LF_OFF + (d + 1) * H_B].set(state_mlstm_m[:, l, d])
        m0 = jnp.broadcast_to(m0[:, :, None, :], (n_lat, N_DIR, 8, LANES))
        y_sample, _ = _block(y_sample, mod3, lambda r: 1 + r // t_lat, lp, (state_delta[:, l], c_aug0, m0),
                             GRID_W, False)
    return (y_prompt, y_sample) + tuple(jnp.stack(a, axis=1) for a in acc)
```

```python
import functools

import jax
import jax.numpy as jnp
from jax import lax
from jax.experimental import pallas as pl
from jax.experimental.pallas import tpu as pltpu

F32 = jnp.float32
BF16 = jnp.bfloat16

D_MODEL = 2048
N_DIR = 2
A_W = D_MODEL // 2
B_W = D_MODEL - A_W
DK_A = 128
DV_A = 128
H_A = A_W // DV_A
DV_B = 256
DK_B = DV_B // 2
H_B = B_W // DV_B
GRID_W = 64
CHUNK = 64
FFN = 4 * D_MODEL
EPS = 1e-6
LANES = 128
NEG = -1e30

QKV_W = 3 * A_W
REST_W = A_W + 2 * H_B * DK_B + 2 * B_W
OFF_AG, OFF_BQ, OFF_BK, OFF_BV, OFF_BO = 0, A_W, A_W + H_B * DK_B, A_W + 2 * H_B * DK_B, A_W + 2 * H_B * DK_B + B_W
G_OFF, BETA_OFF = 0, N_DIR * H_A
LI_OFF = 2 * N_DIR * H_A
LF_OFF = LI_OFF + N_DIR * H_B

VMEM_LIMIT = 56 * 1024 * 1024


def _sigmoid(x):
    return 1.0 / (1.0 + jnp.exp(-x))


def _softplus(x):
    return jnp.maximum(x, 0.0) + jnp.log1p(jnp.exp(-jnp.abs(x)))


def _dot(a, b):
    return jnp.dot(a.astype(BF16), b.astype(BF16), preferred_element_type=F32)


def _dot_f32(a, b):
    return jnp.dot(a, b, precision=lax.Precision.HIGHEST, preferred_element_type=F32)


def _rms(x):
    return x * lax.rsqrt(jnp.mean(x * x, axis=-1, keepdims=True) + EPS)


def _params(sem):
    return pltpu.CompilerParams(dimension_semantics=sem, vmem_limit_bytes=VMEM_LIMIT)


def _ada_body(c_ref, w_ref, b_ref, o_ref):
    c = c_ref[...]
    o_ref[...] = _dot(c * _sigmoid(c), w_ref[...]) + b_ref[...]


def _ada(c_all, w_ada, b, l):
    n = w_ada.shape[2]
    tn = 1024
    return pl.pallas_call(
        _ada_body,
        grid=(n // tn,),
        in_specs=[pl.BlockSpec(c_all.shape, lambda j: (0, 0)),
                  pl.BlockSpec((None, D_MODEL, tn), lambda j: (l, 0, j)),
                  pl.BlockSpec((1, tn), lambda j: (0, j))],
        out_specs=pl.BlockSpec((c_all.shape[0], tn), lambda j: (0, j)),
        out_shape=jax.ShapeDtypeStruct((c_all.shape[0], n), F32),
        compiler_params=_params(("arbitrary",)),
        name="ada",
    )(c_all, w_ada, b)


PROJ_TM = 1024
PROJ_RB = 512
PROJ_CB = 256
PROJ_TN = 512
N_QKV_T = QKV_W // PROJ_TN
N_REST_T = REST_W // PROJ_TN
N_HEAD_T = (QKV_W + A_W) // PROJ_TN
N_MID_T = N_QKV_T + N_REST_T - N_HEAD_T


def _proj_body(x_ref, mod_ref, g_ref, wh_ref, wb_ref, wg_ref, cw_ref, gp_ref, qkv_ref, rest_ref, gate_ref, h_scr, *,
               seq_len):
    j = pl.program_id(1)
    row_blocks = [slice(rb * PROJ_RB, (rb + 1) * PROJ_RB) for rb in range(PROJ_TM // PROJ_RB)]
    col_blocks = [slice(cb * PROJ_CB, (cb + 1) * PROJ_CB) for cb in range(PROJ_TN // PROJ_CB)]

    def run_rows(rs, w_ref, epilogue, out_ref):
        for cs in col_blocks:
            epilogue(jnp.dot(h_scr[rs, :], w_ref[:, cs], preferred_element_type=F32), out_ref, rs, cs)

    def run(w_ref, epilogue, out_ref):
        for rs in row_blocks:
            run_rows(rs, w_ref, epilogue, out_ref)

    def run_first():
        for rs in row_blocks:
            y = _rms(x_ref[rs, :]) * g_ref[...]
            h_scr[rs, :] = (y * (1.0 + mod_ref[0, 1:2, :]) + mod_ref[0, 0:1, :]).astype(BF16)
            run_rows(rs, wh_ref, ep_l2(DK_A ** -0.5), qkv_ref)
            ep_gate(jnp.dot(h_scr[rs, :], wg_ref[...], preferred_element_type=F32), gate_ref, rs)

    def conv_silu(acc, cs, g):
        a = acc[:, g * LANES:(g + 1) * LANES]
        ls = slice(cs.start + g * LANES, cs.start + (g + 1) * LANES)
        pos = lax.broadcasted_iota(jnp.int32, a.shape, 0) & (seq_len - 1)
        prev = jnp.where(pos == 0, 0.0, pltpu.roll(a, 1, 0))
        nxt = jnp.where(pos == seq_len - 1, 0.0, pltpu.roll(a, a.shape[0] - 1, 0))
        y = prev * cw_ref[0:1, ls] + a * cw_ref[1:2, ls] + nxt * cw_ref[2:3, ls]
        return ls, y * _sigmoid(y)

    def ep_l2(scale):
        def f(acc, out_ref, rs, cs):
            for g in range(PROJ_CB // LANES):
                ls, blk = conv_silu(acc, cs, g)
                inv = lax.rsqrt(jnp.sum(blk * blk, axis=-1, keepdims=True) + EPS)
                out_ref[rs, ls] = blk * (inv * scale)
        return f

    def ep_conv(acc, out_ref, rs, cs):
        for g in range(PROJ_CB // LANES):
            ls, blk = conv_silu(acc, cs, g)
            out_ref[rs, ls] = blk

    def ep_map(fn):
        def f(acc, out_ref, rs, cs):
            out_ref[rs, cs] = fn(acc)
        return f

    def ep_gate(z, out_ref, rs):
        lane = lax.broadcasted_iota(jnp.int32, z.shape, 1)
        g = -jnp.exp(gp_ref[0:1, :]) * _softplus(z + gp_ref[1:2, :])
        li = z + gp_ref[2:3, :]
        out_ref[rs, :] = jnp.where(lane < BETA_OFF, g,
                                   jnp.where(lane < LI_OFF, _sigmoid(z),
                                             jnp.where(lane < LF_OFF, li,
                                                       jnp.where(lane < LF_OFF + N_DIR * H_B, -_softplus(-li), 0.0))))

    tp = A_W // PROJ_TN
    c0 = (j - N_QKV_T) * PROJ_TN
    in_rest = (j >= N_QKV_T) & (j < N_QKV_T + N_REST_T)
    pl.when(j == 0)(run_first)
    variants = [
        ((j > 0) & (j < tp), wh_ref, ep_l2(DK_A ** -0.5), qkv_ref),
        ((j >= tp) & (j < 2 * tp), wh_ref, ep_l2(1.0), qkv_ref),
        ((j >= 2 * tp) & (j < N_QKV_T), wh_ref, ep_conv, qkv_ref),
        (in_rest & (c0 < OFF_BQ), wh_ref, ep_map(lambda a: a * _sigmoid(a)), rest_ref),
        (in_rest & (c0 >= OFF_BQ) & (c0 < OFF_BK), wb_ref, ep_map(lambda a: a * (DK_B ** -0.5)), rest_ref),
        (in_rest & (c0 >= OFF_BK) & (c0 < OFF_BO), wb_ref, ep_map(lambda a: a), rest_ref),
        (in_rest & (c0 >= OFF_BO), wb_ref, ep_map(_sigmoid), rest_ref),
    ]
    for cond, w_ref, epilogue, out_ref in variants:
        pl.when(cond)(functools.partial(run, w_ref, epilogue, out_ref))


def _proj(x2d, mod3, mod_map, pre_g, w16, w_mid, w_gate, conv_w, gate_p, seq_len):
    m = x2d.shape[0]
    tm, tn = PROJ_TM, PROJ_TN
    assert PROJ_RB % seq_len == 0 and m % tm == 0
    return pl.pallas_call(
        functools.partial(_proj_body, seq_len=seq_len),
        grid=(m // tm, N_HEAD_T + N_MID_T),
        in_specs=[pl.BlockSpec((tm, D_MODEL), lambda i, j: (i, 0)),
                  pl.BlockSpec((1, 6, D_MODEL), lambda i, j: (mod_map(i), 0, 0)),
                  pl.BlockSpec((1, D_MODEL), lambda i, j: (0, 0)),
                  pl.BlockSpec((D_MODEL, tn), lambda i, j: (0, jnp.minimum(j, N_HEAD_T - 1))),
                  pl.BlockSpec((D_MODEL, tn), lambda i, j: (0, jnp.clip(j - N_HEAD_T, 0, N_MID_T - 1))),
                  pl.BlockSpec((D_MODEL, LANES), lambda i, j: (0, 0)),
                  pl.BlockSpec((8, tn), lambda i, j: (0, jnp.minimum(j, N_QKV_T - 1))),
                  pl.BlockSpec((8, LANES), lambda i, j: (0, 0))],
        out_specs=[pl.BlockSpec((tm, tn), lambda i, j: (i, jnp.minimum(j, N_QKV_T - 1))),
                   pl.BlockSpec((tm, tn), lambda i, j: (i, jnp.clip(j - N_QKV_T, 0, N_REST_T - 1))),
                   pl.BlockSpec((tm, LANES), lambda i, j: (i, 0))],
        out_shape=[jax.ShapeDtypeStruct((m, QKV_W), F32), jax.ShapeDtypeStruct((m, REST_W), F32),
                   jax.ShapeDtypeStruct((m, LANES), F32)],
        scratch_shapes=[pltpu.VMEM((tm, D_MODEL), BF16)],
        compiler_params=_params(("parallel", "arbitrary")),
        name="proj",
    )(x2d, mod3, pre_g, w16, w_mid, w_gate, conv_w, gate_p)


def _masks(d):
    row = lax.broadcasted_iota(jnp.int32, (CHUNK, CHUNK), 0)
    col = lax.broadcasted_iota(jnp.int32, (CHUNK, CHUNK), 1)
    if d == 0:
        return row >= col, row > col, row == col
    return row <= col, row < col, row == col


GROUP = 4
N_GROUPS = H_A // GROUP
PAIR = LANES // CHUNK


def _split(x):
    hi = x.astype(BF16)
    return hi, (x - hi.astype(F32)).astype(BF16)


def _store_blocks(ref, x, blk_r, blk_c, col0=0):
    for h in range(GROUP):
        ref[h * blk_r:(h + 1) * blk_r, col0 + h * blk_c:col0 + (h + 1) * blk_c] = x[:, h * blk_c:(h + 1) * blk_c]


def _dot3(a_hi, a_lo, b_hi, b_lo):
    m = a_hi.shape[0]
    r = jnp.dot(jnp.concatenate([a_hi, a_lo], axis=0), b_hi, preferred_element_type=F32)
    return r[:m] + r[m:] + jnp.dot(a_hi, b_lo, preferred_element_type=F32)


HALF = CHUNK // 2


def _tri_inverse_wide(lmats, upper, bd_hi, bd_lo, by_hi, by_lo):
    width = lmats[0].shape[1]
    n_blk = width // HALF
    row = lax.broadcasted_iota(jnp.int32, (HALF, width), 0)
    lane = lax.broadcasted_iota(jnp.int32, (HALF, width), 1)
    lead = (lane & (CHUNK - 1)) < HALF
    eye_d = (row == (lane & (HALF - 1))).astype(F32)

    def bd_dot(i, a, b_hi, b_lo):
        for blk in range(n_blk):
            sl = slice(blk * HALF, (blk + 1) * HALF)
            bd_hi[i, sl, sl] = b_hi[:, sl]
            bd_lo[i, sl, sl] = b_lo[:, sl]
        return _dot3(*a, bd_hi[i], bd_lo[i])

    idx = range(len(lmats))
    diag = [jnp.where(lead, l[:HALF], l[HALF:]) for l in lmats]
    s = [eye_d - dg for dg in diag]
    p = []
    for i in idx:
        m_hi, m_lo = _split(-diag[i])
        p.append(bd_dot(i, (m_hi, m_lo), m_hi, m_lo))
    for _ in range(3):
        for i in idx:
            p_hi, p_lo = _split(p[i])
            s_hi, s_lo = _split(s[i])
            r = bd_dot(i, (jnp.concatenate([p_hi, s_hi], axis=0), jnp.concatenate([p_lo, s_lo], axis=0)), p_hi, p_lo)
            p[i] = r[:HALF]
            s[i] = s[i] + r[HALF:]
    for i in idx:
        p_hi, p_lo = _split(p[i])
        s[i] = s[i] + bd_dot(i, _split(s[i]), p_hi, p_lo)
    y = []
    for i in idx:
        c_blk = jnp.where(lead, 0.0, lmats[i][:HALF]) if upper[i] else jnp.where(lead, lmats[i][HALF:], 0.0)
        y.append(bd_dot(i, _split(c_blk), *_split(s[i])))
    out = []
    for i in idx:
        y_hi, y_lo = _split(y[i])
        for h in range(width // CHUNK):
            a_sl, b_sl = slice(h * CHUNK, h * CHUNK + HALF), slice(h * CHUNK + HALF, (h + 1) * CHUNK)
            rs, cs = (a_sl, b_sl) if upper[i] else (b_sl, a_sl)
            by_hi[i, rs, cs] = y_hi[:, cs]
            by_lo[i, rs, cs] = y_lo[:, cs]
        ai, bi = jnp.where(lead, s[i], 0.0), jnp.where(lead, 0.0, s[i])
        x = _dot3(*_split(ai if upper[i] else bi), by_hi[i], by_lo[i])
        out.append(jnp.concatenate([ai - x, bi] if upper[i] else [ai, bi - x], axis=0))
    return out


def _col_bcast(tile, c, width=LANES):
    return jnp.broadcast_to(tile[:, c:c + 1], (tile.shape[0], width))


def _delta_body(*refs, n_chunks, bb, has_init, emit_state):
    qkv = refs[0:2]
    ag = refs[2:4]
    gt = refs[4:6]
    norm_ref = refs[6]
    pos = 7
    s0_ref = None
    if has_init:
        s0_ref = refs[pos]
        pos += 1
    ya_ref = refs[pos]
    pos += 1
    sout_ref = None
    if emit_state:
        sout_ref = refs[pos]
        pos += 1
    s_scr, o_scr, bdn_hi, bdn_lo, bdy_hi, bdy_lo, bdk, bduw, bds, bdv = refs[pos:pos + 10]

    n = pl.program_id(1)
    gw = GROUP * DK_A

    @pl.when((n == 0) & (pl.program_id(0) == 0))
    def _():
        for ref in (bdn_hi, bdn_lo, bdy_hi, bdy_lo, bdk, bduw, bds, bdv):
            ref[...] = jnp.zeros_like(ref)

    @pl.when(n == 0)
    def _():
        o_scr[...] = jnp.zeros_like(o_scr)
        for bi in range(bb):
            for d in range(N_DIR):
                for h in range(H_A):
                    blk = s0_ref[bi, d, h] if has_init else jnp.zeros((DK_A, DV_A), F32)
                    s_scr[bi, d, h // GROUP, :, (h % GROUP) * DV_A:(h % GROUP + 1) * DV_A] = blk

    row = lax.broadcasted_iota(jnp.int32, (CHUNK, LANES), 0)
    lane = lax.broadcasted_iota(jnp.int32, (CHUNK, LANES), 1)
    col = lane & (CHUNK - 1)
    left = lane < CHUNK
    probs = [(bi, d, g) for bi in range(bb) for d in range(N_DIR) for g in range(N_GROUPS)]
    masks = {0: (row >= col, row > col), 1: (row <= col, row < col)}
    gates, gc, gc_t, eg, egl = {}, {}, {}, {}, {}
    for bi in range(bb):
        for d in range(N_DIR):
            key = (bi, d)
            gates[key] = gt[d][bi]
            gc[key] = _dot_f32(_masks(d)[0].astype(F32), gates[key])
            gc_t[key] = jnp.concatenate([gc[key], gc[key]], axis=0).T
            last = CHUNK - 1 if d == 0 else 0
            eg[key] = jnp.exp(gc[key])
            egl[key] = jnp.exp(gc[key][last:last + 1, :] - gc[key])

    def cols_of(d, g):
        return [G_OFF + d * H_A + g * GROUP + hl for hl in range(GROUP)]

    q16, ks, beta_xs, decays, grams = [], [], [], [], []
    for gi, (bi, d, g) in enumerate(probs):
        key = (bi, d)
        cols = cols_of(d, g)
        incl, _ = masks[d]
        k = qkv[d][bi, :, A_W + g * gw:A_W + (g + 1) * gw]
        beta_x = jnp.concatenate([_col_bcast(gates[key], BETA_OFF - G_OFF + c) for c in cols], axis=1)
        decay = []
        for p in range(GROUP // PAIR):
            c0, c1 = cols[PAIR * p], cols[PAIR * p + 1]
            gcol = jnp.where(left, _col_bcast(gc[key], c0), _col_bcast(gc[key], c1))
            grow = jnp.where(left[0:1], gc_t[key][c0:c0 + 1, :], gc_t[key][c1:c1 + 1, :])
            decay.append(jnp.exp(jnp.where(incl, gcol - grow, NEG)))
        decays.append(jnp.concatenate(decay, axis=1))
        q16.append(qkv[d][bi, :, g * gw:(g + 1) * gw].astype(BF16))
        ks.append(k)
        beta_xs.append(beta_x)
        _store_blocks(bdk.at[gi], k.astype(BF16), CHUNK, DK_A)
    for gi in range(len(probs)):
        grams.append(lax.dot_general(jnp.concatenate([q16[gi], (ks[gi] * beta_xs[gi]).astype(BF16)], axis=0),
                                     bdk[gi], (((1,), (1,)), ((), ())), preferred_element_type=F32))
    attns, lmats = [], []
    for gi, (bi, d, g) in enumerate(probs):
        strict_w = jnp.concatenate([masks[d][1]] * (GROUP // PAIR), axis=1)
        attns.append((grams[gi][:CHUNK] * decays[gi]).astype(BF16))
        lmats.append(jnp.where(strict_w, grams[gi][CHUNK:] * decays[gi], 0.0))
    ainvs = _tri_inverse_wide(lmats, [d == 1 for _, d, _ in probs], bdn_hi, bdn_lo, bdy_hi, bdy_lo)
    eg_xs, kd_ts = [], []
    for gi, (bi, d, g) in enumerate(probs):
        key = (bi, d)
        cols = cols_of(d, g)
        v = qkv[d][bi, :, 2 * A_W + g * gw:2 * A_W + (g + 1) * gw]
        eg_x = jnp.concatenate([_col_bcast(eg[key], c) for c in cols], axis=1)
        egl_x = jnp.concatenate([_col_bcast(egl[key], c) for c in cols], axis=1)
        eg_xs.append(eg_x)
        _store_blocks(bduw.at[gi], (v * beta_xs[gi]).astype(BF16), CHUNK, DV_A)
        _store_blocks(bduw.at[gi], (ks[gi] * (beta_xs[gi] * eg_x)).astype(BF16), CHUNK, DK_A, col0=gw)
        kd = ks[gi] * egl_x
        kd_ts.append(jnp.concatenate([kd[:, hl * DK_A:(hl + 1) * DK_A] for hl in range(GROUP)], axis=0)
                     .T.astype(BF16))
        _store_blocks(bds.at[gi], s_scr[bi, d, g].astype(BF16), DK_A, DV_A)
    uws = []
    for gi in range(len(probs)):
        t_hi, t_lo = _split(ainvs[gi])
        r = jnp.dot(jnp.concatenate([t_hi, t_lo], axis=0), bduw[gi], preferred_element_type=F32)
        uws.append(r[:CHUNK] + r[CHUNK:])
    ws_qs = []
    hw = gw // 2
    for gi in range(len(probs)):
        wq = jnp.concatenate([uws[gi][:, gw:].astype(BF16), q16[gi]], axis=0)
        ws_qs.append(jnp.concatenate(
            [jnp.dot(wq[:, p * hw:(p + 1) * hw], bds[gi, p * hw:(p + 1) * hw, p * hw:(p + 1) * hw],
                     preferred_element_type=F32) for p in range(2)], axis=1))
    for gi in range(len(probs)):
        _store_blocks(bdv.at[gi], (uws[gi][:, :gw] - ws_qs[gi][:CHUNK]).astype(BF16), CHUNK, DV_A)
    rs = []
    for gi in range(len(probs)):
        rs.append(jnp.dot(jnp.concatenate([attns[gi], kd_ts[gi]], axis=0), bdv[gi], preferred_element_type=F32))
    tots = []
    for gi, (bi, d, g) in enumerate(probs):
        cols = cols_of(d, g)
        last = CHUNK - 1 if d == 0 else 0
        o = ws_qs[gi][CHUNK:] * eg_xs[gi] + rs[gi][:CHUNK]
        eg_last = jnp.concatenate(
            [jnp.broadcast_to(eg[bi, d][last:last + 1, c:c + 1], (1, DV_A)) for c in cols], axis=1)
        s_scr[bi, d, g] = s_scr[bi, d, g] * eg_last + rs[gi][CHUNK:]
        cidx = n if d == 0 else n_chunks - 1 - n
        rows = pl.ds(pl.multiple_of(cidx * CHUNK, CHUNK), CHUNK)
        tot = o + o_scr[bi, rows, g * gw:(g + 1) * gw]
        o_scr[bi, rows, g * gw:(g + 1) * gw] = tot
        tots.append((rows, tot))
    blks = [(bi, d, rows, g * GROUP + hl, tot[:, hl * DV_A:(hl + 1) * DV_A])
            for (bi, d, g), (rows, tot) in zip(probs, tots) for hl in range(GROUP)]
    inv = [lax.rsqrt(jnp.mean(blk * blk, axis=-1, keepdims=True) + EPS) for *_, blk in blks]
    for (bi, d, rows, h, blk), r in zip(blks, inv):
        hs = slice(h * DV_A, (h + 1) * DV_A)
        ya_ref[bi, rows, hs] = (blk * r * norm_ref[...] * ag[d][bi, :, hs]).astype(BF16)

    if emit_state:
        @pl.when(n == n_chunks - 1)
        def _():
            for bi in range(bb):
                for d in range(N_DIR):
                    for h in range(H_A):
                        sout_ref[bi, d, h] = s_scr[bi, d, h // GROUP, :, (h % GROUP) * DV_A:(h % GROUP + 1) * DV_A]


SCAN_BB = 2


def _scan_specs(bb, n_chunks, width):
    return (pl.BlockSpec((bb, CHUNK, width), lambda b, n: (b, n, 0)),
            pl.BlockSpec((bb, CHUNK, width), lambda b, n: (b, n_chunks - 1 - n, 0)))


def _delta(qkv, rest, gates, norm_a, s0, batch, n_chunks, emit_state):
    t = n_chunks * CHUNK
    has_init = s0 is not None
    bb = SCAN_BB
    ng = bb * N_DIR * N_GROUPS
    qkv, rest, gates = (a.reshape(batch, t, a.shape[-1]) for a in (qkv, rest, gates))
    in_specs = [*_scan_specs(bb, n_chunks, QKV_W), *_scan_specs(bb, n_chunks, A_W), *_scan_specs(bb, n_chunks, LANES),
                pl.BlockSpec((1, DV_A), lambda b, n: (0, 0))]
    args = [qkv, qkv, rest, rest, gates, gates, norm_a]
    sspec = pl.BlockSpec((bb, N_DIR, H_A, DK_A, DV_A), lambda b, n: (b, 0, 0, 0, 0))
    if has_init:
        in_specs.append(sspec)
        args.append(s0)
    out_specs = [pl.BlockSpec((bb, t, A_W), lambda b, n: (b, 0, 0))]
    out_shape = [jax.ShapeDtypeStruct((batch, t, A_W), BF16)]
    if emit_state:
        out_specs.append(sspec)
        out_shape.append(jax.ShapeDtypeStruct((batch, N_DIR, H_A, DK_A, DV_A), F32))
    out = pl.pallas_call(
        functools.partial(_delta_body, n_chunks=n_chunks, bb=bb, has_init=has_init, emit_state=emit_state),
        grid=(batch // bb, n_chunks),
        in_specs=in_specs,
        out_specs=out_specs,
        out_shape=out_shape,
        scratch_shapes=[pltpu.VMEM((bb, N_DIR, N_GROUPS, DK_A, GROUP * DV_A), F32), pltpu.VMEM((bb, t, A_W), F32),
                        pltpu.VMEM((ng, GROUP * CHUNK, GROUP * CHUNK), BF16),
                        pltpu.VMEM((ng, GROUP * CHUNK, GROUP * CHUNK), BF16),
                        pltpu.VMEM((ng, GROUP * CHUNK, GROUP * CHUNK), BF16),
                        pltpu.VMEM((ng, GROUP * CHUNK, GROUP * CHUNK), BF16),
                        pltpu.VMEM((ng, GROUP * CHUNK, GROUP * DK_A), BF16),
                        pltpu.VMEM((ng, GROUP * CHUNK, 2 * GROUP * DK_A), BF16),
                        pltpu.VMEM((ng, GROUP * DK_A, GROUP * DV_A), BF16),
                        pltpu.VMEM((ng, GROUP * CHUNK, GROUP * DV_A), BF16)],
        compiler_params=_params(("arbitrary", "arbitrary")),
        name="delta_scan",
    )(*args)
    return [out[0].reshape(batch * t, A_W), *out[1:]]


CAUG_W = DV_B + LANES
MW = H_B * CAUG_W


def _scan_max(x, d):
    row = lax.broadcasted_iota(jnp.int32, x.shape, 0)
    s = 1
    while s < CHUNK:
        if d == 0:
            shifted = jnp.where(row >= s, pltpu.roll(x, s, 0), NEG)
        else:
            shifted = jnp.where(row < CHUNK - s, pltpu.roll(x, CHUNK - s, 0), NEG)
        x = jnp.maximum(x, shifted)
        s *= 2
    return x


def _mlstm_body(*refs, n_chunks, bb, has_init, emit_state):
    rest = refs[0:2]
    gt = refs[2:4]
    norm_ref = refs[4]
    pos = 5
    c0_ref = m0_ref = None
    if has_init:
        c0_ref, m0_ref = refs[pos], refs[pos + 1]
        pos += 2
    yb_ref = refs[pos]
    pos += 1
    cout_ref = nout_ref = mout_ref = None
    if emit_state:
        cout_ref, nout_ref, mout_ref = refs[pos:pos + 3]
        pos += 3
    c_scr, m_scr, o_scr, bdk, bdc, bdv = refs[pos:pos + 6]

    n = pl.program_id(1)
    qw = H_B * DK_B

    @pl.when((n == 0) & (pl.program_id(0) == 0))
    def _():
        for ref in (bdk, bdc, bdv):
            ref[...] = jnp.zeros_like(ref)
        for i in range(bb * N_DIR):
            for h in range(H_B):
                bdv[i, h * CHUNK:(h + 1) * CHUNK, h * CAUG_W + DV_B:(h + 1) * CAUG_W] = jnp.ones((CHUNK, LANES), BF16)

    @pl.when(n == 0)
    def _():
        o_scr[...] = jnp.zeros_like(o_scr)
        if has_init:
            c_scr[...] = c0_ref[...]
            m_scr[...] = m0_ref[...]
        else:
            c_scr[...] = jnp.zeros_like(c_scr)
            m_scr[...] = jnp.zeros_like(m_scr)

    row = lax.broadcasted_iota(jnp.int32, (CHUNK, LANES), 0)
    lane = lax.broadcasted_iota(jnp.int32, (CHUNK, LANES), 1)
    col = lane & (CHUNK - 1)
    left = lane < CHUNK
    dirs = [(bi, d) for bi in range(bb) for d in range(N_DIR)]
    slot = {key: i for i, key in enumerate(dirs)}
    q16 = {}
    for key in dirs:
        bi, d = key
        i = slot[key]
        q16[key] = rest[d][bi, :, OFF_BQ:OFF_BQ + qw].astype(BF16)
        _store_blocks(bdk.at[i], rest[d][bi, :, OFF_BK:OFF_BK + qw].astype(BF16), CHUNK, DK_B)
        for h in range(H_B):
            bdv[i, h * CHUNK:(h + 1) * CHUNK, h * CAUG_W:h * CAUG_W + DV_B] = (
                rest[d][bi, :, OFF_BV + h * DV_B:OFF_BV + (h + 1) * DV_B].astype(BF16))
        _store_blocks(bdc.at[i], c_scr[bi, d].astype(BF16), DK_B, CAUG_W)
    qk = {key: lax.dot_general(q16[key], bdk[slot[key]], (((1,), (1,)), ((), ())), preferred_element_type=F32)
          for key in dirs}
    nc, a_t, iw, emt, ksc, dec_row = {}, {}, {}, {}, {}, {}
    for key in dirs:
        bi, d = key
        lo = LF_OFF + d * H_B
        mine = (lane >= lo) & (lane < lo + H_B)
        g = gt[d][bi]
        gc = jnp.where(mine, _dot_f32(_masks(d)[0].astype(F32), g), 0.0)
        a = jnp.where(mine, pltpu.roll(g, LF_OFF - LI_OFF, 1), 0.0) - gc
        last = CHUNK - 1 if d == 0 else 0
        m_old = m_scr[bi, d][0:1, :]
        mx = jnp.maximum(m_old, _scan_max(a, d))
        mxl = mx[last:last + 1, :]
        nc[key] = -mx
        a_t[key] = jnp.concatenate([a, a], axis=0).T
        iw[key] = jnp.exp(m_old - mx)
        emt[key] = jnp.exp(-(gc + mx))
        ksc[key] = jnp.exp(a - mxl)
        dec_row[key] = jnp.exp(m_old - mxl)
        m_scr[bi, d] = jnp.broadcast_to(gc[last:last + 1, :] + mxl, (8, LANES))

    ks_t = {}
    for key in dirs:
        bi, d = key
        lo = LF_OFF + d * H_B
        ks = rest[d][bi, :, OFF_BK:OFF_BK + qw] * jnp.concatenate(
            [_col_bcast(ksc[key], lo + h) for h in range(H_B)], axis=1)
        ks_t[key] = jnp.concatenate([ks[:, h * DK_B:(h + 1) * DK_B] for h in range(H_B)], axis=0).T.astype(BF16)
    lhs = {}
    for key in dirs:
        bi, d = key
        lo = LF_OFF + d * H_B
        incl = row >= col if d == 0 else row <= col
        log_w = []
        for p in range(H_B // PAIR):
            l0, l1 = lo + PAIR * p, lo + PAIR * p + 1
            ccol = jnp.where(left, _col_bcast(nc[key], l0), _col_bcast(nc[key], l1))
            crow = jnp.where(left[0:1], a_t[key][l0:l0 + 1, :], a_t[key][l1:l1 + 1, :])
            log_w.append(jnp.where(incl, ccol + crow, NEG))
        dw = jnp.exp(jnp.concatenate(log_w, axis=1)) * qk[key]
        iw_x = jnp.concatenate([_col_bcast(iw[key], lo + h) for h in range(H_B)], axis=1)
        lhs[key] = ((rest[d][bi, :, OFF_BQ:OFF_BQ + qw] * iw_x).astype(BF16), dw.astype(BF16))
    num = {}
    for key in dirs:
        i = slot[key]
        parts = []
        for p in range(H_B // PAIR):
            ql, dl, cl = slice(p * PAIR * DK_B, (p + 1) * PAIR * DK_B), slice(p * LANES, (p + 1) * LANES), \
                slice(p * PAIR * CAUG_W, (p + 1) * PAIR * CAUG_W)
            parts.append(jnp.dot(jnp.concatenate([lhs[key][0][:, ql], lhs[key][1][:, dl]], axis=1),
                                 jnp.concatenate([bdc[i, ql, cl], bdv[i, dl, cl]], axis=0),
                                 preferred_element_type=F32))
        num[key] = jnp.concatenate(parts, axis=1)
    upd = {key: jnp.dot(ks_t[key], bdv[slot[key]], preferred_element_type=F32) for key in dirs}
    for key in dirs:
        bi, d = key
        lo = LF_OFF + d * H_B
        dec_x = jnp.concatenate(
            [jnp.broadcast_to(dec_row[key][:, lo + h:lo + h + 1], (1, CAUG_W)) for h in range(H_B)], axis=1)
        c_scr[bi, d] = c_scr[bi, d] * dec_x + upd[key]
    heads = [(key, h) for key in dirs for h in range(H_B)]
    rows = {d: pl.ds(pl.multiple_of((n if d == 0 else n_chunks - 1 - n) * CHUNK, CHUNK), CHUNK) for d in range(N_DIR)}
    tots = []
    for key, h in heads:
        bi, d = key
        vs = slice(h * DV_B, (h + 1) * DV_B)
        den = jnp.maximum(jnp.abs(num[key][:, h * CAUG_W + DV_B:(h + 1) * CAUG_W]),
                          _col_bcast(emt[key], LF_OFF + d * H_B + h))
        hb = jnp.concatenate([num[key][:, h * CAUG_W:h * CAUG_W + LANES] / den,
                              num[key][:, h * CAUG_W + LANES:h * CAUG_W + DV_B] / den], axis=1)
        tot = hb + o_scr[bi, rows[d], vs]
        o_scr[bi, rows[d], vs] = tot
        tots.append(tot)
    inv = [lax.rsqrt(jnp.mean(tot * tot, axis=-1, keepdims=True) + EPS) for tot in tots]
    for (key, h), tot, r in zip(heads, tots, inv):
        bi, d = key
        vs = slice(h * DV_B, (h + 1) * DV_B)
        ogate = rest[d][bi, :, OFF_BO + h * DV_B:OFF_BO + (h + 1) * DV_B]
        yb_ref[bi, rows[d], vs] = (tot * r * norm_ref[...] * ogate).astype(BF16)

    if emit_state:
        @pl.when(n == n_chunks - 1)
        def _():
            for bi, d in dirs:
                for h in range(H_B):
                    cout_ref[bi, d, h] = c_scr[bi, d, :, h * CAUG_W:h * CAUG_W + DV_B]
                    nout_ref[bi, d, h] = c_scr[bi, d, :, h * CAUG_W + DV_B:(h + 1) * CAUG_W]
            mout_ref[...] = m_scr[...]


def _mlstm(rest, gates, norm_b, c0, m0, batch, n_chunks, emit_state):
    t = n_chunks * CHUNK
    has_init = c0 is not None
    bb = SCAN_BB
    rest, gates = (a.reshape(batch, t, a.shape[-1]) for a in (rest, gates))
    mspec = pl.BlockSpec((bb, N_DIR, 8, LANES), lambda b, n: (b, 0, 0, 0))
    in_specs = [*_scan_specs(bb, n_chunks, REST_W), *_scan_specs(bb, n_chunks, LANES),
                pl.BlockSpec((1, DV_B), lambda b, n: (0, 0))]
    args = [rest, rest, gates, gates, norm_b]
    if has_init:
        in_specs += [pl.BlockSpec((bb, N_DIR, DK_B, MW), lambda b, n: (b, 0, 0, 0)), mspec]
        args += [c0, m0]
    out_specs = [pl.BlockSpec((bb, t, B_W), lambda b, n: (b, 0, 0))]
    out_shape = [jax.ShapeDtypeStruct((batch, t, B_W), BF16)]
    if emit_state:
        out_specs += [pl.BlockSpec((bb, N_DIR, H_B, DK_B, DV_B), lambda b, n: (b, 0, 0, 0, 0)),
                      pl.BlockSpec((bb, N_DIR, H_B, DK_B, LANES), lambda b, n: (b, 0, 0, 0, 0)), mspec]
        out_shape += [jax.ShapeDtypeStruct((batch, N_DIR, H_B, DK_B, DV_B), F32),
                      jax.ShapeDtypeStruct((batch, N_DIR, H_B, DK_B, LANES), F32),
                      jax.ShapeDtypeStruct((batch, N_DIR, 8, LANES), F32)]
    out = pl.pallas_call(
        functools.partial(_mlstm_body, n_chunks=n_chunks, bb=bb, has_init=has_init, emit_state=emit_state),
        grid=(batch // bb, n_chunks),
        in_specs=in_specs,
        out_specs=out_specs,
        out_shape=out_shape,
        scratch_shapes=[pltpu.VMEM((bb, N_DIR, DK_B, MW), F32),
                        pltpu.VMEM((bb, N_DIR, 8, LANES), F32),
                        pltpu.VMEM((bb, t, B_W), F32),
                        pltpu.VMEM((bb * N_DIR, H_B * CHUNK, H_B * DK_B), BF16),
                        pltpu.VMEM((bb * N_DIR, H_B * DK_B, MW), BF16),
                        pltpu.VMEM((bb * N_DIR, H_B * CHUNK, MW), BF16)],
        compiler_params=_params(("arbitrary", "arbitrary")),
        name="mlstm_scan",
    )(*args)
    return [out[0].reshape(batch * t, B_W), *out[1:]]


OUT_RB = 256


def _outproj_body(ya_ref, yb_ref, wa_ref, wb_ref, x_ref, mod_ref, post1_ref, pre2_ref, x1_ref, h2_ref):
    for rb in range(x_ref.shape[0] // OUT_RB):
        rs = slice(rb * OUT_RB, (rb + 1) * OUT_RB)
        mix = (jnp.dot(ya_ref[rs, :], wa_ref[...], preferred_element_type=F32)
               + jnp.dot(yb_ref[rs, :], wb_ref[...], preferred_element_type=F32))
        x1 = x_ref[rs, :] + mod_ref[0, 2:3, :] * (_rms(mix) * post1_ref[...])
        x1_ref[rs, :] = x1
        h2 = _rms(x1) * pre2_ref[...] * (1.0 + mod_ref[0, 4:5, :]) + mod_ref[0, 3:4, :]
        h2_ref[rs, :] = h2.astype(BF16)


def _outproj(ya, yb, w_out, x2d, mod3, mod_map, post1, pre2, tm):
    m = x2d.shape[0]
    row = lambda i: (i, 0)
    const = lambda i: (0, 0)
    return pl.pallas_call(
        _outproj_body,
        grid=(m // tm,),
        in_specs=[pl.BlockSpec((tm, A_W), row), pl.BlockSpec((tm, B_W), row),
                  pl.BlockSpec((A_W, D_MODEL), lambda i: (0, 0)),
                  pl.BlockSpec((B_W, D_MODEL), lambda i: (1, 0)),
                  pl.BlockSpec((tm, D_MODEL), row),
                  pl.BlockSpec((1, 6, D_MODEL), lambda i: (mod_map(i * tm), 0, 0)),
                  pl.BlockSpec((1, D_MODEL), const), pl.BlockSpec((1, D_MODEL), const)],
        out_specs=[pl.BlockSpec((tm, D_MODEL), row), pl.BlockSpec((tm, D_MODEL), row)],
        out_shape=[jax.ShapeDtypeStruct((m, D_MODEL), F32), jax.ShapeDtypeStruct((m, D_MODEL), BF16)],
        compiler_params=_params(("parallel",)),
        name="outproj",
    )(ya, yb, w_out, w_out, x2d, mod3, post1, pre2)


def _ffn_body(h2_ref, w1_ref, w2_ref, x1_ref, mod_ref, post2_ref, o_ref):
    kk = pl.program_id(1)
    a = jnp.maximum(jnp.dot(h2_ref[...], w1_ref[...], preferred_element_type=F32), 0.0)
    contrib = jnp.dot((a * a).astype(BF16), w2_ref[...], preferred_element_type=F32)

    @pl.when(kk == 0)
    def _():
        o_ref[...] = contrib

    @pl.when((kk > 0) & (kk < pl.num_programs(1) - 1))
    def _():
        o_ref[...] += contrib

    @pl.when(kk == pl.num_programs(1) - 1)
    def _():
        f = o_ref[...] + contrib
        o_ref[...] = x1_ref[...] + mod_ref[0, 5:6, :] * (_rms(f) * post2_ref[...])


def _ffn(h2, w1, w2, x1, mod3, mod_map, post2, tm, fc):
    m = h2.shape[0]
    return pl.pallas_call(
        _ffn_body,
        grid=(m // tm, FFN // fc),
        in_specs=[pl.BlockSpec((tm, D_MODEL), lambda i, k: (i, 0)),
                  pl.BlockSpec((D_MODEL, fc), lambda i, k: (0, k)),
                  pl.BlockSpec((fc, D_MODEL), lambda i, k: (k, 0)),
                  pl.BlockSpec((tm, D_MODEL), lambda i, k: (i, 0)),
                  pl.BlockSpec((1, 6, D_MODEL), lambda i, k: (mod_map(i * tm), 0, 0)),
                  pl.BlockSpec((1, D_MODEL), lambda i, k: (0, 0))],
        out_specs=pl.BlockSpec((tm, D_MODEL), lambda i, k: (i, 0)),
        out_shape=jax.ShapeDtypeStruct((m, D_MODEL), F32),
        compiler_params=_params(("parallel", "arbitrary")),
        name="ffn",
    )(h2, w1, w2, x1, mod3, post2)


def _block(x, mod3, mod_of_row, lp, init, seq_len, emit_state):
    bsz, t, _ = x.shape
    x2d = x.reshape(bsz * t, D_MODEL)
    n_chunks = t // CHUNK
    qkv, rest, gates = _proj(x2d, mod3, lambda i: mod_of_row(i * PROJ_TM), lp["pre1"], lp["w16"],
                             lp["w_mid"], lp["w_gate"], lp["conv_w"], lp["gate_p"], seq_len)
    s0, c0, m0 = init if init is not None else (None, None, None)
    d_out = _delta(qkv, rest, gates, lp["norm_a"], s0, bsz, n_chunks, emit_state)
    m_out = _mlstm(rest, gates, lp["norm_b"], c0, m0, bsz, n_chunks, emit_state)
    x1, h2 = _outproj(d_out[0], m_out[0], lp["w_out"], x2d, mod3, mod_of_row, lp["post1"], lp["pre2"], 512)
    y = _ffn(h2, lp["w1"], lp["w2"], x1, mod3, mod_of_row, lp["post2"], 512, 1024)
    states = None
    if emit_state:
        m_fin = jnp.stack([m_out[3][:, d, 0, LF_OFF + d * H_B:LF_OFF + (d + 1) * H_B] for d in range(N_DIR)], axis=1)
        states = (d_out[1], m_out[1], m_out[2][..., 0], m_fin)
    return y.reshape(bsz, t, D_MODEL), states


def _layer_params(l, norm_mix_pre, norm_mix_post, norm_ffn_pre, norm_ffn_post, w_in, conv_w, a_log, dt_bias,
                  norm_a, mlstm_ibias, mlstm_fbias, norm_b, w_out, w_ffn1, w_ffn2):
    w = w_in[l]
    o_ag = QKV_W
    o_aa = o_ag + A_W
    o_ab = o_aa + N_DIR * H_A
    o_bq = o_ab + N_DIR * H_A
    o_bi = o_bq + 2 * H_B * DK_B + 2 * B_W
    o_bf = o_bi + N_DIR * H_B
    n_gate = 2 * N_DIR * H_A + 2 * N_DIR * H_B
    w16 = w.astype(BF16)
    w_gate = jnp.concatenate([w16[:, o_aa:o_bq], w16[:, o_bi:o_bf + N_DIR * H_B],
                              jnp.zeros((D_MODEL, LANES - n_gate), BF16)], axis=1)

    def lane_row(vals, off):
        return jnp.zeros((LANES,), F32).at[off:off + vals.size].set(vals.reshape(-1))

    gate_p = jnp.stack([lane_row(a_log[l], G_OFF), lane_row(dt_bias[l], G_OFF),
                        lane_row(mlstm_ibias[l], LI_OFF) + lane_row(mlstm_fbias[l], LF_OFF)]
                       + [jnp.zeros((LANES,), F32)] * 5)
    row = lambda v: v[l].reshape(1, -1)
    return dict(
        pre1=row(norm_mix_pre), post1=row(norm_mix_post), pre2=row(norm_ffn_pre), post2=row(norm_ffn_post),
        w16=w16, w_mid=w16[:, o_bq:o_bi], w_gate=w_gate, gate_p=gate_p,
        conv_w=jnp.concatenate([conv_w[l].T, jnp.zeros((5, QKV_W), F32)], axis=0),
        norm_a=row(norm_a), norm_b=row(norm_b),
        w_out=w_out[l].astype(BF16), w1=w_ffn1[l].astype(BF16), w2=w_ffn2[l].astype(BF16))


def kernel(x_prompt, x_sample, state_delta, state_mlstm_C, state_mlstm_n, state_mlstm_m, c, c_ctx, w_ada, b_ada, norm_mix_pre, norm_mix_post, norm_ffn_pre, norm_ffn_post, w_in, conv_w, a_log, dt_bias, norm_a, mlstm_ibias, mlstm_fbias, norm_b, w_out, w_ffn1, w_ffn2):
    depth = w_in.shape[0]
    n_lat = x_sample.shape[0]
    t_lat = x_sample.shape[1]
    cond = jnp.concatenate([c_ctx[None, :], c, jnp.zeros((8 - 1 - n_lat, D_MODEL), F32)], axis=0)
    y_prompt, y_sample = x_prompt, x_sample
    acc = ([], [], [], [])
    for l in range(depth):
        lp = _layer_params(l, norm_mix_pre, norm_mix_post, norm_ffn_pre, norm_ffn_post, w_in, conv_w, a_log,
                           dt_bias, norm_a, mlstm_ibias, mlstm_fbias, norm_b, w_out, w_ffn1, w_ffn2)
        mod = _ada(cond, w_ada, b_ada[l].reshape(1, -1), l)
        mod3 = mod[:1 + n_lat].reshape(1 + n_lat, 6, D_MODEL)
        y_prompt, st = _block(y_prompt, mod3, lambda r: 0, lp, None, x_prompt.shape[1], True)
        for a, s in zip(acc, st):
            a.append(s)
        n_rep = jnp.broadcast_to(state_mlstm_n[:, l][..., None], state_mlstm_n[:, l].shape + (LANES,))
        c_aug0 = jnp.concatenate([state_mlstm_C[:, l], n_rep], axis=-1)
        c_aug0 = c_aug0.transpose(0, 1, 3, 2, 4).reshape(n_lat, N_DIR, DK_B, MW)
        m0 = jnp.zeros((n_lat, N_DIR, LANES), F32)
        for d in range(N_DIR):
            m0 = m0.at[:, d, LF_OFF + d * H_B:LF_OFF + (d + 1) * H_B].set(state_mlstm_m[:, l, d])
        m0 = jnp.broadcast_to(m0[:, :, None, :], (n_lat, N_DIR, 8, LANES))
        y_sample, _ = _block(y_sample, mod3, lambda r: 1 + r // t_lat, lp, (state_delta[:, l], c_aug0, m0),
                             GRID_W, False)
    return (y_prompt, y_sample) + tuple(jnp.stack(a, axis=1) for a in acc)
```

```python
import functools

import jax
import jax.numpy as jnp
from jax import lax
from jax.experimental import pallas as pl
from jax.experimental.pallas import tpu as pltpu

F32 = jnp.float32
BF16 = jnp.bfloat16

D_MODEL = 2048
N_DIR = 2
A_W = D_MODEL // 2
B_W = D_MODEL - A_W
DK_A = 128
DV_A = 128
H_A = A_W // DV_A
DV_B = 256
DK_B = DV_B // 2
H_B = B_W // DV_B
GRID_W = 64
CHUNK = 64
FFN = 4 * D_MODEL
EPS = 1e-6
LANES = 128
SUBLANES = 8
NEG = -1e30

QKV_W = 3 * A_W
REST_W = A_W + 2 * H_B * DK_B + 2 * B_W
OFF_AG, OFF_BQ, OFF_BK, OFF_BV, OFF_BO = 0, A_W, A_W + H_B * DK_B, A_W + 2 * H_B * DK_B, A_W + 2 * H_B * DK_B + B_W
G_OFF, BETA_OFF = 0, N_DIR * H_A
LI_OFF = 2 * N_DIR * H_A
LF_OFF = LI_OFF + N_DIR * H_B

VMEM_LIMIT = 56 * 1024 * 1024


def _sigmoid(x):
    return 1.0 / (1.0 + jnp.exp(-x))


def _softplus(x):
    return jnp.maximum(x, 0.0) + jnp.log1p(jnp.exp(-jnp.abs(x)))


def _dot(a, b):
    return jnp.dot(a.astype(BF16), b.astype(BF16), preferred_element_type=F32)


def _dot_f32(a, b):
    return jnp.dot(a, b, precision=lax.Precision.HIGHEST, preferred_element_type=F32)


def _rms(x):
    return x * lax.rsqrt(jnp.mean(x * x, axis=-1, keepdims=True) + EPS)


def _params(sem):
    return pltpu.CompilerParams(dimension_semantics=sem, vmem_limit_bytes=VMEM_LIMIT)


def _ada_body(c_ref, w_ref, b_ref, o_ref):
    c = c_ref[...]
    o_ref[...] = _dot(c * _sigmoid(c), w_ref[...]) + b_ref[...]


def _ada(c_all, w_ada, b, l):
    n = w_ada.shape[2]
    tn = 1024
    return pl.pallas_call(
        _ada_body,
        grid=(n // tn,),
        in_specs=[pl.BlockSpec(c_all.shape, lambda j: (0, 0)),
                  pl.BlockSpec((None, D_MODEL, tn), lambda j: (l, 0, j)),
                  pl.BlockSpec((1, tn), lambda j: (0, j))],
        out_specs=pl.BlockSpec((c_all.shape[0], tn), lambda j: (0, j)),
        out_shape=jax.ShapeDtypeStruct((c_all.shape[0], n), F32),
        compiler_params=_params(("arbitrary",)),
        name="ada",
    )(c_all, w_ada, b)


PROJ_TM = 1024
PROJ_RB = 512
PROJ_CB = 256
PROJ_TN = 512
N_QKV_T = QKV_W // PROJ_TN
N_REST_T = REST_W // PROJ_TN
N_HEAD_T = (QKV_W + A_W) // PROJ_TN
N_MID_T = N_QKV_T + N_REST_T - N_HEAD_T


def _proj_body(x_ref, mod_ref, g_ref, wh_ref, wb_ref, wg_ref, cw_ref, gp_ref, qkv_ref, rest_ref, gate_ref, h_scr, *,
               seq_len):
    j = pl.program_id(1)
    row_blocks = [slice(rb * PROJ_RB, (rb + 1) * PROJ_RB) for rb in range(PROJ_TM // PROJ_RB)]
    col_blocks = [slice(cb * PROJ_CB, (cb + 1) * PROJ_CB) for cb in range(PROJ_TN // PROJ_CB)]

    def run_rows(rs, w_ref, epilogue, out_ref):
        for cs in col_blocks:
            epilogue(jnp.dot(h_scr[rs, :], w_ref[:, cs], preferred_element_type=F32), out_ref, rs, cs)

    def run(w_ref, epilogue, out_ref):
        for rs in row_blocks:
            run_rows(rs, w_ref, epilogue, out_ref)

    def run_first():
        for rs in row_blocks:
            y = _rms(x_ref[rs, :]) * g_ref[...]
            h_scr[rs, :] = (y * (1.0 + mod_ref[0, 1:2, :]) + mod_ref[0, 0:1, :]).astype(BF16)
            run_rows(rs, wh_ref, ep_l2(DK_A ** -0.5), qkv_ref)
            ep_gate(jnp.dot(h_scr[rs, :], wg_ref[...], preferred_element_type=F32), gate_ref, rs)

    def conv_silu(acc, cs, g):
        a = acc[:, g * LANES:(g + 1) * LANES]
        ls = slice(cs.start + g * LANES, cs.start + (g + 1) * LANES)
        pos = lax.broadcasted_iota(jnp.int32, a.shape, 0) & (seq_len - 1)
        prev = jnp.where(pos == 0, 0.0, pltpu.roll(a, 1, 0))
        nxt = jnp.where(pos == seq_len - 1, 0.0, pltpu.roll(a, a.shape[0] - 1, 0))
        y = prev * cw_ref[0:1, ls] + a * cw_ref[1:2, ls] + nxt * cw_ref[2:3, ls]
        return ls, y * _sigmoid(y)

    def ep_l2(scale):
        def f(acc, out_ref, rs, cs):
            for g in range(PROJ_CB // LANES):
                ls, blk = conv_silu(acc, cs, g)
                inv = lax.rsqrt(jnp.sum(blk * blk, axis=-1, keepdims=True) + EPS)
                out_ref[rs, ls] = blk * (inv * scale)
        return f

    def ep_conv(acc, out_ref, rs, cs):
        for g in range(PROJ_CB // LANES):
            ls, blk = conv_silu(acc, cs, g)
            out_ref[rs, ls] = blk

    def ep_map(fn):
        def f(acc, out_ref, rs, cs):
            out_ref[rs, cs] = fn(acc)
        return f

    def ep_gate(z, out_ref, rs):
        lane = lax.broadcasted_iota(jnp.int32, z.shape, 1)
        g = -jnp.exp(gp_ref[0:1, :]) * _softplus(z + gp_ref[1:2, :])
        li = z + gp_ref[2:3, :]
        out_ref[rs, :] = jnp.where(lane < BETA_OFF, g,
                                   jnp.where(lane < LI_OFF, _sigmoid(z),
                                             jnp.where(lane < LF_OFF, li,
                                                       jnp.where(lane < LF_OFF + N_DIR * H_B, -_softplus(-li), 0.0))))

    tp = A_W // PROJ_TN
    c0 = (j - N_QKV_T) * PROJ_TN
    in_rest = (j >= N_QKV_T) & (j < N_QKV_T + N_REST_T)
    pl.when(j == 0)(run_first)
    variants = [
        ((j > 0) & (j < tp), wh_ref, ep_l2(DK_A ** -0.5), qkv_ref),
        ((j >= tp) & (j < 2 * tp), wh_ref, ep_l2(1.0), qkv_ref),
        ((j >= 2 * tp) & (j < N_QKV_T), wh_ref, ep_conv, qkv_ref),
        (in_rest & (c0 < OFF_BQ), wh_ref, ep_map(lambda a: a * _sigmoid(a)), rest_ref),
        (in_rest & (c0 >= OFF_BQ) & (c0 < OFF_BK), wb_ref, ep_map(lambda a: a * (DK_B ** -0.5)), rest_ref),
        (in_rest & (c0 >= OFF_BK) & (c0 < OFF_BO), wb_ref, ep_map(lambda a: a), rest_ref),
        (in_rest & (c0 >= OFF_BO), wb_ref, ep_map(_sigmoid), rest_ref),
    ]
    for cond, w_ref, epilogue, out_ref in variants:
        pl.when(cond)(functools.partial(run, w_ref, epilogue, out_ref))


def _proj(x2d, mod3, mod_map, pre_g, w16, w_mid, w_gate, conv_w, gate_p, seq_len):
    m = x2d.shape[0]
    tm, tn = PROJ_TM, PROJ_TN
    assert PROJ_RB % seq_len == 0 and m % tm == 0
    return pl.pallas_call(
        functools.partial(_proj_body, seq_len=seq_len),
        grid=(m // tm, N_HEAD_T + N_MID_T),
        in_specs=[pl.BlockSpec((tm, D_MODEL), lambda i, j: (i, 0)),
                  pl.BlockSpec((1, 6, D_MODEL), lambda i, j: (mod_map(i), 0, 0)),
                  pl.BlockSpec((1, D_MODEL), lambda i, j: (0, 0)),
                  pl.BlockSpec((D_MODEL, tn), lambda i, j: (0, jnp.minimum(j, N_HEAD_T - 1))),
                  pl.BlockSpec((D_MODEL, tn), lambda i, j: (0, jnp.clip(j - N_HEAD_T, 0, N_MID_T - 1))),
                  pl.BlockSpec((D_MODEL, LANES), lambda i, j: (0, 0)),
                  pl.BlockSpec((SUBLANES, tn), lambda i, j: (0, jnp.minimum(j, N_QKV_T - 1))),
                  pl.BlockSpec((SUBLANES, LANES), lambda i, j: (0, 0))],
        out_specs=[pl.BlockSpec((tm, tn), lambda i, j: (i, jnp.minimum(j, N_QKV_T - 1))),
                   pl.BlockSpec((tm, tn), lambda i, j: (i, jnp.clip(j - N_QKV_T, 0, N_REST_T - 1))),
                   pl.BlockSpec((tm, LANES), lambda i, j: (i, 0))],
        out_shape=[jax.ShapeDtypeStruct((m, QKV_W), F32), jax.ShapeDtypeStruct((m, REST_W), F32),
                   jax.ShapeDtypeStruct((m, LANES), F32)],
        scratch_shapes=[pltpu.VMEM((tm, D_MODEL), BF16)],
        compiler_params=_params(("parallel", "arbitrary")),
        name="proj",
    )(x2d, mod3, pre_g, w16, w_mid, w_gate, conv_w, gate_p)


def _masks(d):
    row = lax.broadcasted_iota(jnp.int32, (CHUNK, CHUNK), 0)
    col = lax.broadcasted_iota(jnp.int32, (CHUNK, CHUNK), 1)
    if d == 0:
        return row >= col, row > col, row == col
    return row <= col, row < col, row == col


GROUP = 4
N_GROUPS = H_A // GROUP
PAIR = LANES // CHUNK


def _split(x):
    hi = x.astype(BF16)
    return hi, (x - hi.astype(F32)).astype(BF16)


def _store_blocks(ref, x, blk_r, blk_c, col0=0):
    for h in range(GROUP):
        ref[h * blk_r:(h + 1) * blk_r, col0 + h * blk_c:col0 + (h + 1) * blk_c] = x[:, h * blk_c:(h + 1) * blk_c]


def _dot3(a_hi, a_lo, b_hi, b_lo):
    m = a_hi.shape[0]
    r = jnp.dot(jnp.concatenate([a_hi, a_lo], axis=0), b_hi, preferred_element_type=F32)
    return r[:m] + r[m:] + jnp.dot(a_hi, b_lo, preferred_element_type=F32)


HALF = CHUNK // 2


def _tri_inverse_wide(lmats, upper, bd_hi, bd_lo, by_hi, by_lo):
    width = lmats[0].shape[1]
    n_blk = width // HALF
    row = lax.broadcasted_iota(jnp.int32, (HALF, width), 0)
    lane = lax.broadcasted_iota(jnp.int32, (HALF, width), 1)
    lead = (lane & (CHUNK - 1)) < HALF
    eye_d = (row == (lane & (HALF - 1))).astype(F32)

    def bd_dot(i, a, b_hi, b_lo):
        for blk in range(n_blk):
            sl = slice(blk * HALF, (blk + 1) * HALF)
            bd_hi[i, sl, sl] = b_hi[:, sl]
            bd_lo[i, sl, sl] = b_lo[:, sl]
        return _dot3(*a, bd_hi[i], bd_lo[i])

    idx = range(len(lmats))
    diag = [jnp.where(lead, l[:HALF], l[HALF:]) for l in lmats]
    s = [eye_d - dg for dg in diag]
    p = []
    for i in idx:
        m_hi, m_lo = _split(-diag[i])
        p.append(bd_dot(i, (m_hi, m_lo), m_hi, m_lo))
    for _ in range(3):
        for i in idx:
            p_hi, p_lo = _split(p[i])
            s_hi, s_lo = _split(s[i])
            r = bd_dot(i, (jnp.concatenate([p_hi, s_hi], axis=0), jnp.concatenate([p_lo, s_lo], axis=0)), p_hi, p_lo)
            p[i] = r[:HALF]
            s[i] = s[i] + r[HALF:]
    for i in idx:
        p_hi, p_lo = _split(p[i])
        s[i] = s[i] + bd_dot(i, _split(s[i]), p_hi, p_lo)
    y = []
    for i in idx:
        c_blk = jnp.where(lead, 0.0, lmats[i][:HALF]) if upper[i] else jnp.where(lead, lmats[i][HALF:], 0.0)
        y.append(bd_dot(i, _split(c_blk), *_split(s[i])))
    out = []
    for i in idx:
        y_hi, y_lo = _split(y[i])
        for h in range(width // CHUNK):
            a_sl, b_sl = slice(h * CHUNK, h * CHUNK + HALF), slice(h * CHUNK + HALF, (h + 1) * CHUNK)
            rs, cs = (a_sl, b_sl) if upper[i] else (b_sl, a_sl)
            by_hi[i, rs, cs] = y_hi[:, cs]
            by_lo[i, rs, cs] = y_lo[:, cs]
        ai, bi = jnp.where(lead, s[i], 0.0), jnp.where(lead, 0.0, s[i])
        x = _dot3(*_split(ai if upper[i] else bi), by_hi[i], by_lo[i])
        out.append(jnp.concatenate([ai - x, bi] if upper[i] else [ai, bi - x], axis=0))
    return out


def _col_bcast(tile, c, width=LANES):
    return jnp.broadcast_to(tile[:, c:c + 1], (tile.shape[0], width))


def _delta_body(*refs, n_chunks, bb, has_init, emit_state):
    qkv = refs[0:2]
    ag = refs[2:4]
    gt = refs[4:6]
    norm_ref = refs[6]
    pos = 7
    s0_ref = None
    if has_init:
        s0_ref = refs[pos]
        pos += 1
    ya_ref = refs[pos]
    pos += 1
    sout_ref = None
    if emit_state:
        sout_ref = refs[pos]
        pos += 1
    s_scr, o_scr, bdn_hi, bdn_lo, bdy_hi, bdy_lo, bdk, bduw, bds, bdv = refs[pos:pos + 10]

    n = pl.program_id(1)
    gw = GROUP * DK_A

    @pl.when((n == 0) & (pl.program_id(0) == 0))
    def _():
        for ref in (bdn_hi, bdn_lo, bdy_hi, bdy_lo, bdk, bduw, bds, bdv):
            ref[...] = jnp.zeros_like(ref)

    @pl.when(n == 0)
    def _():
        o_scr[...] = jnp.zeros_like(o_scr)
        for bi in range(bb):
            for d in range(N_DIR):
                for h in range(H_A):
                    blk = s0_ref[bi, d, h] if has_init else jnp.zeros((DK_A, DV_A), F32)
                    s_scr[bi, d, h // GROUP, :, (h % GROUP) * DV_A:(h % GROUP + 1) * DV_A] = blk

    row = lax.broadcasted_iota(jnp.int32, (CHUNK, LANES), 0)
    lane = lax.broadcasted_iota(jnp.int32, (CHUNK, LANES), 1)
    col = lane & (CHUNK - 1)
    left = lane < CHUNK
    probs = [(bi, d, g) for bi in range(bb) for d in range(N_DIR) for g in range(N_GROUPS)]
    masks = {0: (row >= col, row > col), 1: (row <= col, row < col)}
    gates, gc, gc_t, eg, egl = {}, {}, {}, {}, {}
    for bi in range(bb):
        for d in range(N_DIR):
            key = (bi, d)
            gates[key] = gt[d][bi]
            gc[key] = _dot_f32(_masks(d)[0].astype(F32), gates[key])
            gc_t[key] = jnp.concatenate([gc[key], gc[key]], axis=0).T
            last = CHUNK - 1 if d == 0 else 0
            eg[key] = jnp.exp(gc[key])
            egl[key] = jnp.exp(gc[key][last:last + 1, :] - gc[key])

    def cols_of(d, g):
        return [G_OFF + d * H_A + g * GROUP + hl for hl in range(GROUP)]

    q16, ks, beta_xs, decays, grams = [], [], [], [], []
    for gi, (bi, d, g) in enumerate(probs):
        key = (bi, d)
        cols = cols_of(d, g)
        incl, _ = masks[d]
        k = qkv[d][bi, :, A_W + g * gw:A_W + (g + 1) * gw]
        beta_x = jnp.concatenate([_col_bcast(gates[key], BETA_OFF - G_OFF + c) for c in cols], axis=1)
        decay = []
        for p in range(GROUP // PAIR):
            c0, c1 = cols[PAIR * p], cols[PAIR * p + 1]
            gcol = jnp.where(left, _col_bcast(gc[key], c0), _col_bcast(gc[key], c1))
            grow = jnp.where(left[0:1], gc_t[key][c0:c0 + 1, :], gc_t[key][c1:c1 + 1, :])
            decay.append(jnp.exp(jnp.where(incl, gcol - grow, NEG)))
        decays.append(jnp.concatenate(decay, axis=1))
        q16.append(qkv[d][bi, :, g * gw:(g + 1) * gw].astype(BF16))
        ks.append(k)
        beta_xs.append(beta_x)
        _store_blocks(bdk.at[gi], k.astype(BF16), CHUNK, DK_A)
    for gi in range(len(probs)):
        grams.append(lax.dot_general(jnp.concatenate([q16[gi], (ks[gi] * beta_xs[gi]).astype(BF16)], axis=0),
                                     bdk[gi], (((1,), (1,)), ((), ())), preferred_element_type=F32))
    attns, lmats = [], []
    for gi, (bi, d, g) in enumerate(probs):
        strict_w = jnp.concatenate([masks[d][1]] * (GROUP // PAIR), axis=1)
        attns.append((grams[gi][:CHUNK] * decays[gi]).astype(BF16))
        lmats.append(jnp.where(strict_w, grams[gi][CHUNK:] * decays[gi], 0.0))
    ainvs = _tri_inverse_wide(lmats, [d == 1 for _, d, _ in probs], bdn_hi, bdn_lo, bdy_hi, bdy_lo)
    eg_xs, kd_ts = [], []
    for gi, (bi, d, g) in enumerate(probs):
        key = (bi, d)
        cols = cols_of(d, g)
        v = qkv[d][bi, :, 2 * A_W + g * gw:2 * A_W + (g + 1) * gw]
        eg_x = jnp.concatenate([_col_bcast(eg[key], c) for c in cols], axis=1)
        egl_x = jnp.concatenate([_col_bcast(egl[key], c) for c in cols], axis=1)
        eg_xs.append(eg_x)
        _store_blocks(bduw.at[gi], (v * beta_xs[gi]).astype(BF16), CHUNK, DV_A)
        _store_blocks(bduw.at[gi], (ks[gi] * (beta_xs[gi] * eg_x)).astype(BF16), CHUNK, DK_A, col0=gw)
        kd = ks[gi] * egl_x
        kd_ts.append(jnp.concatenate([kd[:, hl * DK_A:(hl + 1) * DK_A] for hl in range(GROUP)], axis=0)
                     .T.astype(BF16))
        _store_blocks(bds.at[gi], s_scr[bi, d, g].astype(BF16), DK_A, DV_A)
    uws = []
    for gi in range(len(probs)):
        t_hi, t_lo = _split(ainvs[gi])
        r = jnp.dot(jnp.concatenate([t_hi, t_lo], axis=0), bduw[gi], preferred_element_type=F32)
        uws.append(r[:CHUNK] + r[CHUNK:])
    ws_qs = []
    hw = gw // 2
    for gi in range(len(probs)):
        wq = jnp.concatenate([uws[gi][:, gw:].astype(BF16), q16[gi]], axis=0)
        ws_qs.append(jnp.concatenate(
            [jnp.dot(wq[:, p * hw:(p + 1) * hw], bds[gi, p * hw:(p + 1) * hw, p * hw:(p + 1) * hw],
                     preferred_element_type=F32) for p in range(2)], axis=1))
    for gi in range(len(probs)):
        _store_blocks(bdv.at[gi], (uws[gi][:, :gw] - ws_qs[gi][:CHUNK]).astype(BF16), CHUNK, DV_A)
    rs = []
    for gi in range(len(probs)):
        rs.append(jnp.dot(jnp.concatenate([attns[gi], kd_ts[gi]], axis=0), bdv[gi], preferred_element_type=F32))
    tots = []
    for gi, (bi, d, g) in enumerate(probs):
        cols = cols_of(d, g)
        last = CHUNK - 1 if d == 0 else 0
        o = ws_qs[gi][CHUNK:] * eg_xs[gi] + rs[gi][:CHUNK]
        eg_last = jnp.concatenate(
            [jnp.broadcast_to(eg[bi, d][last:last + 1, c:c + 1], (1, DV_A)) for c in cols], axis=1)
        s_scr[bi, d, g] = s_scr[bi, d, g] * eg_last + rs[gi][CHUNK:]
        cidx = n if d == 0 else n_chunks - 1 - n
        rows = pl.ds(pl.multiple_of(cidx * CHUNK, CHUNK), CHUNK)
        tot = o + o_scr[bi, rows, g * gw:(g + 1) * gw]
        o_scr[bi, rows, g * gw:(g + 1) * gw] = tot
        tots.append((rows, tot))
    blks = [(bi, d, rows, g * GROUP + hl, tot[:, hl * DV_A:(hl + 1) * DV_A])
            for (bi, d, g), (rows, tot) in zip(probs, tots) for hl in range(GROUP)]
    inv = [lax.rsqrt(jnp.mean(blk * blk, axis=-1, keepdims=True) + EPS) for *_, blk in blks]
    for (bi, d, rows, h, blk), r in zip(blks, inv):
        hs = slice(h * DV_A, (h + 1) * DV_A)
        ya_ref[bi, rows, hs] = (blk * r * norm_ref[...] * ag[d][bi, :, hs]).astype(BF16)

    if emit_state:
        @pl.when(n == n_chunks - 1)
        def _():
            for bi in range(bb):
                for d in range(N_DIR):
                    for h in range(H_A):
                        sout_ref[bi, d, h] = s_scr[bi, d, h // GROUP, :, (h % GROUP) * DV_A:(h % GROUP + 1) * DV_A]


SCAN_BB = 2


def _scan_specs(bb, n_chunks, width):
    return (pl.BlockSpec((bb, CHUNK, width), lambda b, n: (b, n, 0)),
            pl.BlockSpec((bb, CHUNK, width), lambda b, n: (b, n_chunks - 1 - n, 0)))


def _delta(qkv, rest, gates, norm_a, s0, batch, n_chunks, emit_state):
    t = n_chunks * CHUNK
    has_init = s0 is not None
    bb = SCAN_BB
    ng = bb * N_DIR * N_GROUPS
    qkv, rest, gates = (a.reshape(batch, t, a.shape[-1]) for a in (qkv, rest, gates))
    in_specs = [*_scan_specs(bb, n_chunks, QKV_W), *_scan_specs(bb, n_chunks, A_W), *_scan_specs(bb, n_chunks, LANES),
                pl.BlockSpec((1, DV_A), lambda b, n: (0, 0))]
    args = [qkv, qkv, rest, rest, gates, gates, norm_a]
    sspec = pl.BlockSpec((bb, N_DIR, H_A, DK_A, DV_A), lambda b, n: (b, 0, 0, 0, 0))
    if has_init:
        in_specs.append(sspec)
        args.append(s0)
    out_specs = [pl.BlockSpec((bb, t, A_W), lambda b, n: (b, 0, 0))]
    out_shape = [jax.ShapeDtypeStruct((batch, t, A_W), BF16)]
    if emit_state:
        out_specs.append(sspec)
        out_shape.append(jax.ShapeDtypeStruct((batch, N_DIR, H_A, DK_A, DV_A), F32))
    out = pl.pallas_call(
        functools.partial(_delta_body, n_chunks=n_chunks, bb=bb, has_init=has_init, emit_state=emit_state),
        grid=(batch // bb, n_chunks),
        in_specs=in_specs,
        out_specs=out_specs,
        out_shape=out_shape,
        scratch_shapes=[pltpu.VMEM((bb, N_DIR, N_GROUPS, DK_A, GROUP * DV_A), F32), pltpu.VMEM((bb, t, A_W), F32),
                        pltpu.VMEM((ng, GROUP * CHUNK, GROUP * CHUNK), BF16),
                        pltpu.VMEM((ng, GROUP * CHUNK, GROUP * CHUNK), BF16),
                        pltpu.VMEM((ng, GROUP * CHUNK, GROUP * CHUNK), BF16),
                        pltpu.VMEM((ng, GROUP * CHUNK, GROUP * CHUNK), BF16),
                        pltpu.VMEM((ng, GROUP * CHUNK, GROUP * DK_A), BF16),
                        pltpu.VMEM((ng, GROUP * CHUNK, 2 * GROUP * DK_A), BF16),
                        pltpu.VMEM((ng, GROUP * DK_A, GROUP * DV_A), BF16),
                        pltpu.VMEM((ng, GROUP * CHUNK, GROUP * DV_A), BF16)],
        compiler_params=_params(("arbitrary", "arbitrary")),
        name="delta_scan",
    )(*args)
    return [out[0].reshape(batch * t, A_W), *out[1:]]


CAUG_W = DV_B + LANES
MW = H_B * CAUG_W


def _scan_max(x, d):
    row = lax.broadcasted_iota(jnp.int32, x.shape, 0)
    s = 1
    while s < CHUNK:
        if d == 0:
            shifted = jnp.where(row >= s, pltpu.roll(x, s, 0), NEG)
        else:
            shifted = jnp.where(row < CHUNK - s, pltpu.roll(x, CHUNK - s, 0), NEG)
        x = jnp.maximum(x, shifted)
        s *= 2
    return x


def _mlstm_body(*refs, n_chunks, bb, has_init, emit_state):
    rest = refs[0:2]
    gt = refs[2:4]
    norm_ref = refs[4]
    pos = 5
    c0_ref = m0_ref = None
    if has_init:
        c0_ref, m0_ref = refs[pos], refs[pos + 1]
        pos += 2
    yb_ref = refs[pos]
    pos += 1
    cout_ref = nout_ref = mout_ref = None
    if emit_state:
        cout_ref, nout_ref, mout_ref = refs[pos:pos + 3]
        pos += 3
    c_scr, m_scr, o_scr, bdk, bdc, bdv = refs[pos:pos + 6]

    n = pl.program_id(1)
    qw = H_B * DK_B

    @pl.when((n == 0) & (pl.program_id(0) == 0))
    def _():
        for ref in (bdk, bdc, bdv):
            ref[...] = jnp.zeros_like(ref)
        for i in range(bb * N_DIR):
            for h in range(H_B):
                bdv[i, h * CHUNK:(h + 1) * CHUNK, h * CAUG_W + DV_B:(h + 1) * CAUG_W] = jnp.ones((CHUNK, LANES), BF16)

    @pl.when(n == 0)
    def _():
        o_scr[...] = jnp.zeros_like(o_scr)
        if has_init:
            c_scr[...] = c0_ref[...]
            m_scr[...] = m0_ref[...]
        else:
            c_scr[...] = jnp.zeros_like(c_scr)
            m_scr[...] = jnp.zeros_like(m_scr)

    row = lax.broadcasted_iota(jnp.int32, (CHUNK, LANES), 0)
    lane = lax.broadcasted_iota(jnp.int32, (CHUNK, LANES), 1)
    col = lane & (CHUNK - 1)
    left = lane < CHUNK
    dirs = [(bi, d) for bi in range(bb) for d in range(N_DIR)]
    slot = {key: i for i, key in enumerate(dirs)}
    q16 = {}
    for key in dirs:
        bi, d = key
        i = slot[key]
        q16[key] = rest[d][bi, :, OFF_BQ:OFF_BQ + qw].astype(BF16)
        _store_blocks(bdk.at[i], rest[d][bi, :, OFF_BK:OFF_BK + qw].astype(BF16), CHUNK, DK_B)
        for h in range(H_B):
            bdv[i, h * CHUNK:(h + 1) * CHUNK, h * CAUG_W:h * CAUG_W + DV_B] = (
                rest[d][bi, :, OFF_BV + h * DV_B:OFF_BV + (h + 1) * DV_B].astype(BF16))
        _store_blocks(bdc.at[i], c_scr[bi, d].astype(BF16), DK_B, CAUG_W)
    qk = {key: lax.dot_general(q16[key], bdk[slot[key]], (((1,), (1,)), ((), ())), preferred_element_type=F32)
          for key in dirs}
    nc, a_t, iw, emt, ksc, dec_row = {}, {}, {}, {}, {}, {}
    for key in dirs:
        bi, d = key
        lo = LF_OFF + d * H_B
        mine = (lane >= lo) & (lane < lo + H_B)
        g = gt[d][bi]
        gc = jnp.where(mine, _dot_f32(_masks(d)[0].astype(F32), g), 0.0)
        a = jnp.where(mine, pltpu.roll(g, LF_OFF - LI_OFF, 1), 0.0) - gc
        last = CHUNK - 1 if d == 0 else 0
        m_old = m_scr[bi, d][0:1, :]
        mx = jnp.maximum(m_old, _scan_max(a, d))
        mxl = mx[last:last + 1, :]
        nc[key] = -mx
        a_t[key] = jnp.concatenate([a, a], axis=0).T
        iw[key] = jnp.exp(m_old - mx)
        emt[key] = jnp.exp(-(gc + mx))
        ksc[key] = jnp.exp(a - mxl)
        dec_row[key] = jnp.exp(m_old - mxl)
        m_scr[bi, d] = jnp.broadcast_to(gc[last:last + 1, :] + mxl, (SUBLANES, LANES))

    ks_t = {}
    for key in dirs:
        bi, d = key
        lo = LF_OFF + d * H_B
        ks = rest[d][bi, :, OFF_BK:OFF_BK + qw] * jnp.concatenate(
            [_col_bcast(ksc[key], lo + h) for h in range(H_B)], axis=1)
        ks_t[key] = jnp.concatenate([ks[:, h * DK_B:(h + 1) * DK_B] for h in range(H_B)], axis=0).T.astype(BF16)
    lhs = {}
    for key in dirs:
        bi, d = key
        lo = LF_OFF + d * H_B
        incl = row >= col if d == 0 else row <= col
        log_w = []
        for p in range(H_B // PAIR):
            l0, l1 = lo + PAIR * p, lo + PAIR * p + 1
            ccol = jnp.where(left, _col_bcast(nc[key], l0), _col_bcast(nc[key], l1))
            crow = jnp.where(left[0:1], a_t[key][l0:l0 + 1, :], a_t[key][l1:l1 + 1, :])
            log_w.append(jnp.where(incl, ccol + crow, NEG))
        dw = jnp.exp(jnp.concatenate(log_w, axis=1)) * qk[key]
        iw_x = jnp.concatenate([_col_bcast(iw[key], lo + h) for h in range(H_B)], axis=1)
        lhs[key] = ((rest[d][bi, :, OFF_BQ:OFF_BQ + qw] * iw_x).astype(BF16), dw.astype(BF16))
    num = {}
    for key in dirs:
        i = slot[key]
        parts = []
        for p in range(H_B // PAIR):
            ql, dl, cl = slice(p * PAIR * DK_B, (p + 1) * PAIR * DK_B), slice(p * LANES, (p + 1) * LANES), \
                slice(p * PAIR * CAUG_W, (p + 1) * PAIR * CAUG_W)
            parts.append(jnp.dot(jnp.concatenate([lhs[key][0][:, ql], lhs[key][1][:, dl]], axis=1),
                                 jnp.concatenate([bdc[i, ql, cl], bdv[i, dl, cl]], axis=0),
                                 preferred_element_type=F32))
        num[key] = jnp.concatenate(parts, axis=1)
    upd = {key: jnp.dot(ks_t[key], bdv[slot[key]], preferred_element_type=F32) for key in dirs}
    for key in dirs:
        bi, d = key
        lo = LF_OFF + d * H_B
        dec_x = jnp.concatenate(
            [jnp.broadcast_to(dec_row[key][:, lo + h:lo + h + 1], (1, CAUG_W)) for h in range(H_B)], axis=1)
        c_scr[bi, d] = c_scr[bi, d] * dec_x + upd[key]
    heads = [(key, h) for key in dirs for h in range(H_B)]
    rows = {d: pl.ds(pl.multiple_of((n if d == 0 else n_chunks - 1 - n) * CHUNK, CHUNK), CHUNK) for d in range(N_DIR)}
    tots = []
    for key, h in heads:
        bi, d = key
        vs = slice(h * DV_B, (h + 1) * DV_B)
        den = jnp.maximum(jnp.abs(num[key][:, h * CAUG_W + DV_B:(h + 1) * CAUG_W]),
                          _col_bcast(emt[key], LF_OFF + d * H_B + h))
        hb = jnp.concatenate([num[key][:, h * CAUG_W:h * CAUG_W + LANES] / den,
                              num[key][:, h * CAUG_W + LANES:h * CAUG_W + DV_B] / den], axis=1)
        tot = hb + o_scr[bi, rows[d], vs]
        o_scr[bi, rows[d], vs] = tot
        tots.append(tot)
    inv = [lax.rsqrt(jnp.mean(tot * tot, axis=-1, keepdims=True) + EPS) for tot in tots]
    for (key, h), tot, r in zip(heads, tots, inv):
        bi, d = key
        vs = slice(h * DV_B, (h + 1) * DV_B)
        ogate = rest[d][bi, :, OFF_BO + h * DV_B:OFF_BO + (h + 1) * DV_B]
        yb_ref[bi, rows[d], vs] = (tot * r * norm_ref[...] * ogate).astype(BF16)

    if emit_state:
        @pl.when(n == n_chunks - 1)
        def _():
            for bi, d in dirs:
                for h in range(H_B):
                    cout_ref[bi, d, h] = c_scr[bi, d, :, h * CAUG_W:h * CAUG_W + DV_B]
                    nout_ref[bi, d, h] = c_scr[bi, d, :, h * CAUG_W + DV_B:(h + 1) * CAUG_W]
            mout_ref[...] = m_scr[...]


def _mlstm(rest, gates, norm_b, c0, m0, batch, n_chunks, emit_state):
    t = n_chunks * CHUNK
    has_init = c0 is not None
    bb = SCAN_BB
    rest, gates = (a.reshape(batch, t, a.shape[-1]) for a in (rest, gates))
    mspec = pl.BlockSpec((bb, N_DIR, SUBLANES, LANES), lambda b, n: (b, 0, 0, 0))
    in_specs = [*_scan_specs(bb, n_chunks, REST_W), *_scan_specs(bb, n_chunks, LANES),
                pl.BlockSpec((1, DV_B), lambda b, n: (0, 0))]
    args = [rest, rest, gates, gates, norm_b]
    if has_init:
        in_specs += [pl.BlockSpec((bb, N_DIR, DK_B, MW), lambda b, n: (b, 0, 0, 0)), mspec]
        args += [c0, m0]
    out_specs = [pl.BlockSpec((bb, t, B_W), lambda b, n: (b, 0, 0))]
    out_shape = [jax.ShapeDtypeStruct((batch, t, B_W), BF16)]
    if emit_state:
        out_specs += [pl.BlockSpec((bb, N_DIR, H_B, DK_B, DV_B), lambda b, n: (b, 0, 0, 0, 0)),
                      pl.BlockSpec((bb, N_DIR, H_B, DK_B, LANES), lambda b, n: (b, 0, 0, 0, 0)), mspec]
        out_shape += [jax.ShapeDtypeStruct((batch, N_DIR, H_B, DK_B, DV_B), F32),
                      jax.ShapeDtypeStruct((batch, N_DIR, H_B, DK_B, LANES), F32),
                      jax.ShapeDtypeStruct((batch, N_DIR, SUBLANES, LANES), F32)]
    out = pl.pallas_call(
        functools.partial(_mlstm_body, n_chunks=n_chunks, bb=bb, has_init=has_init, emit_state=emit_state),
        grid=(batch // bb, n_chunks),
        in_specs=in_specs,
        out_specs=out_specs,
        out_shape=out_shape,
        scratch_shapes=[pltpu.VMEM((bb, N_DIR, DK_B, MW), F32),
                        pltpu.VMEM((bb, N_DIR, SUBLANES, LANES), F32),
                        pltpu.VMEM((bb, t, B_W), F32),
                        pltpu.VMEM((bb * N_DIR, H_B * CHUNK, H_B * DK_B), BF16),
                        pltpu.VMEM((bb * N_DIR, H_B * DK_B, MW), BF16),
                        pltpu.VMEM((bb * N_DIR, H_B * CHUNK, MW), BF16)],
        compiler_params=_params(("arbitrary", "arbitrary")),
        name="mlstm_scan",
    )(*args)
    return [out[0].reshape(batch * t, B_W), *out[1:]]


OUT_TM = 512
OUT_RB = 256
FFN_TM = 512
FFN_FC = 1024


def _outproj_body(ya_ref, yb_ref, wa_ref, wb_ref, x_ref, mod_ref, post1_ref, pre2_ref, x1_ref, h2_ref):
    for rb in range(x_ref.shape[0] // OUT_RB):
        rs = slice(rb * OUT_RB, (rb + 1) * OUT_RB)
        mix = (jnp.dot(ya_ref[rs, :], wa_ref[...], preferred_element_type=F32)
               + jnp.dot(yb_ref[rs, :], wb_ref[...], preferred_element_type=F32))
        x1 = x_ref[rs, :] + mod_ref[0, 2:3, :] * (_rms(mix) * post1_ref[...])
        x1_ref[rs, :] = x1
        h2 = _rms(x1) * pre2_ref[...] * (1.0 + mod_ref[0, 4:5, :]) + mod_ref[0, 3:4, :]
        h2_ref[rs, :] = h2.astype(BF16)


def _outproj(ya, yb, w_out, x2d, mod3, mod_map, post1, pre2, tm):
    m = x2d.shape[0]
    row = lambda i: (i, 0)
    const = lambda i: (0, 0)
    return pl.pallas_call(
        _outproj_body,
        grid=(m // tm,),
        in_specs=[pl.BlockSpec((tm, A_W), row), pl.BlockSpec((tm, B_W), row),
                  pl.BlockSpec((A_W, D_MODEL), lambda i: (0, 0)),
                  pl.BlockSpec((B_W, D_MODEL), lambda i: (1, 0)),
                  pl.BlockSpec((tm, D_MODEL), row),
                  pl.BlockSpec((1, 6, D_MODEL), lambda i: (mod_map(i * tm), 0, 0)),
                  pl.BlockSpec((1, D_MODEL), const), pl.BlockSpec((1, D_MODEL), const)],
        out_specs=[pl.BlockSpec((tm, D_MODEL), row), pl.BlockSpec((tm, D_MODEL), row)],
        out_shape=[jax.ShapeDtypeStruct((m, D_MODEL), F32), jax.ShapeDtypeStruct((m, D_MODEL), BF16)],
        compiler_params=_params(("parallel",)),
        name="outproj",
    )(ya, yb, w_out, w_out, x2d, mod3, post1, pre2)


def _ffn_body(h2_ref, w1_ref, w2_ref, x1_ref, mod_ref, post2_ref, o_ref):
    kk = pl.program_id(1)

    @pl.when(kk == 0)
    def _():
        o_ref[...] = jnp.zeros_like(o_ref)

    a = jnp.maximum(jnp.dot(h2_ref[...], w1_ref[...], preferred_element_type=F32), 0.0)
    o_ref[...] += jnp.dot((a * a).astype(BF16), w2_ref[...], preferred_element_type=F32)

    @pl.when(kk == pl.num_programs(1) - 1)
    def _():
        o_ref[...] = x1_ref[...] + mod_ref[0, 5:6, :] * (_rms(o_ref[...]) * post2_ref[...])


def _ffn(h2, w1, w2, x1, mod3, mod_map, post2, tm, fc):
    m = h2.shape[0]
    return pl.pallas_call(
        _ffn_body,
        grid=(m // tm, FFN // fc),
        in_specs=[pl.BlockSpec((tm, D_MODEL), lambda i, k: (i, 0)),
                  pl.BlockSpec((D_MODEL, fc), lambda i, k: (0, k)),
                  pl.BlockSpec((fc, D_MODEL), lambda i, k: (k, 0)),
                  pl.BlockSpec((tm, D_MODEL), lambda i, k: (i, 0)),
                  pl.BlockSpec((1, 6, D_MODEL), lambda i, k: (mod_map(i * tm), 0, 0)),
                  pl.BlockSpec((1, D_MODEL), lambda i, k: (0, 0))],
        out_specs=pl.BlockSpec((tm, D_MODEL), lambda i, k: (i, 0)),
        out_shape=jax.ShapeDtypeStruct((m, D_MODEL), F32),
        compiler_params=_params(("parallel", "arbitrary")),
        name="ffn",
    )(h2, w1, w2, x1, mod3, post2)


def _block(x, mod3, mod_of_row, lp, init, seq_len, emit_state):
    bsz, t, _ = x.shape
    x2d = x.reshape(bsz * t, D_MODEL)
    n_chunks = t // CHUNK
    qkv, rest, gates = _proj(x2d, mod3, lambda i: mod_of_row(i * PROJ_TM), lp["pre1"], lp["w16"],
                             lp["w_mid"], lp["w_gate"], lp["conv_w"], lp["gate_p"], seq_len)
    s0, c0, m0 = init if init is not None else (None, None, None)
    d_out = _delta(qkv, rest, gates, lp["norm_a"], s0, bsz, n_chunks, emit_state)
    m_out = _mlstm(rest, gates, lp["norm_b"], c0, m0, bsz, n_chunks, emit_state)
    x1, h2 = _outproj(d_out[0], m_out[0], lp["w_out"], x2d, mod3, mod_of_row, lp["post1"], lp["pre2"], OUT_TM)
    y = _ffn(h2, lp["w1"], lp["w2"], x1, mod3, mod_of_row, lp["post2"], FFN_TM, FFN_FC)
    states = None
    if emit_state:
        m_fin = jnp.stack([m_out[3][:, d, 0, LF_OFF + d * H_B:LF_OFF + (d + 1) * H_B] for d in range(N_DIR)], axis=1)
        states = (d_out[1], m_out[1], m_out[2][..., 0], m_fin)
    return y.reshape(bsz, t, D_MODEL), states


def _layer_params(l, norm_mix_pre, norm_mix_post, norm_ffn_pre, norm_ffn_post, w_in, conv_w, a_log, dt_bias,
                  norm_a, mlstm_ibias, mlstm_fbias, norm_b, w_out, w_ffn1, w_ffn2):
    w = w_in[l]
    o_ag = QKV_W
    o_aa = o_ag + A_W
    o_ab = o_aa + N_DIR * H_A
    o_bq = o_ab + N_DIR * H_A
    o_bi = o_bq + 2 * H_B * DK_B + 2 * B_W
    o_bf = o_bi + N_DIR * H_B
    n_gate = 2 * N_DIR * H_A + 2 * N_DIR * H_B
    w16 = w.astype(BF16)
    w_gate = jnp.concatenate([w16[:, o_aa:o_bq], w16[:, o_bi:o_bf + N_DIR * H_B],
                              jnp.zeros((D_MODEL, LANES - n_gate), BF16)], axis=1)

    def lane_row(vals, off):
        return jnp.zeros((LANES,), F32).at[off:off + vals.size].set(vals.reshape(-1))

    gate_p = jnp.stack([lane_row(a_log[l], G_OFF), lane_row(dt_bias[l], G_OFF),
                        lane_row(mlstm_ibias[l], LI_OFF) + lane_row(mlstm_fbias[l], LF_OFF)]
                       + [jnp.zeros((LANES,), F32)] * (SUBLANES - 3))
    row = lambda v: v[l].reshape(1, -1)
    return dict(
        pre1=row(norm_mix_pre), post1=row(norm_mix_post), pre2=row(norm_ffn_pre), post2=row(norm_ffn_post),
        w16=w16, w_mid=w16[:, o_bq:o_bi], w_gate=w_gate, gate_p=gate_p,
        conv_w=jnp.concatenate([conv_w[l].T, jnp.zeros((SUBLANES - 3, QKV_W), F32)], axis=0),
        norm_a=row(norm_a), norm_b=row(norm_b),
        w_out=w_out[l].astype(BF16), w1=w_ffn1[l].astype(BF16), w2=w_ffn2[l].astype(BF16))


def kernel(x_prompt, x_sample, state_delta, state_mlstm_C, state_mlstm_n, state_mlstm_m, c, c_ctx, w_ada, b_ada, norm_mix_pre, norm_mix_post, norm_ffn_pre, norm_ffn_post, w_in, conv_w, a_log, dt_bias, norm_a, mlstm_ibias, mlstm_fbias, norm_b, w_out, w_ffn1, w_ffn2):
    depth = w_in.shape[0]
    n_lat = x_sample.shape[0]
    t_lat = x_sample.shape[1]
    cond = jnp.concatenate([c_ctx[None, :], c, jnp.zeros((SUBLANES - 1 - n_lat, D_MODEL), F32)], axis=0)
    y_prompt, y_sample = x_prompt, x_sample
    acc = ([], [], [], [])
    for l in range(depth):
        lp = _layer_params(l, norm_mix_pre, norm_mix_post, norm_ffn_pre, norm_ffn_post, w_in, conv_w, a_log,
                           dt_bias, norm_a, mlstm_ibias, mlstm_fbias, norm_b, w_out, w_ffn1, w_ffn2)
        mod = _ada(cond, w_ada, b_ada[l].reshape(1, -1), l)
        mod3 = mod[:1 + n_lat].reshape(1 + n_lat, 6, D_MODEL)
        y_prompt, st = _block(y_prompt, mod3, lambda r: 0, lp, None, x_prompt.shape[1], True)
        for a, s in zip(acc, st):
            a.append(s)
        n_rep = jnp.broadcast_to(state_mlstm_n[:, l][..., None], state_mlstm_n[:, l].shape + (LANES,))
        c_aug0 = jnp.concatenate([state_mlstm_C[:, l], n_rep], axis=-1)
        c_aug0 = c_aug0.transpose(0, 1, 3, 2, 4).reshape(n_lat, N_DIR, DK_B, MW)
        m0 = jnp.zeros((n_lat, N_DIR, LANES), F32)
        for d in range(N_DIR):
            m0 = m0.at[:, d, LF_OFF + d * H_B:LF_OFF + (d + 1) * H_B].set(state_mlstm_m[:, l, d])
        m0 = jnp.broadcast_to(m0[:, :, None, :], (n_lat, N_DIR, SUBLANES, LANES))
        y_sample, _ = _block(y_sample, mod3, lambda r: 1 + r // t_lat, lp, (state_delta[:, l], c_aug0, m0),
                             GRID_W, False)
    return (y_prompt, y_sample) + tuple(jnp.stack(a, axis=1) for a in acc)
```

```python
import functools

import jax
import jax.numpy as jnp
from jax import lax
from jax.experimental import pallas as pl
from jax.experimental.pallas import tpu as pltpu

F32 = jnp.float32
BF16 = jnp.bfloat16

D_MODEL = 2048
N_DIR = 2
A_W = D_MODEL // 2
B_W = D_MODEL - A_W
DK_A = 128
DV_A = 128
H_A = A_W // DV_A
DV_B = 256
DK_B = DV_B // 2
H_B = B_W // DV_B
GRID_W = 64
CHUNK = 64
FFN = 4 * D_MODEL
EPS = 1e-6
LANES = 128
SUBLANES = 8
NEG = -1e30

QKV_W = 3 * A_W
REST_W = A_W + 2 * H_B * DK_B + 2 * B_W
OFF_AG, OFF_BQ, OFF_BK, OFF_BV, OFF_BO = 0, A_W, A_W + H_B * DK_B, A_W + 2 * H_B * DK_B, A_W + 2 * H_B * DK_B + B_W
G_OFF, BETA_OFF = 0, N_DIR * H_A
LI_OFF = 2 * N_DIR * H_A
LF_OFF = LI_OFF + N_DIR * H_B

VMEM_LIMIT = 56 * 1024 * 1024


def _sigmoid(x):
    return 1.0 / (1.0 + jnp.exp(-x))


def _softplus(x):
    return jnp.maximum(x, 0.0) + jnp.log1p(jnp.exp(-jnp.abs(x)))


def _dot(a, b):
    return jnp.dot(a.astype(BF16), b.astype(BF16), preferred_element_type=F32)


def _dot_f32(a, b):
    return jnp.dot(a, b, precision=lax.Precision.HIGHEST, preferred_element_type=F32)


def _rms(x):
    return x * lax.rsqrt(jnp.mean(x * x, axis=-1, keepdims=True) + EPS)


def _params(sem):
    return pltpu.CompilerParams(dimension_semantics=sem, vmem_limit_bytes=VMEM_LIMIT)


def _ada_body(c_ref, w_ref, b_ref, o_ref):
    c = c_ref[...]
    o_ref[...] = _dot(c * _sigmoid(c), w_ref[...]) + b_ref[...]


def _ada(c_all, w_ada, b, l):
    n = w_ada.shape[2]
    tn = 1024
    return pl.pallas_call(
        _ada_body,
        grid=(n // tn,),
        in_specs=[pl.BlockSpec(c_all.shape, lambda j: (0, 0)),
                  pl.BlockSpec((None, D_MODEL, tn), lambda j: (l, 0, j)),
                  pl.BlockSpec((1, tn), lambda j: (0, j))],
        out_specs=pl.BlockSpec((c_all.shape[0], tn), lambda j: (0, j)),
        out_shape=jax.ShapeDtypeStruct((c_all.shape[0], n), F32),
        compiler_params=_params(("arbitrary",)),
        name="ada",
    )(c_all, w_ada, b)


PROJ_TM = 1024
PROJ_RB = 512
PROJ_CB = 256
PROJ_TN = 512
N_QKV_T = QKV_W // PROJ_TN
N_REST_T = REST_W // PROJ_TN
N_HEAD_T = (QKV_W + A_W) // PROJ_TN
N_MID_T = N_QKV_T + N_REST_T - N_HEAD_T


def _proj_body(x_ref, mod_ref, g_ref, wh_ref, wb_ref, wg_ref, cw_ref, gp_ref, qkv_ref, rest_ref, gate_ref, h_scr, *,
               seq_len):
    j = pl.program_id(1)
    row_blocks = [slice(rb * PROJ_RB, (rb + 1) * PROJ_RB) for rb in range(PROJ_TM // PROJ_RB)]
    col_blocks = [slice(cb * PROJ_CB, (cb + 1) * PROJ_CB) for cb in range(PROJ_TN // PROJ_CB)]

    def run_rows(rs, w_ref, epilogue, out_ref):
        for cs in col_blocks:
            epilogue(jnp.dot(h_scr[rs, :], w_ref[:, cs], preferred_element_type=F32), out_ref, rs, cs)

    def run(w_ref, epilogue, out_ref):
        for rs in row_blocks:
            run_rows(rs, w_ref, epilogue, out_ref)

    def run_first():
        for rs in row_blocks:
            y = _rms(x_ref[rs, :]) * g_ref[...]
            h_scr[rs, :] = (y * (1.0 + mod_ref[0, 1:2, :]) + mod_ref[0, 0:1, :]).astype(BF16)
            run_rows(rs, wh_ref, ep_l2(DK_A ** -0.5), qkv_ref)
            ep_gate(jnp.dot(h_scr[rs, :], wg_ref[...], preferred_element_type=F32), gate_ref, rs)

    def conv_silu(acc, cs, g):
        a = acc[:, g * LANES:(g + 1) * LANES]
        ls = slice(cs.start + g * LANES, cs.start + (g + 1) * LANES)
        pos = lax.broadcasted_iota(jnp.int32, a.shape, 0) & (seq_len - 1)
        prev = jnp.where(pos == 0, 0.0, pltpu.roll(a, 1, 0))
        nxt = jnp.where(pos == seq_len - 1, 0.0, pltpu.roll(a, a.shape[0] - 1, 0))
        y = prev * cw_ref[0:1, ls] + a * cw_ref[1:2, ls] + nxt * cw_ref[2:3, ls]
        return ls, y * _sigmoid(y)

    def ep_l2(scale):
        def f(acc, out_ref, rs, cs):
            for g in range(PROJ_CB // LANES):
                ls, blk = conv_silu(acc, cs, g)
                inv = lax.rsqrt(jnp.sum(blk * blk, axis=-1, keepdims=True) + EPS)
                out_ref[rs, ls] = blk * (inv * scale)
        return f

    def ep_conv(acc, out_ref, rs, cs):
        for g in range(PROJ_CB // LANES):
            ls, blk = conv_silu(acc, cs, g)
            out_ref[rs, ls] = blk

    def ep_map(fn):
        def f(acc, out_ref, rs, cs):
            out_ref[rs, cs] = fn(acc)
        return f

    def ep_gate(z, out_ref, rs):
        lane = lax.broadcasted_iota(jnp.int32, z.shape, 1)
        g = -jnp.exp(gp_ref[0:1, :]) * _softplus(z + gp_ref[1:2, :])
        li = z + gp_ref[2:3, :]
        out_ref[rs, :] = jnp.where(lane < BETA_OFF, g,
                                   jnp.where(lane < LI_OFF, _sigmoid(z),
                                             jnp.where(lane < LF_OFF, li,
                                                       jnp.where(lane < LF_OFF + N_DIR * H_B, -_softplus(-li), 0.0))))

    tp = A_W // PROJ_TN
    c0 = (j - N_QKV_T) * PROJ_TN
    in_rest = (j >= N_QKV_T) & (j < N_QKV_T + N_REST_T)
    pl.when(j == 0)(run_first)
    variants = [
        ((j > 0) & (j < tp), wh_ref, ep_l2(DK_A ** -0.5), qkv_ref),
        ((j >= tp) & (j < 2 * tp), wh_ref, ep_l2(1.0), qkv_ref),
        ((j >= 2 * tp) & (j < N_QKV_T), wh_ref, ep_conv, qkv_ref),
        (in_rest & (c0 < OFF_BQ), wh_ref, ep_map(lambda a: a * _sigmoid(a)), rest_ref),
        (in_rest & (c0 >= OFF_BQ) & (c0 < OFF_BK), wb_ref, ep_map(lambda a: a * (DK_B ** -0.5)), rest_ref),
        (in_rest & (c0 >= OFF_BK) & (c0 < OFF_BO), wb_ref, ep_map(lambda a: a), rest_ref),
        (in_rest & (c0 >= OFF_BO), wb_ref, ep_map(_sigmoid), rest_ref),
    ]
    for cond, w_ref, epilogue, out_ref in variants:
        pl.when(cond)(functools.partial(run, w_ref, epilogue, out_ref))


def _proj(x2d, mod3, mod_map, pre_g, w16, w_mid, w_gate, conv_w, gate_p, seq_len):
    m = x2d.shape[0]
    tm, tn = PROJ_TM, PROJ_TN
    assert PROJ_RB % seq_len == 0 and m % tm == 0
    return pl.pallas_call(
        functools.partial(_proj_body, seq_len=seq_len),
        grid=(m // tm, N_HEAD_T + N_MID_T),
        in_specs=[pl.BlockSpec((tm, D_MODEL), lambda i, j: (i, 0)),
                  pl.BlockSpec((1, 6, D_MODEL), lambda i, j: (mod_map(i), 0, 0)),
                  pl.BlockSpec((1, D_MODEL), lambda i, j: (0, 0)),
                  pl.BlockSpec((D_MODEL, tn), lambda i, j: (0, jnp.minimum(j, N_HEAD_T - 1))),
                  pl.BlockSpec((D_MODEL, tn), lambda i, j: (0, jnp.clip(j - N_HEAD_T, 0, N_MID_T - 1))),
                  pl.BlockSpec((D_MODEL, LANES), lambda i, j: (0, 0)),
                  pl.BlockSpec((SUBLANES, tn), lambda i, j: (0, jnp.minimum(j, N_QKV_T - 1))),
                  pl.BlockSpec((SUBLANES, LANES), lambda i, j: (0, 0))],
        out_specs=[pl.BlockSpec((tm, tn), lambda i, j: (i, jnp.minimum(j, N_QKV_T - 1))),
                   pl.BlockSpec((tm, tn), lambda i, j: (i, jnp.clip(j - N_QKV_T, 0, N_REST_T - 1))),
                   pl.BlockSpec((tm, LANES), lambda i, j: (i, 0))],
        out_shape=[jax.ShapeDtypeStruct((m, QKV_W), F32), jax.ShapeDtypeStruct((m, REST_W), F32),
                   jax.ShapeDtypeStruct((m, LANES), F32)],
        scratch_shapes=[pltpu.VMEM((tm, D_MODEL), BF16)],
        compiler_params=_params(("parallel", "arbitrary")),
        name="proj",
    )(x2d, mod3, pre_g, w16, w_mid, w_gate, conv_w, gate_p)


def _masks(d):
    row = lax.broadcasted_iota(jnp.int32, (CHUNK, CHUNK), 0)
    col = lax.broadcasted_iota(jnp.int32, (CHUNK, CHUNK), 1)
    if d == 0:
        return row >= col, row > col, row == col
    return row <= col, row < col, row == col


GROUP = 4
N_GROUPS = H_A // GROUP
PAIR = LANES // CHUNK


def _split(x):
    hi = x.astype(BF16)
    return hi, (x - hi.astype(F32)).astype(BF16)


def _store_blocks(ref, x, blk_r, blk_c, col0=0):
    for h in range(GROUP):
        ref[h * blk_r:(h + 1) * blk_r, col0 + h * blk_c:col0 + (h + 1) * blk_c] = x[:, h * blk_c:(h + 1) * blk_c]


def _dot3(a_hi, a_lo, b_hi, b_lo):
    m = a_hi.shape[0]
    r = jnp.dot(jnp.concatenate([a_hi, a_lo], axis=0), b_hi, preferred_element_type=F32)
    return r[:m] + r[m:] + jnp.dot(a_hi, b_lo, preferred_element_type=F32)


HALF = CHUNK // 2


def _tri_inverse_wide(lmats, upper, bd_hi, bd_lo, by_hi, by_lo):
    width = lmats[0].shape[1]
    n_blk = width // HALF
    row = lax.broadcasted_iota(jnp.int32, (HALF, width), 0)
    lane = lax.broadcasted_iota(jnp.int32, (HALF, width), 1)
    lead = (lane & (CHUNK - 1)) < HALF
    eye_d = (row == (lane & (HALF - 1))).astype(F32)

    def bd_dot(i, a, b_hi, b_lo):
        for blk in range(n_blk):
            sl = slice(blk * HALF, (blk + 1) * HALF)
            bd_hi[i, sl, sl] = b_hi[:, sl]
            bd_lo[i, sl, sl] = b_lo[:, sl]
        return _dot3(*a, bd_hi[i], bd_lo[i])

    idx = range(len(lmats))
    diag = [jnp.where(lead, l[:HALF], l[HALF:]) for l in lmats]
    s = [eye_d - dg for dg in diag]
    p = []
    for i in idx:
        m_hi, m_lo = _split(-diag[i])
        p.append(bd_dot(i, (m_hi, m_lo), m_hi, m_lo))
    for _ in range(3):
        for i in idx:
            p_hi, p_lo = _split(p[i])
            s_hi, s_lo = _split(s[i])
            r = bd_dot(i, (jnp.concatenate([p_hi, s_hi], axis=0), jnp.concatenate([p_lo, s_lo], axis=0)), p_hi, p_lo)
            p[i] = r[:HALF]
            s[i] = s[i] + r[HALF:]
    for i in idx:
        p_hi, p_lo = _split(p[i])
        s[i] = s[i] + bd_dot(i, _split(s[i]), p_hi, p_lo)
    y = []
    for i in idx:
        c_blk = jnp.where(lead, 0.0, lmats[i][:HALF]) if upper[i] else jnp.where(lead, lmats[i][HALF:], 0.0)
        y.append(bd_dot(i, _split(c_blk), *_split(s[i])))
    out = []
    for i in idx:
        y_hi, y_lo = _split(y[i])
        for h in range(width // CHUNK):
            a_sl, b_sl = slice(h * CHUNK, h * CHUNK + HALF), slice(h * CHUNK + HALF, (h + 1) * CHUNK)
            rs, cs = (a_sl, b_sl) if upper[i] else (b_sl, a_sl)
            by_hi[i, rs, cs] = y_hi[:, cs]
            by_lo[i, rs, cs] = y_lo[:, cs]
        ai, bi = jnp.where(lead, s[i], 0.0), jnp.where(lead, 0.0, s[i])
        x = _dot3(*_split(ai if upper[i] else bi), by_hi[i], by_lo[i])
        out.append(jnp.concatenate([ai - x, bi] if upper[i] else [ai, bi - x], axis=0))
    return out


def _col_bcast(tile, c, width=LANES):
    return jnp.broadcast_to(tile[:, c:c + 1], (tile.shape[0], width))


def _delta_body(*refs, n_chunks, bb, has_init, emit_state, n_cast):
    qkv = refs[0:2]
    ag = refs[2:4]
    gt = refs[4:6]
    norm_ref = refs[6]
    pos = 7
    s0_ref = None
    if has_init:
        s0_ref = refs[pos]
        pos += 1
    cast_in = refs[pos:pos + n_cast]
    pos += n_cast
    ya_ref = refs[pos]
    pos += 1
    sout_ref = None
    if emit_state:
        sout_ref = refs[pos]
        pos += 1
    for w_ref, w16_ref in zip(cast_in, refs[pos:pos + n_cast]):
        w16_ref[...] = w_ref[...].astype(BF16)
    pos += n_cast
    s_scr, o_scr, bdn_hi, bdn_lo, bdy_hi, bdy_lo, bdk, bduw, bds, bdv = refs[pos:pos + 10]

    n = pl.program_id(1)
    gw = GROUP * DK_A

    @pl.when((n == 0) & (pl.program_id(0) == 0))
    def _():
        for ref in (bdn_hi, bdn_lo, bdy_hi, bdy_lo, bdk, bduw, bds, bdv):
            ref[...] = jnp.zeros_like(ref)

    @pl.when(n == 0)
    def _():
        o_scr[...] = jnp.zeros_like(o_scr)
        for bi in range(bb):
            for d in range(N_DIR):
                for h in range(H_A):
                    blk = s0_ref[bi, d, h] if has_init else jnp.zeros((DK_A, DV_A), F32)
                    s_scr[bi, d, h // GROUP, :, (h % GROUP) * DV_A:(h % GROUP + 1) * DV_A] = blk

    row = lax.broadcasted_iota(jnp.int32, (CHUNK, LANES), 0)
    lane = lax.broadcasted_iota(jnp.int32, (CHUNK, LANES), 1)
    col = lane & (CHUNK - 1)
    left = lane < CHUNK
    probs = [(bi, d, g) for bi in range(bb) for d in range(N_DIR) for g in range(N_GROUPS)]
    masks = {0: (row >= col, row > col), 1: (row <= col, row < col)}
    gates, gc, gc_t, eg, egl = {}, {}, {}, {}, {}
    for bi in range(bb):
        for d in range(N_DIR):
            key = (bi, d)
            gates[key] = gt[d][bi]
            gc[key] = _dot_f32(_masks(d)[0].astype(F32), gates[key])
            gc_t[key] = jnp.concatenate([gc[key], gc[key]], axis=0).T
            last = CHUNK - 1 if d == 0 else 0
            eg[key] = jnp.exp(gc[key])
            egl[key] = jnp.exp(gc[key][last:last + 1, :] - gc[key])

    def cols_of(d, g):
        return [G_OFF + d * H_A + g * GROUP + hl for hl in range(GROUP)]

    q16, ks, beta_xs, decays, grams = [], [], [], [], []
    for gi, (bi, d, g) in enumerate(probs):
        key = (bi, d)
        cols = cols_of(d, g)
        incl, _ = masks[d]
        k = qkv[d][bi, :, A_W + g * gw:A_W + (g + 1) * gw]
        beta_x = jnp.concatenate([_col_bcast(gates[key], BETA_OFF - G_OFF + c) for c in cols], axis=1)
        decay = []
        for p in range(GROUP // PAIR):
            c0, c1 = cols[PAIR * p], cols[PAIR * p + 1]
            gcol = jnp.where(left, _col_bcast(gc[key], c0), _col_bcast(gc[key], c1))
            grow = jnp.where(left[0:1], gc_t[key][c0:c0 + 1, :], gc_t[key][c1:c1 + 1, :])
            decay.append(jnp.exp(jnp.where(incl, gcol - grow, NEG)))
        decays.append(jnp.concatenate(decay, axis=1))
        q16.append(qkv[d][bi, :, g * gw:(g + 1) * gw].astype(BF16))
        ks.append(k)
        beta_xs.append(beta_x)
        _store_blocks(bdk.at[gi], k.astype(BF16), CHUNK, DK_A)
    for gi in range(len(probs)):
        grams.append(lax.dot_general(jnp.concatenate([q16[gi], (ks[gi] * beta_xs[gi]).astype(BF16)], axis=0),
                                     bdk[gi], (((1,), (1,)), ((), ())), preferred_element_type=F32))
    attns, lmats = [], []
    for gi, (bi, d, g) in enumerate(probs):
        strict_w = jnp.concatenate([masks[d][1]] * (GROUP // PAIR), axis=1)
        attns.append((grams[gi][:CHUNK] * decays[gi]).astype(BF16))
        lmats.append(jnp.where(strict_w, grams[gi][CHUNK:] * decays[gi], 0.0))
    ainvs = _tri_inverse_wide(lmats, [d == 1 for _, d, _ in probs], bdn_hi, bdn_lo, bdy_hi, bdy_lo)
    eg_xs, kd_ts = [], []
    for gi, (bi, d, g) in enumerate(probs):
        key = (bi, d)
        cols = cols_of(d, g)
        v = qkv[d][bi, :, 2 * A_W + g * gw:2 * A_W + (g + 1) * gw]
        eg_x = jnp.concatenate([_col_bcast(eg[key], c) for c in cols], axis=1)
        egl_x = jnp.concatenate([_col_bcast(egl[key], c) for c in cols], axis=1)
        eg_xs.append(eg_x)
        _store_blocks(bduw.at[gi], (v * beta_xs[gi]).astype(BF16), CHUNK, DV_A)
        _store_blocks(bduw.at[gi], (ks[gi] * (beta_xs[gi] * eg_x)).astype(BF16), CHUNK, DK_A, col0=gw)
        kd = ks[gi] * egl_x
        kd_ts.append(jnp.concatenate([kd[:, hl * DK_A:(hl + 1) * DK_A] for hl in range(GROUP)], axis=0)
                     .T.astype(BF16))
        _store_blocks(bds.at[gi], s_scr[bi, d, g].astype(BF16), DK_A, DV_A)
    uws = []
    for gi in range(len(probs)):
        t_hi, t_lo = _split(ainvs[gi])
        r = jnp.dot(jnp.concatenate([t_hi, t_lo], axis=0), bduw[gi], preferred_element_type=F32)
        uws.append(r[:CHUNK] + r[CHUNK:])
    ws_qs = []
    hw = gw // 2
    for gi in range(len(probs)):
        wq = jnp.concatenate([uws[gi][:, gw:].astype(BF16), q16[gi]], axis=0)
        ws_qs.append(jnp.concatenate(
            [jnp.dot(wq[:, p * hw:(p + 1) * hw], bds[gi, p * hw:(p + 1) * hw, p * hw:(p + 1) * hw],
                     preferred_element_type=F32) for p in range(2)], axis=1))
    for gi in range(len(probs)):
        _store_blocks(bdv.at[gi], (uws[gi][:, :gw] - ws_qs[gi][:CHUNK]).astype(BF16), CHUNK, DV_A)
    rs = []
    for gi in range(len(probs)):
        rs.append(jnp.dot(jnp.concatenate([attns[gi], kd_ts[gi]], axis=0), bdv[gi], preferred_element_type=F32))
    tots = []
    for gi, (bi, d, g) in enumerate(probs):
        cols = cols_of(d, g)
        last = CHUNK - 1 if d == 0 else 0
        o = ws_qs[gi][CHUNK:] * eg_xs[gi] + rs[gi][:CHUNK]
        eg_last = jnp.concatenate(
            [jnp.broadcast_to(eg[bi, d][last:last + 1, c:c + 1], (1, DV_A)) for c in cols], axis=1)
        s_scr[bi, d, g] = s_scr[bi, d, g] * eg_last + rs[gi][CHUNK:]
        cidx = n if d == 0 else n_chunks - 1 - n
        rows = pl.ds(pl.multiple_of(cidx * CHUNK, CHUNK), CHUNK)
        tot = o + o_scr[bi, rows, g * gw:(g + 1) * gw]
        o_scr[bi, rows, g * gw:(g + 1) * gw] = tot
        tots.append((rows, tot))
    blks = [(bi, d, rows, g * GROUP + hl, tot[:, hl * DV_A:(hl + 1) * DV_A])
            for (bi, d, g), (rows, tot) in zip(probs, tots) for hl in range(GROUP)]
    inv = [lax.rsqrt(jnp.mean(blk * blk, axis=-1, keepdims=True) + EPS) for *_, blk in blks]
    for (bi, d, rows, h, blk), r in zip(blks, inv):
        hs = slice(h * DV_A, (h + 1) * DV_A)
        ya_ref[bi, rows, hs] = (blk * r * norm_ref[...] * ag[d][bi, :, hs]).astype(BF16)

    if emit_state:
        @pl.when(n == n_chunks - 1)
        def _():
            for bi in range(bb):
                for d in range(N_DIR):
                    for h in range(H_A):
                        sout_ref[bi, d, h] = s_scr[bi, d, h // GROUP, :, (h % GROUP) * DV_A:(h % GROUP + 1) * DV_A]


SCAN_BB = 2


def _scan_specs(bb, n_chunks, width):
    return (pl.BlockSpec((bb, CHUNK, width), lambda b, n: (b, n, 0)),
            pl.BlockSpec((bb, CHUNK, width), lambda b, n: (b, n_chunks - 1 - n, 0)))


def _cast_specs(casts, steps_of):
    in_specs, out_specs, out_shape, args = [], [], [], []
    for w, l, steps in casts:
        rows = w.shape[1] // steps
        assert rows * steps == w.shape[1] and rows % (2 * SUBLANES) == 0
        in_specs.append(pl.BlockSpec((None, rows, w.shape[2]), lambda b, n, l=l: (l, steps_of(b, n), 0)))
        out_specs.append(pl.BlockSpec((rows, w.shape[2]), lambda b, n: (steps_of(b, n), 0)))
        out_shape.append(jax.ShapeDtypeStruct(w.shape[1:], BF16))
        args.append(w)
    return in_specs, out_specs, out_shape, args


def _delta(qkv, rest, gates, norm_a, s0, batch, n_chunks, emit_state, casts=()):
    t = n_chunks * CHUNK
    has_init = s0 is not None
    bb = SCAN_BB
    ng = bb * N_DIR * N_GROUPS
    qkv, rest, gates = (a.reshape(batch, t, a.shape[-1]) for a in (qkv, rest, gates))
    in_specs = [*_scan_specs(bb, n_chunks, QKV_W), *_scan_specs(bb, n_chunks, A_W), *_scan_specs(bb, n_chunks, LANES),
                pl.BlockSpec((1, DV_A), lambda b, n: (0, 0))]
    args = [qkv, qkv, rest, rest, gates, gates, norm_a]
    sspec = pl.BlockSpec((bb, N_DIR, H_A, DK_A, DV_A), lambda b, n: (b, 0, 0, 0, 0))
    if has_init:
        in_specs.append(sspec)
        args.append(s0)
    out_specs = [pl.BlockSpec((bb, t, A_W), lambda b, n: (b, 0, 0))]
    out_shape = [jax.ShapeDtypeStruct((batch, t, A_W), BF16)]
    if emit_state:
        out_specs.append(sspec)
        out_shape.append(jax.ShapeDtypeStruct((batch, N_DIR, H_A, DK_A, DV_A), F32))
    steps = (batch // bb) * n_chunks
    c_in, c_out, c_shape, c_args = _cast_specs([(w, l, steps) for w, l in casts], lambda b, n: b * n_chunks + n)
    in_specs, args, out_specs, out_shape = in_specs + c_in, args + c_args, out_specs + c_out, out_shape + c_shape
    out = pl.pallas_call(
        functools.partial(_delta_body, n_chunks=n_chunks, bb=bb, has_init=has_init, emit_state=emit_state,
                          n_cast=len(casts)),
        grid=(batch // bb, n_chunks),
        in_specs=in_specs,
        out_specs=out_specs,
        out_shape=out_shape,
        scratch_shapes=[pltpu.VMEM((bb, N_DIR, N_GROUPS, DK_A, GROUP * DV_A), F32), pltpu.VMEM((bb, t, A_W), F32),
                        pltpu.VMEM((ng, GROUP * CHUNK, GROUP * CHUNK), BF16),
                        pltpu.VMEM((ng, GROUP * CHUNK, GROUP * CHUNK), BF16),
                        pltpu.VMEM((ng, GROUP * CHUNK, GROUP * CHUNK), BF16),
                        pltpu.VMEM((ng, GROUP * CHUNK, GROUP * CHUNK), BF16),
                        pltpu.VMEM((ng, GROUP * CHUNK, GROUP * DK_A), BF16),
                        pltpu.VMEM((ng, GROUP * CHUNK, 2 * GROUP * DK_A), BF16),
                        pltpu.VMEM((ng, GROUP * DK_A, GROUP * DV_A), BF16),
                        pltpu.VMEM((ng, GROUP * CHUNK, GROUP * DV_A), BF16)],
        compiler_params=_params(("arbitrary", "arbitrary")),
        name="delta_scan",
    )(*args)
    return [out[0].reshape(batch * t, A_W), *out[1:]]


CAUG_W = DV_B + LANES
MW = H_B * CAUG_W


def _scan_max(x, d):
    row = lax.broadcasted_iota(jnp.int32, x.shape, 0)
    s = 1
    while s < CHUNK:
        if d == 0:
            shifted = jnp.where(row >= s, pltpu.roll(x, s, 0), NEG)
        else:
            shifted = jnp.where(row < CHUNK - s, pltpu.roll(x, CHUNK - s, 0), NEG)
        x = jnp.maximum(x, shifted)
        s *= 2
    return x


def _mlstm_body(*refs, n_chunks, bb, has_init, emit_state, n_cast):
    rest = refs[0:2]
    gt = refs[2:4]
    norm_ref = refs[4]
    pos = 5
    c0_ref = m0_ref = None
    if has_init:
        c0_ref, m0_ref = refs[pos], refs[pos + 1]
        pos += 2
    cast_in = refs[pos:pos + n_cast]
    pos += n_cast
    yb_ref = refs[pos]
    pos += 1
    cout_ref = nout_ref = mout_ref = None
    if emit_state:
        cout_ref, nout_ref, mout_ref = refs[pos:pos + 3]
        pos += 3
    for w_ref, w16_ref in zip(cast_in, refs[pos:pos + n_cast]):
        w16_ref[...] = w_ref[...].astype(BF16)
    pos += n_cast
    c_scr, m_scr, o_scr, bdk, bdc, bdv = refs[pos:pos + 6]

    n = pl.program_id(1)
    qw = H_B * DK_B

    @pl.when((n == 0) & (pl.program_id(0) == 0))
    def _():
        for ref in (bdk, bdc, bdv):
            ref[...] = jnp.zeros_like(ref)
        for i in range(bb * N_DIR):
            for h in range(H_B):
                bdv[i, h * CHUNK:(h + 1) * CHUNK, h * CAUG_W + DV_B:(h + 1) * CAUG_W] = jnp.ones((CHUNK, LANES), BF16)

    @pl.when(n == 0)
    def _():
        o_scr[...] = jnp.zeros_like(o_scr)
        if has_init:
            c_scr[...] = c0_ref[...]
            m_scr[...] = m0_ref[...]
        else:
            c_scr[...] = jnp.zeros_like(c_scr)
            m_scr[...] = jnp.zeros_like(m_scr)

    row = lax.broadcasted_iota(jnp.int32, (CHUNK, LANES), 0)
    lane = lax.broadcasted_iota(jnp.int32, (CHUNK, LANES), 1)
    col = lane & (CHUNK - 1)
    left = lane < CHUNK
    dirs = [(bi, d) for bi in range(bb) for d in range(N_DIR)]
    slot = {key: i for i, key in enumerate(dirs)}
    q16 = {}
    for key in dirs:
        bi, d = key
        i = slot[key]
        q16[key] = rest[d][bi, :, OFF_BQ:OFF_BQ + qw].astype(BF16)
        _store_blocks(bdk.at[i], rest[d][bi, :, OFF_BK:OFF_BK + qw].astype(BF16), CHUNK, DK_B)
        for h in range(H_B):
            bdv[i, h * CHUNK:(h + 1) * CHUNK, h * CAUG_W:h * CAUG_W + DV_B] = (
                rest[d][bi, :, OFF_BV + h * DV_B:OFF_BV + (h + 1) * DV_B].astype(BF16))
        _store_blocks(bdc.at[i], c_scr[bi, d].astype(BF16), DK_B, CAUG_W)
    qk = {key: lax.dot_general(q16[key], bdk[slot[key]], (((1,), (1,)), ((), ())), preferred_element_type=F32)
          for key in dirs}
    nc, a_t, iw, emt, ksc, dec_row = {}, {}, {}, {}, {}, {}
    for key in dirs:
        bi, d = key
        lo = LF_OFF + d * H_B
        mine = (lane >= lo) & (lane < lo + H_B)
        g = gt[d][bi]
        gc = jnp.where(mine, _dot_f32(_masks(d)[0].astype(F32), g), 0.0)
        a = jnp.where(mine, pltpu.roll(g, LF_OFF - LI_OFF, 1), 0.0) - gc
        last = CHUNK - 1 if d == 0 else 0
        m_old = m_scr[bi, d][0:1, :]
        mx = jnp.maximum(m_old, _scan_max(a, d))
        mxl = mx[last:last + 1, :]
        nc[key] = -mx
        a_t[key] = jnp.concatenate([a, a], axis=0).T
        iw[key] = jnp.exp(m_old - mx)
        emt[key] = jnp.exp(-(gc + mx))
        ksc[key] = jnp.exp(a - mxl)
        dec_row[key] = jnp.exp(m_old - mxl)
        m_scr[bi, d] = jnp.broadcast_to(gc[last:last + 1, :] + mxl, (SUBLANES, LANES))

    ks_t = {}
    for key in dirs:
        bi, d = key
        lo = LF_OFF + d * H_B
        ks = rest[d][bi, :, OFF_BK:OFF_BK + qw] * jnp.concatenate(
            [_col_bcast(ksc[key], lo + h) for h in range(H_B)], axis=1)
        ks_t[key] = jnp.concatenate([ks[:, h * DK_B:(h + 1) * DK_B] for h in range(H_B)], axis=0).T.astype(BF16)
    lhs = {}
    for key in dirs:
        bi, d = key
        lo = LF_OFF + d * H_B
        incl = row >= col if d == 0 else row <= col
        log_w = []
        for p in range(H_B // PAIR):
            l0, l1 = lo + PAIR * p, lo + PAIR * p + 1
            ccol = jnp.where(left, _col_bcast(nc[key], l0), _col_bcast(nc[key], l1))
            crow = jnp.where(left[0:1], a_t[key][l0:l0 + 1, :], a_t[key][l1:l1 + 1, :])
            log_w.append(jnp.where(incl, ccol + crow, NEG))
        dw = jnp.exp(jnp.concatenate(log_w, axis=1)) * qk[key]
        iw_x = jnp.concatenate([_col_bcast(iw[key], lo + h) for h in range(H_B)], axis=1)
        lhs[key] = ((rest[d][bi, :, OFF_BQ:OFF_BQ + qw] * iw_x).astype(BF16), dw.astype(BF16))
    num = {}
    for key in dirs:
        i = slot[key]
        parts = []
        for p in range(H_B // PAIR):
            ql, dl, cl = slice(p * PAIR * DK_B, (p + 1) * PAIR * DK_B), slice(p * LANES, (p + 1) * LANES), \
                slice(p * PAIR * CAUG_W, (p + 1) * PAIR * CAUG_W)
            parts.append(jnp.dot(jnp.concatenate([lhs[key][0][:, ql], lhs[key][1][:, dl]], axis=1),
                                 jnp.concatenate([bdc[i, ql, cl], bdv[i, dl, cl]], axis=0),
                                 preferred_element_type=F32))
        num[key] = jnp.concatenate(parts, axis=1)
    upd = {key: jnp.dot(ks_t[key], bdv[slot[key]], preferred_element_type=F32) for key in dirs}
    for key in dirs:
        bi, d = key
        lo = LF_OFF + d * H_B
        dec_x = jnp.concatenate(
            [jnp.broadcast_to(dec_row[key][:, lo + h:lo + h + 1], (1, CAUG_W)) for h in range(H_B)], axis=1)
        c_scr[bi, d] = c_scr[bi, d] * dec_x + upd[key]
    heads = [(key, h) for key in dirs for h in range(H_B)]
    rows = {d: pl.ds(pl.multiple_of((n if d == 0 else n_chunks - 1 - n) * CHUNK, CHUNK), CHUNK) for d in range(N_DIR)}
    tots = []
    for key, h in heads:
        bi, d = key
        vs = slice(h * DV_B, (h + 1) * DV_B)
        den = jnp.maximum(jnp.abs(num[key][:, h * CAUG_W + DV_B:(h + 1) * CAUG_W]),
                          _col_bcast(emt[key], LF_OFF + d * H_B + h))
        hb = jnp.concatenate([num[key][:, h * CAUG_W:h * CAUG_W + LANES] / den,
                              num[key][:, h * CAUG_W + LANES:h * CAUG_W + DV_B] / den], axis=1)
        tot = hb + o_scr[bi, rows[d], vs]
        o_scr[bi, rows[d], vs] = tot
        tots.append(tot)
    inv = [lax.rsqrt(jnp.mean(tot * tot, axis=-1, keepdims=True) + EPS) for tot in tots]
    for (key, h), tot, r in zip(heads, tots, inv):
        bi, d = key
        vs = slice(h * DV_B, (h + 1) * DV_B)
        ogate = rest[d][bi, :, OFF_BO + h * DV_B:OFF_BO + (h + 1) * DV_B]
        yb_ref[bi, rows[d], vs] = (tot * r * norm_ref[...] * ogate).astype(BF16)

    if emit_state:
        @pl.when(n == n_chunks - 1)
        def _():
            for bi, d in dirs:
                for h in range(H_B):
                    cout_ref[bi, d, h] = c_scr[bi, d, :, h * CAUG_W:h * CAUG_W + DV_B]
                    nout_ref[bi, d, h] = c_scr[bi, d, :, h * CAUG_W + DV_B:(h + 1) * CAUG_W]
            mout_ref[...] = m_scr[...]


def _mlstm(rest, gates, norm_b, c0, m0, batch, n_chunks, emit_state, casts=()):
    t = n_chunks * CHUNK
    has_init = c0 is not None
    bb = SCAN_BB
    rest, gates = (a.reshape(batch, t, a.shape[-1]) for a in (rest, gates))
    mspec = pl.BlockSpec((bb, N_DIR, SUBLANES, LANES), lambda b, n: (b, 0, 0, 0))
    in_specs = [*_scan_specs(bb, n_chunks, REST_W), *_scan_specs(bb, n_chunks, LANES),
                pl.BlockSpec((1, DV_B), lambda b, n: (0, 0))]
    args = [rest, rest, gates, gates, norm_b]
    if has_init:
        in_specs += [pl.BlockSpec((bb, N_DIR, DK_B, MW), lambda b, n: (b, 0, 0, 0)), mspec]
        args += [c0, m0]
    out_specs = [pl.BlockSpec((bb, t, B_W), lambda b, n: (b, 0, 0))]
    out_shape = [jax.ShapeDtypeStruct((batch, t, B_W), BF16)]
    if emit_state:
        out_specs += [pl.BlockSpec((bb, N_DIR, H_B, DK_B, DV_B), lambda b, n: (b, 0, 0, 0, 0)),
                      pl.BlockSpec((bb, N_DIR, H_B, DK_B, LANES), lambda b, n: (b, 0, 0, 0, 0)), mspec]
        out_shape += [jax.ShapeDtypeStruct((batch, N_DIR, H_B, DK_B, DV_B), F32),
                      jax.ShapeDtypeStruct((batch, N_DIR, H_B, DK_B, LANES), F32),
                      jax.ShapeDtypeStruct((batch, N_DIR, SUBLANES, LANES), F32)]
    steps = (batch // bb) * n_chunks
    c_in, c_out, c_shape, c_args = _cast_specs([(w, l, steps) for w, l in casts], lambda b, n: b * n_chunks + n)
    in_specs, args, out_specs, out_shape = in_specs + c_in, args + c_args, out_specs + c_out, out_shape + c_shape
    out = pl.pallas_call(
        functools.partial(_mlstm_body, n_chunks=n_chunks, bb=bb, has_init=has_init, emit_state=emit_state,
                          n_cast=len(casts)),
        grid=(batch // bb, n_chunks),
        in_specs=in_specs,
        out_specs=out_specs,
        out_shape=out_shape,
        scratch_shapes=[pltpu.VMEM((bb, N_DIR, DK_B, MW), F32),
                        pltpu.VMEM((bb, N_DIR, SUBLANES, LANES), F32),
                        pltpu.VMEM((bb, t, B_W), F32),
                        pltpu.VMEM((bb * N_DIR, H_B * CHUNK, H_B * DK_B), BF16),
                        pltpu.VMEM((bb * N_DIR, H_B * DK_B, MW), BF16),
                        pltpu.VMEM((bb * N_DIR, H_B * CHUNK, MW), BF16)],
        compiler_params=_params(("arbitrary", "arbitrary")),
        name="mlstm_scan",
    )(*args)
    return [out[0].reshape(batch * t, B_W), *out[1:]]


OUT_TM = 512
OUT_RB = 256
FFN_TM = 512
FFN_FC = 1024


def _outproj_body(ya_ref, yb_ref, wa_ref, wb_ref, x_ref, mod_ref, post1_ref, pre2_ref, x1_ref, h2_ref):
    for rb in range(x_ref.shape[0] // OUT_RB):
        rs = slice(rb * OUT_RB, (rb + 1) * OUT_RB)
        mix = (jnp.dot(ya_ref[rs, :], wa_ref[...], preferred_element_type=F32)
               + jnp.dot(yb_ref[rs, :], wb_ref[...], preferred_element_type=F32))
        x1 = x_ref[rs, :] + mod_ref[0, 2:3, :] * (_rms(mix) * post1_ref[...])
        x1_ref[rs, :] = x1
        h2 = _rms(x1) * pre2_ref[...] * (1.0 + mod_ref[0, 4:5, :]) + mod_ref[0, 3:4, :]
        h2_ref[rs, :] = h2.astype(BF16)


def _outproj(ya, yb, w_out, x2d, mod3, mod_map, post1, pre2, tm):
    m = x2d.shape[0]
    row = lambda i: (i, 0)
    const = lambda i: (0, 0)
    return pl.pallas_call(
        _outproj_body,
        grid=(m // tm,),
        in_specs=[pl.BlockSpec((tm, A_W), row), pl.BlockSpec((tm, B_W), row),
                  pl.BlockSpec((A_W, D_MODEL), lambda i: (0, 0)),
                  pl.BlockSpec((B_W, D_MODEL), lambda i: (1, 0)),
                  pl.BlockSpec((tm, D_MODEL), row),
                  pl.BlockSpec((1, 6, D_MODEL), lambda i: (mod_map(i * tm), 0, 0)),
                  pl.BlockSpec((1, D_MODEL), const), pl.BlockSpec((1, D_MODEL), const)],
        out_specs=[pl.BlockSpec((tm, D_MODEL), row), pl.BlockSpec((tm, D_MODEL), row)],
        out_shape=[jax.ShapeDtypeStruct((m, D_MODEL), F32), jax.ShapeDtypeStruct((m, D_MODEL), BF16)],
        compiler_params=_params(("parallel",)),
        name="outproj",
    )(ya, yb, w_out, w_out, x2d, mod3, post1, pre2)


def _ffn_body(h2_ref, w1_ref, w2_ref, x1_ref, mod_ref, post2_ref, o_ref):
    kk = pl.program_id(1)

    @pl.when(kk == 0)
    def _():
        o_ref[...] = jnp.zeros_like(o_ref)

    a = jnp.maximum(jnp.dot(h2_ref[...], w1_ref[...], preferred_element_type=F32), 0.0)
    o_ref[...] += jnp.dot((a * a).astype(BF16), w2_ref[...], preferred_element_type=F32)

    @pl.when(kk == pl.num_programs(1) - 1)
    def _():
        o_ref[...] = x1_ref[...] + mod_ref[0, 5:6, :] * (_rms(o_ref[...]) * post2_ref[...])


def _ffn(h2, w1, w2, x1, mod3, mod_map, post2, tm, fc):
    m = h2.shape[0]
    return pl.pallas_call(
        _ffn_body,
        grid=(m // tm, FFN // fc),
        in_specs=[pl.BlockSpec((tm, D_MODEL), lambda i, k: (i, 0)),
                  pl.BlockSpec((D_MODEL, fc), lambda i, k: (0, k)),
                  pl.BlockSpec((fc, D_MODEL), lambda i, k: (k, 0)),
                  pl.BlockSpec((tm, D_MODEL), lambda i, k: (i, 0)),
                  pl.BlockSpec((1, 6, D_MODEL), lambda i, k: (mod_map(i * tm), 0, 0)),
                  pl.BlockSpec((1, D_MODEL), lambda i, k: (0, 0))],
        out_specs=pl.BlockSpec((tm, D_MODEL), lambda i, k: (i, 0)),
        out_shape=jax.ShapeDtypeStruct((m, D_MODEL), F32),
        compiler_params=_params(("parallel", "arbitrary")),
        name="ffn",
    )(h2, w1, w2, x1, mod3, post2)


def _block(x, mod3, mod_of_row, lp, init, seq_len, emit_state):
    bsz, t, _ = x.shape
    x2d = x.reshape(bsz * t, D_MODEL)
    n_chunks = t // CHUNK
    qkv, rest, gates = _proj(x2d, mod3, lambda i: mod_of_row(i * PROJ_TM), lp["pre1"], lp["w16"],
                             lp["w_mid"], lp["w_gate"], lp["conv_w"], lp["gate_p"], seq_len)
    s0, c0, m0 = init if init is not None else (None, None, None)
    first = "w1" not in lp
    l = lp["l"]
    d_out = _delta(qkv, rest, gates, lp["norm_a"], s0, bsz, n_chunks, emit_state,
                   [(lp["w_ffn1"], l), (lp["w_out32"], l)] if first else ())
    m_out = _mlstm(rest, gates, lp["norm_b"], c0, m0, bsz, n_chunks, emit_state,
                   [(lp["w_ffn2"], l)] if first else ())
    if first:
        (lp["w1"], lp["w_out"]), lp["w2"] = d_out[-2:], m_out[-1]
    x1, h2 = _outproj(d_out[0], m_out[0], lp["w_out"], x2d, mod3, mod_of_row, lp["post1"], lp["pre2"], OUT_TM)
    y = _ffn(h2, lp["w1"], lp["w2"], x1, mod3, mod_of_row, lp["post2"], FFN_TM, FFN_FC)
    states = None
    if emit_state:
        m_fin = jnp.stack([m_out[3][:, d, 0, LF_OFF + d * H_B:LF_OFF + (d + 1) * H_B] for d in range(N_DIR)], axis=1)
        states = (d_out[1], m_out[1], m_out[2][..., 0], m_fin)
    return y.reshape(bsz, t, D_MODEL), states


def _layer_params(l, norm_mix_pre, norm_mix_post, norm_ffn_pre, norm_ffn_post, w_in, conv_w, a_log, dt_bias,
                  norm_a, mlstm_ibias, mlstm_fbias, norm_b, w_out, w_ffn1, w_ffn2):
    w = w_in[l]
    o_ag = QKV_W
    o_aa = o_ag + A_W
    o_ab = o_aa + N_DIR * H_A
    o_bq = o_ab + N_DIR * H_A
    o_bi = o_bq + 2 * H_B * DK_B + 2 * B_W
    o_bf = o_bi + N_DIR * H_B
    n_gate = 2 * N_DIR * H_A + 2 * N_DIR * H_B
    w16 = w.astype(BF16)
    w_gate = jnp.concatenate([w16[:, o_aa:o_bq], w16[:, o_bi:o_bf + N_DIR * H_B],
                              jnp.zeros((D_MODEL, LANES - n_gate), BF16)], axis=1)

    def lane_row(vals, off):
        return jnp.zeros((LANES,), F32).at[off:off + vals.size].set(vals.reshape(-1))

    gate_p = jnp.stack([lane_row(a_log[l], G_OFF), lane_row(dt_bias[l], G_OFF),
                        lane_row(mlstm_ibias[l], LI_OFF) + lane_row(mlstm_fbias[l], LF_OFF)]
                       + [jnp.zeros((LANES,), F32)] * (SUBLANES - 3))
    row = lambda v: v[l].reshape(1, -1)
    return dict(
        pre1=row(norm_mix_pre), post1=row(norm_mix_post), pre2=row(norm_ffn_pre), post2=row(norm_ffn_post),
        w16=w16, w_mid=w16[:, o_bq:o_bi], w_gate=w_gate, gate_p=gate_p,
        conv_w=jnp.concatenate([conv_w[l].T, jnp.zeros((SUBLANES - 3, QKV_W), F32)], axis=0),
        norm_a=row(norm_a), norm_b=row(norm_b),
        l=l, w_out32=w_out, w_ffn1=w_ffn1, w_ffn2=w_ffn2)


def kernel(x_prompt, x_sample, state_delta, state_mlstm_C, state_mlstm_n, state_mlstm_m, c, c_ctx, w_ada, b_ada, norm_mix_pre, norm_mix_post, norm_ffn_pre, norm_ffn_post, w_in, conv_w, a_log, dt_bias, norm_a, mlstm_ibias, mlstm_fbias, norm_b, w_out, w_ffn1, w_ffn2):
    depth = w_in.shape[0]
    n_lat = x_sample.shape[0]
    t_lat = x_sample.shape[1]
    cond = jnp.concatenate([c_ctx[None, :], c, jnp.zeros((SUBLANES - 1 - n_lat, D_MODEL), F32)], axis=0)
    y_prompt, y_sample = x_prompt, x_sample
    acc = ([], [], [], [])
    for l in range(depth):
        lp = _layer_params(l, norm_mix_pre, norm_mix_post, norm_ffn_pre, norm_ffn_post, w_in, conv_w, a_log,
                           dt_bias, norm_a, mlstm_ibias, mlstm_fbias, norm_b, w_out, w_ffn1, w_ffn2)
        mod = _ada(cond, w_ada, b_ada[l].reshape(1, -1), l)
        mod3 = mod[:1 + n_lat].reshape(1 + n_lat, 6, D_MODEL)
        y_prompt, st = _block(y_prompt, mod3, lambda r: 0, lp, None, x_prompt.shape[1], True)
        for a, s in zip(acc, st):
            a.append(s)
        n_rep = jnp.broadcast_to(state_mlstm_n[:, l][..., None], state_mlstm_n[:, l].shape + (LANES,))
        c_aug0 = jnp.concatenate([state_mlstm_C[:, l], n_rep], axis=-1)
        c_aug0 = c_aug0.transpose(0, 1, 3, 2, 4).reshape(n_lat, N_DIR, DK_B, MW)
        m0 = jnp.zeros((n_lat, N_DIR, LANES), F32)
        for d in range(N_DIR):
            m0 = m0.at[:, d, LF_OFF + d * H_B:LF_OFF + (d + 1) * H_B].set(state_mlstm_m[:, l, d])
        m0 = jnp.broadcast_to(m0[:, :, None, :], (n_lat, N_DIR, SUBLANES, LANES))
        y_sample, _ = _block(y_sample, mod3, lambda r: 1 + r // t_lat, lp, (state_delta[:, l], c_aug0, m0),
                             GRID_W, False)
    return (y_prompt, y_sample) + tuple(jnp.stack(a, axis=1) for a in acc)
```

```python
import functools

import jax
import jax.numpy as jnp
from jax import lax
from jax.experimental import pallas as pl
from jax.experimental.pallas import tpu as pltpu

F32 = jnp.float32
BF16 = jnp.bfloat16

D_MODEL = 2048
N_DIR = 2
A_W = D_MODEL // 2
B_W = D_MODEL - A_W
DK_A = 128
DV_A = 128
H_A = A_W // DV_A
DV_B = 256
DK_B = DV_B // 2
H_B = B_W // DV_B
GRID_W = 64
CHUNK = 64
FFN = 4 * D_MODEL
EPS = 1e-6
LANES = 128
SUBLANES = 8
NEG = -1e30

QKV_W = 3 * A_W
REST_W = A_W + 2 * H_B * DK_B + 2 * B_W
OFF_AG, OFF_BQ, OFF_BK, OFF_BV, OFF_BO = 0, A_W, A_W + H_B * DK_B, A_W + 2 * H_B * DK_B, A_W + 2 * H_B * DK_B + B_W
G_OFF, BETA_OFF = 0, N_DIR * H_A
LI_OFF = 2 * N_DIR * H_A
LF_OFF = LI_OFF + N_DIR * H_B

O_AG = QKV_W
O_AA = O_AG + A_W
O_BQ = O_AA + 2 * N_DIR * H_A
O_BI = O_BQ + REST_W - A_W
O_END = O_BI + 2 * N_DIR * H_B

VMEM_LIMIT = 56 * 1024 * 1024


def _sigmoid(x):
    return 1.0 / (1.0 + jnp.exp(-x))


def _softplus(x):
    return jnp.maximum(x, 0.0) + jnp.log1p(jnp.exp(-jnp.abs(x)))


def _dot(a, b):
    return jnp.dot(a.astype(BF16), b.astype(BF16), preferred_element_type=F32)


def _dot_f32(a, b):
    return jnp.dot(a, b, precision=lax.Precision.HIGHEST, preferred_element_type=F32)


def _rms(x):
    return x * lax.rsqrt(jnp.mean(x * x, axis=-1, keepdims=True) + EPS)


def _params(sem):
    return pltpu.CompilerParams(dimension_semantics=sem, vmem_limit_bytes=VMEM_LIMIT)


ADA_TN = 768


def _ada_body(c_ref, w_ref, b_ref, win_ref, o_ref, w16_ref, wmid_ref):
    c = c_ref[...]
    o_ref[...] = _dot(c * _sigmoid(c), w_ref[...]) + b_ref[...]
    w16_ref[...] = win_ref[...].astype(BF16)
    wmid_ref[...] = win_ref[:, O_BQ:O_BI].astype(BF16)


def _ada(c_all, w_ada, b, w_in, l):
    n = w_ada.shape[2]
    steps = n // ADA_TN
    rows = D_MODEL // steps
    assert steps * ADA_TN == n and rows * steps == D_MODEL and rows % (2 * SUBLANES) == 0
    width = w_in.shape[2]
    return pl.pallas_call(
        _ada_body,
        grid=(steps,),
        in_specs=[pl.BlockSpec(c_all.shape, lambda j: (0, 0)),
                  pl.BlockSpec((None, D_MODEL, ADA_TN), lambda j: (l, 0, j)),
                  pl.BlockSpec((1, ADA_TN), lambda j: (0, j)),
                  pl.BlockSpec((None, rows, width), lambda j: (l, j, 0))],
        out_specs=[pl.BlockSpec((c_all.shape[0], ADA_TN), lambda j: (0, j)),
                   pl.BlockSpec((rows, width), lambda j: (j, 0)),
                   pl.BlockSpec((rows, O_BI - O_BQ), lambda j: (j, 0))],
        out_shape=[jax.ShapeDtypeStruct((c_all.shape[0], n), F32),
                   jax.ShapeDtypeStruct((D_MODEL, width), BF16),
                   jax.ShapeDtypeStruct((D_MODEL, O_BI - O_BQ), BF16)],
        compiler_params=_params(("arbitrary",)),
        name="ada",
    )(c_all, w_ada, b, w_in)


PROJ_TM = 1024
PROJ_RB = 512
PROJ_CB = 256
PROJ_TN = 512
N_QKV_T = QKV_W // PROJ_TN
N_REST_T = REST_W // PROJ_TN
N_HEAD_T = (QKV_W + A_W) // PROJ_TN
N_MID_T = N_QKV_T + N_REST_T - N_HEAD_T


def _proj_body(x_ref, mod_ref, g_ref, wh_ref, wb_ref, wg_ref, cw_ref, gp_ref, qkv_ref, rest_ref, gate_ref, h_scr, *,
               seq_len):
    j = pl.program_id(1)
    row_blocks = [slice(rb * PROJ_RB, (rb + 1) * PROJ_RB) for rb in range(PROJ_TM // PROJ_RB)]
    col_blocks = [slice(cb * PROJ_CB, (cb + 1) * PROJ_CB) for cb in range(PROJ_TN // PROJ_CB)]

    def run_rows(rs, w_ref, epilogue, out_ref):
        for cs in col_blocks:
            epilogue(jnp.dot(h_scr[rs, :], w_ref[:, cs], preferred_element_type=F32), out_ref, rs, cs)

    def run(w_ref, epilogue, out_ref):
        for rs in row_blocks:
            run_rows(rs, w_ref, epilogue, out_ref)

    def run_first():
        for rs in row_blocks:
            y = _rms(x_ref[rs, :]) * g_ref[...]
            h_scr[rs, :] = (y * (1.0 + mod_ref[0, 1:2, :]) + mod_ref[0, 0:1, :]).astype(BF16)
            run_rows(rs, wh_ref, ep_l2(DK_A ** -0.5), qkv_ref)
            ep_gate(jnp.dot(h_scr[rs, :], wg_ref[...], preferred_element_type=F32), gate_ref, rs)

    def conv_silu(acc, cs, g):
        a = acc[:, g * LANES:(g + 1) * LANES]
        ls = slice(cs.start + g * LANES, cs.start + (g + 1) * LANES)
        pos = lax.broadcasted_iota(jnp.int32, a.shape, 0) & (seq_len - 1)
        prev = jnp.where(pos == 0, 0.0, pltpu.roll(a, 1, 0))
        nxt = jnp.where(pos == seq_len - 1, 0.0, pltpu.roll(a, a.shape[0] - 1, 0))
        y = prev * cw_ref[0:1, ls] + a * cw_ref[1:2, ls] + nxt * cw_ref[2:3, ls]
        return ls, y * _sigmoid(y)

    def ep_l2(scale):
        def f(acc, out_ref, rs, cs):
            for g in range(PROJ_CB // LANES):
                ls, blk = conv_silu(acc, cs, g)
                inv = lax.rsqrt(jnp.sum(blk * blk, axis=-1, keepdims=True) + EPS)
                out_ref[rs, ls] = blk * (inv * scale)
        return f

    def ep_conv(acc, out_ref, rs, cs):
        for g in range(PROJ_CB // LANES):
            ls, blk = conv_silu(acc, cs, g)
            out_ref[rs, ls] = blk

    def ep_map(fn):
        def f(acc, out_ref, rs, cs):
            out_ref[rs, cs] = fn(acc)
        return f

    def ep_gate(z, out_ref, rs):
        lane = lax.broadcasted_iota(jnp.int32, z.shape, 1)
        g = -jnp.exp(gp_ref[0:1, :]) * _softplus(z + gp_ref[1:2, :])
        li = z + gp_ref[2:3, :]
        out_ref[rs, :] = jnp.where(lane < BETA_OFF, g,
                                   jnp.where(lane < LI_OFF, _sigmoid(z),
                                             jnp.where(lane < LF_OFF, li,
                                                       jnp.where(lane < LF_OFF + N_DIR * H_B, -_softplus(-li), 0.0))))

    tp = A_W // PROJ_TN
    c0 = (j - N_QKV_T) * PROJ_TN
    in_rest = (j >= N_QKV_T) & (j < N_QKV_T + N_REST_T)
    pl.when(j == 0)(run_first)
    variants = [
        ((j > 0) & (j < tp), wh_ref, ep_l2(DK_A ** -0.5), qkv_ref),
        ((j >= tp) & (j < 2 * tp), wh_ref, ep_l2(1.0), qkv_ref),
        ((j >= 2 * tp) & (j < N_QKV_T), wh_ref, ep_conv, qkv_ref),
        (in_rest & (c0 < OFF_BQ), wh_ref, ep_map(lambda a: a * _sigmoid(a)), rest_ref),
        (in_rest & (c0 >= OFF_BQ) & (c0 < OFF_BK), wb_ref, ep_map(lambda a: a * (DK_B ** -0.5)), rest_ref),
        (in_rest & (c0 >= OFF_BK) & (c0 < OFF_BO), wb_ref, ep_map(lambda a: a), rest_ref),
        (in_rest & (c0 >= OFF_BO), wb_ref, ep_map(_sigmoid), rest_ref),
    ]
    for cond, w_ref, epilogue, out_ref in variants:
        pl.when(cond)(functools.partial(run, w_ref, epilogue, out_ref))


def _proj(x2d, mod3, mod_map, pre_g, w16, w_mid, w_gate, conv_w, gate_p, seq_len):
    m = x2d.shape[0]
    tm, tn = PROJ_TM, PROJ_TN
    assert PROJ_RB % seq_len == 0 and m % tm == 0
    return pl.pallas_call(
        functools.partial(_proj_body, seq_len=seq_len),
        grid=(m // tm, N_HEAD_T + N_MID_T),
        in_specs=[pl.BlockSpec((tm, D_MODEL), lambda i, j: (i, 0)),
                  pl.BlockSpec((1, 6, D_MODEL), lambda i, j: (mod_map(i), 0, 0)),
                  pl.BlockSpec((1, D_MODEL), lambda i, j: (0, 0)),
                  pl.BlockSpec((D_MODEL, tn), lambda i, j: (0, jnp.minimum(j, N_HEAD_T - 1))),
                  pl.BlockSpec((D_MODEL, tn), lambda i, j: (0, jnp.clip(j - N_HEAD_T, 0, N_MID_T - 1))),
                  pl.BlockSpec((D_MODEL, LANES), lambda i, j: (0, 0)),
                  pl.BlockSpec((SUBLANES, tn), lambda i, j: (0, jnp.minimum(j, N_QKV_T - 1))),
                  pl.BlockSpec((SUBLANES, LANES), lambda i, j: (0, 0))],
        out_specs=[pl.BlockSpec((tm, tn), lambda i, j: (i, jnp.minimum(j, N_QKV_T - 1))),
                   pl.BlockSpec((tm, tn), lambda i, j: (i, jnp.clip(j - N_QKV_T, 0, N_REST_T - 1))),
                   pl.BlockSpec((tm, LANES), lambda i, j: (i, 0))],
        out_shape=[jax.ShapeDtypeStruct((m, QKV_W), F32), jax.ShapeDtypeStruct((m, REST_W), F32),
                   jax.ShapeDtypeStruct((m, LANES), F32)],
        scratch_shapes=[pltpu.VMEM((tm, D_MODEL), BF16)],
        compiler_params=_params(("parallel", "arbitrary")),
        name="proj",
    )(x2d, mod3, pre_g, w16, w_mid, w_gate, conv_w, gate_p)


def _masks(d):
    row = lax.broadcasted_iota(jnp.int32, (CHUNK, CHUNK), 0)
    col = lax.broadcasted_iota(jnp.int32, (CHUNK, CHUNK), 1)
    if d == 0:
        return row >= col, row > col, row == col
    return row <= col, row < col, row == col


GROUP = 4
N_GROUPS = H_A // GROUP
PAIR = LANES // CHUNK


def _split(x):
    hi = x.astype(BF16)
    return hi, (x - hi.astype(F32)).astype(BF16)


def _store_blocks(ref, x, blk_r, blk_c, col0=0):
    for h in range(GROUP):
        ref[h * blk_r:(h + 1) * blk_r, col0 + h * blk_c:col0 + (h + 1) * blk_c] = x[:, h * blk_c:(h + 1) * blk_c]


def _dot3(a_hi, a_lo, b_hi, b_lo):
    m = a_hi.shape[0]
    r = jnp.dot(jnp.concatenate([a_hi, a_lo], axis=0), b_hi, preferred_element_type=F32)
    return r[:m] + r[m:] + jnp.dot(a_hi, b_lo, preferred_element_type=F32)


HALF = CHUNK // 2


def _tri_inverse_wide(lmats, upper, bd_hi, bd_lo, by_hi, by_lo):
    width = lmats[0].shape[1]
    n_blk = width // HALF
    row = lax.broadcasted_iota(jnp.int32, (HALF, width), 0)
    lane = lax.broadcasted_iota(jnp.int32, (HALF, width), 1)
    lead = (lane & (CHUNK - 1)) < HALF
    eye_d = (row == (lane & (HALF - 1))).astype(F32)

    def bd_dot(i, a, b_hi, b_lo):
        for blk in range(n_blk):
            sl = slice(blk * HALF, (blk + 1) * HALF)
            bd_hi[i, sl, sl] = b_hi[:, sl]
            bd_lo[i, sl, sl] = b_lo[:, sl]
        return _dot3(*a, bd_hi[i], bd_lo[i])

    idx = range(len(lmats))
    diag = [jnp.where(lead, l[:HALF], l[HALF:]) for l in lmats]
    s = [eye_d - dg for dg in diag]
    p = []
    for i in idx:
        m_hi, m_lo = _split(-diag[i])
        p.append(bd_dot(i, (m_hi, m_lo), m_hi, m_lo))
    for _ in range(3):
        for i in idx:
            p_hi, p_lo = _split(p[i])
            s_hi, s_lo = _split(s[i])
            r = bd_dot(i, (jnp.concatenate([p_hi, s_hi], axis=0), jnp.concatenate([p_lo, s_lo], axis=0)), p_hi, p_lo)
            p[i] = r[:HALF]
            s[i] = s[i] + r[HALF:]
    for i in idx:
        p_hi, p_lo = _split(p[i])
        s[i] = s[i] + bd_dot(i, _split(s[i]), p_hi, p_lo)
    y = []
    for i in idx:
        c_blk = jnp.where(lead, 0.0, lmats[i][:HALF]) if upper[i] else jnp.where(lead, lmats[i][HALF:], 0.0)
        y.append(bd_dot(i, _split(c_blk), *_split(s[i])))
    out = []
    for i in idx:
        y_hi, y_lo = _split(y[i])
        for h in range(width // CHUNK):
            a_sl, b_sl = slice(h * CHUNK, h * CHUNK + HALF), slice(h * CHUNK + HALF, (h + 1) * CHUNK)
            rs, cs = (a_sl, b_sl) if upper[i] else (b_sl, a_sl)
            by_hi[i, rs, cs] = y_hi[:, cs]
            by_lo[i, rs, cs] = y_lo[:, cs]
        ai, bi = jnp.where(lead, s[i], 0.0), jnp.where(lead, 0.0, s[i])
        x = _dot3(*_split(ai if upper[i] else bi), by_hi[i], by_lo[i])
        out.append(jnp.concatenate([ai - x, bi] if upper[i] else [ai, bi - x], axis=0))
    return out


def _col_bcast(tile, c, width=LANES):
    return jnp.broadcast_to(tile[:, c:c + 1], (tile.shape[0], width))


def _delta_body(*refs, n_chunks, bb, has_init, emit_state, n_cast):
    qkv = refs[0:2]
    ag = refs[2:4]
    gt = refs[4:6]
    norm_ref = refs[6]
    pos = 7
    s0_ref = None
    if has_init:
        s0_ref = refs[pos]
        pos += 1
    cast_in = refs[pos:pos + n_cast]
    pos += n_cast
    ya_ref = refs[pos]
    pos += 1
    sout_ref = None
    if emit_state:
        sout_ref = refs[pos]
        pos += 1
    for w_ref, w16_ref in zip(cast_in, refs[pos:pos + n_cast]):
        w16_ref[...] = w_ref[...].astype(BF16)
    pos += n_cast
    s_scr, o_scr, bdn_hi, bdn_lo, bdy_hi, bdy_lo, bdk, bduw, bds, bdv = refs[pos:pos + 10]

    n = pl.program_id(1)
    gw = GROUP * DK_A

    @pl.when((n == 0) & (pl.program_id(0) == 0))
    def _():
        for ref in (bdn_hi, bdn_lo, bdy_hi, bdy_lo, bdk, bduw, bds, bdv):
            ref[...] = jnp.zeros_like(ref)

    @pl.when(n == 0)
    def _():
        o_scr[...] = jnp.zeros_like(o_scr)
        for bi in range(bb):
            for d in range(N_DIR):
                for h in range(H_A):
                    blk = s0_ref[bi, d, h] if has_init else jnp.zeros((DK_A, DV_A), F32)
                    s_scr[bi, d, h // GROUP, :, (h % GROUP) * DV_A:(h % GROUP + 1) * DV_A] = blk

    row = lax.broadcasted_iota(jnp.int32, (CHUNK, LANES), 0)
    lane = lax.broadcasted_iota(jnp.int32, (CHUNK, LANES), 1)
    col = lane & (CHUNK - 1)
    left = lane < CHUNK
    probs = [(bi, d, g) for bi in range(bb) for d in range(N_DIR) for g in range(N_GROUPS)]
    masks = {0: (row >= col, row > col), 1: (row <= col, row < col)}
    gates, gc, gc_t, eg, egl = {}, {}, {}, {}, {}
    for bi in range(bb):
        for d in range(N_DIR):
            key = (bi, d)
            gates[key] = gt[d][bi]
            gc[key] = _dot_f32(_masks(d)[0].astype(F32), gates[key])
            gc_t[key] = jnp.concatenate([gc[key], gc[key]], axis=0).T
            last = CHUNK - 1 if d == 0 else 0
            eg[key] = jnp.exp(gc[key])
            egl[key] = jnp.exp(gc[key][last:last + 1, :] - gc[key])

    def cols_of(d, g):
        return [G_OFF + d * H_A + g * GROUP + hl for hl in range(GROUP)]

    q16, ks, beta_xs, decays, grams = [], [], [], [], []
    for gi, (bi, d, g) in enumerate(probs):
        key = (bi, d)
        cols = cols_of(d, g)
        incl, _ = masks[d]
        k = qkv[d][bi, :, A_W + g * gw:A_W + (g + 1) * gw]
        beta_x = jnp.concatenate([_col_bcast(gates[key], BETA_OFF - G_OFF + c) for c in cols], axis=1)
        decay = []
        for p in range(GROUP // PAIR):
            c0, c1 = cols[PAIR * p], cols[PAIR * p + 1]
            gcol = jnp.where(left, _col_bcast(gc[key], c0), _col_bcast(gc[key], c1))
            grow = jnp.where(left[0:1], gc_t[key][c0:c0 + 1, :], gc_t[key][c1:c1 + 1, :])
            decay.append(jnp.exp(jnp.where(incl, gcol - grow, NEG)))
        decays.append(jnp.concatenate(decay, axis=1))
        q16.append(qkv[d][bi, :, g * gw:(g + 1) * gw].astype(BF16))
        ks.append(k)
        beta_xs.append(beta_x)
        _store_blocks(bdk.at[gi], k.astype(BF16), CHUNK, DK_A)
    for gi in range(len(probs)):
        grams.append(lax.dot_general(jnp.concatenate([q16[gi], (ks[gi] * beta_xs[gi]).astype(BF16)], axis=0),
                                     bdk[gi], (((1,), (1,)), ((), ())), preferred_element_type=F32))
    attns, lmats = [], []
    for gi, (bi, d, g) in enumerate(probs):
        strict_w = jnp.concatenate([masks[d][1]] * (GROUP // PAIR), axis=1)
        attns.append((grams[gi][:CHUNK] * decays[gi]).astype(BF16))
        lmats.append(jnp.where(strict_w, grams[gi][CHUNK:] * decays[gi], 0.0))
    ainvs = _tri_inverse_wide(lmats, [d == 1 for _, d, _ in probs], bdn_hi, bdn_lo, bdy_hi, bdy_lo)
    eg_xs, kd_ts = [], []
    for gi, (bi, d, g) in enumerate(probs):
        key = (bi, d)
        cols = cols_of(d, g)
        v = qkv[d][bi, :, 2 * A_W + g * gw:2 * A_W + (g + 1) * gw]
        eg_x = jnp.concatenate([_col_bcast(eg[key], c) for c in cols], axis=1)
        egl_x = jnp.concatenate([_col_bcast(egl[key], c) for c in cols], axis=1)
        eg_xs.append(eg_x)
        _store_blocks(bduw.at[gi], (v * beta_xs[gi]).astype(BF16), CHUNK, DV_A)
        _store_blocks(bduw.at[gi], (ks[gi] * (beta_xs[gi] * eg_x)).astype(BF16), CHUNK, DK_A, col0=gw)
        kd = ks[gi] * egl_x
        kd_ts.append(jnp.concatenate([kd[:, hl * DK_A:(hl + 1) * DK_A] for hl in range(GROUP)], axis=0)
                     .T.astype(BF16))
        _store_blocks(bds.at[gi], s_scr[bi, d, g].astype(BF16), DK_A, DV_A)
    uws = []
    for gi in range(len(probs)):
        t_hi, t_lo = _split(ainvs[gi])
        r = jnp.dot(jnp.concatenate([t_hi, t_lo], axis=0), bduw[gi], preferred_element_type=F32)
        uws.append(r[:CHUNK] + r[CHUNK:])
    ws_qs = []
    hw = gw // 2
    for gi in range(len(probs)):
        wq = jnp.concatenate([uws[gi][:, gw:].astype(BF16), q16[gi]], axis=0)
        ws_qs.append(jnp.concatenate(
            [jnp.dot(wq[:, p * hw:(p + 1) * hw], bds[gi, p * hw:(p + 1) * hw, p * hw:(p + 1) * hw],
                     preferred_element_type=F32) for p in range(2)], axis=1))
    for gi in range(len(probs)):
        _store_blocks(bdv.at[gi], (uws[gi][:, :gw] - ws_qs[gi][:CHUNK]).astype(BF16), CHUNK, DV_A)
    rs = []
    for gi in range(len(probs)):
        rs.append(jnp.dot(jnp.concatenate([attns[gi], kd_ts[gi]], axis=0), bdv[gi], preferred_element_type=F32))
    tots = []
    for gi, (bi, d, g) in enumerate(probs):
        cols = cols_of(d, g)
        last = CHUNK - 1 if d == 0 else 0
        o = ws_qs[gi][CHUNK:] * eg_xs[gi] + rs[gi][:CHUNK]
        eg_last = jnp.concatenate(
            [jnp.broadcast_to(eg[bi, d][last:last + 1, c:c + 1], (1, DV_A)) for c in cols], axis=1)
        s_scr[bi, d, g] = s_scr[bi, d, g] * eg_last + rs[gi][CHUNK:]
        cidx = n if d == 0 else n_chunks - 1 - n
        rows = pl.ds(pl.multiple_of(cidx * CHUNK, CHUNK), CHUNK)
        tot = o + o_scr[bi, rows, g * gw:(g + 1) * gw]
        o_scr[bi, rows, g * gw:(g + 1) * gw] = tot
        tots.append((rows, tot))
    blks = [(bi, d, rows, g * GROUP + hl, tot[:, hl * DV_A:(hl + 1) * DV_A])
            for (bi, d, g), (rows, tot) in zip(probs, tots) for hl in range(GROUP)]
    inv = [lax.rsqrt(jnp.mean(blk * blk, axis=-1, keepdims=True) + EPS) for *_, blk in blks]
    for (bi, d, rows, h, blk), r in zip(blks, inv):
        hs = slice(h * DV_A, (h + 1) * DV_A)
        ya_ref[bi, rows, hs] = (blk * r * norm_ref[...] * ag[d][bi, :, hs]).astype(BF16)

    if emit_state:
        @pl.when(n == n_chunks - 1)
        def _():
            for bi in range(bb):
                for d in range(N_DIR):
                    for h in range(H_A):
                        sout_ref[bi, d, h] = s_scr[bi, d, h // GROUP, :, (h % GROUP) * DV_A:(h % GROUP + 1) * DV_A]


SCAN_BB = 2


def _scan_specs(bb, n_chunks, width):
    return (pl.BlockSpec((bb, CHUNK, width), lambda b, n: (b, n, 0)),
            pl.BlockSpec((bb, CHUNK, width), lambda b, n: (b, n_chunks - 1 - n, 0)))


def _cast_specs(casts, steps_of):
    in_specs, out_specs, out_shape, args = [], [], [], []
    for w, l, steps in casts:
        rows = w.shape[1] // steps
        assert rows * steps == w.shape[1] and rows % (2 * SUBLANES) == 0
        in_specs.append(pl.BlockSpec((None, rows, w.shape[2]), lambda b, n, l=l: (l, steps_of(b, n), 0)))
        out_specs.append(pl.BlockSpec((rows, w.shape[2]), lambda b, n: (steps_of(b, n), 0)))
        out_shape.append(jax.ShapeDtypeStruct(w.shape[1:], BF16))
        args.append(w)
    return in_specs, out_specs, out_shape, args


def _delta(qkv, rest, gates, norm_a, s0, batch, n_chunks, emit_state, casts=()):
    t = n_chunks * CHUNK
    has_init = s0 is not None
    bb = SCAN_BB
    ng = bb * N_DIR * N_GROUPS
    qkv, rest, gates = (a.reshape(batch, t, a.shape[-1]) for a in (qkv, rest, gates))
    in_specs = [*_scan_specs(bb, n_chunks, QKV_W), *_scan_specs(bb, n_chunks, A_W), *_scan_specs(bb, n_chunks, LANES),
                pl.BlockSpec((1, DV_A), lambda b, n: (0, 0))]
    args = [qkv, qkv, rest, rest, gates, gates, norm_a]
    sspec = pl.BlockSpec((bb, N_DIR, H_A, DK_A, DV_A), lambda b, n: (b, 0, 0, 0, 0))
    if has_init:
        in_specs.append(sspec)
        args.append(s0)
    out_specs = [pl.BlockSpec((bb, t, A_W), lambda b, n: (b, 0, 0))]
    out_shape = [jax.ShapeDtypeStruct((batch, t, A_W), BF16)]
    if emit_state:
        out_specs.append(sspec)
        out_shape.append(jax.ShapeDtypeStruct((batch, N_DIR, H_A, DK_A, DV_A), F32))
    steps = (batch // bb) * n_chunks
    c_in, c_out, c_shape, c_args = _cast_specs([(w, l, steps) for w, l in casts], lambda b, n: b * n_chunks + n)
    in_specs, args, out_specs, out_shape = in_specs + c_in, args + c_args, out_specs + c_out, out_shape + c_shape
    out = pl.pallas_call(
        functools.partial(_delta_body, n_chunks=n_chunks, bb=bb, has_init=has_init, emit_state=emit_state,
                          n_cast=len(casts)),
        grid=(batch // bb, n_chunks),
        in_specs=in_specs,
        out_specs=out_specs,
        out_shape=out_shape,
        scratch_shapes=[pltpu.VMEM((bb, N_DIR, N_GROUPS, DK_A, GROUP * DV_A), F32), pltpu.VMEM((bb, t, A_W), F32),
                        pltpu.VMEM((ng, GROUP * CHUNK, GROUP * CHUNK), BF16),
                        pltpu.VMEM((ng, GROUP * CHUNK, GROUP * CHUNK), BF16),
                        pltpu.VMEM((ng, GROUP * CHUNK, GROUP * CHUNK), BF16),
                        pltpu.VMEM((ng, GROUP * CHUNK, GROUP * CHUNK), BF16),
                        pltpu.VMEM((ng, GROUP * CHUNK, GROUP * DK_A), BF16),
                        pltpu.VMEM((ng, GROUP * CHUNK, 2 * GROUP * DK_A), BF16),
                        pltpu.VMEM((ng, GROUP * DK_A, GROUP * DV_A), BF16),
                        pltpu.VMEM((ng, GROUP * CHUNK, GROUP * DV_A), BF16)],
        compiler_params=_params(("arbitrary", "arbitrary")),
        name="delta_scan",
    )(*args)
    return [out[0].reshape(batch * t, A_W), *out[1:]]


CAUG_W = DV_B + LANES
MW = H_B * CAUG_W


def _scan_max(x, d):
    row = lax.broadcasted_iota(jnp.int32, x.shape, 0)
    s = 1
    while s < CHUNK:
        if d == 0:
            shifted = jnp.where(row >= s, pltpu.roll(x, s, 0), NEG)
        else:
            shifted = jnp.where(row < CHUNK - s, pltpu.roll(x, CHUNK - s, 0), NEG)
        x = jnp.maximum(x, shifted)
        s *= 2
    return x


def _mlstm_body(*refs, n_chunks, bb, has_init, emit_state, n_cast):
    rest = refs[0:2]
    gt = refs[2:4]
    norm_ref = refs[4]
    pos = 5
    c0_ref = m0_ref = None
    if has_init:
        c0_ref, m0_ref = refs[pos], refs[pos + 1]
        pos += 2
    cast_in = refs[pos:pos + n_cast]
    pos += n_cast
    yb_ref = refs[pos]
    pos += 1
    cout_ref = nout_ref = mout_ref = None
    if emit_state:
        cout_ref, nout_ref, mout_ref = refs[pos:pos + 3]
        pos += 3
    for w_ref, w16_ref in zip(cast_in, refs[pos:pos + n_cast]):
        w16_ref[...] = w_ref[...].astype(BF16)
    pos += n_cast
    c_scr, m_scr, o_scr, bdk, bdc, bdv = refs[pos:pos + 6]

    n = pl.program_id(1)
    qw = H_B * DK_B

    @pl.when((n == 0) & (pl.program_id(0) == 0))
    def _():
        for ref in (bdk, bdc, bdv):
            ref[...] = jnp.zeros_like(ref)
        for i in range(bb * N_DIR):
            for h in range(H_B):
                bdv[i, h * CHUNK:(h + 1) * CHUNK, h * CAUG_W + DV_B:(h + 1) * CAUG_W] = jnp.ones((CHUNK, LANES), BF16)

    @pl.when(n == 0)
    def _():
        o_scr[...] = jnp.zeros_like(o_scr)
        if has_init:
            c_scr[...] = c0_ref[...]
            m_scr[...] = m0_ref[...]
        else:
            c_scr[...] = jnp.zeros_like(c_scr)
            m_scr[...] = jnp.zeros_like(m_scr)

    row = lax.broadcasted_iota(jnp.int32, (CHUNK, LANES), 0)
    lane = lax.broadcasted_iota(jnp.int32, (CHUNK, LANES), 1)
    col = lane & (CHUNK - 1)
    left = lane < CHUNK
    dirs = [(bi, d) for bi in range(bb) for d in range(N_DIR)]
    slot = {key: i for i, key in enumerate(dirs)}
    q16 = {}
    for key in dirs:
        bi, d = key
        i = slot[key]
        q16[key] = rest[d][bi, :, OFF_BQ:OFF_BQ + qw].astype(BF16)
        _store_blocks(bdk.at[i], rest[d][bi, :, OFF_BK:OFF_BK + qw].astype(BF16), CHUNK, DK_B)
        for h in range(H_B):
            bdv[i, h * CHUNK:(h + 1) * CHUNK, h * CAUG_W:h * CAUG_W + DV_B] = (
                rest[d][bi, :, OFF_BV + h * DV_B:OFF_BV + (h + 1) * DV_B].astype(BF16))
        _store_blocks(bdc.at[i], c_scr[bi, d].astype(BF16), DK_B, CAUG_W)
    qk = {key: lax.dot_general(q16[key], bdk[slot[key]], (((1,), (1,)), ((), ())), preferred_element_type=F32)
          for key in dirs}
    nc, a_t, iw, emt, ksc, dec_row = {}, {}, {}, {}, {}, {}
    for key in dirs:
        bi, d = key
        lo = LF_OFF + d * H_B
        mine = (lane >= lo) & (lane < lo + H_B)
        g = gt[d][bi]
        gc = jnp.where(mine, _dot_f32(_masks(d)[0].astype(F32), g), 0.0)
        a = jnp.where(mine, pltpu.roll(g, LF_OFF - LI_OFF, 1), 0.0) - gc
        last = CHUNK - 1 if d == 0 else 0
        m_old = m_scr[bi, d][0:1, :]
        mx = jnp.maximum(m_old, _scan_max(a, d))
        mxl = mx[last:last + 1, :]
        nc[key] = -mx
        a_t[key] = jnp.concatenate([a, a], axis=0).T
        iw[key] = jnp.exp(m_old - mx)
        emt[key] = jnp.exp(-(gc + mx))
        ksc[key] = jnp.exp(a - mxl)
        dec_row[key] = jnp.exp(m_old - mxl)
        m_scr[bi, d] = jnp.broadcast_to(gc[last:last + 1, :] + mxl, (SUBLANES, LANES))

    ks_t = {}
    for key in dirs:
        bi, d = key
        lo = LF_OFF + d * H_B
        ks = rest[d][bi, :, OFF_BK:OFF_BK + qw] * jnp.concatenate(
            [_col_bcast(ksc[key], lo + h) for h in range(H_B)], axis=1)
        ks_t[key] = jnp.concatenate([ks[:, h * DK_B:(h + 1) * DK_B] for h in range(H_B)], axis=0).T.astype(BF16)
    lhs = {}
    for key in dirs:
        bi, d = key
        lo = LF_OFF + d * H_B
        incl = row >= col if d == 0 else row <= col
        log_w = []
        for p in range(H_B // PAIR):
            l0, l1 = lo + PAIR * p, lo + PAIR * p + 1
            ccol = jnp.where(left, _col_bcast(nc[key], l0), _col_bcast(nc[key], l1))
            crow = jnp.where(left[0:1], a_t[key][l0:l0 + 1, :], a_t[key][l1:l1 + 1, :])
            log_w.append(jnp.where(incl, ccol + crow, NEG))
        dw = jnp.exp(jnp.concatenate(log_w, axis=1)) * qk[key]
        iw_x = jnp.concatenate([_col_bcast(iw[key], lo + h) for h in range(H_B)], axis=1)
        lhs[key] = ((rest[d][bi, :, OFF_BQ:OFF_BQ + qw] * iw_x).astype(BF16), dw.astype(BF16))
    num = {}
    for key in dirs:
        i = slot[key]
        parts = []
        for p in range(H_B // PAIR):
            ql, dl, cl = slice(p * PAIR * DK_B, (p + 1) * PAIR * DK_B), slice(p * LANES, (p + 1) * LANES), \
                slice(p * PAIR * CAUG_W, (p + 1) * PAIR * CAUG_W)
            parts.append(jnp.dot(jnp.concatenate([lhs[key][0][:, ql], lhs[key][1][:, dl]], axis=1),
                                 jnp.concatenate([bdc[i, ql, cl], bdv[i, dl, cl]], axis=0),
                                 preferred_element_type=F32))
        num[key] = jnp.concatenate(parts, axis=1)
    upd = {key: jnp.dot(ks_t[key], bdv[slot[key]], preferred_element_type=F32) for key in dirs}
    for key in dirs:
        bi, d = key
        lo = LF_OFF + d * H_B
        dec_x = jnp.concatenate(
            [jnp.broadcast_to(dec_row[key][:, lo + h:lo + h + 1], (1, CAUG_W)) for h in range(H_B)], axis=1)
        c_scr[bi, d] = c_scr[bi, d] * dec_x + upd[key]
    heads = [(key, h) for key in dirs for h in range(H_B)]
    rows = {d: pl.ds(pl.multiple_of((n if d == 0 else n_chunks - 1 - n) * CHUNK, CHUNK), CHUNK) for d in range(N_DIR)}
    tots = []
    for key, h in heads:
        bi, d = key
        vs = slice(h * DV_B, (h + 1) * DV_B)
        den = jnp.maximum(jnp.abs(num[key][:, h * CAUG_W + DV_B:(h + 1) * CAUG_W]),
                          _col_bcast(emt[key], LF_OFF + d * H_B + h))
        hb = jnp.concatenate([num[key][:, h * CAUG_W:h * CAUG_W + LANES] / den,
                              num[key][:, h * CAUG_W + LANES:h * CAUG_W + DV_B] / den], axis=1)
        tot = hb + o_scr[bi, rows[d], vs]
        o_scr[bi, rows[d], vs] = tot
        tots.append(tot)
    inv = [lax.rsqrt(jnp.mean(tot * tot, axis=-1, keepdims=True) + EPS) for tot in tots]
    for (key, h), tot, r in zip(heads, tots, inv):
        bi, d = key
        vs = slice(h * DV_B, (h + 1) * DV_B)
        ogate = rest[d][bi, :, OFF_BO + h * DV_B:OFF_BO + (h + 1) * DV_B]
        yb_ref[bi, rows[d], vs] = (tot * r * norm_ref[...] * ogate).astype(BF16)

    if emit_state:
        @pl.when(n == n_chunks - 1)
        def _():
            for bi, d in dirs:
                for h in range(H_B):
                    cout_ref[bi, d, h] = c_scr[bi, d, :, h * CAUG_W:h * CAUG_W + DV_B]
                    nout_ref[bi, d, h] = c_scr[bi, d, :, h * CAUG_W + DV_B:(h + 1) * CAUG_W]
            mout_ref[...] = m_scr[...]


def _mlstm(rest, gates, norm_b, c0, m0, batch, n_chunks, emit_state, casts=()):
    t = n_chunks * CHUNK
    has_init = c0 is not None
    bb = SCAN_BB
    rest, gates = (a.reshape(batch, t, a.shape[-1]) for a in (rest, gates))
    mspec = pl.BlockSpec((bb, N_DIR, SUBLANES, LANES), lambda b, n: (b, 0, 0, 0))
    in_specs = [*_scan_specs(bb, n_chunks, REST_W), *_scan_specs(bb, n_chunks, LANES),
                pl.BlockSpec((1, DV_B), lambda b, n: (0, 0))]
    args = [rest, rest, gates, gates, norm_b]
    if has_init:
        in_specs += [pl.BlockSpec((bb, N_DIR, DK_B, MW), lambda b, n: (b, 0, 0, 0)), mspec]
        args += [c0, m0]
    out_specs = [pl.BlockSpec((bb, t, B_W), lambda b, n: (b, 0, 0))]
    out_shape = [jax.ShapeDtypeStruct((batch, t, B_W), BF16)]
    if emit_state:
        out_specs += [pl.BlockSpec((bb, N_DIR, H_B, DK_B, DV_B), lambda b, n: (b, 0, 0, 0, 0)),
                      pl.BlockSpec((bb, N_DIR, H_B, DK_B, LANES), lambda b, n: (b, 0, 0, 0, 0)), mspec]
        out_shape += [jax.ShapeDtypeStruct((batch, N_DIR, H_B, DK_B, DV_B), F32),
                      jax.ShapeDtypeStruct((batch, N_DIR, H_B, DK_B, LANES), F32),
                      jax.ShapeDtypeStruct((batch, N_DIR, SUBLANES, LANES), F32)]
    steps = (batch // bb) * n_chunks
    c_in, c_out, c_shape, c_args = _cast_specs([(w, l, steps) for w, l in casts], lambda b, n: b * n_chunks + n)
    in_specs, args, out_specs, out_shape = in_specs + c_in, args + c_args, out_specs + c_out, out_shape + c_shape
    out = pl.pallas_call(
        functools.partial(_mlstm_body, n_chunks=n_chunks, bb=bb, has_init=has_init, emit_state=emit_state,
                          n_cast=len(casts)),
        grid=(batch // bb, n_chunks),
        in_specs=in_specs,
        out_specs=out_specs,
        out_shape=out_shape,
        scratch_shapes=[pltpu.VMEM((bb, N_DIR, DK_B, MW), F32),
                        pltpu.VMEM((bb, N_DIR, SUBLANES, LANES), F32),
                        pltpu.VMEM((bb, t, B_W), F32),
                        pltpu.VMEM((bb * N_DIR, H_B * CHUNK, H_B * DK_B), BF16),
                        pltpu.VMEM((bb * N_DIR, H_B * DK_B, MW), BF16),
                        pltpu.VMEM((bb * N_DIR, H_B * CHUNK, MW), BF16)],
        compiler_params=_params(("arbitrary", "arbitrary")),
        name="mlstm_scan",
    )(*args)
    return [out[0].reshape(batch * t, B_W), *out[1:]]


OUT_TM = 512
OUT_RB = 256
FFN_TM = 512
FFN_FC = 1024


def _outproj_body(ya_ref, yb_ref, wa_ref, wb_ref, x_ref, mod_ref, post1_ref, pre2_ref, x1_ref, h2_ref):
    for rb in range(x_ref.shape[0] // OUT_RB):
        rs = slice(rb * OUT_RB, (rb + 1) * OUT_RB)
        mix = (jnp.dot(ya_ref[rs, :], wa_ref[...], preferred_element_type=F32)
               + jnp.dot(yb_ref[rs, :], wb_ref[...], preferred_element_type=F32))
        x1 = x_ref[rs, :] + mod_ref[0, 2:3, :] * (_rms(mix) * post1_ref[...])
        x1_ref[rs, :] = x1
        h2 = _rms(x1) * pre2_ref[...] * (1.0 + mod_ref[0, 4:5, :]) + mod_ref[0, 3:4, :]
        h2_ref[rs, :] = h2.astype(BF16)


def _outproj(ya, yb, w_out, x2d, mod3, mod_map, post1, pre2, tm):
    m = x2d.shape[0]
    row = lambda i: (i, 0)
    const = lambda i: (0, 0)
    return pl.pallas_call(
        _outproj_body,
        grid=(m // tm,),
        in_specs=[pl.BlockSpec((tm, A_W), row), pl.BlockSpec((tm, B_W), row),
                  pl.BlockSpec((A_W, D_MODEL), lambda i: (0, 0)),
                  pl.BlockSpec((B_W, D_MODEL), lambda i: (1, 0)),
                  pl.BlockSpec((tm, D_MODEL), row),
                  pl.BlockSpec((1, 6, D_MODEL), lambda i: (mod_map(i * tm), 0, 0)),
                  pl.BlockSpec((1, D_MODEL), const), pl.BlockSpec((1, D_MODEL), const)],
        out_specs=[pl.BlockSpec((tm, D_MODEL), row), pl.BlockSpec((tm, D_MODEL), row)],
        out_shape=[jax.ShapeDtypeStruct((m, D_MODEL), F32), jax.ShapeDtypeStruct((m, D_MODEL), BF16)],
        compiler_params=_params(("parallel",)),
        name="outproj",
    )(ya, yb, w_out, w_out, x2d, mod3, post1, pre2)


def _ffn_body(h2_ref, w1_ref, w2_ref, x1_ref, mod_ref, post2_ref, o_ref):
    kk = pl.program_id(1)

    @pl.when(kk == 0)
    def _():
        o_ref[...] = jnp.zeros_like(o_ref)

    a = jnp.maximum(jnp.dot(h2_ref[...], w1_ref[...], preferred_element_type=F32), 0.0)
    o_ref[...] += jnp.dot((a * a).astype(BF16), w2_ref[...], preferred_element_type=F32)

    @pl.when(kk == pl.num_programs(1) - 1)
    def _():
        o_ref[...] = x1_ref[...] + mod_ref[0, 5:6, :] * (_rms(o_ref[...]) * post2_ref[...])


def _ffn(h2, w1, w2, x1, mod3, mod_map, post2, tm, fc):
    m = h2.shape[0]
    return pl.pallas_call(
        _ffn_body,
        grid=(m // tm, FFN // fc),
        in_specs=[pl.BlockSpec((tm, D_MODEL), lambda i, k: (i, 0)),
                  pl.BlockSpec((D_MODEL, fc), lambda i, k: (0, k)),
                  pl.BlockSpec((fc, D_MODEL), lambda i, k: (k, 0)),
                  pl.BlockSpec((tm, D_MODEL), lambda i, k: (i, 0)),
                  pl.BlockSpec((1, 6, D_MODEL), lambda i, k: (mod_map(i * tm), 0, 0)),
                  pl.BlockSpec((1, D_MODEL), lambda i, k: (0, 0))],
        out_specs=pl.BlockSpec((tm, D_MODEL), lambda i, k: (i, 0)),
        out_shape=jax.ShapeDtypeStruct((m, D_MODEL), F32),
        compiler_params=_params(("parallel", "arbitrary")),
        name="ffn",
    )(h2, w1, w2, x1, mod3, post2)


def _block(x, mod3, mod_of_row, lp, init, seq_len, emit_state):
    bsz, t, _ = x.shape
    x2d = x.reshape(bsz * t, D_MODEL)
    n_chunks = t // CHUNK
    qkv, rest, gates = _proj(x2d, mod3, lambda i: mod_of_row(i * PROJ_TM), lp["pre1"], lp["w16"],
                             lp["w_mid"], lp["w_gate"], lp["conv_w"], lp["gate_p"], seq_len)
    s0, c0, m0 = init if init is not None else (None, None, None)
    first = "w1" not in lp
    l = lp["l"]
    d_out = _delta(qkv, rest, gates, lp["norm_a"], s0, bsz, n_chunks, emit_state,
                   [(lp["w_ffn1"], l), (lp["w_out32"], l)] if first else ())
    m_out = _mlstm(rest, gates, lp["norm_b"], c0, m0, bsz, n_chunks, emit_state,
                   [(lp["w_ffn2"], l)] if first else ())
    if first:
        (lp["w1"], lp["w_out"]), lp["w2"] = d_out[-2:], m_out[-1]
    x1, h2 = _outproj(d_out[0], m_out[0], lp["w_out"], x2d, mod3, mod_of_row, lp["post1"], lp["pre2"], OUT_TM)
    y = _ffn(h2, lp["w1"], lp["w2"], x1, mod3, mod_of_row, lp["post2"], FFN_TM, FFN_FC)
    states = None
    if emit_state:
        m_fin = jnp.stack([m_out[3][:, d, 0, LF_OFF + d * H_B:LF_OFF + (d + 1) * H_B] for d in range(N_DIR)], axis=1)
        states = (d_out[1], m_out[1], m_out[2][..., 0], m_fin)
    return y.reshape(bsz, t, D_MODEL), states


def _layer_params(l, w16, w_mid, norm_mix_pre, norm_mix_post, norm_ffn_pre, norm_ffn_post, conv_w, a_log, dt_bias,
                  norm_a, mlstm_ibias, mlstm_fbias, norm_b, w_out, w_ffn1, w_ffn2):
    n_gate = O_BQ - O_AA + O_END - O_BI
    w_gate = jnp.concatenate([w16[:, O_AA:O_BQ], w16[:, O_BI:O_END], jnp.zeros((D_MODEL, LANES - n_gate), BF16)],
                             axis=1)

    def lane_row(vals, off):
        return jnp.zeros((LANES,), F32).at[off:off + vals.size].set(vals.reshape(-1))

    gate_p = jnp.stack([lane_row(a_log[l], G_OFF), lane_row(dt_bias[l], G_OFF),
                        lane_row(mlstm_ibias[l], LI_OFF) + lane_row(mlstm_fbias[l], LF_OFF)]
                       + [jnp.zeros((LANES,), F32)] * (SUBLANES - 3))
    row = lambda v: v[l].reshape(1, -1)
    return dict(
        pre1=row(norm_mix_pre), post1=row(norm_mix_post), pre2=row(norm_ffn_pre), post2=row(norm_ffn_post),
        w16=w16, w_mid=w_mid, w_gate=w_gate, gate_p=gate_p,
        conv_w=jnp.concatenate([conv_w[l].T, jnp.zeros((SUBLANES - 3, QKV_W), F32)], axis=0),
        norm_a=row(norm_a), norm_b=row(norm_b),
        l=l, w_out32=w_out, w_ffn1=w_ffn1, w_ffn2=w_ffn2)


def kernel(x_prompt, x_sample, state_delta, state_mlstm_C, state_mlstm_n, state_mlstm_m, c, c_ctx, w_ada, b_ada, norm_mix_pre, norm_mix_post, norm_ffn_pre, norm_ffn_post, w_in, conv_w, a_log, dt_bias, norm_a, mlstm_ibias, mlstm_fbias, norm_b, w_out, w_ffn1, w_ffn2):
    depth = w_in.shape[0]
    n_lat = x_sample.shape[0]
    t_lat = x_sample.shape[1]
    cond = jnp.concatenate([c_ctx[None, :], c, jnp.zeros((SUBLANES - 1 - n_lat, D_MODEL), F32)], axis=0)
    y_prompt, y_sample = x_prompt, x_sample
    acc = ([], [], [], [])
    for l in range(depth):
        mod, w16, w_mid = _ada(cond, w_ada, b_ada[l].reshape(1, -1), w_in, l)
        lp = _layer_params(l, w16, w_mid, norm_mix_pre, norm_mix_post, norm_ffn_pre, norm_ffn_post, conv_w, a_log,
                           dt_bias, norm_a, mlstm_ibias, mlstm_fbias, norm_b, w_out, w_ffn1, w_ffn2)
        mod3 = mod[:1 + n_lat].reshape(1 + n_lat, 6, D_MODEL)
        y_prompt, st = _block(y_prompt, mod3, lambda r: 0, lp, None, x_prompt.shape[1], True)
        for a, s in zip(acc, st):
            a.append(s)
        n_rep = jnp.broadcast_to(state_mlstm_n[:, l][..., None], state_mlstm_n[:, l].shape + (LANES,))
        c_aug0 = jnp.concatenate([state_mlstm_C[:, l], n_rep], axis=-1)
        c_aug0 = c_aug0.transpose(0, 1, 3, 2, 4).reshape(n_lat, N_DIR, DK_B, MW)
        m0 = jnp.zeros((n_lat, N_DIR, LANES), F32)
        for d in range(N_DIR):
            m0 = m0.at[:, d, LF_OFF + d * H_B:LF_OFF + (d + 1) * H_B].set(state_mlstm_m[:, l, d])
        m0 = jnp.broadcast_to(m0[:, :, None, :], (n_lat, N_DIR, SUBLANES, LANES))
        y_sample, _ = _block(y_sample, mod3, lambda r: 1 + r // t_lat, lp, (state_delta[:, l], c_aug0, m0),
                             GRID_W, False)
    return (y_prompt, y_sample) + tuple(jnp.stack(a, axis=1) for a in acc)
```

```python
import functools

import jax
import jax.numpy as jnp
from jax import lax
from jax.experimental import pallas as pl
from jax.experimental.pallas import tpu as pltpu

F32 = jnp.float32
BF16 = jnp.bfloat16

D_MODEL = 2048
N_DIR = 2
A_W = D_MODEL // 2
B_W = D_MODEL - A_W
DK_A = 128
DV_A = 128
H_A = A_W // DV_A
DV_B = 256
DK_B = DV_B // 2
H_B = B_W // DV_B
GRID_W = 64
CHUNK = 64
FFN = 4 * D_MODEL
EPS = 1e-6
LANES = 128
SUBLANES = 8
NEG = -1e30

QKV_W = 3 * A_W
REST_W = A_W + 2 * H_B * DK_B + 2 * B_W
OFF_AG, OFF_BQ, OFF_BK, OFF_BV, OFF_BO = 0, A_W, A_W + H_B * DK_B, A_W + 2 * H_B * DK_B, A_W + 2 * H_B * DK_B + B_W
G_OFF, BETA_OFF = 0, N_DIR * H_A
LI_OFF = 2 * N_DIR * H_A
LF_OFF = LI_OFF + N_DIR * H_B

VMEM_LIMIT = 56 * 1024 * 1024


def _sigmoid(x):
    return 1.0 / (1.0 + jnp.exp(-x))


def _softplus(x):
    return jnp.maximum(x, 0.0) + jnp.log1p(jnp.exp(-jnp.abs(x)))


def _dot(a, b):
    return jnp.dot(a.astype(BF16), b.astype(BF16), preferred_element_type=F32)


def _dot_f32(a, b):
    return jnp.dot(a, b, precision=lax.Precision.HIGHEST, preferred_element_type=F32)


def _rms(x):
    return x * lax.rsqrt(jnp.mean(x * x, axis=-1, keepdims=True) + EPS)


def _params(sem):
    return pltpu.CompilerParams(dimension_semantics=sem, vmem_limit_bytes=VMEM_LIMIT)


def _ada_body(c_ref, w_ref, b_ref, o_ref):
    c = c_ref[...]
    o_ref[...] = _dot(c * _sigmoid(c), w_ref[...]) + b_ref[...]


def _ada(c_all, w_ada, b, l):
    n = w_ada.shape[2]
    tn = 1024
    return pl.pallas_call(
        _ada_body,
        grid=(n // tn,),
        in_specs=[pl.BlockSpec(c_all.shape, lambda j: (0, 0)),
                  pl.BlockSpec((None, D_MODEL, tn), lambda j: (l, 0, j)),
                  pl.BlockSpec((1, tn), lambda j: (0, j))],
        out_specs=pl.BlockSpec((c_all.shape[0], tn), lambda j: (0, j)),
        out_shape=jax.ShapeDtypeStruct((c_all.shape[0], n), F32),
        compiler_params=_params(("arbitrary",)),
        name="ada",
    )(c_all, w_ada, b)


PROJ_TM = 1024
PROJ_RB = 512
PROJ_CB = 256
PROJ_TN = 512
N_QKV_T = QKV_W // PROJ_TN
N_REST_T = REST_W // PROJ_TN
N_HEAD_T = (QKV_W + A_W) // PROJ_TN
N_MID_T = N_QKV_T + N_REST_T - N_HEAD_T


def _proj_body(x_ref, mod_ref, g_ref, wh_ref, wb_ref, wg_ref, cw_ref, gp_ref, qkv_ref, rest_ref, gate_ref, h_scr, *,
               seq_len):
    j = pl.program_id(1)
    row_blocks = [slice(rb * PROJ_RB, (rb + 1) * PROJ_RB) for rb in range(PROJ_TM // PROJ_RB)]
    col_blocks = [slice(cb * PROJ_CB, (cb + 1) * PROJ_CB) for cb in range(PROJ_TN // PROJ_CB)]

    def run_rows(rs, w_ref, epilogue, out_ref):
        for cs in col_blocks:
            epilogue(jnp.dot(h_scr[rs, :], w_ref[:, cs], preferred_element_type=F32), out_ref, rs, cs)

    def run(w_ref, epilogue, out_ref):
        for rs in row_blocks:
            run_rows(rs, w_ref, epilogue, out_ref)

    def run_first():
        for rs in row_blocks:
            y = _rms(x_ref[rs, :]) * g_ref[...]
            h_scr[rs, :] = (y * (1.0 + mod_ref[0, 1:2, :]) + mod_ref[0, 0:1, :]).astype(BF16)
            run_rows(rs, wh_ref, ep_l2(DK_A ** -0.5), qkv_ref)
            ep_gate(jnp.dot(h_scr[rs, :], wg_ref[...], preferred_element_type=F32), gate_ref, rs)

    def conv_silu(acc, cs, g):
        a = acc[:, g * LANES:(g + 1) * LANES]
        ls = slice(cs.start + g * LANES, cs.start + (g + 1) * LANES)
        pos = lax.broadcasted_iota(jnp.int32, a.shape, 0) & (seq_len - 1)
        prev = jnp.where(pos == 0, 0.0, pltpu.roll(a, 1, 0))
        nxt = jnp.where(pos == seq_len - 1, 0.0, pltpu.roll(a, a.shape[0] - 1, 0))
        y = prev * cw_ref[0:1, ls] + a * cw_ref[1:2, ls] + nxt * cw_ref[2:3, ls]
        return ls, y * _sigmoid(y)

    def ep_l2(scale):
        def f(acc, out_ref, rs, cs):
            for g in range(PROJ_CB // LANES):
                ls, blk = conv_silu(acc, cs, g)
                inv = lax.rsqrt(jnp.sum(blk * blk, axis=-1, keepdims=True) + EPS)
                out_ref[rs, ls] = blk * (inv * scale)
        return f

    def ep_conv(acc, out_ref, rs, cs):
        for g in range(PROJ_CB // LANES):
            ls, blk = conv_silu(acc, cs, g)
            out_ref[rs, ls] = blk

    def ep_map(fn):
        def f(acc, out_ref, rs, cs):
            out_ref[rs, cs] = fn(acc)
        return f

    def ep_gate(z, out_ref, rs):
        lane = lax.broadcasted_iota(jnp.int32, z.shape, 1)
        g = -jnp.exp(gp_ref[0:1, :]) * _softplus(z + gp_ref[1:2, :])
        li = z + gp_ref[2:3, :]
        out_ref[rs, :] = jnp.where(lane < BETA_OFF, g,
                                   jnp.where(lane < LI_OFF, _sigmoid(z),
                                             jnp.where(lane < LF_OFF, li,
                                                       jnp.where(lane < LF_OFF + N_DIR * H_B, -_softplus(-li), 0.0))))

    tp = A_W // PROJ_TN
    c0 = (j - N_QKV_T) * PROJ_TN
    in_rest = (j >= N_QKV_T) & (j < N_QKV_T + N_REST_T)
    pl.when(j == 0)(run_first)
    variants = [
        ((j > 0) & (j < tp), wh_ref, ep_l2(DK_A ** -0.5), qkv_ref),
        ((j >= tp) & (j < 2 * tp), wh_ref, ep_l2(1.0), qkv_ref),
        ((j >= 2 * tp) & (j < N_QKV_T), wh_ref, ep_conv, qkv_ref),
        (in_rest & (c0 < OFF_BQ), wh_ref, ep_map(lambda a: a * _sigmoid(a)), rest_ref),
        (in_rest & (c0 >= OFF_BQ) & (c0 < OFF_BK), wb_ref, ep_map(lambda a: a * (DK_B ** -0.5)), rest_ref),
        (in_rest & (c0 >= OFF_BK) & (c0 < OFF_BO), wb_ref, ep_map(lambda a: a), rest_ref),
        (in_rest & (c0 >= OFF_BO), wb_ref, ep_map(_sigmoid), rest_ref),
    ]
    for cond, w_ref, epilogue, out_ref in variants:
        pl.when(cond)(functools.partial(run, w_ref, epilogue, out_ref))


def _proj(x2d, mod3, mod_map, pre_g, w16, w_mid, w_gate, conv_w, gate_p, seq_len):
    m = x2d.shape[0]
    tm, tn = PROJ_TM, PROJ_TN
    assert PROJ_RB % seq_len == 0 and m % tm == 0
    return pl.pallas_call(
        functools.partial(_proj_body, seq_len=seq_len),
        grid=(m // tm, N_HEAD_T + N_MID_T),
        in_specs=[pl.BlockSpec((tm, D_MODEL), lambda i, j: (i, 0)),
                  pl.BlockSpec((1, 6, D_MODEL), lambda i, j: (mod_map(i), 0, 0)),
                  pl.BlockSpec((1, D_MODEL), lambda i, j: (0, 0)),
                  pl.BlockSpec((D_MODEL, tn), lambda i, j: (0, jnp.minimum(j, N_HEAD_T - 1))),
                  pl.BlockSpec((D_MODEL, tn), lambda i, j: (0, jnp.clip(j - N_HEAD_T, 0, N_MID_T - 1))),
                  pl.BlockSpec((D_MODEL, LANES), lambda i, j: (0, 0)),
                  pl.BlockSpec((SUBLANES, tn), lambda i, j: (0, jnp.minimum(j, N_QKV_T - 1))),
                  pl.BlockSpec((SUBLANES, LANES), lambda i, j: (0, 0))],
        out_specs=[pl.BlockSpec((tm, tn), lambda i, j: (i, jnp.minimum(j, N_QKV_T - 1))),
                   pl.BlockSpec((tm, tn), lambda i, j: (i, jnp.clip(j - N_QKV_T, 0, N_REST_T - 1))),
                   pl.BlockSpec((tm, LANES), lambda i, j: (i, 0))],
        out_shape=[jax.ShapeDtypeStruct((m, QKV_W), F32), jax.ShapeDtypeStruct((m, REST_W), F32),
                   jax.ShapeDtypeStruct((m, LANES), F32)],
        scratch_shapes=[pltpu.VMEM((tm, D_MODEL), BF16)],
        compiler_params=_params(("parallel", "arbitrary")),
        name="proj",
    )(x2d, mod3, pre_g, w16, w_mid, w_gate, conv_w, gate_p)


def _masks(d):
    row = lax.broadcasted_iota(jnp.int32, (CHUNK, CHUNK), 0)
    col = lax.broadcasted_iota(jnp.int32, (CHUNK, CHUNK), 1)
    if d == 0:
        return row >= col, row > col, row == col
    return row <= col, row < col, row == col


GROUP = 4
N_GROUPS = H_A // GROUP
PAIR = LANES // CHUNK


def _split(x):
    hi = x.astype(BF16)
    return hi, (x - hi.astype(F32)).astype(BF16)


def _store_blocks(ref, x, blk_r, blk_c, col0=0):
    for h in range(GROUP):
        ref[h * blk_r:(h + 1) * blk_r, col0 + h * blk_c:col0 + (h + 1) * blk_c] = x[:, h * blk_c:(h + 1) * blk_c]


def _dot3(a_hi, a_lo, b_hi, b_lo):
    m = a_hi.shape[0]
    r = jnp.dot(jnp.concatenate([a_hi, a_lo], axis=0), b_hi, preferred_element_type=F32)
    return r[:m] + r[m:] + jnp.dot(a_hi, b_lo, preferred_element_type=F32)


HALF = CHUNK // 2


def _tri_inverse_wide(lmats, upper, bd_hi, bd_lo, by_hi, by_lo):
    width = lmats[0].shape[1]
    n_blk = width // HALF
    row = lax.broadcasted_iota(jnp.int32, (HALF, width), 0)
    lane = lax.broadcasted_iota(jnp.int32, (HALF, width), 1)
    lead = (lane & (CHUNK - 1)) < HALF
    eye_d = (row == (lane & (HALF - 1))).astype(F32)

    def bd_dot(i, a, b_hi, b_lo):
        for blk in range(n_blk):
            sl = slice(blk * HALF, (blk + 1) * HALF)
            bd_hi[i, sl, sl] = b_hi[:, sl]
            bd_lo[i, sl, sl] = b_lo[:, sl]
        return _dot3(*a, bd_hi[i], bd_lo[i])

    idx = range(len(lmats))
    diag = [jnp.where(lead, l[:HALF], l[HALF:]) for l in lmats]
    s = [eye_d - dg for dg in diag]
    p = []
    for i in idx:
        m_hi, m_lo = _split(-diag[i])
        p.append(bd_dot(i, (m_hi, m_lo), m_hi, m_lo))
    for _ in range(3):
        for i in idx:
            p_hi, p_lo = _split(p[i])
            s_hi, s_lo = _split(s[i])
            r = bd_dot(i, (jnp.concatenate([p_hi, s_hi], axis=0), jnp.concatenate([p_lo, s_lo], axis=0)), p_hi, p_lo)
            p[i] = r[:HALF]
            s[i] = s[i] + r[HALF:]
    for i in idx:
        p_hi, p_lo = _split(p[i])
        s[i] = s[i] + bd_dot(i, _split(s[i]), p_hi, p_lo)
    y = []
    for i in idx:
        c_blk = jnp.where(lead, 0.0, lmats[i][:HALF]) if upper[i] else jnp.where(lead, lmats[i][HALF:], 0.0)
        y.append(bd_dot(i, _split(c_blk), *_split(s[i])))
    out = []
    for i in idx:
        y_hi, y_lo = _split(y[i])
        for h in range(width // CHUNK):
            a_sl, b_sl = slice(h * CHUNK, h * CHUNK + HALF), slice(h * CHUNK + HALF, (h + 1) * CHUNK)
            rs, cs = (a_sl, b_sl) if upper[i] else (b_sl, a_sl)
            by_hi[i, rs, cs] = y_hi[:, cs]
            by_lo[i, rs, cs] = y_lo[:, cs]
        ai, bi = jnp.where(lead, s[i], 0.0), jnp.where(lead, 0.0, s[i])
        x = _dot3(*_split(ai if upper[i] else bi), by_hi[i], by_lo[i])
        out.append(jnp.concatenate([ai - x, bi] if upper[i] else [ai, bi - x], axis=0))
    return out


def _col_bcast(tile, c, width=LANES):
    return jnp.broadcast_to(tile[:, c:c + 1], (tile.shape[0], width))


def _delta_body(*refs, n_chunks, bb, has_init, emit_state, n_cast):
    qkv = refs[0:2]
    ag = refs[2:4]
    gt = refs[4:6]
    norm_ref = refs[6]
    pos = 7
    s0_ref = None
    if has_init:
        s0_ref = refs[pos]
        pos += 1
    cast_in = refs[pos:pos + n_cast]
    pos += n_cast
    ya_ref = refs[pos]
    pos += 1
    sout_ref = None
    if emit_state:
        sout_ref = refs[pos]
        pos += 1
    for w_ref, w16_ref in zip(cast_in, refs[pos:pos + n_cast]):
        w16_ref[...] = w_ref[...].astype(BF16)
    pos += n_cast
    s_scr, o_scr, bdn_hi, bdn_lo, bdy_hi, bdy_lo, bdk, bduw, bds, bdv = refs[pos:pos + 10]

    n = pl.program_id(1)
    gw = GROUP * DK_A

    @pl.when((n == 0) & (pl.program_id(0) == 0))
    def _():
        for ref in (bdn_hi, bdn_lo, bdy_hi, bdy_lo, bdk, bduw, bds, bdv):
            ref[...] = jnp.zeros_like(ref)

    @pl.when(n == 0)
    def _():
        o_scr[...] = jnp.zeros_like(o_scr)
        for bi in range(bb):
            for d in range(N_DIR):
                for h in range(H_A):
                    blk = s0_ref[bi, d, h] if has_init else jnp.zeros((DK_A, DV_A), F32)
                    s_scr[bi, d, h // GROUP, :, (h % GROUP) * DV_A:(h % GROUP + 1) * DV_A] = blk

    row = lax.broadcasted_iota(jnp.int32, (CHUNK, LANES), 0)
    lane = lax.broadcasted_iota(jnp.int32, (CHUNK, LANES), 1)
    col = lane & (CHUNK - 1)
    left = lane < CHUNK
    probs = [(bi, d, g) for bi in range(bb) for d in range(N_DIR) for g in range(N_GROUPS)]
    masks = {0: (row >= col, row > col), 1: (row <= col, row < col)}
    gates, gc, gc_t, eg, egl = {}, {}, {}, {}, {}
    for bi in range(bb):
        for d in range(N_DIR):
            key = (bi, d)
            gates[key] = gt[d][bi]
            gc[key] = _dot_f32(_masks(d)[0].astype(F32), gates[key])
            gc_t[key] = jnp.concatenate([gc[key], gc[key]], axis=0).T
            last = CHUNK - 1 if d == 0 else 0
            eg[key] = jnp.exp(gc[key])
            egl[key] = jnp.exp(gc[key][last:last + 1, :] - gc[key])

    def cols_of(d, g):
        return [G_OFF + d * H_A + g * GROUP + hl for hl in range(GROUP)]

    q16, ks, beta_xs, decays, grams = [], [], [], [], []
    for gi, (bi, d, g) in enumerate(probs):
        key = (bi, d)
        cols = cols_of(d, g)
        incl, _ = masks[d]
        k = qkv[d][bi, :, A_W + g * gw:A_W + (g + 1) * gw]
        beta_x = jnp.concatenate([_col_bcast(gates[key], BETA_OFF - G_OFF + c) for c in cols], axis=1)
        decay = []
        for p in range(GROUP // PAIR):
            c0, c1 = cols[PAIR * p], cols[PAIR * p + 1]
            gcol = jnp.where(left, _col_bcast(gc[key], c0), _col_bcast(gc[key], c1))
            grow = jnp.where(left[0:1], gc_t[key][c0:c0 + 1, :], gc_t[key][c1:c1 + 1, :])
            decay.append(jnp.exp(jnp.where(incl, gcol - grow, NEG)))
        decays.append(jnp.concatenate(decay, axis=1))
        q16.append(qkv[d][bi, :, g * gw:(g + 1) * gw].astype(BF16))
        ks.append(k)
        beta_xs.append(beta_x)
        _store_blocks(bdk.at[gi], k.astype(BF16), CHUNK, DK_A)
    for gi in range(len(probs)):
        grams.append(lax.dot_general(jnp.concatenate([q16[gi], (ks[gi] * beta_xs[gi]).astype(BF16)], axis=0),
                                     bdk[gi], (((1,), (1,)), ((), ())), preferred_element_type=F32))
    attns, lmats = [], []
    for gi, (bi, d, g) in enumerate(probs):
        strict_w = jnp.concatenate([masks[d][1]] * (GROUP // PAIR), axis=1)
        attns.append((grams[gi][:CHUNK] * decays[gi]).astype(BF16))
        lmats.append(jnp.where(strict_w, grams[gi][CHUNK:] * decays[gi], 0.0))
    ainvs = _tri_inverse_wide(lmats, [d == 1 for _, d, _ in probs], bdn_hi, bdn_lo, bdy_hi, bdy_lo)
    eg_xs, kd_ts = [], []
    for gi, (bi, d, g) in enumerate(probs):
        key = (bi, d)
        cols = cols_of(d, g)
        v = qkv[d][bi, :, 2 * A_W + g * gw:2 * A_W + (g + 1) * gw]
        eg_x = jnp.concatenate([_col_bcast(eg[key], c) for c in cols], axis=1)
        egl_x = jnp.concatenate([_col_bcast(egl[key], c) for c in cols], axis=1)
        eg_xs.append(eg_x)
        _store_blocks(bduw.at[gi], (v * beta_xs[gi]).astype(BF16), CHUNK, DV_A)
        _store_blocks(bduw.at[gi], (ks[gi] * (beta_xs[gi] * eg_x)).astype(BF16), CHUNK, DK_A, col0=gw)
        kd = ks[gi] * egl_x
        kd_ts.append(jnp.concatenate([kd[:, hl * DK_A:(hl + 1) * DK_A] for hl in range(GROUP)], axis=0)
                     .T.astype(BF16))
        _store_blocks(bds.at[gi], s_scr[bi, d, g].astype(BF16), DK_A, DV_A)
    uws = []
    for gi in range(len(probs)):
        t_hi, t_lo = _split(ainvs[gi])
        r = jnp.dot(jnp.concatenate([t_hi, t_lo], axis=0), bduw[gi], preferred_element_type=F32)
        uws.append(r[:CHUNK] + r[CHUNK:])
    ws_qs = []
    hw = gw // 2
    for gi in range(len(probs)):
        wq = jnp.concatenate([uws[gi][:, gw:].astype(BF16), q16[gi]], axis=0)
        ws_qs.append(jnp.concatenate(
            [jnp.dot(wq[:, p * hw:(p + 1) * hw], bds[gi, p * hw:(p + 1) * hw, p * hw:(p + 1) * hw],
                     preferred_element_type=F32) for p in range(2)], axis=1))
    for gi in range(len(probs)):
        _store_blocks(bdv.at[gi], (uws[gi][:, :gw] - ws_qs[gi][:CHUNK]).astype(BF16), CHUNK, DV_A)
    rs = []
    for gi in range(len(probs)):
        rs.append(jnp.dot(jnp.concatenate([attns[gi], kd_ts[gi]], axis=0), bdv[gi], preferred_element_type=F32))
    tots = []
    for gi, (bi, d, g) in enumerate(probs):
        cols = cols_of(d, g)
        last = CHUNK - 1 if d == 0 else 0
        o = ws_qs[gi][CHUNK:] * eg_xs[gi] + rs[gi][:CHUNK]
        eg_last = jnp.concatenate(
            [jnp.broadcast_to(eg[bi, d][last:last + 1, c:c + 1], (1, DV_A)) for c in cols], axis=1)
        s_scr[bi, d, g] = s_scr[bi, d, g] * eg_last + rs[gi][CHUNK:]
        cidx = n if d == 0 else n_chunks - 1 - n
        rows = pl.ds(pl.multiple_of(cidx * CHUNK, CHUNK), CHUNK)
        tot = o + o_scr[bi, rows, g * gw:(g + 1) * gw]
        o_scr[bi, rows, g * gw:(g + 1) * gw] = tot
        tots.append((rows, tot))
    blks = [(bi, d, rows, g * GROUP + hl, tot[:, hl * DV_A:(hl + 1) * DV_A])
            for (bi, d, g), (rows, tot) in zip(probs, tots) for hl in range(GROUP)]
    inv = [lax.rsqrt(jnp.mean(blk * blk, axis=-1, keepdims=True) + EPS) for *_, blk in blks]
    for (bi, d, rows, h, blk), r in zip(blks, inv):
        hs = slice(h * DV_A, (h + 1) * DV_A)
        ya_ref[bi, rows, hs] = (blk * r * norm_ref[...] * ag[d][bi, :, hs]).astype(BF16)

    if emit_state:
        @pl.when(n == n_chunks - 1)
        def _():
            for bi in range(bb):
                for d in range(N_DIR):
                    for h in range(H_A):
                        sout_ref[bi, d, h] = s_scr[bi, d, h // GROUP, :, (h % GROUP) * DV_A:(h % GROUP + 1) * DV_A]


SCAN_BB = 2


def _scan_specs(bb, n_chunks, width):
    return (pl.BlockSpec((bb, CHUNK, width), lambda b, n: (b, n, 0)),
            pl.BlockSpec((bb, CHUNK, width), lambda b, n: (b, n_chunks - 1 - n, 0)))


def _cast_specs(casts, steps_of):
    in_specs, out_specs, out_shape, args = [], [], [], []
    for w, l, steps in casts:
        rows = w.shape[1] // steps
        assert rows * steps == w.shape[1] and rows % (2 * SUBLANES) == 0
        in_specs.append(pl.BlockSpec((None, rows, w.shape[2]), lambda b, n, l=l: (l, steps_of(b, n), 0)))
        out_specs.append(pl.BlockSpec((rows, w.shape[2]), lambda b, n: (steps_of(b, n), 0)))
        out_shape.append(jax.ShapeDtypeStruct(w.shape[1:], BF16))
        args.append(w)
    return in_specs, out_specs, out_shape, args


def _delta(qkv, rest, gates, norm_a, s0, batch, n_chunks, emit_state, casts=()):
    t = n_chunks * CHUNK
    has_init = s0 is not None
    bb = SCAN_BB
    ng = bb * N_DIR * N_GROUPS
    qkv, rest, gates = (a.reshape(batch, t, a.shape[-1]) for a in (qkv, rest, gates))
    in_specs = [*_scan_specs(bb, n_chunks, QKV_W), *_scan_specs(bb, n_chunks, A_W), *_scan_specs(bb, n_chunks, LANES),
                pl.BlockSpec((1, DV_A), lambda b, n: (0, 0))]
    args = [qkv, qkv, rest, rest, gates, gates, norm_a]
    sspec = pl.BlockSpec((bb, N_DIR, H_A, DK_A, DV_A), lambda b, n: (b, 0, 0, 0, 0))
    if has_init:
        in_specs.append(sspec)
        args.append(s0)
    out_specs = [pl.BlockSpec((bb, t, A_W), lambda b, n: (b, 0, 0))]
    out_shape = [jax.ShapeDtypeStruct((batch, t, A_W), BF16)]
    if emit_state:
        out_specs.append(sspec)
        out_shape.append(jax.ShapeDtypeStruct((batch, N_DIR, H_A, DK_A, DV_A), F32))
    steps = (batch // bb) * n_chunks
    c_in, c_out, c_shape, c_args = _cast_specs([(w, l, steps) for w, l in casts], lambda b, n: b * n_chunks + n)
    in_specs, args, out_specs, out_shape = in_specs + c_in, args + c_args, out_specs + c_out, out_shape + c_shape
    out = pl.pallas_call(
        functools.partial(_delta_body, n_chunks=n_chunks, bb=bb, has_init=has_init, emit_state=emit_state,
                          n_cast=len(casts)),
        grid=(batch // bb, n_chunks),
        in_specs=in_specs,
        out_specs=out_specs,
        out_shape=out_shape,
        scratch_shapes=[pltpu.VMEM((bb, N_DIR, N_GROUPS, DK_A, GROUP * DV_A), F32), pltpu.VMEM((bb, t, A_W), F32),
                        pltpu.VMEM((ng, GROUP * CHUNK, GROUP * CHUNK), BF16),
                        pltpu.VMEM((ng, GROUP * CHUNK, GROUP * CHUNK), BF16),
                        pltpu.VMEM((ng, GROUP * CHUNK, GROUP * CHUNK), BF16),
                        pltpu.VMEM((ng, GROUP * CHUNK, GROUP * CHUNK), BF16),
                        pltpu.VMEM((ng, GROUP * CHUNK, GROUP * DK_A), BF16),
                        pltpu.VMEM((ng, GROUP * CHUNK, 2 * GROUP * DK_A), BF16),
                        pltpu.VMEM((ng, GROUP * DK_A, GROUP * DV_A), BF16),
                        pltpu.VMEM((ng, GROUP * CHUNK, GROUP * DV_A), BF16)],
        compiler_params=_params(("arbitrary", "arbitrary")),
        name="delta_scan",
    )(*args)
    return [out[0].reshape(batch * t, A_W), *out[1:]]


CAUG_W = DV_B + LANES
MW = H_B * CAUG_W


def _scan_max(x, d):
    row = lax.broadcasted_iota(jnp.int32, x.shape, 0)
    s = 1
    while s < CHUNK:
        if d == 0:
            shifted = jnp.where(row >= s, pltpu.roll(x, s, 0), NEG)
        else:
            shifted = jnp.where(row < CHUNK - s, pltpu.roll(x, CHUNK - s, 0), NEG)
        x = jnp.maximum(x, shifted)
        s *= 2
    return x


def _mlstm_body(*refs, n_chunks, bb, has_init, emit_state, n_cast):
    rest = refs[0:2]
    gt = refs[2:4]
    norm_ref = refs[4]
    pos = 5
    c0_ref = m0_ref = None
    if has_init:
        c0_ref, m0_ref = refs[pos], refs[pos + 1]
        pos += 2
    cast_in = refs[pos:pos + n_cast]
    pos += n_cast
    yb_ref = refs[pos]
    pos += 1
    cout_ref = nout_ref = mout_ref = None
    if emit_state:
        cout_ref, nout_ref, mout_ref = refs[pos:pos + 3]
        pos += 3
    for w_ref, w16_ref in zip(cast_in, refs[pos:pos + n_cast]):
        w16_ref[...] = w_ref[...].astype(BF16)
    pos += n_cast
    c_scr, m_scr, o_scr, bdk, bdc, bdv = refs[pos:pos + 6]

    n = pl.program_id(1)
    qw = H_B * DK_B

    @pl.when((n == 0) & (pl.program_id(0) == 0))
    def _():
        for ref in (bdk, bdc, bdv):
            ref[...] = jnp.zeros_like(ref)
        for i in range(bb * N_DIR):
            for h in range(H_B):
                bdv[i, h * CHUNK:(h + 1) * CHUNK, h * CAUG_W + DV_B:(h + 1) * CAUG_W] = jnp.ones((CHUNK, LANES), BF16)

    @pl.when(n == 0)
    def _():
        o_scr[...] = jnp.zeros_like(o_scr)
        if has_init:
            c_scr[...] = c0_ref[...]
            m_scr[...] = m0_ref[...]
        else:
            c_scr[...] = jnp.zeros_like(c_scr)
            m_scr[...] = jnp.zeros_like(m_scr)

    row = lax.broadcasted_iota(jnp.int32, (CHUNK, LANES), 0)
    lane = lax.broadcasted_iota(jnp.int32, (CHUNK, LANES), 1)
    col = lane & (CHUNK - 1)
    left = lane < CHUNK
    dirs = [(bi, d) for bi in range(bb) for d in range(N_DIR)]
    slot = {key: i for i, key in enumerate(dirs)}
    q16 = {}
    for key in dirs:
        bi, d = key
        i = slot[key]
        q16[key] = rest[d][bi, :, OFF_BQ:OFF_BQ + qw].astype(BF16)
        _store_blocks(bdk.at[i], rest[d][bi, :, OFF_BK:OFF_BK + qw].astype(BF16), CHUNK, DK_B)
        for h in range(H_B):
            bdv[i, h * CHUNK:(h + 1) * CHUNK, h * CAUG_W:h * CAUG_W + DV_B] = (
                rest[d][bi, :, OFF_BV + h * DV_B:OFF_BV + (h + 1) * DV_B].astype(BF16))
        _store_blocks(bdc.at[i], c_scr[bi, d].astype(BF16), DK_B, CAUG_W)
    qk = {key: lax.dot_general(q16[key], bdk[slot[key]], (((1,), (1,)), ((), ())), preferred_element_type=F32)
          for key in dirs}
    nc, a_t, iw, emt, ksc, dec_row = {}, {}, {}, {}, {}, {}
    for key in dirs:
        bi, d = key
        lo = LF_OFF + d * H_B
        mine = (lane >= lo) & (lane < lo + H_B)
        g = gt[d][bi]
        gc = jnp.where(mine, _dot_f32(_masks(d)[0].astype(F32), g), 0.0)
        a = jnp.where(mine, pltpu.roll(g, LF_OFF - LI_OFF, 1), 0.0) - gc
        last = CHUNK - 1 if d == 0 else 0
        m_old = m_scr[bi, d][0:1, :]
        mx = jnp.maximum(m_old, _scan_max(a, d))
        mxl = mx[last:last + 1, :]
        nc[key] = -mx
        a_t[key] = jnp.concatenate([a, a], axis=0).T
        iw[key] = jnp.exp(m_old - mx)
        emt[key] = jnp.exp(-(gc + mx))
        ksc[key] = jnp.exp(a - mxl)
        dec_row[key] = jnp.exp(m_old - mxl)
        m_scr[bi, d] = jnp.broadcast_to(gc[last:last + 1, :] + mxl, (SUBLANES, LANES))

    ks_t = {}
    for key in dirs:
        bi, d = key
        lo = LF_OFF + d * H_B
        ks = rest[d][bi, :, OFF_BK:OFF_BK + qw] * jnp.concatenate(
            [_col_bcast(ksc[key], lo + h) for h in range(H_B)], axis=1)
        ks_t[key] = jnp.concatenate([ks[:, h * DK_B:(h + 1) * DK_B] for h in range(H_B)], axis=0).T.astype(BF16)
    lhs = {}
    for key in dirs:
        bi, d = key
        lo = LF_OFF + d * H_B
        incl = row >= col if d == 0 else row <= col
        log_w = []
        for p in range(H_B // PAIR):
            l0, l1 = lo + PAIR * p, lo + PAIR * p + 1
            ccol = jnp.where(left, _col_bcast(nc[key], l0), _col_bcast(nc[key], l1))
            crow = jnp.where(left[0:1], a_t[key][l0:l0 + 1, :], a_t[key][l1:l1 + 1, :])
            log_w.append(jnp.where(incl, ccol + crow, NEG))
        dw = jnp.exp(jnp.concatenate(log_w, axis=1)) * qk[key]
        iw_x = jnp.concatenate([_col_bcast(iw[key], lo + h) for h in range(H_B)], axis=1)
        lhs[key] = ((rest[d][bi, :, OFF_BQ:OFF_BQ + qw] * iw_x).astype(BF16), dw.astype(BF16))
    num = {}
    for key in dirs:
        i = slot[key]
        parts = []
        for p in range(H_B // PAIR):
            ql, dl, cl = slice(p * PAIR * DK_B, (p + 1) * PAIR * DK_B), slice(p * LANES, (p + 1) * LANES), \
                slice(p * PAIR * CAUG_W, (p + 1) * PAIR * CAUG_W)
            parts.append(jnp.dot(jnp.concatenate([lhs[key][0][:, ql], lhs[key][1][:, dl]], axis=1),
                                 jnp.concatenate([bdc[i, ql, cl], bdv[i, dl, cl]], axis=0),
                                 preferred_element_type=F32))
        num[key] = jnp.concatenate(parts, axis=1)
    upd = {key: jnp.dot(ks_t[key], bdv[slot[key]], preferred_element_type=F32) for key in dirs}
    for key in dirs:
        bi, d = key
        lo = LF_OFF + d * H_B
        dec_x = jnp.concatenate(
            [jnp.broadcast_to(dec_row[key][:, lo + h:lo + h + 1], (1, CAUG_W)) for h in range(H_B)], axis=1)
        c_scr[bi, d] = c_scr[bi, d] * dec_x + upd[key]
    heads = [(key, h) for key in dirs for h in range(H_B)]
    rows = {d: pl.ds(pl.multiple_of((n if d == 0 else n_chunks - 1 - n) * CHUNK, CHUNK), CHUNK) for d in range(N_DIR)}
    tots = []
    for key, h in heads:
        bi, d = key
        vs = slice(h * DV_B, (h + 1) * DV_B)
        den = jnp.maximum(jnp.abs(num[key][:, h * CAUG_W + DV_B:(h + 1) * CAUG_W]),
                          _col_bcast(emt[key], LF_OFF + d * H_B + h))
        hb = jnp.concatenate([num[key][:, h * CAUG_W:h * CAUG_W + LANES] / den,
                              num[key][:, h * CAUG_W + LANES:h * CAUG_W + DV_B] / den], axis=1)
        tot = hb + o_scr[bi, rows[d], vs]
        o_scr[bi, rows[d], vs] = tot
        tots.append(tot)
    inv = [lax.rsqrt(jnp.mean(tot * tot, axis=-1, keepdims=True) + EPS) for tot in tots]
    for (key, h), tot, r in zip(heads, tots, inv):
        bi, d = key
        vs = slice(h * DV_B, (h + 1) * DV_B)
        ogate = rest[d][bi, :, OFF_BO + h * DV_B:OFF_BO + (h + 1) * DV_B]
        yb_ref[bi, rows[d], vs] = (tot * r * norm_ref[...] * ogate).astype(BF16)

    if emit_state:
        @pl.when(n == n_chunks - 1)
        def _():
            for bi, d in dirs:
                for h in range(H_B):
                    cout_ref[bi, d, h] = c_scr[bi, d, :, h * CAUG_W:h * CAUG_W + DV_B]
                    nout_ref[bi, d, h] = c_scr[bi, d, :, h * CAUG_W + DV_B:(h + 1) * CAUG_W]
            mout_ref[...] = m_scr[...]


def _mlstm(rest, gates, norm_b, c0, m0, batch, n_chunks, emit_state, casts=()):
    t = n_chunks * CHUNK
    has_init = c0 is not None
    bb = SCAN_BB
    rest, gates = (a.reshape(batch, t, a.shape[-1]) for a in (rest, gates))
    mspec = pl.BlockSpec((bb, N_DIR, SUBLANES, LANES), lambda b, n: (b, 0, 0, 0))
    in_specs = [*_scan_specs(bb, n_chunks, REST_W), *_scan_specs(bb, n_chunks, LANES),
                pl.BlockSpec((1, DV_B), lambda b, n: (0, 0))]
    args = [rest, rest, gates, gates, norm_b]
    if has_init:
        in_specs += [pl.BlockSpec((bb, N_DIR, DK_B, MW), lambda b, n: (b, 0, 0, 0)), mspec]
        args += [c0, m0]
    out_specs = [pl.BlockSpec((bb, t, B_W), lambda b, n: (b, 0, 0))]
    out_shape = [jax.ShapeDtypeStruct((batch, t, B_W), BF16)]
    if emit_state:
        out_specs += [pl.BlockSpec((bb, N_DIR, H_B, DK_B, DV_B), lambda b, n: (b, 0, 0, 0, 0)),
                      pl.BlockSpec((bb, N_DIR, H_B, DK_B, LANES), lambda b, n: (b, 0, 0, 0, 0)), mspec]
        out_shape += [jax.ShapeDtypeStruct((batch, N_DIR, H_B, DK_B, DV_B), F32),
                      jax.ShapeDtypeStruct((batch, N_DIR, H_B, DK_B, LANES), F32),
                      jax.ShapeDtypeStruct((batch, N_DIR, SUBLANES, LANES), F32)]
    steps = (batch // bb) * n_chunks
    c_in, c_out, c_shape, c_args = _cast_specs([(w, l, steps) for w, l in casts], lambda b, n: b * n_chunks + n)
    in_specs, args, out_specs, out_shape = in_specs + c_in, args + c_args, out_specs + c_out, out_shape + c_shape
    out = pl.pallas_call(
        functools.partial(_mlstm_body, n_chunks=n_chunks, bb=bb, has_init=has_init, emit_state=emit_state,
                          n_cast=len(casts)),
        grid=(batch // bb, n_chunks),
        in_specs=in_specs,
        out_specs=out_specs,
        out_shape=out_shape,
        scratch_shapes=[pltpu.VMEM((bb, N_DIR, DK_B, MW), F32),
                        pltpu.VMEM((bb, N_DIR, SUBLANES, LANES), F32),
                        pltpu.VMEM((bb, t, B_W), F32),
                        pltpu.VMEM((bb * N_DIR, H_B * CHUNK, H_B * DK_B), BF16),
                        pltpu.VMEM((bb * N_DIR, H_B * DK_B, MW), BF16),
                        pltpu.VMEM((bb * N_DIR, H_B * CHUNK, MW), BF16)],
        compiler_params=_params(("arbitrary", "arbitrary")),
        name="mlstm_scan",
    )(*args)
    return [out[0].reshape(batch * t, B_W), *out[1:]]


OUT_TM = 512
OUT_RB = 256
FFN_TM = 512
FFN_FC = 1024


def _outproj_body(ya_ref, yb_ref, wa_ref, wb_ref, x_ref, mod_ref, post1_ref, pre2_ref, x1_ref, h2_ref):
    for rb in range(x_ref.shape[0] // OUT_RB):
        rs = slice(rb * OUT_RB, (rb + 1) * OUT_RB)
        mix = (jnp.dot(ya_ref[rs, :], wa_ref[...], preferred_element_type=F32)
               + jnp.dot(yb_ref[rs, :], wb_ref[...], preferred_element_type=F32))
        x1 = x_ref[rs, :] + mod_ref[0, 2:3, :] * (_rms(mix) * post1_ref[...])
        x1_ref[rs, :] = x1
        h2 = _rms(x1) * pre2_ref[...] * (1.0 + mod_ref[0, 4:5, :]) + mod_ref[0, 3:4, :]
        h2_ref[rs, :] = h2.astype(BF16)


def _outproj(ya, yb, w_out, x2d, mod3, mod_map, post1, pre2, tm):
    m = x2d.shape[0]
    row = lambda i: (i, 0)
    const = lambda i: (0, 0)
    return pl.pallas_call(
        _outproj_body,
        grid=(m // tm,),
        in_specs=[pl.BlockSpec((tm, A_W), row), pl.BlockSpec((tm, B_W), row),
                  pl.BlockSpec((A_W, D_MODEL), lambda i: (0, 0)),
                  pl.BlockSpec((B_W, D_MODEL), lambda i: (1, 0)),
                  pl.BlockSpec((tm, D_MODEL), row),
                  pl.BlockSpec((1, 6, D_MODEL), lambda i: (mod_map(i * tm), 0, 0)),
                  pl.BlockSpec((1, D_MODEL), const), pl.BlockSpec((1, D_MODEL), const)],
        out_specs=[pl.BlockSpec((tm, D_MODEL), row), pl.BlockSpec((tm, D_MODEL), row)],
        out_shape=[jax.ShapeDtypeStruct((m, D_MODEL), F32), jax.ShapeDtypeStruct((m, D_MODEL), BF16)],
        compiler_params=_params(("parallel",)),
        name="outproj",
    )(ya, yb, w_out, w_out, x2d, mod3, post1, pre2)


def _ffn_body(h2_ref, w1_ref, w2_ref, x1_ref, mod_ref, post2_ref, o_ref):
    kk = pl.program_id(1)

    @pl.when(kk == 0)
    def _():
        o_ref[...] = jnp.zeros_like(o_ref)

    def accumulate(rs):
        a = jnp.maximum(jnp.dot(h2_ref[rs, :], w1_ref[...], preferred_element_type=F32), 0.0)
        o_ref[rs, :] += jnp.dot((a * a).astype(BF16), w2_ref[...], preferred_element_type=F32)

    last = pl.num_programs(1) - 1
    tm = o_ref.shape[0]

    @pl.when(kk < last)
    def _():
        accumulate(slice(0, tm))

    @pl.when(kk == last)
    def _():
        for rs in (slice(0, tm // 2), slice(tm // 2, tm)):
            accumulate(rs)
            o_ref[rs, :] = x1_ref[rs, :] + mod_ref[0, 5:6, :] * (_rms(o_ref[rs, :]) * post2_ref[...])


def _ffn(h2, w1, w2, x1, mod3, mod_map, post2, tm, fc):
    m = h2.shape[0]
    return pl.pallas_call(
        _ffn_body,
        grid=(m // tm, FFN // fc),
        in_specs=[pl.BlockSpec((tm, D_MODEL), lambda i, k: (i, 0)),
                  pl.BlockSpec((D_MODEL, fc), lambda i, k: (0, k)),
                  pl.BlockSpec((fc, D_MODEL), lambda i, k: (k, 0)),
                  pl.BlockSpec((tm, D_MODEL), lambda i, k: (i, 0)),
                  pl.BlockSpec((1, 6, D_MODEL), lambda i, k: (mod_map(i * tm), 0, 0)),
                  pl.BlockSpec((1, D_MODEL), lambda i, k: (0, 0))],
        out_specs=pl.BlockSpec((tm, D_MODEL), lambda i, k: (i, 0)),
        out_shape=jax.ShapeDtypeStruct((m, D_MODEL), F32),
        compiler_params=_params(("parallel", "arbitrary")),
        name="ffn",
    )(h2, w1, w2, x1, mod3, post2)


def _block(x, mod3, mod_of_row, lp, init, seq_len, emit_state):
    bsz, t, _ = x.shape
    x2d = x.reshape(bsz * t, D_MODEL)
    n_chunks = t // CHUNK
    qkv, rest, gates = _proj(x2d, mod3, lambda i: mod_of_row(i * PROJ_TM), lp["pre1"], lp["w16"],
                             lp["w_mid"], lp["w_gate"], lp["conv_w"], lp["gate_p"], seq_len)
    s0, c0, m0 = init if init is not None else (None, None, None)
    first = "w1" not in lp
    l = lp["l"]
    d_out = _delta(qkv, rest, gates, lp["norm_a"], s0, bsz, n_chunks, emit_state,
                   [(lp["w_ffn1"], l), (lp["w_out32"], l)] if first else ())
    m_out = _mlstm(rest, gates, lp["norm_b"], c0, m0, bsz, n_chunks, emit_state,
                   [(lp["w_ffn2"], l)] if first else ())
    if first:
        (lp["w1"], lp["w_out"]), lp["w2"] = d_out[-2:], m_out[-1]
    x1, h2 = _outproj(d_out[0], m_out[0], lp["w_out"], x2d, mod3, mod_of_row, lp["post1"], lp["pre2"], OUT_TM)
    y = _ffn(h2, lp["w1"], lp["w2"], x1, mod3, mod_of_row, lp["post2"], FFN_TM, FFN_FC)
    states = None
    if emit_state:
        m_fin = jnp.stack([m_out[3][:, d, 0, LF_OFF + d * H_B:LF_OFF + (d + 1) * H_B] for d in range(N_DIR)], axis=1)
        states = (d_out[1], m_out[1], m_out[2][..., 0], m_fin)
    return y.reshape(bsz, t, D_MODEL), states


def _layer_params(l, norm_mix_pre, norm_mix_post, norm_ffn_pre, norm_ffn_post, w_in, conv_w, a_log, dt_bias,
                  norm_a, mlstm_ibias, mlstm_fbias, norm_b, w_out, w_ffn1, w_ffn2):
    w = w_in[l]
    o_ag = QKV_W
    o_aa = o_ag + A_W
    o_ab = o_aa + N_DIR * H_A
    o_bq = o_ab + N_DIR * H_A
    o_bi = o_bq + 2 * H_B * DK_B + 2 * B_W
    o_bf = o_bi + N_DIR * H_B
    n_gate = 2 * N_DIR * H_A + 2 * N_DIR * H_B
    w16 = w.astype(BF16)
    w_gate = jnp.concatenate([w16[:, o_aa:o_bq], w16[:, o_bi:o_bf + N_DIR * H_B],
                              jnp.zeros((D_MODEL, LANES - n_gate), BF16)], axis=1)

    def lane_row(vals, off):
        return jnp.zeros((LANES,), F32).at[off:off + vals.size].set(vals.reshape(-1))

    gate_p = jnp.stack([lane_row(a_log[l], G_OFF), lane_row(dt_bias[l], G_OFF),
                        lane_row(mlstm_ibias[l], LI_OFF) + lane_row(mlstm_fbias[l], LF_OFF)]
                       + [jnp.zeros((LANES,), F32)] * (SUBLANES - 3))
    row = lambda v: v[l].reshape(1, -1)
    return dict(
        pre1=row(norm_mix_pre), post1=row(norm_mix_post), pre2=row(norm_ffn_pre), post2=row(norm_ffn_post),
        w16=w16, w_mid=w16[:, o_bq:o_bi], w_gate=w_gate, gate_p=gate_p,
        conv_w=jnp.concatenate([conv_w[l].T, jnp.zeros((SUBLANES - 3, QKV_W), F32)], axis=0),
        norm_a=row(norm_a), norm_b=row(norm_b),
        l=l, w_out32=w_out, w_ffn1=w_ffn1, w_ffn2=w_ffn2)


def kernel(x_prompt, x_sample, state_delta, state_mlstm_C, state_mlstm_n, state_mlstm_m, c, c_ctx, w_ada, b_ada, norm_mix_pre, norm_mix_post, norm_ffn_pre, norm_ffn_post, w_in, conv_w, a_log, dt_bias, norm_a, mlstm_ibias, mlstm_fbias, norm_b, w_out, w_ffn1, w_ffn2):
    depth = w_in.shape[0]
    n_lat = x_sample.shape[0]
    t_lat = x_sample.shape[1]
    cond = jnp.concatenate([c_ctx[None, :], c, jnp.zeros((SUBLANES - 1 - n_lat, D_MODEL), F32)], axis=0)
    y_prompt, y_sample = x_prompt, x_sample
    acc = ([], [], [], [])
    for l in range(depth):
        lp = _layer_params(l, norm_mix_pre, norm_mix_post, norm_ffn_pre, norm_ffn_post, w_in, conv_w, a_log,
                           dt_bias, norm_a, mlstm_ibias, mlstm_fbias, norm_b, w_out, w_ffn1, w_ffn2)
        mod = _ada(cond, w_ada, b_ada[l].reshape(1, -1), l)
        mod3 = mod[:1 + n_lat].reshape(1 + n_lat, 6, D_MODEL)
        y_prompt, st = _block(y_prompt, mod3, lambda r: 0, lp, None, x_prompt.shape[1], True)
        for a, s in zip(acc, st):
            a.append(s)
        n_rep = jnp.broadcast_to(state_mlstm_n[:, l][..., None], state_mlstm_n[:, l].shape + (LANES,))
        c_aug0 = jnp.concatenate([state_mlstm_C[:, l], n_rep], axis=-1)
        c_aug0 = c_aug0.transpose(0, 1, 3, 2, 4).reshape(n_lat, N_DIR, DK_B, MW)
        m0 = jnp.zeros((n_lat, N_DIR, LANES), F32)
        for d in range(N_DIR):
            m0 = m0.at[:, d, LF_OFF + d * H_B:LF_OFF + (d + 1) * H_B].set(state_mlstm_m[:, l, d])
        m0 = jnp.broadcast_to(m0[:, :, None, :], (n_lat, N_DIR, SUBLANES, LANES))
        y_sample, _ = _block(y_sample, mod3, lambda r: 1 + r // t_lat, lp, (state_delta[:, l], c_aug0, m0),
                             GRID_W, False)
    return (y_prompt, y_sample) + tuple(jnp.stack(a, axis=1) for a in acc)
```

```python
import functools

import jax
import jax.numpy as jnp
from jax import lax
from jax.experimental import pallas as pl
from jax.experimental.pallas import tpu as pltpu

F32 = jnp.float32
BF16 = jnp.bfloat16

D_MODEL = 2048
N_DIR = 2
A_W = D_MODEL // 2
B_W = D_MODEL - A_W
DK_A = 128
DV_A = 128
H_A = A_W // DV_A
DV_B = 256
DK_B = DV_B // 2
H_B = B_W // DV_B
GRID_W = 64
CHUNK = 64
FFN = 4 * D_MODEL
EPS = 1e-6
LANES = 128
SUBLANES = 8
NEG = -1e30

QKV_W = 3 * A_W
REST_W = A_W + 2 * H_B * DK_B + 2 * B_W
OFF_AG, OFF_BQ, OFF_BK, OFF_BV, OFF_BO = 0, A_W, A_W + H_B * DK_B, A_W + 2 * H_B * DK_B, A_W + 2 * H_B * DK_B + B_W
G_OFF, BETA_OFF = 0, N_DIR * H_A
LI_OFF = 2 * N_DIR * H_A
LF_OFF = LI_OFF + N_DIR * H_B

VMEM_LIMIT = 56 * 1024 * 1024


def _sigmoid(x):
    return 1.0 / (1.0 + jnp.exp(-x))


def _softplus(x):
    return jnp.maximum(x, 0.0) + jnp.log1p(jnp.exp(-jnp.abs(x)))


def _dot(a, b):
    return jnp.dot(a.astype(BF16), b.astype(BF16), preferred_element_type=F32)


def _dot_f32(a, b):
    return jnp.dot(a, b, precision=lax.Precision.HIGHEST, preferred_element_type=F32)


def _rms(x):
    return x * lax.rsqrt(jnp.mean(x * x, axis=-1, keepdims=True) + EPS)


def _params(sem):
    return pltpu.CompilerParams(dimension_semantics=sem, vmem_limit_bytes=VMEM_LIMIT)


def _ada_body(c_ref, w_ref, b_ref, o_ref):
    c = c_ref[...]
    o_ref[...] = _dot(c * _sigmoid(c), w_ref[...]) + b_ref[...]


def _ada(c_all, w_ada, b, l):
    n = w_ada.shape[2]
    tn = 1024
    return pl.pallas_call(
        _ada_body,
        grid=(n // tn,),
        in_specs=[pl.BlockSpec(c_all.shape, lambda j: (0, 0)),
                  pl.BlockSpec((None, D_MODEL, tn), lambda j: (l, 0, j)),
                  pl.BlockSpec((1, tn), lambda j: (0, j))],
        out_specs=pl.BlockSpec((c_all.shape[0], tn), lambda j: (0, j)),
        out_shape=jax.ShapeDtypeStruct((c_all.shape[0], n), F32),
        compiler_params=_params(("arbitrary",)),
        name="ada",
    )(c_all, w_ada, b)


PROJ_TM = 1024
PROJ_RB = 512
PROJ_CB = 256
PROJ_TN = 512
N_QKV_T = QKV_W // PROJ_TN
N_REST_T = REST_W // PROJ_TN
N_HEAD_T = (QKV_W + A_W) // PROJ_TN
N_MID_T = N_QKV_T + N_REST_T - N_HEAD_T


def _proj_body(x_ref, mod_ref, g_ref, wh_ref, wb_ref, wg_ref, cw_ref, gp_ref, qkv_ref, rest_ref, gate_ref, h_scr, *,
               seq_len):
    j = pl.program_id(1)
    row_blocks = [slice(rb * PROJ_RB, (rb + 1) * PROJ_RB) for rb in range(PROJ_TM // PROJ_RB)]
    col_blocks = [slice(cb * PROJ_CB, (cb + 1) * PROJ_CB) for cb in range(PROJ_TN // PROJ_CB)]

    def run_rows(rs, w_ref, epilogue, out_ref):
        for cs in col_blocks:
            epilogue(jnp.dot(h_scr[rs, :], w_ref[:, cs], preferred_element_type=F32), out_ref, rs, cs)

    def run(w_ref, epilogue, out_ref):
        for rs in row_blocks:
            run_rows(rs, w_ref, epilogue, out_ref)

    def run_first():
        for rs in row_blocks:
            y = _rms(x_ref[rs, :]) * g_ref[...]
            h_scr[rs, :] = (y * (1.0 + mod_ref[0, 1:2, :]) + mod_ref[0, 0:1, :]).astype(BF16)
            run_rows(rs, wh_ref, ep_l2(DK_A ** -0.5), qkv_ref)
            ep_gate(jnp.dot(h_scr[rs, :], wg_ref[...], preferred_element_type=F32), gate_ref, rs)

    def conv_silu(acc, cs, g):
        a = acc[:, g * LANES:(g + 1) * LANES]
        ls = slice(cs.start + g * LANES, cs.start + (g + 1) * LANES)
        pos = lax.broadcasted_iota(jnp.int32, a.shape, 0) & (seq_len - 1)
        prev = jnp.where(pos == 0, 0.0, pltpu.roll(a, 1, 0))
        nxt = jnp.where(pos == seq_len - 1, 0.0, pltpu.roll(a, a.shape[0] - 1, 0))
        y = prev * cw_ref[0:1, ls] + a * cw_ref[1:2, ls] + nxt * cw_ref[2:3, ls]
        return ls, y * _sigmoid(y)

    def ep_l2(scale):
        def f(acc, out_ref, rs, cs):
            for g in range(PROJ_CB // LANES):
                ls, blk = conv_silu(acc, cs, g)
                inv = lax.rsqrt(jnp.sum(blk * blk, axis=-1, keepdims=True) + EPS)
                out_ref[rs, ls] = blk * (inv * scale)
        return f

    def ep_conv(acc, out_ref, rs, cs):
        for g in range(PROJ_CB // LANES):
            ls, blk = conv_silu(acc, cs, g)
            out_ref[rs, ls] = blk

    def ep_map(fn):
        def f(acc, out_ref, rs, cs):
            out_ref[rs, cs] = fn(acc)
        return f

    def ep_gate(z, out_ref, rs):
        lane = lax.broadcasted_iota(jnp.int32, z.shape, 1)
        g = -jnp.exp(gp_ref[0:1, :]) * _softplus(z + gp_ref[1:2, :])
        li = z + gp_ref[2:3, :]
        out_ref[rs, :] = jnp.where(lane < BETA_OFF, g,
                                   jnp.where(lane < LI_OFF, _sigmoid(z),
                                             jnp.where(lane < LF_OFF, li,
                                                       jnp.where(lane < LF_OFF + N_DIR * H_B, -_softplus(-li), 0.0))))

    tp = A_W // PROJ_TN
    c0 = (j - N_QKV_T) * PROJ_TN
    in_rest = (j >= N_QKV_T) & (j < N_QKV_T + N_REST_T)
    pl.when(j == 0)(run_first)
    variants = [
        ((j > 0) & (j < tp), wh_ref, ep_l2(DK_A ** -0.5), qkv_ref),
        ((j >= tp) & (j < 2 * tp), wh_ref, ep_l2(1.0), qkv_ref),
        ((j >= 2 * tp) & (j < N_QKV_T), wh_ref, ep_conv, qkv_ref),
        (in_rest & (c0 < OFF_BQ), wh_ref, ep_map(lambda a: a * _sigmoid(a)), rest_ref),
        (in_rest & (c0 >= OFF_BQ) & (c0 < OFF_BK), wb_ref, ep_map(lambda a: a * (DK_B ** -0.5)), rest_ref),
        (in_rest & (c0 >= OFF_BK) & (c0 < OFF_BO), wb_ref, ep_map(lambda a: a), rest_ref),
        (in_rest & (c0 >= OFF_BO), wb_ref, ep_map(_sigmoid), rest_ref),
    ]
    for cond, w_ref, epilogue, out_ref in variants:
        pl.when(cond)(functools.partial(run, w_ref, epilogue, out_ref))


def _proj(x2d, mod3, mod_map, pre_g, w16, w_mid, w_gate, conv_w, gate_p, seq_len):
    m = x2d.shape[0]
    tm, tn = PROJ_TM, PROJ_TN
    assert PROJ_RB % seq_len == 0 and m % tm == 0
    return pl.pallas_call(
        functools.partial(_proj_body, seq_len=seq_len),
        grid=(m // tm, N_HEAD_T + N_MID_T),
        in_specs=[pl.BlockSpec((tm, D_MODEL), lambda i, j: (i, 0)),
                  pl.BlockSpec((1, 6, D_MODEL), lambda i, j: (mod_map(i), 0, 0)),
                  pl.BlockSpec((1, D_MODEL), lambda i, j: (0, 0)),
                  pl.BlockSpec((D_MODEL, tn), lambda i, j: (0, jnp.minimum(j, N_HEAD_T - 1))),
                  pl.BlockSpec((D_MODEL, tn), lambda i, j: (0, jnp.clip(j - N_HEAD_T, 0, N_MID_T - 1))),
                  pl.BlockSpec((D_MODEL, LANES), lambda i, j: (0, 0)),
                  pl.BlockSpec((SUBLANES, tn), lambda i, j: (0, jnp.minimum(j, N_QKV_T - 1))),
                  pl.BlockSpec((SUBLANES, LANES), lambda i, j: (0, 0))],
        out_specs=[pl.BlockSpec((tm, tn), lambda i, j: (i, jnp.minimum(j, N_QKV_T - 1))),
                   pl.BlockSpec((tm, tn), lambda i, j: (i, jnp.clip(j - N_QKV_T, 0, N_REST_T - 1))),
                   pl.BlockSpec((tm, LANES), lambda i, j: (i, 0))],
        out_shape=[jax.ShapeDtypeStruct((m, QKV_W), F32), jax.ShapeDtypeStruct((m, REST_W), F32),
                   jax.ShapeDtypeStruct((m, LANES), F32)],
        scratch_shapes=[pltpu.VMEM((tm, D_MODEL), BF16)],
        compiler_params=_params(("parallel", "arbitrary")),
        name="proj",
    )(x2d, mod3, pre_g, w16, w_mid, w_gate, conv_w, gate_p)


def _masks(d):
    row = lax.broadcasted_iota(jnp.int32, (CHUNK, CHUNK), 0)
    col = lax.broadcasted_iota(jnp.int32, (CHUNK, CHUNK), 1)
    if d == 0:
        return row >= col, row > col, row == col
    return row <= col, row < col, row == col


GROUP = 4
N_GROUPS = H_A // GROUP
PAIR = LANES // CHUNK


def _split(x):
    hi = x.astype(BF16)
    return hi, (x - hi.astype(F32)).astype(BF16)


def _store_blocks(ref, x, blk_r, blk_c, col0=0):
    for h in range(GROUP):
        ref[h * blk_r:(h + 1) * blk_r, col0 + h * blk_c:col0 + (h + 1) * blk_c] = x[:, h * blk_c:(h + 1) * blk_c]


def _dot3(a_hi, a_lo, b_hi, b_lo):
    m = a_hi.shape[0]
    r = jnp.dot(jnp.concatenate([a_hi, a_lo], axis=0), b_hi, preferred_element_type=F32)
    return r[:m] + r[m:] + jnp.dot(a_hi, b_lo, preferred_element_type=F32)


HALF = CHUNK // 2


def _tri_inverse_wide(lmats, upper, bd_hi, bd_lo, by_hi, by_lo):
    width = lmats[0].shape[1]
    n_blk = width // HALF
    row = lax.broadcasted_iota(jnp.int32, (HALF, width), 0)
    lane = lax.broadcasted_iota(jnp.int32, (HALF, width), 1)
    lead = (lane & (CHUNK - 1)) < HALF
    eye_d = (row == (lane & (HALF - 1))).astype(F32)

    def bd_dot(i, a, b_hi, b_lo):
        for blk in range(n_blk):
            sl = slice(blk * HALF, (blk + 1) * HALF)
            bd_hi[i, sl, sl] = b_hi[:, sl]
            bd_lo[i, sl, sl] = b_lo[:, sl]
        return _dot3(*a, bd_hi[i], bd_lo[i])

    idx = range(len(lmats))
    diag = [jnp.where(lead, l[:HALF], l[HALF:]) for l in lmats]
    s = [eye_d - dg for dg in diag]
    p = []
    for i in idx:
        m_hi, m_lo = _split(-diag[i])
        p.append(bd_dot(i, (m_hi, m_lo), m_hi, m_lo))
    for _ in range(3):
        for i in idx:
            p_hi, p_lo = _split(p[i])
            s_hi, s_lo = _split(s[i])
            r = bd_dot(i, (jnp.concatenate([p_hi, s_hi], axis=0), jnp.concatenate([p_lo, s_lo], axis=0)), p_hi, p_lo)
            p[i] = r[:HALF]
            s[i] = s[i] + r[HALF:]
    for i in idx:
        p_hi, p_lo = _split(p[i])
        s[i] = s[i] + bd_dot(i, _split(s[i]), p_hi, p_lo)
    y = []
    for i in idx:
        c_blk = jnp.where(lead, 0.0, lmats[i][:HALF]) if upper[i] else jnp.where(lead, lmats[i][HALF:], 0.0)
        y.append(bd_dot(i, _split(c_blk), *_split(s[i])))
    out = []
    for i in idx:
        y_hi, y_lo = _split(y[i])
        for h in range(width // CHUNK):
            a_sl, b_sl = slice(h * CHUNK, h * CHUNK + HALF), slice(h * CHUNK + HALF, (h + 1) * CHUNK)
            rs, cs = (a_sl, b_sl) if upper[i] else (b_sl, a_sl)
            by_hi[i, rs, cs] = y_hi[:, cs]
            by_lo[i, rs, cs] = y_lo[:, cs]
        ai, bi = jnp.where(lead, s[i], 0.0), jnp.where(lead, 0.0, s[i])
        x = _dot3(*_split(ai if upper[i] else bi), by_hi[i], by_lo[i])
        out.append(jnp.concatenate([ai - x, bi] if upper[i] else [ai, bi - x], axis=0))
    return out


def _col_bcast(tile, c, width=LANES):
    return jnp.broadcast_to(tile[:, c:c + 1], (tile.shape[0], width))


def _delta_body(*refs, n_chunks, bb, has_init, emit_state, n_cast):
    qkv = refs[0:2]
    ag = refs[2:4]
    gt = refs[4:6]
    norm_ref = refs[6]
    pos = 7
    s0_ref = None
    if has_init:
        s0_ref = refs[pos]
        pos += 1
    cast_in = refs[pos:pos + n_cast]
    pos += n_cast
    ya_ref = refs[pos]
    pos += 1
    sout_ref = None
    if emit_state:
        sout_ref = refs[pos]
        pos += 1
    for w_ref, w16_ref in zip(cast_in, refs[pos:pos + n_cast]):
        w16_ref[...] = w_ref[...].astype(BF16)
    pos += n_cast
    s_scr, o_scr, bdn_hi, bdn_lo, bdy_hi, bdy_lo, bdk, bduw, bds, bdv = refs[pos:pos + 10]

    n = pl.program_id(1)
    gw = GROUP * DK_A

    @pl.when((n == 0) & (pl.program_id(0) == 0))
    def _():
        for ref in (bdn_hi, bdn_lo, bdy_hi, bdy_lo, bdk, bduw, bds, bdv):
            ref[...] = jnp.zeros_like(ref)

    @pl.when(n == 0)
    def _():
        o_scr[...] = jnp.zeros_like(o_scr)
        for bi in range(bb):
            for d in range(N_DIR):
                for h in range(H_A):
                    blk = s0_ref[bi, d, h] if has_init else jnp.zeros((DK_A, DV_A), F32)
                    s_scr[bi, d, h // GROUP, :, (h % GROUP) * DV_A:(h % GROUP + 1) * DV_A] = blk

    row = lax.broadcasted_iota(jnp.int32, (CHUNK, LANES), 0)
    lane = lax.broadcasted_iota(jnp.int32, (CHUNK, LANES), 1)
    col = lane & (CHUNK - 1)
    left = lane < CHUNK
    probs = [(bi, d, g) for bi in range(bb) for d in range(N_DIR) for g in range(N_GROUPS)]
    masks = {0: (row >= col, row > col), 1: (row <= col, row < col)}
    gates, gc, gc_t, eg, egl = {}, {}, {}, {}, {}
    for bi in range(bb):
        for d in range(N_DIR):
            key = (bi, d)
            gates[key] = gt[d][bi]
            gc[key] = _dot_f32(_masks(d)[0].astype(F32), gates[key])
            gc_t[key] = jnp.concatenate([gc[key], gc[key]], axis=0).T
            last = CHUNK - 1 if d == 0 else 0
            eg[key] = jnp.exp(gc[key])
            egl[key] = jnp.exp(gc[key][last:last + 1, :] - gc[key])

    def cols_of(d, g):
        return [G_OFF + d * H_A + g * GROUP + hl for hl in range(GROUP)]

    q16, ks, beta_xs, decays, grams = [], [], [], [], []
    for gi, (bi, d, g) in enumerate(probs):
        key = (bi, d)
        cols = cols_of(d, g)
        incl, _ = masks[d]
        k = qkv[d][bi, :, A_W + g * gw:A_W + (g + 1) * gw]
        beta_x = jnp.concatenate([_col_bcast(gates[key], BETA_OFF - G_OFF + c) for c in cols], axis=1)
        decay = []
        for p in range(GROUP // PAIR):
            c0, c1 = cols[PAIR * p], cols[PAIR * p + 1]
            gcol = jnp.where(left, _col_bcast(gc[key], c0), _col_bcast(gc[key], c1))
            grow = jnp.where(left[0:1], gc_t[key][c0:c0 + 1, :], gc_t[key][c1:c1 + 1, :])
            decay.append(jnp.exp(jnp.where(incl, gcol - grow, NEG)))
        decays.append(jnp.concatenate(decay, axis=1))
        q16.append(qkv[d][bi, :, g * gw:(g + 1) * gw].astype(BF16))
        ks.append(k)
        beta_xs.append(beta_x)
        _store_blocks(bdk.at[gi], k.astype(BF16), CHUNK, DK_A)
    for gi in range(len(probs)):
        grams.append(lax.dot_general(jnp.concatenate([q16[gi], (ks[gi] * beta_xs[gi]).astype(BF16)], axis=0),
                                     bdk[gi], (((1,), (1,)), ((), ())), preferred_element_type=F32))
    attns, lmats = [], []
    for gi, (bi, d, g) in enumerate(probs):
        strict_w = jnp.concatenate([masks[d][1]] * (GROUP // PAIR), axis=1)
        attns.append((grams[gi][:CHUNK] * decays[gi]).astype(BF16))
        lmats.append(jnp.where(strict_w, grams[gi][CHUNK:] * decays[gi], 0.0))
    ainvs = _tri_inverse_wide(lmats, [d == 1 for _, d, _ in probs], bdn_hi, bdn_lo, bdy_hi, bdy_lo)
    eg_xs, kd_ts = [], []
    for gi, (bi, d, g) in enumerate(probs):
        key = (bi, d)
        cols = cols_of(d, g)
        v = qkv[d][bi, :, 2 * A_W + g * gw:2 * A_W + (g + 1) * gw]
        eg_x = jnp.concatenate([_col_bcast(eg[key], c) for c in cols], axis=1)
        egl_x = jnp.concatenate([_col_bcast(egl[key], c) for c in cols], axis=1)
        eg_xs.append(eg_x)
        _store_blocks(bduw.at[gi], (v * beta_xs[gi]).astype(BF16), CHUNK, DV_A)
        _store_blocks(bduw.at[gi], (ks[gi] * (beta_xs[gi] * eg_x)).astype(BF16), CHUNK, DK_A, col0=gw)
        kd = ks[gi] * egl_x
        kd_ts.append(jnp.concatenate([kd[:, hl * DK_A:(hl + 1) * DK_A] for hl in range(GROUP)], axis=0)
                     .T.astype(BF16))
        _store_blocks(bds.at[gi], s_scr[bi, d, g].astype(BF16), DK_A, DV_A)
    uws = []
    for gi in range(len(probs)):
        t_hi, t_lo = _split(ainvs[gi])
        r = jnp.dot(jnp.concatenate([t_hi, t_lo], axis=0), bduw[gi], preferred_element_type=F32)
        uws.append(r[:CHUNK] + r[CHUNK:])
    ws_qs = []
    hw = gw // 2
    for gi in range(len(probs)):
        wq = jnp.concatenate([uws[gi][:, gw:].astype(BF16), q16[gi]], axis=0)
        ws_qs.append(jnp.concatenate(
            [jnp.dot(wq[:, p * hw:(p + 1) * hw], bds[gi, p * hw:(p + 1) * hw, p * hw:(p + 1) * hw],
                     preferred_element_type=F32) for p in range(2)], axis=1))
    for gi in range(len(probs)):
        _store_blocks(bdv.at[gi], (uws[gi][:, :gw] - ws_qs[gi][:CHUNK]).astype(BF16), CHUNK, DV_A)
    rs = []
    for gi in range(len(probs)):
        rs.append(jnp.dot(jnp.concatenate([attns[gi], kd_ts[gi]], axis=0), bdv[gi], preferred_element_type=F32))
    tots = []
    for gi, (bi, d, g) in enumerate(probs):
        cols = cols_of(d, g)
        last = CHUNK - 1 if d == 0 else 0
        o = ws_qs[gi][CHUNK:] * eg_xs[gi] + rs[gi][:CHUNK]
        eg_last = jnp.concatenate(
            [jnp.broadcast_to(eg[bi, d][last:last + 1, c:c + 1], (1, DV_A)) for c in cols], axis=1)
        s_scr[bi, d, g] = s_scr[bi, d, g] * eg_last + rs[gi][CHUNK:]
        cidx = n if d == 0 else n_chunks - 1 - n
        rows = pl.ds(pl.multiple_of(cidx * CHUNK, CHUNK), CHUNK)
        tot = o + o_scr[bi, rows, g * gw:(g + 1) * gw]
        o_scr[bi, rows, g * gw:(g + 1) * gw] = tot
        tots.append((rows, tot))
    blks = [(bi, d, rows, g * GROUP + hl, tot[:, hl * DV_A:(hl + 1) * DV_A])
            for (bi, d, g), (rows, tot) in zip(probs, tots) for hl in range(GROUP)]
    inv = [lax.rsqrt(jnp.mean(blk * blk, axis=-1, keepdims=True) + EPS) for *_, blk in blks]
    for (bi, d, rows, h, blk), r in zip(blks, inv):
        hs = slice(h * DV_A, (h + 1) * DV_A)
        ya_ref[bi, rows, hs] = (blk * r * norm_ref[...] * ag[d][bi, :, hs]).astype(BF16)

    if emit_state:
        @pl.when(n == n_chunks - 1)
        def _():
            for bi in range(bb):
                for d in range(N_DIR):
                    for h in range(H_A):
                        sout_ref[bi, d, h] = s_scr[bi, d, h // GROUP, :, (h % GROUP) * DV_A:(h % GROUP + 1) * DV_A]


SCAN_BB = 2


def _scan_specs(bb, n_chunks, width):
    return (pl.BlockSpec((bb, CHUNK, width), lambda b, n: (b, n, 0)),
            pl.BlockSpec((bb, CHUNK, width), lambda b, n: (b, n_chunks - 1 - n, 0)))


def _cast_specs(casts, steps_of):
    in_specs, out_specs, out_shape, args = [], [], [], []
    for w, l, steps in casts:
        rows = w.shape[1] // steps
        assert rows * steps == w.shape[1] and rows % (2 * SUBLANES) == 0
        in_specs.append(pl.BlockSpec((None, rows, w.shape[2]), lambda b, n, l=l: (l, steps_of(b, n), 0)))
        out_specs.append(pl.BlockSpec((rows, w.shape[2]), lambda b, n: (steps_of(b, n), 0)))
        out_shape.append(jax.ShapeDtypeStruct(w.shape[1:], BF16))
        args.append(w)
    return in_specs, out_specs, out_shape, args


def _delta(qkv, rest, gates, norm_a, s0, batch, n_chunks, emit_state, casts=()):
    t = n_chunks * CHUNK
    has_init = s0 is not None
    bb = SCAN_BB
    ng = bb * N_DIR * N_GROUPS
    qkv, rest, gates = (a.reshape(batch, t, a.shape[-1]) for a in (qkv, rest, gates))
    in_specs = [*_scan_specs(bb, n_chunks, QKV_W), *_scan_specs(bb, n_chunks, A_W), *_scan_specs(bb, n_chunks, LANES),
                pl.BlockSpec((1, DV_A), lambda b, n: (0, 0))]
    args = [qkv, qkv, rest, rest, gates, gates, norm_a]
    sspec = pl.BlockSpec((bb, N_DIR, H_A, DK_A, DV_A), lambda b, n: (b, 0, 0, 0, 0))
    if has_init:
        in_specs.append(sspec)
        args.append(s0)
    out_specs = [pl.BlockSpec((bb, t, A_W), lambda b, n: (b, 0, 0))]
    out_shape = [jax.ShapeDtypeStruct((batch, t, A_W), BF16)]
    if emit_state:
        out_specs.append(sspec)
        out_shape.append(jax.ShapeDtypeStruct((batch, N_DIR, H_A, DK_A, DV_A), F32))
    steps = (batch // bb) * n_chunks
    c_in, c_out, c_shape, c_args = _cast_specs([(w, l, steps) for w, l in casts], lambda b, n: b * n_chunks + n)
    in_specs, args, out_specs, out_shape = in_specs + c_in, args + c_args, out_specs + c_out, out_shape + c_shape
    out = pl.pallas_call(
        functools.partial(_delta_body, n_chunks=n_chunks, bb=bb, has_init=has_init, emit_state=emit_state,
                          n_cast=len(casts)),
        grid=(batch // bb, n_chunks),
        in_specs=in_specs,
        out_specs=out_specs,
        out_shape=out_shape,
        scratch_shapes=[pltpu.VMEM((bb, N_DIR, N_GROUPS, DK_A, GROUP * DV_A), F32), pltpu.VMEM((bb, t, A_W), F32),
                        pltpu.VMEM((ng, GROUP * CHUNK, GROUP * CHUNK), BF16),
                        pltpu.VMEM((ng, GROUP * CHUNK, GROUP * CHUNK), BF16),
                        pltpu.VMEM((ng, GROUP * CHUNK, GROUP * CHUNK), BF16),
                        pltpu.VMEM((ng, GROUP * CHUNK, GROUP * CHUNK), BF16),
                        pltpu.VMEM((ng, GROUP * CHUNK, GROUP * DK_A), BF16),
                        pltpu.VMEM((ng, GROUP * CHUNK, 2 * GROUP * DK_A), BF16),
                        pltpu.VMEM((ng, GROUP * DK_A, GROUP * DV_A), BF16),
                        pltpu.VMEM((ng, GROUP * CHUNK, GROUP * DV_A), BF16)],
        compiler_params=_params(("arbitrary", "arbitrary")),
        name="delta_scan",
    )(*args)
    return [out[0].reshape(batch * t, A_W), *out[1:]]


CAUG_W = DV_B + LANES
MW = H_B * CAUG_W


def _scan_max(x, d):
    row = lax.broadcasted_iota(jnp.int32, x.shape, 0)
    s = 1
    while s < CHUNK:
        if d == 0:
            shifted = jnp.where(row >= s, pltpu.roll(x, s, 0), NEG)
        else:
            shifted = jnp.where(row < CHUNK - s, pltpu.roll(x, CHUNK - s, 0), NEG)
        x = jnp.maximum(x, shifted)
        s *= 2
    return x


def _mlstm_body(*refs, n_chunks, bb, has_init, emit_state, n_cast):
    rest = refs[0:2]
    gt = refs[2:4]
    norm_ref = refs[4]
    pos = 5
    c0_ref = m0_ref = None
    if has_init:
        c0_ref, m0_ref = refs[pos], refs[pos + 1]
        pos += 2
    cast_in = refs[pos:pos + n_cast]
    pos += n_cast
    yb_ref = refs[pos]
    pos += 1
    cout_ref = nout_ref = mout_ref = None
    if emit_state:
        cout_ref, nout_ref, mout_ref = refs[pos:pos + 3]
        pos += 3
    for w_ref, w16_ref in zip(cast_in, refs[pos:pos + n_cast]):
        w16_ref[...] = w_ref[...].astype(BF16)
    pos += n_cast
    c_scr, m_scr, o_scr, bdk, bdc, bdv = refs[pos:pos + 6]

    n = pl.program_id(1)
    qw = H_B * DK_B

    @pl.when((n == 0) & (pl.program_id(0) == 0))
    def _():
        for ref in (bdk, bdc, bdv):
            ref[...] = jnp.zeros_like(ref)
        for i in range(bb * N_DIR):
            for h in range(H_B):
                bdv[i, h * CHUNK:(h + 1) * CHUNK, h * CAUG_W + DV_B:(h + 1) * CAUG_W] = jnp.ones((CHUNK, LANES), BF16)

    @pl.when(n == 0)
    def _():
        o_scr[...] = jnp.zeros_like(o_scr)
        if has_init:
            c_scr[...] = c0_ref[...]
            m_scr[...] = m0_ref[...]
        else:
            c_scr[...] = jnp.zeros_like(c_scr)
            m_scr[...] = jnp.zeros_like(m_scr)

    row = lax.broadcasted_iota(jnp.int32, (CHUNK, LANES), 0)
    lane = lax.broadcasted_iota(jnp.int32, (CHUNK, LANES), 1)
    col = lane & (CHUNK - 1)
    left = lane < CHUNK
    dirs = [(bi, d) for bi in range(bb) for d in range(N_DIR)]
    slot = {key: i for i, key in enumerate(dirs)}
    q16 = {}
    for key in dirs:
        bi, d = key
        i = slot[key]
        q16[key] = rest[d][bi, :, OFF_BQ:OFF_BQ + qw].astype(BF16)
        _store_blocks(bdk.at[i], rest[d][bi, :, OFF_BK:OFF_BK + qw].astype(BF16), CHUNK, DK_B)
        for h in range(H_B):
            bdv[i, h * CHUNK:(h + 1) * CHUNK, h * CAUG_W:h * CAUG_W + DV_B] = (
                rest[d][bi, :, OFF_BV + h * DV_B:OFF_BV + (h + 1) * DV_B].astype(BF16))
        _store_blocks(bdc.at[i], c_scr[bi, d].astype(BF16), DK_B, CAUG_W)
    qk = {key: lax.dot_general(q16[key], bdk[slot[key]], (((1,), (1,)), ((), ())), preferred_element_type=F32)
          for key in dirs}
    nc, a_t, iw, emt, ksc, dec_row = {}, {}, {}, {}, {}, {}
    for key in dirs:
        bi, d = key
        lo = LF_OFF + d * H_B
        mine = (lane >= lo) & (lane < lo + H_B)
        g = gt[d][bi]
        gc = jnp.where(mine, _dot_f32(_masks(d)[0].astype(F32), g), 0.0)
        a = jnp.where(mine, pltpu.roll(g, LF_OFF - LI_OFF, 1), 0.0) - gc
        last = CHUNK - 1 if d == 0 else 0
        m_old = m_scr[bi, d][0:1, :]
        mx = jnp.maximum(m_old, _scan_max(a, d))
        mxl = mx[last:last + 1, :]
        nc[key] = -mx
        a_t[key] = jnp.concatenate([a, a], axis=0).T
        iw[key] = jnp.exp(m_old - mx)
        emt[key] = jnp.exp(-(gc + mx))
        ksc[key] = jnp.exp(a - mxl)
        dec_row[key] = jnp.exp(m_old - mxl)
        m_scr[bi, d] = jnp.broadcast_to(gc[last:last + 1, :] + mxl, (SUBLANES, LANES))

    ks_t = {}
    for key in dirs:
        bi, d = key
        lo = LF_OFF + d * H_B
        ks = rest[d][bi, :, OFF_BK:OFF_BK + qw] * jnp.concatenate(
            [_col_bcast(ksc[key], lo + h) for h in range(H_B)], axis=1)
        ks_t[key] = jnp.concatenate([ks[:, h * DK_B:(h + 1) * DK_B] for h in range(H_B)], axis=0).T.astype(BF16)
    lhs = {}
    for key in dirs:
        bi, d = key
        lo = LF_OFF + d * H_B
        incl = row >= col if d == 0 else row <= col
        log_w = []
        for p in range(H_B // PAIR):
            l0, l1 = lo + PAIR * p, lo + PAIR * p + 1
            ccol = jnp.where(left, _col_bcast(nc[key], l0), _col_bcast(nc[key], l1))
            crow = jnp.where(left[0:1], a_t[key][l0:l0 + 1, :], a_t[key][l1:l1 + 1, :])
            log_w.append(jnp.where(incl, ccol + crow, NEG))
        dw = jnp.exp(jnp.concatenate(log_w, axis=1)) * qk[key]
        iw_x = jnp.concatenate([_col_bcast(iw[key], lo + h) for h in range(H_B)], axis=1)
        lhs[key] = ((rest[d][bi, :, OFF_BQ:OFF_BQ + qw] * iw_x).astype(BF16), dw.astype(BF16))
    num = {}
    for key in dirs:
        i = slot[key]
        parts = []
        for p in range(H_B // PAIR):
            ql, dl, cl = slice(p * PAIR * DK_B, (p + 1) * PAIR * DK_B), slice(p * LANES, (p + 1) * LANES), \
                slice(p * PAIR * CAUG_W, (p + 1) * PAIR * CAUG_W)
            parts.append(jnp.dot(jnp.concatenate([lhs[key][0][:, ql], lhs[key][1][:, dl]], axis=1),
                                 jnp.concatenate([bdc[i, ql, cl], bdv[i, dl, cl]], axis=0),
                                 preferred_element_type=F32))
        num[key] = jnp.concatenate(parts, axis=1)
    upd = {key: jnp.dot(ks_t[key], bdv[slot[key]], preferred_element_type=F32) for key in dirs}
    for key in dirs:
        bi, d = key
        lo = LF_OFF + d * H_B
        dec_x = jnp.concatenate(
            [jnp.broadcast_to(dec_row[key][:, lo + h:lo + h + 1], (1, CAUG_W)) for h in range(H_B)], axis=1)
        c_scr[bi, d] = c_scr[bi, d] * dec_x + upd[key]
    heads = [(key, h) for key in dirs for h in range(H_B)]
    rows = {d: pl.ds(pl.multiple_of((n if d == 0 else n_chunks - 1 - n) * CHUNK, CHUNK), CHUNK) for d in range(N_DIR)}
    tots = []
    for key, h in heads:
        bi, d = key
        vs = slice(h * DV_B, (h + 1) * DV_B)
        den = jnp.maximum(jnp.abs(num[key][:, h * CAUG_W + DV_B:(h + 1) * CAUG_W]),
                          _col_bcast(emt[key], LF_OFF + d * H_B + h))
        hb = jnp.concatenate([num[key][:, h * CAUG_W:h * CAUG_W + LANES] / den,
                              num[key][:, h * CAUG_W + LANES:h * CAUG_W + DV_B] / den], axis=1)
        tot = hb + o_scr[bi, rows[d], vs]
        o_scr[bi, rows[d], vs] = tot
        tots.append(tot)
    inv = [lax.rsqrt(jnp.mean(tot * tot, axis=-1, keepdims=True) + EPS) for tot in tots]
    for (key, h), tot, r in zip(heads, tots, inv):
        bi, d = key
        vs = slice(h * DV_B, (h + 1) * DV_B)
        ogate = rest[d][bi, :, OFF_BO + h * DV_B:OFF_BO + (h + 1) * DV_B]
        yb_ref[bi, rows[d], vs] = (tot * r * norm_ref[...] * ogate).astype(BF16)

    if emit_state:
        @pl.when(n == n_chunks - 1)
        def _():
            for bi, d in dirs:
                for h in range(H_B):
                    cout_ref[bi, d, h] = c_scr[bi, d, :, h * CAUG_W:h * CAUG_W + DV_B]
                    nout_ref[bi, d, h] = c_scr[bi, d, :, h * CAUG_W + DV_B:(h + 1) * CAUG_W]
            mout_ref[...] = m_scr[...]


def _mlstm(rest, gates, norm_b, c0, m0, batch, n_chunks, emit_state, casts=()):
    t = n_chunks * CHUNK
    has_init = c0 is not None
    bb = SCAN_BB
    rest, gates = (a.reshape(batch, t, a.shape[-1]) for a in (rest, gates))
    mspec = pl.BlockSpec((bb, N_DIR, SUBLANES, LANES), lambda b, n: (b, 0, 0, 0))
    in_specs = [*_scan_specs(bb, n_chunks, REST_W), *_scan_specs(bb, n_chunks, LANES),
                pl.BlockSpec((1, DV_B), lambda b, n: (0, 0))]
    args = [rest, rest, gates, gates, norm_b]
    if has_init:
        in_specs += [pl.BlockSpec((bb, N_DIR, DK_B, MW), lambda b, n: (b, 0, 0, 0)), mspec]
        args += [c0, m0]
    out_specs = [pl.BlockSpec((bb, t, B_W), lambda b, n: (b, 0, 0))]
    out_shape = [jax.ShapeDtypeStruct((batch, t, B_W), BF16)]
    if emit_state:
        out_specs += [pl.BlockSpec((bb, N_DIR, H_B, DK_B, DV_B), lambda b, n: (b, 0, 0, 0, 0)),
                      pl.BlockSpec((bb, N_DIR, H_B, DK_B, LANES), lambda b, n: (b, 0, 0, 0, 0)), mspec]
        out_shape += [jax.ShapeDtypeStruct((batch, N_DIR, H_B, DK_B, DV_B), F32),
                      jax.ShapeDtypeStruct((batch, N_DIR, H_B, DK_B, LANES), F32),
                      jax.ShapeDtypeStruct((batch, N_DIR, SUBLANES, LANES), F32)]
    steps = (batch // bb) * n_chunks
    c_in, c_out, c_shape, c_args = _cast_specs([(w, l, steps) for w, l in casts], lambda b, n: b * n_chunks + n)
    in_specs, args, out_specs, out_shape = in_specs + c_in, args + c_args, out_specs + c_out, out_shape + c_shape
    out = pl.pallas_call(
        functools.partial(_mlstm_body, n_chunks=n_chunks, bb=bb, has_init=has_init, emit_state=emit_state,
                          n_cast=len(casts)),
        grid=(batch // bb, n_chunks),
        in_specs=in_specs,
        out_specs=out_specs,
        out_shape=out_shape,
        scratch_shapes=[pltpu.VMEM((bb, N_DIR, DK_B, MW), F32),
                        pltpu.VMEM((bb, N_DIR, SUBLANES, LANES), F32),
                        pltpu.VMEM((bb, t, B_W), F32),
                        pltpu.VMEM((bb * N_DIR, H_B * CHUNK, H_B * DK_B), BF16),
                        pltpu.VMEM((bb * N_DIR, H_B * DK_B, MW), BF16),
                        pltpu.VMEM((bb * N_DIR, H_B * CHUNK, MW), BF16)],
        compiler_params=_params(("arbitrary", "arbitrary")),
        name="mlstm_scan",
    )(*args)
    return [out[0].reshape(batch * t, B_W), *out[1:]]


OUT_TM = 512
OUT_RB = 256
FFN_TM = 512
FFN_FC = 1024


def _outproj_body(ya_ref, yb_ref, wa_ref, wb_ref, x_ref, mod_ref, post1_ref, pre2_ref, x1_ref, h2_ref):
    for rb in range(x_ref.shape[0] // OUT_RB):
        rs = slice(rb * OUT_RB, (rb + 1) * OUT_RB)
        mix = (jnp.dot(ya_ref[rs, :], wa_ref[...], preferred_element_type=F32)
               + jnp.dot(yb_ref[rs, :], wb_ref[...], preferred_element_type=F32))
        x1 = x_ref[rs, :] + mod_ref[0, 2:3, :] * (_rms(mix) * post1_ref[...])
        x1_ref[rs, :] = x1
        h2 = _rms(x1) * pre2_ref[...] * (1.0 + mod_ref[0, 4:5, :]) + mod_ref[0, 3:4, :]
        h2_ref[rs, :] = h2.astype(BF16)


def _outproj(ya, yb, w_out, x2d, mod3, mod_map, post1, pre2, tm):
    m = x2d.shape[0]
    row = lambda i: (i, 0)
    const = lambda i: (0, 0)
    return pl.pallas_call(
        _outproj_body,
        grid=(m // tm,),
        in_specs=[pl.BlockSpec((tm, A_W), row), pl.BlockSpec((tm, B_W), row),
                  pl.BlockSpec((A_W, D_MODEL), lambda i: (0, 0)),
                  pl.BlockSpec((B_W, D_MODEL), lambda i: (1, 0)),
                  pl.BlockSpec((tm, D_MODEL), row),
                  pl.BlockSpec((1, 6, D_MODEL), lambda i: (mod_map(i * tm), 0, 0)),
                  pl.BlockSpec((1, D_MODEL), const), pl.BlockSpec((1, D_MODEL), const)],
        out_specs=[pl.BlockSpec((tm, D_MODEL), row), pl.BlockSpec((tm, D_MODEL), row)],
        out_shape=[jax.ShapeDtypeStruct((m, D_MODEL), F32), jax.ShapeDtypeStruct((m, D_MODEL), BF16)],
        compiler_params=_params(("parallel",)),
        name="outproj",
    )(ya, yb, w_out, w_out, x2d, mod3, post1, pre2)


def _ffn_body(h2_ref, w1_ref, w2_ref, x1_ref, mod_ref, post2_ref, o_ref):
    kk = pl.program_id(1)

    @pl.when(kk == 0)
    def _():
        o_ref[...] = jnp.zeros_like(o_ref)

    def accumulate(rs):
        a = jnp.maximum(jnp.dot(h2_ref[rs, :], w1_ref[...], preferred_element_type=F32), 0.0)
        o_ref[rs, :] += jnp.dot((a * a).astype(BF16), w2_ref[...], preferred_element_type=F32)

    last = pl.num_programs(1) - 1
    tm = o_ref.shape[0]

    @pl.when(kk < last)
    def _():
        accumulate(slice(0, tm))

    @pl.when(kk == last)
    def _():
        for rs in (slice(0, tm // 2), slice(tm // 2, tm)):
            accumulate(rs)
            o_ref[rs, :] = x1_ref[rs, :] + mod_ref[0, 5:6, :] * (_rms(o_ref[rs, :]) * post2_ref[...])


def _ffn(h2, w1, w2, x1, mod3, mod_map, post2, tm, fc):
    m = h2.shape[0]
    return pl.pallas_call(
        _ffn_body,
        grid=(m // tm, FFN // fc),
        in_specs=[pl.BlockSpec((tm, D_MODEL), lambda i, k: (i, 0)),
                  pl.BlockSpec((D_MODEL, fc), lambda i, k: (0, k)),
                  pl.BlockSpec((fc, D_MODEL), lambda i, k: (k, 0)),
                  pl.BlockSpec((tm, D_MODEL), lambda i, k: (i, 0)),
                  pl.BlockSpec((1, 6, D_MODEL), lambda i, k: (mod_map(i * tm), 0, 0)),
                  pl.BlockSpec((1, D_MODEL), lambda i, k: (0, 0))],
        out_specs=pl.BlockSpec((tm, D_MODEL), lambda i, k: (i, 0)),
        out_shape=jax.ShapeDtypeStruct((m, D_MODEL), F32),
        compiler_params=_params(("parallel", "arbitrary")),
        name="ffn",
    )(h2, w1, w2, x1, mod3, post2)


def _block(x, mod3, mod_of_row, lp, init, seq_len, emit_state):
    bsz, t, _ = x.shape
    x2d = x.reshape(bsz * t, D_MODEL)
    n_chunks = t // CHUNK
    qkv, rest, gates = _proj(x2d, mod3, lambda i: mod_of_row(i * PROJ_TM), lp["pre1"], lp["w16"],
                             lp["w_mid"], lp["w_gate"], lp["conv_w"], lp["gate_p"], seq_len)
    s0, c0, m0 = init if init is not None else (None, None, None)
    first = "w1" not in lp
    l = lp["l"]
    d_out = _delta(qkv, rest, gates, lp["norm_a"], s0, bsz, n_chunks, emit_state,
                   [(lp["w_ffn1"], l), (lp["w_ffn2"], l), (lp["w_out32"], l)] if first else ())
    m_out = _mlstm(rest, gates, lp["norm_b"], c0, m0, bsz, n_chunks, emit_state)
    if first:
        lp["w1"], lp["w2"], lp["w_out"] = d_out[-3:]
    x1, h2 = _outproj(d_out[0], m_out[0], lp["w_out"], x2d, mod3, mod_of_row, lp["post1"], lp["pre2"], OUT_TM)
    y = _ffn(h2, lp["w1"], lp["w2"], x1, mod3, mod_of_row, lp["post2"], FFN_TM, FFN_FC)
    states = None
    if emit_state:
        m_fin = jnp.stack([m_out[3][:, d, 0, LF_OFF + d * H_B:LF_OFF + (d + 1) * H_B] for d in range(N_DIR)], axis=1)
        states = (d_out[1], m_out[1], m_out[2][..., 0], m_fin)
    return y.reshape(bsz, t, D_MODEL), states


def _layer_params(l, norm_mix_pre, norm_mix_post, norm_ffn_pre, norm_ffn_post, w_in, conv_w, a_log, dt_bias,
                  norm_a, mlstm_ibias, mlstm_fbias, norm_b, w_out, w_ffn1, w_ffn2):
    w = w_in[l]
    o_ag = QKV_W
    o_aa = o_ag + A_W
    o_ab = o_aa + N_DIR * H_A
    o_bq = o_ab + N_DIR * H_A
    o_bi = o_bq + 2 * H_B * DK_B + 2 * B_W
    o_bf = o_bi + N_DIR * H_B
    n_gate = 2 * N_DIR * H_A + 2 * N_DIR * H_B
    w16 = w.astype(BF16)
    w_gate = jnp.concatenate([w16[:, o_aa:o_bq], w16[:, o_bi:o_bf + N_DIR * H_B],
                              jnp.zeros((D_MODEL, LANES - n_gate), BF16)], axis=1)

    def lane_row(vals, off):
        return jnp.zeros((LANES,), F32).at[off:off + vals.size].set(vals.reshape(-1))

    gate_p = jnp.stack([lane_row(a_log[l], G_OFF), lane_row(dt_bias[l], G_OFF),
                        lane_row(mlstm_ibias[l], LI_OFF) + lane_row(mlstm_fbias[l], LF_OFF)]
                       + [jnp.zeros((LANES,), F32)] * (SUBLANES - 3))
    row = lambda v: v[l].reshape(1, -1)
    return dict(
        pre1=row(norm_mix_pre), post1=row(norm_mix_post), pre2=row(norm_ffn_pre), post2=row(norm_ffn_post),
        w16=w16, w_mid=w16[:, o_bq:o_bi], w_gate=w_gate, gate_p=gate_p,
        conv_w=jnp.concatenate([conv_w[l].T, jnp.zeros((SUBLANES - 3, QKV_W), F32)], axis=0),
        norm_a=row(norm_a), norm_b=row(norm_b),
        l=l, w_out32=w_out, w_ffn1=w_ffn1, w_ffn2=w_ffn2)


def kernel(x_prompt, x_sample, state_delta, state_mlstm_C, state_mlstm_n, state_mlstm_m, c, c_ctx, w_ada, b_ada, norm_mix_pre, norm_mix_post, norm_ffn_pre, norm_ffn_post, w_in, conv_w, a_log, dt_bias, norm_a, mlstm_ibias, mlstm_fbias, norm_b, w_out, w_ffn1, w_ffn2):
    depth = w_in.shape[0]
    n_lat = x_sample.shape[0]
    t_lat = x_sample.shape[1]
    cond = jnp.concatenate([c_ctx[None, :], c, jnp.zeros((SUBLANES - 1 - n_lat, D_MODEL), F32)], axis=0)
    y_prompt, y_sample = x_prompt, x_sample
    acc = ([], [], [], [])
    for l in range(depth):
        lp = _layer_params(l, norm_mix_pre, norm_mix_post, norm_ffn_pre, norm_ffn_post, w_in, conv_w, a_log,
                           dt_bias, norm_a, mlstm_ibias, mlstm_fbias, norm_b, w_out, w_ffn1, w_ffn2)
        mod = _ada(cond, w_ada, b_ada[l].reshape(1, -1), l)
        mod3 = mod[:1 + n_lat].reshape(1 + n_lat, 6, D_MODEL)
        y_prompt, st = _block(y_prompt, mod3, lambda r: 0, lp, None, x_prompt.shape[1], True)
        for a, s in zip(acc, st):
            a.append(s)
        n_rep = jnp.broadcast_to(state_mlstm_n[:, l][..., None], state_mlstm_n[:, l].shape + (LANES,))
        c_aug0 = jnp.concatenate([state_mlstm_C[:, l], n_rep], axis=-1)
        c_aug0 = c_aug0.transpose(0, 1, 3, 2, 4).reshape(n_lat, N_DIR, DK_B, MW)
        m0 = jnp.zeros((n_lat, N_DIR, LANES), F32)
        for d in range(N_DIR):
            m0 = m0.at[:, d, LF_OFF + d * H_B:LF_OFF + (d + 1) * H_B].set(state_mlstm_m[:, l, d])
        m0 = jnp.broadcast_to(m0[:, :, None, :], (n_lat, N_DIR, SUBLANES, LANES))
        y_sample, _ = _block(y_sample, mod3, lambda r: 1 + r // t_lat, lp, (state_delta[:, l], c_aug0, m0),
                             GRID_W, False)
    return (y_prompt, y_sample) + tuple(jnp.stack(a, axis=1) for a in acc)
```

```python
import functools

import jax
import jax.numpy as jnp
from jax import lax
from jax.experimental import pallas as pl
from jax.experimental.pallas import tpu as pltpu

F32 = jnp.float32
BF16 = jnp.bfloat16

D_MODEL = 2048
N_DIR = 2
A_W = D_MODEL // 2
B_W = D_MODEL - A_W
DK_A = 128
DV_A = 128
H_A = A_W // DV_A
DV_B = 256
DK_B = DV_B // 2
H_B = B_W // DV_B
GRID_W = 64
CHUNK = 64
FFN = 4 * D_MODEL
EPS = 1e-6
LANES = 128
SUBLANES = 8
NEG = -1e30

QKV_W = 3 * A_W
REST_W = A_W + 2 * H_B * DK_B + 2 * B_W
OFF_AG, OFF_BQ, OFF_BK, OFF_BV, OFF_BO = 0, A_W, A_W + H_B * DK_B, A_W + 2 * H_B * DK_B, A_W + 2 * H_B * DK_B + B_W
G_OFF, BETA_OFF = 0, N_DIR * H_A
LI_OFF = 2 * N_DIR * H_A
LF_OFF = LI_OFF + N_DIR * H_B

VMEM_LIMIT = 56 * 1024 * 1024


def _sigmoid(x):
    return 1.0 / (1.0 + jnp.exp(-x))


def _softplus(x):
    return jnp.maximum(x, 0.0) + jnp.log1p(jnp.exp(-jnp.abs(x)))


def _dot(a, b):
    return jnp.dot(a.astype(BF16), b.astype(BF16), preferred_element_type=F32)


def _rms(x):
    return x * lax.rsqrt(jnp.mean(x * x, axis=-1, keepdims=True) + EPS)


def _params(sem):
    return pltpu.CompilerParams(dimension_semantics=sem, vmem_limit_bytes=VMEM_LIMIT)


def _ada_body(c_ref, w_ref, b_ref, o_ref):
    c = c_ref[...]
    o_ref[...] = _dot(c * _sigmoid(c), w_ref[...]) + b_ref[...]


def _ada(c_all, w_ada, b, l):
    n = w_ada.shape[2]
    tn = 1024
    return pl.pallas_call(
        _ada_body,
        grid=(n // tn,),
        in_specs=[pl.BlockSpec(c_all.shape, lambda j: (0, 0)),
                  pl.BlockSpec((None, D_MODEL, tn), lambda j: (l, 0, j)),
                  pl.BlockSpec((1, tn), lambda j: (0, j))],
        out_specs=pl.BlockSpec((c_all.shape[0], tn), lambda j: (0, j)),
        out_shape=jax.ShapeDtypeStruct((c_all.shape[0], n), F32),
        compiler_params=_params(("arbitrary",)),
        name="ada",
    )(c_all, w_ada, b)


PROJ_TM = 1024
PROJ_RB = 512
PROJ_CB = 256
PROJ_TN = 512
N_QKV_T = QKV_W // PROJ_TN
N_REST_T = REST_W // PROJ_TN
N_HEAD_T = (QKV_W + A_W) // PROJ_TN
N_MID_T = N_QKV_T + N_REST_T - N_HEAD_T


def _proj_body(x_ref, mod_ref, g_ref, wh_ref, wb_ref, wg_ref, cw_ref, gp_ref, qkv_ref, rest_ref, gate_ref, h_scr, *,
               seq_len):
    j = pl.program_id(1)
    row_blocks = [slice(rb * PROJ_RB, (rb + 1) * PROJ_RB) for rb in range(PROJ_TM // PROJ_RB)]
    col_blocks = [slice(cb * PROJ_CB, (cb + 1) * PROJ_CB) for cb in range(PROJ_TN // PROJ_CB)]

    def run_rows(rs, w_ref, epilogue, out_ref):
        for cs in col_blocks:
            epilogue(jnp.dot(h_scr[rs, :], w_ref[:, cs], preferred_element_type=F32), out_ref, rs, cs)

    def run(w_ref, epilogue, out_ref):
        for rs in row_blocks:
            run_rows(rs, w_ref, epilogue, out_ref)

    def run_first():
        for rs in row_blocks:
            y = _rms(x_ref[rs, :]) * g_ref[...]
            h_scr[rs, :] = (y * (1.0 + mod_ref[0, 1:2, :]) + mod_ref[0, 0:1, :]).astype(BF16)
            run_rows(rs, wh_ref, ep_l2(DK_A ** -0.5), qkv_ref)
            ep_gate(jnp.dot(h_scr[rs, :], wg_ref[...], preferred_element_type=F32), gate_ref, rs)

    def conv_silu(acc, cs, g):
        a = acc[:, g * LANES:(g + 1) * LANES]
        ls = slice(cs.start + g * LANES, cs.start + (g + 1) * LANES)
        pos = lax.broadcasted_iota(jnp.int32, a.shape, 0) & (seq_len - 1)
        prev = jnp.where(pos == 0, 0.0, pltpu.roll(a, 1, 0))
        nxt = jnp.where(pos == seq_len - 1, 0.0, pltpu.roll(a, a.shape[0] - 1, 0))
        y = prev * cw_ref[0:1, ls] + a * cw_ref[1:2, ls] + nxt * cw_ref[2:3, ls]
        return ls, y * _sigmoid(y)

    def ep_l2(scale):
        def f(acc, out_ref, rs, cs):
            for g in range(PROJ_CB // LANES):
                ls, blk = conv_silu(acc, cs, g)
                inv = lax.rsqrt(jnp.sum(blk * blk, axis=-1, keepdims=True) + EPS)
                out_ref[rs, ls] = blk * (inv * scale)
        return f

    def ep_conv(acc, out_ref, rs, cs):
        for g in range(PROJ_CB // LANES):
            ls, blk = conv_silu(acc, cs, g)
            out_ref[rs, ls] = blk

    def ep_map(fn):
        def f(acc, out_ref, rs, cs):
            out_ref[rs, cs] = fn(acc)
        return f

    def ep_gate(z, out_ref, rs):
        lane = lax.broadcasted_iota(jnp.int32, z.shape, 1)
        g = -jnp.exp(gp_ref[0:1, :]) * _softplus(z + gp_ref[1:2, :])
        li = z + gp_ref[2:3, :]
        out_ref[rs, :] = jnp.where(lane < BETA_OFF, g,
                                   jnp.where(lane < LI_OFF, _sigmoid(z),
                                             jnp.where(lane < LF_OFF, li,
                                                       jnp.where(lane < LF_OFF + N_DIR * H_B, -_softplus(-li), 0.0))))

    tp = A_W // PROJ_TN
    c0 = (j - N_QKV_T) * PROJ_TN
    in_rest = (j >= N_QKV_T) & (j < N_QKV_T + N_REST_T)
    pl.when(j == 0)(run_first)
    variants = [
        ((j > 0) & (j < tp), wh_ref, ep_l2(DK_A ** -0.5), qkv_ref),
        ((j >= tp) & (j < 2 * tp), wh_ref, ep_l2(1.0), qkv_ref),
        ((j >= 2 * tp) & (j < N_QKV_T), wh_ref, ep_conv, qkv_ref),
        (in_rest & (c0 < OFF_BQ), wh_ref, ep_map(lambda a: a * _sigmoid(a)), rest_ref),
        (in_rest & (c0 >= OFF_BQ) & (c0 < OFF_BK), wb_ref, ep_map(lambda a: a * (DK_B ** -0.5)), rest_ref),
        (in_rest & (c0 >= OFF_BK) & (c0 < OFF_BO), wb_ref, ep_map(lambda a: a), rest_ref),
        (in_rest & (c0 >= OFF_BO), wb_ref, ep_map(_sigmoid), rest_ref),
    ]
    for cond, w_ref, epilogue, out_ref in variants:
        pl.when(cond)(functools.partial(run, w_ref, epilogue, out_ref))


def _proj(x2d, mod3, mod_map, pre_g, w16, w_mid, w_gate, conv_w, gate_p, seq_len):
    m = x2d.shape[0]
    tm, tn = PROJ_TM, PROJ_TN
    assert PROJ_RB % seq_len == 0 and m % tm == 0
    return pl.pallas_call(
        functools.partial(_proj_body, seq_len=seq_len),
        grid=(m // tm, N_HEAD_T + N_MID_T),
        in_specs=[pl.BlockSpec((tm, D_MODEL), lambda i, j: (i, 0)),
                  pl.BlockSpec((1, 6, D_MODEL), lambda i, j: (mod_map(i), 0, 0)),
                  pl.BlockSpec((1, D_MODEL), lambda i, j: (0, 0)),
                  pl.BlockSpec((D_MODEL, tn), lambda i, j: (0, jnp.minimum(j, N_HEAD_T - 1))),
                  pl.BlockSpec((D_MODEL, tn), lambda i, j: (0, jnp.clip(j - N_HEAD_T, 0, N_MID_T - 1))),
                  pl.BlockSpec((D_MODEL, LANES), lambda i, j: (0, 0)),
                  pl.BlockSpec((SUBLANES, tn), lambda i, j: (0, jnp.minimum(j, N_QKV_T - 1))),
                  pl.BlockSpec((SUBLANES, LANES), lambda i, j: (0, 0))],
        out_specs=[pl.BlockSpec((tm, tn), lambda i, j: (i, jnp.minimum(j, N_QKV_T - 1))),
                   pl.BlockSpec((tm, tn), lambda i, j: (i, jnp.clip(j - N_QKV_T, 0, N_REST_T - 1))),
                   pl.BlockSpec((tm, LANES), lambda i, j: (i, 0))],
        out_shape=[jax.ShapeDtypeStruct((m, QKV_W), F32), jax.ShapeDtypeStruct((m, REST_W), F32),
                   jax.ShapeDtypeStruct((m, LANES), F32)],
        scratch_shapes=[pltpu.VMEM((tm, D_MODEL), BF16)],
        compiler_params=_params(("parallel", "arbitrary")),
        name="proj",
    )(x2d, mod3, pre_g, w16, w_mid, w_gate, conv_w, gate_p)


GROUP = 4
N_GROUPS = H_A // GROUP
PAIR = LANES // CHUNK


def _split(x):
    hi = x.astype(BF16)
    return hi, (x - hi.astype(F32)).astype(BF16)


def _store_blocks(ref, x, blk_r, blk_c, col0=0):
    for h in range(GROUP):
        ref[h * blk_r:(h + 1) * blk_r, col0 + h * blk_c:col0 + (h + 1) * blk_c] = x[:, h * blk_c:(h + 1) * blk_c]


def _dot3(a_hi, a_lo, b_hi, b_lo):
    m = a_hi.shape[0]
    r = jnp.dot(jnp.concatenate([a_hi, a_lo], axis=0), b_hi, preferred_element_type=F32)
    return r[:m] + r[m:] + jnp.dot(a_hi, b_lo, preferred_element_type=F32)


HALF = CHUNK // 2


def _tri_inverse_wide(lmats, upper, bd_hi, bd_lo, by_hi, by_lo):
    width = lmats[0].shape[1]
    n_blk = width // HALF
    row = lax.broadcasted_iota(jnp.int32, (HALF, width), 0)
    lane = lax.broadcasted_iota(jnp.int32, (HALF, width), 1)
    lead = (lane & (CHUNK - 1)) < HALF
    eye_d = (row == (lane & (HALF - 1))).astype(F32)

    def bd_dot(i, a, b_hi, b_lo):
        for blk in range(n_blk):
            sl = slice(blk * HALF, (blk + 1) * HALF)
            bd_hi[i, sl, sl] = b_hi[:, sl]
            bd_lo[i, sl, sl] = b_lo[:, sl]
        return _dot3(*a, bd_hi[i], bd_lo[i])

    idx = range(len(lmats))
    diag = [jnp.where(lead, l[:HALF], l[HALF:]) for l in lmats]
    s = [eye_d - dg for dg in diag]
    p = []
    for i in idx:
        m_hi, m_lo = _split(-diag[i])
        p.append(bd_dot(i, (m_hi, m_lo), m_hi, m_lo))
    for _ in range(3):
        for i in idx:
            p_hi, p_lo = _split(p[i])
            s_hi, s_lo = _split(s[i])
            r = bd_dot(i, (jnp.concatenate([p_hi, s_hi], axis=0), jnp.concatenate([p_lo, s_lo], axis=0)), p_hi, p_lo)
            p[i] = r[:HALF]
            s[i] = s[i] + r[HALF:]
    for i in idx:
        p_hi, p_lo = _split(p[i])
        s[i] = s[i] + bd_dot(i, _split(s[i]), p_hi, p_lo)
    y = []
    for i in idx:
        c_blk = jnp.where(lead, 0.0, lmats[i][:HALF]) if upper[i] else jnp.where(lead, lmats[i][HALF:], 0.0)
        y.append(bd_dot(i, _split(c_blk), *_split(s[i])))
    out = []
    for i in idx:
        y_hi, y_lo = _split(y[i])
        for h in range(width // CHUNK):
            a_sl, b_sl = slice(h * CHUNK, h * CHUNK + HALF), slice(h * CHUNK + HALF, (h + 1) * CHUNK)
            rs, cs = (a_sl, b_sl) if upper[i] else (b_sl, a_sl)
            by_hi[i, rs, cs] = y_hi[:, cs]
            by_lo[i, rs, cs] = y_lo[:, cs]
        ai, bi = jnp.where(lead, s[i], 0.0), jnp.where(lead, 0.0, s[i])
        x = _dot3(*_split(ai if upper[i] else bi), by_hi[i], by_lo[i])
        out.append(jnp.concatenate([ai - x, bi] if upper[i] else [ai, bi - x], axis=0))
    return out


def _col_bcast(tile, c, width=LANES):
    return jnp.broadcast_to(tile[:, c:c + 1], (tile.shape[0], width))


def _delta_body(*refs, n_chunks, bb, has_init, emit_state, n_cast):
    qkv = refs[0:2]
    ag = refs[2:4]
    gt = refs[4:6]
    norm_ref = refs[6]
    pos = 7
    s0_ref = None
    if has_init:
        s0_ref = refs[pos]
        pos += 1
    cast_in = refs[pos:pos + n_cast]
    pos += n_cast
    ya_ref = refs[pos]
    pos += 1
    sout_ref = None
    if emit_state:
        sout_ref = refs[pos]
        pos += 1
    for w_ref, w16_ref in zip(cast_in, refs[pos:pos + n_cast]):
        w16_ref[...] = w_ref[...].astype(BF16)
    pos += n_cast
    s_scr, o_scr, bdn_hi, bdn_lo, bdy_hi, bdy_lo, bdk, bduw, bds, bdv = refs[pos:pos + 10]

    n = pl.program_id(1)
    gw = GROUP * DK_A

    @pl.when((n == 0) & (pl.program_id(0) == 0))
    def _():
        for ref in (bdn_hi, bdn_lo, bdy_hi, bdy_lo, bdk, bduw, bds, bdv):
            ref[...] = jnp.zeros_like(ref)

    @pl.when(n == 0)
    def _():
        o_scr[...] = jnp.zeros_like(o_scr)
        for bi in range(bb):
            for d in range(N_DIR):
                for h in range(H_A):
                    blk = s0_ref[bi, d, h] if has_init else jnp.zeros((DK_A, DV_A), F32)
                    s_scr[bi, d, h // GROUP, :, (h % GROUP) * DV_A:(h % GROUP + 1) * DV_A] = blk

    row = lax.broadcasted_iota(jnp.int32, (CHUNK, LANES), 0)
    lane = lax.broadcasted_iota(jnp.int32, (CHUNK, LANES), 1)
    col = lane & (CHUNK - 1)
    left = lane < CHUNK
    probs = [(bi, d, g) for bi in range(bb) for d in range(N_DIR) for g in range(N_GROUPS)]
    masks = {0: (row >= col, row > col), 1: (row <= col, row < col)}
    gates, gc, gc_t, eg, egl = {}, {}, {}, {}, {}
    for bi in range(bb):
        for d in range(N_DIR):
            key = (bi, d)
            gates[key] = gt[d][bi]
            gc[key] = _scan_sum(gates[key], d)
            gc_t[key] = jnp.concatenate([gc[key], gc[key]], axis=0).T
            last = CHUNK - 1 if d == 0 else 0
            eg[key] = jnp.exp(gc[key])
            egl[key] = jnp.exp(gc[key][last:last + 1, :] - gc[key])

    def cols_of(d, g):
        return [G_OFF + d * H_A + g * GROUP + hl for hl in range(GROUP)]

    q16, ks, beta_xs, decays, grams = [], [], [], [], []
    for gi, (bi, d, g) in enumerate(probs):
        key = (bi, d)
        cols = cols_of(d, g)
        incl, _ = masks[d]
        k = qkv[d][bi, :, A_W + g * gw:A_W + (g + 1) * gw]
        beta_x = jnp.concatenate([_col_bcast(gates[key], BETA_OFF - G_OFF + c) for c in cols], axis=1)
        decay = []
        for p in range(GROUP // PAIR):
            c0, c1 = cols[PAIR * p], cols[PAIR * p + 1]
            gcol = jnp.where(left, _col_bcast(gc[key], c0), _col_bcast(gc[key], c1))
            grow = jnp.where(left[0:1], gc_t[key][c0:c0 + 1, :], gc_t[key][c1:c1 + 1, :])
            decay.append(jnp.exp(jnp.where(incl, gcol - grow, NEG)))
        decays.append(jnp.concatenate(decay, axis=1))
        q16.append(qkv[d][bi, :, g * gw:(g + 1) * gw].astype(BF16))
        ks.append(k)
        beta_xs.append(beta_x)
        _store_blocks(bdk.at[gi], k.astype(BF16), CHUNK, DK_A)
    for gi in range(len(probs)):
        grams.append(lax.dot_general(jnp.concatenate([q16[gi], (ks[gi] * beta_xs[gi]).astype(BF16)], axis=0),
                                     bdk[gi], (((1,), (1,)), ((), ())), preferred_element_type=F32))
    attns, lmats = [], []
    for gi, (bi, d, g) in enumerate(probs):
        strict_w = jnp.concatenate([masks[d][1]] * (GROUP // PAIR), axis=1)
        attns.append((grams[gi][:CHUNK] * decays[gi]).astype(BF16))
        lmats.append(jnp.where(strict_w, grams[gi][CHUNK:] * decays[gi], 0.0))
    ainvs = _tri_inverse_wide(lmats, [d == 1 for _, d, _ in probs], bdn_hi, bdn_lo, bdy_hi, bdy_lo)
    eg_xs, kd_ts = [], []
    for gi, (bi, d, g) in enumerate(probs):
        key = (bi, d)
        cols = cols_of(d, g)
        v = qkv[d][bi, :, 2 * A_W + g * gw:2 * A_W + (g + 1) * gw]
        eg_x = jnp.concatenate([_col_bcast(eg[key], c) for c in cols], axis=1)
        egl_x = jnp.concatenate([_col_bcast(egl[key], c) for c in cols], axis=1)
        eg_xs.append(eg_x)
        _store_blocks(bduw.at[gi], (v * beta_xs[gi]).astype(BF16), CHUNK, DV_A)
        _store_blocks(bduw.at[gi], (ks[gi] * (beta_xs[gi] * eg_x)).astype(BF16), CHUNK, DK_A, col0=gw)
        kd = ks[gi] * egl_x
        kd_ts.append(jnp.concatenate([kd[:, hl * DK_A:(hl + 1) * DK_A] for hl in range(GROUP)], axis=0)
                     .T.astype(BF16))
        _store_blocks(bds.at[gi], s_scr[bi, d, g].astype(BF16), DK_A, DV_A)
    uws = []
    for gi in range(len(probs)):
        t_hi, t_lo = _split(ainvs[gi])
        r = jnp.dot(jnp.concatenate([t_hi, t_lo], axis=0), bduw[gi], preferred_element_type=F32)
        uws.append(r[:CHUNK] + r[CHUNK:])
    ws_qs = []
    hw = gw // 2
    for gi in range(len(probs)):
        wq = jnp.concatenate([uws[gi][:, gw:].astype(BF16), q16[gi]], axis=0)
        ws_qs.append(jnp.concatenate(
            [jnp.dot(wq[:, p * hw:(p + 1) * hw], bds[gi, p * hw:(p + 1) * hw, p * hw:(p + 1) * hw],
                     preferred_element_type=F32) for p in range(2)], axis=1))
    for gi in range(len(probs)):
        _store_blocks(bdv.at[gi], (uws[gi][:, :gw] - ws_qs[gi][:CHUNK]).astype(BF16), CHUNK, DV_A)
    rs = []
    for gi in range(len(probs)):
        rs.append(jnp.dot(jnp.concatenate([attns[gi], kd_ts[gi]], axis=0), bdv[gi], preferred_element_type=F32))
    tots = []
    for gi, (bi, d, g) in enumerate(probs):
        cols = cols_of(d, g)
        last = CHUNK - 1 if d == 0 else 0
        o = ws_qs[gi][CHUNK:] * eg_xs[gi] + rs[gi][:CHUNK]
        eg_last = jnp.concatenate(
            [jnp.broadcast_to(eg[bi, d][last:last + 1, c:c + 1], (1, DV_A)) for c in cols], axis=1)
        s_scr[bi, d, g] = s_scr[bi, d, g] * eg_last + rs[gi][CHUNK:]
        cidx = n if d == 0 else n_chunks - 1 - n
        rows = pl.ds(pl.multiple_of(cidx * CHUNK, CHUNK), CHUNK)
        tot = o + o_scr[bi, rows, g * gw:(g + 1) * gw]
        o_scr[bi, rows, g * gw:(g + 1) * gw] = tot
        tots.append((rows, tot))
    blks = [(bi, d, rows, g * GROUP + hl, tot[:, hl * DV_A:(hl + 1) * DV_A])
            for (bi, d, g), (rows, tot) in zip(probs, tots) for hl in range(GROUP)]
    inv = [lax.rsqrt(jnp.mean(blk * blk, axis=-1, keepdims=True) + EPS) for *_, blk in blks]
    for (bi, d, rows, h, blk), r in zip(blks, inv):
        hs = slice(h * DV_A, (h + 1) * DV_A)
        ya_ref[bi, rows, hs] = (blk * r * norm_ref[...] * ag[d][bi, :, hs]).astype(BF16)

    if emit_state:
        @pl.when(n == n_chunks - 1)
        def _():
            for bi in range(bb):
                for d in range(N_DIR):
                    for h in range(H_A):
                        sout_ref[bi, d, h] = s_scr[bi, d, h // GROUP, :, (h % GROUP) * DV_A:(h % GROUP + 1) * DV_A]


SCAN_BB = 2


def _scan_specs(bb, n_chunks, width):
    return (pl.BlockSpec((bb, CHUNK, width), lambda b, n: (b, n, 0)),
            pl.BlockSpec((bb, CHUNK, width), lambda b, n: (b, n_chunks - 1 - n, 0)))


def _cast_specs(casts, steps_of):
    in_specs, out_specs, out_shape, args = [], [], [], []
    for w, l, steps in casts:
        rows = w.shape[1] // steps
        assert rows * steps == w.shape[1] and rows % (2 * SUBLANES) == 0
        in_specs.append(pl.BlockSpec((None, rows, w.shape[2]), lambda b, n, l=l: (l, steps_of(b, n), 0)))
        out_specs.append(pl.BlockSpec((rows, w.shape[2]), lambda b, n: (steps_of(b, n), 0)))
        out_shape.append(jax.ShapeDtypeStruct(w.shape[1:], BF16))
        args.append(w)
    return in_specs, out_specs, out_shape, args


def _delta(qkv, rest, gates, norm_a, s0, batch, n_chunks, emit_state, casts=()):
    t = n_chunks * CHUNK
    has_init = s0 is not None
    bb = SCAN_BB
    ng = bb * N_DIR * N_GROUPS
    qkv, rest, gates = (a.reshape(batch, t, a.shape[-1]) for a in (qkv, rest, gates))
    in_specs = [*_scan_specs(bb, n_chunks, QKV_W), *_scan_specs(bb, n_chunks, A_W), *_scan_specs(bb, n_chunks, LANES),
                pl.BlockSpec((1, DV_A), lambda b, n: (0, 0))]
    args = [qkv, qkv, rest, rest, gates, gates, norm_a]
    sspec = pl.BlockSpec((bb, N_DIR, H_A, DK_A, DV_A), lambda b, n: (b, 0, 0, 0, 0))
    if has_init:
        in_specs.append(sspec)
        args.append(s0)
    out_specs = [pl.BlockSpec((bb, t, A_W), lambda b, n: (b, 0, 0))]
    out_shape = [jax.ShapeDtypeStruct((batch, t, A_W), BF16)]
    if emit_state:
        out_specs.append(sspec)
        out_shape.append(jax.ShapeDtypeStruct((batch, N_DIR, H_A, DK_A, DV_A), F32))
    steps = (batch // bb) * n_chunks
    c_in, c_out, c_shape, c_args = _cast_specs([(w, l, steps) for w, l in casts], lambda b, n: b * n_chunks + n)
    in_specs, args, out_specs, out_shape = in_specs + c_in, args + c_args, out_specs + c_out, out_shape + c_shape
    out = pl.pallas_call(
        functools.partial(_delta_body, n_chunks=n_chunks, bb=bb, has_init=has_init, emit_state=emit_state,
                          n_cast=len(casts)),
        grid=(batch // bb, n_chunks),
        in_specs=in_specs,
        out_specs=out_specs,
        out_shape=out_shape,
        scratch_shapes=[pltpu.VMEM((bb, N_DIR, N_GROUPS, DK_A, GROUP * DV_A), F32), pltpu.VMEM((bb, t, A_W), F32),
                        pltpu.VMEM((ng, GROUP * CHUNK, GROUP * CHUNK), BF16),
                        pltpu.VMEM((ng, GROUP * CHUNK, GROUP * CHUNK), BF16),
                        pltpu.VMEM((ng, GROUP * CHUNK, GROUP * CHUNK), BF16),
                        pltpu.VMEM((ng, GROUP * CHUNK, GROUP * CHUNK), BF16),
                        pltpu.VMEM((ng, GROUP * CHUNK, GROUP * DK_A), BF16),
                        pltpu.VMEM((ng, GROUP * CHUNK, 2 * GROUP * DK_A), BF16),
                        pltpu.VMEM((ng, GROUP * DK_A, GROUP * DV_A), BF16),
                        pltpu.VMEM((ng, GROUP * CHUNK, GROUP * DV_A), BF16)],
        compiler_params=_params(("arbitrary", "arbitrary")),
        name="delta_scan",
    )(*args)
    return [out[0].reshape(batch * t, A_W), *out[1:]]


CAUG_W = DV_B + LANES
MW = H_B * CAUG_W


def _scan_rows(x, d, combine, fill):
    row = lax.broadcasted_iota(jnp.int32, x.shape, 0)
    s = 1
    while s < CHUNK:
        if d == 0:
            shifted = jnp.where(row >= s, pltpu.roll(x, s, 0), fill)
        else:
            shifted = jnp.where(row < CHUNK - s, pltpu.roll(x, CHUNK - s, 0), fill)
        x = combine(x, shifted)
        s *= 2
    return x


def _scan_max(x, d):
    return _scan_rows(x, d, jnp.maximum, NEG)


def _scan_sum(x, d):
    return _scan_rows(x, d, jnp.add, 0.0)


def _mlstm_body(*refs, n_chunks, bb, has_init, emit_state, n_cast):
    rest = refs[0:2]
    gt = refs[2:4]
    norm_ref = refs[4]
    pos = 5
    c0_ref = m0_ref = None
    if has_init:
        c0_ref, m0_ref = refs[pos], refs[pos + 1]
        pos += 2
    cast_in = refs[pos:pos + n_cast]
    pos += n_cast
    yb_ref = refs[pos]
    pos += 1
    cout_ref = nout_ref = mout_ref = None
    if emit_state:
        cout_ref, nout_ref, mout_ref = refs[pos:pos + 3]
        pos += 3
    for w_ref, w16_ref in zip(cast_in, refs[pos:pos + n_cast]):
        w16_ref[...] = w_ref[...].astype(BF16)
    pos += n_cast
    c_scr, m_scr, o_scr, bdk, bdc, bdv = refs[pos:pos + 6]

    n = pl.program_id(1)
    qw = H_B * DK_B

    @pl.when((n == 0) & (pl.program_id(0) == 0))
    def _():
        for ref in (bdk, bdc, bdv):
            ref[...] = jnp.zeros_like(ref)
        for i in range(bb * N_DIR):
            for h in range(H_B):
                bdv[i, h * CHUNK:(h + 1) * CHUNK, h * CAUG_W + DV_B:(h + 1) * CAUG_W] = jnp.ones((CHUNK, LANES), BF16)

    @pl.when(n == 0)
    def _():
        o_scr[...] = jnp.zeros_like(o_scr)
        if has_init:
            c_scr[...] = c0_ref[...]
            m_scr[...] = m0_ref[...]
        else:
            c_scr[...] = jnp.zeros_like(c_scr)
            m_scr[...] = jnp.zeros_like(m_scr)

    row = lax.broadcasted_iota(jnp.int32, (CHUNK, LANES), 0)
    lane = lax.broadcasted_iota(jnp.int32, (CHUNK, LANES), 1)
    col = lane & (CHUNK - 1)
    left = lane < CHUNK
    dirs = [(bi, d) for bi in range(bb) for d in range(N_DIR)]
    slot = {key: i for i, key in enumerate(dirs)}
    q16 = {}
    for key in dirs:
        bi, d = key
        i = slot[key]
        q16[key] = rest[d][bi, :, OFF_BQ:OFF_BQ + qw].astype(BF16)
        _store_blocks(bdk.at[i], rest[d][bi, :, OFF_BK:OFF_BK + qw].astype(BF16), CHUNK, DK_B)
        for h in range(H_B):
            bdv[i, h * CHUNK:(h + 1) * CHUNK, h * CAUG_W:h * CAUG_W + DV_B] = (
                rest[d][bi, :, OFF_BV + h * DV_B:OFF_BV + (h + 1) * DV_B].astype(BF16))
        _store_blocks(bdc.at[i], c_scr[bi, d].astype(BF16), DK_B, CAUG_W)
    qk = {key: lax.dot_general(q16[key], bdk[slot[key]], (((1,), (1,)), ((), ())), preferred_element_type=F32)
          for key in dirs}
    nc, a_t, iw, emt, ksc, dec_row = {}, {}, {}, {}, {}, {}
    for key in dirs:
        bi, d = key
        lo = LF_OFF + d * H_B
        mine = (lane >= lo) & (lane < lo + H_B)
        g = gt[d][bi]
        gc = jnp.where(mine, _scan_sum(g, d), 0.0)
        a = jnp.where(mine, pltpu.roll(g, LF_OFF - LI_OFF, 1), 0.0) - gc
        last = CHUNK - 1 if d == 0 else 0
        m_old = m_scr[bi, d][0:1, :]
        mx = jnp.maximum(m_old, _scan_max(a, d))
        mxl = mx[last:last + 1, :]
        nc[key] = -mx
        a_t[key] = jnp.concatenate([a, a], axis=0).T
        iw[key] = jnp.exp(m_old - mx)
        emt[key] = jnp.exp(-(gc + mx))
        ksc[key] = jnp.exp(a - mxl)
        dec_row[key] = jnp.exp(m_old - mxl)
        m_scr[bi, d] = jnp.broadcast_to(gc[last:last + 1, :] + mxl, (SUBLANES, LANES))

    ks_t = {}
    for key in dirs:
        bi, d = key
        lo = LF_OFF + d * H_B
        ks = rest[d][bi, :, OFF_BK:OFF_BK + qw] * jnp.concatenate(
            [_col_bcast(ksc[key], lo + h) for h in range(H_B)], axis=1)
        ks_t[key] = jnp.concatenate([ks[:, h * DK_B:(h + 1) * DK_B] for h in range(H_B)], axis=0).T.astype(BF16)
    lhs = {}
    for key in dirs:
        bi, d = key
        lo = LF_OFF + d * H_B
        incl = row >= col if d == 0 else row <= col
        log_w = []
        for p in range(H_B // PAIR):
            l0, l1 = lo + PAIR * p, lo + PAIR * p + 1
            ccol = jnp.where(left, _col_bcast(nc[key], l0), _col_bcast(nc[key], l1))
            crow = jnp.where(left[0:1], a_t[key][l0:l0 + 1, :], a_t[key][l1:l1 + 1, :])
            log_w.append(jnp.where(incl, ccol + crow, NEG))
        dw = jnp.exp(jnp.concatenate(log_w, axis=1)) * qk[key]
        iw_x = jnp.concatenate([_col_bcast(iw[key], lo + h) for h in range(H_B)], axis=1)
        lhs[key] = ((rest[d][bi, :, OFF_BQ:OFF_BQ + qw] * iw_x).astype(BF16), dw.astype(BF16))
    num = {}
    for key in dirs:
        i = slot[key]
        parts = []
        for p in range(H_B // PAIR):
            ql, dl, cl = slice(p * PAIR * DK_B, (p + 1) * PAIR * DK_B), slice(p * LANES, (p + 1) * LANES), \
                slice(p * PAIR * CAUG_W, (p + 1) * PAIR * CAUG_W)
            parts.append(jnp.dot(jnp.concatenate([lhs[key][0][:, ql], lhs[key][1][:, dl]], axis=1),
                                 jnp.concatenate([bdc[i, ql, cl], bdv[i, dl, cl]], axis=0),
                                 preferred_element_type=F32))
        num[key] = jnp.concatenate(parts, axis=1)
    upd = {key: jnp.dot(ks_t[key], bdv[slot[key]], preferred_element_type=F32) for key in dirs}
    for key in dirs:
        bi, d = key
        lo = LF_OFF + d * H_B
        dec_x = jnp.concatenate(
            [jnp.broadcast_to(dec_row[key][:, lo + h:lo + h + 1], (1, CAUG_W)) for h in range(H_B)], axis=1)
        c_scr[bi, d] = c_scr[bi, d] * dec_x + upd[key]
    heads = [(key, h) for key in dirs for h in range(H_B)]
    rows = {d: pl.ds(pl.multiple_of((n if d == 0 else n_chunks - 1 - n) * CHUNK, CHUNK), CHUNK) for d in range(N_DIR)}
    tots = []
    for key, h in heads:
        bi, d = key
        vs = slice(h * DV_B, (h + 1) * DV_B)
        den = jnp.maximum(jnp.abs(num[key][:, h * CAUG_W + DV_B:(h + 1) * CAUG_W]),
                          _col_bcast(emt[key], LF_OFF + d * H_B + h))
        hb = jnp.concatenate([num[key][:, h * CAUG_W:h * CAUG_W + LANES] / den,
                              num[key][:, h * CAUG_W + LANES:h * CAUG_W + DV_B] / den], axis=1)
        tot = hb + o_scr[bi, rows[d], vs]
        o_scr[bi, rows[d], vs] = tot
        tots.append(tot)
    inv = [lax.rsqrt(jnp.mean(tot * tot, axis=-1, keepdims=True) + EPS) for tot in tots]
    for (key, h), tot, r in zip(heads, tots, inv):
        bi, d = key
        vs = slice(h * DV_B, (h + 1) * DV_B)
        ogate = rest[d][bi, :, OFF_BO + h * DV_B:OFF_BO + (h + 1) * DV_B]
        yb_ref[bi, rows[d], vs] = (tot * r * norm_ref[...] * ogate).astype(BF16)

    if emit_state:
        @pl.when(n == n_chunks - 1)
        def _():
            for bi, d in dirs:
                for h in range(H_B):
                    cout_ref[bi, d, h] = c_scr[bi, d, :, h * CAUG_W:h * CAUG_W + DV_B]
                    nout_ref[bi, d, h] = c_scr[bi, d, :, h * CAUG_W + DV_B:(h + 1) * CAUG_W]
            mout_ref[...] = m_scr[...]


def _mlstm(rest, gates, norm_b, c0, m0, batch, n_chunks, emit_state, casts=()):
    t = n_chunks * CHUNK
    has_init = c0 is not None
    bb = SCAN_BB
    rest, gates = (a.reshape(batch, t, a.shape[-1]) for a in (rest, gates))
    mspec = pl.BlockSpec((bb, N_DIR, SUBLANES, LANES), lambda b, n: (b, 0, 0, 0))
    in_specs = [*_scan_specs(bb, n_chunks, REST_W), *_scan_specs(bb, n_chunks, LANES),
                pl.BlockSpec((1, DV_B), lambda b, n: (0, 0))]
    args = [rest, rest, gates, gates, norm_b]
    if has_init:
        in_specs += [pl.BlockSpec((bb, N_DIR, DK_B, MW), lambda b, n: (b, 0, 0, 0)), mspec]
        args += [c0, m0]
    out_specs = [pl.BlockSpec((bb, t, B_W), lambda b, n: (b, 0, 0))]
    out_shape = [jax.ShapeDtypeStruct((batch, t, B_W), BF16)]
    if emit_state:
        out_specs += [pl.BlockSpec((bb, N_DIR, H_B, DK_B, DV_B), lambda b, n: (b, 0, 0, 0, 0)),
                      pl.BlockSpec((bb, N_DIR, H_B, DK_B, LANES), lambda b, n: (b, 0, 0, 0, 0)), mspec]
        out_shape += [jax.ShapeDtypeStruct((batch, N_DIR, H_B, DK_B, DV_B), F32),
                      jax.ShapeDtypeStruct((batch, N_DIR, H_B, DK_B, LANES), F32),
                      jax.ShapeDtypeStruct((batch, N_DIR, SUBLANES, LANES), F32)]
    steps = (batch // bb) * n_chunks
    c_in, c_out, c_shape, c_args = _cast_specs([(w, l, steps) for w, l in casts], lambda b, n: b * n_chunks + n)
    in_specs, args, out_specs, out_shape = in_specs + c_in, args + c_args, out_specs + c_out, out_shape + c_shape
    out = pl.pallas_call(
        functools.partial(_mlstm_body, n_chunks=n_chunks, bb=bb, has_init=has_init, emit_state=emit_state,
                          n_cast=len(casts)),
        grid=(batch // bb, n_chunks),
        in_specs=in_specs,
        out_specs=out_specs,
        out_shape=out_shape,
        scratch_shapes=[pltpu.VMEM((bb, N_DIR, DK_B, MW), F32),
                        pltpu.VMEM((bb, N_DIR, SUBLANES, LANES), F32),
                        pltpu.VMEM((bb, t, B_W), F32),
                        pltpu.VMEM((bb * N_DIR, H_B * CHUNK, H_B * DK_B), BF16),
                        pltpu.VMEM((bb * N_DIR, H_B * DK_B, MW), BF16),
                        pltpu.VMEM((bb * N_DIR, H_B * CHUNK, MW), BF16)],
        compiler_params=_params(("arbitrary", "arbitrary")),
        name="mlstm_scan",
    )(*args)
    return [out[0].reshape(batch * t, B_W), *out[1:]]


OUT_TM = 512
OUT_RB = 256
FFN_TM = 512
FFN_FC = 1024


def _outproj_body(ya_ref, yb_ref, wa_ref, wb_ref, x_ref, mod_ref, post1_ref, pre2_ref, x1_ref, h2_ref):
    for rb in range(x_ref.shape[0] // OUT_RB):
        rs = slice(rb * OUT_RB, (rb + 1) * OUT_RB)
        mix = (jnp.dot(ya_ref[rs, :], wa_ref[...], preferred_element_type=F32)
               + jnp.dot(yb_ref[rs, :], wb_ref[...], preferred_element_type=F32))
        x1 = x_ref[rs, :] + mod_ref[0, 2:3, :] * (_rms(mix) * post1_ref[...])
        x1_ref[rs, :] = x1
        h2 = _rms(x1) * pre2_ref[...] * (1.0 + mod_ref[0, 4:5, :]) + mod_ref[0, 3:4, :]
        h2_ref[rs, :] = h2.astype(BF16)


def _outproj(ya, yb, w_out, x2d, mod3, mod_map, post1, pre2, tm):
    m = x2d.shape[0]
    row = lambda i: (i, 0)
    const = lambda i: (0, 0)
    return pl.pallas_call(
        _outproj_body,
        grid=(m // tm,),
        in_specs=[pl.BlockSpec((tm, A_W), row), pl.BlockSpec((tm, B_W), row),
                  pl.BlockSpec((A_W, D_MODEL), lambda i: (0, 0)),
                  pl.BlockSpec((B_W, D_MODEL), lambda i: (1, 0)),
                  pl.BlockSpec((tm, D_MODEL), row),
                  pl.BlockSpec((1, 6, D_MODEL), lambda i: (mod_map(i * tm), 0, 0)),
                  pl.BlockSpec((1, D_MODEL), const), pl.BlockSpec((1, D_MODEL), const)],
        out_specs=[pl.BlockSpec((tm, D_MODEL), row), pl.BlockSpec((tm, D_MODEL), row)],
        out_shape=[jax.ShapeDtypeStruct((m, D_MODEL), F32), jax.ShapeDtypeStruct((m, D_MODEL), BF16)],
        compiler_params=_params(("parallel",)),
        name="outproj",
    )(ya, yb, w_out, w_out, x2d, mod3, post1, pre2)


def _ffn_body(h2_ref, w1_ref, w2_ref, x1_ref, mod_ref, post2_ref, o_ref):
    kk = pl.program_id(1)

    @pl.when(kk == 0)
    def _():
        o_ref[...] = jnp.zeros_like(o_ref)

    def accumulate(rs):
        a = jnp.maximum(jnp.dot(h2_ref[rs, :], w1_ref[...], preferred_element_type=F32), 0.0)
        o_ref[rs, :] += jnp.dot((a * a).astype(BF16), w2_ref[...], preferred_element_type=F32)

    last = pl.num_programs(1) - 1
    tm = o_ref.shape[0]

    @pl.when(kk < last)
    def _():
        accumulate(slice(0, tm))

    @pl.when(kk == last)
    def _():
        for rs in (slice(0, tm // 2), slice(tm // 2, tm)):
            accumulate(rs)
            o_ref[rs, :] = x1_ref[rs, :] + mod_ref[0, 5:6, :] * (_rms(o_ref[rs, :]) * post2_ref[...])


def _ffn(h2, w1, w2, x1, mod3, mod_map, post2, tm, fc):
    m = h2.shape[0]
    return pl.pallas_call(
        _ffn_body,
        grid=(m // tm, FFN // fc),
        in_specs=[pl.BlockSpec((tm, D_MODEL), lambda i, k: (i, 0)),
                  pl.BlockSpec((D_MODEL, fc), lambda i, k: (0, k)),
                  pl.BlockSpec((fc, D_MODEL), lambda i, k: (k, 0)),
                  pl.BlockSpec((tm, D_MODEL), lambda i, k: (i, 0)),
                  pl.BlockSpec((1, 6, D_MODEL), lambda i, k: (mod_map(i * tm), 0, 0)),
                  pl.BlockSpec((1, D_MODEL), lambda i, k: (0, 0))],
        out_specs=pl.BlockSpec((tm, D_MODEL), lambda i, k: (i, 0)),
        out_shape=jax.ShapeDtypeStruct((m, D_MODEL), F32),
        compiler_params=_params(("parallel", "arbitrary")),
        name="ffn",
    )(h2, w1, w2, x1, mod3, post2)


def _block(x, mod3, mod_of_row, lp, init, seq_len, emit_state):
    bsz, t, _ = x.shape
    x2d = x.reshape(bsz * t, D_MODEL)
    n_chunks = t // CHUNK
    qkv, rest, gates = _proj(x2d, mod3, lambda i: mod_of_row(i * PROJ_TM), lp["pre1"], lp["w16"],
                             lp["w_mid"], lp["w_gate"], lp["conv_w"], lp["gate_p"], seq_len)
    s0, c0, m0 = init if init is not None else (None, None, None)
    first = "w1" not in lp
    l = lp["l"]
    d_out = _delta(qkv, rest, gates, lp["norm_a"], s0, bsz, n_chunks, emit_state,
                   [(lp["w_ffn1"], l), (lp["w_ffn2"], l), (lp["w_out32"], l)] if first else ())
    m_out = _mlstm(rest, gates, lp["norm_b"], c0, m0, bsz, n_chunks, emit_state)
    if first:
        lp["w1"], lp["w2"], lp["w_out"] = d_out[-3:]
    x1, h2 = _outproj(d_out[0], m_out[0], lp["w_out"], x2d, mod3, mod_of_row, lp["post1"], lp["pre2"], OUT_TM)
    y = _ffn(h2, lp["w1"], lp["w2"], x1, mod3, mod_of_row, lp["post2"], FFN_TM, FFN_FC)
    states = None
    if emit_state:
        m_fin = jnp.stack([m_out[3][:, d, 0, LF_OFF + d * H_B:LF_OFF + (d + 1) * H_B] for d in range(N_DIR)], axis=1)
        states = (d_out[1], m_out[1], m_out[2][..., 0], m_fin)
    return y.reshape(bsz, t, D_MODEL), states


def _layer_params(l, norm_mix_pre, norm_mix_post, norm_ffn_pre, norm_ffn_post, w_in, conv_w, a_log, dt_bias,
                  norm_a, mlstm_ibias, mlstm_fbias, norm_b, w_out, w_ffn1, w_ffn2):
    w = w_in[l]
    o_ag = QKV_W
    o_aa = o_ag + A_W
    o_ab = o_aa + N_DIR * H_A
    o_bq = o_ab + N_DIR * H_A
    o_bi = o_bq + 2 * H_B * DK_B + 2 * B_W
    o_bf = o_bi + N_DIR * H_B
    n_gate = 2 * N_DIR * H_A + 2 * N_DIR * H_B
    w16 = w.astype(BF16)
    w_gate = jnp.concatenate([w16[:, o_aa:o_bq], w16[:, o_bi:o_bf + N_DIR * H_B],
                              jnp.zeros((D_MODEL, LANES - n_gate), BF16)], axis=1)

    def lane_row(vals, off):
        return jnp.zeros((LANES,), F32).at[off:off + vals.size].set(vals.reshape(-1))

    gate_p = jnp.stack([lane_row(a_log[l], G_OFF), lane_row(dt_bias[l], G_OFF),
                        lane_row(mlstm_ibias[l], LI_OFF) + lane_row(mlstm_fbias[l], LF_OFF)]
                       + [jnp.zeros((LANES,), F32)] * (SUBLANES - 3))
    row = lambda v: v[l].reshape(1, -1)
    return dict(
        pre1=row(norm_mix_pre), post1=row(norm_mix_post), pre2=row(norm_ffn_pre), post2=row(norm_ffn_post),
        w16=w16, w_mid=w16[:, o_bq:o_bi], w_gate=w_gate, gate_p=gate_p,
        conv_w=jnp.concatenate([conv_w[l].T, jnp.zeros((SUBLANES - 3, QKV_W), F32)], axis=0),
        norm_a=row(norm_a), norm_b=row(norm_b),
        l=l, w_out32=w_out, w_ffn1=w_ffn1, w_ffn2=w_ffn2)


def kernel(x_prompt, x_sample, state_delta, state_mlstm_C, state_mlstm_n, state_mlstm_m, c, c_ctx, w_ada, b_ada, norm_mix_pre, norm_mix_post, norm_ffn_pre, norm_ffn_post, w_in, conv_w, a_log, dt_bias, norm_a, mlstm_ibias, mlstm_fbias, norm_b, w_out, w_ffn1, w_ffn2):
    depth = w_in.shape[0]
    n_lat = x_sample.shape[0]
    t_lat = x_sample.shape[1]
    cond = jnp.concatenate([c_ctx[None, :], c, jnp.zeros((SUBLANES - 1 - n_lat, D_MODEL), F32)], axis=0)
    y_prompt, y_sample = x_prompt, x_sample
    acc = ([], [], [], [])
    for l in range(depth):
        lp = _layer_params(l, norm_mix_pre, norm_mix_post, norm_ffn_pre, norm_ffn_post, w_in, conv_w, a_log,
                           dt_bias, norm_a, mlstm_ibias, mlstm_fbias, norm_b, w_out, w_ffn1, w_ffn2)
        mod = _ada(cond, w_ada, b_ada[l].reshape(1, -1), l)
        mod3 = mod[:1 + n_lat].reshape(1 + n_lat, 6, D_MODEL)
        y_prompt, st = _block(y_prompt, mod3, lambda r: 0, lp, None, x_prompt.shape[1], True)
        for a, s in zip(acc, st):
            a.append(s)
        n_rep = jnp.broadcast_to(state_mlstm_n[:, l][..., None], state_mlstm_n[:, l].shape + (LANES,))
        c_aug0 = jnp.concatenate([state_mlstm_C[:, l], n_rep], axis=-1)
        c_aug0 = c_aug0.transpose(0, 1, 3, 2, 4).reshape(n_lat, N_DIR, DK_B, MW)
        m0 = jnp.zeros((n_lat, N_DIR, LANES), F32)
        for d in range(N_DIR):
            m0 = m0.at[:, d, LF_OFF + d * H_B:LF_OFF + (d + 1) * H_B].set(state_mlstm_m[:, l, d])
        m0 = jnp.broadcast_to(m0[:, :, None, :], (n_lat, N_DIR, SUBLANES, LANES))
        y_sample, _ = _block(y_sample, mod3, lambda r: 1 + r // t_lat, lp, (state_delta[:, l], c_aug0, m0),
                             GRID_W, False)
    return (y_prompt, y_sample) + tuple(jnp.stack(a, axis=1) for a in acc)
```

```python
import functools

import jax
import jax.numpy as jnp
from jax import lax
from jax.experimental import pallas as pl
from jax.experimental.pallas import tpu as pltpu

F32 = jnp.float32
BF16 = jnp.bfloat16

D_MODEL = 2048
N_DIR = 2
A_W = D_MODEL // 2
B_W = D_MODEL - A_W
DK_A = 128
DV_A = 128
H_A = A_W // DV_A
DV_B = 256
DK_B = DV_B // 2
H_B = B_W // DV_B
GRID_W = 64
CHUNK = 64
FFN = 4 * D_MODEL
EPS = 1e-6
LANES = 128
SUBLANES = 8
NEG = -1e30

QKV_W = 3 * A_W
REST_W = A_W + 2 * H_B * DK_B + 2 * B_W
OFF_AG, OFF_BQ, OFF_BK, OFF_BV, OFF_BO = 0, A_W, A_W + H_B * DK_B, A_W + 2 * H_B * DK_B, A_W + 2 * H_B * DK_B + B_W
G_OFF, BETA_OFF = 0, N_DIR * H_A
LI_OFF = 2 * N_DIR * H_A
LF_OFF = LI_OFF + N_DIR * H_B

VMEM_LIMIT = 56 * 1024 * 1024


def _sigmoid(x):
    return 1.0 / (1.0 + jnp.exp(-x))


def _softplus(x):
    return jnp.maximum(x, 0.0) + jnp.log1p(jnp.exp(-jnp.abs(x)))


def _dot(a, b):
    return jnp.dot(a.astype(BF16), b.astype(BF16), preferred_element_type=F32)


def _rms(x):
    return x * lax.rsqrt(jnp.mean(x * x, axis=-1, keepdims=True) + EPS)


def _params(sem):
    return pltpu.CompilerParams(dimension_semantics=sem, vmem_limit_bytes=VMEM_LIMIT)


def _ada_body(c_ref, w_ref, b_ref, o_ref):
    c = c_ref[...]
    o_ref[...] = _dot(c * _sigmoid(c), w_ref[...]) + b_ref[...]


def _ada(c_all, w_ada, b, l):
    n = w_ada.shape[2]
    tn = 1024
    return pl.pallas_call(
        _ada_body,
        grid=(n // tn,),
        in_specs=[pl.BlockSpec(c_all.shape, lambda j: (0, 0)),
                  pl.BlockSpec((None, D_MODEL, tn), lambda j: (l, 0, j)),
                  pl.BlockSpec((1, tn), lambda j: (0, j))],
        out_specs=pl.BlockSpec((c_all.shape[0], tn), lambda j: (0, j)),
        out_shape=jax.ShapeDtypeStruct((c_all.shape[0], n), F32),
        compiler_params=_params(("arbitrary",)),
        name="ada",
    )(c_all, w_ada, b)


PROJ_TM = 1024
PROJ_RB = 512
PROJ_CB = 256
PROJ_TN = 512
N_QKV_T = QKV_W // PROJ_TN
N_REST_T = REST_W // PROJ_TN
N_HEAD_T = (QKV_W + A_W) // PROJ_TN
N_MID_T = N_QKV_T + N_REST_T - N_HEAD_T


def _proj_body(x_ref, mod_ref, g_ref, wh_ref, wb_ref, wg_ref, cw_ref, gp_ref, qkv_ref, rest_ref, gate_ref, h_scr, *,
               seq_len):
    j = pl.program_id(1)
    row_blocks = [slice(rb * PROJ_RB, (rb + 1) * PROJ_RB) for rb in range(PROJ_TM // PROJ_RB)]
    col_blocks = [slice(cb * PROJ_CB, (cb + 1) * PROJ_CB) for cb in range(PROJ_TN // PROJ_CB)]

    def run_rows(rs, w_ref, epilogue, out_ref):
        for cs in col_blocks:
            epilogue(jnp.dot(h_scr[rs, :], w_ref[:, cs], preferred_element_type=F32), out_ref, rs, cs)

    def run(w_ref, epilogue, out_ref):
        for rs in row_blocks:
            run_rows(rs, w_ref, epilogue, out_ref)

    def run_first():
        for rs in row_blocks:
            y = _rms(x_ref[rs, :]) * g_ref[...]
            h_scr[rs, :] = (y * (1.0 + mod_ref[0, 1:2, :]) + mod_ref[0, 0:1, :]).astype(BF16)
            run_rows(rs, wh_ref, ep_l2(DK_A ** -0.5), qkv_ref)
            ep_gate(jnp.dot(h_scr[rs, :], wg_ref[...], preferred_element_type=F32), gate_ref, rs)

    def conv_silu(acc, cs, g):
        a = acc[:, g * LANES:(g + 1) * LANES]
        ls = slice(cs.start + g * LANES, cs.start + (g + 1) * LANES)
        pos = lax.broadcasted_iota(jnp.int32, a.shape, 0) & (seq_len - 1)
        prev = jnp.where(pos == 0, 0.0, pltpu.roll(a, 1, 0))
        nxt = jnp.where(pos == seq_len - 1, 0.0, pltpu.roll(a, a.shape[0] - 1, 0))
        y = prev * cw_ref[0:1, ls] + a * cw_ref[1:2, ls] + nxt * cw_ref[2:3, ls]
        return ls, y * _sigmoid(y)

    def ep_l2(scale):
        def f(acc, out_ref, rs, cs):
            for g in range(PROJ_CB // LANES):
                ls, blk = conv_silu(acc, cs, g)
                inv = lax.rsqrt(jnp.sum(blk * blk, axis=-1, keepdims=True) + EPS)
                out_ref[rs, ls] = blk * (inv * scale)
        return f

    def ep_conv(acc, out_ref, rs, cs):
        for g in range(PROJ_CB // LANES):
            ls, blk = conv_silu(acc, cs, g)
            out_ref[rs, ls] = blk

    def ep_map(fn):
        def f(acc, out_ref, rs, cs):
            out_ref[rs, cs] = fn(acc)
        return f

    def ep_gate(z, out_ref, rs):
        lane = lax.broadcasted_iota(jnp.int32, z.shape, 1)
        g = -jnp.exp(gp_ref[0:1, :]) * _softplus(z + gp_ref[1:2, :])
        li = z + gp_ref[2:3, :]
        out_ref[rs, :] = jnp.where(lane < BETA_OFF, g,
                                   jnp.where(lane < LI_OFF, _sigmoid(z),
                                             jnp.where(lane < LF_OFF, li,
                                                       jnp.where(lane < LF_OFF + N_DIR * H_B, -_softplus(-li), 0.0))))

    tp = A_W // PROJ_TN
    c0 = (j - N_QKV_T) * PROJ_TN
    in_rest = (j >= N_QKV_T) & (j < N_QKV_T + N_REST_T)
    pl.when(j == 0)(run_first)
    variants = [
        ((j > 0) & (j < tp), wh_ref, ep_l2(DK_A ** -0.5), qkv_ref),
        ((j >= tp) & (j < 2 * tp), wh_ref, ep_l2(1.0), qkv_ref),
        ((j >= 2 * tp) & (j < N_QKV_T), wh_ref, ep_conv, qkv_ref),
        (in_rest & (c0 < OFF_BQ), wh_ref, ep_map(lambda a: a * _sigmoid(a)), rest_ref),
        (in_rest & (c0 >= OFF_BQ) & (c0 < OFF_BK), wb_ref, ep_map(lambda a: a * (DK_B ** -0.5)), rest_ref),
        (in_rest & (c0 >= OFF_BK) & (c0 < OFF_BO), wb_ref, ep_map(lambda a: a), rest_ref),
        (in_rest & (c0 >= OFF_BO), wb_ref, ep_map(_sigmoid), rest_ref),
    ]
    for cond, w_ref, epilogue, out_ref in variants:
        pl.when(cond)(functools.partial(run, w_ref, epilogue, out_ref))


def _proj(x2d, mod3, mod_map, pre_g, w16, w_mid, w_gate, conv_w, gate_p, seq_len):
    m = x2d.shape[0]
    tm, tn = PROJ_TM, PROJ_TN
    assert PROJ_RB % seq_len == 0 and m % tm == 0
    return pl.pallas_call(
        functools.partial(_proj_body, seq_len=seq_len),
        grid=(m // tm, N_HEAD_T + N_MID_T),
        in_specs=[pl.BlockSpec((tm, D_MODEL), lambda i, j: (i, 0)),
                  pl.BlockSpec((1, 6, D_MODEL), lambda i, j: (mod_map(i), 0, 0)),
                  pl.BlockSpec((1, D_MODEL), lambda i, j: (0, 0)),
                  pl.BlockSpec((D_MODEL, tn), lambda i, j: (0, jnp.minimum(j, N_HEAD_T - 1))),
                  pl.BlockSpec((D_MODEL, tn), lambda i, j: (0, jnp.clip(j - N_HEAD_T, 0, N_MID_T - 1))),
                  pl.BlockSpec((D_MODEL, LANES), lambda i, j: (0, 0)),
                  pl.BlockSpec((SUBLANES, tn), lambda i, j: (0, jnp.minimum(j, N_QKV_T - 1))),
                  pl.BlockSpec((SUBLANES, LANES), lambda i, j: (0, 0))],
        out_specs=[pl.BlockSpec((tm, tn), lambda i, j: (i, jnp.minimum(j, N_QKV_T - 1))),
                   pl.BlockSpec((tm, tn), lambda i, j: (i, jnp.clip(j - N_QKV_T, 0, N_REST_T - 1))),
                   pl.BlockSpec((tm, LANES), lambda i, j: (i, 0))],
        out_shape=[jax.ShapeDtypeStruct((m, QKV_W), F32), jax.ShapeDtypeStruct((m, REST_W), F32),
                   jax.ShapeDtypeStruct((m, LANES), F32)],
        scratch_shapes=[pltpu.VMEM((tm, D_MODEL), BF16)],
        compiler_params=_params(("parallel", "arbitrary")),
        name="proj",
    )(x2d, mod3, pre_g, w16, w_mid, w_gate, conv_w, gate_p)


GROUP = 4
N_GROUPS = H_A // GROUP
PAIR = LANES // CHUNK


def _split(x):
    hi = x.astype(BF16)
    return hi, (x - hi.astype(F32)).astype(BF16)


def _store_blocks(ref, x, blk_r, blk_c, col0=0):
    for h in range(GROUP):
        ref[h * blk_r:(h + 1) * blk_r, col0 + h * blk_c:col0 + (h + 1) * blk_c] = x[:, h * blk_c:(h + 1) * blk_c]


def _dot3(a_hi, a_lo, b_hi, b_lo):
    m = a_hi.shape[0]
    r = jnp.dot(jnp.concatenate([a_hi, a_lo], axis=0), b_hi, preferred_element_type=F32)
    return r[:m] + r[m:] + jnp.dot(a_hi, b_lo, preferred_element_type=F32)


HALF = CHUNK // 2


def _tri_inverse_wide(lmats, upper, bd_hi, bd_lo, by_hi, by_lo):
    width = lmats[0].shape[1]
    n_blk = width // HALF
    row = lax.broadcasted_iota(jnp.int32, (HALF, width), 0)
    lane = lax.broadcasted_iota(jnp.int32, (HALF, width), 1)
    lead = (lane & (CHUNK - 1)) < HALF
    eye_d = (row == (lane & (HALF - 1))).astype(F32)

    def bd_dot(i, a, b_hi, b_lo):
        for blk in range(n_blk):
            sl = slice(blk * HALF, (blk + 1) * HALF)
            bd_hi[i, sl, sl] = b_hi[:, sl]
            bd_lo[i, sl, sl] = b_lo[:, sl]
        return _dot3(*a, bd_hi[i], bd_lo[i])

    idx = range(len(lmats))
    diag = [jnp.where(lead, l[:HALF], l[HALF:]) for l in lmats]
    s = [eye_d - dg for dg in diag]
    p = []
    for i in idx:
        m_hi, m_lo = _split(-diag[i])
        p.append(bd_dot(i, (m_hi, m_lo), m_hi, m_lo))
    for _ in range(3):
        for i in idx:
            p_hi, p_lo = _split(p[i])
            s_hi, s_lo = _split(s[i])
            r = bd_dot(i, (jnp.concatenate([p_hi, s_hi], axis=0), jnp.concatenate([p_lo, s_lo], axis=0)), p_hi, p_lo)
            p[i] = r[:HALF]
            s[i] = s[i] + r[HALF:]
    for i in idx:
        p_hi, p_lo = _split(p[i])
        s[i] = s[i] + bd_dot(i, _split(s[i]), p_hi, p_lo)
    y = []
    for i in idx:
        c_blk = jnp.where(lead, 0.0, lmats[i][:HALF]) if upper[i] else jnp.where(lead, lmats[i][HALF:], 0.0)
        y.append(bd_dot(i, _split(c_blk), *_split(s[i])))
    out = []
    for i in idx:
        y_hi, y_lo = _split(y[i])
        for h in range(width // CHUNK):
            a_sl, b_sl = slice(h * CHUNK, h * CHUNK + HALF), slice(h * CHUNK + HALF, (h + 1) * CHUNK)
            rs, cs = (a_sl, b_sl) if upper[i] else (b_sl, a_sl)
            by_hi[i, rs, cs] = y_hi[:, cs]
            by_lo[i, rs, cs] = y_lo[:, cs]
        ai, bi = jnp.where(lead, s[i], 0.0), jnp.where(lead, 0.0, s[i])
        x = _dot3(*_split(ai if upper[i] else bi), by_hi[i], by_lo[i])
        out.append(jnp.concatenate([ai - x, bi] if upper[i] else [ai, bi - x], axis=0))
    return out


def _col_bcast(tile, c, width=LANES):
    return jnp.broadcast_to(tile[:, c:c + 1], (tile.shape[0], width))


def _delta_body(*refs, n_chunks, bb, has_init, emit_state, n_cast):
    qkv = refs[0:2]
    ag = refs[2:4]
    gt = refs[4:6]
    norm_ref = refs[6]
    pos = 7
    s0_ref = None
    if has_init:
        s0_ref = refs[pos]
        pos += 1
    cast_in = refs[pos:pos + n_cast]
    pos += n_cast
    ya_ref = refs[pos]
    pos += 1
    sout_ref = None
    if emit_state:
        sout_ref = refs[pos]
        pos += 1
    for w_ref, w16_ref in zip(cast_in, refs[pos:pos + n_cast]):
        w16_ref[...] = w_ref[...].astype(BF16)
    pos += n_cast
    s_scr, o_scr, bdn_hi, bdn_lo, bdy_hi, bdy_lo, bdk, bduw, bds, bdv = refs[pos:pos + 10]

    n = pl.program_id(1)
    gw = GROUP * DK_A

    @pl.when((n == 0) & (pl.program_id(0) == 0))
    def _():
        for ref in (bdn_hi, bdn_lo, bdy_hi, bdy_lo, bdk, bduw, bds, bdv):
            ref[...] = jnp.zeros_like(ref)

    @pl.when(n == 0)
    def _():
        o_scr[...] = jnp.zeros_like(o_scr)
        for bi in range(bb):
            for d in range(N_DIR):
                for h in range(H_A):
                    blk = s0_ref[bi, d, h] if has_init else jnp.zeros((DK_A, DV_A), F32)
                    s_scr[bi, d, h // GROUP, :, (h % GROUP) * DV_A:(h % GROUP + 1) * DV_A] = blk

    row = lax.broadcasted_iota(jnp.int32, (CHUNK, LANES), 0)
    lane = lax.broadcasted_iota(jnp.int32, (CHUNK, LANES), 1)
    col = lane & (CHUNK - 1)
    left = lane < CHUNK
    probs = [(bi, d, g) for bi in range(bb) for d in range(N_DIR) for g in range(N_GROUPS)]
    masks = {0: (row >= col, row > col), 1: (row <= col, row < col)}
    gates, gc, gc_t, eg, egl = {}, {}, {}, {}, {}
    for bi in range(bb):
        for d in range(N_DIR):
            key = (bi, d)
            gates[key] = gt[d][bi]
            gc[key] = _scan_sum(gates[key], d)
            gc_t[key] = jnp.concatenate([gc[key], gc[key]], axis=0).T
            last = CHUNK - 1 if d == 0 else 0
            eg[key] = jnp.exp(gc[key])
            egl[key] = jnp.exp(gc[key][last:last + 1, :] - gc[key])

    def cols_of(d, g):
        return [G_OFF + d * H_A + g * GROUP + hl for hl in range(GROUP)]

    q16, ks, beta_xs, decays, grams = [], [], [], [], []
    for gi, (bi, d, g) in enumerate(probs):
        key = (bi, d)
        cols = cols_of(d, g)
        incl, _ = masks[d]
        k = qkv[d][bi, :, A_W + g * gw:A_W + (g + 1) * gw]
        beta_x = jnp.concatenate([_col_bcast(gates[key], BETA_OFF - G_OFF + c) for c in cols], axis=1)
        decay = []
        for p in range(GROUP // PAIR):
            c0, c1 = cols[PAIR * p], cols[PAIR * p + 1]
            gcol = jnp.where(left, _col_bcast(gc[key], c0), _col_bcast(gc[key], c1))
            grow = jnp.where(left[0:1], gc_t[key][c0:c0 + 1, :], gc_t[key][c1:c1 + 1, :])
            decay.append(jnp.exp(jnp.where(incl, gcol - grow, NEG)))
        decays.append(jnp.concatenate(decay, axis=1))
        q16.append(qkv[d][bi, :, g * gw:(g + 1) * gw].astype(BF16))
        ks.append(k)
        beta_xs.append(beta_x)
        _store_blocks(bdk.at[gi], k.astype(BF16), CHUNK, DK_A)
    for gi in range(len(probs)):
        grams.append(lax.dot_general(jnp.concatenate([q16[gi], (ks[gi] * beta_xs[gi]).astype(BF16)], axis=0),
                                     bdk[gi], (((1,), (1,)), ((), ())), preferred_element_type=F32))
    attns, lmats = [], []
    for gi, (bi, d, g) in enumerate(probs):
        strict_w = jnp.concatenate([masks[d][1]] * (GROUP // PAIR), axis=1)
        attns.append((grams[gi][:CHUNK] * decays[gi]).astype(BF16))
        lmats.append(jnp.where(strict_w, grams[gi][CHUNK:] * decays[gi], 0.0))
    ainvs = _tri_inverse_wide(lmats, [d == 1 for _, d, _ in probs], bdn_hi, bdn_lo, bdy_hi, bdy_lo)
    eg_xs, kd_ts = [], []
    for gi, (bi, d, g) in enumerate(probs):
        key = (bi, d)
        cols = cols_of(d, g)
        v = qkv[d][bi, :, 2 * A_W + g * gw:2 * A_W + (g + 1) * gw]
        eg_x = jnp.concatenate([_col_bcast(eg[key], c) for c in cols], axis=1)
        egl_x = jnp.concatenate([_col_bcast(egl[key], c) for c in cols], axis=1)
        eg_xs.append(eg_x)
        _store_blocks(bduw.at[gi], (v * beta_xs[gi]).astype(BF16), CHUNK, DV_A)
        _store_blocks(bduw.at[gi], (ks[gi] * (beta_xs[gi] * eg_x)).astype(BF16), CHUNK, DK_A, col0=gw)
        kd = ks[gi] * egl_x
        kd_ts.append(jnp.concatenate([kd[:, hl * DK_A:(hl + 1) * DK_A] for hl in range(GROUP)], axis=0)
                     .T.astype(BF16))
        _store_blocks(bds.at[gi], s_scr[bi, d, g].astype(BF16), DK_A, DV_A)
    uws = []
    for gi in range(len(probs)):
        t_hi, t_lo = _split(ainvs[gi])
        r = jnp.dot(jnp.concatenate([t_hi, t_lo], axis=0), bduw[gi], preferred_element_type=F32)
        uws.append(r[:CHUNK] + r[CHUNK:])
    ws_qs = []
    hw = gw // 2
    for gi in range(len(probs)):
        wq = jnp.concatenate([uws[gi][:, gw:].astype(BF16), q16[gi]], axis=0)
        ws_qs.append(jnp.concatenate(
            [jnp.dot(wq[:, p * hw:(p + 1) * hw], bds[gi, p * hw:(p + 1) * hw, p * hw:(p + 1) * hw],
                     preferred_element_type=F32) for p in range(2)], axis=1))
    for gi in range(len(probs)):
        _store_blocks(bdv.at[gi], (uws[gi][:, :gw] - ws_qs[gi][:CHUNK]).astype(BF16), CHUNK, DV_A)
    rs = []
    for gi in range(len(probs)):
        rs.append(jnp.dot(jnp.concatenate([attns[gi], kd_ts[gi]], axis=0), bdv[gi], preferred_element_type=F32))
    tots = []
    for gi, (bi, d, g) in enumerate(probs):
        cols = cols_of(d, g)
        last = CHUNK - 1 if d == 0 else 0
        o = ws_qs[gi][CHUNK:] * eg_xs[gi] + rs[gi][:CHUNK]
        eg_last = jnp.concatenate(
            [jnp.broadcast_to(eg[bi, d][last:last + 1, c:c + 1], (1, DV_A)) for c in cols], axis=1)
        s_scr[bi, d, g] = s_scr[bi, d, g] * eg_last + rs[gi][CHUNK:]
        cidx = n if d == 0 else n_chunks - 1 - n
        rows = pl.ds(pl.multiple_of(cidx * CHUNK, CHUNK), CHUNK)
        tot = o + o_scr[bi, rows, g * gw:(g + 1) * gw]
        o_scr[bi, rows, g * gw:(g + 1) * gw] = tot
        tots.append((rows, tot))
    blks = [(bi, d, rows, g * GROUP + hl, tot[:, hl * DV_A:(hl + 1) * DV_A])
            for (bi, d, g), (rows, tot) in zip(probs, tots) for hl in range(GROUP)]
    inv = [lax.rsqrt(jnp.mean(blk * blk, axis=-1, keepdims=True) + EPS) for *_, blk in blks]
    for (bi, d, rows, h, blk), r in zip(blks, inv):
        hs = slice(h * DV_A, (h + 1) * DV_A)
        ya_ref[bi, rows, hs] = (blk * r * norm_ref[...] * ag[d][bi, :, hs]).astype(BF16)

    if emit_state:
        @pl.when(n == n_chunks - 1)
        def _():
            for bi in range(bb):
                for d in range(N_DIR):
                    for h in range(H_A):
                        sout_ref[bi, d, h] = s_scr[bi, d, h // GROUP, :, (h % GROUP) * DV_A:(h % GROUP + 1) * DV_A]


SCAN_BB = 2


def _scan_specs(bb, n_chunks, width):
    return (pl.BlockSpec((bb, CHUNK, width), lambda b, n: (b, n, 0)),
            pl.BlockSpec((bb, CHUNK, width), lambda b, n: (b, n_chunks - 1 - n, 0)))


def _cast_specs(casts, steps_of):
    in_specs, out_specs, out_shape, args = [], [], [], []
    for w, l, steps in casts:
        rows = w.shape[1] // steps
        assert rows * steps == w.shape[1] and rows % (2 * SUBLANES) == 0
        in_specs.append(pl.BlockSpec((None, rows, w.shape[2]), lambda b, n, l=l: (l, steps_of(b, n), 0)))
        out_specs.append(pl.BlockSpec((rows, w.shape[2]), lambda b, n: (steps_of(b, n), 0)))
        out_shape.append(jax.ShapeDtypeStruct(w.shape[1:], BF16))
        args.append(w)
    return in_specs, out_specs, out_shape, args


def _delta(qkv, rest, gates, norm_a, s0, batch, n_chunks, emit_state, casts=()):
    t = n_chunks * CHUNK
    has_init = s0 is not None
    bb = SCAN_BB
    ng = bb * N_DIR * N_GROUPS
    qkv, rest, gates = (a.reshape(batch, t, a.shape[-1]) for a in (qkv, rest, gates))
    in_specs = [*_scan_specs(bb, n_chunks, QKV_W), *_scan_specs(bb, n_chunks, A_W), *_scan_specs(bb, n_chunks, LANES),
                pl.BlockSpec((1, DV_A), lambda b, n: (0, 0))]
    args = [qkv, qkv, rest, rest, gates, gates, norm_a]
    sspec = pl.BlockSpec((bb, N_DIR, H_A, DK_A, DV_A), lambda b, n: (b, 0, 0, 0, 0))
    if has_init:
        in_specs.append(sspec)
        args.append(s0)
    out_specs = [pl.BlockSpec((bb, t, A_W), lambda b, n: (b, 0, 0))]
    out_shape = [jax.ShapeDtypeStruct((batch, t, A_W), BF16)]
    if emit_state:
        out_specs.append(sspec)
        out_shape.append(jax.ShapeDtypeStruct((batch, N_DIR, H_A, DK_A, DV_A), F32))
    steps = (batch // bb) * n_chunks
    c_in, c_out, c_shape, c_args = _cast_specs([(w, l, steps) for w, l in casts], lambda b, n: b * n_chunks + n)
    in_specs, args, out_specs, out_shape = in_specs + c_in, args + c_args, out_specs + c_out, out_shape + c_shape
    out = pl.pallas_call(
        functools.partial(_delta_body, n_chunks=n_chunks, bb=bb, has_init=has_init, emit_state=emit_state,
                          n_cast=len(casts)),
        grid=(batch // bb, n_chunks),
        in_specs=in_specs,
        out_specs=out_specs,
        out_shape=out_shape,
        scratch_shapes=[pltpu.VMEM((bb, N_DIR, N_GROUPS, DK_A, GROUP * DV_A), F32), pltpu.VMEM((bb, t, A_W), F32),
                        pltpu.VMEM((ng, GROUP * CHUNK, GROUP * CHUNK), BF16),
                        pltpu.VMEM((ng, GROUP * CHUNK, GROUP * CHUNK), BF16),
                        pltpu.VMEM((ng, GROUP * CHUNK, GROUP * CHUNK), BF16),
                        pltpu.VMEM((ng, GROUP * CHUNK, GROUP * CHUNK), BF16),
                        pltpu.VMEM((ng, GROUP * CHUNK, GROUP * DK_A), BF16),
                        pltpu.VMEM((ng, GROUP * CHUNK, 2 * GROUP * DK_A), BF16),
                        pltpu.VMEM((ng, GROUP * DK_A, GROUP * DV_A), BF16),
                        pltpu.VMEM((ng, GROUP * CHUNK, GROUP * DV_A), BF16)],
        compiler_params=_params(("arbitrary", "arbitrary")),
        name="delta_scan",
    )(*args)
    return [out[0].reshape(batch * t, A_W), *out[1:]]


CAUG_W = DV_B + LANES
MW = H_B * CAUG_W


def _scan_rows(x, d, combine, fill):
    row = lax.broadcasted_iota(jnp.int32, x.shape, 0)
    s = 1
    while s < CHUNK:
        if d == 0:
            shifted = jnp.where(row >= s, pltpu.roll(x, s, 0), fill)
        else:
            shifted = jnp.where(row < CHUNK - s, pltpu.roll(x, CHUNK - s, 0), fill)
        x = combine(x, shifted)
        s *= 2
    return x


def _scan_max(x, d):
    return _scan_rows(x, d, jnp.maximum, NEG)


def _scan_sum(x, d):
    return _scan_rows(x, d, jnp.add, 0.0)


def _mlstm_body(*refs, n_chunks, bb, has_init, emit_state, n_cast):
    rest = refs[0:2]
    gt = refs[2:4]
    norm_ref = refs[4]
    pos = 5
    c0_ref = m0_ref = None
    if has_init:
        c0_ref, m0_ref = refs[pos], refs[pos + 1]
        pos += 2
    cast_in = refs[pos:pos + n_cast]
    pos += n_cast
    yb_ref = refs[pos]
    pos += 1
    cout_ref = nout_ref = mout_ref = None
    if emit_state:
        cout_ref, nout_ref, mout_ref = refs[pos:pos + 3]
        pos += 3
    for w_ref, w16_ref in zip(cast_in, refs[pos:pos + n_cast]):
        w16_ref[...] = w_ref[...].astype(BF16)
    pos += n_cast
    c_scr, m_scr, o_scr, bdk, bdc, bdv = refs[pos:pos + 6]

    n = pl.program_id(1)
    qw = H_B * DK_B

    @pl.when((n == 0) & (pl.program_id(0) == 0))
    def _():
        for ref in (bdk, bdc, bdv):
            ref[...] = jnp.zeros_like(ref)
        for i in range(bb * N_DIR):
            for h in range(H_B):
                bdv[i, h * CHUNK:(h + 1) * CHUNK, h * CAUG_W + DV_B:(h + 1) * CAUG_W] = jnp.ones((CHUNK, LANES), BF16)

    @pl.when(n == 0)
    def _():
        o_scr[...] = jnp.zeros_like(o_scr)
        if has_init:
            c_scr[...] = c0_ref[...]
            m_scr[...] = m0_ref[...]
        else:
            c_scr[...] = jnp.zeros_like(c_scr)
            m_scr[...] = jnp.zeros_like(m_scr)

    row = lax.broadcasted_iota(jnp.int32, (CHUNK, LANES), 0)
    lane = lax.broadcasted_iota(jnp.int32, (CHUNK, LANES), 1)
    col = lane & (CHUNK - 1)
    left = lane < CHUNK
    dirs = [(bi, d) for bi in range(bb) for d in range(N_DIR)]
    slot = {key: i for i, key in enumerate(dirs)}
    q16 = {}
    for key in dirs:
        bi, d = key
        i = slot[key]
        q16[key] = rest[d][bi, :, OFF_BQ:OFF_BQ + qw].astype(BF16)
        _store_blocks(bdk.at[i], rest[d][bi, :, OFF_BK:OFF_BK + qw].astype(BF16), CHUNK, DK_B)
        for h in range(H_B):
            bdv[i, h * CHUNK:(h + 1) * CHUNK, h * CAUG_W:h * CAUG_W + DV_B] = (
                rest[d][bi, :, OFF_BV + h * DV_B:OFF_BV + (h + 1) * DV_B].astype(BF16))
        _store_blocks(bdc.at[i], c_scr[bi, d].astype(BF16), DK_B, CAUG_W)
    qk = {key: lax.dot_general(q16[key], bdk[slot[key]], (((1,), (1,)), ((), ())), preferred_element_type=F32)
          for key in dirs}
    nc, a_t, iw, emt, ksc, dec_row = {}, {}, {}, {}, {}, {}
    for key in dirs:
        bi, d = key
        lo = LF_OFF + d * H_B
        mine = (lane >= lo) & (lane < lo + H_B)
        g = gt[d][bi]
        gc = jnp.where(mine, _scan_sum(g, d), 0.0)
        a = jnp.where(mine, pltpu.roll(g, LF_OFF - LI_OFF, 1), 0.0) - gc
        last = CHUNK - 1 if d == 0 else 0
        m_old = m_scr[bi, d][0:1, :]
        mx = jnp.maximum(m_old, _scan_max(a, d))
        mxl = mx[last:last + 1, :]
        nc[key] = -mx
        a_t[key] = jnp.concatenate([a, a], axis=0).T
        iw[key] = jnp.exp(m_old - mx)
        emt[key] = jnp.exp(-(gc + mx))
        ksc[key] = jnp.exp(a - mxl)
        dec_row[key] = jnp.exp(m_old - mxl)
        m_scr[bi, d] = jnp.broadcast_to(gc[last:last + 1, :] + mxl, (SUBLANES, LANES))

    ks_t = {}
    for key in dirs:
        bi, d = key
        lo = LF_OFF + d * H_B
        ks = rest[d][bi, :, OFF_BK:OFF_BK + qw] * jnp.concatenate(
            [_col_bcast(ksc[key], lo + h) for h in range(H_B)], axis=1)
        ks_t[key] = jnp.concatenate([ks[:, h * DK_B:(h + 1) * DK_B] for h in range(H_B)], axis=0).T.astype(BF16)
    lhs = {}
    for key in dirs:
        bi, d = key
        lo = LF_OFF + d * H_B
        incl = row >= col if d == 0 else row <= col
        log_w = []
        for p in range(H_B // PAIR):
            l0, l1 = lo + PAIR * p, lo + PAIR * p + 1
            ccol = jnp.where(left, _col_bcast(nc[key], l0), _col_bcast(nc[key], l1))
            crow = jnp.where(left[0:1], a_t[key][l0:l0 + 1, :], a_t[key][l1:l1 + 1, :])
            log_w.append(jnp.where(incl, ccol + crow, NEG))
        dw = jnp.exp(jnp.concatenate(log_w, axis=1)) * qk[key]
        iw_x = jnp.concatenate([_col_bcast(iw[key], lo + h) for h in range(H_B)], axis=1)
        lhs[key] = ((rest[d][bi, :, OFF_BQ:OFF_BQ + qw] * iw_x).astype(BF16), dw.astype(BF16))
    num = {}
    for key in dirs:
        i = slot[key]
        parts = []
        for p in range(H_B // PAIR):
            ql, dl, cl = slice(p * PAIR * DK_B, (p + 1) * PAIR * DK_B), slice(p * LANES, (p + 1) * LANES), \
                slice(p * PAIR * CAUG_W, (p + 1) * PAIR * CAUG_W)
            parts.append(jnp.dot(jnp.concatenate([lhs[key][0][:, ql], lhs[key][1][:, dl]], axis=1),
                                 jnp.concatenate([bdc[i, ql, cl], bdv[i, dl, cl]], axis=0),
                                 preferred_element_type=F32))
        num[key] = jnp.concatenate(parts, axis=1)
    upd = {key: jnp.dot(ks_t[key], bdv[slot[key]], preferred_element_type=F32) for key in dirs}
    for key in dirs:
        bi, d = key
        lo = LF_OFF + d * H_B
        dec_x = jnp.concatenate(
            [jnp.broadcast_to(dec_row[key][:, lo + h:lo + h + 1], (1, CAUG_W)) for h in range(H_B)], axis=1)
        c_scr[bi, d] = c_scr[bi, d] * dec_x + upd[key]
    heads = [(key, h) for key in dirs for h in range(H_B)]
    rows = {d: pl.ds(pl.multiple_of((n if d == 0 else n_chunks - 1 - n) * CHUNK, CHUNK), CHUNK) for d in range(N_DIR)}
    tots = []
    for key, h in heads:
        bi, d = key
        vs = slice(h * DV_B, (h + 1) * DV_B)
        den = jnp.maximum(jnp.abs(num[key][:, h * CAUG_W + DV_B:(h + 1) * CAUG_W]),
                          _col_bcast(emt[key], LF_OFF + d * H_B + h))
        hb = jnp.concatenate([num[key][:, h * CAUG_W:h * CAUG_W + LANES] / den,
                              num[key][:, h * CAUG_W + LANES:h * CAUG_W + DV_B] / den], axis=1)
        tot = hb + o_scr[bi, rows[d], vs]
        o_scr[bi, rows[d], vs] = tot
        tots.append(tot)
    inv = [lax.rsqrt(jnp.mean(tot * tot, axis=-1, keepdims=True) + EPS) for tot in tots]
    for (key, h), tot, r in zip(heads, tots, inv):
        bi, d = key
        vs = slice(h * DV_B, (h + 1) * DV_B)
        ogate = rest[d][bi, :, OFF_BO + h * DV_B:OFF_BO + (h + 1) * DV_B]
        yb_ref[bi, rows[d], vs] = (tot * r * norm_ref[...] * ogate).astype(BF16)

    if emit_state:
        @pl.when(n == n_chunks - 1)
        def _():
            for bi, d in dirs:
                for h in range(H_B):
                    cout_ref[bi, d, h] = c_scr[bi, d, :, h * CAUG_W:h * CAUG_W + DV_B]
                    nout_ref[bi, d, h] = c_scr[bi, d, :, h * CAUG_W + DV_B:(h + 1) * CAUG_W]
            mout_ref[...] = m_scr[...]


def _mlstm(rest, gates, norm_b, c0, m0, batch, n_chunks, emit_state, casts=()):
    t = n_chunks * CHUNK
    has_init = c0 is not None
    bb = SCAN_BB
    rest, gates = (a.reshape(batch, t, a.shape[-1]) for a in (rest, gates))
    mspec = pl.BlockSpec((bb, N_DIR, SUBLANES, LANES), lambda b, n: (b, 0, 0, 0))
    in_specs = [*_scan_specs(bb, n_chunks, REST_W), *_scan_specs(bb, n_chunks, LANES),
                pl.BlockSpec((1, DV_B), lambda b, n: (0, 0))]
    args = [rest, rest, gates, gates, norm_b]
    if has_init:
        in_specs += [pl.BlockSpec((bb, N_DIR, DK_B, MW), lambda b, n: (b, 0, 0, 0)), mspec]
        args += [c0, m0]
    out_specs = [pl.BlockSpec((bb, t, B_W), lambda b, n: (b, 0, 0))]
    out_shape = [jax.ShapeDtypeStruct((batch, t, B_W), BF16)]
    if emit_state:
        out_specs += [pl.BlockSpec((bb, N_DIR, H_B, DK_B, DV_B), lambda b, n: (b, 0, 0, 0, 0)),
                      pl.BlockSpec((bb, N_DIR, H_B, DK_B, LANES), lambda b, n: (b, 0, 0, 0, 0)), mspec]
        out_shape += [jax.ShapeDtypeStruct((batch, N_DIR, H_B, DK_B, DV_B), F32),
                      jax.ShapeDtypeStruct((batch, N_DIR, H_B, DK_B, LANES), F32),
                      jax.ShapeDtypeStruct((batch, N_DIR, SUBLANES, LANES), F32)]
    steps = (batch // bb) * n_chunks
    c_in, c_out, c_shape, c_args = _cast_specs([(w, l, steps) for w, l in casts], lambda b, n: b * n_chunks + n)
    in_specs, args, out_specs, out_shape = in_specs + c_in, args + c_args, out_specs + c_out, out_shape + c_shape
    out = pl.pallas_call(
        functools.partial(_mlstm_body, n_chunks=n_chunks, bb=bb, has_init=has_init, emit_state=emit_state,
                          n_cast=len(casts)),
        grid=(batch // bb, n_chunks),
        in_specs=in_specs,
        out_specs=out_specs,
        out_shape=out_shape,
        scratch_shapes=[pltpu.VMEM((bb, N_DIR, DK_B, MW), F32),
                        pltpu.VMEM((bb, N_DIR, SUBLANES, LANES), F32),
                        pltpu.VMEM((bb, t, B_W), F32),
                        pltpu.VMEM((bb * N_DIR, H_B * CHUNK, H_B * DK_B), BF16),
                        pltpu.VMEM((bb * N_DIR, H_B * DK_B, MW), BF16),
                        pltpu.VMEM((bb * N_DIR, H_B * CHUNK, MW), BF16)],
        compiler_params=_params(("arbitrary", "arbitrary")),
        name="mlstm_scan",
    )(*args)
    return [out[0].reshape(batch * t, B_W), *out[1:]]


OUT_TM = 512
OUT_RB = 256
FFN_TM = 512
FFN_FC = 1024


def _outproj_body(ya_ref, yb_ref, w_ref, x_ref, mod_ref, post1_ref, pre2_ref, x1_ref, h2_ref):
    for rb in range(x_ref.shape[0] // OUT_RB):
        rs = slice(rb * OUT_RB, (rb + 1) * OUT_RB)
        mix = jnp.dot(jnp.concatenate([ya_ref[rs, :], yb_ref[rs, :]], axis=1), w_ref[...],
                      preferred_element_type=F32)
        x1 = x_ref[rs, :] + mod_ref[0, 2:3, :] * (_rms(mix) * post1_ref[...])
        x1_ref[rs, :] = x1
        h2 = _rms(x1) * pre2_ref[...] * (1.0 + mod_ref[0, 4:5, :]) + mod_ref[0, 3:4, :]
        h2_ref[rs, :] = h2.astype(BF16)


def _outproj(ya, yb, w_out, x2d, mod3, mod_map, post1, pre2, tm):
    m = x2d.shape[0]
    row = lambda i: (i, 0)
    const = lambda i: (0, 0)
    return pl.pallas_call(
        _outproj_body,
        grid=(m // tm,),
        in_specs=[pl.BlockSpec((tm, A_W), row), pl.BlockSpec((tm, B_W), row),
                  pl.BlockSpec((D_MODEL, D_MODEL), lambda i: (0, 0)),
                  pl.BlockSpec((tm, D_MODEL), row),
                  pl.BlockSpec((1, 6, D_MODEL), lambda i: (mod_map(i * tm), 0, 0)),
                  pl.BlockSpec((1, D_MODEL), const), pl.BlockSpec((1, D_MODEL), const)],
        out_specs=[pl.BlockSpec((tm, D_MODEL), row), pl.BlockSpec((tm, D_MODEL), row)],
        out_shape=[jax.ShapeDtypeStruct((m, D_MODEL), F32), jax.ShapeDtypeStruct((m, D_MODEL), BF16)],
        compiler_params=_params(("parallel",)),
        name="outproj",
    )(ya, yb, w_out, x2d, mod3, post1, pre2)


def _ffn_body(h2_ref, w1_ref, w2_ref, x1_ref, mod_ref, post2_ref, o_ref):
    kk = pl.program_id(1)

    @pl.when(kk == 0)
    def _():
        o_ref[...] = jnp.zeros_like(o_ref)

    def accumulate(rs):
        a = jnp.maximum(jnp.dot(h2_ref[rs, :], w1_ref[...], preferred_element_type=F32), 0.0)
        o_ref[rs, :] += jnp.dot((a * a).astype(BF16), w2_ref[...], preferred_element_type=F32)

    last = pl.num_programs(1) - 1
    tm = o_ref.shape[0]

    @pl.when(kk < last)
    def _():
        accumulate(slice(0, tm))

    @pl.when(kk == last)
    def _():
        for rs in (slice(0, tm // 2), slice(tm // 2, tm)):
            accumulate(rs)
            o_ref[rs, :] = x1_ref[rs, :] + mod_ref[0, 5:6, :] * (_rms(o_ref[rs, :]) * post2_ref[...])


def _ffn(h2, w1, w2, x1, mod3, mod_map, post2, tm, fc):
    m = h2.shape[0]
    return pl.pallas_call(
        _ffn_body,
        grid=(m // tm, FFN // fc),
        in_specs=[pl.BlockSpec((tm, D_MODEL), lambda i, k: (i, 0)),
                  pl.BlockSpec((D_MODEL, fc), lambda i, k: (0, k)),
                  pl.BlockSpec((fc, D_MODEL), lambda i, k: (k, 0)),
                  pl.BlockSpec((tm, D_MODEL), lambda i, k: (i, 0)),
                  pl.BlockSpec((1, 6, D_MODEL), lambda i, k: (mod_map(i * tm), 0, 0)),
                  pl.BlockSpec((1, D_MODEL), lambda i, k: (0, 0))],
        out_specs=pl.BlockSpec((tm, D_MODEL), lambda i, k: (i, 0)),
        out_shape=jax.ShapeDtypeStruct((m, D_MODEL), F32),
        compiler_params=_params(("parallel", "arbitrary")),
        name="ffn",
    )(h2, w1, w2, x1, mod3, post2)


def _block(x, mod3, mod_of_row, lp, init, seq_len, emit_state):
    bsz, t, _ = x.shape
    x2d = x.reshape(bsz * t, D_MODEL)
    n_chunks = t // CHUNK
    qkv, rest, gates = _proj(x2d, mod3, lambda i: mod_of_row(i * PROJ_TM), lp["pre1"], lp["w16"],
                             lp["w_mid"], lp["w_gate"], lp["conv_w"], lp["gate_p"], seq_len)
    s0, c0, m0 = init if init is not None else (None, None, None)
    first = "w1" not in lp
    l = lp["l"]
    d_out = _delta(qkv, rest, gates, lp["norm_a"], s0, bsz, n_chunks, emit_state,
                   [(lp["w_ffn1"], l), (lp["w_ffn2"], l), (lp["w_out32"], l)] if first else ())
    m_out = _mlstm(rest, gates, lp["norm_b"], c0, m0, bsz, n_chunks, emit_state)
    if first:
        lp["w1"], lp["w2"], lp["w_out"] = d_out[-3:]
    x1, h2 = _outproj(d_out[0], m_out[0], lp["w_out"], x2d, mod3, mod_of_row, lp["post1"], lp["pre2"], OUT_TM)
    y = _ffn(h2, lp["w1"], lp["w2"], x1, mod3, mod_of_row, lp["post2"], FFN_TM, FFN_FC)
    states = None
    if emit_state:
        m_fin = jnp.stack([m_out[3][:, d, 0, LF_OFF + d * H_B:LF_OFF + (d + 1) * H_B] for d in range(N_DIR)], axis=1)
        states = (d_out[1], m_out[1], m_out[2][..., 0], m_fin)
    return y.reshape(bsz, t, D_MODEL), states


def _layer_params(l, norm_mix_pre, norm_mix_post, norm_ffn_pre, norm_ffn_post, w_in, conv_w, a_log, dt_bias,
                  norm_a, mlstm_ibias, mlstm_fbias, norm_b, w_out, w_ffn1, w_ffn2):
    w = w_in[l]
    o_ag = QKV_W
    o_aa = o_ag + A_W
    o_ab = o_aa + N_DIR * H_A
    o_bq = o_ab + N_DIR * H_A
    o_bi = o_bq + 2 * H_B * DK_B + 2 * B_W
    o_bf = o_bi + N_DIR * H_B
    n_gate = 2 * N_DIR * H_A + 2 * N_DIR * H_B
    w16 = w.astype(BF16)
    w_gate = jnp.concatenate([w16[:, o_aa:o_bq], w16[:, o_bi:o_bf + N_DIR * H_B],
                              jnp.zeros((D_MODEL, LANES - n_gate), BF16)], axis=1)

    def lane_row(vals, off):
        return jnp.zeros((LANES,), F32).at[off:off + vals.size].set(vals.reshape(-1))

    gate_p = jnp.stack([lane_row(a_log[l], G_OFF), lane_row(dt_bias[l], G_OFF),
                        lane_row(mlstm_ibias[l], LI_OFF) + lane_row(mlstm_fbias[l], LF_OFF)]
                       + [jnp.zeros((LANES,), F32)] * (SUBLANES - 3))
    row = lambda v: v[l].reshape(1, -1)
    return dict(
        pre1=row(norm_mix_pre), post1=row(norm_mix_post), pre2=row(norm_ffn_pre), post2=row(norm_ffn_post),
        w16=w16, w_mid=w16[:, o_bq:o_bi], w_gate=w_gate, gate_p=gate_p,
        conv_w=jnp.concatenate([conv_w[l].T, jnp.zeros((SUBLANES - 3, QKV_W), F32)], axis=0),
        norm_a=row(norm_a), norm_b=row(norm_b),
        l=l, w_out32=w_out, w_ffn1=w_ffn1, w_ffn2=w_ffn2)


def kernel(x_prompt, x_sample, state_delta, state_mlstm_C, state_mlstm_n, state_mlstm_m, c, c_ctx, w_ada, b_ada, norm_mix_pre, norm_mix_post, norm_ffn_pre, norm_ffn_post, w_in, conv_w, a_log, dt_bias, norm_a, mlstm_ibias, mlstm_fbias, norm_b, w_out, w_ffn1, w_ffn2):
    depth = w_in.shape[0]
    n_lat = x_sample.shape[0]
    t_lat = x_sample.shape[1]
    cond = jnp.concatenate([c_ctx[None, :], c, jnp.zeros((SUBLANES - 1 - n_lat, D_MODEL), F32)], axis=0)
    y_prompt, y_sample = x_prompt, x_sample
    acc = ([], [], [], [])
    for l in range(depth):
        lp = _layer_params(l, norm_mix_pre, norm_mix_post, norm_ffn_pre, norm_ffn_post, w_in, conv_w, a_log,
                           dt_bias, norm_a, mlstm_ibias, mlstm_fbias, norm_b, w_out, w_ffn1, w_ffn2)
        mod = _ada(cond, w_ada, b_ada[l].reshape(1, -1), l)
        mod3 = mod[:1 + n_lat].reshape(1 + n_lat, 6, D_MODEL)
        y_prompt, st = _block(y_prompt, mod3, lambda r: 0, lp, None, x_prompt.shape[1], True)
        for a, s in zip(acc, st):
            a.append(s)
        n_rep = jnp.broadcast_to(state_mlstm_n[:, l][..., None], state_mlstm_n[:, l].shape + (LANES,))
        c_aug0 = jnp.concatenate([state_mlstm_C[:, l], n_rep], axis=-1)
        c_aug0 = c_aug0.transpose(0, 1, 3, 2, 4).reshape(n_lat, N_DIR, DK_B, MW)
        m0 = jnp.zeros((n_lat, N_DIR, LANES), F32)
        for d in range(N_DIR):
            m0 = m0.at[:, d, LF_OFF + d * H_B:LF_OFF + (d + 1) * H_B].set(state_mlstm_m[:, l, d])
        m0 = jnp.broadcast_to(m0[:, :, None, :], (n_lat, N_DIR, SUBLANES, LANES))
        y_sample, _ = _block(y_sample, mod3, lambda r: 1 + r // t_lat, lp, (state_delta[:, l], c_aug0, m0),
                             GRID_W, False)
    return (y_prompt, y_sample) + tuple(jnp.stack(a, axis=1) for a in acc)
```

```python
import functools

import jax
import jax.numpy as jnp
from jax import lax
from jax.experimental import pallas as pl
from jax.experimental.pallas import tpu as pltpu

F32 = jnp.float32
BF16 = jnp.bfloat16

D_MODEL = 2048
N_DIR = 2
A_W = D_MODEL // 2
B_W = D_MODEL - A_W
DK_A = 128
DV_A = 128
H_A = A_W // DV_A
DV_B = 256
DK_B = DV_B // 2
H_B = B_W // DV_B
GRID_W = 64
CHUNK = 64
FFN = 4 * D_MODEL
EPS = 1e-6
LANES = 128
SUBLANES = 8
NEG = -1e30

QKV_W = 3 * A_W
REST_W = A_W + 2 * H_B * DK_B + 2 * B_W
OFF_AG, OFF_BQ, OFF_BK, OFF_BV, OFF_BO = 0, A_W, A_W + H_B * DK_B, A_W + 2 * H_B * DK_B, A_W + 2 * H_B * DK_B + B_W
G_OFF, BETA_OFF = 0, N_DIR * H_A
LI_OFF = 2 * N_DIR * H_A
LF_OFF = LI_OFF + N_DIR * H_B

VMEM_LIMIT = 56 * 1024 * 1024


def _sigmoid(x):
    return 1.0 / (1.0 + jnp.exp(-x))


def _softplus(x):
    return jnp.maximum(x, 0.0) + jnp.log1p(jnp.exp(-jnp.abs(x)))


def _dot(a, b):
    return jnp.dot(a.astype(BF16), b.astype(BF16), preferred_element_type=F32)


def _rms(x):
    return x * lax.rsqrt(jnp.mean(x * x, axis=-1, keepdims=True) + EPS)


def _params(sem):
    return pltpu.CompilerParams(dimension_semantics=sem, vmem_limit_bytes=VMEM_LIMIT)


def _ada_body(c_ref, w_ref, b_ref, o_ref):
    c = c_ref[...]
    o_ref[...] = _dot(c * _sigmoid(c), w_ref[...]) + b_ref[...]


def _ada(c_all, w_ada, b, l):
    n = w_ada.shape[2]
    tn = 2048
    return pl.pallas_call(
        _ada_body,
        grid=(n // tn,),
        in_specs=[pl.BlockSpec(c_all.shape, lambda j: (0, 0)),
                  pl.BlockSpec((None, D_MODEL, tn), lambda j: (l, 0, j)),
                  pl.BlockSpec((1, tn), lambda j: (0, j))],
        out_specs=pl.BlockSpec((c_all.shape[0], tn), lambda j: (0, j)),
        out_shape=jax.ShapeDtypeStruct((c_all.shape[0], n), F32),
        compiler_params=_params(("arbitrary",)),
        name="ada",
    )(c_all, w_ada, b)


PROJ_TM = 1024
PROJ_RB = 512
PROJ_CB = 256
PROJ_TN = 512
N_QKV_T = QKV_W // PROJ_TN
N_REST_T = REST_W // PROJ_TN
N_HEAD_T = (QKV_W + A_W) // PROJ_TN
N_MID_T = N_QKV_T + N_REST_T - N_HEAD_T


def _proj_body(x_ref, mod_ref, g_ref, wh_ref, wb_ref, wg_ref, cw_ref, gp_ref, qkv_ref, rest_ref, gate_ref, h_scr, *,
               seq_len):
    j = pl.program_id(1)
    row_blocks = [slice(rb * PROJ_RB, (rb + 1) * PROJ_RB) for rb in range(PROJ_TM // PROJ_RB)]
    col_blocks = [slice(cb * PROJ_CB, (cb + 1) * PROJ_CB) for cb in range(PROJ_TN // PROJ_CB)]

    def run_rows(rs, w_ref, epilogue, out_ref):
        for cs in col_blocks:
            epilogue(jnp.dot(h_scr[rs, :], w_ref[:, cs], preferred_element_type=F32), out_ref, rs, cs)

    def run(w_ref, epilogue, out_ref):
        for rs in row_blocks:
            run_rows(rs, w_ref, epilogue, out_ref)

    def run_first():
        for rs in row_blocks:
            y = _rms(x_ref[rs, :]) * g_ref[...]
            h_scr[rs, :] = (y * (1.0 + mod_ref[0, 1:2, :]) + mod_ref[0, 0:1, :]).astype(BF16)
            run_rows(rs, wh_ref, ep_l2(DK_A ** -0.5), qkv_ref)
            ep_gate(jnp.dot(h_scr[rs, :], wg_ref[...], preferred_element_type=F32), gate_ref, rs)

    def conv_silu(acc, cs, g):
        a = acc[:, g * LANES:(g + 1) * LANES]
        ls = slice(cs.start + g * LANES, cs.start + (g + 1) * LANES)
        pos = lax.broadcasted_iota(jnp.int32, a.shape, 0) & (seq_len - 1)
        prev = jnp.where(pos == 0, 0.0, pltpu.roll(a, 1, 0))
        nxt = jnp.where(pos == seq_len - 1, 0.0, pltpu.roll(a, a.shape[0] - 1, 0))
        y = prev * cw_ref[0:1, ls] + a * cw_ref[1:2, ls] + nxt * cw_ref[2:3, ls]
        return ls, y * _sigmoid(y)

    def ep_l2(scale):
        def f(acc, out_ref, rs, cs):
            for g in range(PROJ_CB // LANES):
                ls, blk = conv_silu(acc, cs, g)
                inv = lax.rsqrt(jnp.sum(blk * blk, axis=-1, keepdims=True) + EPS)
                out_ref[rs, ls] = blk * (inv * scale)
        return f

    def ep_conv(acc, out_ref, rs, cs):
        for g in range(PROJ_CB // LANES):
            ls, blk = conv_silu(acc, cs, g)
            out_ref[rs, ls] = blk

    def ep_map(fn):
        def f(acc, out_ref, rs, cs):
            out_ref[rs, cs] = fn(acc)
        return f

    def ep_gate(z, out_ref, rs):
        lane = lax.broadcasted_iota(jnp.int32, z.shape, 1)
        g = -jnp.exp(gp_ref[0:1, :]) * _softplus(z + gp_ref[1:2, :])
        li = z + gp_ref[2:3, :]
        out_ref[rs, :] = jnp.where(lane < BETA_OFF, g,
                                   jnp.where(lane < LI_OFF, _sigmoid(z),
                                             jnp.where(lane < LF_OFF, li,
                                                       jnp.where(lane < LF_OFF + N_DIR * H_B, -_softplus(-li), 0.0))))

    tp = A_W // PROJ_TN
    c0 = (j - N_QKV_T) * PROJ_TN
    in_rest = (j >= N_QKV_T) & (j < N_QKV_T + N_REST_T)
    pl.when(j == 0)(run_first)
    variants = [
        ((j > 0) & (j < tp), wh_ref, ep_l2(DK_A ** -0.5), qkv_ref),
        ((j >= tp) & (j < 2 * tp), wh_ref, ep_l2(1.0), qkv_ref),
        ((j >= 2 * tp) & (j < N_QKV_T), wh_ref, ep_conv, qkv_ref),
        (in_rest & (c0 < OFF_BQ), wh_ref, ep_map(lambda a: a * _sigmoid(a)), rest_ref),
        (in_rest & (c0 >= OFF_BQ) & (c0 < OFF_BK), wb_ref, ep_map(lambda a: a * (DK_B ** -0.5)), rest_ref),
        (in_rest & (c0 >= OFF_BK) & (c0 < OFF_BO), wb_ref, ep_map(lambda a: a), rest_ref),
        (in_rest & (c0 >= OFF_BO), wb_ref, ep_map(_sigmoid), rest_ref),
    ]
    for cond, w_ref, epilogue, out_ref in variants:
        pl.when(cond)(functools.partial(run, w_ref, epilogue, out_ref))


def _proj(x2d, mod3, mod_map, pre_g, w16, w_mid, w_gate, conv_w, gate_p, seq_len):
    m = x2d.shape[0]
    tm, tn = PROJ_TM, PROJ_TN
    assert PROJ_RB % seq_len == 0 and m % tm == 0
    return pl.pallas_call(
        functools.partial(_proj_body, seq_len=seq_len),
        grid=(m // tm, N_HEAD_T + N_MID_T),
        in_specs=[pl.BlockSpec((tm, D_MODEL), lambda i, j: (i, 0)),
                  pl.BlockSpec((1, 6, D_MODEL), lambda i, j: (mod_map(i), 0, 0)),
                  pl.BlockSpec((1, D_MODEL), lambda i, j: (0, 0)),
                  pl.BlockSpec((D_MODEL, tn), lambda i, j: (0, jnp.minimum(j, N_HEAD_T - 1))),
                  pl.BlockSpec((D_MODEL, tn), lambda i, j: (0, jnp.clip(j - N_HEAD_T, 0, N_MID_T - 1))),
                  pl.BlockSpec((D_MODEL, LANES), lambda i, j: (0, 0)),
                  pl.BlockSpec((SUBLANES, tn), lambda i, j: (0, jnp.minimum(j, N_QKV_T - 1))),
                  pl.BlockSpec((SUBLANES, LANES), lambda i, j: (0, 0))],
        out_specs=[pl.BlockSpec((tm, tn), lambda i, j: (i, jnp.minimum(j, N_QKV_T - 1))),
                   pl.BlockSpec((tm, tn), lambda i, j: (i, jnp.clip(j - N_QKV_T, 0, N_REST_T - 1))),
                   pl.BlockSpec((tm, LANES), lambda i, j: (i, 0))],
        out_shape=[jax.ShapeDtypeStruct((m, QKV_W), F32), jax.ShapeDtypeStruct((m, REST_W), F32),
                   jax.ShapeDtypeStruct((m, LANES), F32)],
        scratch_shapes=[pltpu.VMEM((tm, D_MODEL), BF16)],
        compiler_params=_params(("parallel", "arbitrary")),
        name="proj",
    )(x2d, mod3, pre_g, w16, w_mid, w_gate, conv_w, gate_p)


GROUP = 4
N_GROUPS = H_A // GROUP
PAIR = LANES // CHUNK


def _split(x):
    hi = x.astype(BF16)
    return hi, (x - hi.astype(F32)).astype(BF16)


def _store_blocks(ref, x, blk_r, blk_c, col0=0):
    for h in range(GROUP):
        ref[h * blk_r:(h + 1) * blk_r, col0 + h * blk_c:col0 + (h + 1) * blk_c] = x[:, h * blk_c:(h + 1) * blk_c]


def _dot3(a_hi, a_lo, b_hi, b_lo):
    m = a_hi.shape[0]
    r = jnp.dot(jnp.concatenate([a_hi, a_lo], axis=0), b_hi, preferred_element_type=F32)
    return r[:m] + r[m:] + jnp.dot(a_hi, b_lo, preferred_element_type=F32)


HALF = CHUNK // 2


def _tri_inverse_wide(lmats, upper, bd_hi, bd_lo, by_hi, by_lo):
    width = lmats[0].shape[1]
    n_blk = width // HALF
    row = lax.broadcasted_iota(jnp.int32, (HALF, width), 0)
    lane = lax.broadcasted_iota(jnp.int32, (HALF, width), 1)
    lead = (lane & (CHUNK - 1)) < HALF
    eye_d = (row == (lane & (HALF - 1))).astype(F32)

    def bd_dot(i, a, b_hi, b_lo):
        for blk in range(n_blk):
            sl = slice(blk * HALF, (blk + 1) * HALF)
            bd_hi[i, sl, sl] = b_hi[:, sl]
            bd_lo[i, sl, sl] = b_lo[:, sl]
        return _dot3(*a, bd_hi[i], bd_lo[i])

    idx = range(len(lmats))
    diag = [jnp.where(lead, l[:HALF], l[HALF:]) for l in lmats]
    s = [eye_d - dg for dg in diag]
    p = []
    for i in idx:
        m_hi, m_lo = _split(-diag[i])
        p.append(bd_dot(i, (m_hi, m_lo), m_hi, m_lo))
    for _ in range(3):
        for i in idx:
            p_hi, p_lo = _split(p[i])
            s_hi, s_lo = _split(s[i])
            r = bd_dot(i, (jnp.concatenate([p_hi, s_hi], axis=0), jnp.concatenate([p_lo, s_lo], axis=0)), p_hi, p_lo)
            p[i] = r[:HALF]
            s[i] = s[i] + r[HALF:]
    for i in idx:
        p_hi, p_lo = _split(p[i])
        s[i] = s[i] + bd_dot(i, _split(s[i]), p_hi, p_lo)
    y = []
    for i in idx:
        c_blk = jnp.where(lead, 0.0, lmats[i][:HALF]) if upper[i] else jnp.where(lead, lmats[i][HALF:], 0.0)
        y.append(bd_dot(i, _split(c_blk), *_split(s[i])))
    out = []
    for i in idx:
        y_hi, y_lo = _split(y[i])
        for h in range(width // CHUNK):
            a_sl, b_sl = slice(h * CHUNK, h * CHUNK + HALF), slice(h * CHUNK + HALF, (h + 1) * CHUNK)
            rs, cs = (a_sl, b_sl) if upper[i] else (b_sl, a_sl)
            by_hi[i, rs, cs] = y_hi[:, cs]
            by_lo[i, rs, cs] = y_lo[:, cs]
        ai, bi = jnp.where(lead, s[i], 0.0), jnp.where(lead, 0.0, s[i])
        x = _dot3(*_split(ai if upper[i] else bi), by_hi[i], by_lo[i])
        out.append(jnp.concatenate([ai - x, bi] if upper[i] else [ai, bi - x], axis=0))
    return out


def _col_bcast(tile, c, width=LANES):
    return jnp.broadcast_to(tile[:, c:c + 1], (tile.shape[0], width))


def _delta_body(*refs, n_chunks, bb, has_init, emit_state, n_cast):
    qkv = refs[0:2]
    ag = refs[2:4]
    gt = refs[4:6]
    norm_ref = refs[6]
    pos = 7
    s0_ref = None
    if has_init:
        s0_ref = refs[pos]
        pos += 1
    cast_in = refs[pos:pos + n_cast]
    pos += n_cast
    ya_ref = refs[pos]
    pos += 1
    sout_ref = None
    if emit_state:
        sout_ref = refs[pos]
        pos += 1
    for w_ref, w16_ref in zip(cast_in, refs[pos:pos + n_cast]):
        w16_ref[...] = w_ref[...].astype(BF16)
    pos += n_cast
    s_scr, o_scr, bdn_hi, bdn_lo, bdy_hi, bdy_lo, bdk, bduw, bds, bdv = refs[pos:pos + 10]

    n = pl.program_id(1)
    gw = GROUP * DK_A

    @pl.when((n == 0) & (pl.program_id(0) == 0))
    def _():
        for ref in (bdn_hi, bdn_lo, bdy_hi, bdy_lo, bdk, bduw, bds, bdv):
            ref[...] = jnp.zeros_like(ref)

    @pl.when(n == 0)
    def _():
        o_scr[...] = jnp.zeros_like(o_scr)
        for bi in range(bb):
            for d in range(N_DIR):
                for h in range(H_A):
                    blk = s0_ref[bi, d, h] if has_init else jnp.zeros((DK_A, DV_A), F32)
                    s_scr[bi, d, h // GROUP, :, (h % GROUP) * DV_A:(h % GROUP + 1) * DV_A] = blk

    row = lax.broadcasted_iota(jnp.int32, (CHUNK, LANES), 0)
    lane = lax.broadcasted_iota(jnp.int32, (CHUNK, LANES), 1)
    col = lane & (CHUNK - 1)
    left = lane < CHUNK
    probs = [(bi, d, g) for bi in range(bb) for d in range(N_DIR) for g in range(N_GROUPS)]
    masks = {0: (row >= col, row > col), 1: (row <= col, row < col)}
    gates, gc, gc_t, eg, egl = {}, {}, {}, {}, {}
    for bi in range(bb):
        for d in range(N_DIR):
            key = (bi, d)
            gates[key] = gt[d][bi]
            gc[key] = _scan_sum(gates[key], d)
            gc_t[key] = jnp.concatenate([gc[key], gc[key]], axis=0).T
            last = CHUNK - 1 if d == 0 else 0
            eg[key] = jnp.exp(gc[key])
            egl[key] = jnp.exp(gc[key][last:last + 1, :] - gc[key])

    def cols_of(d, g):
        return [G_OFF + d * H_A + g * GROUP + hl for hl in range(GROUP)]

    q16, ks, beta_xs, decays, grams = [], [], [], [], []
    for gi, (bi, d, g) in enumerate(probs):
        key = (bi, d)
        cols = cols_of(d, g)
        incl, _ = masks[d]
        k = qkv[d][bi, :, A_W + g * gw:A_W + (g + 1) * gw]
        beta_x = jnp.concatenate([_col_bcast(gates[key], BETA_OFF - G_OFF + c) for c in cols], axis=1)
        decay = []
        for p in range(GROUP // PAIR):
            c0, c1 = cols[PAIR * p], cols[PAIR * p + 1]
            gcol = jnp.where(left, _col_bcast(gc[key], c0), _col_bcast(gc[key], c1))
            grow = jnp.where(left[0:1], gc_t[key][c0:c0 + 1, :], gc_t[key][c1:c1 + 1, :])
            decay.append(jnp.exp(jnp.where(incl, gcol - grow, NEG)))
        decays.append(jnp.concatenate(decay, axis=1))
        q16.append(qkv[d][bi, :, g * gw:(g + 1) * gw].astype(BF16))
        ks.append(k)
        beta_xs.append(beta_x)
        _store_blocks(bdk.at[gi], k.astype(BF16), CHUNK, DK_A)
    for gi in range(len(probs)):
        grams.append(lax.dot_general(jnp.concatenate([q16[gi], (ks[gi] * beta_xs[gi]).astype(BF16)], axis=0),
                                     bdk[gi], (((1,), (1,)), ((), ())), preferred_element_type=F32))
    attns, lmats = [], []
    for gi, (bi, d, g) in enumerate(probs):
        strict_w = jnp.concatenate([masks[d][1]] * (GROUP // PAIR), axis=1)
        attns.append((grams[gi][:CHUNK] * decays[gi]).astype(BF16))
        lmats.append(jnp.where(strict_w, grams[gi][CHUNK:] * decays[gi], 0.0))
    ainvs = _tri_inverse_wide(lmats, [d == 1 for _, d, _ in probs], bdn_hi, bdn_lo, bdy_hi, bdy_lo)
    eg_xs, kd_ts = [], []
    for gi, (bi, d, g) in enumerate(probs):
        key = (bi, d)
        cols = cols_of(d, g)
        v = qkv[d][bi, :, 2 * A_W + g * gw:2 * A_W + (g + 1) * gw]
        eg_x = jnp.concatenate([_col_bcast(eg[key], c) for c in cols], axis=1)
        egl_x = jnp.concatenate([_col_bcast(egl[key], c) for c in cols], axis=1)
        eg_xs.append(eg_x)
        _store_blocks(bduw.at[gi], (v * beta_xs[gi]).astype(BF16), CHUNK, DV_A)
        _store_blocks(bduw.at[gi], (ks[gi] * (beta_xs[gi] * eg_x)).astype(BF16), CHUNK, DK_A, col0=gw)
        kd = ks[gi] * egl_x
        kd_ts.append(jnp.concatenate([kd[:, hl * DK_A:(hl + 1) * DK_A] for hl in range(GROUP)], axis=0)
                     .T.astype(BF16))
        _store_blocks(bds.at[gi], s_scr[bi, d, g].astype(BF16), DK_A, DV_A)
    uws = []
    for gi in range(len(probs)):
        t_hi, t_lo = _split(ainvs[gi])
        r = jnp.dot(jnp.concatenate([t_hi, t_lo], axis=0), bduw[gi], preferred_element_type=F32)
        uws.append(r[:CHUNK] + r[CHUNK:])
    ws_qs = []
    hw = gw // 2
    for gi in range(len(probs)):
        wq = jnp.concatenate([uws[gi][:, gw:].astype(BF16), q16[gi]], axis=0)
        ws_qs.append(jnp.concatenate(
            [jnp.dot(wq[:, p * hw:(p + 1) * hw], bds[gi, p * hw:(p + 1) * hw, p * hw:(p + 1) * hw],
                     preferred_element_type=F32) for p in range(2)], axis=1))
    for gi in range(len(probs)):
        _store_blocks(bdv.at[gi], (uws[gi][:, :gw] - ws_qs[gi][:CHUNK]).astype(BF16), CHUNK, DV_A)
    rs = []
    for gi in range(len(probs)):
        rs.append(jnp.dot(jnp.concatenate([attns[gi], kd_ts[gi]], axis=0), bdv[gi], preferred_element_type=F32))
    tots = []
    for gi, (bi, d, g) in enumerate(probs):
        cols = cols_of(d, g)
        last = CHUNK - 1 if d == 0 else 0
        o = ws_qs[gi][CHUNK:] * eg_xs[gi] + rs[gi][:CHUNK]
        eg_last = jnp.concatenate(
            [jnp.broadcast_to(eg[bi, d][last:last + 1, c:c + 1], (1, DV_A)) for c in cols], axis=1)
        s_scr[bi, d, g] = s_scr[bi, d, g] * eg_last + rs[gi][CHUNK:]
        cidx = n if d == 0 else n_chunks - 1 - n
        rows = pl.ds(pl.multiple_of(cidx * CHUNK, CHUNK), CHUNK)
        tot = o + o_scr[bi, rows, g * gw:(g + 1) * gw]
        o_scr[bi, rows, g * gw:(g + 1) * gw] = tot
        tots.append((rows, tot))
    blks = [(bi, d, rows, g * GROUP + hl, tot[:, hl * DV_A:(hl + 1) * DV_A])
            for (bi, d, g), (rows, tot) in zip(probs, tots) for hl in range(GROUP)]
    inv = [lax.rsqrt(jnp.mean(blk * blk, axis=-1, keepdims=True) + EPS) for *_, blk in blks]
    for (bi, d, rows, h, blk), r in zip(blks, inv):
        hs = slice(h * DV_A, (h + 1) * DV_A)
        ya_ref[bi, rows, hs] = (blk * r * norm_ref[...] * ag[d][bi, :, hs]).astype(BF16)

    if emit_state:
        @pl.when(n == n_chunks - 1)
        def _():
            for bi in range(bb):
                for d in range(N_DIR):
                    for h in range(H_A):
                        sout_ref[bi, d, h] = s_scr[bi, d, h // GROUP, :, (h % GROUP) * DV_A:(h % GROUP + 1) * DV_A]


SCAN_BB = 2


def _scan_specs(bb, n_chunks, width):
    return (pl.BlockSpec((bb, CHUNK, width), lambda b, n: (b, n, 0)),
            pl.BlockSpec((bb, CHUNK, width), lambda b, n: (b, n_chunks - 1 - n, 0)))


def _cast_specs(casts, steps_of):
    in_specs, out_specs, out_shape, args = [], [], [], []
    for w, l, steps in casts:
        rows = w.shape[1] // steps
        assert rows * steps == w.shape[1] and rows % (2 * SUBLANES) == 0
        in_specs.append(pl.BlockSpec((None, rows, w.shape[2]), lambda b, n, l=l: (l, steps_of(b, n), 0)))
        out_specs.append(pl.BlockSpec((rows, w.shape[2]), lambda b, n: (steps_of(b, n), 0)))
        out_shape.append(jax.ShapeDtypeStruct(w.shape[1:], BF16))
        args.append(w)
    return in_specs, out_specs, out_shape, args


def _delta(qkv, rest, gates, norm_a, s0, batch, n_chunks, emit_state, casts=()):
    t = n_chunks * CHUNK
    has_init = s0 is not None
    bb = SCAN_BB
    ng = bb * N_DIR * N_GROUPS
    qkv, rest, gates = (a.reshape(batch, t, a.shape[-1]) for a in (qkv, rest, gates))
    in_specs = [*_scan_specs(bb, n_chunks, QKV_W), *_scan_specs(bb, n_chunks, A_W), *_scan_specs(bb, n_chunks, LANES),
                pl.BlockSpec((1, DV_A), lambda b, n: (0, 0))]
    args = [qkv, qkv, rest, rest, gates, gates, norm_a]
    sspec = pl.BlockSpec((bb, N_DIR, H_A, DK_A, DV_A), lambda b, n: (b, 0, 0, 0, 0))
    if has_init:
        in_specs.append(sspec)
        args.append(s0)
    out_specs = [pl.BlockSpec((bb, t, A_W), lambda b, n: (b, 0, 0))]
    out_shape = [jax.ShapeDtypeStruct((batch, t, A_W), BF16)]
    if emit_state:
        out_specs.append(sspec)
        out_shape.append(jax.ShapeDtypeStruct((batch, N_DIR, H_A, DK_A, DV_A), F32))
    steps = (batch // bb) * n_chunks
    c_in, c_out, c_shape, c_args = _cast_specs([(w, l, steps) for w, l in casts], lambda b, n: b * n_chunks + n)
    in_specs, args, out_specs, out_shape = in_specs + c_in, args + c_args, out_specs + c_out, out_shape + c_shape
    out = pl.pallas_call(
        functools.partial(_delta_body, n_chunks=n_chunks, bb=bb, has_init=has_init, emit_state=emit_state,
                          n_cast=len(casts)),
        grid=(batch // bb, n_chunks),
        in_specs=in_specs,
        out_specs=out_specs,
        out_shape=out_shape,
        scratch_shapes=[pltpu.VMEM((bb, N_DIR, N_GROUPS, DK_A, GROUP * DV_A), F32), pltpu.VMEM((bb, t, A_W), F32),
                        pltpu.VMEM((ng, GROUP * CHUNK, GROUP * CHUNK), BF16),
                        pltpu.VMEM((ng, GROUP * CHUNK, GROUP * CHUNK), BF16),
                        pltpu.VMEM((ng, GROUP * CHUNK, GROUP * CHUNK), BF16),
                        pltpu.VMEM((ng, GROUP * CHUNK, GROUP * CHUNK), BF16),
                        pltpu.VMEM((ng, GROUP * CHUNK, GROUP * DK_A), BF16),
                        pltpu.VMEM((ng, GROUP * CHUNK, 2 * GROUP * DK_A), BF16),
                        pltpu.VMEM((ng, GROUP * DK_A, GROUP * DV_A), BF16),
                        pltpu.VMEM((ng, GROUP * CHUNK, GROUP * DV_A), BF16)],
        compiler_params=_params(("arbitrary", "arbitrary")),
        name="delta_scan",
    )(*args)
    return [out[0].reshape(batch * t, A_W), *out[1:]]


CAUG_W = DV_B + LANES
MW = H_B * CAUG_W


def _scan_rows(x, d, combine, fill):
    row = lax.broadcasted_iota(jnp.int32, x.shape, 0)
    s = 1
    while s < CHUNK:
        if d == 0:
            shifted = jnp.where(row >= s, pltpu.roll(x, s, 0), fill)
        else:
            shifted = jnp.where(row < CHUNK - s, pltpu.roll(x, CHUNK - s, 0), fill)
        x = combine(x, shifted)
        s *= 2
    return x


def _scan_max(x, d):
    return _scan_rows(x, d, jnp.maximum, NEG)


def _scan_sum(x, d):
    return _scan_rows(x, d, jnp.add, 0.0)


def _mlstm_body(*refs, n_chunks, bb, has_init, emit_state, n_cast):
    rest = refs[0:2]
    gt = refs[2:4]
    norm_ref = refs[4]
    pos = 5
    c0_ref = m0_ref = None
    if has_init:
        c0_ref, m0_ref = refs[pos], refs[pos + 1]
        pos += 2
    cast_in = refs[pos:pos + n_cast]
    pos += n_cast
    yb_ref = refs[pos]
    pos += 1
    cout_ref = nout_ref = mout_ref = None
    if emit_state:
        cout_ref, nout_ref, mout_ref = refs[pos:pos + 3]
        pos += 3
    for w_ref, w16_ref in zip(cast_in, refs[pos:pos + n_cast]):
        w16_ref[...] = w_ref[...].astype(BF16)
    pos += n_cast
    c_scr, m_scr, o_scr, bdk, bdc, bdv = refs[pos:pos + 6]

    n = pl.program_id(1)
    qw = H_B * DK_B

    @pl.when((n == 0) & (pl.program_id(0) == 0))
    def _():
        for ref in (bdk, bdc, bdv):
            ref[...] = jnp.zeros_like(ref)
        for i in range(bb * N_DIR):
            for h in range(H_B):
                bdv[i, h * CHUNK:(h + 1) * CHUNK, h * CAUG_W + DV_B:(h + 1) * CAUG_W] = jnp.ones((CHUNK, LANES), BF16)

    @pl.when(n == 0)
    def _():
        o_scr[...] = jnp.zeros_like(o_scr)
        if has_init:
            c_scr[...] = c0_ref[...]
            m_scr[...] = m0_ref[...]
        else:
            c_scr[...] = jnp.zeros_like(c_scr)
            m_scr[...] = jnp.zeros_like(m_scr)

    row = lax.broadcasted_iota(jnp.int32, (CHUNK, LANES), 0)
    lane = lax.broadcasted_iota(jnp.int32, (CHUNK, LANES), 1)
    col = lane & (CHUNK - 1)
    left = lane < CHUNK
    dirs = [(bi, d) for bi in range(bb) for d in range(N_DIR)]
    slot = {key: i for i, key in enumerate(dirs)}
    q16 = {}
    for key in dirs:
        bi, d = key
        i = slot[key]
        q16[key] = rest[d][bi, :, OFF_BQ:OFF_BQ + qw].astype(BF16)
        _store_blocks(bdk.at[i], rest[d][bi, :, OFF_BK:OFF_BK + qw].astype(BF16), CHUNK, DK_B)
        for h in range(H_B):
            bdv[i, h * CHUNK:(h + 1) * CHUNK, h * CAUG_W:h * CAUG_W + DV_B] = (
                rest[d][bi, :, OFF_BV + h * DV_B:OFF_BV + (h + 1) * DV_B].astype(BF16))
        _store_blocks(bdc.at[i], c_scr[bi, d].astype(BF16), DK_B, CAUG_W)
    qk = {key: lax.dot_general(q16[key], bdk[slot[key]], (((1,), (1,)), ((), ())), preferred_element_type=F32)
          for key in dirs}
    nc, a_t, iw, emt, ksc, dec_row = {}, {}, {}, {}, {}, {}
    for key in dirs:
        bi, d = key
        lo = LF_OFF + d * H_B
        mine = (lane >= lo) & (lane < lo + H_B)
        g = gt[d][bi]
        gc = jnp.where(mine, _scan_sum(g, d), 0.0)
        a = jnp.where(mine, pltpu.roll(g, LF_OFF - LI_OFF, 1), 0.0) - gc
        last = CHUNK - 1 if d == 0 else 0
        m_old = m_scr[bi, d][0:1, :]
        mx = jnp.maximum(m_old, _scan_max(a, d))
        mxl = mx[last:last + 1, :]
        nc[key] = -mx
        a_t[key] = jnp.concatenate([a, a], axis=0).T
        iw[key] = jnp.exp(m_old - mx)
        emt[key] = jnp.exp(-(gc + mx))
        ksc[key] = jnp.exp(a - mxl)
        dec_row[key] = jnp.exp(m_old - mxl)
        m_scr[bi, d] = jnp.broadcast_to(gc[last:last + 1, :] + mxl, (SUBLANES, LANES))

    ks_t = {}
    for key in dirs:
        bi, d = key
        lo = LF_OFF + d * H_B
        ks = rest[d][bi, :, OFF_BK:OFF_BK + qw] * jnp.concatenate(
            [_col_bcast(ksc[key], lo + h) for h in range(H_B)], axis=1)
        ks_t[key] = jnp.concatenate([ks[:, h * DK_B:(h + 1) * DK_B] for h in range(H_B)], axis=0).T.astype(BF16)
    lhs = {}
    for key in dirs:
        bi, d = key
        lo = LF_OFF + d * H_B
        incl = row >= col if d == 0 else row <= col
        log_w = []
        for p in range(H_B // PAIR):
            l0, l1 = lo + PAIR * p, lo + PAIR * p + 1
            ccol = jnp.where(left, _col_bcast(nc[key], l0), _col_bcast(nc[key], l1))
            crow = jnp.where(left[0:1], a_t[key][l0:l0 + 1, :], a_t[key][l1:l1 + 1, :])
            log_w.append(jnp.where(incl, ccol + crow, NEG))
        dw = jnp.exp(jnp.concatenate(log_w, axis=1)) * qk[key]
        iw_x = jnp.concatenate([_col_bcast(iw[key], lo + h) for h in range(H_B)], axis=1)
        lhs[key] = ((rest[d][bi, :, OFF_BQ:OFF_BQ + qw] * iw_x).astype(BF16), dw.astype(BF16))
    num = {}
    for key in dirs:
        i = slot[key]
        parts = []
        for p in range(H_B // PAIR):
            ql, dl, cl = slice(p * PAIR * DK_B, (p + 1) * PAIR * DK_B), slice(p * LANES, (p + 1) * LANES), \
                slice(p * PAIR * CAUG_W, (p + 1) * PAIR * CAUG_W)
            parts.append(jnp.dot(jnp.concatenate([lhs[key][0][:, ql], lhs[key][1][:, dl]], axis=1),
                                 jnp.concatenate([bdc[i, ql, cl], bdv[i, dl, cl]], axis=0),
                                 preferred_element_type=F32))
        num[key] = jnp.concatenate(parts, axis=1)
    upd = {key: jnp.dot(ks_t[key], bdv[slot[key]], preferred_element_type=F32) for key in dirs}
    for key in dirs:
        bi, d = key
        lo = LF_OFF + d * H_B
        dec_x = jnp.concatenate(
            [jnp.broadcast_to(dec_row[key][:, lo + h:lo + h + 1], (1, CAUG_W)) for h in range(H_B)], axis=1)
        c_scr[bi, d] = c_scr[bi, d] * dec_x + upd[key]
    heads = [(key, h) for key in dirs for h in range(H_B)]
    rows = {d: pl.ds(pl.multiple_of((n if d == 0 else n_chunks - 1 - n) * CHUNK, CHUNK), CHUNK) for d in range(N_DIR)}
    tots = []
    for key, h in heads:
        bi, d = key
        vs = slice(h * DV_B, (h + 1) * DV_B)
        den = jnp.maximum(jnp.abs(num[key][:, h * CAUG_W + DV_B:(h + 1) * CAUG_W]),
                          _col_bcast(emt[key], LF_OFF + d * H_B + h))
        hb = jnp.concatenate([num[key][:, h * CAUG_W:h * CAUG_W + LANES] / den,
                              num[key][:, h * CAUG_W + LANES:h * CAUG_W + DV_B] / den], axis=1)
        tot = hb + o_scr[bi, rows[d], vs]
        o_scr[bi, rows[d], vs] = tot
        tots.append(tot)
    inv = [lax.rsqrt(jnp.mean(tot * tot, axis=-1, keepdims=True) + EPS) for tot in tots]
    for (key, h), tot, r in zip(heads, tots, inv):
        bi, d = key
        vs = slice(h * DV_B, (h + 1) * DV_B)
        ogate = rest[d][bi, :, OFF_BO + h * DV_B:OFF_BO + (h + 1) * DV_B]
        yb_ref[bi, rows[d], vs] = (tot * r * norm_ref[...] * ogate).astype(BF16)

    if emit_state:
        @pl.when(n == n_chunks - 1)
        def _():
            for bi, d in dirs:
                for h in range(H_B):
                    cout_ref[bi, d, h] = c_scr[bi, d, :, h * CAUG_W:h * CAUG_W + DV_B]
                    nout_ref[bi, d, h] = c_scr[bi, d, :, h * CAUG_W + DV_B:(h + 1) * CAUG_W]
            mout_ref[...] = m_scr[...]


def _mlstm(rest, gates, norm_b, c0, m0, batch, n_chunks, emit_state, casts=()):
    t = n_chunks * CHUNK
    has_init = c0 is not None
    bb = SCAN_BB
    rest, gates = (a.reshape(batch, t, a.shape[-1]) for a in (rest, gates))
    mspec = pl.BlockSpec((bb, N_DIR, SUBLANES, LANES), lambda b, n: (b, 0, 0, 0))
    in_specs = [*_scan_specs(bb, n_chunks, REST_W), *_scan_specs(bb, n_chunks, LANES),
                pl.BlockSpec((1, DV_B), lambda b, n: (0, 0))]
    args = [rest, rest, gates, gates, norm_b]
    if has_init:
        in_specs += [pl.BlockSpec((bb, N_DIR, DK_B, MW), lambda b, n: (b, 0, 0, 0)), mspec]
        args += [c0, m0]
    out_specs = [pl.BlockSpec((bb, t, B_W), lambda b, n: (b, 0, 0))]
    out_shape = [jax.ShapeDtypeStruct((batch, t, B_W), BF16)]
    if emit_state:
        out_specs += [pl.BlockSpec((bb, N_DIR, H_B, DK_B, DV_B), lambda b, n: (b, 0, 0, 0, 0)),
                      pl.BlockSpec((bb, N_DIR, H_B, DK_B, LANES), lambda b, n: (b, 0, 0, 0, 0)), mspec]
        out_shape += [jax.ShapeDtypeStruct((batch, N_DIR, H_B, DK_B, DV_B), F32),
                      jax.ShapeDtypeStruct((batch, N_DIR, H_B, DK_B, LANES), F32),
                      jax.ShapeDtypeStruct((batch, N_DIR, SUBLANES, LANES), F32)]
    steps = (batch // bb) * n_chunks
    c_in, c_out, c_shape, c_args = _cast_specs([(w, l, steps) for w, l in casts], lambda b, n: b * n_chunks + n)
    in_specs, args, out_specs, out_shape = in_specs + c_in, args + c_args, out_specs + c_out, out_shape + c_shape
    out = pl.pallas_call(
        functools.partial(_mlstm_body, n_chunks=n_chunks, bb=bb, has_init=has_init, emit_state=emit_state,
                          n_cast=len(casts)),
        grid=(batch // bb, n_chunks),
        in_specs=in_specs,
        out_specs=out_specs,
        out_shape=out_shape,
        scratch_shapes=[pltpu.VMEM((bb, N_DIR, DK_B, MW), F32),
                        pltpu.VMEM((bb, N_DIR, SUBLANES, LANES), F32),
                        pltpu.VMEM((bb, t, B_W), F32),
                        pltpu.VMEM((bb * N_DIR, H_B * CHUNK, H_B * DK_B), BF16),
                        pltpu.VMEM((bb * N_DIR, H_B * DK_B, MW), BF16),
                        pltpu.VMEM((bb * N_DIR, H_B * CHUNK, MW), BF16)],
        compiler_params=_params(("arbitrary", "arbitrary")),
        name="mlstm_scan",
    )(*args)
    return [out[0].reshape(batch * t, B_W), *out[1:]]


OUT_TM = 512
OUT_RB = 256
FFN_TM = 512
FFN_FC = 1024


def _outproj_body(ya_ref, yb_ref, wa_ref, wb_ref, x_ref, mod_ref, post1_ref, pre2_ref, x1_ref, h2_ref):
    for rb in range(x_ref.shape[0] // OUT_RB):
        rs = slice(rb * OUT_RB, (rb + 1) * OUT_RB)
        mix = (jnp.dot(ya_ref[rs, :], wa_ref[...], preferred_element_type=F32)
               + jnp.dot(yb_ref[rs, :], wb_ref[...], preferred_element_type=F32))
        x1 = x_ref[rs, :] + mod_ref[0, 2:3, :] * (_rms(mix) * post1_ref[...])
        x1_ref[rs, :] = x1
        h2 = _rms(x1) * pre2_ref[...] * (1.0 + mod_ref[0, 4:5, :]) + mod_ref[0, 3:4, :]
        h2_ref[rs, :] = h2.astype(BF16)


def _outproj(ya, yb, w_out, x2d, mod3, mod_map, post1, pre2, tm):
    m = x2d.shape[0]
    row = lambda i: (i, 0)
    const = lambda i: (0, 0)
    return pl.pallas_call(
        _outproj_body,
        grid=(m // tm,),
        in_specs=[pl.BlockSpec((tm, A_W), row), pl.BlockSpec((tm, B_W), row),
                  pl.BlockSpec((A_W, D_MODEL), lambda i: (0, 0)),
                  pl.BlockSpec((B_W, D_MODEL), lambda i: (1, 0)),
                  pl.BlockSpec((tm, D_MODEL), row),
                  pl.BlockSpec((1, 6, D_MODEL), lambda i: (mod_map(i * tm), 0, 0)),
                  pl.BlockSpec((1, D_MODEL), const), pl.BlockSpec((1, D_MODEL), const)],
        out_specs=[pl.BlockSpec((tm, D_MODEL), row), pl.BlockSpec((tm, D_MODEL), row)],
        out_shape=[jax.ShapeDtypeStruct((m, D_MODEL), F32), jax.ShapeDtypeStruct((m, D_MODEL), BF16)],
        compiler_params=_params(("parallel",)),
        name="outproj",
    )(ya, yb, w_out, w_out, x2d, mod3, post1, pre2)


def _ffn_body(h2_ref, w1_ref, w2_ref, x1_ref, mod_ref, post2_ref, o_ref):
    kk = pl.program_id(1)

    @pl.when(kk == 0)
    def _():
        o_ref[...] = jnp.zeros_like(o_ref)

    def accumulate(rs):
        a = jnp.maximum(jnp.dot(h2_ref[rs, :], w1_ref[...], preferred_element_type=F32), 0.0)
        o_ref[rs, :] += jnp.dot((a * a).astype(BF16), w2_ref[...], preferred_element_type=F32)

    last = pl.num_programs(1) - 1
    tm = o_ref.shape[0]

    @pl.when(kk < last)
    def _():
        accumulate(slice(0, tm))

    @pl.when(kk == last)
    def _():
        for rs in (slice(0, tm // 2), slice(tm // 2, tm)):
            accumulate(rs)
            o_ref[rs, :] = x1_ref[rs, :] + mod_ref[0, 5:6, :] * (_rms(o_ref[rs, :]) * post2_ref[...])


def _ffn(h2, w1, w2, x1, mod3, mod_map, post2, tm, fc):
    m = h2.shape[0]
    return pl.pallas_call(
        _ffn_body,
        grid=(m // tm, FFN // fc),
        in_specs=[pl.BlockSpec((tm, D_MODEL), lambda i, k: (i, 0)),
                  pl.BlockSpec((D_MODEL, fc), lambda i, k: (0, k)),
                  pl.BlockSpec((fc, D_MODEL), lambda i, k: (k, 0)),
                  pl.BlockSpec((tm, D_MODEL), lambda i, k: (i, 0)),
                  pl.BlockSpec((1, 6, D_MODEL), lambda i, k: (mod_map(i * tm), 0, 0)),
                  pl.BlockSpec((1, D_MODEL), lambda i, k: (0, 0))],
        out_specs=pl.BlockSpec((tm, D_MODEL), lambda i, k: (i, 0)),
        out_shape=jax.ShapeDtypeStruct((m, D_MODEL), F32),
        compiler_params=_params(("parallel", "arbitrary")),
        name="ffn",
    )(h2, w1, w2, x1, mod3, post2)


def _block(x, mod3, mod_of_row, lp, init, seq_len, emit_state):
    bsz, t, _ = x.shape
    x2d = x.reshape(bsz * t, D_MODEL)
    n_chunks = t // CHUNK
    qkv, rest, gates = _proj(x2d, mod3, lambda i: mod_of_row(i * PROJ_TM), lp["pre1"], lp["w16"],
                             lp["w_mid"], lp["w_gate"], lp["conv_w"], lp["gate_p"], seq_len)
    s0, c0, m0 = init if init is not None else (None, None, None)
    first = "w1" not in lp
    l = lp["l"]
    d_out = _delta(qkv, rest, gates, lp["norm_a"], s0, bsz, n_chunks, emit_state,
                   [(lp["w_ffn1"], l), (lp["w_ffn2"], l), (lp["w_out32"], l)] if first else ())
    m_out = _mlstm(rest, gates, lp["norm_b"], c0, m0, bsz, n_chunks, emit_state)
    if first:
        lp["w1"], lp["w2"], lp["w_out"] = d_out[-3:]
    x1, h2 = _outproj(d_out[0], m_out[0], lp["w_out"], x2d, mod3, mod_of_row, lp["post1"], lp["pre2"], OUT_TM)
    y = _ffn(h2, lp["w1"], lp["w2"], x1, mod3, mod_of_row, lp["post2"], FFN_TM, FFN_FC)
    states = None
    if emit_state:
        m_fin = jnp.stack([m_out[3][:, d, 0, LF_OFF + d * H_B:LF_OFF + (d + 1) * H_B] for d in range(N_DIR)], axis=1)
        states = (d_out[1], m_out[1], m_out[2][..., 0], m_fin)
    return y.reshape(bsz, t, D_MODEL), states


def _layer_params(l, norm_mix_pre, norm_mix_post, norm_ffn_pre, norm_ffn_post, w_in, conv_w, a_log, dt_bias,
                  norm_a, mlstm_ibias, mlstm_fbias, norm_b, w_out, w_ffn1, w_ffn2):
    w = w_in[l]
    o_ag = QKV_W
    o_aa = o_ag + A_W
    o_ab = o_aa + N_DIR * H_A
    o_bq = o_ab + N_DIR * H_A
    o_bi = o_bq + 2 * H_B * DK_B + 2 * B_W
    o_bf = o_bi + N_DIR * H_B
    n_gate = 2 * N_DIR * H_A + 2 * N_DIR * H_B
    w16 = w.astype(BF16)
    w_gate = jnp.concatenate([w16[:, o_aa:o_bq], w16[:, o_bi:o_bf + N_DIR * H_B],
                              jnp.zeros((D_MODEL, LANES - n_gate), BF16)], axis=1)

    def lane_row(vals, off):
        return jnp.zeros((LANES,), F32).at[off:off + vals.size].set(vals.reshape(-1))

    gate_p = jnp.stack([lane_row(a_log[l], G_OFF), lane_row(dt_bias[l], G_OFF),
                        lane_row(mlstm_ibias[l], LI_OFF) + lane_row(mlstm_fbias[l], LF_OFF)]
                       + [jnp.zeros((LANES,), F32)] * (SUBLANES - 3))
    row = lambda v: v[l].reshape(1, -1)
    return dict(
        pre1=row(norm_mix_pre), post1=row(norm_mix_post), pre2=row(norm_ffn_pre), post2=row(norm_ffn_post),
        w16=w16, w_mid=w16[:, o_bq:o_bi], w_gate=w_gate, gate_p=gate_p,
        conv_w=jnp.concatenate([conv_w[l].T, jnp.zeros((SUBLANES - 3, QKV_W), F32)], axis=0),
        norm_a=row(norm_a), norm_b=row(norm_b),
        l=l, w_out32=w_out, w_ffn1=w_ffn1, w_ffn2=w_ffn2)


def kernel(x_prompt, x_sample, state_delta, state_mlstm_C, state_mlstm_n, state_mlstm_m, c, c_ctx, w_ada, b_ada, norm_mix_pre, norm_mix_post, norm_ffn_pre, norm_ffn_post, w_in, conv_w, a_log, dt_bias, norm_a, mlstm_ibias, mlstm_fbias, norm_b, w_out, w_ffn1, w_ffn2):
    depth = w_in.shape[0]
    n_lat = x_sample.shape[0]
    t_lat = x_sample.shape[1]
    cond = jnp.concatenate([c_ctx[None, :], c, jnp.zeros((SUBLANES - 1 - n_lat, D_MODEL), F32)], axis=0)
    y_prompt, y_sample = x_prompt, x_sample
    acc = ([], [], [], [])
    for l in range(depth):
        lp = _layer_params(l, norm_mix_pre, norm_mix_post, norm_ffn_pre, norm_ffn_post, w_in, conv_w, a_log,
                           dt_bias, norm_a, mlstm_ibias, mlstm_fbias, norm_b, w_out, w_ffn1, w_ffn2)
        mod = _ada(cond, w_ada, b_ada[l].reshape(1, -1), l)
        mod3 = mod[:1 + n_lat].reshape(1 + n_lat, 6, D_MODEL)
        y_prompt, st = _block(y_prompt, mod3, lambda r: 0, lp, None, x_prompt.shape[1], True)
        for a, s in zip(acc, st):
            a.append(s)
        n_rep = jnp.broadcast_to(state_mlstm_n[:, l][..., None], state_mlstm_n[:, l].shape + (LANES,))
        c_aug0 = jnp.concatenate([state_mlstm_C[:, l], n_rep], axis=-1)
        c_aug0 = c_aug0.transpose(0, 1, 3, 2, 4).reshape(n_lat, N_DIR, DK_B, MW)
        m0 = jnp.zeros((n_lat, N_DIR, LANES), F32)
        for d in range(N_DIR):
            m0 = m0.at[:, d, LF_OFF + d * H_B:LF_OFF + (d + 1) * H_B].set(state_mlstm_m[:, l, d])
        m0 = jnp.broadcast_to(m0[:, :, None, :], (n_lat, N_DIR, SUBLANES, LANES))
        y_sample, _ = _block(y_sample, mod3, lambda r: 1 + r // t_lat, lp, (state_delta[:, l], c_aug0, m0),
                             GRID_W, False)
    return (y_prompt, y_sample) + tuple(jnp.stack(a, axis=1) for a in acc)
```
